```python
import math
import jax, jax.numpy as jnp
from jax import lax
import numpy as np

D_MODEL = 1024
BATCH = 32
SEQ = 2048
DEPTH = 1

CHUNK = 64
LEFT_CHUNKS = 8
BAND_CHUNKS = LEFT_CHUNKS + 1
BAND = BAND_CHUNKS * CHUNK
MAX_REL = 128
N_REL = (CHUNK - 1) + MAX_REL + 1

CONV_WIDTH = D_MODEL // 2
CONV_KERNEL = 31
ATT_HEADS = 8
ATT_HEAD_DIM = 64
ATT_WIDTH = ATT_HEADS * ATT_HEAD_DIM
MEM_LEN = 256
MEM_HEADS = 4
MEM_HEAD_DIM = 128
MEM_WIDTH = MEM_HEADS * MEM_HEAD_DIM
N_BRANCHES = 3
D_FF = 2816
EPS = 1e-6
MASK_VALUE = -1e30

COL_CONV = 2 * CONV_WIDTH
COL_ATT = 3 * ATT_WIDTH
COL_MEM = MEM_WIDTH
COL_GATE = N_BRANCHES * D_MODEL
IN_COLS = COL_CONV + COL_ATT + COL_MEM + COL_GATE

kernel_name = "chunk_causal_hybrid_conformer_block"


def rms_norm(x, g):
    xf = x.astype(jnp.float32)
    y = xf * lax.rsqrt(jnp.mean(xf * xf, axis=-1, keepdims=True) + EPS)
    return (y * g.astype(jnp.float32)).astype(x.dtype)


def layer_norm(x, g, b):
    xf = x.astype(jnp.float32)
    mu = jnp.mean(xf, axis=-1, keepdims=True)
    xc = xf - mu
    y = xc * lax.rsqrt(jnp.mean(xc * xc, axis=-1, keepdims=True) + EPS)
    return (y * g.astype(jnp.float32) + b.astype(jnp.float32)).astype(x.dtype)


def swiglu_ffn(h, w_up, w_down):
    a, b = jnp.split(h @ w_up, 2, axis=-1)
    return (jax.nn.silu(a) * b) @ w_down


def conv_module(u, dw_w, dw_b, ln_g, ln_b, w_pw):
    a, g = jnp.split(u, 2, axis=-1)
    v = a * jax.nn.sigmoid(g)
    v = jnp.pad(v, ((0, 0), (CONV_KERNEL - 1, 0), (0, 0)))
    v = lax.conv_general_dilated(
        v, dw_w[:, None, :].astype(v.dtype), window_strides=(1,), padding="VALID",
        dimension_numbers=("NWC", "WIO", "NWC"), feature_group_count=CONV_WIDTH)
    v = v + dw_b
    v = jax.nn.silu(layer_norm(v, ln_g, ln_b))
    return v @ w_pw


def chunked_attention(q, k, v, rel_bias, w_o):
    B, S, _ = q.shape
    nc = S // CHUNK
    q = q.reshape(B, nc, CHUNK, ATT_HEADS, ATT_HEAD_DIM)
    pad = ((0, 0), (LEFT_CHUNKS * CHUNK, 0), (0, 0))
    kc = jnp.pad(k, pad).reshape(B, nc + LEFT_CHUNKS, CHUNK, ATT_HEADS, ATT_HEAD_DIM)
    vc = jnp.pad(v, pad).reshape(B, nc + LEFT_CHUNKS, CHUNK, ATT_HEADS, ATT_HEAD_DIM)
    band_idx = jnp.arange(nc)[:, None] + jnp.arange(BAND_CHUNKS)[None, :]
    kb = kc[:, band_idx].reshape(B, nc, BAND, ATT_HEADS, ATT_HEAD_DIM)
    vb = vc[:, band_idx].reshape(B, nc, BAND, ATT_HEADS, ATT_HEAD_DIM)
    scores = jnp.einsum("bcqhd,bckhd->bhcqk", q, kb,
                        preferred_element_type=jnp.float32) * (ATT_HEAD_DIM ** -0.5)
    dist = (LEFT_CHUNKS * CHUNK + jnp.arange(CHUNK)[:, None]) - jnp.arange(BAND)[None, :]
    rel_idx = jnp.clip(dist, -(CHUNK - 1), MAX_REL) + (CHUNK - 1)
    bias = rel_bias.astype(jnp.float32)[:, rel_idx]
    key_pos = jnp.arange(nc)[:, None] * CHUNK + jnp.arange(BAND)[None, :] - LEFT_CHUNKS * CHUNK
    valid = (key_pos >= 0)[None, None, :, None, :]
    scores = jnp.where(valid, scores + bias[None, :, None], MASK_VALUE)
    p = jax.nn.softmax(scores, axis=-1).astype(v.dtype)
    o = jnp.einsum("bhcqk,bckhd->bcqhd", p, vb).reshape(B, S, ATT_WIDTH)
    return o @ w_o


def memory_attention(mq, mem_h, w_kv, w_o):
    B, S, _ = mq.shape
    mk, mv = jnp.split(mem_h @ w_kv, 2, axis=-1)
    mq = mq.reshape(B, S, MEM_HEADS, MEM_HEAD_DIM)
    mk = mk.reshape(B, -1, MEM_HEADS, MEM_HEAD_DIM)
    mv = mv.reshape(B, -1, MEM_HEADS, MEM_HEAD_DIM)
    scores = jnp.einsum("bshd,bmhd->bhsm", mq, mk,
                        preferred_element_type=jnp.float32) * (MEM_HEAD_DIM ** -0.5)
    p = jax.nn.softmax(scores, axis=-1).astype(mv.dtype)
    o = jnp.einsum("bhsm,bmhd->bshd", p, mv).reshape(B, S, MEM_WIDTH)
    return o @ w_o


def _fwd_setup_inputs(seed: int = 0) -> dict:
    key = jax.random.key(seed)
    ks = jax.random.split(key, 24)
    L = DEPTH

    def nrm(k, shape, scale):
        return jax.random.normal(k, shape, jnp.float32) * scale

    def gain(k, shape):
        return 1.0 + nrm(k, shape, 0.05)

    return {
        "x": nrm(ks[0], (BATCH, SEQ, D_MODEL), 1.0),
        "mem": nrm(ks[1], (BATCH, MEM_LEN, D_MODEL), 1.0),
        "ffn1_norm": gain(ks[2], (L, D_MODEL)),
        "ffn1_w_up": nrm(ks[3], (L, D_MODEL, 2 * D_FF), D_MODEL ** -0.5),
        "ffn1_w_down": nrm(ks[4], (L, D_FF, D_MODEL), D_FF ** -0.5),
        "mix_norm": gain(ks[5], (L, D_MODEL)),
        "mem_norm": gain(ks[6], (L, D_MODEL)),
        "w_in": nrm(ks[7], (L, D_MODEL, IN_COLS), D_MODEL ** -0.5),
        "b_gate": nrm(ks[8], (L, N_BRANCHES * D_MODEL), 0.1),
        "conv_dw_w": nrm(ks[9], (L, CONV_KERNEL, CONV_WIDTH), CONV_KERNEL ** -0.5),
        "conv_dw_b": nrm(ks[10], (L, CONV_WIDTH), 0.02),
        "conv_ln_g": gain(ks[11], (L, CONV_WIDTH)),
        "conv_ln_b": nrm(ks[12], (L, CONV_WIDTH), 0.02),
        "conv_w_pw": nrm(ks[13], (L, CONV_WIDTH, D_MODEL), CONV_WIDTH ** -0.5),
        "att_rel_bias": nrm(ks[14], (L, ATT_HEADS, N_REL), 0.5),
        "att_w_o": nrm(ks[15], (L, ATT_WIDTH, D_MODEL), ATT_WIDTH ** -0.5),
        "mem_w_kv": nrm(ks[16], (L, D_MODEL, 2 * MEM_WIDTH), D_MODEL ** -0.5),
        "mem_w_o": nrm(ks[17], (L, MEM_WIDTH, D_MODEL), MEM_WIDTH ** -0.5),
        "w_out": nrm(ks[18], (L, D_MODEL, D_MODEL), D_MODEL ** -0.5),
        "ffn2_norm": gain(ks[19], (L, D_MODEL)),
        "ffn2_w_up": nrm(ks[20], (L, D_MODEL, 2 * D_FF), D_MODEL ** -0.5),
        "ffn2_w_down": nrm(ks[21], (L, D_FF, D_MODEL), D_FF ** -0.5),
        "final_norm": gain(ks[22], (D_MODEL,)),
    }


def _fwd_reference(x, mem, ffn1_norm, ffn1_w_up, ffn1_w_down, mix_norm, mem_norm, w_in,
              b_gate, conv_dw_w, conv_dw_b, conv_ln_g, conv_ln_b, conv_w_pw,
              att_rel_bias, att_w_o, mem_w_kv, mem_w_o, w_out, ffn2_norm,
              ffn2_w_up, ffn2_w_down, final_norm):
    B, S, _ = x.shape
    for l in range(DEPTH):
        x = x + 0.5 * swiglu_ffn(rms_norm(x, ffn1_norm[l]), ffn1_w_up[l], ffn1_w_down[l])

        h = rms_norm(x, mix_norm[l])
        proj = h @ w_in[l]
        u_conv, qkv, mq, g_logits = jnp.split(
            proj, [COL_CONV, COL_CONV + COL_ATT, COL_CONV + COL_ATT + COL_MEM], axis=-1)
        q, k, v = jnp.split(qkv, 3, axis=-1)

        y_conv = conv_module(u_conv, conv_dw_w[l], conv_dw_b[l], conv_ln_g[l],
                             conv_ln_b[l], conv_w_pw[l])
        y_att = chunked_attention(q, k, v, att_rel_bias[l], att_w_o[l])
        y_mem = memory_attention(mq, rms_norm(mem, mem_norm[l]), mem_w_kv[l], mem_w_o[l])

        gates = jax.nn.sigmoid(g_logits + b_gate[l]).reshape(B, S, N_BRANCHES, D_MODEL)
        y = gates[:, :, 0] * y_conv + gates[:, :, 1] * y_att + gates[:, :, 2] * y_mem
        x = x + y @ w_out[l]

        x = x + 0.5 * swiglu_ffn(rms_norm(x, ffn2_norm[l]), ffn2_w_up[l], ffn2_w_down[l])
    return rms_norm(x, final_norm)


import jax as _jax
import jax.numpy as _jnp

TWIN_FORMAT = 'train_step'
FWD_PARAMS = ['x', 'mem', 'ffn1_norm', 'ffn1_w_up', 'ffn1_w_down', 'mix_norm', 'mem_norm', 'w_in', 'b_gate', 'conv_dw_w', 'conv_dw_b', 'conv_ln_g', 'conv_ln_b', 'conv_w_pw', 'att_rel_bias', 'att_w_o', 'mem_w_kv', 'mem_w_o', 'w_out', 'ffn2_norm', 'ffn2_w_up', 'ffn2_w_down', 'final_norm']
TWIN_WEIGHTS = ['ffn1_norm', 'ffn1_w_up', 'ffn1_w_down', 'mix_norm', 'mem_norm', 'w_in', 'b_gate', 'conv_dw_w', 'conv_dw_b', 'conv_ln_g', 'conv_ln_b', 'conv_w_pw', 'att_rel_bias', 'att_w_o', 'mem_w_kv', 'mem_w_o', 'w_out', 'ffn2_norm', 'ffn2_w_up', 'ffn2_w_down', 'final_norm']
TWIN_DIFF_INPUT = 'x'
TWIN_INPUTS = ['x', 'mem', 'ffn1_norm', 'ffn1_w_up', 'ffn1_w_down', 'mix_norm', 'mem_norm', 'w_in', 'b_gate', 'conv_dw_w', 'conv_dw_b', 'conv_ln_g', 'conv_ln_b', 'conv_w_pw', 'att_rel_bias', 'att_w_o', 'mem_w_kv', 'mem_w_o', 'w_out', 'ffn2_norm', 'ffn2_w_up', 'ffn2_w_down', 'final_norm', 'loss_target', 'm_ffn1_norm', 'm_ffn1_w_up', 'm_ffn1_w_down', 'm_mix_norm', 'm_mem_norm', 'm_w_in', 'm_b_gate', 'm_conv_dw_w', 'm_conv_dw_b', 'm_conv_ln_g', 'm_conv_ln_b', 'm_conv_w_pw', 'm_att_rel_bias', 'm_att_w_o', 'm_mem_w_kv', 'm_mem_w_o', 'm_w_out', 'm_ffn2_norm', 'm_ffn2_w_up', 'm_ffn2_w_down', 'm_final_norm', 'v_ffn1_norm', 'v_ffn1_w_up', 'v_ffn1_w_down', 'v_mix_norm', 'v_mem_norm', 'v_w_in', 'v_b_gate', 'v_conv_dw_w', 'v_conv_dw_b', 'v_conv_ln_g', 'v_conv_ln_b', 'v_conv_w_pw', 'v_att_rel_bias', 'v_att_w_o', 'v_mem_w_kv', 'v_mem_w_o', 'v_w_out', 'v_ffn2_norm', 'v_ffn2_w_up', 'v_ffn2_w_down', 'v_final_norm']
TWIN_OUTPUTS = ['loss', 'grad_x', 'grad_ffn1_norm', 'grad_ffn1_w_up', 'grad_ffn1_w_down', 'grad_mix_norm', 'grad_mem_norm', 'grad_w_in', 'grad_b_gate', 'grad_conv_dw_w', 'grad_conv_dw_b', 'grad_conv_ln_g', 'grad_conv_ln_b', 'grad_conv_w_pw', 'grad_att_rel_bias', 'grad_att_w_o', 'grad_mem_w_kv', 'grad_mem_w_o', 'grad_w_out', 'grad_ffn2_norm', 'grad_ffn2_w_up', 'grad_ffn2_w_down', 'grad_final_norm', 'delta_ffn1_norm', 'delta_ffn1_w_up', 'delta_ffn1_w_down', 'delta_mix_norm', 'delta_mem_norm', 'delta_w_in', 'delta_b_gate', 'delta_conv_dw_w', 'delta_conv_dw_b', 'delta_conv_ln_g', 'delta_conv_ln_b', 'delta_conv_w_pw', 'delta_att_rel_bias', 'delta_att_w_o', 'delta_mem_w_kv', 'delta_mem_w_o', 'delta_w_out', 'delta_ffn2_norm', 'delta_ffn2_w_up', 'delta_ffn2_w_down', 'delta_final_norm', 'new_m_ffn1_norm', 'new_m_ffn1_w_up', 'new_m_ffn1_w_down', 'new_m_mix_norm', 'new_m_mem_norm', 'new_m_w_in', 'new_m_b_gate', 'new_m_conv_dw_w', 'new_m_conv_dw_b', 'new_m_conv_ln_g', 'new_m_conv_ln_b', 'new_m_conv_w_pw', 'new_m_att_rel_bias', 'new_m_att_w_o', 'new_m_mem_w_kv', 'new_m_mem_w_o', 'new_m_w_out', 'new_m_ffn2_norm', 'new_m_ffn2_w_up', 'new_m_ffn2_w_down', 'new_m_final_norm', 'new_v_ffn1_norm', 'new_v_ffn1_w_up', 'new_v_ffn1_w_down', 'new_v_mix_norm', 'new_v_mem_norm', 'new_v_w_in', 'new_v_b_gate', 'new_v_conv_dw_w', 'new_v_conv_dw_b', 'new_v_conv_ln_g', 'new_v_conv_ln_b', 'new_v_conv_w_pw', 'new_v_att_rel_bias', 'new_v_att_w_o', 'new_v_mem_w_kv', 'new_v_mem_w_o', 'new_v_w_out', 'new_v_ffn2_norm', 'new_v_ffn2_w_up', 'new_v_ffn2_w_down', 'new_v_final_norm']
TWIN_LEAF_KINDS = {'loss': 'loss', 'grad_x': 'grad_x', 'grad_ffn1_norm': 'grad_w', 'grad_ffn1_w_up': 'grad_w', 'grad_ffn1_w_down': 'grad_w', 'grad_mix_norm': 'grad_w', 'grad_mem_norm': 'grad_w', 'grad_w_in': 'grad_w', 'grad_b_gate': 'grad_w', 'grad_conv_dw_w': 'grad_w', 'grad_conv_dw_b': 'grad_w', 'grad_conv_ln_g': 'grad_w', 'grad_conv_ln_b': 'grad_w', 'grad_conv_w_pw': 'grad_w', 'grad_att_rel_bias': 'grad_w', 'grad_att_w_o': 'grad_w', 'grad_mem_w_kv': 'grad_w', 'grad_mem_w_o': 'grad_w', 'grad_w_out': 'grad_w', 'grad_ffn2_norm': 'grad_w', 'grad_ffn2_w_up': 'grad_w', 'grad_ffn2_w_down': 'grad_w', 'grad_final_norm': 'grad_w', 'delta_ffn1_norm': 'delta_w', 'delta_ffn1_w_up': 'delta_w', 'delta_ffn1_w_down': 'delta_w', 'delta_mix_norm': 'delta_w', 'delta_mem_norm': 'delta_w', 'delta_w_in': 'delta_w', 'delta_b_gate': 'delta_w', 'delta_conv_dw_w': 'delta_w', 'delta_conv_dw_b': 'delta_w', 'delta_conv_ln_g': 'delta_w', 'delta_conv_ln_b': 'delta_w', 'delta_conv_w_pw': 'delta_w', 'delta_att_rel_bias': 'delta_w', 'delta_att_w_o': 'delta_w', 'delta_mem_w_kv': 'delta_w', 'delta_mem_w_o': 'delta_w', 'delta_w_out': 'delta_w', 'delta_ffn2_norm': 'delta_w', 'delta_ffn2_w_up': 'delta_w', 'delta_ffn2_w_down': 'delta_w', 'delta_final_norm': 'delta_w', 'new_m_ffn1_norm': 'new_m', 'new_m_ffn1_w_up': 'new_m', 'new_m_ffn1_w_down': 'new_m', 'new_m_mix_norm': 'new_m', 'new_m_mem_norm': 'new_m', 'new_m_w_in': 'new_m', 'new_m_b_gate': 'new_m', 'new_m_conv_dw_w': 'new_m', 'new_m_conv_dw_b': 'new_m', 'new_m_conv_ln_g': 'new_m', 'new_m_conv_ln_b': 'new_m', 'new_m_conv_w_pw': 'new_m', 'new_m_att_rel_bias': 'new_m', 'new_m_att_w_o': 'new_m', 'new_m_mem_w_kv': 'new_m', 'new_m_mem_w_o': 'new_m', 'new_m_w_out': 'new_m', 'new_m_ffn2_norm': 'new_m', 'new_m_ffn2_w_up': 'new_m', 'new_m_ffn2_w_down': 'new_m', 'new_m_final_norm': 'new_m', 'new_v_ffn1_norm': 'new_v', 'new_v_ffn1_w_up': 'new_v', 'new_v_ffn1_w_down': 'new_v', 'new_v_mix_norm': 'new_v', 'new_v_mem_norm': 'new_v', 'new_v_w_in': 'new_v', 'new_v_b_gate': 'new_v', 'new_v_conv_dw_w': 'new_v', 'new_v_conv_dw_b': 'new_v', 'new_v_conv_ln_g': 'new_v', 'new_v_conv_ln_b': 'new_v', 'new_v_conv_w_pw': 'new_v', 'new_v_att_rel_bias': 'new_v', 'new_v_att_w_o': 'new_v', 'new_v_mem_w_kv': 'new_v', 'new_v_mem_w_o': 'new_v', 'new_v_w_out': 'new_v', 'new_v_ffn2_norm': 'new_v', 'new_v_ffn2_w_up': 'new_v', 'new_v_ffn2_w_down': 'new_v', 'new_v_final_norm': 'new_v'}


def _forward(args):
    return _fwd_reference(*[args[k] for k in FWD_PARAMS])


def _output_shape():
    out = _jax.eval_shape(lambda: _forward(_fwd_setup_inputs(0)))
    return out.shape, out.dtype

N_MICROBATCH = 1
ADAM_LR = 0.001
ADAM_B1 = 0.9
ADAM_B2 = 0.999
ADAM_EPS = 1e-08
ADAM_WD = 0.01
ADAM_STEP = 10
PER_EXAMPLE_BATCH_AXIS = {'x': 0, 'mem': 0, 'loss_target': 0}
SHARED_INPUTS = []
_WEIGHT_DTYPES = {'ffn1_norm': _jnp.float32, 'ffn1_w_up': _jnp.float32, 'ffn1_w_down': _jnp.float32, 'mix_norm': _jnp.float32, 'mem_norm': _jnp.float32, 'w_in': _jnp.float32, 'b_gate': _jnp.float32, 'conv_dw_w': _jnp.float32, 'conv_dw_b': _jnp.float32, 'conv_ln_g': _jnp.float32, 'conv_ln_b': _jnp.float32, 'conv_w_pw': _jnp.float32, 'att_rel_bias': _jnp.float32, 'att_w_o': _jnp.float32, 'mem_w_kv': _jnp.float32, 'mem_w_o': _jnp.float32, 'w_out': _jnp.float32, 'ffn2_norm': _jnp.float32, 'ffn2_w_up': _jnp.float32, 'ffn2_w_down': _jnp.float32, 'final_norm': _jnp.float32}
MOMENT_SCALE = {'ffn1_norm': 1.163672e-01, 'ffn1_w_up': 4.682121e-02, 'ffn1_w_down': 7.648765e-02, 'mix_norm': 9.841342e-02, 'mem_norm': 2.129294e-02, 'w_in': 4.059057e-02, 'b_gate': 1.966947e-02, 'conv_dw_w': 1.134195e-01, 'conv_dw_b': 2.514326e-01, 'conv_ln_g': 1.548500e-01, 'conv_ln_b': 1.576989e-01, 'conv_w_pw': 8.297643e-02, 'att_rel_bias': 1.568715e-02, 'att_w_o': 2.096649e-02, 'mem_w_kv': 1.953768e-02, 'mem_w_o': 1.419506e-02, 'w_out': 8.311627e-02, 'ffn2_norm': 9.540977e-02, 'ffn2_w_up': 4.064230e-02, 'ffn2_w_down': 6.665165e-02, 'final_norm': 6.403358e+01}


def _to_microbatches(a, axis):
    t = _jnp.moveaxis(a, axis, 0)
    t = t.reshape((N_MICROBATCH, t.shape[0] // N_MICROBATCH) + t.shape[1:])
    return _jnp.moveaxis(t, 1, axis + 1)


def setup_inputs(seed: int = 0) -> dict:
    inp = _fwd_setup_inputs(seed)
    key = _jax.random.fold_in(_jax.random.key(seed), 7919)
    shape, _ = _output_shape()
    out = dict(inp)
    out["loss_target"] = _jax.random.normal(_jax.random.fold_in(key, 0), shape, _jnp.float32)
    for i, name in enumerate(TWIN_WEIGHTS):
        w = inp[name].astype(_jnp.float32)
        if MOMENT_SCALE is None:
            s = _jnp.sqrt(_jnp.mean(_jnp.square(w)) + 1e-30)
        else:
            s = MOMENT_SCALE[name]
        km, kv = _jax.random.split(_jax.random.fold_in(key, i + 1))
        out[name] = w
        out["m_" + name] = s * _jax.random.normal(km, w.shape, _jnp.float32)
        out["v_" + name] = (s * s) * _jax.random.uniform(kv, w.shape, _jnp.float32, 0.5, 1.5)
    if N_MICROBATCH > 1:
        for name, axis in PER_EXAMPLE_BATCH_AXIS.items():
            out[name] = _to_microbatches(out[name], axis)
    return {'x': out['x'], 'mem': out['mem'], 'ffn1_norm': out['ffn1_norm'], 'ffn1_w_up': out['ffn1_w_up'], 'ffn1_w_down': out['ffn1_w_down'], 'mix_norm': out['mix_norm'], 'mem_norm': out['mem_norm'], 'w_in': out['w_in'], 'b_gate': out['b_gate'], 'conv_dw_w': out['conv_dw_w'], 'conv_dw_b': out['conv_dw_b'], 'conv_ln_g': out['conv_ln_g'], 'conv_ln_b': out['conv_ln_b'], 'conv_w_pw': out['conv_w_pw'], 'att_rel_bias': out['att_rel_bias'], 'att_w_o': out['att_w_o'], 'mem_w_kv': out['mem_w_kv'], 'mem_w_o': out['mem_w_o'], 'w_out': out['w_out'], 'ffn2_norm': out['ffn2_norm'], 'ffn2_w_up': out['ffn2_w_up'], 'ffn2_w_down': out['ffn2_w_down'], 'final_norm': out['final_norm'], 'loss_target': out['loss_target'], 'm_ffn1_norm': out['m_ffn1_norm'], 'm_ffn1_w_up': out['m_ffn1_w_up'], 'm_ffn1_w_down': out['m_ffn1_w_down'], 'm_mix_norm': out['m_mix_norm'], 'm_mem_norm': out['m_mem_norm'], 'm_w_in': out['m_w_in'], 'm_b_gate': out['m_b_gate'], 'm_conv_dw_w': out['m_conv_dw_w'], 'm_conv_dw_b': out['m_conv_dw_b'], 'm_conv_ln_g': out['m_conv_ln_g'], 'm_conv_ln_b': out['m_conv_ln_b'], 'm_conv_w_pw': out['m_conv_w_pw'], 'm_att_rel_bias': out['m_att_rel_bias'], 'm_att_w_o': out['m_att_w_o'], 'm_mem_w_kv': out['m_mem_w_kv'], 'm_mem_w_o': out['m_mem_w_o'], 'm_w_out': out['m_w_out'], 'm_ffn2_norm': out['m_ffn2_norm'], 'm_ffn2_w_up': out['m_ffn2_w_up'], 'm_ffn2_w_down': out['m_ffn2_w_down'], 'm_final_norm': out['m_final_norm'], 'v_ffn1_norm': out['v_ffn1_norm'], 'v_ffn1_w_up': out['v_ffn1_w_up'], 'v_ffn1_w_down': out['v_ffn1_w_down'], 'v_mix_norm': out['v_mix_norm'], 'v_mem_norm': out['v_mem_norm'], 'v_w_in': out['v_w_in'], 'v_b_gate': out['v_b_gate'], 'v_conv_dw_w': out['v_conv_dw_w'], 'v_conv_dw_b': out['v_conv_dw_b'], 'v_conv_ln_g': out['v_conv_ln_g'], 'v_conv_ln_b': out['v_conv_ln_b'], 'v_conv_w_pw': out['v_conv_w_pw'], 'v_att_rel_bias': out['v_att_rel_bias'], 'v_att_w_o': out['v_att_w_o'], 'v_mem_w_kv': out['v_mem_w_kv'], 'v_mem_w_o': out['v_mem_w_o'], 'v_w_out': out['v_w_out'], 'v_ffn2_norm': out['v_ffn2_norm'], 'v_ffn2_w_up': out['v_ffn2_w_up'], 'v_ffn2_w_down': out['v_ffn2_w_down'], 'v_final_norm': out['v_final_norm']}


def _loss(weights, diff, rest, loss_target):
    with _jax.named_scope("forward"):
        args = {**rest, TWIN_DIFF_INPUT: diff, **{k: w.astype(_WEIGHT_DTYPES[k]) for k, w in weights.items()}}
        y = _forward(args)
    with _jax.named_scope("loss_head"):
        err = _jnp.square(y.astype(_jnp.float32) - loss_target)
        return 0.5 * _jnp.sum(_jnp.mean(err, axis=-1)) if err.ndim else 0.5 * err


def _adamw(w, g, m, v):
    m = ADAM_B1 * m + (1.0 - ADAM_B1) * g
    v = ADAM_B2 * v + (1.0 - ADAM_B2) * _jnp.square(g)
    m_hat = m / (1.0 - ADAM_B1 ** ADAM_STEP)
    v_hat = v / (1.0 - ADAM_B2 ** ADAM_STEP)
    delta = -ADAM_LR * (m_hat / (_jnp.sqrt(v_hat) + ADAM_EPS) + ADAM_WD * w)
    return delta, m, v


def reference(x, mem, ffn1_norm, ffn1_w_up, ffn1_w_down, mix_norm, mem_norm, w_in, b_gate, conv_dw_w, conv_dw_b, conv_ln_g, conv_ln_b, conv_w_pw, att_rel_bias, att_w_o, mem_w_kv, mem_w_o, w_out, ffn2_norm, ffn2_w_up, ffn2_w_down, final_norm, loss_target, m_ffn1_norm, m_ffn1_w_up, m_ffn1_w_down, m_mix_norm, m_mem_norm, m_w_in, m_b_gate, m_conv_dw_w, m_conv_dw_b, m_conv_ln_g, m_conv_ln_b, m_conv_w_pw, m_att_rel_bias, m_att_w_o, m_mem_w_kv, m_mem_w_o, m_w_out, m_ffn2_norm, m_ffn2_w_up, m_ffn2_w_down, m_final_norm, v_ffn1_norm, v_ffn1_w_up, v_ffn1_w_down, v_mix_norm, v_mem_norm, v_w_in, v_b_gate, v_conv_dw_w, v_conv_dw_b, v_conv_ln_g, v_conv_ln_b, v_conv_w_pw, v_att_rel_bias, v_att_w_o, v_mem_w_kv, v_mem_w_o, v_w_out, v_ffn2_norm, v_ffn2_w_up, v_ffn2_w_down, v_final_norm):
    given = dict(x=x, mem=mem, ffn1_norm=ffn1_norm, ffn1_w_up=ffn1_w_up, ffn1_w_down=ffn1_w_down, mix_norm=mix_norm, mem_norm=mem_norm, w_in=w_in, b_gate=b_gate, conv_dw_w=conv_dw_w, conv_dw_b=conv_dw_b, conv_ln_g=conv_ln_g, conv_ln_b=conv_ln_b, conv_w_pw=conv_w_pw, att_rel_bias=att_rel_bias, att_w_o=att_w_o, mem_w_kv=mem_w_kv, mem_w_o=mem_w_o, w_out=w_out, ffn2_norm=ffn2_norm, ffn2_w_up=ffn2_w_up, ffn2_w_down=ffn2_w_down, final_norm=final_norm, loss_target=loss_target, m_ffn1_norm=m_ffn1_norm, m_ffn1_w_up=m_ffn1_w_up, m_ffn1_w_down=m_ffn1_w_down, m_mix_norm=m_mix_norm, m_mem_norm=m_mem_norm, m_w_in=m_w_in, m_b_gate=m_b_gate, m_conv_dw_w=m_conv_dw_w, m_conv_dw_b=m_conv_dw_b, m_conv_ln_g=m_conv_ln_g, m_conv_ln_b=m_conv_ln_b, m_conv_w_pw=m_conv_w_pw, m_att_rel_bias=m_att_rel_bias, m_att_w_o=m_att_w_o, m_mem_w_kv=m_mem_w_kv, m_mem_w_o=m_mem_w_o, m_w_out=m_w_out, m_ffn2_norm=m_ffn2_norm, m_ffn2_w_up=m_ffn2_w_up, m_ffn2_w_down=m_ffn2_w_down, m_final_norm=m_final_norm, v_ffn1_norm=v_ffn1_norm, v_ffn1_w_up=v_ffn1_w_up, v_ffn1_w_down=v_ffn1_w_down, v_mix_norm=v_mix_norm, v_mem_norm=v_mem_norm, v_w_in=v_w_in, v_b_gate=v_b_gate, v_conv_dw_w=v_conv_dw_w, v_conv_dw_b=v_conv_dw_b, v_conv_ln_g=v_conv_ln_g, v_conv_ln_b=v_conv_ln_b, v_conv_w_pw=v_conv_w_pw, v_att_rel_bias=v_att_rel_bias, v_att_w_o=v_att_w_o, v_mem_w_kv=v_mem_w_kv, v_mem_w_o=v_mem_w_o, v_w_out=v_w_out, v_ffn2_norm=v_ffn2_norm, v_ffn2_w_up=v_ffn2_w_up, v_ffn2_w_down=v_ffn2_w_down, v_final_norm=v_final_norm)
    weights = {n: given[n] for n in TWIN_WEIGHTS}
    shared = {n: given[n] for n in SHARED_INPUTS}
    per_example = {n: given[n] for n in ['x', 'mem']}
    grad_fn = _jax.value_and_grad(_loss, argnums=(0, 1))

    def one_microbatch(ex, loss_target):
        ex = dict(ex)
        diff = ex.pop(TWIN_DIFF_INPUT)
        return grad_fn(weights, diff, {**shared, **ex}, loss_target)

    if N_MICROBATCH == 1:
        loss, (grad_w, grad_x) = one_microbatch(per_example, given["loss_target"])
    else:
        def body(carry, xs):
            loss_sum, grad_sum = carry
            l_k, (gw_k, gx_k) = one_microbatch(xs[0], xs[1])
            with _jax.named_scope("update"):
                return (loss_sum + l_k, _jax.tree.map(_jnp.add, grad_sum, gw_k)), gx_k

        init = (_jnp.zeros((), _jnp.float32), _jax.tree.map(_jnp.zeros_like, weights))
        (loss, grad_w), grad_x = _jax.lax.scan(body, init, (per_example, given["loss_target"]))
    with _jax.named_scope("update"):
        delta_w, new_m, new_v = {}, {}, {}
        for n in TWIN_WEIGHTS:
            delta_w[n], new_m[n], new_v[n] = _adamw(weights[n], grad_w[n], given["m_" + n], given["v_" + n])
    return (loss, grad_x, *[grad_w[n] for n in TWIN_WEIGHTS], *[delta_w[n] for n in TWIN_WEIGHTS],
            *[new_m[n] for n in TWIN_WEIGHTS], *[new_v[n] for n in TWIN_WEIGHTS])
```

```python
import jax
import jax.numpy as jnp
from jax import lax
from jax.experimental import pallas as pl
from jax.experimental.pallas import tpu as pltpu

F32 = jnp.float32
BF16 = jnp.bfloat16

D_MODEL = 1024
D_FF = 2816
CHUNK = 64
LEFT_CHUNKS = 8
MAX_REL = 128
N_REL = (CHUNK - 1) + MAX_REL + 1
CONV_WIDTH = 512
CONV_KERNEL = 31
ATT_HEADS = 8
ATT_WIDTH = 512
MEM_LEN = 256
MEM_HEADS = 4
MEM_HEAD_DIM = 128
MEM_WIDTH = 512
EPS = 1e-6
MASK_VALUE = -1e30
ATT_SCALE = 64 ** -0.5
MEM_SCALE = 128 ** -0.5

ADAM_LR = 0.001
ADAM_B1 = 0.9
ADAM_B2 = 0.999
ADAM_EPS = 1e-08
ADAM_WD = 0.01
ADAM_STEP = 10

QB = 256
KW = 3 * QB
CONV_PAD = 32
CONV_TILE = 256

VMEM_LIMIT = 56 << 20

WEIGHTS = ['ffn1_norm', 'ffn1_w_up', 'ffn1_w_down', 'mix_norm', 'mem_norm', 'w_in', 'b_gate', 'conv_dw_w',
           'conv_dw_b', 'conv_ln_g', 'conv_ln_b', 'conv_w_pw', 'att_rel_bias', 'att_w_o', 'mem_w_kv', 'mem_w_o',
           'w_out', 'ffn2_norm', 'ffn2_w_up', 'ffn2_w_down', 'final_norm']
BIG = [('ffn1_w_up', 1), ('ffn1_w_down', 0), ('w_in', 1), ('conv_w_pw', 1), ('att_w_o', 1), ('mem_w_kv', 0),
       ('mem_w_o', 1), ('w_out', 0), ('ffn2_w_up', 1), ('ffn2_w_down', 0)]
SMALL = ['ffn1_norm', 'mix_norm', 'mem_norm', 'b_gate', 'conv_dw_b', 'conv_ln_g', 'conv_ln_b', 'att_rel_bias',
         'ffn2_norm', 'final_norm']
N_CHIPS = 4
N_DEV = 8
MESH = pl.DeviceIdType.MESH


def _pick(n, cands):
    for c in cands:
        if n % c == 0:
            return c
    return n


def _sig(x):
    return 1.0 / (1.0 + jnp.exp(-x))


def _params(sem=None, vmem=VMEM_LIMIT):
    return pltpu.CompilerParams(dimension_semantics=sem, vmem_limit_bytes=vmem)


def _dot(a, b, mode='nn'):
    dims = {'nn': (((1,), (0,)), ((), ())), 'nt': (((1,), (1,)), ((), ())), 'tn': (((0,), (0,)), ((), ()))}[mode]
    return lax.dot_general(a.astype(BF16), b.astype(BF16), dims, preferred_element_type=F32)


def _mm(a, b, mode, name, out_dtype, res=None, scale=1.0):
    if mode == 'nn':
        (M, C), (_, N) = a.shape, b.shape
    elif mode == 'nt':
        (M, C), (N, _) = a.shape, b.shape
    else:
        (C, M), (_, N) = a.shape, b.shape
    wide = a.dtype == F32 or out_dtype == F32
    tm = _pick(M, (512, 1408, 256, 128) if wide else (1024, 1408, 512, 256, 128))
    tn = _pick(N, (1024, 1408, 512, 256, 128))
    tc = _pick(C, (1024, 1408, 512, 256, 128))
    nk = C // tc
    if mode == 'nn':
        a_spec = pl.BlockSpec((tm, tc), lambda i, j, k: (i, k))
        b_spec = pl.BlockSpec((tc, tn), lambda i, j, k: (k, j))
    elif mode == 'nt':
        a_spec = pl.BlockSpec((tm, tc), lambda i, j, k: (i, k))
        b_spec = pl.BlockSpec((tn, tc), lambda i, j, k: (j, k))
    else:
        a_spec = pl.BlockSpec((tc, tm), lambda i, j, k: (k, i))
        b_spec = pl.BlockSpec((tc, tn), lambda i, j, k: (k, j))
    o_spec = pl.BlockSpec((tm, tn), lambda i, j, k: (i, j))
    has_res = res is not None

    def body(*refs):
        if has_res:
            a_ref, b_ref, r_ref, o_ref, acc_ref = refs
        else:
            a_ref, b_ref, o_ref, acc_ref = refs
            r_ref = None
        k = pl.program_id(2)
        p = _dot(a_ref[...], b_ref[...], mode)

        def finish(acc):
            if scale != 1.0:
                acc = acc * scale
            if r_ref is not None:
                acc = r_ref[...] + acc
            o_ref[...] = acc.astype(o_ref.dtype)

        if nk == 1:
            finish(p)
        else:
            @pl.when(k == 0)
            def _():
                acc_ref[...] = p

            @pl.when(k > 0)
            def _():
                acc_ref[...] += p

            @pl.when(k == nk - 1)
            def _():
                finish(acc_ref[...])

    in_specs = [a_spec, b_spec] + ([o_spec] if has_res else [])
    args = (a, b) + ((res,) if has_res else ())
    acc_shape = (tm, tn) if nk > 1 else (8, 128)
    return pl.pallas_call(
        body, name=name, grid=(M // tm, N // tn, nk), in_specs=in_specs, out_specs=o_spec,
        out_shape=jax.ShapeDtypeStruct((M, N), out_dtype), scratch_shapes=[pltpu.VMEM(acc_shape, F32)],
        compiler_params=_params(("parallel", "parallel", "arbitrary")))(*args)


def _row_tile(T):
    return _pick(T, (512, 256, 128, 64, 32, 16, 8))


def _rms_fwd(x, g, name):
    T, Dm = x.shape
    tm = _row_tile(T)

    def body(x_ref, g_ref, o_ref):
        xv = x_ref[...]
        r = lax.rsqrt(jnp.mean(xv * xv, axis=-1, keepdims=True) + EPS)
        o_ref[...] = ((xv * r) * g_ref[...]).astype(o_ref.dtype)

    return pl.pallas_call(
        body, name=name, grid=(T // tm,),
        in_specs=[pl.BlockSpec((tm, Dm), lambda i: (i, 0)), pl.BlockSpec((1, Dm), lambda i: (0, 0))],
        out_specs=pl.BlockSpec((tm, Dm), lambda i: (i, 0)), out_shape=jax.ShapeDtypeStruct((T, Dm), BF16),
        compiler_params=_params(("parallel",)))(x, g)


def _rms_bwd(x, g, dh, dres, name):
    T, Dm = x.shape
    tm = _row_tile(T)

    def body(x_ref, g_ref, dh_ref, dr_ref, dx_ref, dg_ref):
        i = pl.program_id(0)
        xv = x_ref[...]
        r = lax.rsqrt(jnp.mean(xv * xv, axis=-1, keepdims=True) + EPS)
        xr = xv * r
        dh_v = dh_ref[...].astype(F32)
        dyg = dh_v * g_ref[...]
        dx = r * (dyg - xr * jnp.mean(dyg * xr, axis=-1, keepdims=True))
        dx_ref[...] = dr_ref[...] + dx

        @pl.when(i == 0)
        def _():
            dg_ref[...] = jnp.zeros_like(dg_ref)

        dg_ref[...] += jnp.sum(dh_v * xr, axis=0, keepdims=True)

    row = pl.BlockSpec((tm, Dm), lambda i: (i, 0))
    vec = pl.BlockSpec((1, Dm), lambda i: (0, 0))
    return pl.pallas_call(
        body, name=name, grid=(T // tm,), in_specs=[row, vec, row, row], out_specs=[row, vec],
        out_shape=[jax.ShapeDtypeStruct((T, Dm), F32), jax.ShapeDtypeStruct((1, Dm), F32)],
        compiler_params=_params(("arbitrary",)))(x, g, dh, dres)


def _final_fwd_bwd(x3, tgt, g):
    T, Dm = x3.shape
    tm = _row_tile(T)

    def body(x_ref, t_ref, g_ref, dx_ref, dg_ref, loss_ref):
        i = pl.program_id(0)
        xv = x_ref[...]
        gg = g_ref[...]
        r = lax.rsqrt(jnp.mean(xv * xv, axis=-1, keepdims=True) + EPS)
        xr = xv * r
        err = xr * gg - t_ref[...]
        dout = err * (1.0 / Dm)
        dyg = dout * gg
        dx_ref[...] = r * (dyg - xr * jnp.mean(dyg * xr, axis=-1, keepdims=True))

        @pl.when(i == 0)
        def _():
            dg_ref[...] = jnp.zeros_like(dg_ref)
            loss_ref[...] = jnp.zeros_like(loss_ref)

        dg_ref[...] += jnp.sum(dout * xr, axis=0, keepdims=True)
        loss_ref[...] += jnp.zeros_like(loss_ref) + (0.5 / Dm) * jnp.sum(err * err)

    row = pl.BlockSpec((tm, Dm), lambda i: (i, 0))
    vec = pl.BlockSpec((1, Dm), lambda i: (0, 0))
    one = pl.BlockSpec((1, 128), lambda i: (0, 0))
    return pl.pallas_call(
        body, name="final_fwd_bwd", grid=(T // tm,), in_specs=[row, row, vec], out_specs=[row, vec, one],
        out_shape=[jax.ShapeDtypeStruct((T, Dm), F32), jax.ShapeDtypeStruct((1, Dm), F32),
                   jax.ShapeDtypeStruct((1, 128), F32)],
        compiler_params=_params(("arbitrary",)))(x3, tgt, g)


def _swiglu_fwd(up, name):
    T, F2 = up.shape
    Fh = F2 // 2
    tm = _pick(T, (256, 128, 64, 32, 16, 8))

    def body(up_ref, o_ref):
        a = up_ref[:, :Fh].astype(F32)
        b = up_ref[:, Fh:].astype(F32)
        o_ref[...] = (a * _sig(a) * b).astype(o_ref.dtype)

    return pl.pallas_call(
        body, name=name, grid=(T // tm,), in_specs=[pl.BlockSpec((tm, F2), lambda i: (i, 0))],
        out_specs=pl.BlockSpec((tm, Fh), lambda i: (i, 0)), out_shape=jax.ShapeDtypeStruct((T, Fh), BF16),
        compiler_params=_params(("parallel",)))(up)


def _swiglu_bwd(dact, up, name):
    T, F2 = up.shape
    Fh = F2 // 2
    tm = _pick(T, (256, 128, 64, 32, 16, 8))

    def body(d_ref, up_ref, o_ref):
        a = up_ref[:, :Fh].astype(F32)
        b = up_ref[:, Fh:].astype(F32)
        d = d_ref[...].astype(F32)
        s = _sig(a)
        o_ref[:, :Fh] = (d * b * s * (1.0 + a * (1.0 - s))).astype(o_ref.dtype)
        o_ref[:, Fh:] = (d * a * s).astype(o_ref.dtype)

    return pl.pallas_call(
        body, name=name, grid=(T // tm,),
        in_specs=[pl.BlockSpec((tm, Fh), lambda i: (i, 0)), pl.BlockSpec((tm, F2), lambda i: (i, 0))],
        out_specs=pl.BlockSpec((tm, F2), lambda i: (i, 0)), out_shape=jax.ShapeDtypeStruct((T, F2), BF16),
        compiler_params=_params(("parallel",)))(dact, up)


def _mix_fwd(c_act, o_att, o_mem, proj, b_gate, x1, w_pw, w_o, w_mo, w_out):
    T, Dm = x1.shape
    W = c_act.shape[1]
    tm = _pick(T, (256, 128, 64, 32, 16, 8))

    def body(c_ref, oa_ref, om_ref, gl_ref, bg_ref, x1_ref, wpw_ref, wo_ref, wmo_ref, wout_ref,
             x2_ref, yc_ref, ya_ref, ym_ref, y_ref):
        yc = _dot(c_ref[...], wpw_ref[...])
        ya = _dot(oa_ref[...], wo_ref[...])
        ym = _dot(om_ref[...], wmo_ref[...])
        g = _sig(gl_ref[...].astype(F32) + bg_ref[...])
        y = g[:, :Dm] * yc + g[:, Dm:2 * Dm] * ya + g[:, 2 * Dm:] * ym
        x2_ref[...] = x1_ref[...] + _dot(y, wout_ref[...])
        yc_ref[...] = yc.astype(BF16)
        ya_ref[...] = ya.astype(BF16)
        ym_ref[...] = ym.astype(BF16)
        y_ref[...] = y.astype(BF16)

    rowW = pl.BlockSpec((tm, W), lambda i: (i, 0))
    rowD = pl.BlockSpec((tm, Dm), lambda i: (i, 0))
    full = lambda s: pl.BlockSpec(s, lambda i: (0, 0))
    return pl.pallas_call(
        body, name="mix_fwd", grid=(T // tm,),
        in_specs=[rowW, rowW, rowW, pl.BlockSpec((tm, 3 * Dm), lambda i: (i, 1)), full((1, 3 * Dm)), rowD,
                  full((W, Dm)), full((W, Dm)), full((W, Dm)), full((Dm, Dm))],
        out_specs=[rowD] * 5,
        out_shape=[jax.ShapeDtypeStruct((T, Dm), F32)] + [jax.ShapeDtypeStruct((T, Dm), BF16)] * 4,
        compiler_params=_params(("parallel",)))(c_act, o_att, o_mem, proj, b_gate, x1, w_pw, w_o, w_mo, w_out)


def _mix_bwd(dx2, yc, ya, ym, proj, b_gate, w_pw, w_o, w_mo, w_out):
    T, Dm = dx2.shape
    W = w_pw.shape[0]
    tm = _pick(T, (256, 128, 64, 32, 16, 8))

    def body(dx_ref, yc_ref, ya_ref, ym_ref, gl_ref, bg_ref, wpw_ref, wo_ref, wmo_ref, wout_ref,
             dyc_ref, dya_ref, dym_ref, dgl_ref, dbg_ref, dc_ref, doa_ref, dom_ref):
        i = pl.program_id(0)

        @pl.when(i == 0)
        def _():
            dbg_ref[...] = jnp.zeros_like(dbg_ref)

        dy = _dot(dx_ref[...], wout_ref[...], 'nt')
        g = _sig(gl_ref[...].astype(F32) + bg_ref[...])
        branches = ((yc_ref, dyc_ref, wpw_ref, dc_ref), (ya_ref, dya_ref, wo_ref, doa_ref),
                    (ym_ref, dym_ref, wmo_ref, dom_ref))
        for n, (y_ref, dyk_ref, w_ref, dk_ref) in enumerate(branches):
            gk = g[:, n * Dm:(n + 1) * Dm]
            dyk = dy * gk
            dgl = dyk * y_ref[...].astype(F32) * (1.0 - gk)
            dyk_ref[...] = dyk.astype(BF16)
            dgl_ref[:, n * Dm:(n + 1) * Dm] = dgl.astype(BF16)
            dbg_ref[:, n * Dm:(n + 1) * Dm] += jnp.sum(dgl, axis=0, keepdims=True)
            dk_ref[...] = _dot(dyk, w_ref[...], 'nt').astype(BF16)

    rowW = pl.BlockSpec((tm, W), lambda i: (i, 0))
    rowD = pl.BlockSpec((tm, Dm), lambda i: (i, 0))
    row3 = pl.BlockSpec((tm, 3 * Dm), lambda i: (i, 0))
    full = lambda s: pl.BlockSpec(s, lambda i: (0, 0))
    return pl.pallas_call(
        body, name="mix_bwd", grid=(T // tm,),
        in_specs=[rowD, rowD, rowD, rowD, pl.BlockSpec((tm, 3 * Dm), lambda i: (i, 1)), full((1, 3 * Dm)),
                  full((W, Dm)), full((W, Dm)), full((W, Dm)), full((Dm, Dm))],
        out_specs=[rowD, rowD, rowD, row3, full((1, 3 * Dm)), rowW, rowW, rowW],
        out_shape=[jax.ShapeDtypeStruct((T, Dm), BF16)] * 3 + [jax.ShapeDtypeStruct((T, 3 * Dm), BF16),
                                                                jax.ShapeDtypeStruct((1, 3 * Dm), F32)]
        + [jax.ShapeDtypeStruct((T, W), BF16)] * 3,
        compiler_params=_params(("arbitrary",)))(dx2, yc, ya, ym, proj, b_gate, w_pw, w_o, w_mo, w_out)


def _ln_swish(cv, lg, lb):
    mu = jnp.mean(cv, axis=-1, keepdims=True)
    xc = cv - mu
    r = lax.rsqrt(jnp.mean(xc * xc, axis=-1, keepdims=True) + EPS)
    n = xc * r
    l = n * lg + lb
    return r, n, l


def _conv_fwd(proj3, dw_w, dw_b, ln_g, ln_b):
    Bl, S, _ = proj3.shape
    C, K, TS, PAD = CONV_WIDTH, CONV_KERNEL, CONV_TILE, CONV_PAD
    nt = S // TS

    def body(u_ref, w_ref, b_ref, lg_ref, lb_ref, cv_ref, c_ref, vbuf, win):
        vbuf[0:PAD, :] = jnp.zeros((PAD, C), F32)

        def glu(t, carry):
            r0 = pl.multiple_of(t * TS, TS)
            u = u_ref[pl.ds(r0, TS), :].astype(F32)
            vbuf[pl.ds(PAD + r0, TS), :] = u[:, :C] * _sig(u[:, C:])
            return carry

        lax.fori_loop(0, nt, glu, 0)

        def conv(t, carry):
            r0 = pl.multiple_of(t * TS, TS)
            win[...] = vbuf[pl.ds(r0, TS + PAD), :]
            acc = jnp.zeros((TS, C), F32)
            for j in range(K):
                acc = acc + w_ref[j:j + 1, :] * win[PAD - (K - 1) + j:PAD - (K - 1) + j + TS, :]
            cv = acc + b_ref[...]
            cv_ref[pl.ds(r0, TS), :] = cv
            _, _, l = _ln_swish(cv, lg_ref[...], lb_ref[...])
            c_ref[pl.ds(r0, TS), :] = (l * _sig(l)).astype(BF16)
            return carry

        lax.fori_loop(0, nt, conv, 0)

    vec = pl.BlockSpec((1, C), lambda b: (0, 0))
    return pl.pallas_call(
        body, name="conv_fwd", grid=(Bl,),
        in_specs=[pl.BlockSpec((None, S, 2 * C), lambda b: (b, 0, 0)), pl.BlockSpec((K, C), lambda b: (0, 0)),
                  vec, vec, vec],
        out_specs=[pl.BlockSpec((None, S, C), lambda b: (b, 0, 0))] * 2,
        out_shape=[jax.ShapeDtypeStruct((Bl, S, C), F32), jax.ShapeDtypeStruct((Bl, S, C), BF16)],
        scratch_shapes=[pltpu.VMEM((S + PAD, C), F32), pltpu.VMEM((TS + PAD, C), F32)],
        compiler_params=_params(("parallel",)))(proj3, dw_w, dw_b, ln_g, ln_b)


def _conv_bwd(proj3, cv, dc, dw_w, ln_g, ln_b):
    Bl, S, _ = proj3.shape
    C, K, TS, PAD = CONV_WIDTH, CONV_KERNEL, CONV_TILE, CONV_PAD
    nt = S // TS

    def body(u_ref, cv_ref, dc_ref, w_ref, lg_ref, lb_ref, du_ref, dw_ref, db_ref, dlg_ref, dlb_ref,
             vbuf, gbuf, win, dwacc):
        b = pl.program_id(0)

        @pl.when(b == 0)
        def _():
            dw_ref[...] = jnp.zeros_like(dw_ref)
            db_ref[...] = jnp.zeros_like(db_ref)
            dlg_ref[...] = jnp.zeros_like(dlg_ref)
            dlb_ref[...] = jnp.zeros_like(dlb_ref)

        vbuf[0:PAD, :] = jnp.zeros((PAD, C), F32)
        gbuf[S:S + PAD, :] = jnp.zeros((PAD, C), F32)
        dwacc[...] = jnp.zeros_like(dwacc)

        def norm_bwd(t, carry):
            r0 = pl.multiple_of(t * TS, TS)
            u = u_ref[pl.ds(r0, TS), :].astype(F32)
            vbuf[pl.ds(PAD + r0, TS), :] = u[:, :C] * _sig(u[:, C:])
            r, n, l = _ln_swish(cv_ref[pl.ds(r0, TS), :], lg_ref[...], lb_ref[...])
            s = _sig(l)
            dl = dc_ref[pl.ds(r0, TS), :].astype(F32) * s * (1.0 + l * (1.0 - s))
            dlg_ref[...] += jnp.sum(dl * n, axis=0, keepdims=True)
            dlb_ref[...] += jnp.sum(dl, axis=0, keepdims=True)
            dn = dl * lg_ref[...]
            dcv = r * (dn - jnp.mean(dn, axis=-1, keepdims=True) - n * jnp.mean(dn * n, axis=-1, keepdims=True))
            gbuf[pl.ds(r0, TS), :] = dcv
            db_ref[...] += jnp.sum(dcv, axis=0, keepdims=True)
            return carry

        lax.fori_loop(0, nt, norm_bwd, 0)

        def conv_bwd(t, carry):
            r0 = pl.multiple_of(t * TS, TS)
            win[...] = gbuf[pl.ds(r0, TS + PAD), :]
            dv = jnp.zeros((TS, C), F32)
            for j in range(K):
                dv = dv + w_ref[j:j + 1, :] * win[K - 1 - j:K - 1 - j + TS, :]
            u = u_ref[pl.ds(r0, TS), :].astype(F32)
            a, g = u[:, :C], u[:, C:]
            s = _sig(g)
            du_ref[pl.ds(r0, TS), 0:C] = (dv * s).astype(BF16)
            du_ref[pl.ds(r0, TS), C:2 * C] = (dv * a * s * (1.0 - s)).astype(BF16)
            dcv = gbuf[pl.ds(r0, TS), :]
            win[...] = vbuf[pl.ds(r0, TS + PAD), :]
            for j in range(K):
                prod = dcv * win[PAD - (K - 1) + j:PAD - (K - 1) + j + TS, :]
                dwacc[j] += jnp.sum(prod.reshape(TS // 8, 8, C), axis=0)
            return carry

        lax.fori_loop(0, nt, conv_bwd, 0)
        dw_ref[...] += jnp.sum(dwacc[...], axis=1)

    vec = pl.BlockSpec((1, C), lambda b: (0, 0))
    seq = lambda w: pl.BlockSpec((None, S, w), lambda b: (b, 0, 0))
    return pl.pallas_call(
        body, name="conv_bwd", grid=(Bl,),
        in_specs=[seq(2 * C), seq(C), seq(C), pl.BlockSpec((K, C), lambda b: (0, 0)), vec, vec],
        out_specs=[seq(2 * C), pl.BlockSpec((K, C), lambda b: (0, 0)), vec, vec, vec],
        out_shape=[jax.ShapeDtypeStruct((Bl, S, 2 * C), BF16), jax.ShapeDtypeStruct((K, C), F32)]
        + [jax.ShapeDtypeStruct((1, C), F32)] * 3,
        scratch_shapes=[pltpu.VMEM((S + PAD, C), F32), pltpu.VMEM((S + PAD, C), F32),
                        pltpu.VMEM((TS + PAD, C), F32), pltpu.VMEM((K, 8, C), F32)],
        compiler_params=_params(("arbitrary",)))(proj3, cv, dc, dw_w, ln_g, ln_b)


def _att_bias(rel_bias):
    qi = jnp.arange(QB)[:, None]
    kj = jnp.arange(KW)[None, :]
    dist = (KW - QB) + qi - kj
    idx = jnp.clip(dist, -(CHUNK - 1), MAX_REL) + (CHUNK - 1)
    dchunk = ((KW - QB) + qi) // CHUNK - kj // CHUNK
    band = (dchunk >= 0) & (dchunk <= LEFT_CHUNKS)
    return jnp.where(band[None], rel_bias[:, idx], MASK_VALUE)


def _head_masks():
    lane = lax.broadcasted_iota(jnp.int32, (1, 128), 1)
    return (lane < 64, lane >= 64)


def _att_probs(qh, k2, bias, valid):
    s = _dot(qh, k2, 'nt') * ATT_SCALE + bias
    s = jnp.where(valid, s, MASK_VALUE)
    e = jnp.exp(s - jnp.max(s, axis=-1, keepdims=True))
    return e * (1.0 / jnp.sum(e, axis=-1, keepdims=True))


def _att_specs(S, q_col):
    nb = S // QB
    q_spec = pl.BlockSpec((None, QB, ATT_WIDTH), lambda b, i: (b, jnp.minimum(i, nb - 1), q_col))

    def kv_spec(col, kb):
        return pl.BlockSpec((None, QB, ATT_WIDTH),
                            lambda b, i: (b, jnp.clip(i - 2 + kb, 0, nb - 1), col))

    return q_spec, [kv_spec(3, kb) for kb in range(3)], [kv_spec(4, kb) for kb in range(3)]


def _att_fwd(proj3, bias):
    Bl, S, _ = proj3.shape
    nb = S // QB
    q_spec, k_specs, v_specs = _att_specs(S, 2)

    def body(q_ref, k0, k1, k2r, v0, v1, v2r, bias_ref, o_ref):
        i = pl.program_id(1)
        masks = _head_masks()
        valid = lax.broadcasted_iota(jnp.int32, (QB, KW), 1) >= (2 - i) * QB
        for pr in range(ATT_HEADS // 2):
            ls = slice(128 * pr, 128 * (pr + 1))
            q2 = q_ref[:, ls]
            k2 = jnp.concatenate([k0[:, ls], k1[:, ls], k2r[:, ls]], axis=0)
            v2 = jnp.concatenate([v0[:, ls], v1[:, ls], v2r[:, ls]], axis=0)
            o2 = jnp.zeros((QB, 128), F32)
            for hh in range(2):
                p = _att_probs(jnp.where(masks[hh], q2, 0), k2, bias_ref[2 * pr + hh], valid)
                o2 = o2 + _dot(p, jnp.where(masks[hh], v2, 0))
            o_ref[:, ls] = o2.astype(BF16)

    return pl.pallas_call(
        body, name="att_fwd", grid=(Bl, nb),
        in_specs=[q_spec] + k_specs + v_specs + [pl.BlockSpec((ATT_HEADS, QB, KW), lambda b, i: (0, 0, 0))],
        out_specs=pl.BlockSpec((None, QB, ATT_WIDTH), lambda b, i: (b, i, 0)),
        out_shape=jax.ShapeDtypeStruct((Bl, S, ATT_WIDTH), BF16),
        compiler_params=_params(("parallel", "arbitrary")))(*([proj3] * 7), bias)


def _att_bwd(proj3, do, bias):
    Bl, S, _ = proj3.shape
    nb = S // QB
    q_spec, k_specs, v_specs = _att_specs(S, 2)
    do_spec = pl.BlockSpec((None, QB, ATT_WIDTH), lambda b, i: (b, jnp.minimum(i, nb - 1), 0))
    kv_out = pl.BlockSpec((None, QB, ATT_WIDTH), lambda b, i: (b, jnp.clip(i - 2, 0, nb - 1), 0))
    bias_spec = pl.BlockSpec((ATT_HEADS, QB, KW), lambda b, i: (0, 0, 0))

    def body(q_ref, k0, k1, k2r, v0, v1, v2r, do_ref, bias_ref, dq_ref, dk_ref, dv_ref, db_ref, dkw, dvw):
        b = pl.program_id(0)
        i = pl.program_id(1)

        @pl.when((b == 0) & (i == 0))
        def _():
            db_ref[...] = jnp.zeros_like(db_ref)

        @pl.when(i == 0)
        def _():
            dkw[...] = jnp.zeros_like(dkw)
            dvw[...] = jnp.zeros_like(dvw)

        @pl.when(i < nb)
        def _():
            masks = _head_masks()
            valid = lax.broadcasted_iota(jnp.int32, (QB, KW), 1) >= (2 - i) * QB
            for pr in range(ATT_HEADS // 2):
                ls = slice(128 * pr, 128 * (pr + 1))
                q2 = q_ref[:, ls]
                do2 = do_ref[:, ls]
                k2 = jnp.concatenate([k0[:, ls], k1[:, ls], k2r[:, ls]], axis=0)
                v2 = jnp.concatenate([v0[:, ls], v1[:, ls], v2r[:, ls]], axis=0)
                dq2 = jnp.zeros((QB, 128), F32)
                dk2 = jnp.zeros((KW, 128), F32)
                dv2 = jnp.zeros((KW, 128), F32)
                for hh in range(2):
                    h = 2 * pr + hh
                    qh = jnp.where(masks[hh], q2, 0)
                    doh = jnp.where(masks[hh], do2, 0)
                    p = _att_probs(qh, k2, bias_ref[h], valid)
                    dp = _dot(doh, v2, 'nt')
                    ds = p * (dp - jnp.sum(p * dp, axis=-1, keepdims=True))
                    db_ref[h] += ds
                    dq2 = dq2 + _dot(ds, jnp.where(masks[hh], k2, 0))
                    dk2 = dk2 + _dot(ds, qh, 'tn')
                    dv2 = dv2 + _dot(p, doh, 'tn')
                dq_ref[:, ls] = (dq2 * ATT_SCALE).astype(BF16)
                dkw[:, ls] += dk2 * ATT_SCALE
                dvw[:, ls] += dv2

        dk_ref[...] = dkw[0:QB, :].astype(BF16)
        dv_ref[...] = dvw[0:QB, :].astype(BF16)
        for buf in (dkw, dvw):
            rest = buf[QB:KW, :]
            buf[0:KW - QB, :] = rest
            buf[KW - QB:KW, :] = jnp.zeros((QB, ATT_WIDTH), F32)

    blk = jax.ShapeDtypeStruct((Bl, S, ATT_WIDTH), BF16)
    return pl.pallas_call(
        body, name="att_bwd", grid=(Bl, nb + 2),
        in_specs=[q_spec] + k_specs + v_specs + [do_spec, bias_spec],
        out_specs=[do_spec, kv_out, kv_out, bias_spec],
        out_shape=[blk, blk, blk, jax.ShapeDtypeStruct((ATT_HEADS, QB, KW), F32)],
        scratch_shapes=[pltpu.VMEM((KW, ATT_WIDTH), F32), pltpu.VMEM((KW, ATT_WIDTH), F32)],
        compiler_params=_params(("arbitrary", "arbitrary")))(*([proj3] * 7), do, bias)


def _rel_bias_grad(dbias):
    H = dbias.shape[0]
    Wd = KW + QB
    padded = jnp.pad(dbias, ((0, 0), (0, 1), (QB, 0)))
    skew = padded.reshape(H, (QB + 1) * Wd)[:, :QB * (Wd + 1)].reshape(H, QB, Wd + 1)[:, :, :Wd]
    c = jnp.arange(Wd)[:, None]
    bins = (jnp.clip(KW - c, -(CHUNK - 1), MAX_REL) + (CHUNK - 1) == jnp.arange(N_REL)[None, :]).astype(F32)

    def body(s_ref, bins_ref, o_ref):
        col = jnp.sum(s_ref[...], axis=1)
        o_ref[...] = jnp.dot(col, bins_ref[...], preferred_element_type=F32, precision=lax.Precision.HIGHEST)

    return pl.pallas_call(
        body, name="rel_bias_grad", grid=(1,),
        in_specs=[pl.BlockSpec((H, QB, Wd), lambda i: (0, 0, 0)), pl.BlockSpec((Wd, N_REL), lambda i: (0, 0))],
        out_specs=pl.BlockSpec((H, N_REL), lambda i: (0, 0)), out_shape=jax.ShapeDtypeStruct((H, N_REL), F32),
        compiler_params=_params(("arbitrary",)))(skew, bins)


MEM_TILE = 512


def _mem_probs(qh, kh):
    s = _dot(qh, kh, 'nt') * MEM_SCALE
    e = jnp.exp(s - jnp.max(s, axis=-1, keepdims=True))
    return e * (1.0 / jnp.sum(e, axis=-1, keepdims=True))


def _mem_fwd(proj3, kv3):
    Bl, S, _ = proj3.shape
    tq = _pick(S, (MEM_TILE, 256))
    hd = MEM_HEAD_DIM

    def body(q_ref, kv_ref, o_ref):
        for h in range(MEM_HEADS):
            p = _mem_probs(q_ref[:, h * hd:(h + 1) * hd], kv_ref[:, h * hd:(h + 1) * hd])
            o_ref[:, h * hd:(h + 1) * hd] = _dot(p, kv_ref[:, MEM_WIDTH + h * hd:MEM_WIDTH + (h + 1) * hd]).astype(BF16)

    return pl.pallas_call(
        body, name="mem_fwd", grid=(Bl, S // tq),
        in_specs=[pl.BlockSpec((None, tq, MEM_WIDTH), lambda b, i: (b, i, 5)),
                  pl.BlockSpec((None, MEM_LEN, 2 * MEM_WIDTH), lambda b, i: (b, 0, 0))],
        out_specs=pl.BlockSpec((None, tq, MEM_WIDTH), lambda b, i: (b, i, 0)),
        out_shape=jax.ShapeDtypeStruct((Bl, S, MEM_WIDTH), BF16),
        compiler_params=_params(("parallel", "parallel")))(proj3, kv3)


def _mem_bwd(proj3, kv3, do):
    Bl, S, _ = proj3.shape
    tq = _pick(S, (MEM_TILE, 256))
    hd = MEM_HEAD_DIM

    def body(q_ref, kv_ref, do_ref, dq_ref, dkv_ref):
        i = pl.program_id(1)

        @pl.when(i == 0)
        def _():
            dkv_ref[...] = jnp.zeros_like(dkv_ref)

        for h in range(MEM_HEADS):
            ks = slice(h * hd, (h + 1) * hd)
            vs = slice(MEM_WIDTH + h * hd, MEM_WIDTH + (h + 1) * hd)
            qh, kh, vh, doh = q_ref[:, ks], kv_ref[:, ks], kv_ref[:, vs], do_ref[:, ks]
            p = _mem_probs(qh, kh)
            dp = _dot(doh, vh, 'nt')
            ds = p * (dp - jnp.sum(p * dp, axis=-1, keepdims=True))
            dq_ref[:, ks] = (_dot(ds, kh) * MEM_SCALE).astype(BF16)
            dkv_ref[:, ks] += _dot(ds, qh, 'tn') * MEM_SCALE
            dkv_ref[:, vs] += _dot(p, doh, 'tn')

    return pl.pallas_call(
        body, name="mem_bwd", grid=(Bl, S // tq),
        in_specs=[pl.BlockSpec((None, tq, MEM_WIDTH), lambda b, i: (b, i, 5)),
                  pl.BlockSpec((None, MEM_LEN, 2 * MEM_WIDTH), lambda b, i: (b, 0, 0)),
                  pl.BlockSpec((None, tq, MEM_WIDTH), lambda b, i: (b, i, 0))],
        out_specs=[pl.BlockSpec((None, tq, MEM_WIDTH), lambda b, i: (b, i, 0)),
                   pl.BlockSpec((None, MEM_LEN, 2 * MEM_WIDTH), lambda b, i: (b, 0, 0))],
        out_shape=[jax.ShapeDtypeStruct((Bl, S, MEM_WIDTH), BF16),
                   jax.ShapeDtypeStruct((Bl, MEM_LEN, 2 * MEM_WIDTH), F32)],
        compiler_params=_params(("parallel", "arbitrary")))(proj3, kv3, do)


def _position():
    x, y, c = lax.axis_index("x"), lax.axis_index("y"), lax.axis_index("c")
    return x, y, c, 4 * x + 2 * y + c


def _device(idx):
    return ((idx >> 2) & 1, (idx >> 1) & 1, idx & 1)


def _half_block(ref, axis, shard_shape, k, h):
    R, Cn = shard_shape
    if axis == 1:
        return ref.at[pl.ds(h * (R // 2), R // 2), pl.ds(k * Cn, Cn)]
    return ref.at[pl.ds(k * R + h * (R // 2), R // 2), :]


def _block(ref, axis, shard_shape, k):
    R, Cn = shard_shape
    if axis == 1:
        return ref.at[:, pl.ds(k * Cn, Cn)]
    return ref.at[pl.ds(k * R, R), :]


def _half(ref, h):
    R = ref.shape[0]
    return ref.at[pl.ds(h * (R // 2), R // 2), :]


ANY = pl.BlockSpec(memory_space=pl.ANY)


def _all_gather_weights(shards, axes):
    n = len(shards)
    shapes = [s.shape for s in shards]

    def body(*refs):
        ins, outs = refs[:n], refs[n:2 * n]
        send, recv, local = refs[2 * n:]
        x, y, c, me = _position()
        chip = me >> 1
        sib = _device(me ^ 1)

        def ici(w, j, src_chip, to):
            dst = _half_block(outs[w], axes[w], shapes[w], src_chip, c)
            src = _half(ins[w], c) if to is not None else dst
            return pltpu.make_async_remote_copy(src_ref=src, dst_ref=dst, send_sem=send.at[w, j],
                                                recv_sem=recv.at[w, j], device_id=to if to is not None else sib,
                                                device_id_type=MESH)

        def d2d(w, j, src_chip, half):
            blk = _half_block(outs[w], axes[w], shapes[w], src_chip, half)
            return pltpu.make_async_remote_copy(src_ref=blk, dst_ref=blk, send_sem=send.at[w, 3 + j],
                                                recv_sem=recv.at[w, 3 + j], device_id=sib, device_id_type=MESH)

        mine = [pltpu.make_async_copy(ins[w], _block(outs[w], axes[w], shapes[w], chip), local.at[w])
                for w in range(n)]
        for cp in mine:
            cp.start()
        for w in range(n):
            for j in range(3):
                ici(w, j, chip, _device((me ^ (2 * (j + 1))))).start()
        for w in range(n):
            for j in range(3):
                other = chip ^ (j + 1)
                ici(w, j, other, None).wait_recv()
                d2d(w, j, other, c).start()
        for w in range(n):
            for j in range(3):
                d2d(w, j, chip ^ (j + 1), 1 - c).wait_recv()
        for w in range(n):
            for j in range(3):
                ici(w, j, chip, _device((me ^ (2 * (j + 1))))).wait_send()
                d2d(w, j, chip ^ (j + 1), c).wait_send()
            mine[w].wait()

    out_shape = []
    for sh, a in zip(shards, axes):
        R, Cn = sh.shape
        out_shape.append(jax.ShapeDtypeStruct((R, 4 * Cn) if a == 1 else (4 * R, Cn), sh.dtype))
    return pl.pallas_call(
        body, name="all_gather_weights", in_specs=[ANY] * n, out_specs=[ANY] * n, out_shape=out_shape,
        scratch_shapes=[pltpu.SemaphoreType.DMA((n, 6)), pltpu.SemaphoreType.DMA((n, 6)),
                        pltpu.SemaphoreType.DMA((n,))],
        compiler_params=pltpu.CompilerParams(has_side_effects=True))(*shards)


def _scatter_grads(grads, axes):
    n = len(grads)
    shapes = []
    for g, a in zip(grads, axes):
        shapes.append((g.shape[0], g.shape[1] // 4) if a == 1 else (g.shape[0] // 4, g.shape[1]))

    def body(*refs):
        ins, outs = refs[:n], refs[n:2 * n]
        send, recv, local = refs[2 * n:]
        x, y, c, me = _position()

        def copy(w, m):
            peer = me ^ m
            src = _half_block(ins[w], axes[w], shapes[w], peer >> 1, peer & 1)
            return pltpu.make_async_remote_copy(src_ref=src, dst_ref=outs[w].at[me], send_sem=send.at[w, m - 1],
                                                recv_sem=recv.at[w, m - 1], device_id=_device(peer),
                                                device_id_type=MESH)

        def arrival(w, m):
            src = _half_block(ins[w], axes[w], shapes[w], me >> 1, me & 1)
            return pltpu.make_async_remote_copy(src_ref=src, dst_ref=outs[w].at[me ^ m], send_sem=send.at[w, m - 1],
                                                recv_sem=recv.at[w, m - 1], device_id=_device(me ^ m),
                                                device_id_type=MESH)

        mine = [pltpu.make_async_copy(_half_block(ins[w], axes[w], shapes[w], me >> 1, me & 1), outs[w].at[me],
                                      local.at[w]) for w in range(n)]
        for cp in mine:
            cp.start()
        for w in range(n):
            for m in range(1, N_DEV):
                copy(w, m).start()
        for w in range(n):
            for m in range(1, N_DEV):
                arrival(w, m).wait_recv()
        for w in range(n):
            for m in range(1, N_DEV):
                copy(w, m).wait_send()
            mine[w].wait()

    out_shape = [jax.ShapeDtypeStruct((N_DEV, R // 2, Cn), BF16) for R, Cn in shapes]
    return pl.pallas_call(
        body, name="scatter_grads", in_specs=[ANY] * n, out_specs=[ANY] * n, out_shape=out_shape,
        scratch_shapes=[pltpu.SemaphoreType.DMA((n, N_DEV - 1)), pltpu.SemaphoreType.DMA((n, N_DEV - 1)),
                        pltpu.SemaphoreType.DMA((n,))],
        compiler_params=pltpu.CompilerParams(has_side_effects=True))(*grads)


def _sum_partials(parts, name):
    _, R, Cn = parts.shape
    tr = _pick(R, (256, 176, 128, 64, 32, 16, 8))

    def body(p_ref, o_ref):
        acc = p_ref[0].astype(F32)
        for d in range(1, N_DEV):
            acc = acc + p_ref[d].astype(F32)
        o_ref[...] = acc

    return pl.pallas_call(
        body, name=name, grid=(R // tr,), in_specs=[pl.BlockSpec((N_DEV, tr, Cn), lambda i: (0, i, 0))],
        out_specs=pl.BlockSpec((tr, Cn), lambda i: (i, 0)), out_shape=jax.ShapeDtypeStruct((R, Cn), F32),
        compiler_params=_params(("parallel",)))(parts)


def _exchange_halves(halves):
    n = len(halves)

    def body(*refs):
        ins, outs = refs[:n], refs[n:2 * n]
        send, recv, local = refs[2 * n:]
        x, y, c, me = _position()
        sib = _device(me ^ 1)

        def copy(w, half):
            return pltpu.make_async_remote_copy(src_ref=ins[w], dst_ref=_half(outs[w], half), send_sem=send.at[w],
                                                recv_sem=recv.at[w], device_id=sib, device_id_type=MESH)

        mine = [pltpu.make_async_copy(ins[w], _half(outs[w], c), local.at[w]) for w in range(n)]
        for w in range(n):
            mine[w].start()
            copy(w, c).start()
        for w in range(n):
            copy(w, 1 - c).wait_recv()
        for w in range(n):
            copy(w, c).wait_send()
            mine[w].wait()

    out_shape = [jax.ShapeDtypeStruct((2 * h.shape[0], h.shape[1]), F32) for h in halves]
    return pl.pallas_call(
        body, name="exchange_halves", in_specs=[ANY] * n, out_specs=[ANY] * n, out_shape=out_shape,
        scratch_shapes=[pltpu.SemaphoreType.DMA((n,)), pltpu.SemaphoreType.DMA((n,)),
                        pltpu.SemaphoreType.DMA((n,))],
        compiler_params=pltpu.CompilerParams(has_side_effects=True))(*halves)


def _all_reduce_small(vec):
    R, L = vec.shape

    def body(v_ref, o_ref, buf, send, recv):
        x, y, c, me = _position()
        buf[me] = v_ref[...]

        def copy(m, slot):
            return pltpu.make_async_remote_copy(src_ref=v_ref, dst_ref=buf.at[slot], send_sem=send.at[m - 1],
                                                recv_sem=recv.at[m - 1], device_id=_device(me ^ m),
                                                device_id_type=MESH)

        for m in range(1, N_DEV):
            copy(m, me).start()
        for m in range(1, N_DEV):
            copy(m, me ^ m).wait_recv()
        for m in range(1, N_DEV):
            copy(m, me).wait_send()
        acc = buf[0]
        for d in range(1, N_DEV):
            acc = acc + buf[d]
        o_ref[...] = acc

    vm = pl.BlockSpec(memory_space=pltpu.VMEM)
    return pl.pallas_call(
        body, name="all_reduce_small", in_specs=[vm], out_specs=vm, out_shape=jax.ShapeDtypeStruct((R, L), F32),
        scratch_shapes=[pltpu.VMEM((N_DEV, R, L), F32), pltpu.SemaphoreType.DMA((N_DEV - 1,)),
                        pltpu.SemaphoreType.DMA((N_DEV - 1,))],
        compiler_params=pltpu.CompilerParams(has_side_effects=True))(vec)


def _adamw(w, g, m, v, name):
    R, Cn = w.shape
    tr = _pick(R, (256, 176, 128, 64, 40, 32, 16, 8))

    def body(w_ref, g_ref, m_ref, v_ref, d_ref, nm_ref, nv_ref):
        gv = g_ref[...]
        nm = ADAM_B1 * m_ref[...] + (1.0 - ADAM_B1) * gv
        nv = ADAM_B2 * v_ref[...] + (1.0 - ADAM_B2) * (gv * gv)
        m_hat = nm / (1.0 - ADAM_B1 ** ADAM_STEP)
        v_hat = nv / (1.0 - ADAM_B2 ** ADAM_STEP)
        d_ref[...] = -ADAM_LR * (m_hat / (jnp.sqrt(v_hat) + ADAM_EPS) + ADAM_WD * w_ref[...])
        nm_ref[...] = nm
        nv_ref[...] = nv

    spec = pl.BlockSpec((tr, Cn), lambda i: (i, 0))
    return pl.pallas_call(
        body, name=name, grid=(R // tr,), in_specs=[spec] * 4, out_specs=[spec] * 3,
        out_shape=[jax.ShapeDtypeStruct((R, Cn), F32)] * 3, compiler_params=_params(("parallel",)))(w, g, m, v)


def _pack(arrays, rows):
    flat = jnp.concatenate([a.reshape(-1).astype(F32) for a in arrays])
    return jnp.pad(flat, (0, rows * 128 - flat.shape[0])).reshape(rows, 128)


def _unpack(packed, shapes):
    flat = packed.reshape(-1)
    out, off = [], 0
    for s in shapes:
        size = 1
        for d in s:
            size *= d
        out.append(flat[off:off + size].reshape(s))
        off += size
    return out


def _ffn_fwd(x, norm, w_up, w_down, tag):
    h = _rms_fwd(x, norm, f"{tag}_norm")
    up = _mm(h, w_up, 'nn', f"{tag}_up", BF16)
    act = _swiglu_fwd(up, f"{tag}_swiglu")
    out = _mm(act, w_down, 'nn', f"{tag}_down", F32, res=x, scale=0.5)
    return out, (h, up, act)


def _ffn_bwd(dout, x, norm, w_up, w_down, saved, tag):
    h, up, act = saved
    dact = _mm(dout, w_down, 'nt', f"{tag}_down_dx", BF16, scale=0.5)
    g_down = _mm(act, dout, 'tn', f"{tag}_down_dw", BF16, scale=0.5)
    dup = _swiglu_bwd(dact, up, f"{tag}_swiglu_bwd")
    dh = _mm(dup, w_up, 'nt', f"{tag}_up_dx", F32)
    g_up = _mm(h, dup, 'tn', f"{tag}_up_dw", BF16)
    dx, g_norm = _rms_bwd(x, norm, dh, dout, f"{tag}_norm_bwd")
    return dx, g_norm, g_up, g_down


def kernel(x, mem, ffn1_norm, ffn1_w_up, ffn1_w_down, mix_norm, mem_norm, w_in, b_gate, conv_dw_w, conv_dw_b, conv_ln_g, conv_ln_b, conv_w_pw, att_rel_bias, att_w_o, mem_w_kv, mem_w_o, w_out, ffn2_norm, ffn2_w_up, ffn2_w_down, final_norm, loss_target, m_ffn1_norm, m_ffn1_w_up, m_ffn1_w_down, m_mix_norm, m_mem_norm, m_w_in, m_b_gate, m_conv_dw_w, m_conv_dw_b, m_conv_ln_g, m_conv_ln_b, m_conv_w_pw, m_att_rel_bias, m_att_w_o, m_mem_w_kv, m_mem_w_o, m_w_out, m_ffn2_norm, m_ffn2_w_up, m_ffn2_w_down, m_final_norm, v_ffn1_norm, v_ffn1_w_up, v_ffn1_w_down, v_mix_norm, v_mem_norm, v_w_in, v_b_gate, v_conv_dw_w, v_conv_dw_b, v_conv_ln_g, v_conv_ln_b, v_conv_w_pw, v_att_rel_bias, v_att_w_o, v_mem_w_kv, v_mem_w_o, v_w_out, v_ffn2_norm, v_ffn2_w_up, v_ffn2_w_down, v_final_norm):
    given = dict(locals())
    wts = {n: given[n] for n in WEIGHTS}
    mom1 = {n: given["m_" + n] for n in WEIGHTS}
    mom2 = {n: given["v_" + n] for n in WEIGHTS}
    Bl, S, Dm = x.shape
    T = Bl * S
    x0 = x.reshape(T, Dm)
    tgt = loss_target.reshape(T, Dm)
    mem2 = mem.reshape(Bl * MEM_LEN, Dm)

    big_names = [n for n, _ in BIG]
    big_axes = [a for _, a in BIG]
    chip = 2 * lax.axis_index("x") + lax.axis_index("y")

    shards = [wts[n][0].astype(BF16) for n in big_names] + [jnp.pad(conv_dw_w[0], ((0, 1), (0, 0)))]
    gathered = _all_gather_weights(shards, big_axes + [1])
    full = dict(zip(big_names, gathered[:-1]))
    dw_full = gathered[-1][:CONV_KERNEL]

    final_g = final_norm.reshape(1, Dm)
    bias = _att_bias(att_rel_bias[0])

    x1, ffn1_saved = _ffn_fwd(x0, ffn1_norm, full['ffn1_w_up'], full['ffn1_w_down'], "ffn1")
    h = _rms_fwd(x1, mix_norm, "mix_norm")
    proj = _mm(h, full['w_in'], 'nn', "w_in", BF16)
    proj3 = proj.reshape(Bl, S, proj.shape[1])
    cv, c_act = _conv_fwd(proj3, dw_full, conv_dw_b, conv_ln_g, conv_ln_b)
    o_att = _att_fwd(proj3, bias)
    mem_h = _rms_fwd(mem2, mem_norm, "mem_norm")
    kv = _mm(mem_h, full['mem_w_kv'], 'nn', "mem_kv", BF16)
    kv3 = kv.reshape(Bl, MEM_LEN, 2 * MEM_WIDTH)
    o_mem = _mem_fwd(proj3, kv3)
    c_act2, o_att2, o_mem2 = c_act.reshape(T, -1), o_att.reshape(T, -1), o_mem.reshape(T, -1)
    x2, yc, ya, ym, y = _mix_fwd(c_act2, o_att2, o_mem2, proj, b_gate, x1, full['conv_w_pw'], full['att_w_o'],
                                 full['mem_w_o'], full['w_out'])
    x3, ffn2_saved = _ffn_fwd(x2, ffn2_norm, full['ffn2_w_up'], full['ffn2_w_down'], "ffn2")
    dx3, g_final, loss_vec = _final_fwd_bwd(x3, tgt, final_g)

    g = {}
    dx2, g['ffn2_norm'], g['ffn2_w_up'], g['ffn2_w_down'] = _ffn_bwd(
        dx3, x2, ffn2_norm, full['ffn2_w_up'], full['ffn2_w_down'], ffn2_saved, "ffn2")
    dyc, dya, dym, dgl, g['b_gate'], dc, doa, dom = _mix_bwd(
        dx2, yc, ya, ym, proj, b_gate, full['conv_w_pw'], full['att_w_o'], full['mem_w_o'], full['w_out'])
    g['w_out'] = _mm(y, dx2, 'tn', "w_out_dw", BF16)
    g['conv_w_pw'] = _mm(c_act2, dyc, 'tn', "conv_pw_dw", BF16)
    g['att_w_o'] = _mm(o_att2, dya, 'tn', "att_o_dw", BF16)
    g['mem_w_o'] = _mm(o_mem2, dym, 'tn', "mem_o_dw", BF16)
    du, g_dw, g['conv_dw_b'], g['conv_ln_g'], g['conv_ln_b'] = _conv_bwd(
        proj3, cv, dc.reshape(Bl, S, -1), dw_full, conv_ln_g, conv_ln_b)
    dq, dk, dv, dbias = _att_bwd(proj3, doa.reshape(Bl, S, -1), bias)
    g['att_rel_bias'] = _rel_bias_grad(dbias)
    dmq, dkv = _mem_bwd(proj3, kv3, dom.reshape(Bl, S, -1))
    dkv2 = dkv.reshape(Bl * MEM_LEN, 2 * MEM_WIDTH)
    g['mem_w_kv'] = _mm(mem_h, dkv2, 'tn', "mem_kv_dw", BF16)
    dmem_h = _mm(dkv2, full['mem_w_kv'], 'nt', "mem_kv_dx", F32)
    _, g['mem_norm'] = _rms_bwd(mem2, mem_norm, dmem_h, dmem_h, "mem_norm_bwd")
    dproj = jnp.concatenate([du.reshape(T, -1), dq.reshape(T, -1), dk.reshape(T, -1), dv.reshape(T, -1),
                             dmq.reshape(T, -1), dgl], axis=1)
    dh = _mm(dproj, full['w_in'], 'nt', "w_in_dx", F32)
    g['w_in'] = _mm(h, dproj, 'tn', "w_in_dw", BF16)
    dx1, g['mix_norm'] = _rms_bwd(x1, mix_norm, dh, dx2, "mix_norm_bwd")
    dx0, g['ffn1_norm'], g['ffn1_w_up'], g['ffn1_w_down'] = _ffn_bwd(
        dx1, x0, ffn1_norm, full['ffn1_w_up'], full['ffn1_w_down'], ffn1_saved, "ffn1")
    g['final_norm'] = g_final

    parts = _scatter_grads([g[n] for n in big_names], big_axes)
    halves = [_sum_partials(p, f"sum_{n}") for p, n in zip(parts, big_names)]
    for n, sg in zip(big_names, _exchange_halves(halves)):
        g[n] = sg

    small_shapes = [wts[n].shape for n in SMALL]
    n_small = sum(int(wts[n].size) for n in SMALL)
    n_red = n_small + CONV_KERNEL * CONV_WIDTH
    red = _all_reduce_small(_pack([g[n] for n in SMALL] + [g_dw], -(-n_red // 1024) * 8))
    red_list = _unpack(red, small_shapes + [(CONV_KERNEL, CONV_WIDTH)])
    for n, rg in zip(SMALL, red_list[:-1]):
        g[n] = rg
    dw_cols = conv_dw_w.shape[2]
    g['conv_dw_w'] = lax.dynamic_slice(red_list[-1], (0, chip * dw_cols), (CONV_KERNEL, dw_cols))[None]

    delta, new_m, new_v = {}, {}, {}
    for n in big_names:
        g[n] = g[n][None]
        d, nm, nv = _adamw(wts[n][0], g[n][0], mom1[n][0], mom2[n][0], f"adamw_{n}")
        delta[n], new_m[n], new_v[n] = d[None], nm[None], nv[None]
    rest = SMALL + ['conv_dw_w']
    rest_shapes = [wts[n].shape for n in rest]
    rows = -(-sum(int(wts[n].size) for n in rest) // 1024) * 8
    packed = [_pack([src[n] for n in rest], rows) for src in (wts, g, mom1, mom2)]
    for out, res in zip((delta, new_m, new_v), _adamw(*packed, "adamw_small")):
        for n, a in zip(rest, _unpack(res, rest_shapes)):
            out[n] = a

    loss = lax.psum(loss_vec[0, 0], ("x", "y", "c"))
    grad_x = dx0.reshape(Bl, S, Dm)
    return (loss, grad_x, *[g[n] for n in WEIGHTS], *[delta[n] for n in WEIGHTS],
            *[new_m[n] for n in WEIGHTS], *[new_v[n] for n in WEIGHTS])
```

```python
import jax
import jax.numpy as jnp
from jax import lax
from jax.experimental import pallas as pl
from jax.experimental.pallas import tpu as pltpu

F32 = jnp.float32
BF16 = jnp.bfloat16

D_MODEL = 1024
D_FF = 2816
CHUNK = 64
LEFT_CHUNKS = 8
MAX_REL = 128
N_REL = (CHUNK - 1) + MAX_REL + 1
CONV_WIDTH = 512
CONV_KERNEL = 31
ATT_HEADS = 8
ATT_WIDTH = 512
MEM_LEN = 256
MEM_HEADS = 4
MEM_HEAD_DIM = 128
MEM_WIDTH = 512
EPS = 1e-6
MASK_VALUE = -1e30
ATT_SCALE = 64 ** -0.5
MEM_SCALE = 128 ** -0.5

ADAM_LR = 0.001
ADAM_B1 = 0.9
ADAM_B2 = 0.999
ADAM_EPS = 1e-08
ADAM_WD = 0.01
ADAM_STEP = 10

QB = 256
KW = 3 * QB
CONV_PAD = 32
CONV_TILE = 256

VMEM_LIMIT = 56 << 20

WEIGHTS = ['ffn1_norm', 'ffn1_w_up', 'ffn1_w_down', 'mix_norm', 'mem_norm', 'w_in', 'b_gate', 'conv_dw_w',
           'conv_dw_b', 'conv_ln_g', 'conv_ln_b', 'conv_w_pw', 'att_rel_bias', 'att_w_o', 'mem_w_kv', 'mem_w_o',
           'w_out', 'ffn2_norm', 'ffn2_w_up', 'ffn2_w_down', 'final_norm']
BIG = [('ffn1_w_up', 1), ('ffn1_w_down', 0), ('w_in', 1), ('conv_w_pw', 1), ('att_w_o', 1), ('mem_w_kv', 0),
       ('mem_w_o', 1), ('w_out', 0), ('ffn2_w_up', 1), ('ffn2_w_down', 0)]
SMALL = ['ffn1_norm', 'mix_norm', 'mem_norm', 'b_gate', 'conv_dw_b', 'conv_ln_g', 'conv_ln_b', 'att_rel_bias',
         'ffn2_norm', 'final_norm']
N_CHIPS = 4
N_DEV = 8
MESH = pl.DeviceIdType.MESH


def _pick(n, cands):
    for c in cands:
        if n % c == 0:
            return c
    return n


def _sig(x):
    return 1.0 / (1.0 + jnp.exp(-x))


def _params(sem=None, vmem=VMEM_LIMIT):
    return pltpu.CompilerParams(dimension_semantics=sem, vmem_limit_bytes=vmem)


def _dot(a, b, mode='nn'):
    dims = {'nn': (((1,), (0,)), ((), ())), 'nt': (((1,), (1,)), ((), ())), 'tn': (((0,), (0,)), ((), ()))}[mode]
    return lax.dot_general(a.astype(BF16), b.astype(BF16), dims, preferred_element_type=F32)


def _mm(a, b, mode, name, out_dtype, res=None, scale=1.0):
    if mode == 'nn':
        (M, C), (_, N) = a.shape, b.shape
    elif mode == 'nt':
        (M, C), (N, _) = a.shape, b.shape
    else:
        (C, M), (_, N) = a.shape, b.shape
    tm = _pick(M, (1024, 1408, 512, 256, 128))
    tn = _pick(N, (1024, 1408, 512, 256, 128))
    tc = _pick(C, (1024, 1408, 512, 256, 128))
    nk = C // tc
    if mode == 'nn':
        a_spec = pl.BlockSpec((tm, tc), lambda i, j, k: (i, k))
        b_spec = pl.BlockSpec((tc, tn), lambda i, j, k: (k, j))
    elif mode == 'nt':
        a_spec = pl.BlockSpec((tm, tc), lambda i, j, k: (i, k))
        b_spec = pl.BlockSpec((tn, tc), lambda i, j, k: (j, k))
    else:
        a_spec = pl.BlockSpec((tc, tm), lambda i, j, k: (k, i))
        b_spec = pl.BlockSpec((tc, tn), lambda i, j, k: (k, j))
    o_spec = pl.BlockSpec((tm, tn), lambda i, j, k: (i, j))
    has_res = res is not None

    def body(*refs):
        if has_res:
            a_ref, b_ref, r_ref, o_ref, acc_ref = refs
        else:
            a_ref, b_ref, o_ref, acc_ref = refs
            r_ref = None
        k = pl.program_id(2)
        p = _dot(a_ref[...], b_ref[...], mode)

        def finish(acc):
            if scale != 1.0:
                acc = acc * scale
            if r_ref is not None:
                acc = r_ref[...] + acc
            o_ref[...] = acc.astype(o_ref.dtype)

        if nk == 1:
            finish(p)
        else:
            @pl.when(k == 0)
            def _():
                acc_ref[...] = p

            @pl.when(k > 0)
            def _():
                acc_ref[...] += p

            @pl.when(k == nk - 1)
            def _():
                finish(acc_ref[...])

    in_specs = [a_spec, b_spec] + ([o_spec] if has_res else [])
    args = (a, b) + ((res,) if has_res else ())
    acc_shape = (tm, tn) if nk > 1 else (8, 128)
    return pl.pallas_call(
        body, name=name, grid=(M // tm, N // tn, nk), in_specs=in_specs, out_specs=o_spec,
        out_shape=jax.ShapeDtypeStruct((M, N), out_dtype), scratch_shapes=[pltpu.VMEM(acc_shape, F32)],
        compiler_params=_params(("parallel", "parallel", "arbitrary")))(*args)


def _row_tile(T):
    return _pick(T, (512, 256, 128, 64, 32, 16, 8))


def _rms_fwd(x, g, name):
    T, Dm = x.shape
    tm = _row_tile(T)

    def body(x_ref, g_ref, o_ref):
        xv = x_ref[...]
        r = lax.rsqrt(jnp.mean(xv * xv, axis=-1, keepdims=True) + EPS)
        o_ref[...] = ((xv * r) * g_ref[...]).astype(o_ref.dtype)

    return pl.pallas_call(
        body, name=name, grid=(T // tm,),
        in_specs=[pl.BlockSpec((tm, Dm), lambda i: (i, 0)), pl.BlockSpec((1, Dm), lambda i: (0, 0))],
        out_specs=pl.BlockSpec((tm, Dm), lambda i: (i, 0)), out_shape=jax.ShapeDtypeStruct((T, Dm), BF16),
        compiler_params=_params(("parallel",)))(x, g)


def _rms_bwd(x, g, dh, dres, name):
    T, Dm = x.shape
    tm = _row_tile(T)

    def body(x_ref, g_ref, dh_ref, dr_ref, dx_ref, dg_ref):
        i = pl.program_id(0)
        xv = x_ref[...]
        r = lax.rsqrt(jnp.mean(xv * xv, axis=-1, keepdims=True) + EPS)
        xr = xv * r
        dh_v = dh_ref[...].astype(F32)
        dyg = dh_v * g_ref[...]
        dx = r * (dyg - xr * jnp.mean(dyg * xr, axis=-1, keepdims=True))
        dx_ref[...] = dr_ref[...] + dx

        @pl.when(i == 0)
        def _():
            dg_ref[...] = jnp.zeros_like(dg_ref)

        dg_ref[...] += jnp.sum(dh_v * xr, axis=0, keepdims=True)

    row = pl.BlockSpec((tm, Dm), lambda i: (i, 0))
    vec = pl.BlockSpec((1, Dm), lambda i: (0, 0))
    return pl.pallas_call(
        body, name=name, grid=(T // tm,), in_specs=[row, vec, row, row], out_specs=[row, vec],
        out_shape=[jax.ShapeDtypeStruct((T, Dm), F32), jax.ShapeDtypeStruct((1, Dm), F32)],
        compiler_params=_params(("arbitrary",)))(x, g, dh, dres)


def _final_fwd_bwd(x3, tgt, g):
    T, Dm = x3.shape
    tm = _row_tile(T)

    def body(x_ref, t_ref, g_ref, dx_ref, dg_ref, loss_ref):
        i = pl.program_id(0)
        xv = x_ref[...]
        gg = g_ref[...]
        r = lax.rsqrt(jnp.mean(xv * xv, axis=-1, keepdims=True) + EPS)
        xr = xv * r
        err = xr * gg - t_ref[...]
        dout = err * (1.0 / Dm)
        dyg = dout * gg
        dx_ref[...] = r * (dyg - xr * jnp.mean(dyg * xr, axis=-1, keepdims=True))

        @pl.when(i == 0)
        def _():
            dg_ref[...] = jnp.zeros_like(dg_ref)
            loss_ref[...] = jnp.zeros_like(loss_ref)

        dg_ref[...] += jnp.sum(dout * xr, axis=0, keepdims=True)
        loss_ref[...] += jnp.zeros_like(loss_ref) + (0.5 / Dm) * jnp.sum(err * err)

    row = pl.BlockSpec((tm, Dm), lambda i: (i, 0))
    vec = pl.BlockSpec((1, Dm), lambda i: (0, 0))
    one = pl.BlockSpec((1, 128), lambda i: (0, 0))
    return pl.pallas_call(
        body, name="final_fwd_bwd", grid=(T // tm,), in_specs=[row, row, vec], out_specs=[row, vec, one],
        out_shape=[jax.ShapeDtypeStruct((T, Dm), F32), jax.ShapeDtypeStruct((1, Dm), F32),
                   jax.ShapeDtypeStruct((1, 128), F32)],
        compiler_params=_params(("arbitrary",)))(x3, tgt, g)


def _swiglu_fwd(up, name):
    T, F2 = up.shape
    Fh = F2 // 2
    tm = _pick(T, (256, 128, 64, 32, 16, 8))

    def body(up_ref, o_ref):
        a = up_ref[:, :Fh].astype(F32)
        b = up_ref[:, Fh:].astype(F32)
        o_ref[...] = (a * _sig(a) * b).astype(o_ref.dtype)

    return pl.pallas_call(
        body, name=name, grid=(T // tm,), in_specs=[pl.BlockSpec((tm, F2), lambda i: (i, 0))],
        out_specs=pl.BlockSpec((tm, Fh), lambda i: (i, 0)), out_shape=jax.ShapeDtypeStruct((T, Fh), BF16),
        compiler_params=_params(("parallel",)))(up)


def _swiglu_bwd(dact, up, name):
    T, F2 = up.shape
    Fh = F2 // 2
    tm = _pick(T, (256, 128, 64, 32, 16, 8))

    def body(d_ref, up_ref, o_ref):
        a = up_ref[:, :Fh].astype(F32)
        b = up_ref[:, Fh:].astype(F32)
        d = d_ref[...].astype(F32)
        s = _sig(a)
        o_ref[:, :Fh] = (d * b * s * (1.0 + a * (1.0 - s))).astype(o_ref.dtype)
        o_ref[:, Fh:] = (d * a * s).astype(o_ref.dtype)

    return pl.pallas_call(
        body, name=name, grid=(T // tm,),
        in_specs=[pl.BlockSpec((tm, Fh), lambda i: (i, 0)), pl.BlockSpec((tm, F2), lambda i: (i, 0))],
        out_specs=pl.BlockSpec((tm, F2), lambda i: (i, 0)), out_shape=jax.ShapeDtypeStruct((T, F2), BF16),
        compiler_params=_params(("parallel",)))(dact, up)


def _mix_fwd(c_act, o_att, o_mem, proj, b_gate, x1, w_pw, w_o, w_mo, w_out):
    T, Dm = x1.shape
    W = c_act.shape[1]
    tm = _pick(T, (256, 128, 64, 32, 16, 8))

    def body(c_ref, oa_ref, om_ref, gl_ref, bg_ref, x1_ref, wpw_ref, wo_ref, wmo_ref, wout_ref,
             x2_ref, yc_ref, ya_ref, ym_ref, y_ref):
        yc = _dot(c_ref[...], wpw_ref[...])
        ya = _dot(oa_ref[...], wo_ref[...])
        ym = _dot(om_ref[...], wmo_ref[...])
        g = _sig(gl_ref[...].astype(F32) + bg_ref[...])
        y = g[:, :Dm] * yc + g[:, Dm:2 * Dm] * ya + g[:, 2 * Dm:] * ym
        x2_ref[...] = x1_ref[...] + _dot(y, wout_ref[...])
        yc_ref[...] = yc.astype(BF16)
        ya_ref[...] = ya.astype(BF16)
        ym_ref[...] = ym.astype(BF16)
        y_ref[...] = y.astype(BF16)

    rowW = pl.BlockSpec((tm, W), lambda i: (i, 0))
    rowD = pl.BlockSpec((tm, Dm), lambda i: (i, 0))
    full = lambda s: pl.BlockSpec(s, lambda i: (0, 0))
    return pl.pallas_call(
        body, name="mix_fwd", grid=(T // tm,),
        in_specs=[rowW, rowW, rowW, pl.BlockSpec((tm, 3 * Dm), lambda i: (i, 1)), full((1, 3 * Dm)), rowD,
                  full((W, Dm)), full((W, Dm)), full((W, Dm)), full((Dm, Dm))],
        out_specs=[rowD] * 5,
        out_shape=[jax.ShapeDtypeStruct((T, Dm), F32)] + [jax.ShapeDtypeStruct((T, Dm), BF16)] * 4,
        compiler_params=_params(("parallel",)))(c_act, o_att, o_mem, proj, b_gate, x1, w_pw, w_o, w_mo, w_out)


def _mix_bwd(dx2, yc, ya, ym, proj, b_gate, w_pw, w_o, w_mo, w_out):
    T, Dm = dx2.shape
    W = w_pw.shape[0]
    tm = _pick(T, (256, 128, 64, 32, 16, 8))

    def body(dx_ref, yc_ref, ya_ref, ym_ref, gl_ref, bg_ref, wpw_ref, wo_ref, wmo_ref, wout_ref,
             dyc_ref, dya_ref, dym_ref, dgl_ref, dbg_ref, dc_ref, doa_ref, dom_ref):
        i = pl.program_id(0)

        @pl.when(i == 0)
        def _():
            dbg_ref[...] = jnp.zeros_like(dbg_ref)

        dy = _dot(dx_ref[...], wout_ref[...], 'nt')
        g = _sig(gl_ref[...].astype(F32) + bg_ref[...])
        branches = ((yc_ref, dyc_ref, wpw_ref, dc_ref), (ya_ref, dya_ref, wo_ref, doa_ref),
                    (ym_ref, dym_ref, wmo_ref, dom_ref))
        for n, (y_ref, dyk_ref, w_ref, dk_ref) in enumerate(branches):
            gk = g[:, n * Dm:(n + 1) * Dm]
            dyk = dy * gk
            dgl = dyk * y_ref[...].astype(F32) * (1.0 - gk)
            dyk_ref[...] = dyk.astype(BF16)
            dgl_ref[:, n * Dm:(n + 1) * Dm] = dgl.astype(BF16)
            dbg_ref[:, n * Dm:(n + 1) * Dm] += jnp.sum(dgl, axis=0, keepdims=True)
            dk_ref[...] = _dot(dyk, w_ref[...], 'nt').astype(BF16)

    rowW = pl.BlockSpec((tm, W), lambda i: (i, 0))
    rowD = pl.BlockSpec((tm, Dm), lambda i: (i, 0))
    row3 = pl.BlockSpec((tm, 3 * Dm), lambda i: (i, 0))
    full = lambda s: pl.BlockSpec(s, lambda i: (0, 0))
    return pl.pallas_call(
        body, name="mix_bwd", grid=(T // tm,),
        in_specs=[rowD, rowD, rowD, rowD, pl.BlockSpec((tm, 3 * Dm), lambda i: (i, 1)), full((1, 3 * Dm)),
                  full((W, Dm)), full((W, Dm)), full((W, Dm)), full((Dm, Dm))],
        out_specs=[rowD, rowD, rowD, row3, full((1, 3 * Dm)), rowW, rowW, rowW],
        out_shape=[jax.ShapeDtypeStruct((T, Dm), BF16)] * 3 + [jax.ShapeDtypeStruct((T, 3 * Dm), BF16),
                                                                jax.ShapeDtypeStruct((1, 3 * Dm), F32)]
        + [jax.ShapeDtypeStruct((T, W), BF16)] * 3,
        compiler_params=_params(("arbitrary",)))(dx2, yc, ya, ym, proj, b_gate, w_pw, w_o, w_mo, w_out)


def _ln_swish(cv, lg, lb):
    mu = jnp.mean(cv, axis=-1, keepdims=True)
    xc = cv - mu
    r = lax.rsqrt(jnp.mean(xc * xc, axis=-1, keepdims=True) + EPS)
    n = xc * r
    l = n * lg + lb
    return r, n, l


def _conv_fwd(proj3, dw_w, dw_b, ln_g, ln_b):
    Bl, S, _ = proj3.shape
    C, K, TS, PAD = CONV_WIDTH, CONV_KERNEL, CONV_TILE, CONV_PAD
    nt = S // TS

    def body(u_ref, w_ref, b_ref, lg_ref, lb_ref, cv_ref, c_ref, vbuf, win):
        vbuf[0:PAD, :] = jnp.zeros((PAD, C), F32)

        def glu(t, carry):
            r0 = pl.multiple_of(t * TS, TS)
            u = u_ref[pl.ds(r0, TS), :].astype(F32)
            vbuf[pl.ds(PAD + r0, TS), :] = u[:, :C] * _sig(u[:, C:])
            return carry

        lax.fori_loop(0, nt, glu, 0)

        def conv(t, carry):
            r0 = pl.multiple_of(t * TS, TS)
            win[...] = vbuf[pl.ds(r0, TS + PAD), :]
            acc = jnp.zeros((TS, C), F32)
            for j in range(K):
                acc = acc + w_ref[j:j + 1, :] * win[PAD - (K - 1) + j:PAD - (K - 1) + j + TS, :]
            cv = acc + b_ref[...]
            cv_ref[pl.ds(r0, TS), :] = cv
            _, _, l = _ln_swish(cv, lg_ref[...], lb_ref[...])
            c_ref[pl.ds(r0, TS), :] = (l * _sig(l)).astype(BF16)
            return carry

        lax.fori_loop(0, nt, conv, 0)

    vec = pl.BlockSpec((1, C), lambda b: (0, 0))
    return pl.pallas_call(
        body, name="conv_fwd", grid=(Bl,),
        in_specs=[pl.BlockSpec((None, S, 2 * C), lambda b: (b, 0, 0)), pl.BlockSpec((K, C), lambda b: (0, 0)),
                  vec, vec, vec],
        out_specs=[pl.BlockSpec((None, S, C), lambda b: (b, 0, 0))] * 2,
        out_shape=[jax.ShapeDtypeStruct((Bl, S, C), F32), jax.ShapeDtypeStruct((Bl, S, C), BF16)],
        scratch_shapes=[pltpu.VMEM((S + PAD, C), F32), pltpu.VMEM((TS + PAD, C), F32)],
        compiler_params=_params(("parallel",)))(proj3, dw_w, dw_b, ln_g, ln_b)


def _conv_bwd(proj3, cv, dc, dw_w, ln_g, ln_b):
    Bl, S, _ = proj3.shape
    C, K, TS, PAD = CONV_WIDTH, CONV_KERNEL, CONV_TILE, CONV_PAD
    nt = S // TS

    def body(u_ref, cv_ref, dc_ref, w_ref, lg_ref, lb_ref, du_ref, dw_ref, db_ref, dlg_ref, dlb_ref,
             vbuf, gbuf, win, dwacc):
        b = pl.program_id(0)

        @pl.when(b == 0)
        def _():
            dw_ref[...] = jnp.zeros_like(dw_ref)
            db_ref[...] = jnp.zeros_like(db_ref)
            dlg_ref[...] = jnp.zeros_like(dlg_ref)
            dlb_ref[...] = jnp.zeros_like(dlb_ref)

        vbuf[0:PAD, :] = jnp.zeros((PAD, C), F32)
        gbuf[S:S + PAD, :] = jnp.zeros((PAD, C), F32)
        dwacc[...] = jnp.zeros_like(dwacc)

        def norm_bwd(t, carry):
            r0 = pl.multiple_of(t * TS, TS)
            u = u_ref[pl.ds(r0, TS), :].astype(F32)
            vbuf[pl.ds(PAD + r0, TS), :] = u[:, :C] * _sig(u[:, C:])
            r, n, l = _ln_swish(cv_ref[pl.ds(r0, TS), :], lg_ref[...], lb_ref[...])
            s = _sig(l)
            dl = dc_ref[pl.ds(r0, TS), :].astype(F32) * s * (1.0 + l * (1.0 - s))
            dlg_ref[...] += jnp.sum(dl * n, axis=0, keepdims=True)
            dlb_ref[...] += jnp.sum(dl, axis=0, keepdims=True)
            dn = dl * lg_ref[...]
            dcv = r * (dn - jnp.mean(dn, axis=-1, keepdims=True) - n * jnp.mean(dn * n, axis=-1, keepdims=True))
            gbuf[pl.ds(r0, TS), :] = dcv
            db_ref[...] += jnp.sum(dcv, axis=0, keepdims=True)
            return carry

        lax.fori_loop(0, nt, norm_bwd, 0)

        def conv_bwd(t, carry):
            r0 = pl.multiple_of(t * TS, TS)
            win[...] = gbuf[pl.ds(r0, TS + PAD), :]
            dv = jnp.zeros((TS, C), F32)
            for j in range(K):
                dv = dv + w_ref[j:j + 1, :] * win[K - 1 - j:K - 1 - j + TS, :]
            u = u_ref[pl.ds(r0, TS), :].astype(F32)
            a, g = u[:, :C], u[:, C:]
            s = _sig(g)
            du_ref[pl.ds(r0, TS), 0:C] = (dv * s).astype(BF16)
            du_ref[pl.ds(r0, TS), C:2 * C] = (dv * a * s * (1.0 - s)).astype(BF16)
            dcv = gbuf[pl.ds(r0, TS), :]
            win[...] = vbuf[pl.ds(r0, TS + PAD), :]
            for j in range(K):
                prod = dcv * win[PAD - (K - 1) + j:PAD - (K - 1) + j + TS, :]
                dwacc[j] += jnp.sum(prod.reshape(TS // 8, 8, C), axis=0)
            return carry

        lax.fori_loop(0, nt, conv_bwd, 0)
        dw_ref[...] += jnp.sum(dwacc[...], axis=1)

    vec = pl.BlockSpec((1, C), lambda b: (0, 0))
    seq = lambda w: pl.BlockSpec((None, S, w), lambda b: (b, 0, 0))
    return pl.pallas_call(
        body, name="conv_bwd", grid=(Bl,),
        in_specs=[seq(2 * C), seq(C), seq(C), pl.BlockSpec((K, C), lambda b: (0, 0)), vec, vec],
        out_specs=[seq(2 * C), pl.BlockSpec((K, C), lambda b: (0, 0)), vec, vec, vec],
        out_shape=[jax.ShapeDtypeStruct((Bl, S, 2 * C), BF16), jax.ShapeDtypeStruct((K, C), F32)]
        + [jax.ShapeDtypeStruct((1, C), F32)] * 3,
        scratch_shapes=[pltpu.VMEM((S + PAD, C), F32), pltpu.VMEM((S + PAD, C), F32),
                        pltpu.VMEM((TS + PAD, C), F32), pltpu.VMEM((K, 8, C), F32)],
        compiler_params=_params(("arbitrary",)))(proj3, cv, dc, dw_w, ln_g, ln_b)


def _att_bias(rel_bias):
    H = rel_bias.shape[0]
    Wd = KW + QB
    c = jnp.arange(Wd + 1)
    by_offset = rel_bias[:, jnp.clip(KW - c, -(CHUNK - 1), MAX_REL) + (CHUNK - 1)]
    flat = jnp.broadcast_to(by_offset[:, None, :], (H, QB, Wd + 1)).reshape(H, QB * (Wd + 1))
    skew = jnp.pad(flat, ((0, 0), (0, (QB + 1) * Wd - QB * (Wd + 1)))).reshape(H, QB + 1, Wd)[:, :QB, QB:]
    qi = jnp.arange(QB)[:, None]
    kj = jnp.arange(KW)[None, :]
    dchunk = ((KW - QB) + qi) // CHUNK - kj // CHUNK
    band = (dchunk >= 0) & (dchunk <= LEFT_CHUNKS)
    return jnp.where(band[None], skew, MASK_VALUE)


def _head_masks():
    lane = lax.broadcasted_iota(jnp.int32, (1, 128), 1)
    return (lane < 64, lane >= 64)


def _att_probs(qh, k2, bias, valid):
    s = _dot(qh, k2, 'nt') * ATT_SCALE + bias
    s = jnp.where(valid, s, MASK_VALUE)
    e = jnp.exp(s - jnp.max(s, axis=-1, keepdims=True))
    return e * (1.0 / jnp.sum(e, axis=-1, keepdims=True))


def _att_specs(S, q_col):
    nb = S // QB
    q_spec = pl.BlockSpec((None, QB, ATT_WIDTH), lambda b, i: (b, jnp.minimum(i, nb - 1), q_col))

    def kv_spec(col, kb):
        return pl.BlockSpec((None, QB, ATT_WIDTH),
                            lambda b, i: (b, jnp.clip(i - 2 + kb, 0, nb - 1), col))

    return q_spec, [kv_spec(3, kb) for kb in range(3)], [kv_spec(4, kb) for kb in range(3)]


def _att_fwd(proj3, bias):
    Bl, S, _ = proj3.shape
    nb = S // QB
    q_spec, k_specs, v_specs = _att_specs(S, 2)

    def body(q_ref, k0, k1, k2r, v0, v1, v2r, bias_ref, o_ref):
        i = pl.program_id(1)
        masks = _head_masks()
        valid = lax.broadcasted_iota(jnp.int32, (QB, KW), 1) >= (2 - i) * QB
        for pr in range(ATT_HEADS // 2):
            ls = slice(128 * pr, 128 * (pr + 1))
            q2 = q_ref[:, ls]
            k2 = jnp.concatenate([k0[:, ls], k1[:, ls], k2r[:, ls]], axis=0)
            v2 = jnp.concatenate([v0[:, ls], v1[:, ls], v2r[:, ls]], axis=0)
            o2 = jnp.zeros((QB, 128), F32)
            for hh in range(2):
                p = _att_probs(jnp.where(masks[hh], q2, 0), k2, bias_ref[2 * pr + hh], valid)
                o2 = o2 + _dot(p, jnp.where(masks[hh], v2, 0))
            o_ref[:, ls] = o2.astype(BF16)

    return pl.pallas_call(
        body, name="att_fwd", grid=(Bl, nb),
        in_specs=[q_spec] + k_specs + v_specs + [pl.BlockSpec((ATT_HEADS, QB, KW), lambda b, i: (0, 0, 0))],
        out_specs=pl.BlockSpec((None, QB, ATT_WIDTH), lambda b, i: (b, i, 0)),
        out_shape=jax.ShapeDtypeStruct((Bl, S, ATT_WIDTH), BF16),
        compiler_params=_params(("parallel", "arbitrary")))(*([proj3] * 7), bias)


def _att_bwd(proj3, do, bias):
    Bl, S, _ = proj3.shape
    nb = S // QB
    q_spec, k_specs, v_specs = _att_specs(S, 2)
    do_spec = pl.BlockSpec((None, QB, ATT_WIDTH), lambda b, i: (b, jnp.minimum(i, nb - 1), 0))
    kv_out = pl.BlockSpec((None, QB, ATT_WIDTH), lambda b, i: (b, jnp.clip(i - 2, 0, nb - 1), 0))
    bias_spec = pl.BlockSpec((ATT_HEADS, QB, KW), lambda b, i: (0, 0, 0))

    def body(q_ref, k0, k1, k2r, v0, v1, v2r, do_ref, bias_ref, dq_ref, dk_ref, dv_ref, db_ref, dkw, dvw):
        b = pl.program_id(0)
        i = pl.program_id(1)

        @pl.when((b == 0) & (i == 0))
        def _():
            db_ref[...] = jnp.zeros_like(db_ref)

        @pl.when(i == 0)
        def _():
            dkw[...] = jnp.zeros_like(dkw)
            dvw[...] = jnp.zeros_like(dvw)

        @pl.when(i < nb)
        def _():
            masks = _head_masks()
            valid = lax.broadcasted_iota(jnp.int32, (QB, KW), 1) >= (2 - i) * QB
            for pr in range(ATT_HEADS // 2):
                ls = slice(128 * pr, 128 * (pr + 1))
                q2 = q_ref[:, ls]
                do2 = do_ref[:, ls]
                k2 = jnp.concatenate([k0[:, ls], k1[:, ls], k2r[:, ls]], axis=0)
                v2 = jnp.concatenate([v0[:, ls], v1[:, ls], v2r[:, ls]], axis=0)
                dq2 = jnp.zeros((QB, 128), F32)
                dk2 = jnp.zeros((KW, 128), F32)
                dv2 = jnp.zeros((KW, 128), F32)
                for hh in range(2):
                    h = 2 * pr + hh
                    qh = jnp.where(masks[hh], q2, 0)
                    doh = jnp.where(masks[hh], do2, 0)
                    p = _att_probs(qh, k2, bias_ref[h], valid)
                    dp = _dot(doh, v2, 'nt')
                    ds = p * (dp - jnp.sum(p * dp, axis=-1, keepdims=True))
                    db_ref[h] += ds
                    dq2 = dq2 + _dot(ds, jnp.where(masks[hh], k2, 0))
                    dk2 = dk2 + _dot(ds, qh, 'tn')
                    dv2 = dv2 + _dot(p, doh, 'tn')
                dq_ref[:, ls] = (dq2 * ATT_SCALE).astype(BF16)
                dkw[:, ls] += dk2 * ATT_SCALE
                dvw[:, ls] += dv2

        dk_ref[...] = dkw[0:QB, :].astype(BF16)
        dv_ref[...] = dvw[0:QB, :].astype(BF16)
        for buf in (dkw, dvw):
            rest = buf[QB:KW, :]
            buf[0:KW - QB, :] = rest
            buf[KW - QB:KW, :] = jnp.zeros((QB, ATT_WIDTH), F32)

    blk = jax.ShapeDtypeStruct((Bl, S, ATT_WIDTH), BF16)
    return pl.pallas_call(
        body, name="att_bwd", grid=(Bl, nb + 2),
        in_specs=[q_spec] + k_specs + v_specs + [do_spec, bias_spec],
        out_specs=[do_spec, kv_out, kv_out, bias_spec],
        out_shape=[blk, blk, blk, jax.ShapeDtypeStruct((ATT_HEADS, QB, KW), F32)],
        scratch_shapes=[pltpu.VMEM((KW, ATT_WIDTH), F32), pltpu.VMEM((KW, ATT_WIDTH), F32)],
        compiler_params=_params(("arbitrary", "arbitrary")))(*([proj3] * 7), do, bias)


def _rel_bias_grad(dbias):
    H = dbias.shape[0]
    Wd = KW + QB
    padded = jnp.pad(dbias, ((0, 0), (0, 1), (QB, 0)))
    skew = padded.reshape(H, (QB + 1) * Wd)[:, :QB * (Wd + 1)].reshape(H, QB, Wd + 1)[:, :, :Wd]
    c = jnp.arange(Wd)[:, None]
    bins = (jnp.clip(KW - c, -(CHUNK - 1), MAX_REL) + (CHUNK - 1) == jnp.arange(N_REL)[None, :]).astype(F32)

    def body(s_ref, bins_ref, o_ref):
        col = jnp.sum(s_ref[...], axis=1)
        o_ref[...] = jnp.dot(col, bins_ref[...], preferred_element_type=F32, precision=lax.Precision.HIGHEST)

    return pl.pallas_call(
        body, name="rel_bias_grad", grid=(1,),
        in_specs=[pl.BlockSpec((H, QB, Wd), lambda i: (0, 0, 0)), pl.BlockSpec((Wd, N_REL), lambda i: (0, 0))],
        out_specs=pl.BlockSpec((H, N_REL), lambda i: (0, 0)), out_shape=jax.ShapeDtypeStruct((H, N_REL), F32),
        compiler_params=_params(("arbitrary",)))(skew, bins)


MEM_TILE = 512


def _mem_probs(qh, kh):
    s = _dot(qh, kh, 'nt') * MEM_SCALE
    e = jnp.exp(s - jnp.max(s, axis=-1, keepdims=True))
    return e * (1.0 / jnp.sum(e, axis=-1, keepdims=True))


def _mem_fwd(proj3, kv3):
    Bl, S, _ = proj3.shape
    tq = _pick(S, (MEM_TILE, 256))
    hd = MEM_HEAD_DIM

    def body(q_ref, kv_ref, o_ref):
        for h in range(MEM_HEADS):
            p = _mem_probs(q_ref[:, h * hd:(h + 1) * hd], kv_ref[:, h * hd:(h + 1) * hd])
            o_ref[:, h * hd:(h + 1) * hd] = _dot(p, kv_ref[:, MEM_WIDTH + h * hd:MEM_WIDTH + (h + 1) * hd]).astype(BF16)

    return pl.pallas_call(
        body, name="mem_fwd", grid=(Bl, S // tq),
        in_specs=[pl.BlockSpec((None, tq, MEM_WIDTH), lambda b, i: (b, i, 5)),
                  pl.BlockSpec((None, MEM_LEN, 2 * MEM_WIDTH), lambda b, i: (b, 0, 0))],
        out_specs=pl.BlockSpec((None, tq, MEM_WIDTH), lambda b, i: (b, i, 0)),
        out_shape=jax.ShapeDtypeStruct((Bl, S, MEM_WIDTH), BF16),
        compiler_params=_params(("parallel", "parallel")))(proj3, kv3)


def _mem_bwd(proj3, kv3, do):
    Bl, S, _ = proj3.shape
    tq = _pick(S, (MEM_TILE, 256))
    hd = MEM_HEAD_DIM

    def body(q_ref, kv_ref, do_ref, dq_ref, dkv_ref):
        i = pl.program_id(1)

        @pl.when(i == 0)
        def _():
            dkv_ref[...] = jnp.zeros_like(dkv_ref)

        for h in range(MEM_HEADS):
            ks = slice(h * hd, (h + 1) * hd)
            vs = slice(MEM_WIDTH + h * hd, MEM_WIDTH + (h + 1) * hd)
            qh, kh, vh, doh = q_ref[:, ks], kv_ref[:, ks], kv_ref[:, vs], do_ref[:, ks]
            p = _mem_probs(qh, kh)
            dp = _dot(doh, vh, 'nt')
            ds = p * (dp - jnp.sum(p * dp, axis=-1, keepdims=True))
            dq_ref[:, ks] = (_dot(ds, kh) * MEM_SCALE).astype(BF16)
            dkv_ref[:, ks] += _dot(ds, qh, 'tn') * MEM_SCALE
            dkv_ref[:, vs] += _dot(p, doh, 'tn')

    return pl.pallas_call(
        body, name="mem_bwd", grid=(Bl, S // tq),
        in_specs=[pl.BlockSpec((None, tq, MEM_WIDTH), lambda b, i: (b, i, 5)),
                  pl.BlockSpec((None, MEM_LEN, 2 * MEM_WIDTH), lambda b, i: (b, 0, 0)),
                  pl.BlockSpec((None, tq, MEM_WIDTH), lambda b, i: (b, i, 0))],
        out_specs=[pl.BlockSpec((None, tq, MEM_WIDTH), lambda b, i: (b, i, 0)),
                   pl.BlockSpec((None, MEM_LEN, 2 * MEM_WIDTH), lambda b, i: (b, 0, 0))],
        out_shape=[jax.ShapeDtypeStruct((Bl, S, MEM_WIDTH), BF16),
                   jax.ShapeDtypeStruct((Bl, MEM_LEN, 2 * MEM_WIDTH), F32)],
        compiler_params=_params(("parallel", "arbitrary")))(proj3, kv3, do)


def _position():
    x, y, c = lax.axis_index("x"), lax.axis_index("y"), lax.axis_index("c")
    return x, y, c, 4 * x + 2 * y + c


def _device(idx):
    return ((idx >> 2) & 1, (idx >> 1) & 1, idx & 1)


def _half_block(ref, axis, shard_shape, k, h):
    R, Cn = shard_shape
    if axis == 1:
        return ref.at[pl.ds(h * (R // 2), R // 2), pl.ds(k * Cn, Cn)]
    return ref.at[pl.ds(k * R + h * (R // 2), R // 2), :]


def _block(ref, axis, shard_shape, k):
    R, Cn = shard_shape
    if axis == 1:
        return ref.at[:, pl.ds(k * Cn, Cn)]
    return ref.at[pl.ds(k * R, R), :]


def _half(ref, h):
    R = ref.shape[0]
    return ref.at[pl.ds(h * (R // 2), R // 2), :]


ANY = pl.BlockSpec(memory_space=pl.ANY)


def _all_gather_weights(shards, axes):
    n = len(shards)
    shapes = [s.shape for s in shards]

    def body(*refs):
        ins, outs = refs[:n], refs[n:2 * n]
        send, recv, local = refs[2 * n:]
        x, y, c, me = _position()
        chip = me >> 1
        sib = _device(me ^ 1)

        def ici(w, j, src_chip, to):
            dst = _half_block(outs[w], axes[w], shapes[w], src_chip, c)
            src = _half(ins[w], c) if to is not None else dst
            return pltpu.make_async_remote_copy(src_ref=src, dst_ref=dst, send_sem=send.at[w, j],
                                                recv_sem=recv.at[w, j], device_id=to if to is not None else sib,
                                                device_id_type=MESH)

        def d2d(w, j, src_chip, half):
            blk = _half_block(outs[w], axes[w], shapes[w], src_chip, half)
            return pltpu.make_async_remote_copy(src_ref=blk, dst_ref=blk, send_sem=send.at[w, 3 + j],
                                                recv_sem=recv.at[w, 3 + j], device_id=sib, device_id_type=MESH)

        mine = [pltpu.make_async_copy(ins[w], _block(outs[w], axes[w], shapes[w], chip), local.at[w])
                for w in range(n)]
        for cp in mine:
            cp.start()
        for w in range(n):
            for j in range(3):
                ici(w, j, chip, _device((me ^ (2 * (j + 1))))).start()
        for w in range(n):
            for j in range(3):
                other = chip ^ (j + 1)
                ici(w, j, other, None).wait_recv()
                d2d(w, j, other, c).start()
        for w in range(n):
            for j in range(3):
                d2d(w, j, chip ^ (j + 1), 1 - c).wait_recv()
        for w in range(n):
            for j in range(3):
                ici(w, j, chip, _device((me ^ (2 * (j + 1))))).wait_send()
                d2d(w, j, chip ^ (j + 1), c).wait_send()
            mine[w].wait()

    out_shape = []
    for sh, a in zip(shards, axes):
        R, Cn = sh.shape
        out_shape.append(jax.ShapeDtypeStruct((R, 4 * Cn) if a == 1 else (4 * R, Cn), sh.dtype))
    return pl.pallas_call(
        body, name="all_gather_weights", in_specs=[ANY] * n, out_specs=[ANY] * n, out_shape=out_shape,
        scratch_shapes=[pltpu.SemaphoreType.DMA((n, 6)), pltpu.SemaphoreType.DMA((n, 6)),
                        pltpu.SemaphoreType.DMA((n,))],
        compiler_params=pltpu.CompilerParams(has_side_effects=True))(*shards)


def _scatter_grads(grads, axes):
    n = len(grads)
    shapes = []
    for g, a in zip(grads, axes):
        shapes.append((g.shape[0], g.shape[1] // 4) if a == 1 else (g.shape[0] // 4, g.shape[1]))

    def body(*refs):
        ins, outs = refs[:n], refs[n:2 * n]
        send, recv, local = refs[2 * n:]
        x, y, c, me = _position()

        def copy(w, m):
            peer = me ^ m
            src = _half_block(ins[w], axes[w], shapes[w], peer >> 1, peer & 1)
            return pltpu.make_async_remote_copy(src_ref=src, dst_ref=outs[w].at[me], send_sem=send.at[w, m - 1],
                                                recv_sem=recv.at[w, m - 1], device_id=_device(peer),
                                                device_id_type=MESH)

        def arrival(w, m):
            src = _half_block(ins[w], axes[w], shapes[w], me >> 1, me & 1)
            return pltpu.make_async_remote_copy(src_ref=src, dst_ref=outs[w].at[me ^ m], send_sem=send.at[w, m - 1],
                                                recv_sem=recv.at[w, m - 1], device_id=_device(me ^ m),
                                                device_id_type=MESH)

        mine = [pltpu.make_async_copy(_half_block(ins[w], axes[w], shapes[w], me >> 1, me & 1), outs[w].at[me],
                                      local.at[w]) for w in range(n)]
        for cp in mine:
            cp.start()
        for w in range(n):
            for m in range(1, N_DEV):
                copy(w, m).start()
        for w in range(n):
            for m in range(1, N_DEV):
                arrival(w, m).wait_recv()
        for w in range(n):
            for m in range(1, N_DEV):
                copy(w, m).wait_send()
            mine[w].wait()

    out_shape = [jax.ShapeDtypeStruct((N_DEV, R // 2, Cn), BF16) for R, Cn in shapes]
    return pl.pallas_call(
        body, name="scatter_grads", in_specs=[ANY] * n, out_specs=[ANY] * n, out_shape=out_shape,
        scratch_shapes=[pltpu.SemaphoreType.DMA((n, N_DEV - 1)), pltpu.SemaphoreType.DMA((n, N_DEV - 1)),
                        pltpu.SemaphoreType.DMA((n,))],
        compiler_params=pltpu.CompilerParams(has_side_effects=True))(*grads)


def _sum_partials(parts, name):
    _, R, Cn = parts.shape
    tr = _pick(R, (256, 176, 128, 64, 32, 16, 8))

    def body(p_ref, o_ref):
        acc = p_ref[0].astype(F32)
        for d in range(1, N_DEV):
            acc = acc + p_ref[d].astype(F32)
        o_ref[...] = acc

    return pl.pallas_call(
        body, name=name, grid=(R // tr,), in_specs=[pl.BlockSpec((N_DEV, tr, Cn), lambda i: (0, i, 0))],
        out_specs=pl.BlockSpec((tr, Cn), lambda i: (i, 0)), out_shape=jax.ShapeDtypeStruct((R, Cn), F32),
        compiler_params=_params(("parallel",)))(parts)


def _exchange_halves(halves):
    n = len(halves)

    def body(*refs):
        ins, outs = refs[:n], refs[n:2 * n]
        send, recv, local = refs[2 * n:]
        x, y, c, me = _position()
        sib = _device(me ^ 1)

        def copy(w, half):
            return pltpu.make_async_remote_copy(src_ref=ins[w], dst_ref=_half(outs[w], half), send_sem=send.at[w],
                                                recv_sem=recv.at[w], device_id=sib, device_id_type=MESH)

        mine = [pltpu.make_async_copy(ins[w], _half(outs[w], c), local.at[w]) for w in range(n)]
        for w in range(n):
            mine[w].start()
            copy(w, c).start()
        for w in range(n):
            copy(w, 1 - c).wait_recv()
        for w in range(n):
            copy(w, c).wait_send()
            mine[w].wait()

    out_shape = [jax.ShapeDtypeStruct((2 * h.shape[0], h.shape[1]), F32) for h in halves]
    return pl.pallas_call(
        body, name="exchange_halves", in_specs=[ANY] * n, out_specs=[ANY] * n, out_shape=out_shape,
        scratch_shapes=[pltpu.SemaphoreType.DMA((n,)), pltpu.SemaphoreType.DMA((n,)),
                        pltpu.SemaphoreType.DMA((n,))],
        compiler_params=pltpu.CompilerParams(has_side_effects=True))(*halves)


def _all_reduce_small(vec):
    R, L = vec.shape

    def body(v_ref, o_ref, buf, send, recv):
        x, y, c, me = _position()
        buf[me] = v_ref[...]

        def copy(m, slot):
            return pltpu.make_async_remote_copy(src_ref=v_ref, dst_ref=buf.at[slot], send_sem=send.at[m - 1],
                                                recv_sem=recv.at[m - 1], device_id=_device(me ^ m),
                                                device_id_type=MESH)

        for m in range(1, N_DEV):
            copy(m, me).start()
        for m in range(1, N_DEV):
            copy(m, me ^ m).wait_recv()
        for m in range(1, N_DEV):
            copy(m, me).wait_send()
        acc = buf[0]
        for d in range(1, N_DEV):
            acc = acc + buf[d]
        o_ref[...] = acc

    vm = pl.BlockSpec(memory_space=pltpu.VMEM)
    return pl.pallas_call(
        body, name="all_reduce_small", in_specs=[vm], out_specs=vm, out_shape=jax.ShapeDtypeStruct((R, L), F32),
        scratch_shapes=[pltpu.VMEM((N_DEV, R, L), F32), pltpu.SemaphoreType.DMA((N_DEV - 1,)),
                        pltpu.SemaphoreType.DMA((N_DEV - 1,))],
        compiler_params=pltpu.CompilerParams(has_side_effects=True))(vec)


def _adamw(w, g, m, v, name):
    R, Cn = w.shape
    tr = _pick(R, (256, 176, 128, 64, 40, 32, 16, 8))

    def body(w_ref, g_ref, m_ref, v_ref, d_ref, nm_ref, nv_ref):
        gv = g_ref[...]
        nm = ADAM_B1 * m_ref[...] + (1.0 - ADAM_B1) * gv
        nv = ADAM_B2 * v_ref[...] + (1.0 - ADAM_B2) * (gv * gv)
        m_hat = nm / (1.0 - ADAM_B1 ** ADAM_STEP)
        v_hat = nv / (1.0 - ADAM_B2 ** ADAM_STEP)
        d_ref[...] = -ADAM_LR * (m_hat / (jnp.sqrt(v_hat) + ADAM_EPS) + ADAM_WD * w_ref[...])
        nm_ref[...] = nm
        nv_ref[...] = nv

    spec = pl.BlockSpec((tr, Cn), lambda i: (i, 0))
    return pl.pallas_call(
        body, name=name, grid=(R // tr,), in_specs=[spec] * 4, out_specs=[spec] * 3,
        out_shape=[jax.ShapeDtypeStruct((R, Cn), F32)] * 3, compiler_params=_params(("parallel",)))(w, g, m, v)


def _pack(arrays, rows):
    flat = jnp.concatenate([a.reshape(-1).astype(F32) for a in arrays])
    return jnp.pad(flat, (0, rows * 128 - flat.shape[0])).reshape(rows, 128)


def _unpack(packed, shapes):
    flat = packed.reshape(-1)
    out, off = [], 0
    for s in shapes:
        size = 1
        for d in s:
            size *= d
        out.append(flat[off:off + size].reshape(s))
        off += size
    return out


def _ffn_fwd(x, norm, w_up, w_down, tag):
    h = _rms_fwd(x, norm, f"{tag}_norm")
    up = _mm(h, w_up, 'nn', f"{tag}_up", BF16)
    act = _swiglu_fwd(up, f"{tag}_swiglu")
    out = _mm(act, w_down, 'nn', f"{tag}_down", F32, res=x, scale=0.5)
    return out, (h, up, act)


def _ffn_bwd(dout, x, norm, w_up, w_down, saved, tag):
    h, up, act = saved
    dact = _mm(dout, w_down, 'nt', f"{tag}_down_dx", BF16, scale=0.5)
    g_down = _mm(act, dout, 'tn', f"{tag}_down_dw", BF16, scale=0.5)
    dup = _swiglu_bwd(dact, up, f"{tag}_swiglu_bwd")
    dh = _mm(dup, w_up, 'nt', f"{tag}_up_dx", F32)
    g_up = _mm(h, dup, 'tn', f"{tag}_up_dw", BF16)
    dx, g_norm = _rms_bwd(x, norm, dh, dout, f"{tag}_norm_bwd")
    return dx, g_norm, g_up, g_down


def kernel(x, mem, ffn1_norm, ffn1_w_up, ffn1_w_down, mix_norm, mem_norm, w_in, b_gate, conv_dw_w, conv_dw_b, conv_ln_g, conv_ln_b, conv_w_pw, att_rel_bias, att_w_o, mem_w_kv, mem_w_o, w_out, ffn2_norm, ffn2_w_up, ffn2_w_down, final_norm, loss_target, m_ffn1_norm, m_ffn1_w_up, m_ffn1_w_down, m_mix_norm, m_mem_norm, m_w_in, m_b_gate, m_conv_dw_w, m_conv_dw_b, m_conv_ln_g, m_conv_ln_b, m_conv_w_pw, m_att_rel_bias, m_att_w_o, m_mem_w_kv, m_mem_w_o, m_w_out, m_ffn2_norm, m_ffn2_w_up, m_ffn2_w_down, m_final_norm, v_ffn1_norm, v_ffn1_w_up, v_ffn1_w_down, v_mix_norm, v_mem_norm, v_w_in, v_b_gate, v_conv_dw_w, v_conv_dw_b, v_conv_ln_g, v_conv_ln_b, v_conv_w_pw, v_att_rel_bias, v_att_w_o, v_mem_w_kv, v_mem_w_o, v_w_out, v_ffn2_norm, v_ffn2_w_up, v_ffn2_w_down, v_final_norm):
    given = dict(locals())
    wts = {n: given[n] for n in WEIGHTS}
    mom1 = {n: given["m_" + n] for n in WEIGHTS}
    mom2 = {n: given["v_" + n] for n in WEIGHTS}
    Bl, S, Dm = x.shape
    T = Bl * S
    x0 = x.reshape(T, Dm)
    tgt = loss_target.reshape(T, Dm)
    mem2 = mem.reshape(Bl * MEM_LEN, Dm)

    big_names = [n for n, _ in BIG]
    big_axes = [a for _, a in BIG]
    chip = 2 * lax.axis_index("x") + lax.axis_index("y")

    shards = [wts[n][0].astype(BF16) for n in big_names] + [jnp.pad(conv_dw_w[0], ((0, 1), (0, 0)))]
    gathered = _all_gather_weights(shards, big_axes + [1])
    full = dict(zip(big_names, gathered[:-1]))
    dw_full = gathered[-1][:CONV_KERNEL]

    final_g = final_norm.reshape(1, Dm)
    bias = _att_bias(att_rel_bias[0])

    x1, ffn1_saved = _ffn_fwd(x0, ffn1_norm, full['ffn1_w_up'], full['ffn1_w_down'], "ffn1")
    h = _rms_fwd(x1, mix_norm, "mix_norm")
    proj = _mm(h, full['w_in'], 'nn', "w_in", BF16)
    proj3 = proj.reshape(Bl, S, proj.shape[1])
    cv, c_act = _conv_fwd(proj3, dw_full, conv_dw_b, conv_ln_g, conv_ln_b)
    o_att = _att_fwd(proj3, bias)
    mem_h = _rms_fwd(mem2, mem_norm, "mem_norm")
    kv = _mm(mem_h, full['mem_w_kv'], 'nn', "mem_kv", BF16)
    kv3 = kv.reshape(Bl, MEM_LEN, 2 * MEM_WIDTH)
    o_mem = _mem_fwd(proj3, kv3)
    c_act2, o_att2, o_mem2 = c_act.reshape(T, -1), o_att.reshape(T, -1), o_mem.reshape(T, -1)
    x2, yc, ya, ym, y = _mix_fwd(c_act2, o_att2, o_mem2, proj, b_gate, x1, full['conv_w_pw'], full['att_w_o'],
                                 full['mem_w_o'], full['w_out'])
    x3, ffn2_saved = _ffn_fwd(x2, ffn2_norm, full['ffn2_w_up'], full['ffn2_w_down'], "ffn2")
    dx3, g_final, loss_vec = _final_fwd_bwd(x3, tgt, final_g)

    g = {}
    dx2, g['ffn2_norm'], g['ffn2_w_up'], g['ffn2_w_down'] = _ffn_bwd(
        dx3, x2, ffn2_norm, full['ffn2_w_up'], full['ffn2_w_down'], ffn2_saved, "ffn2")
    dyc, dya, dym, dgl, g['b_gate'], dc, doa, dom = _mix_bwd(
        dx2, yc, ya, ym, proj, b_gate, full['conv_w_pw'], full['att_w_o'], full['mem_w_o'], full['w_out'])
    g['w_out'] = _mm(y, dx2, 'tn', "w_out_dw", BF16)
    g['conv_w_pw'] = _mm(c_act2, dyc, 'tn', "conv_pw_dw", BF16)
    g['att_w_o'] = _mm(o_att2, dya, 'tn', "att_o_dw", BF16)
    g['mem_w_o'] = _mm(o_mem2, dym, 'tn', "mem_o_dw", BF16)
    du, g_dw, g['conv_dw_b'], g['conv_ln_g'], g['conv_ln_b'] = _conv_bwd(
        proj3, cv, dc.reshape(Bl, S, -1), dw_full, conv_ln_g, conv_ln_b)
    dq, dk, dv, dbias = _att_bwd(proj3, doa.reshape(Bl, S, -1), bias)
    g['att_rel_bias'] = _rel_bias_grad(dbias)
    dmq, dkv = _mem_bwd(proj3, kv3, dom.reshape(Bl, S, -1))
    dkv2 = dkv.reshape(Bl * MEM_LEN, 2 * MEM_WIDTH)
    g['mem_w_kv'] = _mm(mem_h, dkv2, 'tn', "mem_kv_dw", BF16)
    dmem_h = _mm(dkv2, full['mem_w_kv'], 'nt', "mem_kv_dx", F32)
    _, g['mem_norm'] = _rms_bwd(mem2, mem_norm, dmem_h, dmem_h, "mem_norm_bwd")
    dproj = jnp.concatenate([du.reshape(T, -1), dq.reshape(T, -1), dk.reshape(T, -1), dv.reshape(T, -1),
                             dmq.reshape(T, -1), dgl], axis=1)
    dh = _mm(dproj, full['w_in'], 'nt', "w_in_dx", F32)
    g['w_in'] = _mm(h, dproj, 'tn', "w_in_dw", BF16)
    dx1, g['mix_norm'] = _rms_bwd(x1, mix_norm, dh, dx2, "mix_norm_bwd")
    dx0, g['ffn1_norm'], g['ffn1_w_up'], g['ffn1_w_down'] = _ffn_bwd(
        dx1, x0, ffn1_norm, full['ffn1_w_up'], full['ffn1_w_down'], ffn1_saved, "ffn1")
    g['final_norm'] = g_final

    parts = _scatter_grads([g[n] for n in big_names], big_axes)
    halves = [_sum_partials(p, f"sum_{n}") for p, n in zip(parts, big_names)]
    for n, sg in zip(big_names, _exchange_halves(halves)):
        g[n] = sg

    small_shapes = [wts[n].shape for n in SMALL]
    n_small = sum(int(wts[n].size) for n in SMALL)
    n_red = n_small + CONV_KERNEL * CONV_WIDTH
    red = _all_reduce_small(_pack([g[n] for n in SMALL] + [g_dw], -(-n_red // 1024) * 8))
    red_list = _unpack(red, small_shapes + [(CONV_KERNEL, CONV_WIDTH)])
    for n, rg in zip(SMALL, red_list[:-1]):
        g[n] = rg
    dw_cols = conv_dw_w.shape[2]
    g['conv_dw_w'] = lax.dynamic_slice(red_list[-1], (0, chip * dw_cols), (CONV_KERNEL, dw_cols))[None]

    delta, new_m, new_v = {}, {}, {}
    for n in big_names:
        g[n] = g[n][None]
        d, nm, nv = _adamw(wts[n][0], g[n][0], mom1[n][0], mom2[n][0], f"adamw_{n}")
        delta[n], new_m[n], new_v[n] = d[None], nm[None], nv[None]
    rest = SMALL + ['conv_dw_w']
    rest_shapes = [wts[n].shape for n in rest]
    rows = -(-sum(int(wts[n].size) for n in rest) // 1024) * 8
    packed = [_pack([src[n] for n in rest], rows) for src in (wts, g, mom1, mom2)]
    for out, res in zip((delta, new_m, new_v), _adamw(*packed, "adamw_small")):
        for n, a in zip(rest, _unpack(res, rest_shapes)):
            out[n] = a

    loss = lax.psum(loss_vec[0, 0], ("x", "y", "c"))
    grad_x = dx0.reshape(Bl, S, Dm)
    return (loss, grad_x, *[g[n] for n in WEIGHTS], *[delta[n] for n in WEIGHTS],
            *[new_m[n] for n in WEIGHTS], *[new_v[n] for n in WEIGHTS])
```

```python
import jax
import jax.numpy as jnp
from jax import lax
from jax.experimental import pallas as pl
from jax.experimental.pallas import tpu as pltpu

F32 = jnp.float32
BF16 = jnp.bfloat16

D_MODEL = 1024
D_FF = 2816
CHUNK = 64
LEFT_CHUNKS = 8
MAX_REL = 128
N_REL = (CHUNK - 1) + MAX_REL + 1
CONV_WIDTH = 512
CONV_KERNEL = 31
ATT_HEADS = 8
ATT_WIDTH = 512
MEM_LEN = 256
MEM_HEADS = 4
MEM_HEAD_DIM = 128
MEM_WIDTH = 512
EPS = 1e-6
MASK_VALUE = -1e30
ATT_SCALE = 64 ** -0.5
MEM_SCALE = 128 ** -0.5

ADAM_LR = 0.001
ADAM_B1 = 0.9
ADAM_B2 = 0.999
ADAM_EPS = 1e-08
ADAM_WD = 0.01
ADAM_STEP = 10

QB = 256
KW = 3 * QB
CONV_PAD = 32
CONV_TILE = 256

VMEM_LIMIT = 56 << 20

WEIGHTS = ['ffn1_norm', 'ffn1_w_up', 'ffn1_w_down', 'mix_norm', 'mem_norm', 'w_in', 'b_gate', 'conv_dw_w',
           'conv_dw_b', 'conv_ln_g', 'conv_ln_b', 'conv_w_pw', 'att_rel_bias', 'att_w_o', 'mem_w_kv', 'mem_w_o',
           'w_out', 'ffn2_norm', 'ffn2_w_up', 'ffn2_w_down', 'final_norm']
BIG = [('ffn1_w_up', 1), ('ffn1_w_down', 0), ('w_in', 1), ('conv_w_pw', 1), ('att_w_o', 1), ('mem_w_kv', 0),
       ('mem_w_o', 1), ('w_out', 0), ('ffn2_w_up', 1), ('ffn2_w_down', 0)]
SMALL = ['ffn1_norm', 'mix_norm', 'mem_norm', 'b_gate', 'conv_dw_b', 'conv_ln_g', 'conv_ln_b', 'att_rel_bias',
         'ffn2_norm', 'final_norm']
N_CHIPS = 4
N_DEV = 8
MESH = pl.DeviceIdType.MESH


def _pick(n, cands):
    for c in cands:
        if n % c == 0:
            return c
    return n


def _sig(x):
    return 1.0 / (1.0 + jnp.exp(-x))


def _params(sem=None, vmem=VMEM_LIMIT):
    return pltpu.CompilerParams(dimension_semantics=sem, vmem_limit_bytes=vmem)


def _dot(a, b, mode='nn'):
    dims = {'nn': (((1,), (0,)), ((), ())), 'nt': (((1,), (1,)), ((), ())), 'tn': (((0,), (0,)), ((), ()))}[mode]
    return lax.dot_general(a.astype(BF16), b.astype(BF16), dims, preferred_element_type=F32)


def _mm(a, b, mode, name, out_dtype, res=None, scale=1.0, after=None):
    if mode == 'nn':
        (M, C), (_, N) = a.shape, b.shape
    elif mode == 'nt':
        (M, C), (N, _) = a.shape, b.shape
    else:
        (C, M), (_, N) = a.shape, b.shape
    tm = _pick(M, (1024, 1408, 512, 256, 128))
    tn = _pick(N, (1024, 1408, 512, 256, 128))
    tc = _pick(C, (1024, 1408, 512, 256, 128))
    nk = C // tc
    if mode == 'nn':
        a_spec = pl.BlockSpec((tm, tc), lambda i, j, k: (i, k))
        b_spec = pl.BlockSpec((tc, tn), lambda i, j, k: (k, j))
    elif mode == 'nt':
        a_spec = pl.BlockSpec((tm, tc), lambda i, j, k: (i, k))
        b_spec = pl.BlockSpec((tn, tc), lambda i, j, k: (j, k))
    else:
        a_spec = pl.BlockSpec((tc, tm), lambda i, j, k: (k, i))
        b_spec = pl.BlockSpec((tc, tn), lambda i, j, k: (k, j))
    o_spec = pl.BlockSpec((tm, tn), lambda i, j, k: (i, j))
    has_res = res is not None
    has_after = after is not None

    def body(*refs):
        a_ref, b_ref = refs[:2]
        r_ref = refs[2] if has_res else None
        o_ref, acc_ref = refs[-2:]
        k = pl.program_id(2)
        p = _dot(a_ref[...], b_ref[...], mode)

        def finish(acc):
            if scale != 1.0:
                acc = acc * scale
            if r_ref is not None:
                acc = r_ref[...] + acc
            o_ref[...] = acc.astype(o_ref.dtype)

        if nk == 1:
            finish(p)
        else:
            @pl.when(k == 0)
            def _():
                acc_ref[...] = p

            @pl.when(k > 0)
            def _():
                acc_ref[...] += p

            @pl.when(k == nk - 1)
            def _():
                finish(acc_ref[...])

    in_specs = [a_spec, b_spec] + ([o_spec] if has_res else []) + ([ANY] if has_after else [])
    args = (a, b) + ((res,) if has_res else ()) + ((after,) if has_after else ())
    acc_shape = (tm, tn) if nk > 1 else (8, 128)
    return pl.pallas_call(
        body, name=name, grid=(M // tm, N // tn, nk), in_specs=in_specs, out_specs=o_spec,
        out_shape=jax.ShapeDtypeStruct((M, N), out_dtype), scratch_shapes=[pltpu.VMEM(acc_shape, F32)],
        compiler_params=_params(("parallel", "parallel", "arbitrary")))(*args)


def _row_tile(T):
    return _pick(T, (512, 256, 128, 64, 32, 16, 8))


def _rms_fwd(x, g, name, after=None):
    T, Dm = x.shape
    tm = _row_tile(T)

    def body(x_ref, g_ref, *rest):
        o_ref = rest[-1]
        xv = x_ref[...]
        r = lax.rsqrt(jnp.mean(xv * xv, axis=-1, keepdims=True) + EPS)
        o_ref[...] = ((xv * r) * g_ref[...]).astype(o_ref.dtype)

    extra = () if after is None else (after,)
    return pl.pallas_call(
        body, name=name, grid=(T // tm,),
        in_specs=[pl.BlockSpec((tm, Dm), lambda i: (i, 0)), pl.BlockSpec((1, Dm), lambda i: (0, 0))]
        + [ANY] * len(extra),
        out_specs=pl.BlockSpec((tm, Dm), lambda i: (i, 0)), out_shape=jax.ShapeDtypeStruct((T, Dm), BF16),
        compiler_params=_params(("parallel",)))(x, g, *extra)


def _rms_bwd(x, g, dh, dres, name):
    T, Dm = x.shape
    tm = _row_tile(T)

    def body(x_ref, g_ref, dh_ref, dr_ref, dx_ref, dg_ref):
        i = pl.program_id(0)
        xv = x_ref[...]
        r = lax.rsqrt(jnp.mean(xv * xv, axis=-1, keepdims=True) + EPS)
        xr = xv * r
        dh_v = dh_ref[...].astype(F32)
        dyg = dh_v * g_ref[...]
        dx = r * (dyg - xr * jnp.mean(dyg * xr, axis=-1, keepdims=True))
        dx_ref[...] = dr_ref[...] + dx

        @pl.when(i == 0)
        def _():
            dg_ref[...] = jnp.zeros_like(dg_ref)

        dg_ref[...] += jnp.sum(dh_v * xr, axis=0, keepdims=True)

    row = pl.BlockSpec((tm, Dm), lambda i: (i, 0))
    vec = pl.BlockSpec((1, Dm), lambda i: (0, 0))
    return pl.pallas_call(
        body, name=name, grid=(T // tm,), in_specs=[row, vec, row, row], out_specs=[row, vec],
        out_shape=[jax.ShapeDtypeStruct((T, Dm), F32), jax.ShapeDtypeStruct((1, Dm), F32)],
        compiler_params=_params(("arbitrary",)))(x, g, dh, dres)


def _final_fwd_bwd(x3, tgt, g):
    T, Dm = x3.shape
    tm = _row_tile(T)

    def body(x_ref, t_ref, g_ref, dx_ref, dg_ref, loss_ref):
        i = pl.program_id(0)
        xv = x_ref[...]
        gg = g_ref[...]
        r = lax.rsqrt(jnp.mean(xv * xv, axis=-1, keepdims=True) + EPS)
        xr = xv * r
        err = xr * gg - t_ref[...]
        dout = err * (1.0 / Dm)
        dyg = dout * gg
        dx_ref[...] = r * (dyg - xr * jnp.mean(dyg * xr, axis=-1, keepdims=True))

        @pl.when(i == 0)
        def _():
            dg_ref[...] = jnp.zeros_like(dg_ref)
            loss_ref[...] = jnp.zeros_like(loss_ref)

        dg_ref[...] += jnp.sum(dout * xr, axis=0, keepdims=True)
        loss_ref[...] += jnp.zeros_like(loss_ref) + (0.5 / Dm) * jnp.sum(err * err)

    row = pl.BlockSpec((tm, Dm), lambda i: (i, 0))
    vec = pl.BlockSpec((1, Dm), lambda i: (0, 0))
    one = pl.BlockSpec((1, 128), lambda i: (0, 0))
    return pl.pallas_call(
        body, name="final_fwd_bwd", grid=(T // tm,), in_specs=[row, row, vec], out_specs=[row, vec, one],
        out_shape=[jax.ShapeDtypeStruct((T, Dm), F32), jax.ShapeDtypeStruct((1, Dm), F32),
                   jax.ShapeDtypeStruct((1, 128), F32)],
        compiler_params=_params(("arbitrary",)))(x3, tgt, g)


def _swiglu_fwd(up, name):
    T, F2 = up.shape
    Fh = F2 // 2
    tm = _pick(T, (256, 128, 64, 32, 16, 8))

    def body(up_ref, o_ref):
        a = up_ref[:, :Fh].astype(F32)
        b = up_ref[:, Fh:].astype(F32)
        o_ref[...] = (a * _sig(a) * b).astype(o_ref.dtype)

    return pl.pallas_call(
        body, name=name, grid=(T // tm,), in_specs=[pl.BlockSpec((tm, F2), lambda i: (i, 0))],
        out_specs=pl.BlockSpec((tm, Fh), lambda i: (i, 0)), out_shape=jax.ShapeDtypeStruct((T, Fh), BF16),
        compiler_params=_params(("parallel",)))(up)


def _swiglu_bwd(dact, up, name):
    T, F2 = up.shape
    Fh = F2 // 2
    tm = _pick(T, (256, 128, 64, 32, 16, 8))

    def body(d_ref, up_ref, o_ref):
        a = up_ref[:, :Fh].astype(F32)
        b = up_ref[:, Fh:].astype(F32)
        d = d_ref[...].astype(F32)
        s = _sig(a)
        o_ref[:, :Fh] = (d * b * s * (1.0 + a * (1.0 - s))).astype(o_ref.dtype)
        o_ref[:, Fh:] = (d * a * s).astype(o_ref.dtype)

    return pl.pallas_call(
        body, name=name, grid=(T // tm,),
        in_specs=[pl.BlockSpec((tm, Fh), lambda i: (i, 0)), pl.BlockSpec((tm, F2), lambda i: (i, 0))],
        out_specs=pl.BlockSpec((tm, F2), lambda i: (i, 0)), out_shape=jax.ShapeDtypeStruct((T, F2), BF16),
        compiler_params=_params(("parallel",)))(dact, up)


def _mix_fwd(c_act, o_att, o_mem, proj, b_gate, x1, w_pw, w_o, w_mo, w_out):
    T, Dm = x1.shape
    W = c_act.shape[1]
    tm = _pick(T, (256, 128, 64, 32, 16, 8))

    def body(c_ref, oa_ref, om_ref, gl_ref, bg_ref, x1_ref, wpw_ref, wo_ref, wmo_ref, wout_ref,
             x2_ref, yc_ref, ya_ref, ym_ref, y_ref):
        yc = _dot(c_ref[...], wpw_ref[...])
        ya = _dot(oa_ref[...], wo_ref[...])
        ym = _dot(om_ref[...], wmo_ref[...])
        g = _sig(gl_ref[...].astype(F32) + bg_ref[...])
        y = g[:, :Dm] * yc + g[:, Dm:2 * Dm] * ya + g[:, 2 * Dm:] * ym
        x2_ref[...] = x1_ref[...] + _dot(y, wout_ref[...])
        yc_ref[...] = yc.astype(BF16)
        ya_ref[...] = ya.astype(BF16)
        ym_ref[...] = ym.astype(BF16)
        y_ref[...] = y.astype(BF16)

    rowW = pl.BlockSpec((tm, W), lambda i: (i, 0))
    rowD = pl.BlockSpec((tm, Dm), lambda i: (i, 0))
    full = lambda s: pl.BlockSpec(s, lambda i: (0, 0))
    return pl.pallas_call(
        body, name="mix_fwd", grid=(T // tm,),
        in_specs=[rowW, rowW, rowW, pl.BlockSpec((tm, 3 * Dm), lambda i: (i, 1)), full((1, 3 * Dm)), rowD,
                  full((W, Dm)), full((W, Dm)), full((W, Dm)), full((Dm, Dm))],
        out_specs=[rowD] * 5,
        out_shape=[jax.ShapeDtypeStruct((T, Dm), F32)] + [jax.ShapeDtypeStruct((T, Dm), BF16)] * 4,
        compiler_params=_params(("parallel",)))(c_act, o_att, o_mem, proj, b_gate, x1, w_pw, w_o, w_mo, w_out)


def _mix_bwd(dx2, yc, ya, ym, proj, b_gate, w_pw, w_o, w_mo, w_out):
    T, Dm = dx2.shape
    W = w_pw.shape[0]
    tm = _pick(T, (256, 128, 64, 32, 16, 8))

    def body(dx_ref, yc_ref, ya_ref, ym_ref, gl_ref, bg_ref, wpw_ref, wo_ref, wmo_ref, wout_ref,
             dyc_ref, dya_ref, dym_ref, dgl_ref, dbg_ref, dc_ref, doa_ref, dom_ref):
        i = pl.program_id(0)

        @pl.when(i == 0)
        def _():
            dbg_ref[...] = jnp.zeros_like(dbg_ref)

        dy = _dot(dx_ref[...], wout_ref[...], 'nt')
        g = _sig(gl_ref[...].astype(F32) + bg_ref[...])
        branches = ((yc_ref, dyc_ref, wpw_ref, dc_ref), (ya_ref, dya_ref, wo_ref, doa_ref),
                    (ym_ref, dym_ref, wmo_ref, dom_ref))
        for n, (y_ref, dyk_ref, w_ref, dk_ref) in enumerate(branches):
            gk = g[:, n * Dm:(n + 1) * Dm]
            dyk = dy * gk
            dgl = dyk * y_ref[...].astype(F32) * (1.0 - gk)
            dyk_ref[...] = dyk.astype(BF16)
            dgl_ref[:, n * Dm:(n + 1) * Dm] = dgl.astype(BF16)
            dbg_ref[:, n * Dm:(n + 1) * Dm] += jnp.sum(dgl, axis=0, keepdims=True)
            dk_ref[...] = _dot(dyk, w_ref[...], 'nt').astype(BF16)

    rowW = pl.BlockSpec((tm, W), lambda i: (i, 0))
    rowD = pl.BlockSpec((tm, Dm), lambda i: (i, 0))
    row3 = pl.BlockSpec((tm, 3 * Dm), lambda i: (i, 0))
    full = lambda s: pl.BlockSpec(s, lambda i: (0, 0))
    return pl.pallas_call(
        body, name="mix_bwd", grid=(T // tm,),
        in_specs=[rowD, rowD, rowD, rowD, pl.BlockSpec((tm, 3 * Dm), lambda i: (i, 1)), full((1, 3 * Dm)),
                  full((W, Dm)), full((W, Dm)), full((W, Dm)), full((Dm, Dm))],
        out_specs=[rowD, rowD, rowD, row3, full((1, 3 * Dm)), rowW, rowW, rowW],
        out_shape=[jax.ShapeDtypeStruct((T, Dm), BF16)] * 3 + [jax.ShapeDtypeStruct((T, 3 * Dm), BF16),
                                                                jax.ShapeDtypeStruct((1, 3 * Dm), F32)]
        + [jax.ShapeDtypeStruct((T, W), BF16)] * 3,
        compiler_params=_params(("arbitrary",)))(dx2, yc, ya, ym, proj, b_gate, w_pw, w_o, w_mo, w_out)


def _ln_swish(cv, lg, lb):
    mu = jnp.mean(cv, axis=-1, keepdims=True)
    xc = cv - mu
    r = lax.rsqrt(jnp.mean(xc * xc, axis=-1, keepdims=True) + EPS)
    n = xc * r
    l = n * lg + lb
    return r, n, l


def _conv_fwd(proj3, dw_w, dw_b, ln_g, ln_b):
    Bl, S, _ = proj3.shape
    C, K, TS, PAD = CONV_WIDTH, CONV_KERNEL, CONV_TILE, CONV_PAD
    nt = S // TS

    def body(u_ref, w_ref, b_ref, lg_ref, lb_ref, cv_ref, c_ref, vbuf, win):
        vbuf[0:PAD, :] = jnp.zeros((PAD, C), F32)

        def glu(t, carry):
            r0 = pl.multiple_of(t * TS, TS)
            u = u_ref[pl.ds(r0, TS), :].astype(F32)
            vbuf[pl.ds(PAD + r0, TS), :] = u[:, :C] * _sig(u[:, C:])
            return carry

        lax.fori_loop(0, nt, glu, 0)

        def conv(t, carry):
            r0 = pl.multiple_of(t * TS, TS)
            win[...] = vbuf[pl.ds(r0, TS + PAD), :]
            acc = jnp.zeros((TS, C), F32)
            for j in range(K):
                acc = acc + w_ref[j:j + 1, :] * win[PAD - (K - 1) + j:PAD - (K - 1) + j + TS, :]
            cv = acc + b_ref[...]
            cv_ref[pl.ds(r0, TS), :] = cv
            _, _, l = _ln_swish(cv, lg_ref[...], lb_ref[...])
            c_ref[pl.ds(r0, TS), :] = (l * _sig(l)).astype(BF16)
            return carry

        lax.fori_loop(0, nt, conv, 0)

    vec = pl.BlockSpec((1, C), lambda b: (0, 0))
    return pl.pallas_call(
        body, name="conv_fwd", grid=(Bl,),
        in_specs=[pl.BlockSpec((None, S, 2 * C), lambda b: (b, 0, 0)), pl.BlockSpec((K, C), lambda b: (0, 0)),
                  vec, vec, vec],
        out_specs=[pl.BlockSpec((None, S, C), lambda b: (b, 0, 0))] * 2,
        out_shape=[jax.ShapeDtypeStruct((Bl, S, C), F32), jax.ShapeDtypeStruct((Bl, S, C), BF16)],
        scratch_shapes=[pltpu.VMEM((S + PAD, C), F32), pltpu.VMEM((TS + PAD, C), F32)],
        compiler_params=_params(("parallel",)))(proj3, dw_w, dw_b, ln_g, ln_b)


def _conv_bwd(proj3, cv, dc, dw_w, ln_g, ln_b):
    Bl, S, _ = proj3.shape
    C, K, TS, PAD = CONV_WIDTH, CONV_KERNEL, CONV_TILE, CONV_PAD
    nt = S // TS

    def body(u_ref, cv_ref, dc_ref, w_ref, lg_ref, lb_ref, du_ref, dw_ref, db_ref, dlg_ref, dlb_ref,
             vbuf, gbuf, win, dwacc):
        b = pl.program_id(0)

        @pl.when(b == 0)
        def _():
            dw_ref[...] = jnp.zeros_like(dw_ref)
            db_ref[...] = jnp.zeros_like(db_ref)
            dlg_ref[...] = jnp.zeros_like(dlg_ref)
            dlb_ref[...] = jnp.zeros_like(dlb_ref)

        vbuf[0:PAD, :] = jnp.zeros((PAD, C), F32)
        gbuf[S:S + PAD, :] = jnp.zeros((PAD, C), F32)
        dwacc[...] = jnp.zeros_like(dwacc)

        def norm_bwd(t, carry):
            r0 = pl.multiple_of(t * TS, TS)
            u = u_ref[pl.ds(r0, TS), :].astype(F32)
            vbuf[pl.ds(PAD + r0, TS), :] = u[:, :C] * _sig(u[:, C:])
            r, n, l = _ln_swish(cv_ref[pl.ds(r0, TS), :], lg_ref[...], lb_ref[...])
            s = _sig(l)
            dl = dc_ref[pl.ds(r0, TS), :].astype(F32) * s * (1.0 + l * (1.0 - s))
            dlg_ref[...] += jnp.sum(dl * n, axis=0, keepdims=True)
            dlb_ref[...] += jnp.sum(dl, axis=0, keepdims=True)
            dn = dl * lg_ref[...]
            dcv = r * (dn - jnp.mean(dn, axis=-1, keepdims=True) - n * jnp.mean(dn * n, axis=-1, keepdims=True))
            gbuf[pl.ds(r0, TS), :] = dcv
            db_ref[...] += jnp.sum(dcv, axis=0, keepdims=True)
            return carry

        lax.fori_loop(0, nt, norm_bwd, 0)

        def conv_bwd(t, carry):
            r0 = pl.multiple_of(t * TS, TS)
            win[...] = gbuf[pl.ds(r0, TS + PAD), :]
            dv = jnp.zeros((TS, C), F32)
            for j in range(K):
                dv = dv + w_ref[j:j + 1, :] * win[K - 1 - j:K - 1 - j + TS, :]
            u = u_ref[pl.ds(r0, TS), :].astype(F32)
            a, g = u[:, :C], u[:, C:]
            s = _sig(g)
            du_ref[pl.ds(r0, TS), 0:C] = (dv * s).astype(BF16)
            du_ref[pl.ds(r0, TS), C:2 * C] = (dv * a * s * (1.0 - s)).astype(BF16)
            dcv = gbuf[pl.ds(r0, TS), :]
            win[...] = vbuf[pl.ds(r0, TS + PAD), :]
            for j in range(K):
                prod = dcv * win[PAD - (K - 1) + j:PAD - (K - 1) + j + TS, :]
                dwacc[j] += jnp.sum(prod.reshape(TS // 8, 8, C), axis=0)
            return carry

        lax.fori_loop(0, nt, conv_bwd, 0)
        dw_ref[...] += jnp.sum(dwacc[...], axis=1)

    vec = pl.BlockSpec((1, C), lambda b: (0, 0))
    seq = lambda w: pl.BlockSpec((None, S, w), lambda b: (b, 0, 0))
    return pl.pallas_call(
        body, name="conv_bwd", grid=(Bl,),
        in_specs=[seq(2 * C), seq(C), seq(C), pl.BlockSpec((K, C), lambda b: (0, 0)), vec, vec],
        out_specs=[seq(2 * C), pl.BlockSpec((K, C), lambda b: (0, 0)), vec, vec, vec],
        out_shape=[jax.ShapeDtypeStruct((Bl, S, 2 * C), BF16), jax.ShapeDtypeStruct((K, C), F32)]
        + [jax.ShapeDtypeStruct((1, C), F32)] * 3,
        scratch_shapes=[pltpu.VMEM((S + PAD, C), F32), pltpu.VMEM((S + PAD, C), F32),
                        pltpu.VMEM((TS + PAD, C), F32), pltpu.VMEM((K, 8, C), F32)],
        compiler_params=_params(("arbitrary",)))(proj3, cv, dc, dw_w, ln_g, ln_b)


def _att_bias(rel_bias):
    H = rel_bias.shape[0]
    Wd = KW + QB
    c = jnp.arange(Wd + 1)
    by_offset = rel_bias[:, jnp.clip(KW - c, -(CHUNK - 1), MAX_REL) + (CHUNK - 1)]
    flat = jnp.broadcast_to(by_offset[:, None, :], (H, QB, Wd + 1)).reshape(H, QB * (Wd + 1))
    skew = jnp.pad(flat, ((0, 0), (0, (QB + 1) * Wd - QB * (Wd + 1)))).reshape(H, QB + 1, Wd)[:, :QB, QB:]
    qi = jnp.arange(QB)[:, None]
    kj = jnp.arange(KW)[None, :]
    dchunk = ((KW - QB) + qi) // CHUNK - kj // CHUNK
    band = (dchunk >= 0) & (dchunk <= LEFT_CHUNKS)
    return jnp.where(band[None], skew, MASK_VALUE)


def _head_masks():
    lane = lax.broadcasted_iota(jnp.int32, (1, 128), 1)
    return (lane < 64, lane >= 64)


def _att_probs(qh, k2, bias, valid):
    s = _dot(qh, k2, 'nt') * ATT_SCALE + bias
    s = jnp.where(valid, s, MASK_VALUE)
    e = jnp.exp(s - jnp.max(s, axis=-1, keepdims=True))
    return e * (1.0 / jnp.sum(e, axis=-1, keepdims=True))


def _att_specs(S, q_col):
    nb = S // QB
    q_spec = pl.BlockSpec((None, QB, ATT_WIDTH), lambda b, i: (b, jnp.minimum(i, nb - 1), q_col))

    def kv_spec(col, kb):
        return pl.BlockSpec((None, QB, ATT_WIDTH),
                            lambda b, i: (b, jnp.clip(i - 2 + kb, 0, nb - 1), col))

    return q_spec, [kv_spec(3, kb) for kb in range(3)], [kv_spec(4, kb) for kb in range(3)]


def _att_fwd(proj3, bias):
    Bl, S, _ = proj3.shape
    nb = S // QB
    q_spec, k_specs, v_specs = _att_specs(S, 2)

    def body(q_ref, k0, k1, k2r, v0, v1, v2r, bias_ref, o_ref):
        i = pl.program_id(1)
        masks = _head_masks()
        valid = lax.broadcasted_iota(jnp.int32, (QB, KW), 1) >= (2 - i) * QB
        for pr in range(ATT_HEADS // 2):
            ls = slice(128 * pr, 128 * (pr + 1))
            q2 = q_ref[:, ls]
            k2 = jnp.concatenate([k0[:, ls], k1[:, ls], k2r[:, ls]], axis=0)
            v2 = jnp.concatenate([v0[:, ls], v1[:, ls], v2r[:, ls]], axis=0)
            o2 = jnp.zeros((QB, 128), F32)
            for hh in range(2):
                p = _att_probs(jnp.where(masks[hh], q2, 0), k2, bias_ref[2 * pr + hh], valid)
                o2 = o2 + _dot(p, jnp.where(masks[hh], v2, 0))
            o_ref[:, ls] = o2.astype(BF16)

    return pl.pallas_call(
        body, name="att_fwd", grid=(Bl, nb),
        in_specs=[q_spec] + k_specs + v_specs + [pl.BlockSpec((ATT_HEADS, QB, KW), lambda b, i: (0, 0, 0))],
        out_specs=pl.BlockSpec((None, QB, ATT_WIDTH), lambda b, i: (b, i, 0)),
        out_shape=jax.ShapeDtypeStruct((Bl, S, ATT_WIDTH), BF16),
        compiler_params=_params(("parallel", "arbitrary")))(*([proj3] * 7), bias)


def _att_bwd(proj3, do, bias):
    Bl, S, _ = proj3.shape
    nb = S // QB
    q_spec, k_specs, v_specs = _att_specs(S, 2)
    do_spec = pl.BlockSpec((None, QB, ATT_WIDTH), lambda b, i: (b, jnp.minimum(i, nb - 1), 0))
    kv_out = pl.BlockSpec((None, QB, ATT_WIDTH), lambda b, i: (b, jnp.clip(i - 2, 0, nb - 1), 0))
    bias_spec = pl.BlockSpec((ATT_HEADS, QB, KW), lambda b, i: (0, 0, 0))

    def body(q_ref, k0, k1, k2r, v0, v1, v2r, do_ref, bias_ref, dq_ref, dk_ref, dv_ref, db_ref, dkw, dvw):
        b = pl.program_id(0)
        i = pl.program_id(1)

        @pl.when((b == 0) & (i == 0))
        def _():
            db_ref[...] = jnp.zeros_like(db_ref)

        @pl.when(i == 0)
        def _():
            dkw[...] = jnp.zeros_like(dkw)
            dvw[...] = jnp.zeros_like(dvw)

        @pl.when(i < nb)
        def _():
            masks = _head_masks()
            valid = lax.broadcasted_iota(jnp.int32, (QB, KW), 1) >= (2 - i) * QB
            for pr in range(ATT_HEADS // 2):
                ls = slice(128 * pr, 128 * (pr + 1))
                q2 = q_ref[:, ls]
                do2 = do_ref[:, ls]
                k2 = jnp.concatenate([k0[:, ls], k1[:, ls], k2r[:, ls]], axis=0)
                v2 = jnp.concatenate([v0[:, ls], v1[:, ls], v2r[:, ls]], axis=0)
                dq2 = jnp.zeros((QB, 128), F32)
                dk2 = jnp.zeros((KW, 128), F32)
                dv2 = jnp.zeros((KW, 128), F32)
                for hh in range(2):
                    h = 2 * pr + hh
                    qh = jnp.where(masks[hh], q2, 0)
                    doh = jnp.where(masks[hh], do2, 0)
                    p = _att_probs(qh, k2, bias_ref[h], valid)
                    dp = _dot(doh, v2, 'nt')
                    ds = p * (dp - jnp.sum(p * dp, axis=-1, keepdims=True))
                    db_ref[h] += ds
                    dq2 = dq2 + _dot(ds, jnp.where(masks[hh], k2, 0))
                    dk2 = dk2 + _dot(ds, qh, 'tn')
                    dv2 = dv2 + _dot(p, doh, 'tn')
                dq_ref[:, ls] = (dq2 * ATT_SCALE).astype(BF16)
                dkw[:, ls] += dk2 * ATT_SCALE
                dvw[:, ls] += dv2

        dk_ref[...] = dkw[0:QB, :].astype(BF16)
        dv_ref[...] = dvw[0:QB, :].astype(BF16)
        for buf in (dkw, dvw):
            rest = buf[QB:KW, :]
            buf[0:KW - QB, :] = rest
            buf[KW - QB:KW, :] = jnp.zeros((QB, ATT_WIDTH), F32)

    blk = jax.ShapeDtypeStruct((Bl, S, ATT_WIDTH), BF16)
    return pl.pallas_call(
        body, name="att_bwd", grid=(Bl, nb + 2),
        in_specs=[q_spec] + k_specs + v_specs + [do_spec, bias_spec],
        out_specs=[do_spec, kv_out, kv_out, bias_spec],
        out_shape=[blk, blk, blk, jax.ShapeDtypeStruct((ATT_HEADS, QB, KW), F32)],
        scratch_shapes=[pltpu.VMEM((KW, ATT_WIDTH), F32), pltpu.VMEM((KW, ATT_WIDTH), F32)],
        compiler_params=_params(("arbitrary", "arbitrary")))(*([proj3] * 7), do, bias)


def _rel_bias_grad(dbias):
    H = dbias.shape[0]
    Wd = KW + QB
    padded = jnp.pad(dbias, ((0, 0), (0, 1), (QB, 0)))
    skew = padded.reshape(H, (QB + 1) * Wd)[:, :QB * (Wd + 1)].reshape(H, QB, Wd + 1)[:, :, :Wd]
    c = jnp.arange(Wd)[:, None]
    bins = (jnp.clip(KW - c, -(CHUNK - 1), MAX_REL) + (CHUNK - 1) == jnp.arange(N_REL)[None, :]).astype(F32)

    def body(s_ref, bins_ref, o_ref):
        col = jnp.sum(s_ref[...], axis=1)
        o_ref[...] = jnp.dot(col, bins_ref[...], preferred_element_type=F32, precision=lax.Precision.HIGHEST)

    return pl.pallas_call(
        body, name="rel_bias_grad", grid=(1,),
        in_specs=[pl.BlockSpec((H, QB, Wd), lambda i: (0, 0, 0)), pl.BlockSpec((Wd, N_REL), lambda i: (0, 0))],
        out_specs=pl.BlockSpec((H, N_REL), lambda i: (0, 0)), out_shape=jax.ShapeDtypeStruct((H, N_REL), F32),
        compiler_params=_params(("arbitrary",)))(skew, bins)


MEM_TILE = 512


def _mem_probs(qh, kh):
    s = _dot(qh, kh, 'nt') * MEM_SCALE
    e = jnp.exp(s - jnp.max(s, axis=-1, keepdims=True))
    return e * (1.0 / jnp.sum(e, axis=-1, keepdims=True))


def _mem_fwd(proj3, kv3):
    Bl, S, _ = proj3.shape
    tq = _pick(S, (MEM_TILE, 256))
    hd = MEM_HEAD_DIM

    def body(q_ref, kv_ref, o_ref):
        for h in range(MEM_HEADS):
            p = _mem_probs(q_ref[:, h * hd:(h + 1) * hd], kv_ref[:, h * hd:(h + 1) * hd])
            o_ref[:, h * hd:(h + 1) * hd] = _dot(p, kv_ref[:, MEM_WIDTH + h * hd:MEM_WIDTH + (h + 1) * hd]).astype(BF16)

    return pl.pallas_call(
        body, name="mem_fwd", grid=(Bl, S // tq),
        in_specs=[pl.BlockSpec((None, tq, MEM_WIDTH), lambda b, i: (b, i, 5)),
                  pl.BlockSpec((None, MEM_LEN, 2 * MEM_WIDTH), lambda b, i: (b, 0, 0))],
        out_specs=pl.BlockSpec((None, tq, MEM_WIDTH), lambda b, i: (b, i, 0)),
        out_shape=jax.ShapeDtypeStruct((Bl, S, MEM_WIDTH), BF16),
        compiler_params=_params(("parallel", "parallel")))(proj3, kv3)


def _mem_bwd(proj3, kv3, do):
    Bl, S, _ = proj3.shape
    tq = _pick(S, (MEM_TILE, 256))
    hd = MEM_HEAD_DIM

    def body(q_ref, kv_ref, do_ref, dq_ref, dkv_ref):
        i = pl.program_id(1)

        @pl.when(i == 0)
        def _():
            dkv_ref[...] = jnp.zeros_like(dkv_ref)

        for h in range(MEM_HEADS):
            ks = slice(h * hd, (h + 1) * hd)
            vs = slice(MEM_WIDTH + h * hd, MEM_WIDTH + (h + 1) * hd)
            qh, kh, vh, doh = q_ref[:, ks], kv_ref[:, ks], kv_ref[:, vs], do_ref[:, ks]
            p = _mem_probs(qh, kh)
            dp = _dot(doh, vh, 'nt')
            ds = p * (dp - jnp.sum(p * dp, axis=-1, keepdims=True))
            dq_ref[:, ks] = (_dot(ds, kh) * MEM_SCALE).astype(BF16)
            dkv_ref[:, ks] += _dot(ds, qh, 'tn') * MEM_SCALE
            dkv_ref[:, vs] += _dot(p, doh, 'tn')

    return pl.pallas_call(
        body, name="mem_bwd", grid=(Bl, S // tq),
        in_specs=[pl.BlockSpec((None, tq, MEM_WIDTH), lambda b, i: (b, i, 5)),
                  pl.BlockSpec((None, MEM_LEN, 2 * MEM_WIDTH), lambda b, i: (b, 0, 0)),
                  pl.BlockSpec((None, tq, MEM_WIDTH), lambda b, i: (b, i, 0))],
        out_specs=[pl.BlockSpec((None, tq, MEM_WIDTH), lambda b, i: (b, i, 0)),
                   pl.BlockSpec((None, MEM_LEN, 2 * MEM_WIDTH), lambda b, i: (b, 0, 0))],
        out_shape=[jax.ShapeDtypeStruct((Bl, S, MEM_WIDTH), BF16),
                   jax.ShapeDtypeStruct((Bl, MEM_LEN, 2 * MEM_WIDTH), F32)],
        compiler_params=_params(("parallel", "arbitrary")))(proj3, kv3, do)


def _position():
    x, y, c = lax.axis_index("x"), lax.axis_index("y"), lax.axis_index("c")
    return x, y, c, 4 * x + 2 * y + c


def _device(idx):
    return ((idx >> 2) & 1, (idx >> 1) & 1, idx & 1)


def _half_block(ref, axis, shard_shape, k, h):
    R, Cn = shard_shape
    if axis == 1:
        return ref.at[pl.ds(h * (R // 2), R // 2), pl.ds(k * Cn, Cn)]
    return ref.at[pl.ds(k * R + h * (R // 2), R // 2), :]


def _block(ref, axis, shard_shape, k):
    R, Cn = shard_shape
    if axis == 1:
        return ref.at[:, pl.ds(k * Cn, Cn)]
    return ref.at[pl.ds(k * R, R), :]


def _half(ref, h):
    R = ref.shape[0]
    return ref.at[pl.ds(h * (R // 2), R // 2), :]


ANY = pl.BlockSpec(memory_space=pl.ANY)


HBM = pl.BlockSpec(memory_space=pltpu.HBM)
SEM = pl.BlockSpec(memory_space=pltpu.SEMAPHORE)
VMEM_WHOLE = pl.BlockSpec(memory_space=pltpu.VMEM)
EFFECT = pltpu.SideEffectType.DATAFLOW_SIDE_EFFECTING


def _in_hbm(a):
    return pltpu.with_memory_space_constraint(a, pltpu.HBM)


def _split_start(body, name, sources, lands, n_copies):
    n = len(sources)
    out_shape, out_specs = [], []
    for _ in range(n):
        out_shape += [pltpu.SemaphoreType.DMA((n_copies,)), pltpu.SemaphoreType.DMA((n_copies,))]
        out_specs += [SEM, SEM]
    out_shape += [pltpu.HBM(a.shape, a.dtype) for a in list(sources) + list(lands)]
    out_specs += [HBM] * (2 * n)
    out_shape.append(jax.ShapeDtypeStruct((8, 128), F32))
    out_specs.append(VMEM_WHOLE)

    def call_body(*refs):
        srcs, lnds = refs[:n], refs[n:2 * n]
        sems = refs[2 * n:4 * n]
        token = refs[-1]
        body(srcs, lnds, sems[0::2], sems[1::2])
        token[...] = jnp.zeros_like(token)

    res = pl.pallas_call(
        call_body, name=name, in_specs=[HBM] * (2 * n), out_specs=out_specs, out_shape=out_shape,
        input_output_aliases={i: 2 * n + i for i in range(2 * n)},
        compiler_params=pltpu.CompilerParams(has_side_effects=EFFECT))(
            *[_in_hbm(a) for a in list(sources) + list(lands)])
    pairs = [(res[2 * w], res[2 * w + 1], res[2 * n + w], res[3 * n + w]) for w in range(n)]
    return pairs, res[-1]


def _split_wait(body, name, pairs, after):
    n = len(pairs)

    def call_body(*refs):
        srcs, lnds = refs[:n], refs[n:2 * n]
        sems = refs[2 * n:4 * n]
        body(srcs, lnds, sems[0::2], sems[1::2])

    args = [_in_hbm(p[2]) for p in pairs] + [_in_hbm(p[3]) for p in pairs]
    for p in pairs:
        args += [p[0], p[1]]
    res = pl.pallas_call(
        call_body, name=name, in_specs=[HBM] * (2 * n) + [SEM] * (2 * n) + [ANY], out_specs=[HBM] * (2 * n),
        out_shape=[pltpu.HBM(a.shape, a.dtype) for a in args[:2 * n]],
        input_output_aliases={i: i for i in range(2 * n)},
        compiler_params=pltpu.CompilerParams(has_side_effects=EFFECT))(*args, after)
    return res[:n], res[n:]


def _place_shards(shards, axes):
    n = len(shards)
    shapes = [s.shape for s in shards]

    def body(*refs):
        ins, outs, local = refs[:n], refs[n:2 * n], refs[2 * n]
        x, y, c, me = _position()
        mine = [pltpu.make_async_copy(ins[w], _block(outs[w], axes[w], shapes[w], me >> 1), local.at[w])
                for w in range(n)]
        for cp in mine:
            cp.start()
        for cp in mine:
            cp.wait()

    out_shape = []
    for sh, a in zip(shards, axes):
        R, Cn = sh.shape
        out_shape.append(jax.ShapeDtypeStruct((R, 4 * Cn) if a == 1 else (4 * R, Cn), sh.dtype))
    return pl.pallas_call(
        body, name="place_shards", in_specs=[ANY] * n, out_specs=[ANY] * n, out_shape=out_shape,
        scratch_shapes=[pltpu.SemaphoreType.DMA((n,))])(*shards)


def _gather_copy(srcs, lnds, send, recv, axes, shapes, w, j, me):
    chip = me >> 1
    return (pltpu.make_async_remote_copy(
        src_ref=srcs[w], dst_ref=_block(lnds[w], axes[w], shapes[w], chip), send_sem=send[w].at[j],
        recv_sem=recv[w].at[j], device_id=_device(me ^ (2 * (j + 1))), device_id_type=MESH),
            pltpu.make_async_remote_copy(
        src_ref=srcs[w], dst_ref=_block(lnds[w], axes[w], shapes[w], chip ^ (j + 1)), send_sem=send[w].at[j],
        recv_sem=recv[w].at[j], device_id=_device(me ^ (2 * (j + 1))), device_id_type=MESH))


def _gather_start(shards, lands, axes):
    shapes = [s.shape for s in shards]

    def body(srcs, lnds, send, recv):
        x, y, c, me = _position()
        for w in range(len(shards)):
            for j in range(3):
                _gather_copy(srcs, lnds, send, recv, axes, shapes, w, j, me)[0].start()

    return _split_start(body, "gather_start", shards, lands, 3)


def _gather_wait(pairs, axes, after, name):
    shapes = [p[2].shape for p in pairs]

    def body(srcs, lnds, send, recv):
        x, y, c, me = _position()
        for w in range(len(pairs)):
            for j in range(3):
                sent, landed = _gather_copy(srcs, lnds, send, recv, axes, shapes, w, j, me)
                sent.wait_send()
                landed.wait_recv()

    return _split_wait(body, name, pairs, after)[1]


def _shard_shape(grad, axis):
    return (grad.shape[0], grad.shape[1] // 4) if axis == 1 else (grad.shape[0] // 4, grad.shape[1])


def _scatter_copy(srcs, lnds, send, recv, axes, shapes, w, m, me):
    peer = me ^ m
    return pltpu.make_async_remote_copy(
        src_ref=_half_block(srcs[w], axes[w], shapes[w], peer >> 1, peer & 1), dst_ref=lnds[w].at[m - 1],
        send_sem=send[w].at[m - 1], recv_sem=recv[w].at[m - 1], device_id=_device(peer), device_id_type=MESH)


def _scatter_start(grads, axes, name):
    shapes = [_shard_shape(g, a) for g, a in zip(grads, axes)]
    lands = [lax.empty((N_DEV - 1, R // 2, Cn), BF16) for R, Cn in shapes]

    def body(srcs, lnds, send, recv):
        x, y, c, me = _position()
        for w in range(len(grads)):
            for m in range(1, N_DEV):
                _scatter_copy(srcs, lnds, send, recv, axes, shapes, w, m, me).start()

    return _split_start(body, name, grads, lands, N_DEV - 1)


def _scatter_wait(pairs, axes, after):
    shapes = [_shard_shape(p[2], a) for p, a in zip(pairs, axes)]

    def body(srcs, lnds, send, recv):
        x, y, c, me = _position()
        for w in range(len(pairs)):
            for m in range(1, N_DEV):
                cp = _scatter_copy(srcs, lnds, send, recv, axes, shapes, w, m, me)
                cp.wait_send()
                cp.wait_recv()

    return _split_wait(body, "scatter_wait", pairs, after)


def _sum_partials(own, parts, name):
    R, Cn = own.shape
    tr = _pick(R, (256, 176, 128, 64, 32, 16, 8))

    def body(own_ref, p_ref, o_ref):
        acc = own_ref[...].astype(F32)
        for d in range(N_DEV - 1):
            acc = acc + p_ref[d].astype(F32)
        o_ref[...] = acc

    return pl.pallas_call(
        body, name=name, grid=(R // tr,),
        in_specs=[pl.BlockSpec((tr, Cn), lambda i: (i, 0)), pl.BlockSpec((N_DEV - 1, tr, Cn), lambda i: (0, i, 0))],
        out_specs=pl.BlockSpec((tr, Cn), lambda i: (i, 0)), out_shape=jax.ShapeDtypeStruct((R, Cn), F32),
        compiler_params=_params(("parallel",)))(own, parts)


def _exchange_halves(halves):
    n = len(halves)

    def body(*refs):
        ins, outs = refs[:n], refs[n:2 * n]
        send, recv, local = refs[2 * n:]
        x, y, c, me = _position()
        sib = _device(me ^ 1)

        def copy(w, half):
            return pltpu.make_async_remote_copy(src_ref=ins[w], dst_ref=_half(outs[w], half), send_sem=send.at[w],
                                                recv_sem=recv.at[w], device_id=sib, device_id_type=MESH)

        mine = [pltpu.make_async_copy(ins[w], _half(outs[w], c), local.at[w]) for w in range(n)]
        for w in range(n):
            mine[w].start()
            copy(w, c).start()
        for w in range(n):
            copy(w, 1 - c).wait_recv()
        for w in range(n):
            copy(w, c).wait_send()
            mine[w].wait()

    out_shape = [jax.ShapeDtypeStruct((2 * h.shape[0], h.shape[1]), F32) for h in halves]
    return pl.pallas_call(
        body, name="exchange_halves", in_specs=[ANY] * n, out_specs=[ANY] * n, out_shape=out_shape,
        scratch_shapes=[pltpu.SemaphoreType.DMA((n,)), pltpu.SemaphoreType.DMA((n,)),
                        pltpu.SemaphoreType.DMA((n,))],
        compiler_params=pltpu.CompilerParams(has_side_effects=True))(*halves)


def _all_reduce_small(vec):
    R, L = vec.shape

    def body(v_ref, o_ref, buf, send, recv):
        x, y, c, me = _position()
        buf[me] = v_ref[...]

        def copy(m, slot):
            return pltpu.make_async_remote_copy(src_ref=v_ref, dst_ref=buf.at[slot], send_sem=send.at[m - 1],
                                                recv_sem=recv.at[m - 1], device_id=_device(me ^ m),
                                                device_id_type=MESH)

        for m in range(1, N_DEV):
            copy(m, me).start()
        for m in range(1, N_DEV):
            copy(m, me ^ m).wait_recv()
        for m in range(1, N_DEV):
            copy(m, me).wait_send()
        acc = buf[0]
        for d in range(1, N_DEV):
            acc = acc + buf[d]
        o_ref[...] = acc

    vm = pl.BlockSpec(memory_space=pltpu.VMEM)
    return pl.pallas_call(
        body, name="all_reduce_small", in_specs=[vm], out_specs=vm, out_shape=jax.ShapeDtypeStruct((R, L), F32),
        scratch_shapes=[pltpu.VMEM((N_DEV, R, L), F32), pltpu.SemaphoreType.DMA((N_DEV - 1,)),
                        pltpu.SemaphoreType.DMA((N_DEV - 1,))],
        compiler_params=pltpu.CompilerParams(has_side_effects=True))(vec)


def _adamw(w, g, m, v, name):
    R, Cn = w.shape
    tr = _pick(R, (256, 176, 128, 64, 40, 32, 16, 8))

    def body(w_ref, g_ref, m_ref, v_ref, d_ref, nm_ref, nv_ref):
        gv = g_ref[...]
        nm = ADAM_B1 * m_ref[...] + (1.0 - ADAM_B1) * gv
        nv = ADAM_B2 * v_ref[...] + (1.0 - ADAM_B2) * (gv * gv)
        m_hat = nm / (1.0 - ADAM_B1 ** ADAM_STEP)
        v_hat = nv / (1.0 - ADAM_B2 ** ADAM_STEP)
        d_ref[...] = -ADAM_LR * (m_hat / (jnp.sqrt(v_hat) + ADAM_EPS) + ADAM_WD * w_ref[...])
        nm_ref[...] = nm
        nv_ref[...] = nv

    spec = pl.BlockSpec((tr, Cn), lambda i: (i, 0))
    return pl.pallas_call(
        body, name=name, grid=(R // tr,), in_specs=[spec] * 4, out_specs=[spec] * 3,
        out_shape=[jax.ShapeDtypeStruct((R, Cn), F32)] * 3, compiler_params=_params(("parallel",)))(w, g, m, v)


def _pack(arrays, rows):
    flat = jnp.concatenate([a.reshape(-1).astype(F32) for a in arrays])
    return jnp.pad(flat, (0, rows * 128 - flat.shape[0])).reshape(rows, 128)


def _unpack(packed, shapes):
    flat = packed.reshape(-1)
    out, off = [], 0
    for s in shapes:
        size = 1
        for d in s:
            size *= d
        out.append(flat[off:off + size].reshape(s))
        off += size
    return out


def _ffn_fwd(x, norm, arrived, tag, after=None):
    h = _rms_fwd(x, norm, f"{tag}_norm", after=after)
    w_up = arrived(f"{tag}_w_up", h)
    up = _mm(h, w_up, 'nn', f"{tag}_up", BF16)
    act = _swiglu_fwd(up, f"{tag}_swiglu")
    w_down = arrived(f"{tag}_w_down", act)
    out = _mm(act, w_down, 'nn', f"{tag}_down", F32, res=x, scale=0.5)
    return out, (h, up, act, w_up, w_down)


def _ffn_bwd(dout, x, norm, saved, tag, send):
    h, up, act, w_up, w_down = saved
    g_down = _mm(act, dout, 'tn', f"{tag}_down_dw", BF16, scale=0.5)
    token = send([f"{tag}_w_down"], [g_down])
    dact = _mm(dout, w_down, 'nt', f"{tag}_down_dx", BF16, scale=0.5, after=token)
    dup = _swiglu_bwd(dact, up, f"{tag}_swiglu_bwd")
    g_up = _mm(h, dup, 'tn', f"{tag}_up_dw", BF16)
    token = send([f"{tag}_w_up"], [g_up])
    dh = _mm(dup, w_up, 'nt', f"{tag}_up_dx", F32, after=token)
    dx, g_norm = _rms_bwd(x, norm, dh, dout, f"{tag}_norm_bwd")
    return dx, g_norm


def kernel(x, mem, ffn1_norm, ffn1_w_up, ffn1_w_down, mix_norm, mem_norm, w_in, b_gate, conv_dw_w, conv_dw_b, conv_ln_g, conv_ln_b, conv_w_pw, att_rel_bias, att_w_o, mem_w_kv, mem_w_o, w_out, ffn2_norm, ffn2_w_up, ffn2_w_down, final_norm, loss_target, m_ffn1_norm, m_ffn1_w_up, m_ffn1_w_down, m_mix_norm, m_mem_norm, m_w_in, m_b_gate, m_conv_dw_w, m_conv_dw_b, m_conv_ln_g, m_conv_ln_b, m_conv_w_pw, m_att_rel_bias, m_att_w_o, m_mem_w_kv, m_mem_w_o, m_w_out, m_ffn2_norm, m_ffn2_w_up, m_ffn2_w_down, m_final_norm, v_ffn1_norm, v_ffn1_w_up, v_ffn1_w_down, v_mix_norm, v_mem_norm, v_w_in, v_b_gate, v_conv_dw_w, v_conv_dw_b, v_conv_ln_g, v_conv_ln_b, v_conv_w_pw, v_att_rel_bias, v_att_w_o, v_mem_w_kv, v_mem_w_o, v_w_out, v_ffn2_norm, v_ffn2_w_up, v_ffn2_w_down, v_final_norm):
    given = dict(locals())
    wts = {n: given[n] for n in WEIGHTS}
    mom1 = {n: given["m_" + n] for n in WEIGHTS}
    mom2 = {n: given["v_" + n] for n in WEIGHTS}
    Bl, S, Dm = x.shape
    T = Bl * S
    x0 = x.reshape(T, Dm)
    tgt = loss_target.reshape(T, Dm)
    mem2 = mem.reshape(Bl * MEM_LEN, Dm)

    big_names = [n for n, _ in BIG]
    big_axes = [a for _, a in BIG]
    chip = 2 * lax.axis_index("x") + lax.axis_index("y")

    core = lax.axis_index("c")
    axis_of = dict(BIG)

    gather_groups = [['ffn1_w_up'], ['ffn1_w_down'], ['w_in', 'conv_dw_w'],
                     ['mem_w_kv', 'conv_w_pw', 'att_w_o', 'mem_w_o', 'w_out'], ['ffn2_w_up'], ['ffn2_w_down']]
    gather_names = [n for grp in gather_groups for n in grp]
    gather_axes = [axis_of.get(n, 1) for n in gather_names]
    shards = [jnp.pad(conv_dw_w[0], ((0, 1), (0, 0))) if n == 'conv_dw_w' else wts[n][0].astype(BF16)
              for n in gather_names]
    in_flight, gather_token = _gather_start(shards, _place_shards(shards, gather_axes), gather_axes)
    in_flight = dict(zip(gather_names, in_flight))
    full = {}

    def arrived(name, after):
        if name not in full:
            grp = next(grp for grp in gather_groups if name in grp)
            lands = _gather_wait([in_flight[n] for n in grp], [axis_of.get(n, 1) for n in grp], after,
                                 f"gather_wait_{grp[0]}")
            full.update(zip(grp, lands))
        return full[name]

    scattering = {}

    def send(names, grads):
        pairs, token = _scatter_start(grads, [axis_of[n] for n in names], f"scatter_start_{names[0]}")
        scattering.update(zip(names, pairs))
        return token

    final_g = final_norm.reshape(1, Dm)
    bias = _att_bias(att_rel_bias[0])

    x1, ffn1_saved = _ffn_fwd(x0, ffn1_norm, arrived, "ffn1", after=gather_token)
    h = _rms_fwd(x1, mix_norm, "mix_norm")
    w_in_full = arrived('w_in', h)
    dw_full = full['conv_dw_w'][:CONV_KERNEL]
    proj = _mm(h, w_in_full, 'nn', "w_in", BF16)
    proj3 = proj.reshape(Bl, S, proj.shape[1])
    cv, c_act = _conv_fwd(proj3, dw_full, conv_dw_b, conv_ln_g, conv_ln_b)
    o_att = _att_fwd(proj3, bias)
    mem_h = _rms_fwd(mem2, mem_norm, "mem_norm")
    kv = _mm(mem_h, arrived('mem_w_kv', o_att), 'nn', "mem_kv", BF16)
    kv3 = kv.reshape(Bl, MEM_LEN, 2 * MEM_WIDTH)
    o_mem = _mem_fwd(proj3, kv3)
    c_act2, o_att2, o_mem2 = c_act.reshape(T, -1), o_att.reshape(T, -1), o_mem.reshape(T, -1)
    x2, yc, ya, ym, y = _mix_fwd(c_act2, o_att2, o_mem2, proj, b_gate, x1, full['conv_w_pw'], full['att_w_o'],
                                 full['mem_w_o'], full['w_out'])
    x3, ffn2_saved = _ffn_fwd(x2, ffn2_norm, arrived, "ffn2")
    dx3, g_final, loss_vec = _final_fwd_bwd(x3, tgt, final_g)

    g = {}
    dx2, g['ffn2_norm'] = _ffn_bwd(dx3, x2, ffn2_norm, ffn2_saved, "ffn2", send)
    dyc, dya, dym, dgl, g['b_gate'], dc, doa, dom = _mix_bwd(
        dx2, yc, ya, ym, proj, b_gate, full['conv_w_pw'], full['att_w_o'], full['mem_w_o'], full['w_out'])
    token = send(['w_out', 'conv_w_pw', 'att_w_o', 'mem_w_o'],
                 [_mm(y, dx2, 'tn', "w_out_dw", BF16), _mm(c_act2, dyc, 'tn', "conv_pw_dw", BF16),
                  _mm(o_att2, dya, 'tn', "att_o_dw", BF16), _mm(o_mem2, dym, 'tn', "mem_o_dw", BF16)])
    du, g_dw, g['conv_dw_b'], g['conv_ln_g'], g['conv_ln_b'] = _conv_bwd(
        proj3, cv, dc.reshape(Bl, S, -1), dw_full, conv_ln_g, conv_ln_b)
    dq, dk, dv, dbias = _att_bwd(proj3, doa.reshape(Bl, S, -1), bias)
    g['att_rel_bias'] = _rel_bias_grad(dbias)
    dmq, dkv = _mem_bwd(proj3, kv3, dom.reshape(Bl, S, -1))
    dkv2 = dkv.reshape(Bl * MEM_LEN, 2 * MEM_WIDTH)
    g_kv = _mm(mem_h, dkv2, 'tn', "mem_kv_dw", BF16, after=token)
    dmem_h = _mm(dkv2, full['mem_w_kv'], 'nt', "mem_kv_dx", F32)
    _, g['mem_norm'] = _rms_bwd(mem2, mem_norm, dmem_h, dmem_h, "mem_norm_bwd")
    dproj = jnp.concatenate([du.reshape(T, -1), dq.reshape(T, -1), dk.reshape(T, -1), dv.reshape(T, -1),
                             dmq.reshape(T, -1), dgl], axis=1)
    token = send(['mem_w_kv', 'w_in'], [g_kv, _mm(h, dproj, 'tn', "w_in_dw", BF16)])
    dh = _mm(dproj, w_in_full, 'nt', "w_in_dx", F32, after=token)
    dx1, g['mix_norm'] = _rms_bwd(x1, mix_norm, dh, dx2, "mix_norm_bwd")
    dx0, g['ffn1_norm'] = _ffn_bwd(dx1, x0, ffn1_norm, ffn1_saved, "ffn1", send)
    g['final_norm'] = g_final

    sent, landed = _scatter_wait([scattering[n] for n in big_names], big_axes, dx0)
    halves = []
    for n, a, own_full, parts in zip(big_names, big_axes, sent, landed):
        R, Cn = _shard_shape(own_full, a)
        start = (core * (R // 2), chip * Cn) if a == 1 else (chip * R + core * (R // 2), 0)
        halves.append(_sum_partials(lax.dynamic_slice(own_full, start, (R // 2, Cn)), parts, f"sum_{n}"))
    for n, sg in zip(big_names, _exchange_halves(halves)):
        g[n] = sg

    small_shapes = [wts[n].shape for n in SMALL]
    n_small = sum(int(wts[n].size) for n in SMALL)
    n_red = n_small + CONV_KERNEL * CONV_WIDTH
    red = _all_reduce_small(_pack([g[n] for n in SMALL] + [g_dw], -(-n_red // 1024) * 8))
    red_list = _unpack(red, small_shapes + [(CONV_KERNEL, CONV_WIDTH)])
    for n, rg in zip(SMALL, red_list[:-1]):
        g[n] = rg
    dw_cols = conv_dw_w.shape[2]
    g['conv_dw_w'] = lax.dynamic_slice(red_list[-1], (0, chip * dw_cols), (CONV_KERNEL, dw_cols))[None]

    delta, new_m, new_v = {}, {}, {}
    for n in big_names:
        g[n] = g[n][None]
        d, nm, nv = _adamw(wts[n][0], g[n][0], mom1[n][0], mom2[n][0], f"adamw_{n}")
        delta[n], new_m[n], new_v[n] = d[None], nm[None], nv[None]
    rest = SMALL + ['conv_dw_w']
    rest_shapes = [wts[n].shape for n in rest]
    rows = -(-sum(int(wts[n].size) for n in rest) // 1024) * 8
    packed = [_pack([src[n] for n in rest], rows) for src in (wts, g, mom1, mom2)]
    for out, res in zip((delta, new_m, new_v), _adamw(*packed, "adamw_small")):
        for n, a in zip(rest, _unpack(res, rest_shapes)):
            out[n] = a

    loss = lax.psum(loss_vec[0, 0], ("x", "y", "c"))
    grad_x = dx0.reshape(Bl, S, Dm)
    return (loss, grad_x, *[g[n] for n in WEIGHTS], *[delta[n] for n in WEIGHTS],
            *[new_m[n] for n in WEIGHTS], *[new_v[n] for n in WEIGHTS])
```

```python
import jax
import jax.numpy as jnp
from jax import lax
from jax.experimental import pallas as pl
from jax.experimental.pallas import tpu as pltpu

F32 = jnp.float32
BF16 = jnp.bfloat16

D_MODEL = 1024
D_FF = 2816
CHUNK = 64
LEFT_CHUNKS = 8
MAX_REL = 128
N_REL = (CHUNK - 1) + MAX_REL + 1
CONV_WIDTH = 512
CONV_KERNEL = 31
ATT_HEADS = 8
ATT_WIDTH = 512
MEM_LEN = 256
MEM_HEADS = 4
MEM_HEAD_DIM = 128
MEM_WIDTH = 512
EPS = 1e-6
MASK_VALUE = -1e30
ATT_SCALE = 64 ** -0.5
MEM_SCALE = 128 ** -0.5

ADAM_LR = 0.001
ADAM_B1 = 0.9
ADAM_B2 = 0.999
ADAM_EPS = 1e-08
ADAM_WD = 0.01
ADAM_STEP = 10

QB = 256
KW = 3 * QB
CONV_PAD = 32
CONV_TILE = 256

VMEM_LIMIT = 56 << 20

WEIGHTS = ['ffn1_norm', 'ffn1_w_up', 'ffn1_w_down', 'mix_norm', 'mem_norm', 'w_in', 'b_gate', 'conv_dw_w',
           'conv_dw_b', 'conv_ln_g', 'conv_ln_b', 'conv_w_pw', 'att_rel_bias', 'att_w_o', 'mem_w_kv', 'mem_w_o',
           'w_out', 'ffn2_norm', 'ffn2_w_up', 'ffn2_w_down', 'final_norm']
BIG = [('ffn1_w_up', 1), ('ffn1_w_down', 0), ('w_in', 1), ('conv_w_pw', 1), ('att_w_o', 1), ('mem_w_kv', 0),
       ('mem_w_o', 1), ('w_out', 0), ('ffn2_w_up', 1), ('ffn2_w_down', 0)]
SMALL = ['ffn1_norm', 'mix_norm', 'mem_norm', 'b_gate', 'conv_dw_b', 'conv_ln_g', 'conv_ln_b', 'att_rel_bias',
         'ffn2_norm', 'final_norm']
N_CHIPS = 4
N_DEV = 8
MESH = pl.DeviceIdType.MESH


def _pick(n, cands):
    for c in cands:
        if n % c == 0:
            return c
    return n


def _sig(x):
    return 1.0 / (1.0 + jnp.exp(-x))


def _params(sem=None, vmem=VMEM_LIMIT):
    return pltpu.CompilerParams(dimension_semantics=sem, vmem_limit_bytes=vmem)


def _dot(a, b, mode='nn'):
    dims = {'nn': (((1,), (0,)), ((), ())), 'nt': (((1,), (1,)), ((), ())), 'tn': (((0,), (0,)), ((), ()))}[mode]
    return lax.dot_general(a.astype(BF16), b.astype(BF16), dims, preferred_element_type=F32)


def _mm(a, b, mode, name, out_dtype, res=None, scale=1.0, after=None):
    if mode == 'nn':
        (M, C), (_, N) = a.shape, b.shape
    elif mode == 'nt':
        (M, C), (N, _) = a.shape, b.shape
    else:
        (C, M), (_, N) = a.shape, b.shape
    tm = _pick(M, (1024, 1408, 512, 256, 128))
    tn = _pick(N, (1024, 1408, 512, 256, 128))
    tc = _pick(C, (1024, 1408, 512, 256, 128))
    nk = C // tc
    if mode == 'nn':
        a_spec = pl.BlockSpec((tm, tc), lambda i, j, k: (i, k))
        b_spec = pl.BlockSpec((tc, tn), lambda i, j, k: (k, j))
    elif mode == 'nt':
        a_spec = pl.BlockSpec((tm, tc), lambda i, j, k: (i, k))
        b_spec = pl.BlockSpec((tn, tc), lambda i, j, k: (j, k))
    else:
        a_spec = pl.BlockSpec((tc, tm), lambda i, j, k: (k, i))
        b_spec = pl.BlockSpec((tc, tn), lambda i, j, k: (k, j))
    o_spec = pl.BlockSpec((tm, tn), lambda i, j, k: (i, j))
    has_res = res is not None
    has_after = after is not None

    def body(*refs):
        a_ref, b_ref = refs[:2]
        r_ref = refs[2] if has_res else None
        o_ref, acc_ref = refs[-2:]
        k = pl.program_id(2)
        p = _dot(a_ref[...], b_ref[...], mode)

        def finish(acc):
            if scale != 1.0:
                acc = acc * scale
            if r_ref is not None:
                acc = r_ref[...] + acc
            o_ref[...] = acc.astype(o_ref.dtype)

        if nk == 1:
            finish(p)
        else:
            @pl.when(k == 0)
            def _():
                acc_ref[...] = p

            @pl.when(k > 0)
            def _():
                acc_ref[...] += p

            @pl.when(k == nk - 1)
            def _():
                finish(acc_ref[...])

    in_specs = [a_spec, b_spec] + ([o_spec] if has_res else []) + ([ANY] if has_after else [])
    args = (a, b) + ((res,) if has_res else ()) + ((after,) if has_after else ())
    acc_shape = (tm, tn) if nk > 1 else (8, 128)
    return pl.pallas_call(
        body, name=name, grid=(M // tm, N // tn, nk), in_specs=in_specs, out_specs=o_spec,
        out_shape=jax.ShapeDtypeStruct((M, N), out_dtype), scratch_shapes=[pltpu.VMEM(acc_shape, F32)],
        compiler_params=_params(("parallel", "parallel", "arbitrary")))(*args)


def _row_tile(T):
    return _pick(T, (512, 256, 128, 64, 32, 16, 8))


def _rms_fwd(x, g, name, after=None):
    T, Dm = x.shape
    tm = _row_tile(T)

    def body(x_ref, g_ref, *rest):
        o_ref = rest[-1]
        xv = x_ref[...]
        r = lax.rsqrt(jnp.mean(xv * xv, axis=-1, keepdims=True) + EPS)
        o_ref[...] = ((xv * r) * g_ref[...]).astype(o_ref.dtype)

    extra = () if after is None else (after,)
    return pl.pallas_call(
        body, name=name, grid=(T // tm,),
        in_specs=[pl.BlockSpec((tm, Dm), lambda i: (i, 0)), pl.BlockSpec((1, Dm), lambda i: (0, 0))]
        + [ANY] * len(extra),
        out_specs=pl.BlockSpec((tm, Dm), lambda i: (i, 0)), out_shape=jax.ShapeDtypeStruct((T, Dm), BF16),
        compiler_params=_params(("parallel",)))(x, g, *extra)


def _rms_bwd(x, g, dh, dres, name):
    T, Dm = x.shape
    tm = _row_tile(T)

    def body(x_ref, g_ref, dh_ref, dr_ref, dx_ref, dg_ref):
        i = pl.program_id(0)
        xv = x_ref[...]
        r = lax.rsqrt(jnp.mean(xv * xv, axis=-1, keepdims=True) + EPS)
        xr = xv * r
        dh_v = dh_ref[...].astype(F32)
        dyg = dh_v * g_ref[...]
        dx = r * (dyg - xr * jnp.mean(dyg * xr, axis=-1, keepdims=True))
        dx_ref[...] = dr_ref[...] + dx

        @pl.when(i == 0)
        def _():
            dg_ref[...] = jnp.zeros_like(dg_ref)

        dg_ref[...] += jnp.sum(dh_v * xr, axis=0, keepdims=True)

    row = pl.BlockSpec((tm, Dm), lambda i: (i, 0))
    vec = pl.BlockSpec((1, Dm), lambda i: (0, 0))
    return pl.pallas_call(
        body, name=name, grid=(T // tm,), in_specs=[row, vec, row, row], out_specs=[row, vec],
        out_shape=[jax.ShapeDtypeStruct((T, Dm), F32), jax.ShapeDtypeStruct((1, Dm), F32)],
        compiler_params=_params(("arbitrary",)))(x, g, dh, dres)


def _final_fwd_bwd(x3, tgt, g):
    T, Dm = x3.shape
    tm = _row_tile(T)

    def body(x_ref, t_ref, g_ref, dx_ref, dg_ref, loss_ref):
        i = pl.program_id(0)
        xv = x_ref[...]
        gg = g_ref[...]
        r = lax.rsqrt(jnp.mean(xv * xv, axis=-1, keepdims=True) + EPS)
        xr = xv * r
        err = xr * gg - t_ref[...]
        dout = err * (1.0 / Dm)
        dyg = dout * gg
        dx_ref[...] = r * (dyg - xr * jnp.mean(dyg * xr, axis=-1, keepdims=True))

        @pl.when(i == 0)
        def _():
            dg_ref[...] = jnp.zeros_like(dg_ref)
            loss_ref[...] = jnp.zeros_like(loss_ref)

        dg_ref[...] += jnp.sum(dout * xr, axis=0, keepdims=True)
        loss_ref[...] += jnp.zeros_like(loss_ref) + (0.5 / Dm) * jnp.sum(err * err)

    row = pl.BlockSpec((tm, Dm), lambda i: (i, 0))
    vec = pl.BlockSpec((1, Dm), lambda i: (0, 0))
    one = pl.BlockSpec((1, 128), lambda i: (0, 0))
    return pl.pallas_call(
        body, name="final_fwd_bwd", grid=(T // tm,), in_specs=[row, row, vec], out_specs=[row, vec, one],
        out_shape=[jax.ShapeDtypeStruct((T, Dm), F32), jax.ShapeDtypeStruct((1, Dm), F32),
                   jax.ShapeDtypeStruct((1, 128), F32)],
        compiler_params=_params(("arbitrary",)))(x3, tgt, g)


def _ffn_up(h, w_up, name):
    T, K = h.shape
    Fh = w_up.shape[1] // 2
    tm = _pick(T, (512, 256, 128))
    tn = _pick(Fh, (1408, 512, 256, 128))
    nj = Fh // tn

    def body(h_ref, wa_ref, wb_ref, a_ref, b_ref, act_ref):
        hv = h_ref[...]
        a = _dot(hv, wa_ref[...])
        b = _dot(hv, wb_ref[...])
        a_ref[...] = a.astype(BF16)
        b_ref[...] = b.astype(BF16)
        act_ref[...] = (a * _sig(a) * b).astype(BF16)

    out = pl.BlockSpec((tm, tn), lambda i, j: (i, j))
    return pl.pallas_call(
        body, name=name, grid=(T // tm, nj),
        in_specs=[pl.BlockSpec((tm, K), lambda i, j: (i, 0)), pl.BlockSpec((K, tn), lambda i, j: (0, j)),
                  pl.BlockSpec((K, tn), lambda i, j: (0, j + nj))],
        out_specs=[out, out, out], out_shape=[jax.ShapeDtypeStruct((T, Fh), BF16)] * 3,
        compiler_params=_params(("parallel", "parallel")))(h, w_up, w_up)


def _swiglu_bwd(dact, a, b, name):
    T, Fh = a.shape
    tm = _pick(T, (256, 128, 64, 32, 16, 8))

    def body(d_ref, a_ref, b_ref, o_ref):
        av = a_ref[...].astype(F32)
        bv = b_ref[...].astype(F32)
        d = d_ref[...].astype(F32)
        s = _sig(av)
        o_ref[:, :Fh] = (d * bv * s * (1.0 + av * (1.0 - s))).astype(o_ref.dtype)
        o_ref[:, Fh:] = (d * av * s).astype(o_ref.dtype)

    half = pl.BlockSpec((tm, Fh), lambda i: (i, 0))
    return pl.pallas_call(
        body, name=name, grid=(T // tm,), in_specs=[half, half, half],
        out_specs=pl.BlockSpec((tm, 2 * Fh), lambda i: (i, 0)), out_shape=jax.ShapeDtypeStruct((T, 2 * Fh), BF16),
        compiler_params=_params(("parallel",)))(dact, a, b)


def _mix_fwd(c_act, o_att, o_mem, proj, b_gate, x1, w_pw, w_o, w_mo, w_out):
    T, Dm = x1.shape
    W = c_act.shape[1]
    tm = _pick(T, (256, 128, 64, 32, 16, 8))

    def body(c_ref, oa_ref, om_ref, gl_ref, bg_ref, x1_ref, wpw_ref, wo_ref, wmo_ref, wout_ref,
             x2_ref, yc_ref, ya_ref, ym_ref, y_ref):
        yc = _dot(c_ref[...], wpw_ref[...])
        ya = _dot(oa_ref[...], wo_ref[...])
        ym = _dot(om_ref[...], wmo_ref[...])
        g = _sig(gl_ref[...].astype(F32) + bg_ref[...])
        y = g[:, :Dm] * yc + g[:, Dm:2 * Dm] * ya + g[:, 2 * Dm:] * ym
        x2_ref[...] = x1_ref[...] + _dot(y, wout_ref[...])
        yc_ref[...] = yc.astype(BF16)
        ya_ref[...] = ya.astype(BF16)
        ym_ref[...] = ym.astype(BF16)
        y_ref[...] = y.astype(BF16)

    rowW = pl.BlockSpec((tm, W), lambda i: (i, 0))
    rowD = pl.BlockSpec((tm, Dm), lambda i: (i, 0))
    full = lambda s: pl.BlockSpec(s, lambda i: (0, 0))
    return pl.pallas_call(
        body, name="mix_fwd", grid=(T // tm,),
        in_specs=[rowW, rowW, rowW, pl.BlockSpec((tm, 3 * Dm), lambda i: (i, 1)), full((1, 3 * Dm)), rowD,
                  full((W, Dm)), full((W, Dm)), full((W, Dm)), full((Dm, Dm))],
        out_specs=[rowD] * 5,
        out_shape=[jax.ShapeDtypeStruct((T, Dm), F32)] + [jax.ShapeDtypeStruct((T, Dm), BF16)] * 4,
        compiler_params=_params(("parallel",)))(c_act, o_att, o_mem, proj, b_gate, x1, w_pw, w_o, w_mo, w_out)


def _mix_bwd(dx2, yc, ya, ym, proj, b_gate, w_pw, w_o, w_mo, w_out):
    T, Dm = dx2.shape
    W = w_pw.shape[0]
    tm = _pick(T, (256, 128, 64, 32, 16, 8))

    def body(dx_ref, yc_ref, ya_ref, ym_ref, gl_ref, bg_ref, wpw_ref, wo_ref, wmo_ref, wout_ref,
             dyc_ref, dya_ref, dym_ref, dgl_ref, dbg_ref, dc_ref, doa_ref, dom_ref):
        i = pl.program_id(0)

        @pl.when(i == 0)
        def _():
            dbg_ref[...] = jnp.zeros_like(dbg_ref)

        dy = _dot(dx_ref[...], wout_ref[...], 'nt')
        g = _sig(gl_ref[...].astype(F32) + bg_ref[...])
        branches = ((yc_ref, dyc_ref, wpw_ref, dc_ref), (ya_ref, dya_ref, wo_ref, doa_ref),
                    (ym_ref, dym_ref, wmo_ref, dom_ref))
        for n, (y_ref, dyk_ref, w_ref, dk_ref) in enumerate(branches):
            gk = g[:, n * Dm:(n + 1) * Dm]
            dyk = dy * gk
            dgl = dyk * y_ref[...].astype(F32) * (1.0 - gk)
            dyk_ref[...] = dyk.astype(BF16)
            dgl_ref[:, n * Dm:(n + 1) * Dm] = dgl.astype(BF16)
            dbg_ref[:, n * Dm:(n + 1) * Dm] += jnp.sum(dgl, axis=0, keepdims=True)
            dk_ref[...] = _dot(dyk, w_ref[...], 'nt').astype(BF16)

    rowW = pl.BlockSpec((tm, W), lambda i: (i, 0))
    rowD = pl.BlockSpec((tm, Dm), lambda i: (i, 0))
    row3 = pl.BlockSpec((tm, 3 * Dm), lambda i: (i, 0))
    full = lambda s: pl.BlockSpec(s, lambda i: (0, 0))
    return pl.pallas_call(
        body, name="mix_bwd", grid=(T // tm,),
        in_specs=[rowD, rowD, rowD, rowD, pl.BlockSpec((tm, 3 * Dm), lambda i: (i, 1)), full((1, 3 * Dm)),
                  full((W, Dm)), full((W, Dm)), full((W, Dm)), full((Dm, Dm))],
        out_specs=[rowD, rowD, rowD, row3, full((1, 3 * Dm)), rowW, rowW, rowW],
        out_shape=[jax.ShapeDtypeStruct((T, Dm), BF16)] * 3 + [jax.ShapeDtypeStruct((T, 3 * Dm), BF16),
                                                                jax.ShapeDtypeStruct((1, 3 * Dm), F32)]
        + [jax.ShapeDtypeStruct((T, W), BF16)] * 3,
        compiler_params=_params(("arbitrary",)))(dx2, yc, ya, ym, proj, b_gate, w_pw, w_o, w_mo, w_out)


def _ln_swish(cv, lg, lb):
    mu = jnp.mean(cv, axis=-1, keepdims=True)
    xc = cv - mu
    r = lax.rsqrt(jnp.mean(xc * xc, axis=-1, keepdims=True) + EPS)
    n = xc * r
    l = n * lg + lb
    return r, n, l


def _conv_fwd(proj3, dw_w, dw_b, ln_g, ln_b):
    Bl, S, _ = proj3.shape
    C, K, TS, PAD = CONV_WIDTH, CONV_KERNEL, CONV_TILE, CONV_PAD
    nt = S // TS

    def body(u_ref, w_ref, b_ref, lg_ref, lb_ref, cv_ref, c_ref, vbuf, win):
        vbuf[0:PAD, :] = jnp.zeros((PAD, C), F32)

        def glu(t, carry):
            r0 = pl.multiple_of(t * TS, TS)
            u = u_ref[pl.ds(r0, TS), :].astype(F32)
            vbuf[pl.ds(PAD + r0, TS), :] = u[:, :C] * _sig(u[:, C:])
            return carry

        lax.fori_loop(0, nt, glu, 0)

        def conv(t, carry):
            r0 = pl.multiple_of(t * TS, TS)
            win[...] = vbuf[pl.ds(r0, TS + PAD), :]
            acc = jnp.zeros((TS, C), F32)
            for j in range(K):
                acc = acc + w_ref[j:j + 1, :] * win[PAD - (K - 1) + j:PAD - (K - 1) + j + TS, :]
            cv = acc + b_ref[...]
            cv_ref[pl.ds(r0, TS), :] = cv
            _, _, l = _ln_swish(cv, lg_ref[...], lb_ref[...])
            c_ref[pl.ds(r0, TS), :] = (l * _sig(l)).astype(BF16)
            return carry

        lax.fori_loop(0, nt, conv, 0)

    vec = pl.BlockSpec((1, C), lambda b: (0, 0))
    return pl.pallas_call(
        body, name="conv_fwd", grid=(Bl,),
        in_specs=[pl.BlockSpec((None, S, 2 * C), lambda b: (b, 0, 0)), pl.BlockSpec((K, C), lambda b: (0, 0)),
                  vec, vec, vec],
        out_specs=[pl.BlockSpec((None, S, C), lambda b: (b, 0, 0))] * 2,
        out_shape=[jax.ShapeDtypeStruct((Bl, S, C), F32), jax.ShapeDtypeStruct((Bl, S, C), BF16)],
        scratch_shapes=[pltpu.VMEM((S + PAD, C), F32), pltpu.VMEM((TS + PAD, C), F32)],
        compiler_params=_params(("parallel",)))(proj3, dw_w, dw_b, ln_g, ln_b)


def _conv_bwd(proj3, cv, dc, dw_w, ln_g, ln_b):
    Bl, S, _ = proj3.shape
    C, K, TS, PAD = CONV_WIDTH, CONV_KERNEL, CONV_TILE, CONV_PAD
    nt = S // TS

    def body(u_ref, cv_ref, dc_ref, w_ref, lg_ref, lb_ref, du_ref, dw_ref, db_ref, dlg_ref, dlb_ref,
             vbuf, gbuf, win, dwacc):
        b = pl.program_id(0)

        @pl.when(b == 0)
        def _():
            dw_ref[...] = jnp.zeros_like(dw_ref)
            db_ref[...] = jnp.zeros_like(db_ref)
            dlg_ref[...] = jnp.zeros_like(dlg_ref)
            dlb_ref[...] = jnp.zeros_like(dlb_ref)

        vbuf[0:PAD, :] = jnp.zeros((PAD, C), F32)
        gbuf[S:S + PAD, :] = jnp.zeros((PAD, C), F32)
        dwacc[...] = jnp.zeros_like(dwacc)

        def norm_bwd(t, carry):
            r0 = pl.multiple_of(t * TS, TS)
            u = u_ref[pl.ds(r0, TS), :].astype(F32)
            vbuf[pl.ds(PAD + r0, TS), :] = u[:, :C] * _sig(u[:, C:])
            r, n, l = _ln_swish(cv_ref[pl.ds(r0, TS), :], lg_ref[...], lb_ref[...])
            s = _sig(l)
            dl = dc_ref[pl.ds(r0, TS), :].astype(F32) * s * (1.0 + l * (1.0 - s))
            dlg_ref[...] += jnp.sum(dl * n, axis=0, keepdims=True)
            dlb_ref[...] += jnp.sum(dl, axis=0, keepdims=True)
            dn = dl * lg_ref[...]
            dcv = r * (dn - jnp.mean(dn, axis=-1, keepdims=True) - n * jnp.mean(dn * n, axis=-1, keepdims=True))
            gbuf[pl.ds(r0, TS), :] = dcv
            db_ref[...] += jnp.sum(dcv, axis=0, keepdims=True)
            return carry

        lax.fori_loop(0, nt, norm_bwd, 0)

        def conv_bwd(t, carry):
            r0 = pl.multiple_of(t * TS, TS)
            win[...] = gbuf[pl.ds(r0, TS + PAD), :]
            dv = jnp.zeros((TS, C), F32)
            for j in range(K):
                dv = dv + w_ref[j:j + 1, :] * win[K - 1 - j:K - 1 - j + TS, :]
            u = u_ref[pl.ds(r0, TS), :].astype(F32)
            a, g = u[:, :C], u[:, C:]
            s = _sig(g)
            du_ref[pl.ds(r0, TS), 0:C] = (dv * s).astype(BF16)
            du_ref[pl.ds(r0, TS), C:2 * C] = (dv * a * s * (1.0 - s)).astype(BF16)
            dcv = gbuf[pl.ds(r0, TS), :]
            win[...] = vbuf[pl.ds(r0, TS + PAD), :]
            for j in range(K):
                prod = dcv * win[PAD - (K - 1) + j:PAD - (K - 1) + j + TS, :]
                dwacc[j] += jnp.sum(prod.reshape(TS // 8, 8, C), axis=0)
            return carry

        lax.fori_loop(0, nt, conv_bwd, 0)
        dw_ref[...] += jnp.sum(dwacc[...], axis=1)

    vec = pl.BlockSpec((1, C), lambda b: (0, 0))
    seq = lambda w: pl.BlockSpec((None, S, w), lambda b: (b, 0, 0))
    return pl.pallas_call(
        body, name="conv_bwd", grid=(Bl,),
        in_specs=[seq(2 * C), seq(C), seq(C), pl.BlockSpec((K, C), lambda b: (0, 0)), vec, vec],
        out_specs=[seq(2 * C), pl.BlockSpec((K, C), lambda b: (0, 0)), vec, vec, vec],
        out_shape=[jax.ShapeDtypeStruct((Bl, S, 2 * C), BF16), jax.ShapeDtypeStruct((K, C), F32)]
        + [jax.ShapeDtypeStruct((1, C), F32)] * 3,
        scratch_shapes=[pltpu.VMEM((S + PAD, C), F32), pltpu.VMEM((S + PAD, C), F32),
                        pltpu.VMEM((TS + PAD, C), F32), pltpu.VMEM((K, 8, C), F32)],
        compiler_params=_params(("arbitrary",)))(proj3, cv, dc, dw_w, ln_g, ln_b)


def _att_bias(rel_bias):
    H = rel_bias.shape[0]
    Wd = KW + QB
    c = jnp.arange(Wd + 1)
    by_offset = rel_bias[:, jnp.clip(KW - c, -(CHUNK - 1), MAX_REL) + (CHUNK - 1)]
    flat = jnp.broadcast_to(by_offset[:, None, :], (H, QB, Wd + 1)).reshape(H, QB * (Wd + 1))
    skew = jnp.pad(flat, ((0, 0), (0, (QB + 1) * Wd - QB * (Wd + 1)))).reshape(H, QB + 1, Wd)[:, :QB, QB:]
    qi = jnp.arange(QB)[:, None]
    kj = jnp.arange(KW)[None, :]
    dchunk = ((KW - QB) + qi) // CHUNK - kj // CHUNK
    band = (dchunk >= 0) & (dchunk <= LEFT_CHUNKS)
    return jnp.where(band[None], skew, MASK_VALUE)


def _head_masks():
    lane = lax.broadcasted_iota(jnp.int32, (1, 128), 1)
    return (lane < 64, lane >= 64)


def _att_probs(qh, k2, bias, valid):
    s = _dot(qh, k2, 'nt') * ATT_SCALE + bias
    s = jnp.where(valid, s, MASK_VALUE)
    e = jnp.exp(s - jnp.max(s, axis=-1, keepdims=True))
    return e * (1.0 / jnp.sum(e, axis=-1, keepdims=True))


def _att_specs(S, q_col):
    nb = S // QB
    q_spec = pl.BlockSpec((None, QB, ATT_WIDTH), lambda b, i: (b, jnp.minimum(i, nb - 1), q_col))

    def kv_spec(col, kb):
        return pl.BlockSpec((None, QB, ATT_WIDTH),
                            lambda b, i: (b, jnp.clip(i - 2 + kb, 0, nb - 1), col))

    return q_spec, [kv_spec(3, kb) for kb in range(3)], [kv_spec(4, kb) for kb in range(3)]


def _att_fwd(proj3, bias):
    Bl, S, _ = proj3.shape
    nb = S // QB
    q_spec, k_specs, v_specs = _att_specs(S, 2)

    def body(q_ref, k0, k1, k2r, v0, v1, v2r, bias_ref, o_ref):
        i = pl.program_id(1)
        masks = _head_masks()
        valid = lax.broadcasted_iota(jnp.int32, (QB, KW), 1) >= (2 - i) * QB
        for pr in range(ATT_HEADS // 2):
            ls = slice(128 * pr, 128 * (pr + 1))
            q2 = q_ref[:, ls]
            k2 = jnp.concatenate([k0[:, ls], k1[:, ls], k2r[:, ls]], axis=0)
            v2 = jnp.concatenate([v0[:, ls], v1[:, ls], v2r[:, ls]], axis=0)
            o2 = jnp.zeros((QB, 128), F32)
            for hh in range(2):
                p = _att_probs(jnp.where(masks[hh], q2, 0), k2, bias_ref[2 * pr + hh], valid)
                o2 = o2 + _dot(p, jnp.where(masks[hh], v2, 0))
            o_ref[:, ls] = o2.astype(BF16)

    return pl.pallas_call(
        body, name="att_fwd", grid=(Bl, nb),
        in_specs=[q_spec] + k_specs + v_specs + [pl.BlockSpec((ATT_HEADS, QB, KW), lambda b, i: (0, 0, 0))],
        out_specs=pl.BlockSpec((None, QB, ATT_WIDTH), lambda b, i: (b, i, 0)),
        out_shape=jax.ShapeDtypeStruct((Bl, S, ATT_WIDTH), BF16),
        compiler_params=_params(("parallel", "arbitrary")))(*([proj3] * 7), bias)


def _att_bwd(proj3, do, bias):
    Bl, S, _ = proj3.shape
    nb = S // QB
    q_spec, k_specs, v_specs = _att_specs(S, 2)
    do_spec = pl.BlockSpec((None, QB, ATT_WIDTH), lambda b, i: (b, jnp.minimum(i, nb - 1), 0))
    kv_out = pl.BlockSpec((None, QB, ATT_WIDTH), lambda b, i: (b, jnp.clip(i - 2, 0, nb - 1), 0))
    bias_spec = pl.BlockSpec((ATT_HEADS, QB, KW), lambda b, i: (0, 0, 0))

    def body(q_ref, k0, k1, k2r, v0, v1, v2r, do_ref, bias_ref, dq_ref, dk_ref, dv_ref, db_ref, dkw, dvw):
        b = pl.program_id(0)
        i = pl.program_id(1)

        @pl.when((b == 0) & (i == 0))
        def _():
            db_ref[...] = jnp.zeros_like(db_ref)

        @pl.when(i == 0)
        def _():
            dkw[...] = jnp.zeros_like(dkw)
            dvw[...] = jnp.zeros_like(dvw)

        @pl.when(i < nb)
        def _():
            masks = _head_masks()
            valid = lax.broadcasted_iota(jnp.int32, (QB, KW), 1) >= (2 - i) * QB
            for pr in range(ATT_HEADS // 2):
                ls = slice(128 * pr, 128 * (pr + 1))
                q2 = q_ref[:, ls]
                do2 = do_ref[:, ls]
                k2 = jnp.concatenate([k0[:, ls], k1[:, ls], k2r[:, ls]], axis=0)
                v2 = jnp.concatenate([v0[:, ls], v1[:, ls], v2r[:, ls]], axis=0)
                dq2 = jnp.zeros((QB, 128), F32)
                dk2 = jnp.zeros((KW, 128), F32)
                dv2 = jnp.zeros((KW, 128), F32)
                for hh in range(2):
                    h = 2 * pr + hh
                    qh = jnp.where(masks[hh], q2, 0)
                    doh = jnp.where(masks[hh], do2, 0)
                    p = _att_probs(qh, k2, bias_ref[h], valid)
                    dp = _dot(doh, v2, 'nt')
                    ds = p * (dp - jnp.sum(p * dp, axis=-1, keepdims=True))
                    db_ref[h] += ds
                    dq2 = dq2 + _dot(ds, jnp.where(masks[hh], k2, 0))
                    dk2 = dk2 + _dot(ds, qh, 'tn')
                    dv2 = dv2 + _dot(p, doh, 'tn')
                dq_ref[:, ls] = (dq2 * ATT_SCALE).astype(BF16)
                dkw[:, ls] += dk2 * ATT_SCALE
                dvw[:, ls] += dv2

        dk_ref[...] = dkw[0:QB, :].astype(BF16)
        dv_ref[...] = dvw[0:QB, :].astype(BF16)
        for buf in (dkw, dvw):
            rest = buf[QB:KW, :]
            buf[0:KW - QB, :] = rest
            buf[KW - QB:KW, :] = jnp.zeros((QB, ATT_WIDTH), F32)

    blk = jax.ShapeDtypeStruct((Bl, S, ATT_WIDTH), BF16)
    return pl.pallas_call(
        body, name="att_bwd", grid=(Bl, nb + 2),
        in_specs=[q_spec] + k_specs + v_specs + [do_spec, bias_spec],
        out_specs=[do_spec, kv_out, kv_out, bias_spec],
        out_shape=[blk, blk, blk, jax.ShapeDtypeStruct((ATT_HEADS, QB, KW), F32)],
        scratch_shapes=[pltpu.VMEM((KW, ATT_WIDTH), F32), pltpu.VMEM((KW, ATT_WIDTH), F32)],
        compiler_params=_params(("arbitrary", "arbitrary")))(*([proj3] * 7), do, bias)


def _rel_bias_grad(dbias):
    H = dbias.shape[0]
    Wd = KW + QB
    padded = jnp.pad(dbias, ((0, 0), (0, 1), (QB, 0)))
    skew = padded.reshape(H, (QB + 1) * Wd)[:, :QB * (Wd + 1)].reshape(H, QB, Wd + 1)[:, :, :Wd]
    c = jnp.arange(Wd)[:, None]
    bins = (jnp.clip(KW - c, -(CHUNK - 1), MAX_REL) + (CHUNK - 1) == jnp.arange(N_REL)[None, :]).astype(F32)

    def body(s_ref, bins_ref, o_ref):
        col = jnp.sum(s_ref[...], axis=1)
        o_ref[...] = jnp.dot(col, bins_ref[...], preferred_element_type=F32, precision=lax.Precision.HIGHEST)

    return pl.pallas_call(
        body, name="rel_bias_grad", grid=(1,),
        in_specs=[pl.BlockSpec((H, QB, Wd), lambda i: (0, 0, 0)), pl.BlockSpec((Wd, N_REL), lambda i: (0, 0))],
        out_specs=pl.BlockSpec((H, N_REL), lambda i: (0, 0)), out_shape=jax.ShapeDtypeStruct((H, N_REL), F32),
        compiler_params=_params(("arbitrary",)))(skew, bins)


MEM_TILE = 512


def _mem_probs(qh, kh):
    s = _dot(qh, kh, 'nt') * MEM_SCALE
    e = jnp.exp(s - jnp.max(s, axis=-1, keepdims=True))
    return e * (1.0 / jnp.sum(e, axis=-1, keepdims=True))


def _mem_fwd(proj3, kv3):
    Bl, S, _ = proj3.shape
    tq = _pick(S, (MEM_TILE, 256))
    hd = MEM_HEAD_DIM

    def body(q_ref, kv_ref, o_ref):
        for h in range(MEM_HEADS):
            p = _mem_probs(q_ref[:, h * hd:(h + 1) * hd], kv_ref[:, h * hd:(h + 1) * hd])
            o_ref[:, h * hd:(h + 1) * hd] = _dot(p, kv_ref[:, MEM_WIDTH + h * hd:MEM_WIDTH + (h + 1) * hd]).astype(BF16)

    return pl.pallas_call(
        body, name="mem_fwd", grid=(Bl, S // tq),
        in_specs=[pl.BlockSpec((None, tq, MEM_WIDTH), lambda b, i: (b, i, 5)),
                  pl.BlockSpec((None, MEM_LEN, 2 * MEM_WIDTH), lambda b, i: (b, 0, 0))],
        out_specs=pl.BlockSpec((None, tq, MEM_WIDTH), lambda b, i: (b, i, 0)),
        out_shape=jax.ShapeDtypeStruct((Bl, S, MEM_WIDTH), BF16),
        compiler_params=_params(("parallel", "parallel")))(proj3, kv3)


def _mem_bwd(proj3, kv3, do):
    Bl, S, _ = proj3.shape
    tq = _pick(S, (MEM_TILE, 256))
    hd = MEM_HEAD_DIM

    def body(q_ref, kv_ref, do_ref, dq_ref, dkv_ref):
        i = pl.program_id(1)

        @pl.when(i == 0)
        def _():
            dkv_ref[...] = jnp.zeros_like(dkv_ref)

        for h in range(MEM_HEADS):
            ks = slice(h * hd, (h + 1) * hd)
            vs = slice(MEM_WIDTH + h * hd, MEM_WIDTH + (h + 1) * hd)
            qh, kh, vh, doh = q_ref[:, ks], kv_ref[:, ks], kv_ref[:, vs], do_ref[:, ks]
            p = _mem_probs(qh, kh)
            dp = _dot(doh, vh, 'nt')
            ds = p * (dp - jnp.sum(p * dp, axis=-1, keepdims=True))
            dq_ref[:, ks] = (_dot(ds, kh) * MEM_SCALE).astype(BF16)
            dkv_ref[:, ks] += _dot(ds, qh, 'tn') * MEM_SCALE
            dkv_ref[:, vs] += _dot(p, doh, 'tn')

    return pl.pallas_call(
        body, name="mem_bwd", grid=(Bl, S // tq),
        in_specs=[pl.BlockSpec((None, tq, MEM_WIDTH), lambda b, i: (b, i, 5)),
                  pl.BlockSpec((None, MEM_LEN, 2 * MEM_WIDTH), lambda b, i: (b, 0, 0)),
                  pl.BlockSpec((None, tq, MEM_WIDTH), lambda b, i: (b, i, 0))],
        out_specs=[pl.BlockSpec((None, tq, MEM_WIDTH), lambda b, i: (b, i, 0)),
                   pl.BlockSpec((None, MEM_LEN, 2 * MEM_WIDTH), lambda b, i: (b, 0, 0))],
        out_shape=[jax.ShapeDtypeStruct((Bl, S, MEM_WIDTH), BF16),
                   jax.ShapeDtypeStruct((Bl, MEM_LEN, 2 * MEM_WIDTH), F32)],
        compiler_params=_params(("parallel", "arbitrary")))(proj3, kv3, do)


def _position():
    x, y, c = lax.axis_index("x"), lax.axis_index("y"), lax.axis_index("c")
    return x, y, c, 4 * x + 2 * y + c


def _device(idx):
    return ((idx >> 2) & 1, (idx >> 1) & 1, idx & 1)


def _half_block(ref, axis, shard_shape, k, h):
    R, Cn = shard_shape
    if axis == 1:
        return ref.at[pl.ds(h * (R // 2), R // 2), pl.ds(k * Cn, Cn)]
    return ref.at[pl.ds(k * R + h * (R // 2), R // 2), :]


def _block(ref, axis, shard_shape, k):
    R, Cn = shard_shape
    if axis == 1:
        return ref.at[:, pl.ds(k * Cn, Cn)]
    return ref.at[pl.ds(k * R, R), :]


def _half(ref, h):
    R = ref.shape[0]
    return ref.at[pl.ds(h * (R // 2), R // 2), :]


ANY = pl.BlockSpec(memory_space=pl.ANY)


HBM = pl.BlockSpec(memory_space=pltpu.HBM)
SEM = pl.BlockSpec(memory_space=pltpu.SEMAPHORE)
VMEM_WHOLE = pl.BlockSpec(memory_space=pltpu.VMEM)
EFFECT = pltpu.SideEffectType.DATAFLOW_SIDE_EFFECTING


def _in_hbm(a):
    return pltpu.with_memory_space_constraint(a, pltpu.HBM)


def _split_start(body, name, sources, lands, n_copies):
    n = len(sources)
    out_shape, out_specs = [], []
    for _ in range(n):
        out_shape += [pltpu.SemaphoreType.DMA((n_copies,)), pltpu.SemaphoreType.DMA((n_copies,))]
        out_specs += [SEM, SEM]
    out_shape += [pltpu.HBM(a.shape, a.dtype) for a in list(sources) + list(lands)]
    out_specs += [HBM] * (2 * n)
    out_shape.append(jax.ShapeDtypeStruct((8, 128), F32))
    out_specs.append(VMEM_WHOLE)

    def call_body(*refs):
        srcs, lnds = refs[:n], refs[n:2 * n]
        sems = refs[2 * n:4 * n]
        token = refs[-1]
        body(srcs, lnds, sems[0::2], sems[1::2])
        token[...] = jnp.zeros_like(token)

    res = pl.pallas_call(
        call_body, name=name, in_specs=[HBM] * (2 * n), out_specs=out_specs, out_shape=out_shape,
        input_output_aliases={i: 2 * n + i for i in range(2 * n)},
        compiler_params=pltpu.CompilerParams(has_side_effects=EFFECT))(
            *[_in_hbm(a) for a in list(sources) + list(lands)])
    pairs = [(res[2 * w], res[2 * w + 1], res[2 * n + w], res[3 * n + w]) for w in range(n)]
    return pairs, res[-1]


def _split_wait(body, name, pairs, after):
    n = len(pairs)

    def call_body(*refs):
        srcs, lnds = refs[:n], refs[n:2 * n]
        sems = refs[2 * n:4 * n]
        body(srcs, lnds, sems[0::2], sems[1::2])

    args = [_in_hbm(p[2]) for p in pairs] + [_in_hbm(p[3]) for p in pairs]
    for p in pairs:
        args += [p[0], p[1]]
    res = pl.pallas_call(
        call_body, name=name, in_specs=[HBM] * (2 * n) + [SEM] * (2 * n) + [ANY], out_specs=[HBM] * (2 * n),
        out_shape=[pltpu.HBM(a.shape, a.dtype) for a in args[:2 * n]],
        input_output_aliases={i: i for i in range(2 * n)},
        compiler_params=pltpu.CompilerParams(has_side_effects=EFFECT))(*args, after)
    return res[:n], res[n:]


def _gather_copy(srcs, lnds, send, recv, axes, shapes, w, j, me):
    chip = me >> 1
    return (pltpu.make_async_remote_copy(
        src_ref=srcs[w], dst_ref=_block(lnds[w], axes[w], shapes[w], chip), send_sem=send[w].at[j],
        recv_sem=recv[w].at[j], device_id=_device(me ^ (2 * (j + 1))), device_id_type=MESH),
            pltpu.make_async_remote_copy(
        src_ref=srcs[w], dst_ref=_block(lnds[w], axes[w], shapes[w], chip ^ (j + 1)), send_sem=send[w].at[j],
        recv_sem=recv[w].at[j], device_id=_device(me ^ (2 * (j + 1))), device_id_type=MESH))


def _gather_start(shards, lands, axes):
    shapes = [s.shape for s in shards]

    def body(srcs, lnds, send, recv):
        x, y, c, me = _position()
        for w in range(len(shards)):
            for j in range(3):
                _gather_copy(srcs, lnds, send, recv, axes, shapes, w, j, me)[0].start()

    return _split_start(body, "gather_start", shards, lands, 3)


def _gather_wait(pairs, axes, after, name):
    shapes = [p[2].shape for p in pairs]

    def body(srcs, lnds, send, recv):
        x, y, c, me = _position()
        for w in range(len(pairs)):
            for j in range(3):
                sent, landed = _gather_copy(srcs, lnds, send, recv, axes, shapes, w, j, me)
                sent.wait_send()
                landed.wait_recv()

    return _split_wait(body, name, pairs, after)[1]


def _shard_shape(grad, axis):
    return (grad.shape[0], grad.shape[1] // 4) if axis == 1 else (grad.shape[0] // 4, grad.shape[1])


def _scatter_copy(srcs, lnds, send, recv, axes, shapes, w, m, me):
    peer = me ^ m
    return pltpu.make_async_remote_copy(
        src_ref=_half_block(srcs[w], axes[w], shapes[w], peer >> 1, peer & 1), dst_ref=lnds[w].at[m - 1],
        send_sem=send[w].at[m - 1], recv_sem=recv[w].at[m - 1], device_id=_device(peer), device_id_type=MESH)


def _scatter_start(grads, axes, name):
    shapes = [_shard_shape(g, a) for g, a in zip(grads, axes)]
    lands = [lax.empty((N_DEV - 1, R // 2, Cn), BF16) for R, Cn in shapes]

    def body(srcs, lnds, send, recv):
        x, y, c, me = _position()
        for w in range(len(grads)):
            for m in range(1, N_DEV):
                _scatter_copy(srcs, lnds, send, recv, axes, shapes, w, m, me).start()

    return _split_start(body, name, grads, lands, N_DEV - 1)


def _scatter_wait(pairs, axes, after):
    shapes = [_shard_shape(p[2], a) for p, a in zip(pairs, axes)]

    def body(srcs, lnds, send, recv):
        x, y, c, me = _position()
        for w in range(len(pairs)):
            for m in range(1, N_DEV):
                cp = _scatter_copy(srcs, lnds, send, recv, axes, shapes, w, m, me)
                cp.wait_send()
                cp.wait_recv()

    return _split_wait(body, "scatter_wait", pairs, after)


def _sum_partials(own, parts, half, name):
    R, Cn = own.shape
    tr = _pick(R, (256, 176, 128, 64, 32, 16, 8))
    nblk = R // tr

    def body(half_ref, own_ref, p_ref, o_ref):
        acc = own_ref[...].astype(F32)
        for d in range(N_DEV - 1):
            acc = acc + p_ref[d].astype(F32)
        o_ref[...] = acc

    return pl.pallas_call(
        body, name=name,
        grid_spec=pltpu.PrefetchScalarGridSpec(
            num_scalar_prefetch=1, grid=(nblk,),
            in_specs=[pl.BlockSpec((tr, Cn), lambda i, hr: (i, 0)),
                      pl.BlockSpec((N_DEV - 1, tr, Cn), lambda i, hr: (0, i, 0))],
            out_specs=pl.BlockSpec((tr, Cn), lambda i, hr: (hr[0] * nblk + i, 0))),
        out_shape=jax.ShapeDtypeStruct((2 * R, Cn), F32),
        compiler_params=_params(("parallel",)))(half, own, parts)


def _exchange_halves(grads):
    n = len(grads)

    def body(*refs):
        outs = refs[n:2 * n]
        send, recv = refs[2 * n:]
        x, y, c, me = _position()

        def copy(w, half):
            rows = _half(outs[w], half)
            return pltpu.make_async_remote_copy(src_ref=rows, dst_ref=rows, send_sem=send.at[w],
                                                recv_sem=recv.at[w], device_id=_device(me ^ 1), device_id_type=MESH)

        for w in range(n):
            copy(w, c).start()
        for w in range(n):
            copy(w, 1 - c).wait_recv()
        for w in range(n):
            copy(w, c).wait_send()

    return pl.pallas_call(
        body, name="exchange_halves", in_specs=[ANY] * n, out_specs=[ANY] * n,
        out_shape=[jax.ShapeDtypeStruct(a.shape, a.dtype) for a in grads],
        input_output_aliases={i: i for i in range(n)},
        scratch_shapes=[pltpu.SemaphoreType.DMA((n,)), pltpu.SemaphoreType.DMA((n,))],
        compiler_params=pltpu.CompilerParams(has_side_effects=True))(*grads)


def _all_reduce_small(vec):
    R, L = vec.shape

    def body(v_ref, o_ref, buf, send, recv):
        x, y, c, me = _position()
        buf[me] = v_ref[...]

        def copy(m, slot):
            return pltpu.make_async_remote_copy(src_ref=v_ref, dst_ref=buf.at[slot], send_sem=send.at[m - 1],
                                                recv_sem=recv.at[m - 1], device_id=_device(me ^ m),
                                                device_id_type=MESH)

        for m in range(1, N_DEV):
            copy(m, me).start()
        for m in range(1, N_DEV):
            copy(m, me ^ m).wait_recv()
        for m in range(1, N_DEV):
            copy(m, me).wait_send()
        acc = buf[0]
        for d in range(1, N_DEV):
            acc = acc + buf[d]
        o_ref[...] = acc

    vm = pl.BlockSpec(memory_space=pltpu.VMEM)
    return pl.pallas_call(
        body, name="all_reduce_small", in_specs=[vm], out_specs=vm, out_shape=jax.ShapeDtypeStruct((R, L), F32),
        scratch_shapes=[pltpu.VMEM((N_DEV, R, L), F32), pltpu.SemaphoreType.DMA((N_DEV - 1,)),
                        pltpu.SemaphoreType.DMA((N_DEV - 1,))],
        compiler_params=pltpu.CompilerParams(has_side_effects=True))(vec)


def _adamw(w, g, m, v, name):
    R, Cn = w.shape
    tr = _pick(R, (256, 176, 128, 64, 40, 32, 16, 8))

    def body(w_ref, g_ref, m_ref, v_ref, d_ref, nm_ref, nv_ref):
        gv = g_ref[...]
        nm = ADAM_B1 * m_ref[...] + (1.0 - ADAM_B1) * gv
        nv = ADAM_B2 * v_ref[...] + (1.0 - ADAM_B2) * (gv * gv)
        m_hat = nm / (1.0 - ADAM_B1 ** ADAM_STEP)
        v_hat = nv / (1.0 - ADAM_B2 ** ADAM_STEP)
        d_ref[...] = -ADAM_LR * (m_hat / (jnp.sqrt(v_hat) + ADAM_EPS) + ADAM_WD * w_ref[...])
        nm_ref[...] = nm
        nv_ref[...] = nv

    spec = pl.BlockSpec((tr, Cn), lambda i: (i, 0))
    return pl.pallas_call(
        body, name=name, grid=(R // tr,), in_specs=[spec] * 4, out_specs=[spec] * 3,
        out_shape=[jax.ShapeDtypeStruct((R, Cn), F32)] * 3, compiler_params=_params(("parallel",)))(w, g, m, v)


def _pack(arrays, rows):
    flat = jnp.concatenate([a.reshape(-1).astype(F32) for a in arrays])
    return jnp.pad(flat, (0, rows * 128 - flat.shape[0])).reshape(rows, 128)


def _unpack(packed, shapes):
    flat = packed.reshape(-1)
    out, off = [], 0
    for s in shapes:
        size = 1
        for d in s:
            size *= d
        out.append(flat[off:off + size].reshape(s))
        off += size
    return out


def _ffn_fwd(x, norm, arrived, tag, after=None):
    h = _rms_fwd(x, norm, f"{tag}_norm", after=after)
    w_up = arrived(f"{tag}_w_up", h)
    a, b, act = _ffn_up(h, w_up, f"{tag}_up")
    w_down = arrived(f"{tag}_w_down", act)
    out = _mm(act, w_down, 'nn', f"{tag}_down", F32, res=x, scale=0.5)
    return out, (h, a, b, act, w_up, w_down)


def _ffn_bwd(dout, x, norm, saved, tag, send):
    h, a, b, act, w_up, w_down = saved
    g_down = _mm(act, dout, 'tn', f"{tag}_down_dw", BF16, scale=0.5)
    token = send([f"{tag}_w_down"], [g_down])
    dact = _mm(dout, w_down, 'nt', f"{tag}_down_dx", BF16, scale=0.5, after=token)
    dup = _swiglu_bwd(dact, a, b, f"{tag}_swiglu_bwd")
    g_up = _mm(h, dup, 'tn', f"{tag}_up_dw", BF16)
    token = send([f"{tag}_w_up"], [g_up])
    dh = _mm(dup, w_up, 'nt', f"{tag}_up_dx", F32, after=token)
    dx, g_norm = _rms_bwd(x, norm, dh, dout, f"{tag}_norm_bwd")
    return dx, g_norm


def kernel(x, mem, ffn1_norm, ffn1_w_up, ffn1_w_down, mix_norm, mem_norm, w_in, b_gate, conv_dw_w, conv_dw_b, conv_ln_g, conv_ln_b, conv_w_pw, att_rel_bias, att_w_o, mem_w_kv, mem_w_o, w_out, ffn2_norm, ffn2_w_up, ffn2_w_down, final_norm, loss_target, m_ffn1_norm, m_ffn1_w_up, m_ffn1_w_down, m_mix_norm, m_mem_norm, m_w_in, m_b_gate, m_conv_dw_w, m_conv_dw_b, m_conv_ln_g, m_conv_ln_b, m_conv_w_pw, m_att_rel_bias, m_att_w_o, m_mem_w_kv, m_mem_w_o, m_w_out, m_ffn2_norm, m_ffn2_w_up, m_ffn2_w_down, m_final_norm, v_ffn1_norm, v_ffn1_w_up, v_ffn1_w_down, v_mix_norm, v_mem_norm, v_w_in, v_b_gate, v_conv_dw_w, v_conv_dw_b, v_conv_ln_g, v_conv_ln_b, v_conv_w_pw, v_att_rel_bias, v_att_w_o, v_mem_w_kv, v_mem_w_o, v_w_out, v_ffn2_norm, v_ffn2_w_up, v_ffn2_w_down, v_final_norm):
    given = dict(locals())
    wts = {n: given[n] for n in WEIGHTS}
    mom1 = {n: given["m_" + n] for n in WEIGHTS}
    mom2 = {n: given["v_" + n] for n in WEIGHTS}
    Bl, S, Dm = x.shape
    T = Bl * S
    x0 = x.reshape(T, Dm)
    tgt = loss_target.reshape(T, Dm)
    mem2 = mem.reshape(Bl * MEM_LEN, Dm)

    big_names = [n for n, _ in BIG]
    big_axes = [a for _, a in BIG]
    chip = 2 * lax.axis_index("x") + lax.axis_index("y")

    core = lax.axis_index("c")
    axis_of = dict(BIG)

    gather_groups = [['ffn1_w_up'], ['ffn1_w_down'], ['w_in', 'conv_dw_w'],
                     ['mem_w_kv', 'conv_w_pw', 'att_w_o', 'mem_w_o', 'w_out'], ['ffn2_w_up'], ['ffn2_w_down']]
    gather_names = [n for grp in gather_groups for n in grp]
    gather_axes = [axis_of.get(n, 1) for n in gather_names]
    shards = [jnp.pad(conv_dw_w[0], ((0, 1), (0, 0))) if n == 'conv_dw_w' else wts[n][0].astype(BF16)
              for n in gather_names]
    placed = []
    for sh, a in zip(shards, gather_axes):
        R, Cn = sh.shape
        zeros = jnp.zeros((R, 4 * Cn) if a == 1 else (4 * R, Cn), sh.dtype)
        placed.append(lax.dynamic_update_slice(zeros, sh, (0, chip * Cn) if a == 1 else (chip * R, 0)))
    in_flight, gather_token = _gather_start(shards, placed, gather_axes)
    in_flight = dict(zip(gather_names, in_flight))
    full = {}

    def arrived(name, after):
        if name not in full:
            grp = next(grp for grp in gather_groups if name in grp)
            lands = _gather_wait([in_flight[n] for n in grp], [axis_of.get(n, 1) for n in grp], after,
                                 f"gather_wait_{grp[0]}")
            full.update(zip(grp, lands))
        return full[name]

    scattering = {}

    def send(names, grads):
        pairs, token = _scatter_start(grads, [axis_of[n] for n in names], f"scatter_start_{names[0]}")
        scattering.update(zip(names, pairs))
        return token

    final_g = final_norm.reshape(1, Dm)
    bias = _att_bias(att_rel_bias[0])

    x1, ffn1_saved = _ffn_fwd(x0, ffn1_norm, arrived, "ffn1", after=gather_token)
    h = _rms_fwd(x1, mix_norm, "mix_norm")
    w_in_full = arrived('w_in', h)
    dw_full = full['conv_dw_w'][:CONV_KERNEL]
    proj = _mm(h, w_in_full, 'nn', "w_in", BF16)
    proj3 = proj.reshape(Bl, S, proj.shape[1])
    cv, c_act = _conv_fwd(proj3, dw_full, conv_dw_b, conv_ln_g, conv_ln_b)
    o_att = _att_fwd(proj3, bias)
    mem_h = _rms_fwd(mem2, mem_norm, "mem_norm")
    kv = _mm(mem_h, arrived('mem_w_kv', o_att), 'nn', "mem_kv", BF16)
    kv3 = kv.reshape(Bl, MEM_LEN, 2 * MEM_WIDTH)
    o_mem = _mem_fwd(proj3, kv3)
    c_act2, o_att2, o_mem2 = c_act.reshape(T, -1), o_att.reshape(T, -1), o_mem.reshape(T, -1)
    x2, yc, ya, ym, y = _mix_fwd(c_act2, o_att2, o_mem2, proj, b_gate, x1, full['conv_w_pw'], full['att_w_o'],
                                 full['mem_w_o'], full['w_out'])
    x3, ffn2_saved = _ffn_fwd(x2, ffn2_norm, arrived, "ffn2")
    dx3, g_final, loss_vec = _final_fwd_bwd(x3, tgt, final_g)

    g = {}
    dx2, g['ffn2_norm'] = _ffn_bwd(dx3, x2, ffn2_norm, ffn2_saved, "ffn2", send)
    dyc, dya, dym, dgl, g['b_gate'], dc, doa, dom = _mix_bwd(
        dx2, yc, ya, ym, proj, b_gate, full['conv_w_pw'], full['att_w_o'], full['mem_w_o'], full['w_out'])
    token = send(['w_out', 'conv_w_pw', 'att_w_o', 'mem_w_o'],
                 [_mm(y, dx2, 'tn', "w_out_dw", BF16), _mm(c_act2, dyc, 'tn', "conv_pw_dw", BF16),
                  _mm(o_att2, dya, 'tn', "att_o_dw", BF16), _mm(o_mem2, dym, 'tn', "mem_o_dw", BF16)])
    du, g_dw, g['conv_dw_b'], g['conv_ln_g'], g['conv_ln_b'] = _conv_bwd(
        proj3, cv, dc.reshape(Bl, S, -1), dw_full, conv_ln_g, conv_ln_b)
    dq, dk, dv, dbias = _att_bwd(proj3, doa.reshape(Bl, S, -1), bias)
    g['att_rel_bias'] = _rel_bias_grad(dbias)
    dmq, dkv = _mem_bwd(proj3, kv3, dom.reshape(Bl, S, -1))
    dkv2 = dkv.reshape(Bl * MEM_LEN, 2 * MEM_WIDTH)
    g_kv = _mm(mem_h, dkv2, 'tn', "mem_kv_dw", BF16, after=token)
    dmem_h = _mm(dkv2, full['mem_w_kv'], 'nt', "mem_kv_dx", F32)
    _, g['mem_norm'] = _rms_bwd(mem2, mem_norm, dmem_h, dmem_h, "mem_norm_bwd")
    dproj = jnp.concatenate([du.reshape(T, -1), dq.reshape(T, -1), dk.reshape(T, -1), dv.reshape(T, -1),
                             dmq.reshape(T, -1), dgl], axis=1)
    token = send(['mem_w_kv', 'w_in'], [g_kv, _mm(h, dproj, 'tn', "w_in_dw", BF16)])
    dh = _mm(dproj, w_in_full, 'nt', "w_in_dx", F32, after=token)
    dx1, g['mix_norm'] = _rms_bwd(x1, mix_norm, dh, dx2, "mix_norm_bwd")
    dx0, g['ffn1_norm'] = _ffn_bwd(dx1, x0, ffn1_norm, ffn1_saved, "ffn1", send)
    g['final_norm'] = g_final

    sent, landed = _scatter_wait([scattering[n] for n in big_names], big_axes, dx0)
    halves = []
    half_idx = core.reshape(1).astype(jnp.int32)
    for n, a, own_full, parts in zip(big_names, big_axes, sent, landed):
        R, Cn = _shard_shape(own_full, a)
        start = (core * (R // 2), chip * Cn) if a == 1 else (chip * R + core * (R // 2), 0)
        own = lax.dynamic_slice(own_full, start, (R // 2, Cn))
        halves.append(_sum_partials(own, parts, half_idx, f"sum_{n}"))
    for n, sg in zip(big_names, _exchange_halves(halves)):
        g[n] = sg

    small_shapes = [wts[n].shape for n in SMALL]
    n_small = sum(int(wts[n].size) for n in SMALL)
    n_red = n_small + CONV_KERNEL * CONV_WIDTH
    red = _all_reduce_small(_pack([g[n] for n in SMALL] + [g_dw], -(-n_red // 1024) * 8))
    red_list = _unpack(red, small_shapes + [(CONV_KERNEL, CONV_WIDTH)])
    for n, rg in zip(SMALL, red_list[:-1]):
        g[n] = rg
    dw_cols = conv_dw_w.shape[2]
    g['conv_dw_w'] = lax.dynamic_slice(red_list[-1], (0, chip * dw_cols), (CONV_KERNEL, dw_cols))[None]

    delta, new_m, new_v = {}, {}, {}
    for n in big_names:
        g[n] = g[n][None]
        d, nm, nv = _adamw(wts[n][0], g[n][0], mom1[n][0], mom2[n][0], f"adamw_{n}")
        delta[n], new_m[n], new_v[n] = d[None], nm[None], nv[None]
    rest = SMALL + ['conv_dw_w']
    rest_shapes = [wts[n].shape for n in rest]
    rows = -(-sum(int(wts[n].size) for n in rest) // 1024) * 8
    packed = [_pack([src[n] for n in rest], rows) for src in (wts, g, mom1, mom2)]
    for out, res in zip((delta, new_m, new_v), _adamw(*packed, "adamw_small")):
        for n, a in zip(rest, _unpack(res, rest_shapes)):
            out[n] = a

    loss = lax.psum(loss_vec[0, 0], ("x", "y", "c"))
    grad_x = dx0.reshape(Bl, S, Dm)
    return (loss, grad_x, *[g[n] for n in WEIGHTS], *[delta[n] for n in WEIGHTS],
            *[new_m[n] for n in WEIGHTS], *[new_v[n] for n in WEIGHTS])
```

```python
import jax
import jax.numpy as jnp
from jax import lax
from jax.experimental import pallas as pl
from jax.experimental.pallas import tpu as pltpu

F32 = jnp.float32
BF16 = jnp.bfloat16

D_MODEL = 1024
D_FF = 2816
CHUNK = 64
LEFT_CHUNKS = 8
MAX_REL = 128
N_REL = (CHUNK - 1) + MAX_REL + 1
CONV_WIDTH = 512
CONV_KERNEL = 31
ATT_HEADS = 8
ATT_WIDTH = 512
MEM_LEN = 256
MEM_HEADS = 4
MEM_HEAD_DIM = 128
MEM_WIDTH = 512
EPS = 1e-6
MASK_VALUE = -1e30
ATT_SCALE = 64 ** -0.5
MEM_SCALE = 128 ** -0.5

ADAM_LR = 0.001
ADAM_B1 = 0.9
ADAM_B2 = 0.999
ADAM_EPS = 1e-08
ADAM_WD = 0.01
ADAM_STEP = 10

QB = 256
KW = 3 * QB
CONV_PAD = 32
CONV_TILE = 256

VMEM_LIMIT = 56 << 20

WEIGHTS = ['ffn1_norm', 'ffn1_w_up', 'ffn1_w_down', 'mix_norm', 'mem_norm', 'w_in', 'b_gate', 'conv_dw_w',
           'conv_dw_b', 'conv_ln_g', 'conv_ln_b', 'conv_w_pw', 'att_rel_bias', 'att_w_o', 'mem_w_kv', 'mem_w_o',
           'w_out', 'ffn2_norm', 'ffn2_w_up', 'ffn2_w_down', 'final_norm']
BIG = [('ffn1_w_up', 1), ('ffn1_w_down', 0), ('w_in', 1), ('conv_w_pw', 1), ('att_w_o', 1), ('mem_w_kv', 0),
       ('mem_w_o', 1), ('w_out', 0), ('ffn2_w_up', 1), ('ffn2_w_down', 0)]
SMALL = ['ffn1_norm', 'mix_norm', 'mem_norm', 'b_gate', 'conv_dw_b', 'conv_ln_g', 'conv_ln_b', 'att_rel_bias',
         'ffn2_norm', 'final_norm']
N_CHIPS = 4
N_DEV = 8
MESH = pl.DeviceIdType.MESH


def _pick(n, cands):
    for c in cands:
        if n % c == 0:
            return c
    return n


def _sig(x):
    return 1.0 / (1.0 + jnp.exp(-x))


def _params(sem=None, vmem=VMEM_LIMIT):
    return pltpu.CompilerParams(dimension_semantics=sem, vmem_limit_bytes=vmem)


def _dot(a, b, mode='nn'):
    dims = {'nn': (((1,), (0,)), ((), ())), 'nt': (((1,), (1,)), ((), ())), 'tn': (((0,), (0,)), ((), ()))}[mode]
    return lax.dot_general(a.astype(BF16), b.astype(BF16), dims, preferred_element_type=F32)


def _mm(a, b, mode, name, out_dtype, res=None, scale=1.0, after=None):
    if mode == 'nn':
        (M, C), (_, N) = a.shape, b.shape
    elif mode == 'nt':
        (M, C), (N, _) = a.shape, b.shape
    else:
        (C, M), (_, N) = a.shape, b.shape
    tm = _pick(M, (1024, 1408, 512, 256, 128))
    tn = _pick(N, (1024, 1408, 512, 256, 128))
    tc = _pick(C, (1024, 1408, 512, 256, 128))
    nk = C // tc
    if mode == 'nn':
        a_spec = pl.BlockSpec((tm, tc), lambda i, j, k: (i, k))
        b_spec = pl.BlockSpec((tc, tn), lambda i, j, k: (k, j))
    elif mode == 'nt':
        a_spec = pl.BlockSpec((tm, tc), lambda i, j, k: (i, k))
        b_spec = pl.BlockSpec((tn, tc), lambda i, j, k: (j, k))
    else:
        a_spec = pl.BlockSpec((tc, tm), lambda i, j, k: (k, i))
        b_spec = pl.BlockSpec((tc, tn), lambda i, j, k: (k, j))
    o_spec = pl.BlockSpec((tm, tn), lambda i, j, k: (i, j))
    has_res = res is not None
    has_after = after is not None

    def body(*refs):
        a_ref, b_ref = refs[:2]
        r_ref = refs[2] if has_res else None
        o_ref, acc_ref = refs[-2:]
        k = pl.program_id(2)
        p = _dot(a_ref[...], b_ref[...], mode)

        def finish(acc):
            if scale != 1.0:
                acc = acc * scale
            if r_ref is not None:
                acc = r_ref[...] + acc
            o_ref[...] = acc.astype(o_ref.dtype)

        if nk == 1:
            finish(p)
        else:
            @pl.when(k == 0)
            def _():
                acc_ref[...] = p

            @pl.when(k > 0)
            def _():
                acc_ref[...] += p

            @pl.when(k == nk - 1)
            def _():
                finish(acc_ref[...])

    in_specs = [a_spec, b_spec] + ([o_spec] if has_res else []) + ([ANY] if has_after else [])
    args = (a, b) + ((res,) if has_res else ()) + ((after,) if has_after else ())
    acc_shape = (tm, tn) if nk > 1 else (8, 128)
    return pl.pallas_call(
        body, name=name, grid=(M // tm, N // tn, nk), in_specs=in_specs, out_specs=o_spec,
        out_shape=jax.ShapeDtypeStruct((M, N), out_dtype), scratch_shapes=[pltpu.VMEM(acc_shape, F32)],
        compiler_params=_params(("parallel", "parallel", "arbitrary")))(*args)


def _row_tile(T):
    return _pick(T, (512, 256, 128, 64, 32, 16, 8))


def _rms_fwd(x, g, name, after=None):
    T, Dm = x.shape
    tm = _row_tile(T)

    def body(x_ref, g_ref, *rest):
        o_ref = rest[-1]
        xv = x_ref[...]
        r = lax.rsqrt(jnp.mean(xv * xv, axis=-1, keepdims=True) + EPS)
        o_ref[...] = ((xv * r) * g_ref[...]).astype(o_ref.dtype)

    extra = () if after is None else (after,)
    return pl.pallas_call(
        body, name=name, grid=(T // tm,),
        in_specs=[pl.BlockSpec((tm, Dm), lambda i: (i, 0)), pl.BlockSpec((1, Dm), lambda i: (0, 0))]
        + [ANY] * len(extra),
        out_specs=pl.BlockSpec((tm, Dm), lambda i: (i, 0)), out_shape=jax.ShapeDtypeStruct((T, Dm), BF16),
        compiler_params=_params(("parallel",)))(x, g, *extra)


def _rms_bwd(x, g, dh, dres, name):
    T, Dm = x.shape
    tm = _row_tile(T)

    def body(x_ref, g_ref, dh_ref, dr_ref, dx_ref, dg_ref):
        i = pl.program_id(0)
        xv = x_ref[...]
        r = lax.rsqrt(jnp.mean(xv * xv, axis=-1, keepdims=True) + EPS)
        xr = xv * r
        dh_v = dh_ref[...].astype(F32)
        dyg = dh_v * g_ref[...]
        dx = r * (dyg - xr * jnp.mean(dyg * xr, axis=-1, keepdims=True))
        dx_ref[...] = dr_ref[...] + dx

        @pl.when(i == 0)
        def _():
            dg_ref[...] = jnp.zeros_like(dg_ref)

        dg_ref[...] += jnp.sum(dh_v * xr, axis=0, keepdims=True)

    row = pl.BlockSpec((tm, Dm), lambda i: (i, 0))
    vec = pl.BlockSpec((1, Dm), lambda i: (0, 0))
    return pl.pallas_call(
        body, name=name, grid=(T // tm,), in_specs=[row, vec, row, row], out_specs=[row, vec],
        out_shape=[jax.ShapeDtypeStruct((T, Dm), F32), jax.ShapeDtypeStruct((1, Dm), F32)],
        compiler_params=_params(("arbitrary",)))(x, g, dh, dres)


def _final_fwd_bwd(x3, tgt, g):
    T, Dm = x3.shape
    tm = _row_tile(T)

    def body(x_ref, t_ref, g_ref, dx_ref, dg_ref, loss_ref):
        i = pl.program_id(0)
        xv = x_ref[...]
        gg = g_ref[...]
        r = lax.rsqrt(jnp.mean(xv * xv, axis=-1, keepdims=True) + EPS)
        xr = xv * r
        err = xr * gg - t_ref[...]
        dout = err * (1.0 / Dm)
        dyg = dout * gg
        dx_ref[...] = r * (dyg - xr * jnp.mean(dyg * xr, axis=-1, keepdims=True))

        @pl.when(i == 0)
        def _():
            dg_ref[...] = jnp.zeros_like(dg_ref)
            loss_ref[...] = jnp.zeros_like(loss_ref)

        dg_ref[...] += jnp.sum(dout * xr, axis=0, keepdims=True)
        loss_ref[...] += jnp.zeros_like(loss_ref) + (0.5 / Dm) * jnp.sum(err * err)

    row = pl.BlockSpec((tm, Dm), lambda i: (i, 0))
    vec = pl.BlockSpec((1, Dm), lambda i: (0, 0))
    one = pl.BlockSpec((1, 128), lambda i: (0, 0))
    return pl.pallas_call(
        body, name="final_fwd_bwd", grid=(T // tm,), in_specs=[row, row, vec], out_specs=[row, vec, one],
        out_shape=[jax.ShapeDtypeStruct((T, Dm), F32), jax.ShapeDtypeStruct((1, Dm), F32),
                   jax.ShapeDtypeStruct((1, 128), F32)],
        compiler_params=_params(("arbitrary",)))(x3, tgt, g)


def _ffn_up(h, w_up, name):
    T, K = h.shape
    Fh = w_up.shape[1] // 2
    tm = _pick(T, (512, 256, 128))
    tn = _pick(Fh, (1408, 512, 256, 128))
    nj = Fh // tn

    def body(h_ref, wa_ref, wb_ref, a_ref, b_ref, act_ref):
        hv = h_ref[...]
        a = _dot(hv, wa_ref[...])
        b = _dot(hv, wb_ref[...])
        a_ref[...] = a.astype(BF16)
        b_ref[...] = b.astype(BF16)
        act_ref[...] = (a * _sig(a) * b).astype(BF16)

    out = pl.BlockSpec((tm, tn), lambda i, j: (i, j))
    return pl.pallas_call(
        body, name=name, grid=(T // tm, nj),
        in_specs=[pl.BlockSpec((tm, K), lambda i, j: (i, 0)), pl.BlockSpec((K, tn), lambda i, j: (0, j)),
                  pl.BlockSpec((K, tn), lambda i, j: (0, j + nj))],
        out_specs=[out, out, out], out_shape=[jax.ShapeDtypeStruct((T, Fh), BF16)] * 3,
        compiler_params=_params(("parallel", "parallel")))(h, w_up, w_up)


def _ffn_down_bwd(dout, w_down, a, b, name, after=None):
    T, Dm = dout.shape
    Fh = w_down.shape[0]
    tm = _pick(T, (512, 256, 128))
    tn = _pick(Fh, (1408, 512, 256, 128))

    def body(d_ref, w_ref, a_ref, b_ref, *rest):
        da_ref, db_ref = rest[-2:]
        dact = _dot(d_ref[...], w_ref[...], 'nt') * 0.5
        av = a_ref[...].astype(F32)
        bv = b_ref[...].astype(F32)
        s = _sig(av)
        da_ref[...] = (dact * bv * s * (1.0 + av * (1.0 - s))).astype(BF16)
        db_ref[...] = (dact * av * s).astype(BF16)

    tile = pl.BlockSpec((tm, tn), lambda i, j: (i, j))
    extra = () if after is None else (after,)
    return pl.pallas_call(
        body, name=name, grid=(T // tm, Fh // tn),
        in_specs=[pl.BlockSpec((tm, Dm), lambda i, j: (i, 0)), pl.BlockSpec((tn, Dm), lambda i, j: (j, 0)),
                  tile, tile] + [ANY] * len(extra),
        out_specs=[tile, tile], out_shape=[jax.ShapeDtypeStruct((T, Fh), BF16)] * 2,
        compiler_params=_params(("parallel", "parallel")))(dout, w_down, a, b, *extra)


def _dx_rms_bwd(pieces, w, x, g, dres, name, after=None):
    T, Dm = x.shape
    width = pieces[0].shape[1]
    tm = _pick(T, (512, 256, 128))
    tc = _pick(width, (1408, 1024, 512, 256, 128))
    per = width // tc
    nk = per * len(pieces)
    npc = len(pieces)

    def body(*refs):
        p_refs = refs[:npc]
        w_ref, x_ref, g_ref, dr_ref = refs[npc:npc + 4]
        dx_ref, dg_ref, acc_ref = refs[-3:]
        i = pl.program_id(0)
        k = pl.program_id(1)
        lhs = p_refs[0][...]
        for p in range(1, npc):
            lhs = jnp.where(k >= p * per, p_refs[p][...], lhs)
        part = _dot(lhs, w_ref[...], 'nt')

        @pl.when(k == 0)
        def _():
            acc_ref[...] = part

        @pl.when(k > 0)
        def _():
            acc_ref[...] += part

        @pl.when((i == 0) & (k == 0))
        def _():
            dg_ref[...] = jnp.zeros_like(dg_ref)

        @pl.when(k == nk - 1)
        def _():
            dh = acc_ref[...]
            xv = x_ref[...]
            r = lax.rsqrt(jnp.mean(xv * xv, axis=-1, keepdims=True) + EPS)
            xr = xv * r
            dyg = dh * g_ref[...]
            dx_ref[...] = dr_ref[...] + r * (dyg - xr * jnp.mean(dyg * xr, axis=-1, keepdims=True))
            dg_ref[...] += jnp.sum(dh * xr, axis=0, keepdims=True)

    def piece_spec(p):
        return pl.BlockSpec((tm, tc), lambda i, k: (i, jnp.clip(k - p * per, 0, per - 1)))

    row = pl.BlockSpec((tm, Dm), lambda i, k: (i, 0))
    vec = pl.BlockSpec((1, Dm), lambda i, k: (0, 0))
    extra = () if after is None else (after,)
    return pl.pallas_call(
        body, name=name, grid=(T // tm, nk),
        in_specs=[piece_spec(p) for p in range(npc)] + [pl.BlockSpec((Dm, tc), lambda i, k: (0, k)), row, vec, row]
        + [ANY] * len(extra),
        out_specs=[row, vec], out_shape=[jax.ShapeDtypeStruct((T, Dm), F32), jax.ShapeDtypeStruct((1, Dm), F32)],
        scratch_shapes=[pltpu.VMEM((tm, Dm), F32)],
        compiler_params=_params(("arbitrary", "arbitrary")))(*pieces, w, x, g, dres, *extra)


def _dw_pieces(a, pieces, name):
    C, M = a.shape
    width = pieces[0].shape[1]
    npc = len(pieces)
    tm = _pick(M, (1024, 512, 256, 128))
    tn = _pick(width, (1408, 1024, 512, 256, 128))
    tc = _pick(C, (1024, 512, 256, 128))
    per = width // tn
    nk = C // tc

    def body(*refs):
        a_ref = refs[0]
        p_refs = refs[1:1 + npc]
        o_ref, acc_ref = refs[-2:]
        j = pl.program_id(1)
        k = pl.program_id(2)
        rhs = p_refs[0][...]
        for p in range(1, npc):
            rhs = jnp.where(j >= p * per, p_refs[p][...], rhs)
        part = _dot(a_ref[...], rhs, 'tn')

        @pl.when(k == 0)
        def _():
            acc_ref[...] = part

        @pl.when(k > 0)
        def _():
            acc_ref[...] += part

        @pl.when(k == nk - 1)
        def _():
            o_ref[...] = acc_ref[...].astype(o_ref.dtype)

    def piece_spec(p):
        return pl.BlockSpec((tc, tn), lambda i, j, k: (k, jnp.clip(j - p * per, 0, per - 1)))

    return pl.pallas_call(
        body, name=name, grid=(M // tm, per * npc, nk),
        in_specs=[pl.BlockSpec((tc, tm), lambda i, j, k: (k, i))] + [piece_spec(p) for p in range(npc)],
        out_specs=pl.BlockSpec((tm, tn), lambda i, j, k: (i, j)),
        out_shape=jax.ShapeDtypeStruct((M, width * npc), BF16), scratch_shapes=[pltpu.VMEM((tm, tn), F32)],
        compiler_params=_params(("parallel", "parallel", "arbitrary")))(a, *pieces)


def _mix_fwd(c_act, o_att, o_mem, proj, b_gate, x1, w_pw, w_o, w_mo, w_out):
    T, Dm = x1.shape
    W = c_act.shape[1]
    tm = _pick(T, (256, 128, 64, 32, 16, 8))

    def body(c_ref, oa_ref, om_ref, gl_ref, bg_ref, x1_ref, wpw_ref, wo_ref, wmo_ref, wout_ref,
             x2_ref, yc_ref, ya_ref, ym_ref, y_ref):
        yc = _dot(c_ref[...], wpw_ref[...])
        ya = _dot(oa_ref[...], wo_ref[...])
        ym = _dot(om_ref[...], wmo_ref[...])
        g = _sig(gl_ref[...].astype(F32) + bg_ref[...])
        y = g[:, :Dm] * yc + g[:, Dm:2 * Dm] * ya + g[:, 2 * Dm:] * ym
        x2_ref[...] = x1_ref[...] + _dot(y, wout_ref[...])
        yc_ref[...] = yc.astype(BF16)
        ya_ref[...] = ya.astype(BF16)
        ym_ref[...] = ym.astype(BF16)
        y_ref[...] = y.astype(BF16)

    rowW = pl.BlockSpec((tm, W), lambda i: (i, 0))
    rowD = pl.BlockSpec((tm, Dm), lambda i: (i, 0))
    full = lambda s: pl.BlockSpec(s, lambda i: (0, 0))
    return pl.pallas_call(
        body, name="mix_fwd", grid=(T // tm,),
        in_specs=[rowW, rowW, rowW, pl.BlockSpec((tm, 3 * Dm), lambda i: (i, 1)), full((1, 3 * Dm)), rowD,
                  full((W, Dm)), full((W, Dm)), full((W, Dm)), full((Dm, Dm))],
        out_specs=[rowD] * 5,
        out_shape=[jax.ShapeDtypeStruct((T, Dm), F32)] + [jax.ShapeDtypeStruct((T, Dm), BF16)] * 4,
        compiler_params=_params(("parallel",)))(c_act, o_att, o_mem, proj, b_gate, x1, w_pw, w_o, w_mo, w_out)


def _mix_bwd(dx2, yc, ya, ym, proj, b_gate, w_pw, w_o, w_mo, w_out):
    T, Dm = dx2.shape
    W = w_pw.shape[0]
    tm = _pick(T, (256, 128, 64, 32, 16, 8))

    def body(dx_ref, yc_ref, ya_ref, ym_ref, gl_ref, bg_ref, wpw_ref, wo_ref, wmo_ref, wout_ref,
             dyc_ref, dya_ref, dym_ref, dgl_ref, dbg_ref, dc_ref, doa_ref, dom_ref):
        i = pl.program_id(0)

        @pl.when(i == 0)
        def _():
            dbg_ref[...] = jnp.zeros_like(dbg_ref)

        dy = _dot(dx_ref[...], wout_ref[...], 'nt')
        g = _sig(gl_ref[...].astype(F32) + bg_ref[...])
        branches = ((yc_ref, dyc_ref, wpw_ref, dc_ref), (ya_ref, dya_ref, wo_ref, doa_ref),
                    (ym_ref, dym_ref, wmo_ref, dom_ref))
        for n, (y_ref, dyk_ref, w_ref, dk_ref) in enumerate(branches):
            gk = g[:, n * Dm:(n + 1) * Dm]
            dyk = dy * gk
            dgl = dyk * y_ref[...].astype(F32) * (1.0 - gk)
            dyk_ref[...] = dyk.astype(BF16)
            dgl_ref[:, n * Dm:(n + 1) * Dm] = dgl.astype(BF16)
            dbg_ref[:, n * Dm:(n + 1) * Dm] += jnp.sum(dgl, axis=0, keepdims=True)
            dk_ref[...] = _dot(dyk, w_ref[...], 'nt').astype(BF16)

    rowW = pl.BlockSpec((tm, W), lambda i: (i, 0))
    rowD = pl.BlockSpec((tm, Dm), lambda i: (i, 0))
    row3 = pl.BlockSpec((tm, 3 * Dm), lambda i: (i, 0))
    full = lambda s: pl.BlockSpec(s, lambda i: (0, 0))
    return pl.pallas_call(
        body, name="mix_bwd", grid=(T // tm,),
        in_specs=[rowD, rowD, rowD, rowD, pl.BlockSpec((tm, 3 * Dm), lambda i: (i, 1)), full((1, 3 * Dm)),
                  full((W, Dm)), full((W, Dm)), full((W, Dm)), full((Dm, Dm))],
        out_specs=[rowD, rowD, rowD, row3, full((1, 3 * Dm)), rowW, rowW, rowW],
        out_shape=[jax.ShapeDtypeStruct((T, Dm), BF16)] * 3 + [jax.ShapeDtypeStruct((T, 3 * Dm), BF16),
                                                                jax.ShapeDtypeStruct((1, 3 * Dm), F32)]
        + [jax.ShapeDtypeStruct((T, W), BF16)] * 3,
        compiler_params=_params(("arbitrary",)))(dx2, yc, ya, ym, proj, b_gate, w_pw, w_o, w_mo, w_out)


def _ln_swish(cv, lg, lb):
    mu = jnp.mean(cv, axis=-1, keepdims=True)
    xc = cv - mu
    r = lax.rsqrt(jnp.mean(xc * xc, axis=-1, keepdims=True) + EPS)
    n = xc * r
    l = n * lg + lb
    return r, n, l


def _conv_fwd(proj3, dw_w, dw_b, ln_g, ln_b):
    Bl, S, _ = proj3.shape
    C, K, TS, PAD = CONV_WIDTH, CONV_KERNEL, CONV_TILE, CONV_PAD
    nt = S // TS

    def body(u_ref, w_ref, b_ref, lg_ref, lb_ref, cv_ref, c_ref, vbuf, win):
        vbuf[0:PAD, :] = jnp.zeros((PAD, C), F32)

        def glu(t, carry):
            r0 = pl.multiple_of(t * TS, TS)
            u = u_ref[pl.ds(r0, TS), :].astype(F32)
            vbuf[pl.ds(PAD + r0, TS), :] = u[:, :C] * _sig(u[:, C:])
            return carry

        lax.fori_loop(0, nt, glu, 0)

        def conv(t, carry):
            r0 = pl.multiple_of(t * TS, TS)
            win[...] = vbuf[pl.ds(r0, TS + PAD), :]
            acc = jnp.zeros((TS, C), F32)
            for j in range(K):
                acc = acc + w_ref[j:j + 1, :] * win[PAD - (K - 1) + j:PAD - (K - 1) + j + TS, :]
            cv = acc + b_ref[...]
            cv_ref[pl.ds(r0, TS), :] = cv
            _, _, l = _ln_swish(cv, lg_ref[...], lb_ref[...])
            c_ref[pl.ds(r0, TS), :] = (l * _sig(l)).astype(BF16)
            return carry

        lax.fori_loop(0, nt, conv, 0)

    vec = pl.BlockSpec((1, C), lambda b: (0, 0))
    return pl.pallas_call(
        body, name="conv_fwd", grid=(Bl,),
        in_specs=[pl.BlockSpec((None, S, 2 * C), lambda b: (b, 0, 0)), pl.BlockSpec((K, C), lambda b: (0, 0)),
                  vec, vec, vec],
        out_specs=[pl.BlockSpec((None, S, C), lambda b: (b, 0, 0))] * 2,
        out_shape=[jax.ShapeDtypeStruct((Bl, S, C), F32), jax.ShapeDtypeStruct((Bl, S, C), BF16)],
        scratch_shapes=[pltpu.VMEM((S + PAD, C), F32), pltpu.VMEM((TS + PAD, C), F32)],
        compiler_params=_params(("parallel",)))(proj3, dw_w, dw_b, ln_g, ln_b)


def _conv_bwd(proj3, cv, dc, dw_w, ln_g, ln_b):
    Bl, S, _ = proj3.shape
    C, K, TS, PAD = CONV_WIDTH, CONV_KERNEL, CONV_TILE, CONV_PAD
    nt = S // TS

    def body(u_ref, cv_ref, dc_ref, w_ref, lg_ref, lb_ref, du_ref, dw_ref, db_ref, dlg_ref, dlb_ref,
             vbuf, gbuf, win, dwacc):
        b = pl.program_id(0)

        @pl.when(b == 0)
        def _():
            dw_ref[...] = jnp.zeros_like(dw_ref)
            db_ref[...] = jnp.zeros_like(db_ref)
            dlg_ref[...] = jnp.zeros_like(dlg_ref)
            dlb_ref[...] = jnp.zeros_like(dlb_ref)

        vbuf[0:PAD, :] = jnp.zeros((PAD, C), F32)
        gbuf[S:S + PAD, :] = jnp.zeros((PAD, C), F32)
        dwacc[...] = jnp.zeros_like(dwacc)

        def norm_bwd(t, carry):
            r0 = pl.multiple_of(t * TS, TS)
            u = u_ref[pl.ds(r0, TS), :].astype(F32)
            vbuf[pl.ds(PAD + r0, TS), :] = u[:, :C] * _sig(u[:, C:])
            r, n, l = _ln_swish(cv_ref[pl.ds(r0, TS), :], lg_ref[...], lb_ref[...])
            s = _sig(l)
            dl = dc_ref[pl.ds(r0, TS), :].astype(F32) * s * (1.0 + l * (1.0 - s))
            dlg_ref[...] += jnp.sum(dl * n, axis=0, keepdims=True)
            dlb_ref[...] += jnp.sum(dl, axis=0, keepdims=True)
            dn = dl * lg_ref[...]
            dcv = r * (dn - jnp.mean(dn, axis=-1, keepdims=True) - n * jnp.mean(dn * n, axis=-1, keepdims=True))
            gbuf[pl.ds(r0, TS), :] = dcv
            db_ref[...] += jnp.sum(dcv, axis=0, keepdims=True)
            return carry

        lax.fori_loop(0, nt, norm_bwd, 0)

        def conv_bwd(t, carry):
            r0 = pl.multiple_of(t * TS, TS)
            win[...] = gbuf[pl.ds(r0, TS + PAD), :]
            dv = jnp.zeros((TS, C), F32)
            for j in range(K):
                dv = dv + w_ref[j:j + 1, :] * win[K - 1 - j:K - 1 - j + TS, :]
            u = u_ref[pl.ds(r0, TS), :].astype(F32)
            a, g = u[:, :C], u[:, C:]
            s = _sig(g)
            du_ref[pl.ds(r0, TS), 0:C] = (dv * s).astype(BF16)
            du_ref[pl.ds(r0, TS), C:2 * C] = (dv * a * s * (1.0 - s)).astype(BF16)
            dcv = gbuf[pl.ds(r0, TS), :]
            win[...] = vbuf[pl.ds(r0, TS + PAD), :]
            for j in range(K):
                prod = dcv * win[PAD - (K - 1) + j:PAD - (K - 1) + j + TS, :]
                dwacc[j] += jnp.sum(prod.reshape(TS // 8, 8, C), axis=0)
            return carry

        lax.fori_loop(0, nt, conv_bwd, 0)
        dw_ref[...] += jnp.sum(dwacc[...], axis=1)

    vec = pl.BlockSpec((1, C), lambda b: (0, 0))
    seq = lambda w: pl.BlockSpec((None, S, w), lambda b: (b, 0, 0))
    return pl.pallas_call(
        body, name="conv_bwd", grid=(Bl,),
        in_specs=[seq(2 * C), seq(C), seq(C), pl.BlockSpec((K, C), lambda b: (0, 0)), vec, vec],
        out_specs=[seq(2 * C), pl.BlockSpec((K, C), lambda b: (0, 0)), vec, vec, vec],
        out_shape=[jax.ShapeDtypeStruct((Bl, S, 2 * C), BF16), jax.ShapeDtypeStruct((K, C), F32)]
        + [jax.ShapeDtypeStruct((1, C), F32)] * 3,
        scratch_shapes=[pltpu.VMEM((S + PAD, C), F32), pltpu.VMEM((S + PAD, C), F32),
                        pltpu.VMEM((TS + PAD, C), F32), pltpu.VMEM((K, 8, C), F32)],
        compiler_params=_params(("arbitrary",)))(proj3, cv, dc, dw_w, ln_g, ln_b)


def _att_bias(rel_bias):
    H = rel_bias.shape[0]
    Wd = KW + QB
    c = jnp.arange(Wd + 1)
    by_offset = rel_bias[:, jnp.clip(KW - c, -(CHUNK - 1), MAX_REL) + (CHUNK - 1)]
    flat = jnp.broadcast_to(by_offset[:, None, :], (H, QB, Wd + 1)).reshape(H, QB * (Wd + 1))
    skew = jnp.pad(flat, ((0, 0), (0, (QB + 1) * Wd - QB * (Wd + 1)))).reshape(H, QB + 1, Wd)[:, :QB, QB:]
    qi = jnp.arange(QB)[:, None]
    kj = jnp.arange(KW)[None, :]
    dchunk = ((KW - QB) + qi) // CHUNK - kj // CHUNK
    band = (dchunk >= 0) & (dchunk <= LEFT_CHUNKS)
    return jnp.where(band[None], skew, MASK_VALUE)


def _head_masks():
    lane = lax.broadcasted_iota(jnp.int32, (1, 128), 1)
    return (lane < 64, lane >= 64)


def _att_probs(qh, k2, bias, valid):
    s = _dot(qh, k2, 'nt') * ATT_SCALE + bias
    s = jnp.where(valid, s, MASK_VALUE)
    e = jnp.exp(s - jnp.max(s, axis=-1, keepdims=True))
    return e * (1.0 / jnp.sum(e, axis=-1, keepdims=True))


def _att_specs(S, q_col):
    nb = S // QB
    q_spec = pl.BlockSpec((None, QB, ATT_WIDTH), lambda b, i: (b, jnp.minimum(i, nb - 1), q_col))

    def kv_spec(col, kb):
        return pl.BlockSpec((None, QB, ATT_WIDTH),
                            lambda b, i: (b, jnp.clip(i - 2 + kb, 0, nb - 1), col))

    return q_spec, [kv_spec(3, kb) for kb in range(3)], [kv_spec(4, kb) for kb in range(3)]


def _att_fwd(proj3, bias):
    Bl, S, _ = proj3.shape
    nb = S // QB
    q_spec, k_specs, v_specs = _att_specs(S, 2)

    def body(q_ref, k0, k1, k2r, v0, v1, v2r, bias_ref, o_ref):
        i = pl.program_id(1)
        masks = _head_masks()
        valid = lax.broadcasted_iota(jnp.int32, (QB, KW), 1) >= (2 - i) * QB
        for pr in range(ATT_HEADS // 2):
            ls = slice(128 * pr, 128 * (pr + 1))
            q2 = q_ref[:, ls]
            k2 = jnp.concatenate([k0[:, ls], k1[:, ls], k2r[:, ls]], axis=0)
            v2 = jnp.concatenate([v0[:, ls], v1[:, ls], v2r[:, ls]], axis=0)
            o2 = jnp.zeros((QB, 128), F32)
            for hh in range(2):
                p = _att_probs(jnp.where(masks[hh], q2, 0), k2, bias_ref[2 * pr + hh], valid)
                o2 = o2 + _dot(p, jnp.where(masks[hh], v2, 0))
            o_ref[:, ls] = o2.astype(BF16)

    return pl.pallas_call(
        body, name="att_fwd", grid=(Bl, nb),
        in_specs=[q_spec] + k_specs + v_specs + [pl.BlockSpec((ATT_HEADS, QB, KW), lambda b, i: (0, 0, 0))],
        out_specs=pl.BlockSpec((None, QB, ATT_WIDTH), lambda b, i: (b, i, 0)),
        out_shape=jax.ShapeDtypeStruct((Bl, S, ATT_WIDTH), BF16),
        compiler_params=_params(("parallel", "arbitrary")))(*([proj3] * 7), bias)


def _att_bwd(proj3, do, bias):
    Bl, S, _ = proj3.shape
    nb = S // QB
    q_spec, k_specs, v_specs = _att_specs(S, 2)
    do_spec = pl.BlockSpec((None, QB, ATT_WIDTH), lambda b, i: (b, jnp.minimum(i, nb - 1), 0))
    kv_out = pl.BlockSpec((None, QB, ATT_WIDTH), lambda b, i: (b, jnp.clip(i - 2, 0, nb - 1), 0))
    bias_spec = pl.BlockSpec((ATT_HEADS, QB, KW), lambda b, i: (0, 0, 0))

    def body(q_ref, k0, k1, k2r, v0, v1, v2r, do_ref, bias_ref, dq_ref, dk_ref, dv_ref, db_ref, dkw, dvw):
        b = pl.program_id(0)
        i = pl.program_id(1)

        @pl.when((b == 0) & (i == 0))
        def _():
            db_ref[...] = jnp.zeros_like(db_ref)

        @pl.when(i == 0)
        def _():
            dkw[...] = jnp.zeros_like(dkw)
            dvw[...] = jnp.zeros_like(dvw)

        @pl.when(i < nb)
        def _():
            masks = _head_masks()
            valid = lax.broadcasted_iota(jnp.int32, (QB, KW), 1) >= (2 - i) * QB
            for pr in range(ATT_HEADS // 2):
                ls = slice(128 * pr, 128 * (pr + 1))
                q2 = q_ref[:, ls]
                do2 = do_ref[:, ls]
                k2 = jnp.concatenate([k0[:, ls], k1[:, ls], k2r[:, ls]], axis=0)
                v2 = jnp.concatenate([v0[:, ls], v1[:, ls], v2r[:, ls]], axis=0)
                dq2 = jnp.zeros((QB, 128), F32)
                dk2 = jnp.zeros((KW, 128), F32)
                dv2 = jnp.zeros((KW, 128), F32)
                for hh in range(2):
                    h = 2 * pr + hh
                    qh = jnp.where(masks[hh], q2, 0)
                    doh = jnp.where(masks[hh], do2, 0)
                    p = _att_probs(qh, k2, bias_ref[h], valid)
                    dp = _dot(doh, v2, 'nt')
                    ds = p * (dp - jnp.sum(p * dp, axis=-1, keepdims=True))
                    db_ref[h] += ds
                    dq2 = dq2 + _dot(ds, jnp.where(masks[hh], k2, 0))
                    dk2 = dk2 + _dot(ds, qh, 'tn')
                    dv2 = dv2 + _dot(p, doh, 'tn')
                dq_ref[:, ls] = (dq2 * ATT_SCALE).astype(BF16)
                dkw[:, ls] += dk2 * ATT_SCALE
                dvw[:, ls] += dv2

        dk_ref[...] = dkw[0:QB, :].astype(BF16)
        dv_ref[...] = dvw[0:QB, :].astype(BF16)
        for buf in (dkw, dvw):
            rest = buf[QB:KW, :]
            buf[0:KW - QB, :] = rest
            buf[KW - QB:KW, :] = jnp.zeros((QB, ATT_WIDTH), F32)

    blk = jax.ShapeDtypeStruct((Bl, S, ATT_WIDTH), BF16)
    return pl.pallas_call(
        body, name="att_bwd", grid=(Bl, nb + 2),
        in_specs=[q_spec] + k_specs + v_specs + [do_spec, bias_spec],
        out_specs=[do_spec, kv_out, kv_out, bias_spec],
        out_shape=[blk, blk, blk, jax.ShapeDtypeStruct((ATT_HEADS, QB, KW), F32)],
        scratch_shapes=[pltpu.VMEM((KW, ATT_WIDTH), F32), pltpu.VMEM((KW, ATT_WIDTH), F32)],
        compiler_params=_params(("arbitrary", "arbitrary")))(*([proj3] * 7), do, bias)


def _rel_bias_grad(dbias):
    H = dbias.shape[0]
    Wd = KW + QB
    padded = jnp.pad(dbias, ((0, 0), (0, 1), (QB, 0)))
    skew = padded.reshape(H, (QB + 1) * Wd)[:, :QB * (Wd + 1)].reshape(H, QB, Wd + 1)[:, :, :Wd]
    c = jnp.arange(Wd)[:, None]
    bins = (jnp.clip(KW - c, -(CHUNK - 1), MAX_REL) + (CHUNK - 1) == jnp.arange(N_REL)[None, :]).astype(F32)

    def body(s_ref, bins_ref, o_ref):
        col = jnp.sum(s_ref[...], axis=1)
        o_ref[...] = jnp.dot(col, bins_ref[...], preferred_element_type=F32, precision=lax.Precision.HIGHEST)

    return pl.pallas_call(
        body, name="rel_bias_grad", grid=(1,),
        in_specs=[pl.BlockSpec((H, QB, Wd), lambda i: (0, 0, 0)), pl.BlockSpec((Wd, N_REL), lambda i: (0, 0))],
        out_specs=pl.BlockSpec((H, N_REL), lambda i: (0, 0)), out_shape=jax.ShapeDtypeStruct((H, N_REL), F32),
        compiler_params=_params(("arbitrary",)))(skew, bins)


MEM_TILE = 512


def _mem_probs(qh, kh):
    s = _dot(qh, kh, 'nt') * MEM_SCALE
    e = jnp.exp(s - jnp.max(s, axis=-1, keepdims=True))
    return e * (1.0 / jnp.sum(e, axis=-1, keepdims=True))


def _mem_fwd(proj3, kv3):
    Bl, S, _ = proj3.shape
    tq = _pick(S, (MEM_TILE, 256))
    hd = MEM_HEAD_DIM

    def body(q_ref, kv_ref, o_ref):
        for h in range(MEM_HEADS):
            p = _mem_probs(q_ref[:, h * hd:(h + 1) * hd], kv_ref[:, h * hd:(h + 1) * hd])
            o_ref[:, h * hd:(h + 1) * hd] = _dot(p, kv_ref[:, MEM_WIDTH + h * hd:MEM_WIDTH + (h + 1) * hd]).astype(BF16)

    return pl.pallas_call(
        body, name="mem_fwd", grid=(Bl, S // tq),
        in_specs=[pl.BlockSpec((None, tq, MEM_WIDTH), lambda b, i: (b, i, 5)),
                  pl.BlockSpec((None, MEM_LEN, 2 * MEM_WIDTH), lambda b, i: (b, 0, 0))],
        out_specs=pl.BlockSpec((None, tq, MEM_WIDTH), lambda b, i: (b, i, 0)),
        out_shape=jax.ShapeDtypeStruct((Bl, S, MEM_WIDTH), BF16),
        compiler_params=_params(("parallel", "parallel")))(proj3, kv3)


def _mem_bwd(proj3, kv3, do):
    Bl, S, _ = proj3.shape
    tq = _pick(S, (MEM_TILE, 256))
    hd = MEM_HEAD_DIM

    def body(q_ref, kv_ref, do_ref, dq_ref, dkv_ref):
        i = pl.program_id(1)

        @pl.when(i == 0)
        def _():
            dkv_ref[...] = jnp.zeros_like(dkv_ref)

        for h in range(MEM_HEADS):
            ks = slice(h * hd, (h + 1) * hd)
            vs = slice(MEM_WIDTH + h * hd, MEM_WIDTH + (h + 1) * hd)
            qh, kh, vh, doh = q_ref[:, ks], kv_ref[:, ks], kv_ref[:, vs], do_ref[:, ks]
            p = _mem_probs(qh, kh)
            dp = _dot(doh, vh, 'nt')
            ds = p * (dp - jnp.sum(p * dp, axis=-1, keepdims=True))
            dq_ref[:, ks] = (_dot(ds, kh) * MEM_SCALE).astype(BF16)
            dkv_ref[:, ks] += _dot(ds, qh, 'tn') * MEM_SCALE
            dkv_ref[:, vs] += _dot(p, doh, 'tn')

    return pl.pallas_call(
        body, name="mem_bwd", grid=(Bl, S // tq),
        in_specs=[pl.BlockSpec((None, tq, MEM_WIDTH), lambda b, i: (b, i, 5)),
                  pl.BlockSpec((None, MEM_LEN, 2 * MEM_WIDTH), lambda b, i: (b, 0, 0)),
                  pl.BlockSpec((None, tq, MEM_WIDTH), lambda b, i: (b, i, 0))],
        out_specs=[pl.BlockSpec((None, tq, MEM_WIDTH), lambda b, i: (b, i, 0)),
                   pl.BlockSpec((None, MEM_LEN, 2 * MEM_WIDTH), lambda b, i: (b, 0, 0))],
        out_shape=[jax.ShapeDtypeStruct((Bl, S, MEM_WIDTH), BF16),
                   jax.ShapeDtypeStruct((Bl, MEM_LEN, 2 * MEM_WIDTH), F32)],
        compiler_params=_params(("parallel", "arbitrary")))(proj3, kv3, do)


def _position():
    x, y, c = lax.axis_index("x"), lax.axis_index("y"), lax.axis_index("c")
    return x, y, c, 4 * x + 2 * y + c


def _device(idx):
    return ((idx >> 2) & 1, (idx >> 1) & 1, idx & 1)


def _half_block(ref, axis, shard_shape, k, h):
    R, Cn = shard_shape
    if axis == 1:
        return ref.at[pl.ds(h * (R // 2), R // 2), pl.ds(k * Cn, Cn)]
    return ref.at[pl.ds(k * R + h * (R // 2), R // 2), :]


def _block(ref, axis, shard_shape, k):
    R, Cn = shard_shape
    if axis == 1:
        return ref.at[:, pl.ds(k * Cn, Cn)]
    return ref.at[pl.ds(k * R, R), :]


def _half(ref, h):
    R = ref.shape[0]
    return ref.at[pl.ds(h * (R // 2), R // 2), :]


ANY = pl.BlockSpec(memory_space=pl.ANY)


HBM = pl.BlockSpec(memory_space=pltpu.HBM)
SEM = pl.BlockSpec(memory_space=pltpu.SEMAPHORE)
VMEM_WHOLE = pl.BlockSpec(memory_space=pltpu.VMEM)
EFFECT = pltpu.SideEffectType.DATAFLOW_SIDE_EFFECTING


def _in_hbm(a):
    return pltpu.with_memory_space_constraint(a, pltpu.HBM)


def _split_start(body, name, sources, lands, n_copies):
    n = len(sources)
    out_shape, out_specs = [], []
    for _ in range(n):
        out_shape += [pltpu.SemaphoreType.DMA((n_copies,)), pltpu.SemaphoreType.DMA((n_copies,))]
        out_specs += [SEM, SEM]
    out_shape += [pltpu.HBM(a.shape, a.dtype) for a in list(sources) + list(lands)]
    out_specs += [HBM] * (2 * n)
    out_shape.append(jax.ShapeDtypeStruct((8, 128), F32))
    out_specs.append(VMEM_WHOLE)

    def call_body(*refs):
        srcs, lnds = refs[:n], refs[n:2 * n]
        sems = refs[2 * n:4 * n]
        token = refs[-1]
        body(srcs, lnds, sems[0::2], sems[1::2])
        token[...] = jnp.zeros_like(token)

    res = pl.pallas_call(
        call_body, name=name, in_specs=[HBM] * (2 * n), out_specs=out_specs, out_shape=out_shape,
        input_output_aliases={i: 2 * n + i for i in range(2 * n)},
        compiler_params=pltpu.CompilerParams(has_side_effects=EFFECT))(
            *[_in_hbm(a) for a in list(sources) + list(lands)])
    pairs = [(res[2 * w], res[2 * w + 1], res[2 * n + w], res[3 * n + w]) for w in range(n)]
    return pairs, res[-1]


def _split_wait(body, name, pairs, after):
    n = len(pairs)

    def call_body(*refs):
        srcs, lnds = refs[:n], refs[n:2 * n]
        sems = refs[2 * n:4 * n]
        body(srcs, lnds, sems[0::2], sems[1::2])

    args = [_in_hbm(p[2]) for p in pairs] + [_in_hbm(p[3]) for p in pairs]
    for p in pairs:
        args += [p[0], p[1]]
    res = pl.pallas_call(
        call_body, name=name, in_specs=[HBM] * (2 * n) + [SEM] * (2 * n) + [ANY], out_specs=[HBM] * (2 * n),
        out_shape=[pltpu.HBM(a.shape, a.dtype) for a in args[:2 * n]],
        input_output_aliases={i: i for i in range(2 * n)},
        compiler_params=pltpu.CompilerParams(has_side_effects=EFFECT))(*args, after)
    return res[:n], res[n:]


def _gather_copy(srcs, lnds, send, recv, axes, shapes, w, j, me):
    chip = me >> 1
    return (pltpu.make_async_remote_copy(
        src_ref=srcs[w], dst_ref=_block(lnds[w], axes[w], shapes[w], chip), send_sem=send[w].at[j],
        recv_sem=recv[w].at[j], device_id=_device(me ^ (2 * (j + 1))), device_id_type=MESH),
            pltpu.make_async_remote_copy(
        src_ref=srcs[w], dst_ref=_block(lnds[w], axes[w], shapes[w], chip ^ (j + 1)), send_sem=send[w].at[j],
        recv_sem=recv[w].at[j], device_id=_device(me ^ (2 * (j + 1))), device_id_type=MESH))


def _gather_start(shards, lands, axes, name):
    shapes = [s.shape for s in shards]

    def body(srcs, lnds, send, recv):
        x, y, c, me = _position()
        for w in range(len(shards)):
            for j in range(3):
                _gather_copy(srcs, lnds, send, recv, axes, shapes, w, j, me)[0].start()

    return _split_start(body, name, shards, lands, 3)


def _gather_wait(pairs, axes, after, name):
    shapes = [p[2].shape for p in pairs]

    def body(srcs, lnds, send, recv):
        x, y, c, me = _position()
        for w in range(len(pairs)):
            for j in range(3):
                sent, landed = _gather_copy(srcs, lnds, send, recv, axes, shapes, w, j, me)
                sent.wait_send()
                landed.wait_recv()

    return _split_wait(body, name, pairs, after)[1]


def _shard_shape(grad, axis):
    return (grad.shape[0], grad.shape[1] // 4) if axis == 1 else (grad.shape[0] // 4, grad.shape[1])


def _scatter_copy(srcs, lnds, send, recv, axes, shapes, w, m, me):
    peer = me ^ m
    return pltpu.make_async_remote_copy(
        src_ref=_half_block(srcs[w], axes[w], shapes[w], peer >> 1, peer & 1), dst_ref=lnds[w].at[m - 1],
        send_sem=send[w].at[m - 1], recv_sem=recv[w].at[m - 1], device_id=_device(peer), device_id_type=MESH)


def _scatter_start(grads, axes, name):
    shapes = [_shard_shape(g, a) for g, a in zip(grads, axes)]
    lands = [lax.empty((N_DEV - 1, R // 2, Cn), BF16) for R, Cn in shapes]

    def body(srcs, lnds, send, recv):
        x, y, c, me = _position()
        for w in range(len(grads)):
            for m in range(1, N_DEV):
                _scatter_copy(srcs, lnds, send, recv, axes, shapes, w, m, me).start()

    return _split_start(body, name, grads, lands, N_DEV - 1)


def _scatter_wait(pairs, axes, after):
    shapes = [_shard_shape(p[2], a) for p, a in zip(pairs, axes)]

    def body(srcs, lnds, send, recv):
        x, y, c, me = _position()
        for w in range(len(pairs)):
            for m in range(1, N_DEV):
                cp = _scatter_copy(srcs, lnds, send, recv, axes, shapes, w, m, me)
                cp.wait_send()
                cp.wait_recv()

    return _split_wait(body, "scatter_wait", pairs, after)


def _sum_partials(own, parts, half, name):
    R, Cn = own.shape
    tr = _pick(R, (256, 176, 128, 64, 32, 16, 8))
    nblk = R // tr

    def body(half_ref, own_ref, p_ref, o_ref):
        acc = own_ref[...].astype(F32)
        for d in range(N_DEV - 1):
            acc = acc + p_ref[d].astype(F32)
        o_ref[...] = acc

    return pl.pallas_call(
        body, name=name,
        grid_spec=pltpu.PrefetchScalarGridSpec(
            num_scalar_prefetch=1, grid=(nblk,),
            in_specs=[pl.BlockSpec((tr, Cn), lambda i, hr: (i, 0)),
                      pl.BlockSpec((N_DEV - 1, tr, Cn), lambda i, hr: (0, i, 0))],
            out_specs=pl.BlockSpec((tr, Cn), lambda i, hr: (hr[0] * nblk + i, 0))),
        out_shape=jax.ShapeDtypeStruct((2 * R, Cn), F32),
        compiler_params=_params(("parallel",)))(half, own, parts)


def _exchange_halves(grads):
    n = len(grads)

    def body(*refs):
        outs = refs[n:2 * n]
        send, recv = refs[2 * n:]
        x, y, c, me = _position()

        def copy(w, half):
            rows = _half(outs[w], half)
            return pltpu.make_async_remote_copy(src_ref=rows, dst_ref=rows, send_sem=send.at[w],
                                                recv_sem=recv.at[w], device_id=_device(me ^ 1), device_id_type=MESH)

        for w in range(n):
            copy(w, c).start()
        for w in range(n):
            copy(w, 1 - c).wait_recv()
        for w in range(n):
            copy(w, c).wait_send()

    return pl.pallas_call(
        body, name="exchange_halves", in_specs=[ANY] * n, out_specs=[ANY] * n,
        out_shape=[jax.ShapeDtypeStruct(a.shape, a.dtype) for a in grads],
        input_output_aliases={i: i for i in range(n)},
        scratch_shapes=[pltpu.SemaphoreType.DMA((n,)), pltpu.SemaphoreType.DMA((n,))],
        compiler_params=pltpu.CompilerParams(has_side_effects=True))(*grads)


def _all_reduce_small(vec):
    R, L = vec.shape

    def body(v_ref, o_ref, buf, send, recv):
        x, y, c, me = _position()
        buf[me] = v_ref[...]

        def copy(m, slot):
            return pltpu.make_async_remote_copy(src_ref=v_ref, dst_ref=buf.at[slot], send_sem=send.at[m - 1],
                                                recv_sem=recv.at[m - 1], device_id=_device(me ^ m),
                                                device_id_type=MESH)

        for m in range(1, N_DEV):
            copy(m, me).start()
        for m in range(1, N_DEV):
            copy(m, me ^ m).wait_recv()
        for m in range(1, N_DEV):
            copy(m, me).wait_send()
        acc = buf[0]
        for d in range(1, N_DEV):
            acc = acc + buf[d]
        o_ref[...] = acc

    vm = pl.BlockSpec(memory_space=pltpu.VMEM)
    return pl.pallas_call(
        body, name="all_reduce_small", in_specs=[vm], out_specs=vm, out_shape=jax.ShapeDtypeStruct((R, L), F32),
        scratch_shapes=[pltpu.VMEM((N_DEV, R, L), F32), pltpu.SemaphoreType.DMA((N_DEV - 1,)),
                        pltpu.SemaphoreType.DMA((N_DEV - 1,))],
        compiler_params=pltpu.CompilerParams(has_side_effects=True))(vec)


def _adamw(w, g, m, v, name):
    R, Cn = w.shape
    tr = _pick(R, (256, 176, 128, 64, 40, 32, 16, 8))

    def body(w_ref, g_ref, m_ref, v_ref, d_ref, nm_ref, nv_ref):
        gv = g_ref[...]
        nm = ADAM_B1 * m_ref[...] + (1.0 - ADAM_B1) * gv
        nv = ADAM_B2 * v_ref[...] + (1.0 - ADAM_B2) * (gv * gv)
        m_hat = nm / (1.0 - ADAM_B1 ** ADAM_STEP)
        v_hat = nv / (1.0 - ADAM_B2 ** ADAM_STEP)
        d_ref[...] = -ADAM_LR * (m_hat / (jnp.sqrt(v_hat) + ADAM_EPS) + ADAM_WD * w_ref[...])
        nm_ref[...] = nm
        nv_ref[...] = nv

    spec = pl.BlockSpec((tr, Cn), lambda i: (i, 0))
    return pl.pallas_call(
        body, name=name, grid=(R // tr,), in_specs=[spec] * 4, out_specs=[spec] * 3,
        out_shape=[jax.ShapeDtypeStruct((R, Cn), F32)] * 3, compiler_params=_params(("parallel",)))(w, g, m, v)


def _pack(arrays, rows):
    flat = jnp.concatenate([a.reshape(-1).astype(F32) for a in arrays])
    return jnp.pad(flat, (0, rows * 128 - flat.shape[0])).reshape(rows, 128)


def _unpack(packed, shapes):
    flat = packed.reshape(-1)
    out, off = [], 0
    for s in shapes:
        size = 1
        for d in s:
            size *= d
        out.append(flat[off:off + size].reshape(s))
        off += size
    return out


def _ffn_fwd(x, norm, arrived, tag, after=None):
    h = _rms_fwd(x, norm, f"{tag}_norm", after=after)
    w_up = arrived(f"{tag}_w_up", h)
    a, b, act = _ffn_up(h, w_up, f"{tag}_up")
    w_down = arrived(f"{tag}_w_down", act)
    out = _mm(act, w_down, 'nn', f"{tag}_down", F32, res=x, scale=0.5)
    return out, (h, a, b, act, w_up, w_down)


def _ffn_bwd(dout, x, norm, saved, tag, send):
    h, a, b, act, w_up, w_down = saved
    g_down = _mm(act, dout, 'tn', f"{tag}_down_dw", BF16, scale=0.5)
    token = send([f"{tag}_w_down"], [g_down])
    da, db = _ffn_down_bwd(dout, w_down, a, b, f"{tag}_down_dx", after=token)
    g_up = _dw_pieces(h, [da, db], f"{tag}_up_dw")
    token = send([f"{tag}_w_up"], [g_up])
    return _dx_rms_bwd([da, db], w_up, x, norm, dout, f"{tag}_up_dx", after=token)


def kernel(x, mem, ffn1_norm, ffn1_w_up, ffn1_w_down, mix_norm, mem_norm, w_in, b_gate, conv_dw_w, conv_dw_b, conv_ln_g, conv_ln_b, conv_w_pw, att_rel_bias, att_w_o, mem_w_kv, mem_w_o, w_out, ffn2_norm, ffn2_w_up, ffn2_w_down, final_norm, loss_target, m_ffn1_norm, m_ffn1_w_up, m_ffn1_w_down, m_mix_norm, m_mem_norm, m_w_in, m_b_gate, m_conv_dw_w, m_conv_dw_b, m_conv_ln_g, m_conv_ln_b, m_conv_w_pw, m_att_rel_bias, m_att_w_o, m_mem_w_kv, m_mem_w_o, m_w_out, m_ffn2_norm, m_ffn2_w_up, m_ffn2_w_down, m_final_norm, v_ffn1_norm, v_ffn1_w_up, v_ffn1_w_down, v_mix_norm, v_mem_norm, v_w_in, v_b_gate, v_conv_dw_w, v_conv_dw_b, v_conv_ln_g, v_conv_ln_b, v_conv_w_pw, v_att_rel_bias, v_att_w_o, v_mem_w_kv, v_mem_w_o, v_w_out, v_ffn2_norm, v_ffn2_w_up, v_ffn2_w_down, v_final_norm):
    given = dict(locals())
    wts = {n: given[n] for n in WEIGHTS}
    mom1 = {n: given["m_" + n] for n in WEIGHTS}
    mom2 = {n: given["v_" + n] for n in WEIGHTS}
    Bl, S, Dm = x.shape
    T = Bl * S
    x0 = x.reshape(T, Dm)
    tgt = loss_target.reshape(T, Dm)
    mem2 = mem.reshape(Bl * MEM_LEN, Dm)

    big_names = [n for n, _ in BIG]
    big_axes = [a for _, a in BIG]
    chip = 2 * lax.axis_index("x") + lax.axis_index("y")

    core = lax.axis_index("c")
    axis_of = dict(BIG)

    gather_groups = [['ffn1_w_up'], ['ffn1_w_down'], ['w_in', 'conv_dw_w'],
                     ['mem_w_kv', 'conv_w_pw', 'att_w_o', 'mem_w_o', 'w_out'], ['ffn2_w_up'], ['ffn2_w_down']]
    gather_names = [n for grp in gather_groups for n in grp]
    gather_axes = [axis_of.get(n, 1) for n in gather_names]
    shards = [jnp.pad(conv_dw_w[0], ((0, 1), (0, 0))) if n == 'conv_dw_w' else wts[n][0].astype(BF16)
              for n in gather_names]
    def placed(sh, a, fill):
        R, Cn = sh.shape
        buf = jnp.full((R, 4 * Cn) if a == 1 else (4 * R, Cn), fill, sh.dtype)
        return lax.dynamic_update_slice(buf, sh, (0, chip * Cn) if a == 1 else (chip * R, 0))

    first, first_token = _gather_start(shards[:1], [placed(shards[0], gather_axes[0], 0)], gather_axes[:1],
                                       "gather_start_first")
    lands = [placed(sh, a, first_token[0, 0].astype(sh.dtype)) for sh, a in zip(shards[1:], gather_axes[1:])]
    rest, gather_token = _gather_start(shards[1:], lands, gather_axes[1:], "gather_start_rest")
    in_flight = dict(zip(gather_names, first + rest))
    full = {}

    def arrived(name, after):
        if name not in full:
            grp = next(grp for grp in gather_groups if name in grp)
            lands = _gather_wait([in_flight[n] for n in grp], [axis_of.get(n, 1) for n in grp], after,
                                 f"gather_wait_{grp[0]}")
            full.update(zip(grp, lands))
        return full[name]

    scattering = {}

    def send(names, grads):
        pairs, token = _scatter_start(grads, [axis_of[n] for n in names], f"scatter_start_{names[0]}")
        scattering.update(zip(names, pairs))
        return token

    final_g = final_norm.reshape(1, Dm)
    bias = _att_bias(att_rel_bias[0])

    x1, ffn1_saved = _ffn_fwd(x0, ffn1_norm, arrived, "ffn1", after=gather_token)
    h = _rms_fwd(x1, mix_norm, "mix_norm")
    w_in_full = arrived('w_in', h)
    dw_full = full['conv_dw_w'][:CONV_KERNEL]
    proj = _mm(h, w_in_full, 'nn', "w_in", BF16)
    proj3 = proj.reshape(Bl, S, proj.shape[1])
    cv, c_act = _conv_fwd(proj3, dw_full, conv_dw_b, conv_ln_g, conv_ln_b)
    o_att = _att_fwd(proj3, bias)
    mem_h = _rms_fwd(mem2, mem_norm, "mem_norm")
    kv = _mm(mem_h, arrived('mem_w_kv', o_att), 'nn', "mem_kv", BF16)
    kv3 = kv.reshape(Bl, MEM_LEN, 2 * MEM_WIDTH)
    o_mem = _mem_fwd(proj3, kv3)
    c_act2, o_att2, o_mem2 = c_act.reshape(T, -1), o_att.reshape(T, -1), o_mem.reshape(T, -1)
    x2, yc, ya, ym, y = _mix_fwd(c_act2, o_att2, o_mem2, proj, b_gate, x1, full['conv_w_pw'], full['att_w_o'],
                                 full['mem_w_o'], full['w_out'])
    x3, ffn2_saved = _ffn_fwd(x2, ffn2_norm, arrived, "ffn2")
    dx3, g_final, loss_vec = _final_fwd_bwd(x3, tgt, final_g)

    g = {}
    dx2, g['ffn2_norm'] = _ffn_bwd(dx3, x2, ffn2_norm, ffn2_saved, "ffn2", send)
    dyc, dya, dym, dgl, g['b_gate'], dc, doa, dom = _mix_bwd(
        dx2, yc, ya, ym, proj, b_gate, full['conv_w_pw'], full['att_w_o'], full['mem_w_o'], full['w_out'])
    token = send(['w_out', 'conv_w_pw', 'att_w_o', 'mem_w_o'],
                 [_mm(y, dx2, 'tn', "w_out_dw", BF16), _mm(c_act2, dyc, 'tn', "conv_pw_dw", BF16),
                  _mm(o_att2, dya, 'tn', "att_o_dw", BF16), _mm(o_mem2, dym, 'tn', "mem_o_dw", BF16)])
    du, g_dw, g['conv_dw_b'], g['conv_ln_g'], g['conv_ln_b'] = _conv_bwd(
        proj3, cv, dc.reshape(Bl, S, -1), dw_full, conv_ln_g, conv_ln_b)
    dq, dk, dv, dbias = _att_bwd(proj3, doa.reshape(Bl, S, -1), bias)
    g['att_rel_bias'] = _rel_bias_grad(dbias)
    dmq, dkv = _mem_bwd(proj3, kv3, dom.reshape(Bl, S, -1))
    dkv2 = dkv.reshape(Bl * MEM_LEN, 2 * MEM_WIDTH)
    g_kv = _mm(mem_h, dkv2, 'tn', "mem_kv_dw", BF16, after=token)
    dmem_h = _mm(dkv2, full['mem_w_kv'], 'nt', "mem_kv_dx", F32)
    _, g['mem_norm'] = _rms_bwd(mem2, mem_norm, dmem_h, dmem_h, "mem_norm_bwd")
    dproj = jnp.concatenate([du.reshape(T, -1), dq.reshape(T, -1), dk.reshape(T, -1), dv.reshape(T, -1),
                             dmq.reshape(T, -1), dgl], axis=1)
    token = send(['mem_w_kv', 'w_in'], [g_kv, _mm(h, dproj, 'tn', "w_in_dw", BF16)])
    dx1, g['mix_norm'] = _dx_rms_bwd([dproj], w_in_full, x1, mix_norm, dx2, "w_in_dx", after=token)
    dx0, g['ffn1_norm'] = _ffn_bwd(dx1, x0, ffn1_norm, ffn1_saved, "ffn1", send)
    g['final_norm'] = g_final

    sent, landed = _scatter_wait([scattering[n] for n in big_names], big_axes, dx0)
    halves = []
    half_idx = core.reshape(1).astype(jnp.int32)
    for n, a, own_full, parts in zip(big_names, big_axes, sent, landed):
        R, Cn = _shard_shape(own_full, a)
        start = (core * (R // 2), chip * Cn) if a == 1 else (chip * R + core * (R // 2), 0)
        own = lax.dynamic_slice(own_full, start, (R // 2, Cn))
        halves.append(_sum_partials(own, parts, half_idx, f"sum_{n}"))
    for n, sg in zip(big_names, _exchange_halves(halves)):
        g[n] = sg

    small_shapes = [wts[n].shape for n in SMALL]
    n_small = sum(int(wts[n].size) for n in SMALL)
    n_red = n_small + CONV_KERNEL * CONV_WIDTH
    red = _all_reduce_small(_pack([g[n] for n in SMALL] + [g_dw], -(-n_red // 1024) * 8))
    red_list = _unpack(red, small_shapes + [(CONV_KERNEL, CONV_WIDTH)])
    for n, rg in zip(SMALL, red_list[:-1]):
        g[n] = rg
    dw_cols = conv_dw_w.shape[2]
    g['conv_dw_w'] = lax.dynamic_slice(red_list[-1], (0, chip * dw_cols), (CONV_KERNEL, dw_cols))[None]

    delta, new_m, new_v = {}, {}, {}
    for n in big_names:
        g[n] = g[n][None]
        d, nm, nv = _adamw(wts[n][0], g[n][0], mom1[n][0], mom2[n][0], f"adamw_{n}")
        delta[n], new_m[n], new_v[n] = d[None], nm[None], nv[None]
    rest = SMALL + ['conv_dw_w']
    rest_shapes = [wts[n].shape for n in rest]
    rows = -(-sum(int(wts[n].size) for n in rest) // 1024) * 8
    packed = [_pack([src[n] for n in rest], rows) for src in (wts, g, mom1, mom2)]
    for out, res in zip((delta, new_m, new_v), _adamw(*packed, "adamw_small")):
        for n, a in zip(rest, _unpack(res, rest_shapes)):
            out[n] = a

    loss = lax.psum(loss_vec[0, 0], ("x", "y", "c"))
    grad_x = dx0.reshape(Bl, S, Dm)
    return (loss, grad_x, *[g[n] for n in WEIGHTS], *[delta[n] for n in WEIGHTS],
            *[new_m[n] for n in WEIGHTS], *[new_v[n] for n in WEIGHTS])
```

```python
import jax
import jax.numpy as jnp
from jax import lax
from jax.experimental import pallas as pl
from jax.experimental.pallas import tpu as pltpu

F32 = jnp.float32
BF16 = jnp.bfloat16

D_MODEL = 1024
D_FF = 2816
CHUNK = 64
LEFT_CHUNKS = 8
MAX_REL = 128
N_REL = (CHUNK - 1) + MAX_REL + 1
CONV_WIDTH = 512
CONV_KERNEL = 31
ATT_HEADS = 8
ATT_WIDTH = 512
MEM_LEN = 256
MEM_HEADS = 4
MEM_HEAD_DIM = 128
MEM_WIDTH = 512
EPS = 1e-6
MASK_VALUE = -1e30
ATT_SCALE = 64 ** -0.5
MEM_SCALE = 128 ** -0.5

ADAM_LR = 0.001
ADAM_B1 = 0.9
ADAM_B2 = 0.999
ADAM_EPS = 1e-08
ADAM_WD = 0.01
ADAM_STEP = 10

QB = 256
KW = 3 * QB
CONV_PAD = 32
CONV_TILE = 256

VMEM_LIMIT = 56 << 20

WEIGHTS = ['ffn1_norm', 'ffn1_w_up', 'ffn1_w_down', 'mix_norm', 'mem_norm', 'w_in', 'b_gate', 'conv_dw_w',
           'conv_dw_b', 'conv_ln_g', 'conv_ln_b', 'conv_w_pw', 'att_rel_bias', 'att_w_o', 'mem_w_kv', 'mem_w_o',
           'w_out', 'ffn2_norm', 'ffn2_w_up', 'ffn2_w_down', 'final_norm']
BIG = [('ffn1_w_up', 1), ('ffn1_w_down', 0), ('w_in', 1), ('conv_w_pw', 1), ('att_w_o', 1), ('mem_w_kv', 0),
       ('mem_w_o', 1), ('w_out', 0), ('ffn2_w_up', 1), ('ffn2_w_down', 0)]
SMALL = ['ffn1_norm', 'mix_norm', 'mem_norm', 'b_gate', 'conv_dw_b', 'conv_ln_g', 'conv_ln_b', 'att_rel_bias',
         'ffn2_norm', 'final_norm']
N_CHIPS = 4
N_DEV = 8
MESH = pl.DeviceIdType.MESH


def _pick(n, cands):
    for c in cands:
        if n % c == 0:
            return c
    return n


def _sig(x):
    return 0.5 * jnp.tanh(0.5 * x) + 0.5


def _params(sem=None, vmem=VMEM_LIMIT):
    return pltpu.CompilerParams(dimension_semantics=sem, vmem_limit_bytes=vmem)


def _dot(a, b, mode='nn'):
    dims = {'nn': (((1,), (0,)), ((), ())), 'nt': (((1,), (1,)), ((), ())), 'tn': (((0,), (0,)), ((), ()))}[mode]
    return lax.dot_general(a.astype(BF16), b.astype(BF16), dims, preferred_element_type=F32)


def _mm(a, b, mode, name, out_dtype, res=None, scale=1.0, after=None):
    if mode == 'nn':
        (M, C), (_, N) = a.shape, b.shape
    elif mode == 'nt':
        (M, C), (N, _) = a.shape, b.shape
    else:
        (C, M), (_, N) = a.shape, b.shape
    tm = _pick(M, (1024, 1408, 512, 256, 128))
    tn = _pick(N, (1024, 1408, 512, 256, 128))
    tc = _pick(C, (1024, 1408, 512, 256, 128))
    nk = C // tc
    if mode == 'nn':
        a_spec = pl.BlockSpec((tm, tc), lambda i, j, k: (i, k))
        b_spec = pl.BlockSpec((tc, tn), lambda i, j, k: (k, j))
    elif mode == 'nt':
        a_spec = pl.BlockSpec((tm, tc), lambda i, j, k: (i, k))
        b_spec = pl.BlockSpec((tn, tc), lambda i, j, k: (j, k))
    else:
        a_spec = pl.BlockSpec((tc, tm), lambda i, j, k: (k, i))
        b_spec = pl.BlockSpec((tc, tn), lambda i, j, k: (k, j))
    o_spec = pl.BlockSpec((tm, tn), lambda i, j, k: (i, j))
    has_res = res is not None
    has_after = after is not None

    def body(*refs):
        a_ref, b_ref = refs[:2]
        r_ref = refs[2] if has_res else None
        o_ref, acc_ref = refs[-2:]
        k = pl.program_id(2)
        p = _dot(a_ref[...], b_ref[...], mode)

        def finish(acc):
            if scale != 1.0:
                acc = acc * scale
            if r_ref is not None:
                acc = r_ref[...] + acc
            o_ref[...] = acc.astype(o_ref.dtype)

        if nk == 1:
            finish(p)
        else:
            @pl.when(k == 0)
            def _():
                acc_ref[...] = p

            @pl.when(k > 0)
            def _():
                acc_ref[...] += p

            @pl.when(k == nk - 1)
            def _():
                finish(acc_ref[...])

    in_specs = [a_spec, b_spec] + ([o_spec] if has_res else []) + ([ANY] if has_after else [])
    args = (a, b) + ((res,) if has_res else ()) + ((after,) if has_after else ())
    acc_shape = (tm, tn) if nk > 1 else (8, 128)
    return pl.pallas_call(
        body, name=name, grid=(M // tm, N // tn, nk), in_specs=in_specs, out_specs=o_spec,
        out_shape=jax.ShapeDtypeStruct((M, N), out_dtype), scratch_shapes=[pltpu.VMEM(acc_shape, F32)],
        compiler_params=_params(("parallel", "parallel", "arbitrary")))(*args)


def _row_tile(T):
    return _pick(T, (512, 256, 128, 64, 32, 16, 8))


def _rms_fwd(x, g, name, after=None):
    T, Dm = x.shape
    tm = _row_tile(T)

    def body(x_ref, g_ref, *rest):
        o_ref = rest[-1]
        xv = x_ref[...]
        r = lax.rsqrt(jnp.mean(xv * xv, axis=-1, keepdims=True) + EPS)
        o_ref[...] = ((xv * r) * g_ref[...]).astype(o_ref.dtype)

    extra = () if after is None else (after,)
    return pl.pallas_call(
        body, name=name, grid=(T // tm,),
        in_specs=[pl.BlockSpec((tm, Dm), lambda i: (i, 0)), pl.BlockSpec((1, Dm), lambda i: (0, 0))]
        + [ANY] * len(extra),
        out_specs=pl.BlockSpec((tm, Dm), lambda i: (i, 0)), out_shape=jax.ShapeDtypeStruct((T, Dm), BF16),
        compiler_params=_params(("parallel",)))(x, g, *extra)


def _rms_bwd(x, g, dh, dres, name):
    T, Dm = x.shape
    tm = _row_tile(T)

    def body(x_ref, g_ref, dh_ref, dr_ref, dx_ref, dg_ref):
        i = pl.program_id(0)
        xv = x_ref[...]
        r = lax.rsqrt(jnp.mean(xv * xv, axis=-1, keepdims=True) + EPS)
        xr = xv * r
        dh_v = dh_ref[...].astype(F32)
        dyg = dh_v * g_ref[...]
        dx = r * (dyg - xr * jnp.mean(dyg * xr, axis=-1, keepdims=True))
        dx_ref[...] = dr_ref[...] + dx

        @pl.when(i == 0)
        def _():
            dg_ref[...] = jnp.zeros_like(dg_ref)

        dg_ref[...] += jnp.sum(dh_v * xr, axis=0, keepdims=True)

    row = pl.BlockSpec((tm, Dm), lambda i: (i, 0))
    vec = pl.BlockSpec((1, Dm), lambda i: (0, 0))
    return pl.pallas_call(
        body, name=name, grid=(T // tm,), in_specs=[row, vec, row, row], out_specs=[row, vec],
        out_shape=[jax.ShapeDtypeStruct((T, Dm), F32), jax.ShapeDtypeStruct((1, Dm), F32)],
        compiler_params=_params(("arbitrary",)))(x, g, dh, dres)


def _final_fwd_bwd(x3, tgt, g):
    T, Dm = x3.shape
    tm = _row_tile(T)

    def body(x_ref, t_ref, g_ref, dx_ref, dg_ref, loss_ref):
        i = pl.program_id(0)
        xv = x_ref[...]
        gg = g_ref[...]
        r = lax.rsqrt(jnp.mean(xv * xv, axis=-1, keepdims=True) + EPS)
        xr = xv * r
        err = xr * gg - t_ref[...]
        dout = err * (1.0 / Dm)
        dyg = dout * gg
        dx_ref[...] = r * (dyg - xr * jnp.mean(dyg * xr, axis=-1, keepdims=True))

        @pl.when(i == 0)
        def _():
            dg_ref[...] = jnp.zeros_like(dg_ref)
            loss_ref[...] = jnp.zeros_like(loss_ref)

        dg_ref[...] += jnp.sum(dout * xr, axis=0, keepdims=True)
        loss_ref[...] += jnp.zeros_like(loss_ref) + (0.5 / Dm) * jnp.sum(err * err)

    row = pl.BlockSpec((tm, Dm), lambda i: (i, 0))
    vec = pl.BlockSpec((1, Dm), lambda i: (0, 0))
    one = pl.BlockSpec((1, 128), lambda i: (0, 0))
    return pl.pallas_call(
        body, name="final_fwd_bwd", grid=(T // tm,), in_specs=[row, row, vec], out_specs=[row, vec, one],
        out_shape=[jax.ShapeDtypeStruct((T, Dm), F32), jax.ShapeDtypeStruct((1, Dm), F32),
                   jax.ShapeDtypeStruct((1, 128), F32)],
        compiler_params=_params(("arbitrary",)))(x3, tgt, g)


def _ffn_up(h, w_up, name):
    T, K = h.shape
    Fh = w_up.shape[1] // 2
    tm = _pick(T, (512, 256, 128))
    tn = _pick(Fh, (1408, 512, 256, 128))
    nj = Fh // tn

    def body(h_ref, wa_ref, wb_ref, a_ref, b_ref, act_ref):
        hv = h_ref[...]
        a = _dot(hv, wa_ref[...])
        b = _dot(hv, wb_ref[...])
        a_ref[...] = a.astype(BF16)
        b_ref[...] = b.astype(BF16)
        act_ref[...] = (a * _sig(a) * b).astype(BF16)

    out = pl.BlockSpec((tm, tn), lambda i, j: (i, j))
    return pl.pallas_call(
        body, name=name, grid=(T // tm, nj),
        in_specs=[pl.BlockSpec((tm, K), lambda i, j: (i, 0)), pl.BlockSpec((K, tn), lambda i, j: (0, j)),
                  pl.BlockSpec((K, tn), lambda i, j: (0, j + nj))],
        out_specs=[out, out, out], out_shape=[jax.ShapeDtypeStruct((T, Fh), BF16)] * 3,
        compiler_params=_params(("parallel", "parallel")))(h, w_up, w_up)


def _ffn_down_bwd(dout, w_down, a, b, name, after=None):
    T, Dm = dout.shape
    Fh = w_down.shape[0]
    tm = _pick(T, (512, 256, 128))
    tn = _pick(Fh, (1408, 512, 256, 128))

    def body(d_ref, w_ref, a_ref, b_ref, *rest):
        da_ref, db_ref = rest[-2:]
        dact = _dot(d_ref[...], w_ref[...], 'nt') * 0.5
        av = a_ref[...].astype(F32)
        bv = b_ref[...].astype(F32)
        s = _sig(av)
        da_ref[...] = (dact * bv * s * (1.0 + av * (1.0 - s))).astype(BF16)
        db_ref[...] = (dact * av * s).astype(BF16)

    tile = pl.BlockSpec((tm, tn), lambda i, j: (i, j))
    extra = () if after is None else (after,)
    return pl.pallas_call(
        body, name=name, grid=(T // tm, Fh // tn),
        in_specs=[pl.BlockSpec((tm, Dm), lambda i, j: (i, 0)), pl.BlockSpec((tn, Dm), lambda i, j: (j, 0)),
                  tile, tile] + [ANY] * len(extra),
        out_specs=[tile, tile], out_shape=[jax.ShapeDtypeStruct((T, Fh), BF16)] * 2,
        compiler_params=_params(("parallel", "parallel")))(dout, w_down, a, b, *extra)


def _dx_rms_bwd(pieces, w, x, g, dres, name, after=None):
    T, Dm = x.shape
    width = pieces[0].shape[1]
    tm = _pick(T, (512, 256, 128))
    tc = _pick(width, (1408, 1024, 512, 256, 128))
    per = width // tc
    nk = per * len(pieces)
    npc = len(pieces)

    def body(*refs):
        p_refs = refs[:npc]
        w_ref, x_ref, g_ref, dr_ref = refs[npc:npc + 4]
        dx_ref, dg_ref, acc_ref = refs[-3:]
        i = pl.program_id(0)
        k = pl.program_id(1)
        lhs = p_refs[0][...]
        for p in range(1, npc):
            lhs = jnp.where(k >= p * per, p_refs[p][...], lhs)
        part = _dot(lhs, w_ref[...], 'nt')

        @pl.when(k == 0)
        def _():
            acc_ref[...] = part

        @pl.when(k > 0)
        def _():
            acc_ref[...] += part

        @pl.when((i == 0) & (k == 0))
        def _():
            dg_ref[...] = jnp.zeros_like(dg_ref)

        @pl.when(k == nk - 1)
        def _():
            dh = acc_ref[...]
            xv = x_ref[...]
            r = lax.rsqrt(jnp.mean(xv * xv, axis=-1, keepdims=True) + EPS)
            xr = xv * r
            dyg = dh * g_ref[...]
            dx_ref[...] = dr_ref[...] + r * (dyg - xr * jnp.mean(dyg * xr, axis=-1, keepdims=True))
            dg_ref[...] += jnp.sum(dh * xr, axis=0, keepdims=True)

    def piece_spec(p):
        return pl.BlockSpec((tm, tc), lambda i, k: (i, jnp.clip(k - p * per, 0, per - 1)))

    row = pl.BlockSpec((tm, Dm), lambda i, k: (i, 0))
    vec = pl.BlockSpec((1, Dm), lambda i, k: (0, 0))
    extra = () if after is None else (after,)
    return pl.pallas_call(
        body, name=name, grid=(T // tm, nk),
        in_specs=[piece_spec(p) for p in range(npc)] + [pl.BlockSpec((Dm, tc), lambda i, k: (0, k)), row, vec, row]
        + [ANY] * len(extra),
        out_specs=[row, vec], out_shape=[jax.ShapeDtypeStruct((T, Dm), F32), jax.ShapeDtypeStruct((1, Dm), F32)],
        scratch_shapes=[pltpu.VMEM((tm, Dm), F32)],
        compiler_params=_params(("arbitrary", "arbitrary")))(*pieces, w, x, g, dres, *extra)


def _dw_pieces(a, pieces, name):
    C, M = a.shape
    width = pieces[0].shape[1]
    npc = len(pieces)
    tm = _pick(M, (1024, 512, 256, 128))
    tn = _pick(width, (1408, 1024, 512, 256, 128))
    tc = _pick(C, (1024, 512, 256, 128))
    per = width // tn
    nk = C // tc

    def body(*refs):
        a_ref = refs[0]
        p_refs = refs[1:1 + npc]
        o_ref, acc_ref = refs[-2:]
        j = pl.program_id(1)
        k = pl.program_id(2)
        rhs = p_refs[0][...]
        for p in range(1, npc):
            rhs = jnp.where(j >= p * per, p_refs[p][...], rhs)
        part = _dot(a_ref[...], rhs, 'tn')

        @pl.when(k == 0)
        def _():
            acc_ref[...] = part

        @pl.when(k > 0)
        def _():
            acc_ref[...] += part

        @pl.when(k == nk - 1)
        def _():
            o_ref[...] = acc_ref[...].astype(o_ref.dtype)

    def piece_spec(p):
        return pl.BlockSpec((tc, tn), lambda i, j, k: (k, jnp.clip(j - p * per, 0, per - 1)))

    return pl.pallas_call(
        body, name=name, grid=(M // tm, per * npc, nk),
        in_specs=[pl.BlockSpec((tc, tm), lambda i, j, k: (k, i))] + [piece_spec(p) for p in range(npc)],
        out_specs=pl.BlockSpec((tm, tn), lambda i, j, k: (i, j)),
        out_shape=jax.ShapeDtypeStruct((M, width * npc), BF16), scratch_shapes=[pltpu.VMEM((tm, tn), F32)],
        compiler_params=_params(("parallel", "parallel", "arbitrary")))(a, *pieces)


def _mix_fwd(c_act, o_att, o_mem, proj, b_gate, x1, w_pw, w_o, w_mo, w_out):
    T, Dm = x1.shape
    W = c_act.shape[1]
    tm = _pick(T, (256, 128, 64, 32, 16, 8))

    def body(c_ref, oa_ref, om_ref, gl_ref, bg_ref, x1_ref, wpw_ref, wo_ref, wmo_ref, wout_ref,
             x2_ref, yc_ref, ya_ref, ym_ref, y_ref):
        yc = _dot(c_ref[...], wpw_ref[...])
        ya = _dot(oa_ref[...], wo_ref[...])
        ym = _dot(om_ref[...], wmo_ref[...])
        g = _sig(gl_ref[...].astype(F32) + bg_ref[...])
        y = g[:, :Dm] * yc + g[:, Dm:2 * Dm] * ya + g[:, 2 * Dm:] * ym
        x2_ref[...] = x1_ref[...] + _dot(y, wout_ref[...])
        yc_ref[...] = yc.astype(BF16)
        ya_ref[...] = ya.astype(BF16)
        ym_ref[...] = ym.astype(BF16)
        y_ref[...] = y.astype(BF16)

    rowW = pl.BlockSpec((tm, W), lambda i: (i, 0))
    rowD = pl.BlockSpec((tm, Dm), lambda i: (i, 0))
    full = lambda s: pl.BlockSpec(s, lambda i: (0, 0))
    return pl.pallas_call(
        body, name="mix_fwd", grid=(T // tm,),
        in_specs=[rowW, rowW, rowW, pl.BlockSpec((tm, 3 * Dm), lambda i: (i, 1)), full((1, 3 * Dm)), rowD,
                  full((W, Dm)), full((W, Dm)), full((W, Dm)), full((Dm, Dm))],
        out_specs=[rowD] * 5,
        out_shape=[jax.ShapeDtypeStruct((T, Dm), F32)] + [jax.ShapeDtypeStruct((T, Dm), BF16)] * 4,
        compiler_params=_params(("parallel",)))(c_act, o_att, o_mem, proj, b_gate, x1, w_pw, w_o, w_mo, w_out)


def _mix_bwd(dx2, yc, ya, ym, proj, b_gate, w_pw, w_o, w_mo, w_out):
    T, Dm = dx2.shape
    W = w_pw.shape[0]
    tm = _pick(T, (256, 128, 64, 32, 16, 8))

    def body(dx_ref, yc_ref, ya_ref, ym_ref, gl_ref, bg_ref, wpw_ref, wo_ref, wmo_ref, wout_ref,
             dyc_ref, dya_ref, dym_ref, dgl_ref, dbg_ref, dc_ref, doa_ref, dom_ref):
        i = pl.program_id(0)

        @pl.when(i == 0)
        def _():
            dbg_ref[...] = jnp.zeros_like(dbg_ref)

        dy = _dot(dx_ref[...], wout_ref[...], 'nt')
        g = _sig(gl_ref[...].astype(F32) + bg_ref[...])
        branches = ((yc_ref, dyc_ref, wpw_ref, dc_ref), (ya_ref, dya_ref, wo_ref, doa_ref),
                    (ym_ref, dym_ref, wmo_ref, dom_ref))
        for n, (y_ref, dyk_ref, w_ref, dk_ref) in enumerate(branches):
            gk = g[:, n * Dm:(n + 1) * Dm]
            dyk = dy * gk
            dgl = dyk * y_ref[...].astype(F32) * (1.0 - gk)
            dyk_ref[...] = dyk.astype(BF16)
            dgl_ref[:, n * Dm:(n + 1) * Dm] = dgl.astype(BF16)
            dbg_ref[:, n * Dm:(n + 1) * Dm] += jnp.sum(dgl, axis=0, keepdims=True)
            dk_ref[...] = _dot(dyk, w_ref[...], 'nt').astype(BF16)

    rowW = pl.BlockSpec((tm, W), lambda i: (i, 0))
    rowD = pl.BlockSpec((tm, Dm), lambda i: (i, 0))
    row3 = pl.BlockSpec((tm, 3 * Dm), lambda i: (i, 0))
    full = lambda s: pl.BlockSpec(s, lambda i: (0, 0))
    return pl.pallas_call(
        body, name="mix_bwd", grid=(T // tm,),
        in_specs=[rowD, rowD, rowD, rowD, pl.BlockSpec((tm, 3 * Dm), lambda i: (i, 1)), full((1, 3 * Dm)),
                  full((W, Dm)), full((W, Dm)), full((W, Dm)), full((Dm, Dm))],
        out_specs=[rowD, rowD, rowD, pl.BlockSpec((tm, 3 * Dm), lambda i: (i, 1)), full((1, 3 * Dm)),
                   rowW, rowW, rowW],
        out_shape=[jax.ShapeDtypeStruct((T, Dm), BF16)] * 3 + [jax.ShapeDtypeStruct((T, 6 * Dm), BF16),
                                                                jax.ShapeDtypeStruct((1, 3 * Dm), F32)]
        + [jax.ShapeDtypeStruct((T, W), BF16)] * 3,
        compiler_params=_params(("arbitrary",)))(dx2, yc, ya, ym, proj, b_gate, w_pw, w_o, w_mo, w_out)


def _ln_swish(cv, lg, lb):
    mu = jnp.mean(cv, axis=-1, keepdims=True)
    xc = cv - mu
    r = lax.rsqrt(jnp.mean(xc * xc, axis=-1, keepdims=True) + EPS)
    n = xc * r
    l = n * lg + lb
    return r, n, l


def _conv_fwd(proj3, dw_w, dw_b, ln_g, ln_b):
    Bl, S, _ = proj3.shape
    C, K, TS, PAD = CONV_WIDTH, CONV_KERNEL, CONV_TILE, CONV_PAD
    nt = S // TS

    def body(u_ref, w_ref, b_ref, lg_ref, lb_ref, cv_ref, c_ref, vbuf, win):
        vbuf[0:PAD, :] = jnp.zeros((PAD, C), F32)

        def glu(t, carry):
            r0 = pl.multiple_of(t * TS, TS)
            u = u_ref[pl.ds(r0, TS), :].astype(F32)
            vbuf[pl.ds(PAD + r0, TS), :] = u[:, :C] * _sig(u[:, C:])
            return carry

        lax.fori_loop(0, nt, glu, 0)

        def conv(t, carry):
            r0 = pl.multiple_of(t * TS, TS)
            win[...] = vbuf[pl.ds(r0, TS + PAD), :]
            acc = jnp.zeros((TS, C), F32)
            for j in range(K):
                acc = acc + w_ref[j:j + 1, :] * win[PAD - (K - 1) + j:PAD - (K - 1) + j + TS, :]
            cv = acc + b_ref[...]
            cv_ref[pl.ds(r0, TS), :] = cv
            _, _, l = _ln_swish(cv, lg_ref[...], lb_ref[...])
            c_ref[pl.ds(r0, TS), :] = (l * _sig(l)).astype(BF16)
            return carry

        lax.fori_loop(0, nt, conv, 0)

    vec = pl.BlockSpec((1, C), lambda b: (0, 0))
    return pl.pallas_call(
        body, name="conv_fwd", grid=(Bl,),
        in_specs=[pl.BlockSpec((None, S, 2 * C), lambda b: (b, 0, 0)), pl.BlockSpec((K, C), lambda b: (0, 0)),
                  vec, vec, vec],
        out_specs=[pl.BlockSpec((None, S, C), lambda b: (b, 0, 0))] * 2,
        out_shape=[jax.ShapeDtypeStruct((Bl, S, C), F32), jax.ShapeDtypeStruct((Bl, S, C), BF16)],
        scratch_shapes=[pltpu.VMEM((S + PAD, C), F32), pltpu.VMEM((TS + PAD, C), F32)],
        compiler_params=_params(("parallel",)))(proj3, dw_w, dw_b, ln_g, ln_b)


def _conv_bwd(proj3, cv, dc, dw_w, ln_g, ln_b):
    Bl, S, _ = proj3.shape
    C, K, TS, PAD = CONV_WIDTH, CONV_KERNEL, CONV_TILE, CONV_PAD
    nt = S // TS

    def body(u_ref, cv_ref, dc_ref, w_ref, lg_ref, lb_ref, du_ref, dw_ref, db_ref, dlg_ref, dlb_ref,
             vbuf, gbuf, win, dwacc):
        b = pl.program_id(0)

        @pl.when(b == 0)
        def _():
            dw_ref[...] = jnp.zeros_like(dw_ref)
            db_ref[...] = jnp.zeros_like(db_ref)
            dlg_ref[...] = jnp.zeros_like(dlg_ref)
            dlb_ref[...] = jnp.zeros_like(dlb_ref)

        vbuf[0:PAD, :] = jnp.zeros((PAD, C), F32)
        gbuf[S:S + PAD, :] = jnp.zeros((PAD, C), F32)
        dwacc[...] = jnp.zeros_like(dwacc)

        def norm_bwd(t, carry):
            r0 = pl.multiple_of(t * TS, TS)
            u = u_ref[pl.ds(r0, TS), :].astype(F32)
            vbuf[pl.ds(PAD + r0, TS), :] = u[:, :C] * _sig(u[:, C:])
            r, n, l = _ln_swish(cv_ref[pl.ds(r0, TS), :], lg_ref[...], lb_ref[...])
            s = _sig(l)
            dl = dc_ref[pl.ds(r0, TS), :].astype(F32) * s * (1.0 + l * (1.0 - s))
            dlg_ref[...] += jnp.sum(dl * n, axis=0, keepdims=True)
            dlb_ref[...] += jnp.sum(dl, axis=0, keepdims=True)
            dn = dl * lg_ref[...]
            dcv = r * (dn - jnp.mean(dn, axis=-1, keepdims=True) - n * jnp.mean(dn * n, axis=-1, keepdims=True))
            gbuf[pl.ds(r0, TS), :] = dcv
            db_ref[...] += jnp.sum(dcv, axis=0, keepdims=True)
            return carry

        lax.fori_loop(0, nt, norm_bwd, 0)

        def conv_bwd(t, carry):
            r0 = pl.multiple_of(t * TS, TS)
            win[...] = gbuf[pl.ds(r0, TS + PAD), :]
            dv = jnp.zeros((TS, C), F32)
            for j in range(K):
                dv = dv + w_ref[j:j + 1, :] * win[K - 1 - j:K - 1 - j + TS, :]
            u = u_ref[pl.ds(r0, TS), :].astype(F32)
            a, g = u[:, :C], u[:, C:]
            s = _sig(g)
            du_ref[pl.ds(r0, TS), 0:C] = (dv * s).astype(BF16)
            du_ref[pl.ds(r0, TS), C:2 * C] = (dv * a * s * (1.0 - s)).astype(BF16)
            dcv = gbuf[pl.ds(r0, TS), :]
            win[...] = vbuf[pl.ds(r0, TS + PAD), :]
            for j in range(K):
                prod = dcv * win[PAD - (K - 1) + j:PAD - (K - 1) + j + TS, :]
                dwacc[j] += jnp.sum(prod.reshape(TS // 8, 8, C), axis=0)
            return carry

        lax.fori_loop(0, nt, conv_bwd, 0)
        dw_ref[...] += jnp.sum(dwacc[...], axis=1)

    vec = pl.BlockSpec((1, C), lambda b: (0, 0))
    seq = lambda w: pl.BlockSpec((None, S, w), lambda b: (b, 0, 0))
    return pl.pallas_call(
        body, name="conv_bwd", grid=(Bl,),
        in_specs=[seq(2 * C), seq(C), seq(C), pl.BlockSpec((K, C), lambda b: (0, 0)), vec, vec],
        out_specs=[seq(2 * C), pl.BlockSpec((K, C), lambda b: (0, 0)), vec, vec, vec],
        out_shape=[jax.ShapeDtypeStruct((Bl, S, 2 * C), BF16), jax.ShapeDtypeStruct((K, C), F32)]
        + [jax.ShapeDtypeStruct((1, C), F32)] * 3,
        scratch_shapes=[pltpu.VMEM((S + PAD, C), F32), pltpu.VMEM((S + PAD, C), F32),
                        pltpu.VMEM((TS + PAD, C), F32), pltpu.VMEM((K, 8, C), F32)],
        compiler_params=_params(("arbitrary",)))(proj3, cv, dc, dw_w, ln_g, ln_b)


def _att_bias(rel_bias):
    H = rel_bias.shape[0]
    Wd = KW + QB
    c = jnp.arange(Wd + 1)
    by_offset = rel_bias[:, jnp.clip(KW - c, -(CHUNK - 1), MAX_REL) + (CHUNK - 1)]
    flat = jnp.broadcast_to(by_offset[:, None, :], (H, QB, Wd + 1)).reshape(H, QB * (Wd + 1))
    skew = jnp.pad(flat, ((0, 0), (0, (QB + 1) * Wd - QB * (Wd + 1)))).reshape(H, QB + 1, Wd)[:, :QB, QB:]
    qi = jnp.arange(QB)[:, None]
    kj = jnp.arange(KW)[None, :]
    dchunk = ((KW - QB) + qi) // CHUNK - kj // CHUNK
    band = (dchunk >= 0) & (dchunk <= LEFT_CHUNKS)
    return jnp.where(band[None], skew, MASK_VALUE)


def _head_masks():
    lane = lax.broadcasted_iota(jnp.int32, (1, 128), 1)
    return (lane < 64, lane >= 64)


def _att_probs(qh, k2, bias, valid):
    s = _dot(qh, k2, 'nt') * ATT_SCALE + bias
    s = jnp.where(valid, s, MASK_VALUE)
    e = jnp.exp(s - jnp.max(s, axis=-1, keepdims=True))
    return e * (1.0 / jnp.sum(e, axis=-1, keepdims=True))


def _att_specs(S, q_col):
    nb = S // QB
    q_spec = pl.BlockSpec((None, QB, ATT_WIDTH), lambda b, i: (b, jnp.minimum(i, nb - 1), q_col))

    def kv_spec(col, kb):
        return pl.BlockSpec((None, QB, ATT_WIDTH),
                            lambda b, i: (b, jnp.clip(i - 2 + kb, 0, nb - 1), col))

    return q_spec, [kv_spec(3, kb) for kb in range(3)], [kv_spec(4, kb) for kb in range(3)]


def _att_fwd(proj3, bias):
    Bl, S, _ = proj3.shape
    nb = S // QB
    q_spec, k_specs, v_specs = _att_specs(S, 2)

    def body(q_ref, k0, k1, k2r, v0, v1, v2r, bias_ref, o_ref):
        i = pl.program_id(1)
        masks = _head_masks()
        valid = lax.broadcasted_iota(jnp.int32, (QB, KW), 1) >= (2 - i) * QB
        for pr in range(ATT_HEADS // 2):
            ls = slice(128 * pr, 128 * (pr + 1))
            q2 = q_ref[:, ls]
            k2 = jnp.concatenate([k0[:, ls], k1[:, ls], k2r[:, ls]], axis=0)
            v2 = jnp.concatenate([v0[:, ls], v1[:, ls], v2r[:, ls]], axis=0)
            o2 = jnp.zeros((QB, 128), F32)
            for hh in range(2):
                p = _att_probs(jnp.where(masks[hh], q2, 0), k2, bias_ref[2 * pr + hh], valid)
                o2 = o2 + _dot(p, jnp.where(masks[hh], v2, 0))
            o_ref[:, ls] = o2.astype(BF16)

    return pl.pallas_call(
        body, name="att_fwd", grid=(Bl, nb),
        in_specs=[q_spec] + k_specs + v_specs + [pl.BlockSpec((ATT_HEADS, QB, KW), lambda b, i: (0, 0, 0))],
        out_specs=pl.BlockSpec((None, QB, ATT_WIDTH), lambda b, i: (b, i, 0)),
        out_shape=jax.ShapeDtypeStruct((Bl, S, ATT_WIDTH), BF16),
        compiler_params=_params(("parallel", "arbitrary")))(*([proj3] * 7), bias)


def _att_bwd(proj3, do, bias):
    Bl, S, _ = proj3.shape
    nb = S // QB
    q_spec, k_specs, v_specs = _att_specs(S, 2)
    do_spec = pl.BlockSpec((None, QB, ATT_WIDTH), lambda b, i: (b, jnp.minimum(i, nb - 1), 0))
    kv_out = pl.BlockSpec((None, QB, ATT_WIDTH), lambda b, i: (b, jnp.clip(i - 2, 0, nb - 1), 0))
    bias_spec = pl.BlockSpec((ATT_HEADS, QB, KW), lambda b, i: (0, 0, 0))

    def body(q_ref, k0, k1, k2r, v0, v1, v2r, do_ref, bias_ref, dq_ref, dk_ref, dv_ref, db_ref, dkw, dvw):
        b = pl.program_id(0)
        i = pl.program_id(1)

        @pl.when((b == 0) & (i == 0))
        def _():
            db_ref[...] = jnp.zeros_like(db_ref)

        @pl.when(i == 0)
        def _():
            dkw[...] = jnp.zeros_like(dkw)
            dvw[...] = jnp.zeros_like(dvw)

        @pl.when(i < nb)
        def _():
            masks = _head_masks()
            valid = lax.broadcasted_iota(jnp.int32, (QB, KW), 1) >= (2 - i) * QB
            for pr in range(ATT_HEADS // 2):
                ls = slice(128 * pr, 128 * (pr + 1))
                q2 = q_ref[:, ls]
                do2 = do_ref[:, ls]
                k2 = jnp.concatenate([k0[:, ls], k1[:, ls], k2r[:, ls]], axis=0)
                v2 = jnp.concatenate([v0[:, ls], v1[:, ls], v2r[:, ls]], axis=0)
                dq2 = jnp.zeros((QB, 128), F32)
                dk2 = jnp.zeros((KW, 128), F32)
                dv2 = jnp.zeros((KW, 128), F32)
                for hh in range(2):
                    h = 2 * pr + hh
                    qh = jnp.where(masks[hh], q2, 0)
                    doh = jnp.where(masks[hh], do2, 0)
                    p = _att_probs(qh, k2, bias_ref[h], valid)
                    dp = _dot(doh, v2, 'nt')
                    ds = p * (dp - jnp.sum(p * dp, axis=-1, keepdims=True))
                    db_ref[h] += ds
                    dq2 = dq2 + _dot(ds, jnp.where(masks[hh], k2, 0))
                    dk2 = dk2 + _dot(ds, qh, 'tn')
                    dv2 = dv2 + _dot(p, doh, 'tn')
                dq_ref[:, ls] = (dq2 * ATT_SCALE).astype(BF16)
                dkw[:, ls] += dk2 * ATT_SCALE
                dvw[:, ls] += dv2

        dk_ref[...] = dkw[0:QB, :].astype(BF16)
        dv_ref[...] = dvw[0:QB, :].astype(BF16)
        for buf in (dkw, dvw):
            rest = buf[QB:KW, :]
            buf[0:KW - QB, :] = rest
            buf[KW - QB:KW, :] = jnp.zeros((QB, ATT_WIDTH), F32)

    blk = jax.ShapeDtypeStruct((Bl, S, ATT_WIDTH), BF16)
    return pl.pallas_call(
        body, name="att_bwd", grid=(Bl, nb + 2),
        in_specs=[q_spec] + k_specs + v_specs + [do_spec, bias_spec],
        out_specs=[do_spec, kv_out, kv_out, bias_spec],
        out_shape=[blk, blk, blk, jax.ShapeDtypeStruct((ATT_HEADS, QB, KW), F32)],
        scratch_shapes=[pltpu.VMEM((KW, ATT_WIDTH), F32), pltpu.VMEM((KW, ATT_WIDTH), F32)],
        compiler_params=_params(("arbitrary", "arbitrary")))(*([proj3] * 7), do, bias)


def _rel_bias_grad(dbias):
    H = dbias.shape[0]
    Wd = KW + QB
    padded = jnp.pad(dbias, ((0, 0), (0, 1), (QB, 0)))
    skew = padded.reshape(H, (QB + 1) * Wd)[:, :QB * (Wd + 1)].reshape(H, QB, Wd + 1)[:, :, :Wd]
    c = jnp.arange(Wd)[:, None]
    bins = (jnp.clip(KW - c, -(CHUNK - 1), MAX_REL) + (CHUNK - 1) == jnp.arange(N_REL)[None, :]).astype(F32)

    def body(s_ref, bins_ref, o_ref):
        col = jnp.sum(s_ref[...], axis=1)
        o_ref[...] = jnp.dot(col, bins_ref[...], preferred_element_type=F32, precision=lax.Precision.HIGHEST)

    return pl.pallas_call(
        body, name="rel_bias_grad", grid=(1,),
        in_specs=[pl.BlockSpec((H, QB, Wd), lambda i: (0, 0, 0)), pl.BlockSpec((Wd, N_REL), lambda i: (0, 0))],
        out_specs=pl.BlockSpec((H, N_REL), lambda i: (0, 0)), out_shape=jax.ShapeDtypeStruct((H, N_REL), F32),
        compiler_params=_params(("arbitrary",)))(skew, bins)


MEM_TILE = 512


def _mem_probs(qh, kh):
    s = _dot(qh, kh, 'nt') * MEM_SCALE
    e = jnp.exp(s - jnp.max(s, axis=-1, keepdims=True))
    return e * (1.0 / jnp.sum(e, axis=-1, keepdims=True))


def _mem_fwd(proj3, kv3):
    Bl, S, _ = proj3.shape
    tq = _pick(S, (MEM_TILE, 256))
    hd = MEM_HEAD_DIM

    def body(q_ref, kv_ref, o_ref):
        for h in range(MEM_HEADS):
            p = _mem_probs(q_ref[:, h * hd:(h + 1) * hd], kv_ref[:, h * hd:(h + 1) * hd])
            o_ref[:, h * hd:(h + 1) * hd] = _dot(p, kv_ref[:, MEM_WIDTH + h * hd:MEM_WIDTH + (h + 1) * hd]).astype(BF16)

    return pl.pallas_call(
        body, name="mem_fwd", grid=(Bl, S // tq),
        in_specs=[pl.BlockSpec((None, tq, MEM_WIDTH), lambda b, i: (b, i, 5)),
                  pl.BlockSpec((None, MEM_LEN, 2 * MEM_WIDTH), lambda b, i: (b, 0, 0))],
        out_specs=pl.BlockSpec((None, tq, MEM_WIDTH), lambda b, i: (b, i, 0)),
        out_shape=jax.ShapeDtypeStruct((Bl, S, MEM_WIDTH), BF16),
        compiler_params=_params(("parallel", "parallel")))(proj3, kv3)


def _mem_bwd(proj3, kv3, do):
    Bl, S, _ = proj3.shape
    tq = _pick(S, (MEM_TILE, 256))
    hd = MEM_HEAD_DIM

    def body(q_ref, kv_ref, do_ref, dq_ref, dkv_ref):
        i = pl.program_id(1)

        @pl.when(i == 0)
        def _():
            dkv_ref[...] = jnp.zeros_like(dkv_ref)

        for h in range(MEM_HEADS):
            ks = slice(h * hd, (h + 1) * hd)
            vs = slice(MEM_WIDTH + h * hd, MEM_WIDTH + (h + 1) * hd)
            qh, kh, vh, doh = q_ref[:, ks], kv_ref[:, ks], kv_ref[:, vs], do_ref[:, ks]
            p = _mem_probs(qh, kh)
            dp = _dot(doh, vh, 'nt')
            ds = p * (dp - jnp.sum(p * dp, axis=-1, keepdims=True))
            dq_ref[:, ks] = (_dot(ds, kh) * MEM_SCALE).astype(BF16)
            dkv_ref[:, ks] += _dot(ds, qh, 'tn') * MEM_SCALE
            dkv_ref[:, vs] += _dot(p, doh, 'tn')

    return pl.pallas_call(
        body, name="mem_bwd", grid=(Bl, S // tq),
        in_specs=[pl.BlockSpec((None, tq, MEM_WIDTH), lambda b, i: (b, i, 5)),
                  pl.BlockSpec((None, MEM_LEN, 2 * MEM_WIDTH), lambda b, i: (b, 0, 0)),
                  pl.BlockSpec((None, tq, MEM_WIDTH), lambda b, i: (b, i, 0))],
        out_specs=[pl.BlockSpec((None, tq, MEM_WIDTH), lambda b, i: (b, i, 0)),
                   pl.BlockSpec((None, MEM_LEN, 2 * MEM_WIDTH), lambda b, i: (b, 0, 0))],
        out_shape=[jax.ShapeDtypeStruct((Bl, S, MEM_WIDTH), BF16),
                   jax.ShapeDtypeStruct((Bl, MEM_LEN, 2 * MEM_WIDTH), F32)],
        compiler_params=_params(("parallel", "arbitrary")))(proj3, kv3, do)


def _position():
    x, y, c = lax.axis_index("x"), lax.axis_index("y"), lax.axis_index("c")
    return x, y, c, 4 * x + 2 * y + c


def _device(idx):
    return ((idx >> 2) & 1, (idx >> 1) & 1, idx & 1)


def _half_block(ref, axis, shard_shape, k, h):
    R, Cn = shard_shape
    if axis == 1:
        return ref.at[pl.ds(h * (R // 2), R // 2), pl.ds(k * Cn, Cn)]
    return ref.at[pl.ds(k * R + h * (R // 2), R // 2), :]


def _block(ref, axis, shard_shape, k):
    R, Cn = shard_shape
    if axis == 1:
        return ref.at[:, pl.ds(k * Cn, Cn)]
    return ref.at[pl.ds(k * R, R), :]


def _half(ref, h):
    R = ref.shape[0]
    return ref.at[pl.ds(h * (R // 2), R // 2), :]


ANY = pl.BlockSpec(memory_space=pl.ANY)


HBM = pl.BlockSpec(memory_space=pltpu.HBM)
SEM = pl.BlockSpec(memory_space=pltpu.SEMAPHORE)
VMEM_WHOLE = pl.BlockSpec(memory_space=pltpu.VMEM)
EFFECT = pltpu.SideEffectType.DATAFLOW_SIDE_EFFECTING


def _in_hbm(a):
    return pltpu.with_memory_space_constraint(a, pltpu.HBM)


def _split_start(body, name, sources, lands, n_copies):
    n = len(sources)
    out_shape, out_specs = [], []
    for _ in range(n):
        out_shape += [pltpu.SemaphoreType.DMA((n_copies,)), pltpu.SemaphoreType.DMA((n_copies,))]
        out_specs += [SEM, SEM]
    out_shape += [pltpu.HBM(a.shape, a.dtype) for a in list(sources) + list(lands)]
    out_specs += [HBM] * (2 * n)
    out_shape.append(jax.ShapeDtypeStruct((8, 128), F32))
    out_specs.append(VMEM_WHOLE)

    def call_body(*refs):
        srcs, lnds = refs[:n], refs[n:2 * n]
        sems = refs[2 * n:4 * n]
        token = refs[-1]
        body(srcs, lnds, sems[0::2], sems[1::2])
        token[...] = jnp.zeros_like(token)

    res = pl.pallas_call(
        call_body, name=name, in_specs=[HBM] * (2 * n), out_specs=out_specs, out_shape=out_shape,
        input_output_aliases={i: 2 * n + i for i in range(2 * n)},
        compiler_params=pltpu.CompilerParams(has_side_effects=EFFECT))(
            *[_in_hbm(a) for a in list(sources) + list(lands)])
    pairs = [(res[2 * w], res[2 * w + 1], res[2 * n + w], res[3 * n + w]) for w in range(n)]
    return pairs, res[-1]


def _split_wait(body, name, pairs, after):
    n = len(pairs)

    def call_body(*refs):
        srcs, lnds = refs[:n], refs[n:2 * n]
        sems = refs[2 * n:4 * n]
        body(srcs, lnds, sems[0::2], sems[1::2])

    args = [_in_hbm(p[2]) for p in pairs] + [_in_hbm(p[3]) for p in pairs]
    for p in pairs:
        args += [p[0], p[1]]
    res = pl.pallas_call(
        call_body, name=name, in_specs=[HBM] * (2 * n) + [SEM] * (2 * n) + [ANY], out_specs=[HBM] * (2 * n),
        out_shape=[pltpu.HBM(a.shape, a.dtype) for a in args[:2 * n]],
        input_output_aliases={i: i for i in range(2 * n)},
        compiler_params=pltpu.CompilerParams(has_side_effects=EFFECT))(*args, after)
    return res[:n], res[n:]


def _gather_copy(srcs, lnds, send, recv, axes, shapes, w, j, me):
    chip = me >> 1
    return (pltpu.make_async_remote_copy(
        src_ref=srcs[w], dst_ref=_block(lnds[w], axes[w], shapes[w], chip), send_sem=send[w].at[j],
        recv_sem=recv[w].at[j], device_id=_device(me ^ (2 * (j + 1))), device_id_type=MESH),
            pltpu.make_async_remote_copy(
        src_ref=srcs[w], dst_ref=_block(lnds[w], axes[w], shapes[w], chip ^ (j + 1)), send_sem=send[w].at[j],
        recv_sem=recv[w].at[j], device_id=_device(me ^ (2 * (j + 1))), device_id_type=MESH))


def _gather_start(shards, lands, axes, name):
    shapes = [s.shape for s in shards]

    def body(srcs, lnds, send, recv):
        x, y, c, me = _position()
        for w in range(len(shards)):
            for j in range(3):
                _gather_copy(srcs, lnds, send, recv, axes, shapes, w, j, me)[0].start()

    return _split_start(body, name, shards, lands, 3)


def _gather_wait(pairs, axes, after, name):
    shapes = [p[2].shape for p in pairs]

    def body(srcs, lnds, send, recv):
        x, y, c, me = _position()
        for w in range(len(pairs)):
            for j in range(3):
                sent, landed = _gather_copy(srcs, lnds, send, recv, axes, shapes, w, j, me)
                sent.wait_send()
                landed.wait_recv()

    return _split_wait(body, name, pairs, after)[1]


def _shard_shape(grad, axis):
    return (grad.shape[0], grad.shape[1] // 4) if axis == 1 else (grad.shape[0] // 4, grad.shape[1])


def _scatter_copy(srcs, lnds, send, recv, axes, shapes, w, m, me):
    peer = me ^ m
    return pltpu.make_async_remote_copy(
        src_ref=_half_block(srcs[w], axes[w], shapes[w], peer >> 1, peer & 1), dst_ref=lnds[w].at[m - 1],
        send_sem=send[w].at[m - 1], recv_sem=recv[w].at[m - 1], device_id=_device(peer), device_id_type=MESH)


def _scatter_start(grads, axes, name):
    shapes = [_shard_shape(g, a) for g, a in zip(grads, axes)]
    lands = [lax.empty((N_DEV - 1, R // 2, Cn), BF16) for R, Cn in shapes]

    def body(srcs, lnds, send, recv):
        x, y, c, me = _position()
        for w in range(len(grads)):
            for m in range(1, N_DEV):
                _scatter_copy(srcs, lnds, send, recv, axes, shapes, w, m, me).start()

    return _split_start(body, name, grads, lands, N_DEV - 1)


def _scatter_wait(pairs, axes, after):
    shapes = [_shard_shape(p[2], a) for p, a in zip(pairs, axes)]

    def body(srcs, lnds, send, recv):
        x, y, c, me = _position()
        for w in range(len(pairs)):
            for m in range(1, N_DEV):
                cp = _scatter_copy(srcs, lnds, send, recv, axes, shapes, w, m, me)
                cp.wait_send()
                cp.wait_recv()

    return _split_wait(body, "scatter_wait", pairs, after)


def _sum_partials(own, parts, half, name):
    R, Cn = own.shape
    tr = _pick(R, (256, 176, 128, 64, 32, 16, 8))
    nblk = R // tr

    def body(half_ref, own_ref, p_ref, o_ref):
        acc = own_ref[...].astype(F32)
        for d in range(N_DEV - 1):
            acc = acc + p_ref[d].astype(F32)
        o_ref[...] = acc

    return pl.pallas_call(
        body, name=name,
        grid_spec=pltpu.PrefetchScalarGridSpec(
            num_scalar_prefetch=1, grid=(nblk,),
            in_specs=[pl.BlockSpec((tr, Cn), lambda i, hr: (i, 0)),
                      pl.BlockSpec((N_DEV - 1, tr, Cn), lambda i, hr: (0, i, 0))],
            out_specs=pl.BlockSpec((tr, Cn), lambda i, hr: (hr[0] * nblk + i, 0))),
        out_shape=jax.ShapeDtypeStruct((2 * R, Cn), F32),
        compiler_params=_params(("parallel",)))(half, own, parts)


def _exchange_halves(grads):
    n = len(grads)

    def body(*refs):
        outs = refs[n:2 * n]
        send, recv = refs[2 * n:]
        x, y, c, me = _position()

        def copy(w, half):
            rows = _half(outs[w], half)
            return pltpu.make_async_remote_copy(src_ref=rows, dst_ref=rows, send_sem=send.at[w],
                                                recv_sem=recv.at[w], device_id=_device(me ^ 1), device_id_type=MESH)

        for w in range(n):
            copy(w, c).start()
        for w in range(n):
            copy(w, 1 - c).wait_recv()
        for w in range(n):
            copy(w, c).wait_send()

    return pl.pallas_call(
        body, name="exchange_halves", in_specs=[ANY] * n, out_specs=[ANY] * n,
        out_shape=[jax.ShapeDtypeStruct(a.shape, a.dtype) for a in grads],
        input_output_aliases={i: i for i in range(n)},
        scratch_shapes=[pltpu.SemaphoreType.DMA((n,)), pltpu.SemaphoreType.DMA((n,))],
        compiler_params=pltpu.CompilerParams(has_side_effects=True))(*grads)


def _all_reduce_small(vec):
    R, L = vec.shape

    def body(v_ref, o_ref, buf, send, recv):
        x, y, c, me = _position()
        buf[me] = v_ref[...]

        def copy(m, slot):
            return pltpu.make_async_remote_copy(src_ref=v_ref, dst_ref=buf.at[slot], send_sem=send.at[m - 1],
                                                recv_sem=recv.at[m - 1], device_id=_device(me ^ m),
                                                device_id_type=MESH)

        for m in range(1, N_DEV):
            copy(m, me).start()
        for m in range(1, N_DEV):
            copy(m, me ^ m).wait_recv()
        for m in range(1, N_DEV):
            copy(m, me).wait_send()
        acc = buf[0]
        for d in range(1, N_DEV):
            acc = acc + buf[d]
        o_ref[...] = acc

    vm = pl.BlockSpec(memory_space=pltpu.VMEM)
    return pl.pallas_call(
        body, name="all_reduce_small", in_specs=[vm], out_specs=vm, out_shape=jax.ShapeDtypeStruct((R, L), F32),
        scratch_shapes=[pltpu.VMEM((N_DEV, R, L), F32), pltpu.SemaphoreType.DMA((N_DEV - 1,)),
                        pltpu.SemaphoreType.DMA((N_DEV - 1,))],
        compiler_params=pltpu.CompilerParams(has_side_effects=True))(vec)


def _adamw(w, g, m, v, name):
    R, Cn = w.shape
    tr = _pick(R, (256, 176, 128, 64, 40, 32, 16, 8))

    def body(w_ref, g_ref, m_ref, v_ref, d_ref, nm_ref, nv_ref):
        gv = g_ref[...]
        nm = ADAM_B1 * m_ref[...] + (1.0 - ADAM_B1) * gv
        nv = ADAM_B2 * v_ref[...] + (1.0 - ADAM_B2) * (gv * gv)
        m_hat = nm / (1.0 - ADAM_B1 ** ADAM_STEP)
        v_hat = nv / (1.0 - ADAM_B2 ** ADAM_STEP)
        d_ref[...] = -ADAM_LR * (m_hat / (jnp.sqrt(v_hat) + ADAM_EPS) + ADAM_WD * w_ref[...])
        nm_ref[...] = nm
        nv_ref[...] = nv

    spec = pl.BlockSpec((tr, Cn), lambda i: (i, 0))
    return pl.pallas_call(
        body, name=name, grid=(R // tr,), in_specs=[spec] * 4, out_specs=[spec] * 3,
        out_shape=[jax.ShapeDtypeStruct((R, Cn), F32)] * 3, compiler_params=_params(("parallel",)))(w, g, m, v)


def _pack(arrays, rows):
    flat = jnp.concatenate([a.reshape(-1).astype(F32) for a in arrays])
    return jnp.pad(flat, (0, rows * 128 - flat.shape[0])).reshape(rows, 128)


def _unpack(packed, shapes):
    flat = packed.reshape(-1)
    out, off = [], 0
    for s in shapes:
        size = 1
        for d in s:
            size *= d
        out.append(flat[off:off + size].reshape(s))
        off += size
    return out


def _ffn_fwd(x, norm, arrived, tag, after=None):
    h = _rms_fwd(x, norm, f"{tag}_norm", after=after)
    w_up = arrived(f"{tag}_w_up", h)
    a, b, act = _ffn_up(h, w_up, f"{tag}_up")
    w_down = arrived(f"{tag}_w_down", act)
    out = _mm(act, w_down, 'nn', f"{tag}_down", F32, res=x, scale=0.5)
    return out, (h, a, b, act, w_up, w_down)


def _ffn_bwd(dout, x, norm, saved, tag, send):
    h, a, b, act, w_up, w_down = saved
    g_down = _mm(act, dout, 'tn', f"{tag}_down_dw", BF16, scale=0.5)
    token = send([f"{tag}_w_down"], [g_down])
    da, db = _ffn_down_bwd(dout, w_down, a, b, f"{tag}_down_dx", after=token)
    g_up = _dw_pieces(h, [da, db], f"{tag}_up_dw")
    token = send([f"{tag}_w_up"], [g_up])
    return _dx_rms_bwd([da, db], w_up, x, norm, dout, f"{tag}_up_dx", after=token)


def kernel(x, mem, ffn1_norm, ffn1_w_up, ffn1_w_down, mix_norm, mem_norm, w_in, b_gate, conv_dw_w, conv_dw_b, conv_ln_g, conv_ln_b, conv_w_pw, att_rel_bias, att_w_o, mem_w_kv, mem_w_o, w_out, ffn2_norm, ffn2_w_up, ffn2_w_down, final_norm, loss_target, m_ffn1_norm, m_ffn1_w_up, m_ffn1_w_down, m_mix_norm, m_mem_norm, m_w_in, m_b_gate, m_conv_dw_w, m_conv_dw_b, m_conv_ln_g, m_conv_ln_b, m_conv_w_pw, m_att_rel_bias, m_att_w_o, m_mem_w_kv, m_mem_w_o, m_w_out, m_ffn2_norm, m_ffn2_w_up, m_ffn2_w_down, m_final_norm, v_ffn1_norm, v_ffn1_w_up, v_ffn1_w_down, v_mix_norm, v_mem_norm, v_w_in, v_b_gate, v_conv_dw_w, v_conv_dw_b, v_conv_ln_g, v_conv_ln_b, v_conv_w_pw, v_att_rel_bias, v_att_w_o, v_mem_w_kv, v_mem_w_o, v_w_out, v_ffn2_norm, v_ffn2_w_up, v_ffn2_w_down, v_final_norm):
    given = dict(locals())
    wts = {n: given[n] for n in WEIGHTS}
    mom1 = {n: given["m_" + n] for n in WEIGHTS}
    mom2 = {n: given["v_" + n] for n in WEIGHTS}
    Bl, S, Dm = x.shape
    T = Bl * S
    x0 = x.reshape(T, Dm)
    tgt = loss_target.reshape(T, Dm)
    mem2 = mem.reshape(Bl * MEM_LEN, Dm)

    big_names = [n for n, _ in BIG]
    big_axes = [a for _, a in BIG]
    chip = 2 * lax.axis_index("x") + lax.axis_index("y")

    core = lax.axis_index("c")
    axis_of = dict(BIG)

    gather_groups = [['ffn1_w_up'], ['ffn1_w_down'], ['w_in', 'conv_dw_w'],
                     ['mem_w_kv', 'conv_w_pw', 'att_w_o', 'mem_w_o', 'w_out'], ['ffn2_w_up'], ['ffn2_w_down']]
    gather_names = [n for grp in gather_groups for n in grp]
    gather_axes = [axis_of.get(n, 1) for n in gather_names]
    shards = [jnp.pad(conv_dw_w[0], ((0, 1), (0, 0))) if n == 'conv_dw_w' else wts[n][0].astype(BF16)
              for n in gather_names]
    def placed(sh, a):
        R, Cn = sh.shape
        buf = lax.empty((R, 4 * Cn) if a == 1 else (4 * R, Cn), sh.dtype)
        return lax.dynamic_update_slice(buf, sh, (0, chip * Cn) if a == 1 else (chip * R, 0))

    first, first_token = _gather_start(shards[:1], [placed(shards[0], gather_axes[0])], gather_axes[:1],
                                       "gather_start_first")
    lands = [placed(sh + first_token[0, 0].astype(sh.dtype), a) for sh, a in zip(shards[1:], gather_axes[1:])]
    rest, gather_token = _gather_start(shards[1:], lands, gather_axes[1:], "gather_start_rest")
    in_flight = dict(zip(gather_names, first + rest))
    full = {}

    def arrived(name, after):
        if name not in full:
            grp = next(grp for grp in gather_groups if name in grp)
            lands = _gather_wait([in_flight[n] for n in grp], [axis_of.get(n, 1) for n in grp], after,
                                 f"gather_wait_{grp[0]}")
            full.update(zip(grp, lands))
        return full[name]

    scattering = {}

    def send(names, grads):
        pairs, token = _scatter_start(grads, [axis_of[n] for n in names], f"scatter_start_{names[0]}")
        scattering.update(zip(names, pairs))
        return token

    final_g = final_norm.reshape(1, Dm)
    bias = _att_bias(att_rel_bias[0])

    x1, ffn1_saved = _ffn_fwd(x0, ffn1_norm, arrived, "ffn1", after=gather_token)
    h = _rms_fwd(x1, mix_norm, "mix_norm")
    w_in_full = arrived('w_in', h)
    dw_full = full['conv_dw_w'][:CONV_KERNEL]
    proj = _mm(h, w_in_full, 'nn', "w_in", BF16)
    proj3 = proj.reshape(Bl, S, proj.shape[1])
    cv, c_act = _conv_fwd(proj3, dw_full, conv_dw_b, conv_ln_g, conv_ln_b)
    o_att = _att_fwd(proj3, bias)
    mem_h = _rms_fwd(mem2, mem_norm, "mem_norm")
    kv = _mm(mem_h, arrived('mem_w_kv', o_att), 'nn', "mem_kv", BF16)
    kv3 = kv.reshape(Bl, MEM_LEN, 2 * MEM_WIDTH)
    o_mem = _mem_fwd(proj3, kv3)
    c_act2, o_att2, o_mem2 = c_act.reshape(T, -1), o_att.reshape(T, -1), o_mem.reshape(T, -1)
    x2, yc, ya, ym, y = _mix_fwd(c_act2, o_att2, o_mem2, proj, b_gate, x1, full['conv_w_pw'], full['att_w_o'],
                                 full['mem_w_o'], full['w_out'])
    x3, ffn2_saved = _ffn_fwd(x2, ffn2_norm, arrived, "ffn2")
    dx3, g_final, loss_vec = _final_fwd_bwd(x3, tgt, final_g)

    g = {}
    dx2, g['ffn2_norm'] = _ffn_bwd(dx3, x2, ffn2_norm, ffn2_saved, "ffn2", send)
    dyc, dya, dym, dgl, g['b_gate'], dc, doa, dom = _mix_bwd(
        dx2, yc, ya, ym, proj, b_gate, full['conv_w_pw'], full['att_w_o'], full['mem_w_o'], full['w_out'])
    token = send(['w_out', 'conv_w_pw', 'att_w_o', 'mem_w_o'],
                 [_mm(y, dx2, 'tn', "w_out_dw", BF16), _mm(c_act2, dyc, 'tn', "conv_pw_dw", BF16),
                  _mm(o_att2, dya, 'tn', "att_o_dw", BF16), _mm(o_mem2, dym, 'tn', "mem_o_dw", BF16)])
    du, g_dw, g['conv_dw_b'], g['conv_ln_g'], g['conv_ln_b'] = _conv_bwd(
        proj3, cv, dc.reshape(Bl, S, -1), dw_full, conv_ln_g, conv_ln_b)
    dq, dk, dv, dbias = _att_bwd(proj3, doa.reshape(Bl, S, -1), bias)
    g['att_rel_bias'] = _rel_bias_grad(dbias)
    dmq, dkv = _mem_bwd(proj3, kv3, dom.reshape(Bl, S, -1))
    dkv2 = dkv.reshape(Bl * MEM_LEN, 2 * MEM_WIDTH)
    g_kv = _mm(mem_h, dkv2, 'tn', "mem_kv_dw", BF16, after=token)
    dmem_h = _mm(dkv2, full['mem_w_kv'], 'nt', "mem_kv_dx", F32)
    _, g['mem_norm'] = _rms_bwd(mem2, mem_norm, dmem_h, dmem_h, "mem_norm_bwd")
    left = jnp.concatenate([du.reshape(T, -1), dq.reshape(T, -1), dk.reshape(T, -1), dv.reshape(T, -1),
                            dmq.reshape(T, -1)], axis=1)
    dproj = lax.dynamic_update_slice(dgl, left, (0, 0))
    token = send(['mem_w_kv', 'w_in'], [g_kv, _mm(h, dproj, 'tn', "w_in_dw", BF16)])
    dx1, g['mix_norm'] = _dx_rms_bwd([dproj], w_in_full, x1, mix_norm, dx2, "w_in_dx", after=token)
    dx0, g['ffn1_norm'] = _ffn_bwd(dx1, x0, ffn1_norm, ffn1_saved, "ffn1", send)
    g['final_norm'] = g_final

    sent, landed = _scatter_wait([scattering[n] for n in big_names], big_axes, dx0)
    halves = []
    half_idx = core.reshape(1).astype(jnp.int32)
    for n, a, own_full, parts in zip(big_names, big_axes, sent, landed):
        R, Cn = _shard_shape(own_full, a)
        start = (core * (R // 2), chip * Cn) if a == 1 else (chip * R + core * (R // 2), 0)
        own = lax.dynamic_slice(own_full, start, (R // 2, Cn))
        halves.append(_sum_partials(own, parts, half_idx, f"sum_{n}"))
    for n, sg in zip(big_names, _exchange_halves(halves)):
        g[n] = sg

    small_shapes = [wts[n].shape for n in SMALL]
    n_small = sum(int(wts[n].size) for n in SMALL)
    n_red = n_small + CONV_KERNEL * CONV_WIDTH
    red = _all_reduce_small(_pack([g[n] for n in SMALL] + [g_dw], -(-n_red // 1024) * 8))
    red_list = _unpack(red, small_shapes + [(CONV_KERNEL, CONV_WIDTH)])
    for n, rg in zip(SMALL, red_list[:-1]):
        g[n] = rg
    dw_cols = conv_dw_w.shape[2]
    g['conv_dw_w'] = lax.dynamic_slice(red_list[-1], (0, chip * dw_cols), (CONV_KERNEL, dw_cols))[None]

    delta, new_m, new_v = {}, {}, {}
    for n in big_names:
        g[n] = g[n][None]
        d, nm, nv = _adamw(wts[n][0], g[n][0], mom1[n][0], mom2[n][0], f"adamw_{n}")
        delta[n], new_m[n], new_v[n] = d[None], nm[None], nv[None]
    rest = SMALL + ['conv_dw_w']
    rest_shapes = [wts[n].shape for n in rest]
    rows = -(-sum(int(wts[n].size) for n in rest) // 1024) * 8
    packed = [_pack([src[n] for n in rest], rows) for src in (wts, g, mom1, mom2)]
    for out, res in zip((delta, new_m, new_v), _adamw(*packed, "adamw_small")):
        for n, a in zip(rest, _unpack(res, rest_shapes)):
            out[n] = a

    loss = lax.psum(loss_vec[0, 0], ("x", "y", "c"))
    grad_x = dx0.reshape(Bl, S, Dm)
    return (loss, grad_x, *[g[n] for n in WEIGHTS], *[delta[n] for n in WEIGHTS],
            *[new_m[n] for n in WEIGHTS], *[new_v[n] for n in WEIGHTS])
```

```python
import jax
import jax.numpy as jnp
from jax import lax
from jax.experimental import pallas as pl
from jax.experimental.pallas import tpu as pltpu

F32 = jnp.float32
BF16 = jnp.bfloat16

D_MODEL = 1024
D_FF = 2816
CHUNK = 64
LEFT_CHUNKS = 8
MAX_REL = 128
N_REL = (CHUNK - 1) + MAX_REL + 1
CONV_WIDTH = 512
CONV_KERNEL = 31
ATT_HEADS = 8
ATT_WIDTH = 512
MEM_LEN = 256
MEM_HEADS = 4
MEM_HEAD_DIM = 128
MEM_WIDTH = 512
EPS = 1e-6
MASK_VALUE = -1e30
ATT_SCALE = 64 ** -0.5
MEM_SCALE = 128 ** -0.5

ADAM_LR = 0.001
ADAM_B1 = 0.9
ADAM_B2 = 0.999
ADAM_EPS = 1e-08
ADAM_WD = 0.01
ADAM_STEP = 10

QB = 256
KW = 3 * QB
CONV_PAD = 32
CONV_TILE = 256

VMEM_LIMIT = 56 << 20

WEIGHTS = ['ffn1_norm', 'ffn1_w_up', 'ffn1_w_down', 'mix_norm', 'mem_norm', 'w_in', 'b_gate', 'conv_dw_w',
           'conv_dw_b', 'conv_ln_g', 'conv_ln_b', 'conv_w_pw', 'att_rel_bias', 'att_w_o', 'mem_w_kv', 'mem_w_o',
           'w_out', 'ffn2_norm', 'ffn2_w_up', 'ffn2_w_down', 'final_norm']
BIG = [('ffn1_w_up', 1), ('ffn1_w_down', 0), ('w_in', 1), ('conv_w_pw', 1), ('att_w_o', 1), ('mem_w_kv', 0),
       ('mem_w_o', 1), ('w_out', 0), ('ffn2_w_up', 1), ('ffn2_w_down', 0)]
SMALL = ['ffn1_norm', 'mix_norm', 'mem_norm', 'b_gate', 'conv_dw_b', 'conv_ln_g', 'conv_ln_b', 'att_rel_bias',
         'ffn2_norm', 'final_norm']
N_CHIPS = 4
N_DEV = 8
MESH = pl.DeviceIdType.MESH


def _pick(n, cands):
    for c in cands:
        if n % c == 0:
            return c
    return n


def _sig(x):
    return 0.5 * jnp.tanh(0.5 * x) + 0.5


def _params(sem=None, vmem=VMEM_LIMIT):
    return pltpu.CompilerParams(dimension_semantics=sem, vmem_limit_bytes=vmem)


def _dot(a, b, mode='nn'):
    dims = {'nn': (((1,), (0,)), ((), ())), 'nt': (((1,), (1,)), ((), ())), 'tn': (((0,), (0,)), ((), ()))}[mode]
    return lax.dot_general(a.astype(BF16), b.astype(BF16), dims, preferred_element_type=F32)


def _mm(a, b, mode, name, out_dtype, res=None, scale=1.0, after=None):
    if mode == 'nn':
        (M, C), (_, N) = a.shape, b.shape
    elif mode == 'nt':
        (M, C), (N, _) = a.shape, b.shape
    else:
        (C, M), (_, N) = a.shape, b.shape
    tm = _pick(M, (1024, 1408, 512, 256, 128))
    tn = _pick(N, (1024, 1408, 512, 256, 128))
    tc = _pick(C, (1024, 1408, 512, 256, 128))
    nk = C // tc
    if mode == 'nn':
        a_spec = pl.BlockSpec((tm, tc), lambda i, j, k: (i, k))
        b_spec = pl.BlockSpec((tc, tn), lambda i, j, k: (k, j))
    elif mode == 'nt':
        a_spec = pl.BlockSpec((tm, tc), lambda i, j, k: (i, k))
        b_spec = pl.BlockSpec((tn, tc), lambda i, j, k: (j, k))
    else:
        a_spec = pl.BlockSpec((tc, tm), lambda i, j, k: (k, i))
        b_spec = pl.BlockSpec((tc, tn), lambda i, j, k: (k, j))
    o_spec = pl.BlockSpec((tm, tn), lambda i, j, k: (i, j))
    has_res = res is not None
    has_after = after is not None

    def body(*refs):
        a_ref, b_ref = refs[:2]
        r_ref = refs[2] if has_res else None
        o_ref, acc_ref = refs[-2:]
        k = pl.program_id(2)
        p = _dot(a_ref[...], b_ref[...], mode)

        def finish(acc):
            if scale != 1.0:
                acc = acc * scale
            if r_ref is not None:
                acc = r_ref[...] + acc
            o_ref[...] = acc.astype(o_ref.dtype)

        if nk == 1:
            finish(p)
        else:
            @pl.when(k == 0)
            def _():
                acc_ref[...] = p

            @pl.when(k > 0)
            def _():
                acc_ref[...] += p

            @pl.when(k == nk - 1)
            def _():
                finish(acc_ref[...])

    in_specs = [a_spec, b_spec] + ([o_spec] if has_res else []) + ([ANY] if has_after else [])
    args = (a, b) + ((res,) if has_res else ()) + ((after,) if has_after else ())
    acc_shape = (tm, tn) if nk > 1 else (8, 128)
    return pl.pallas_call(
        body, name=name, grid=(M // tm, N // tn, nk), in_specs=in_specs, out_specs=o_spec,
        out_shape=jax.ShapeDtypeStruct((M, N), out_dtype), scratch_shapes=[pltpu.VMEM(acc_shape, F32)],
        compiler_params=_params(("parallel", "parallel", "arbitrary")))(*args)


def _row_tile(T):
    return _pick(T, (512, 256, 128, 64, 32, 16, 8))


def _rms_fwd(x, g, name, after=None):
    T, Dm = x.shape
    tm = _row_tile(T)

    def body(x_ref, g_ref, *rest):
        o_ref = rest[-1]
        xv = x_ref[...]
        r = lax.rsqrt(jnp.mean(xv * xv, axis=-1, keepdims=True) + EPS)
        o_ref[...] = ((xv * r) * g_ref[...]).astype(o_ref.dtype)

    extra = () if after is None else (after,)
    return pl.pallas_call(
        body, name=name, grid=(T // tm,),
        in_specs=[pl.BlockSpec((tm, Dm), lambda i: (i, 0)), pl.BlockSpec((1, Dm), lambda i: (0, 0))]
        + [ANY] * len(extra),
        out_specs=pl.BlockSpec((tm, Dm), lambda i: (i, 0)), out_shape=jax.ShapeDtypeStruct((T, Dm), BF16),
        compiler_params=_params(("parallel",)))(x, g, *extra)


def _rms_bwd(x, g, dh, dres, name):
    T, Dm = x.shape
    tm = _row_tile(T)

    def body(x_ref, g_ref, dh_ref, dr_ref, dx_ref, dg_ref):
        i = pl.program_id(0)
        xv = x_ref[...]
        r = lax.rsqrt(jnp.mean(xv * xv, axis=-1, keepdims=True) + EPS)
        xr = xv * r
        dh_v = dh_ref[...].astype(F32)
        dyg = dh_v * g_ref[...]
        dx = r * (dyg - xr * jnp.mean(dyg * xr, axis=-1, keepdims=True))
        dx_ref[...] = dr_ref[...] + dx

        @pl.when(i == 0)
        def _():
            dg_ref[...] = jnp.zeros_like(dg_ref)

        dg_ref[...] += jnp.sum(dh_v * xr, axis=0, keepdims=True)

    row = pl.BlockSpec((tm, Dm), lambda i: (i, 0))
    vec = pl.BlockSpec((1, Dm), lambda i: (0, 0))
    return pl.pallas_call(
        body, name=name, grid=(T // tm,), in_specs=[row, vec, row, row], out_specs=[row, vec],
        out_shape=[jax.ShapeDtypeStruct((T, Dm), F32), jax.ShapeDtypeStruct((1, Dm), F32)],
        compiler_params=_params(("arbitrary",)))(x, g, dh, dres)


def _final_fwd_bwd(x3, tgt, g):
    T, Dm = x3.shape
    tm = _row_tile(T)

    def body(x_ref, t_ref, g_ref, dx_ref, dg_ref, loss_ref):
        i = pl.program_id(0)
        xv = x_ref[...]
        gg = g_ref[...]
        r = lax.rsqrt(jnp.mean(xv * xv, axis=-1, keepdims=True) + EPS)
        xr = xv * r
        err = xr * gg - t_ref[...]
        dout = err * (1.0 / Dm)
        dyg = dout * gg
        dx_ref[...] = r * (dyg - xr * jnp.mean(dyg * xr, axis=-1, keepdims=True))

        @pl.when(i == 0)
        def _():
            dg_ref[...] = jnp.zeros_like(dg_ref)
            loss_ref[...] = jnp.zeros_like(loss_ref)

        dg_ref[...] += jnp.sum(dout * xr, axis=0, keepdims=True)
        loss_ref[...] += jnp.zeros_like(loss_ref) + (0.5 / Dm) * jnp.sum(err * err)

    row = pl.BlockSpec((tm, Dm), lambda i: (i, 0))
    vec = pl.BlockSpec((1, Dm), lambda i: (0, 0))
    one = pl.BlockSpec((1, 128), lambda i: (0, 0))
    return pl.pallas_call(
        body, name="final_fwd_bwd", grid=(T // tm,), in_specs=[row, row, vec], out_specs=[row, vec, one],
        out_shape=[jax.ShapeDtypeStruct((T, Dm), F32), jax.ShapeDtypeStruct((1, Dm), F32),
                   jax.ShapeDtypeStruct((1, 128), F32)],
        compiler_params=_params(("arbitrary",)))(x3, tgt, g)


def _ffn_up(h, w_up, name):
    T, K = h.shape
    Fh = w_up.shape[1] // 2
    tm = _pick(T, (512, 256, 128))
    tn = _pick(Fh, (1408, 512, 256, 128))
    nj = Fh // tn

    def body(h_ref, wa_ref, wb_ref, a_ref, b_ref, act_ref):
        hv = h_ref[...]
        a = _dot(hv, wa_ref[...])
        b = _dot(hv, wb_ref[...])
        a_ref[...] = a.astype(BF16)
        b_ref[...] = b.astype(BF16)
        act_ref[...] = (a * _sig(a) * b).astype(BF16)

    out = pl.BlockSpec((tm, tn), lambda i, j: (i, j))
    return pl.pallas_call(
        body, name=name, grid=(T // tm, nj),
        in_specs=[pl.BlockSpec((tm, K), lambda i, j: (i, 0)), pl.BlockSpec((K, tn), lambda i, j: (0, j)),
                  pl.BlockSpec((K, tn), lambda i, j: (0, j + nj))],
        out_specs=[out, out, out], out_shape=[jax.ShapeDtypeStruct((T, Fh), BF16)] * 3,
        compiler_params=_params(("parallel", "parallel")))(h, w_up, w_up)


def _ffn_down_bwd(dout, w_down, a, b, name, after=None):
    T, Dm = dout.shape
    Fh = w_down.shape[0]
    tm = _pick(T, (512, 256, 128))
    tn = _pick(Fh, (1408, 512, 256, 128))

    def body(d_ref, w_ref, a_ref, b_ref, *rest):
        da_ref, db_ref = rest[-2:]
        dact = _dot(d_ref[...], w_ref[...], 'nt') * 0.5
        av = a_ref[...].astype(F32)
        bv = b_ref[...].astype(F32)
        s = _sig(av)
        da_ref[...] = (dact * bv * s * (1.0 + av * (1.0 - s))).astype(BF16)
        db_ref[...] = (dact * av * s).astype(BF16)

    tile = pl.BlockSpec((tm, tn), lambda i, j: (i, j))
    extra = () if after is None else (after,)
    return pl.pallas_call(
        body, name=name, grid=(T // tm, Fh // tn),
        in_specs=[pl.BlockSpec((tm, Dm), lambda i, j: (i, 0)), pl.BlockSpec((tn, Dm), lambda i, j: (j, 0)),
                  tile, tile] + [ANY] * len(extra),
        out_specs=[tile, tile], out_shape=[jax.ShapeDtypeStruct((T, Fh), BF16)] * 2,
        compiler_params=_params(("parallel", "parallel")))(dout, w_down, a, b, *extra)


def _dx_rms_bwd(pieces, w, x, g, dres, name, after=None):
    T, Dm = x.shape
    width = pieces[0].shape[1]
    tm = _pick(T, (512, 256, 128))
    tc = _pick(width, (1408, 1024, 512, 256, 128))
    per = width // tc
    nk = per * len(pieces)
    npc = len(pieces)

    def body(*refs):
        p_refs = refs[:npc]
        w_ref, x_ref, g_ref, dr_ref = refs[npc:npc + 4]
        dx_ref, dg_ref, acc_ref = refs[-3:]
        i = pl.program_id(0)
        k = pl.program_id(1)
        lhs = p_refs[0][...]
        for p in range(1, npc):
            lhs = jnp.where(k >= p * per, p_refs[p][...], lhs)
        part = _dot(lhs, w_ref[...], 'nt')

        @pl.when(k == 0)
        def _():
            acc_ref[...] = part

        @pl.when(k > 0)
        def _():
            acc_ref[...] += part

        @pl.when((i == 0) & (k == 0))
        def _():
            dg_ref[...] = jnp.zeros_like(dg_ref)

        @pl.when(k == nk - 1)
        def _():
            dh = acc_ref[...]
            xv = x_ref[...]
            r = lax.rsqrt(jnp.mean(xv * xv, axis=-1, keepdims=True) + EPS)
            xr = xv * r
            dyg = dh * g_ref[...]
            dx_ref[...] = dr_ref[...] + r * (dyg - xr * jnp.mean(dyg * xr, axis=-1, keepdims=True))
            dg_ref[...] += jnp.sum(dh * xr, axis=0, keepdims=True)

    def piece_spec(p):
        return pl.BlockSpec((tm, tc), lambda i, k: (i, jnp.clip(k - p * per, 0, per - 1)))

    row = pl.BlockSpec((tm, Dm), lambda i, k: (i, 0))
    vec = pl.BlockSpec((1, Dm), lambda i, k: (0, 0))
    extra = () if after is None else (after,)
    return pl.pallas_call(
        body, name=name, grid=(T // tm, nk),
        in_specs=[piece_spec(p) for p in range(npc)] + [pl.BlockSpec((Dm, tc), lambda i, k: (0, k)), row, vec, row]
        + [ANY] * len(extra),
        out_specs=[row, vec], out_shape=[jax.ShapeDtypeStruct((T, Dm), F32), jax.ShapeDtypeStruct((1, Dm), F32)],
        scratch_shapes=[pltpu.VMEM((tm, Dm), F32)],
        compiler_params=_params(("arbitrary", "arbitrary")))(*pieces, w, x, g, dres, *extra)


def _dw_pieces(a, pieces, name):
    C, M = a.shape
    width = pieces[0].shape[1]
    npc = len(pieces)
    tm = _pick(M, (1024, 512, 256, 128))
    tn = _pick(width, (1408, 1024, 512, 256, 128))
    tc = _pick(C, (1024, 512, 256, 128))
    per = width // tn
    nk = C // tc

    def body(*refs):
        a_ref = refs[0]
        p_refs = refs[1:1 + npc]
        o_ref, acc_ref = refs[-2:]
        j = pl.program_id(1)
        k = pl.program_id(2)
        rhs = p_refs[0][...]
        for p in range(1, npc):
            rhs = jnp.where(j >= p * per, p_refs[p][...], rhs)
        part = _dot(a_ref[...], rhs, 'tn')

        @pl.when(k == 0)
        def _():
            acc_ref[...] = part

        @pl.when(k > 0)
        def _():
            acc_ref[...] += part

        @pl.when(k == nk - 1)
        def _():
            o_ref[...] = acc_ref[...].astype(o_ref.dtype)

    def piece_spec(p):
        return pl.BlockSpec((tc, tn), lambda i, j, k: (k, jnp.clip(j - p * per, 0, per - 1)))

    return pl.pallas_call(
        body, name=name, grid=(M // tm, per * npc, nk),
        in_specs=[pl.BlockSpec((tc, tm), lambda i, j, k: (k, i))] + [piece_spec(p) for p in range(npc)],
        out_specs=pl.BlockSpec((tm, tn), lambda i, j, k: (i, j)),
        out_shape=jax.ShapeDtypeStruct((M, width * npc), BF16), scratch_shapes=[pltpu.VMEM((tm, tn), F32)],
        compiler_params=_params(("parallel", "parallel", "arbitrary")))(a, *pieces)


def _mix_fwd(c_act, o_att, o_mem, proj, b_gate, x1, w_pw, w_o, w_mo, w_out):
    T, Dm = x1.shape
    W = c_act.shape[1]
    tm = _pick(T, (256, 128, 64, 32, 16, 8))

    def body(c_ref, oa_ref, om_ref, gl_ref, bg_ref, x1_ref, wpw_ref, wo_ref, wmo_ref, wout_ref,
             x2_ref, yc_ref, ya_ref, ym_ref, y_ref):
        yc = _dot(c_ref[...], wpw_ref[...])
        ya = _dot(oa_ref[...], wo_ref[...])
        ym = _dot(om_ref[...], wmo_ref[...])
        g = _sig(gl_ref[...].astype(F32) + bg_ref[...])
        y = g[:, :Dm] * yc + g[:, Dm:2 * Dm] * ya + g[:, 2 * Dm:] * ym
        x2_ref[...] = x1_ref[...] + _dot(y, wout_ref[...])
        yc_ref[...] = yc.astype(BF16)
        ya_ref[...] = ya.astype(BF16)
        ym_ref[...] = ym.astype(BF16)
        y_ref[...] = y.astype(BF16)

    rowW = pl.BlockSpec((tm, W), lambda i: (i, 0))
    rowD = pl.BlockSpec((tm, Dm), lambda i: (i, 0))
    full = lambda s: pl.BlockSpec(s, lambda i: (0, 0))
    return pl.pallas_call(
        body, name="mix_fwd", grid=(T // tm,),
        in_specs=[rowW, rowW, rowW, pl.BlockSpec((tm, 3 * Dm), lambda i: (i, 1)), full((1, 3 * Dm)), rowD,
                  full((W, Dm)), full((W, Dm)), full((W, Dm)), full((Dm, Dm))],
        out_specs=[rowD] * 5,
        out_shape=[jax.ShapeDtypeStruct((T, Dm), F32)] + [jax.ShapeDtypeStruct((T, Dm), BF16)] * 4,
        compiler_params=_params(("parallel",)))(c_act, o_att, o_mem, proj, b_gate, x1, w_pw, w_o, w_mo, w_out)


def _mix_bwd(dx2, yc, ya, ym, proj, b_gate, w_pw, w_o, w_mo, w_out):
    T, Dm = dx2.shape
    W = w_pw.shape[0]
    tm = _pick(T, (256, 128, 64, 32, 16, 8))

    def body(dx_ref, yc_ref, ya_ref, ym_ref, gl_ref, bg_ref, wpw_ref, wo_ref, wmo_ref, wout_ref,
             dyc_ref, dya_ref, dym_ref, dgl_ref, dbg_ref, dc_ref, doa_ref, dom_ref):
        i = pl.program_id(0)

        @pl.when(i == 0)
        def _():
            dbg_ref[...] = jnp.zeros_like(dbg_ref)

        dy = _dot(dx_ref[...], wout_ref[...], 'nt')
        g = _sig(gl_ref[...].astype(F32) + bg_ref[...])
        branches = ((yc_ref, dyc_ref, wpw_ref, dc_ref), (ya_ref, dya_ref, wo_ref, doa_ref),
                    (ym_ref, dym_ref, wmo_ref, dom_ref))
        for n, (y_ref, dyk_ref, w_ref, dk_ref) in enumerate(branches):
            gk = g[:, n * Dm:(n + 1) * Dm]
            dyk = dy * gk
            dgl = dyk * y_ref[...].astype(F32) * (1.0 - gk)
            dyk_ref[...] = dyk.astype(BF16)
            dgl_ref[:, n * Dm:(n + 1) * Dm] = dgl.astype(BF16)
            dbg_ref[:, n * Dm:(n + 1) * Dm] += jnp.sum(dgl, axis=0, keepdims=True)
            dk_ref[...] = _dot(dyk, w_ref[...], 'nt').astype(BF16)

    rowW = pl.BlockSpec((tm, W), lambda i: (i, 0))
    rowD = pl.BlockSpec((tm, Dm), lambda i: (i, 0))
    row3 = pl.BlockSpec((tm, 3 * Dm), lambda i: (i, 0))
    full = lambda s: pl.BlockSpec(s, lambda i: (0, 0))
    return pl.pallas_call(
        body, name="mix_bwd", grid=(T // tm,),
        in_specs=[rowD, rowD, rowD, rowD, pl.BlockSpec((tm, 3 * Dm), lambda i: (i, 1)), full((1, 3 * Dm)),
                  full((W, Dm)), full((W, Dm)), full((W, Dm)), full((Dm, Dm))],
        out_specs=[rowD, rowD, rowD, pl.BlockSpec((tm, 3 * Dm), lambda i: (i, 1)), full((1, 3 * Dm)),
                   rowW, rowW, rowW],
        out_shape=[jax.ShapeDtypeStruct((T, Dm), BF16)] * 3 + [jax.ShapeDtypeStruct((T, 6 * Dm), BF16),
                                                                jax.ShapeDtypeStruct((1, 3 * Dm), F32)]
        + [jax.ShapeDtypeStruct((T, W), BF16)] * 3,
        compiler_params=_params(("arbitrary",)))(dx2, yc, ya, ym, proj, b_gate, w_pw, w_o, w_mo, w_out)


def _ln_swish(cv, lg, lb):
    mu = jnp.mean(cv, axis=-1, keepdims=True)
    xc = cv - mu
    r = lax.rsqrt(jnp.mean(xc * xc, axis=-1, keepdims=True) + EPS)
    n = xc * r
    l = n * lg + lb
    return r, n, l


def _shift_copies(src, r0, win, shifts):
    win[...] = src[pl.ds(r0, CONV_TILE + CONV_PAD + 8), :]
    for s in range(8):
        shifts[s] = win[s:s + CONV_TILE + CONV_PAD, :]


def _tap(shifts, d):
    return shifts[d % 8, d - d % 8:d - d % 8 + CONV_TILE, :]


def _conv_fwd(proj3, dw_w, dw_b, ln_g, ln_b):
    Bl, S, _ = proj3.shape
    C, K, TS, PAD = CONV_WIDTH, CONV_KERNEL, CONV_TILE, CONV_PAD
    nt = S // TS

    def body(u_ref, w_ref, b_ref, lg_ref, lb_ref, cv_ref, c_ref, vbuf, win, shifts):
        vbuf[0:PAD, :] = jnp.zeros((PAD, C), F32)
        vbuf[S + PAD:S + PAD + 8, :] = jnp.zeros((8, C), F32)

        def glu(t, carry):
            r0 = pl.multiple_of(t * TS, TS)
            u = u_ref[pl.ds(r0, TS), :].astype(F32)
            vbuf[pl.ds(PAD + r0, TS), :] = u[:, :C] * _sig(u[:, C:])
            return carry

        lax.fori_loop(0, nt, glu, 0)

        def conv(t, carry):
            r0 = pl.multiple_of(t * TS, TS)
            _shift_copies(vbuf, r0, win, shifts)
            acc = jnp.zeros((TS, C), F32)
            for j in range(K):
                acc = acc + w_ref[j:j + 1, :] * _tap(shifts, PAD - (K - 1) + j)
            cv = acc + b_ref[...]
            cv_ref[pl.ds(r0, TS), :] = cv
            _, _, l = _ln_swish(cv, lg_ref[...], lb_ref[...])
            c_ref[pl.ds(r0, TS), :] = (l * _sig(l)).astype(BF16)
            return carry

        lax.fori_loop(0, nt, conv, 0)

    vec = pl.BlockSpec((1, C), lambda b: (0, 0))
    return pl.pallas_call(
        body, name="conv_fwd", grid=(Bl,),
        in_specs=[pl.BlockSpec((None, S, 2 * C), lambda b: (b, 0, 0)), pl.BlockSpec((K, C), lambda b: (0, 0)),
                  vec, vec, vec],
        out_specs=[pl.BlockSpec((None, S, C), lambda b: (b, 0, 0))] * 2,
        out_shape=[jax.ShapeDtypeStruct((Bl, S, C), F32), jax.ShapeDtypeStruct((Bl, S, C), BF16)],
        scratch_shapes=[pltpu.VMEM((S + PAD + 8, C), F32), pltpu.VMEM((TS + PAD + 8, C), F32),
                        pltpu.VMEM((8, TS + PAD, C), F32)],
        compiler_params=_params(("parallel",)))(proj3, dw_w, dw_b, ln_g, ln_b)


def _conv_bwd(proj3, cv, dc, dw_w, ln_g, ln_b):
    Bl, S, _ = proj3.shape
    C, K, TS, PAD = CONV_WIDTH, CONV_KERNEL, CONV_TILE, CONV_PAD
    nt = S // TS

    def body(u_ref, cv_ref, dc_ref, w_ref, lg_ref, lb_ref, du_ref, dw_ref, db_ref, dlg_ref, dlb_ref,
             vbuf, gbuf, win, shifts, dwacc):
        b = pl.program_id(0)

        @pl.when(b == 0)
        def _():
            dw_ref[...] = jnp.zeros_like(dw_ref)
            db_ref[...] = jnp.zeros_like(db_ref)
            dlg_ref[...] = jnp.zeros_like(dlg_ref)
            dlb_ref[...] = jnp.zeros_like(dlb_ref)

        vbuf[0:PAD, :] = jnp.zeros((PAD, C), F32)
        vbuf[S + PAD:S + PAD + 8, :] = jnp.zeros((8, C), F32)
        gbuf[S:S + PAD + 8, :] = jnp.zeros((PAD + 8, C), F32)
        dwacc[...] = jnp.zeros_like(dwacc)

        def norm_bwd(t, carry):
            r0 = pl.multiple_of(t * TS, TS)
            u = u_ref[pl.ds(r0, TS), :].astype(F32)
            vbuf[pl.ds(PAD + r0, TS), :] = u[:, :C] * _sig(u[:, C:])
            r, n, l = _ln_swish(cv_ref[pl.ds(r0, TS), :], lg_ref[...], lb_ref[...])
            s = _sig(l)
            dl = dc_ref[pl.ds(r0, TS), :].astype(F32) * s * (1.0 + l * (1.0 - s))
            dlg_ref[...] += jnp.sum(dl * n, axis=0, keepdims=True)
            dlb_ref[...] += jnp.sum(dl, axis=0, keepdims=True)
            dn = dl * lg_ref[...]
            dcv = r * (dn - jnp.mean(dn, axis=-1, keepdims=True) - n * jnp.mean(dn * n, axis=-1, keepdims=True))
            gbuf[pl.ds(r0, TS), :] = dcv
            db_ref[...] += jnp.sum(dcv, axis=0, keepdims=True)
            return carry

        lax.fori_loop(0, nt, norm_bwd, 0)

        def conv_bwd(t, carry):
            r0 = pl.multiple_of(t * TS, TS)
            _shift_copies(gbuf, r0, win, shifts)
            dv = jnp.zeros((TS, C), F32)
            for j in range(K):
                dv = dv + w_ref[j:j + 1, :] * _tap(shifts, K - 1 - j)
            u = u_ref[pl.ds(r0, TS), :].astype(F32)
            a, g = u[:, :C], u[:, C:]
            s = _sig(g)
            du_ref[pl.ds(r0, TS), 0:C] = (dv * s).astype(BF16)
            du_ref[pl.ds(r0, TS), C:2 * C] = (dv * a * s * (1.0 - s)).astype(BF16)
            dcv = gbuf[pl.ds(r0, TS), :]
            _shift_copies(vbuf, r0, win, shifts)
            for j in range(K):
                prod = dcv * _tap(shifts, PAD - (K - 1) + j)
                dwacc[j] += jnp.sum(prod.reshape(TS // 8, 8, C), axis=0)
            return carry

        lax.fori_loop(0, nt, conv_bwd, 0)
        dw_ref[...] += jnp.sum(dwacc[...], axis=1)

    vec = pl.BlockSpec((1, C), lambda b: (0, 0))
    seq = lambda w: pl.BlockSpec((None, S, w), lambda b: (b, 0, 0))
    return pl.pallas_call(
        body, name="conv_bwd", grid=(Bl,),
        in_specs=[seq(2 * C), seq(C), seq(C), pl.BlockSpec((K, C), lambda b: (0, 0)), vec, vec],
        out_specs=[seq(2 * C), pl.BlockSpec((K, C), lambda b: (0, 0)), vec, vec, vec],
        out_shape=[jax.ShapeDtypeStruct((Bl, S, 2 * C), BF16), jax.ShapeDtypeStruct((K, C), F32)]
        + [jax.ShapeDtypeStruct((1, C), F32)] * 3,
        scratch_shapes=[pltpu.VMEM((S + PAD + 8, C), F32), pltpu.VMEM((S + PAD + 8, C), F32),
                        pltpu.VMEM((TS + PAD + 8, C), F32), pltpu.VMEM((8, TS + PAD, C), F32),
                        pltpu.VMEM((K, 8, C), F32)],
        compiler_params=_params(("arbitrary",)))(proj3, cv, dc, dw_w, ln_g, ln_b)


def _att_bias(rel_bias):
    H = rel_bias.shape[0]
    Wd = KW + QB
    c = jnp.arange(Wd + 1)
    by_offset = rel_bias[:, jnp.clip(KW - c, -(CHUNK - 1), MAX_REL) + (CHUNK - 1)]
    flat = jnp.broadcast_to(by_offset[:, None, :], (H, QB, Wd + 1)).reshape(H, QB * (Wd + 1))
    skew = jnp.pad(flat, ((0, 0), (0, (QB + 1) * Wd - QB * (Wd + 1)))).reshape(H, QB + 1, Wd)[:, :QB, QB:]
    qi = jnp.arange(QB)[:, None]
    kj = jnp.arange(KW)[None, :]
    dchunk = ((KW - QB) + qi) // CHUNK - kj // CHUNK
    band = (dchunk >= 0) & (dchunk <= LEFT_CHUNKS)
    return jnp.where(band[None], skew, MASK_VALUE)


def _head_masks():
    lane = lax.broadcasted_iota(jnp.int32, (1, 128), 1)
    return (lane < 64, lane >= 64)


def _att_probs(qh, k2, bias, valid):
    s = _dot(qh, k2, 'nt') * ATT_SCALE + bias
    s = jnp.where(valid, s, MASK_VALUE)
    e = jnp.exp(s - jnp.max(s, axis=-1, keepdims=True))
    return e * (1.0 / jnp.sum(e, axis=-1, keepdims=True))


def _att_specs(S, q_col):
    nb = S // QB
    q_spec = pl.BlockSpec((None, QB, ATT_WIDTH), lambda b, i: (b, jnp.minimum(i, nb - 1), q_col))

    def kv_spec(col, kb):
        return pl.BlockSpec((None, QB, ATT_WIDTH),
                            lambda b, i: (b, jnp.clip(i - 2 + kb, 0, nb - 1), col))

    return q_spec, [kv_spec(3, kb) for kb in range(3)], [kv_spec(4, kb) for kb in range(3)]


def _att_fwd(proj3, bias):
    Bl, S, _ = proj3.shape
    nb = S // QB
    q_spec, k_specs, v_specs = _att_specs(S, 2)

    def body(q_ref, k0, k1, k2r, v0, v1, v2r, bias_ref, o_ref):
        i = pl.program_id(1)
        masks = _head_masks()
        valid = lax.broadcasted_iota(jnp.int32, (QB, KW), 1) >= (2 - i) * QB
        for pr in range(ATT_HEADS // 2):
            ls = slice(128 * pr, 128 * (pr + 1))
            q2 = q_ref[:, ls]
            k2 = jnp.concatenate([k0[:, ls], k1[:, ls], k2r[:, ls]], axis=0)
            v2 = jnp.concatenate([v0[:, ls], v1[:, ls], v2r[:, ls]], axis=0)
            o2 = jnp.zeros((QB, 128), F32)
            for hh in range(2):
                p = _att_probs(jnp.where(masks[hh], q2, 0), k2, bias_ref[2 * pr + hh], valid)
                o2 = o2 + _dot(p, jnp.where(masks[hh], v2, 0))
            o_ref[:, ls] = o2.astype(BF16)

    return pl.pallas_call(
        body, name="att_fwd", grid=(Bl, nb),
        in_specs=[q_spec] + k_specs + v_specs + [pl.BlockSpec((ATT_HEADS, QB, KW), lambda b, i: (0, 0, 0))],
        out_specs=pl.BlockSpec((None, QB, ATT_WIDTH), lambda b, i: (b, i, 0)),
        out_shape=jax.ShapeDtypeStruct((Bl, S, ATT_WIDTH), BF16),
        compiler_params=_params(("parallel", "arbitrary")))(*([proj3] * 7), bias)


def _att_bwd(proj3, do, bias):
    Bl, S, _ = proj3.shape
    nb = S // QB
    q_spec, k_specs, v_specs = _att_specs(S, 2)
    do_spec = pl.BlockSpec((None, QB, ATT_WIDTH), lambda b, i: (b, jnp.minimum(i, nb - 1), 0))
    kv_out = pl.BlockSpec((None, QB, ATT_WIDTH), lambda b, i: (b, jnp.clip(i - 2, 0, nb - 1), 0))
    bias_spec = pl.BlockSpec((ATT_HEADS, QB, KW), lambda b, i: (0, 0, 0))

    def body(q_ref, k0, k1, k2r, v0, v1, v2r, do_ref, bias_ref, dq_ref, dk_ref, dv_ref, db_ref, dkw, dvw):
        b = pl.program_id(0)
        i = pl.program_id(1)

        @pl.when((b == 0) & (i == 0))
        def _():
            db_ref[...] = jnp.zeros_like(db_ref)

        @pl.when(i == 0)
        def _():
            dkw[...] = jnp.zeros_like(dkw)
            dvw[...] = jnp.zeros_like(dvw)

        @pl.when(i < nb)
        def _():
            masks = _head_masks()
            valid = lax.broadcasted_iota(jnp.int32, (QB, KW), 1) >= (2 - i) * QB
            for pr in range(ATT_HEADS // 2):
                ls = slice(128 * pr, 128 * (pr + 1))
                q2 = q_ref[:, ls]
                do2 = do_ref[:, ls]
                k2 = jnp.concatenate([k0[:, ls], k1[:, ls], k2r[:, ls]], axis=0)
                v2 = jnp.concatenate([v0[:, ls], v1[:, ls], v2r[:, ls]], axis=0)
                dq2 = jnp.zeros((QB, 128), F32)
                dk2 = jnp.zeros((KW, 128), F32)
                dv2 = jnp.zeros((KW, 128), F32)
                for hh in range(2):
                    h = 2 * pr + hh
                    qh = jnp.where(masks[hh], q2, 0)
                    doh = jnp.where(masks[hh], do2, 0)
                    p = _att_probs(qh, k2, bias_ref[h], valid)
                    dp = _dot(doh, v2, 'nt')
                    ds = p * (dp - jnp.sum(p * dp, axis=-1, keepdims=True))
                    db_ref[h] += ds
                    dq2 = dq2 + _dot(ds, jnp.where(masks[hh], k2, 0))
                    dk2 = dk2 + _dot(ds, qh, 'tn')
                    dv2 = dv2 + _dot(p, doh, 'tn')
                dq_ref[:, ls] = (dq2 * ATT_SCALE).astype(BF16)
                dkw[:, ls] += dk2 * ATT_SCALE
                dvw[:, ls] += dv2

        dk_ref[...] = dkw[0:QB, :].astype(BF16)
        dv_ref[...] = dvw[0:QB, :].astype(BF16)
        for buf in (dkw, dvw):
            rest = buf[QB:KW, :]
            buf[0:KW - QB, :] = rest
            buf[KW - QB:KW, :] = jnp.zeros((QB, ATT_WIDTH), F32)

    blk = jax.ShapeDtypeStruct((Bl, S, ATT_WIDTH), BF16)
    return pl.pallas_call(
        body, name="att_bwd", grid=(Bl, nb + 2),
        in_specs=[q_spec] + k_specs + v_specs + [do_spec, bias_spec],
        out_specs=[do_spec, kv_out, kv_out, bias_spec],
        out_shape=[blk, blk, blk, jax.ShapeDtypeStruct((ATT_HEADS, QB, KW), F32)],
        scratch_shapes=[pltpu.VMEM((KW, ATT_WIDTH), F32), pltpu.VMEM((KW, ATT_WIDTH), F32)],
        compiler_params=_params(("arbitrary", "arbitrary")))(*([proj3] * 7), do, bias)


def _rel_bias_grad(dbias):
    H = dbias.shape[0]
    Wd = KW + QB
    padded = jnp.pad(dbias, ((0, 0), (0, 1), (QB, 0)))
    skew = padded.reshape(H, (QB + 1) * Wd)[:, :QB * (Wd + 1)].reshape(H, QB, Wd + 1)[:, :, :Wd]
    c = jnp.arange(Wd)[:, None]
    bins = (jnp.clip(KW - c, -(CHUNK - 1), MAX_REL) + (CHUNK - 1) == jnp.arange(N_REL)[None, :]).astype(F32)

    def body(s_ref, bins_ref, o_ref):
        col = jnp.sum(s_ref[...], axis=1)
        o_ref[...] = jnp.dot(col, bins_ref[...], preferred_element_type=F32, precision=lax.Precision.HIGHEST)

    return pl.pallas_call(
        body, name="rel_bias_grad", grid=(1,),
        in_specs=[pl.BlockSpec((H, QB, Wd), lambda i: (0, 0, 0)), pl.BlockSpec((Wd, N_REL), lambda i: (0, 0))],
        out_specs=pl.BlockSpec((H, N_REL), lambda i: (0, 0)), out_shape=jax.ShapeDtypeStruct((H, N_REL), F32),
        compiler_params=_params(("arbitrary",)))(skew, bins)


MEM_TILE = 512


def _mem_probs(qh, kh):
    s = _dot(qh, kh, 'nt') * MEM_SCALE
    e = jnp.exp(s - jnp.max(s, axis=-1, keepdims=True))
    return e * (1.0 / jnp.sum(e, axis=-1, keepdims=True))


def _mem_fwd(proj3, kv3):
    Bl, S, _ = proj3.shape
    tq = _pick(S, (MEM_TILE, 256))
    hd = MEM_HEAD_DIM

    def body(q_ref, kv_ref, o_ref):
        for h in range(MEM_HEADS):
            p = _mem_probs(q_ref[:, h * hd:(h + 1) * hd], kv_ref[:, h * hd:(h + 1) * hd])
            o_ref[:, h * hd:(h + 1) * hd] = _dot(p, kv_ref[:, MEM_WIDTH + h * hd:MEM_WIDTH + (h + 1) * hd]).astype(BF16)

    return pl.pallas_call(
        body, name="mem_fwd", grid=(Bl, S // tq),
        in_specs=[pl.BlockSpec((None, tq, MEM_WIDTH), lambda b, i: (b, i, 5)),
                  pl.BlockSpec((None, MEM_LEN, 2 * MEM_WIDTH), lambda b, i: (b, 0, 0))],
        out_specs=pl.BlockSpec((None, tq, MEM_WIDTH), lambda b, i: (b, i, 0)),
        out_shape=jax.ShapeDtypeStruct((Bl, S, MEM_WIDTH), BF16),
        compiler_params=_params(("parallel", "parallel")))(proj3, kv3)


def _mem_bwd(proj3, kv3, do):
    Bl, S, _ = proj3.shape
    tq = _pick(S, (MEM_TILE, 256))
    hd = MEM_HEAD_DIM

    def body(q_ref, kv_ref, do_ref, dq_ref, dkv_ref):
        i = pl.program_id(1)

        @pl.when(i == 0)
        def _():
            dkv_ref[...] = jnp.zeros_like(dkv_ref)

        for h in range(MEM_HEADS):
            ks = slice(h * hd, (h + 1) * hd)
            vs = slice(MEM_WIDTH + h * hd, MEM_WIDTH + (h + 1) * hd)
            qh, kh, vh, doh = q_ref[:, ks], kv_ref[:, ks], kv_ref[:, vs], do_ref[:, ks]
            p = _mem_probs(qh, kh)
            dp = _dot(doh, vh, 'nt')
            ds = p * (dp - jnp.sum(p * dp, axis=-1, keepdims=True))
            dq_ref[:, ks] = (_dot(ds, kh) * MEM_SCALE).astype(BF16)
            dkv_ref[:, ks] += _dot(ds, qh, 'tn') * MEM_SCALE
            dkv_ref[:, vs] += _dot(p, doh, 'tn')

    return pl.pallas_call(
        body, name="mem_bwd", grid=(Bl, S // tq),
        in_specs=[pl.BlockSpec((None, tq, MEM_WIDTH), lambda b, i: (b, i, 5)),
                  pl.BlockSpec((None, MEM_LEN, 2 * MEM_WIDTH), lambda b, i: (b, 0, 0)),
                  pl.BlockSpec((None, tq, MEM_WIDTH), lambda b, i: (b, i, 0))],
        out_specs=[pl.BlockSpec((None, tq, MEM_WIDTH), lambda b, i: (b, i, 0)),
                   pl.BlockSpec((None, MEM_LEN, 2 * MEM_WIDTH), lambda b, i: (b, 0, 0))],
        out_shape=[jax.ShapeDtypeStruct((Bl, S, MEM_WIDTH), BF16),
                   jax.ShapeDtypeStruct((Bl, MEM_LEN, 2 * MEM_WIDTH), F32)],
        compiler_params=_params(("parallel", "arbitrary")))(proj3, kv3, do)


def _position():
    x, y, c = lax.axis_index("x"), lax.axis_index("y"), lax.axis_index("c")
    return x, y, c, 4 * x + 2 * y + c


def _device(idx):
    return ((idx >> 2) & 1, (idx >> 1) & 1, idx & 1)


def _half_block(ref, axis, shard_shape, k, h):
    R, Cn = shard_shape
    if axis == 1:
        return ref.at[pl.ds(h * (R // 2), R // 2), pl.ds(k * Cn, Cn)]
    return ref.at[pl.ds(k * R + h * (R // 2), R // 2), :]


def _block(ref, axis, shard_shape, k):
    R, Cn = shard_shape
    if axis == 1:
        return ref.at[:, pl.ds(k * Cn, Cn)]
    return ref.at[pl.ds(k * R, R), :]


def _half(ref, h):
    R = ref.shape[0]
    return ref.at[pl.ds(h * (R // 2), R // 2), :]


ANY = pl.BlockSpec(memory_space=pl.ANY)


HBM = pl.BlockSpec(memory_space=pltpu.HBM)
SEM = pl.BlockSpec(memory_space=pltpu.SEMAPHORE)
VMEM_WHOLE = pl.BlockSpec(memory_space=pltpu.VMEM)
EFFECT = pltpu.SideEffectType.DATAFLOW_SIDE_EFFECTING


def _in_hbm(a):
    return pltpu.with_memory_space_constraint(a, pltpu.HBM)


def _split_start(body, name, sources, lands, n_copies):
    n = len(sources)
    out_shape, out_specs = [], []
    for _ in range(n):
        out_shape += [pltpu.SemaphoreType.DMA((n_copies,)), pltpu.SemaphoreType.DMA((n_copies,))]
        out_specs += [SEM, SEM]
    out_shape += [pltpu.HBM(a.shape, a.dtype) for a in list(sources) + list(lands)]
    out_specs += [HBM] * (2 * n)
    out_shape.append(jax.ShapeDtypeStruct((8, 128), F32))
    out_specs.append(VMEM_WHOLE)

    def call_body(*refs):
        srcs, lnds = refs[:n], refs[n:2 * n]
        sems = refs[2 * n:4 * n]
        token = refs[-1]
        body(srcs, lnds, sems[0::2], sems[1::2])
        token[...] = jnp.zeros_like(token)

    res = pl.pallas_call(
        call_body, name=name, in_specs=[HBM] * (2 * n), out_specs=out_specs, out_shape=out_shape,
        input_output_aliases={i: 2 * n + i for i in range(2 * n)},
        compiler_params=pltpu.CompilerParams(has_side_effects=EFFECT))(
            *[_in_hbm(a) for a in list(sources) + list(lands)])
    pairs = [(res[2 * w], res[2 * w + 1], res[2 * n + w], res[3 * n + w]) for w in range(n)]
    return pairs, res[-1]


def _split_wait(body, name, pairs, after):
    n = len(pairs)

    def call_body(*refs):
        srcs, lnds = refs[:n], refs[n:2 * n]
        sems = refs[2 * n:4 * n]
        body(srcs, lnds, sems[0::2], sems[1::2])

    args = [_in_hbm(p[2]) for p in pairs] + [_in_hbm(p[3]) for p in pairs]
    for p in pairs:
        args += [p[0], p[1]]
    res = pl.pallas_call(
        call_body, name=name, in_specs=[HBM] * (2 * n) + [SEM] * (2 * n) + [ANY], out_specs=[HBM] * (2 * n),
        out_shape=[pltpu.HBM(a.shape, a.dtype) for a in args[:2 * n]],
        input_output_aliases={i: i for i in range(2 * n)},
        compiler_params=pltpu.CompilerParams(has_side_effects=EFFECT))(*args, after)
    return res[:n], res[n:]


def _place_block(shard, axis, chip_idx, name, after=None):
    R, Cn = shard.shape
    tr = _pick(R, (256, 176, 128, 64, 32, 16, 8))
    nblk = R // tr

    def body(k_ref, s_ref, *rest):
        rest[-1][...] = s_ref[...]

    if axis == 1:
        out_shape, out_index = (R, 4 * Cn), lambda i, k: (i, k[0])
    else:
        out_shape, out_index = (4 * R, Cn), lambda i, k: (k[0] * nblk + i, 0)
    extra = () if after is None else (after,)
    return pl.pallas_call(
        body, name=name,
        grid_spec=pltpu.PrefetchScalarGridSpec(
            num_scalar_prefetch=1, grid=(nblk,),
            in_specs=[pl.BlockSpec((tr, Cn), lambda i, k: (i, 0))] + [ANY] * len(extra),
            out_specs=pl.BlockSpec((tr, Cn), out_index)),
        out_shape=jax.ShapeDtypeStruct(out_shape, shard.dtype),
        compiler_params=_params(("parallel",)))(chip_idx, shard, *extra)


def _gather_copy(srcs, lnds, send, recv, axes, shapes, w, j, me):
    chip = me >> 1
    return (pltpu.make_async_remote_copy(
        src_ref=srcs[w], dst_ref=_block(lnds[w], axes[w], shapes[w], chip), send_sem=send[w].at[j],
        recv_sem=recv[w].at[j], device_id=_device(me ^ (2 * (j + 1))), device_id_type=MESH),
            pltpu.make_async_remote_copy(
        src_ref=srcs[w], dst_ref=_block(lnds[w], axes[w], shapes[w], chip ^ (j + 1)), send_sem=send[w].at[j],
        recv_sem=recv[w].at[j], device_id=_device(me ^ (2 * (j + 1))), device_id_type=MESH))


def _gather_start(shards, lands, axes, name):
    shapes = [s.shape for s in shards]

    def body(srcs, lnds, send, recv):
        x, y, c, me = _position()
        for w in range(len(shards)):
            for j in range(3):
                _gather_copy(srcs, lnds, send, recv, axes, shapes, w, j, me)[0].start()

    return _split_start(body, name, shards, lands, 3)


def _gather_wait(pairs, axes, after, name):
    shapes = [p[2].shape for p in pairs]

    def body(srcs, lnds, send, recv):
        x, y, c, me = _position()
        for w in range(len(pairs)):
            for j in range(3):
                sent, landed = _gather_copy(srcs, lnds, send, recv, axes, shapes, w, j, me)
                sent.wait_send()
                landed.wait_recv()

    return _split_wait(body, name, pairs, after)[1]


def _shard_shape(grad, axis):
    return (grad.shape[0], grad.shape[1] // 4) if axis == 1 else (grad.shape[0] // 4, grad.shape[1])


def _scatter_copy(srcs, lnds, send, recv, axes, shapes, w, m, me):
    peer = me ^ m
    return pltpu.make_async_remote_copy(
        src_ref=_half_block(srcs[w], axes[w], shapes[w], peer >> 1, peer & 1), dst_ref=lnds[w].at[m - 1],
        send_sem=send[w].at[m - 1], recv_sem=recv[w].at[m - 1], device_id=_device(peer), device_id_type=MESH)


def _scatter_start(grads, axes, name):
    shapes = [_shard_shape(g, a) for g, a in zip(grads, axes)]
    lands = [lax.empty((N_DEV - 1, R // 2, Cn), BF16) for R, Cn in shapes]

    def body(srcs, lnds, send, recv):
        x, y, c, me = _position()
        for w in range(len(grads)):
            for m in range(1, N_DEV):
                _scatter_copy(srcs, lnds, send, recv, axes, shapes, w, m, me).start()

    return _split_start(body, name, grads, lands, N_DEV - 1)


def _scatter_wait(pairs, axes, after):
    shapes = [_shard_shape(p[2], a) for p, a in zip(pairs, axes)]

    def body(srcs, lnds, send, recv):
        x, y, c, me = _position()
        for w in range(len(pairs)):
            for m in range(1, N_DEV):
                cp = _scatter_copy(srcs, lnds, send, recv, axes, shapes, w, m, me)
                cp.wait_send()
                cp.wait_recv()

    return _split_wait(body, "scatter_wait", pairs, after)


def _sum_partials(own, parts, half, name):
    R, Cn = own.shape
    tr = _pick(R, (256, 176, 128, 64, 32, 16, 8))
    nblk = R // tr

    def body(half_ref, own_ref, p_ref, o_ref):
        acc = own_ref[...].astype(F32)
        for d in range(N_DEV - 1):
            acc = acc + p_ref[d].astype(F32)
        o_ref[...] = acc

    return pl.pallas_call(
        body, name=name,
        grid_spec=pltpu.PrefetchScalarGridSpec(
            num_scalar_prefetch=1, grid=(nblk,),
            in_specs=[pl.BlockSpec((tr, Cn), lambda i, hr: (i, 0)),
                      pl.BlockSpec((N_DEV - 1, tr, Cn), lambda i, hr: (0, i, 0))],
            out_specs=pl.BlockSpec((tr, Cn), lambda i, hr: (hr[0] * nblk + i, 0))),
        out_shape=jax.ShapeDtypeStruct((2 * R, Cn), F32),
        compiler_params=_params(("parallel",)))(half, own, parts)


def _exchange_halves(grads):
    n = len(grads)

    def body(*refs):
        outs = refs[n:2 * n]
        send, recv = refs[2 * n:]
        x, y, c, me = _position()

        def copy(w, half):
            rows = _half(outs[w], half)
            return pltpu.make_async_remote_copy(src_ref=rows, dst_ref=rows, send_sem=send.at[w],
                                                recv_sem=recv.at[w], device_id=_device(me ^ 1), device_id_type=MESH)

        for w in range(n):
            copy(w, c).start()
        for w in range(n):
            copy(w, 1 - c).wait_recv()
        for w in range(n):
            copy(w, c).wait_send()

    return pl.pallas_call(
        body, name="exchange_halves", in_specs=[ANY] * n, out_specs=[ANY] * n,
        out_shape=[jax.ShapeDtypeStruct(a.shape, a.dtype) for a in grads],
        input_output_aliases={i: i for i in range(n)},
        scratch_shapes=[pltpu.SemaphoreType.DMA((n,)), pltpu.SemaphoreType.DMA((n,))],
        compiler_params=pltpu.CompilerParams(has_side_effects=True))(*grads)


def _all_reduce_small(vec):
    R, L = vec.shape

    def body(v_ref, o_ref, buf, send, recv):
        x, y, c, me = _position()
        buf[me] = v_ref[...]

        def copy(m, slot):
            return pltpu.make_async_remote_copy(src_ref=v_ref, dst_ref=buf.at[slot], send_sem=send.at[m - 1],
                                                recv_sem=recv.at[m - 1], device_id=_device(me ^ m),
                                                device_id_type=MESH)

        for m in range(1, N_DEV):
            copy(m, me).start()
        for m in range(1, N_DEV):
            copy(m, me ^ m).wait_recv()
        for m in range(1, N_DEV):
            copy(m, me).wait_send()
        acc = buf[0]
        for d in range(1, N_DEV):
            acc = acc + buf[d]
        o_ref[...] = acc

    vm = pl.BlockSpec(memory_space=pltpu.VMEM)
    return pl.pallas_call(
        body, name="all_reduce_small", in_specs=[vm], out_specs=vm, out_shape=jax.ShapeDtypeStruct((R, L), F32),
        scratch_shapes=[pltpu.VMEM((N_DEV, R, L), F32), pltpu.SemaphoreType.DMA((N_DEV - 1,)),
                        pltpu.SemaphoreType.DMA((N_DEV - 1,))],
        compiler_params=pltpu.CompilerParams(has_side_effects=True))(vec)


def _adamw(w, g, m, v, name):
    R, Cn = w.shape
    tr = _pick(R, (256, 176, 128, 64, 40, 32, 16, 8))

    def body(w_ref, g_ref, m_ref, v_ref, d_ref, nm_ref, nv_ref):
        gv = g_ref[...]
        nm = ADAM_B1 * m_ref[...] + (1.0 - ADAM_B1) * gv
        nv = ADAM_B2 * v_ref[...] + (1.0 - ADAM_B2) * (gv * gv)
        m_hat = nm / (1.0 - ADAM_B1 ** ADAM_STEP)
        v_hat = nv / (1.0 - ADAM_B2 ** ADAM_STEP)
        d_ref[...] = -ADAM_LR * (m_hat / (jnp.sqrt(v_hat) + ADAM_EPS) + ADAM_WD * w_ref[...])
        nm_ref[...] = nm
        nv_ref[...] = nv

    spec = pl.BlockSpec((tr, Cn), lambda i: (i, 0))
    return pl.pallas_call(
        body, name=name, grid=(R // tr,), in_specs=[spec] * 4, out_specs=[spec] * 3,
        out_shape=[jax.ShapeDtypeStruct((R, Cn), F32)] * 3, compiler_params=_params(("parallel",)))(w, g, m, v)


def _pack(arrays, rows):
    flat = jnp.concatenate([a.reshape(-1).astype(F32) for a in arrays])
    return jnp.pad(flat, (0, rows * 128 - flat.shape[0])).reshape(rows, 128)


def _unpack(packed, shapes):
    flat = packed.reshape(-1)
    out, off = [], 0
    for s in shapes:
        size = 1
        for d in s:
            size *= d
        out.append(flat[off:off + size].reshape(s))
        off += size
    return out


def _ffn_fwd(x, norm, arrived, tag, after=None):
    h = _rms_fwd(x, norm, f"{tag}_norm", after=after)
    w_up = arrived(f"{tag}_w_up", h)
    a, b, act = _ffn_up(h, w_up, f"{tag}_up")
    w_down = arrived(f"{tag}_w_down", act)
    out = _mm(act, w_down, 'nn', f"{tag}_down", F32, res=x, scale=0.5)
    return out, (h, a, b, act, w_up, w_down)


def _ffn_bwd(dout, x, norm, saved, tag, send):
    h, a, b, act, w_up, w_down = saved
    g_down = _mm(act, dout, 'tn', f"{tag}_down_dw", BF16, scale=0.5)
    token = send([f"{tag}_w_down"], [g_down])
    da, db = _ffn_down_bwd(dout, w_down, a, b, f"{tag}_down_dx", after=token)
    g_up = _dw_pieces(h, [da, db], f"{tag}_up_dw")
    token = send([f"{tag}_w_up"], [g_up])
    return _dx_rms_bwd([da, db], w_up, x, norm, dout, f"{tag}_up_dx", after=token)


def kernel(x, mem, ffn1_norm, ffn1_w_up, ffn1_w_down, mix_norm, mem_norm, w_in, b_gate, conv_dw_w, conv_dw_b, conv_ln_g, conv_ln_b, conv_w_pw, att_rel_bias, att_w_o, mem_w_kv, mem_w_o, w_out, ffn2_norm, ffn2_w_up, ffn2_w_down, final_norm, loss_target, m_ffn1_norm, m_ffn1_w_up, m_ffn1_w_down, m_mix_norm, m_mem_norm, m_w_in, m_b_gate, m_conv_dw_w, m_conv_dw_b, m_conv_ln_g, m_conv_ln_b, m_conv_w_pw, m_att_rel_bias, m_att_w_o, m_mem_w_kv, m_mem_w_o, m_w_out, m_ffn2_norm, m_ffn2_w_up, m_ffn2_w_down, m_final_norm, v_ffn1_norm, v_ffn1_w_up, v_ffn1_w_down, v_mix_norm, v_mem_norm, v_w_in, v_b_gate, v_conv_dw_w, v_conv_dw_b, v_conv_ln_g, v_conv_ln_b, v_conv_w_pw, v_att_rel_bias, v_att_w_o, v_mem_w_kv, v_mem_w_o, v_w_out, v_ffn2_norm, v_ffn2_w_up, v_ffn2_w_down, v_final_norm):
    given = dict(locals())
    wts = {n: given[n] for n in WEIGHTS}
    mom1 = {n: given["m_" + n] for n in WEIGHTS}
    mom2 = {n: given["v_" + n] for n in WEIGHTS}
    Bl, S, Dm = x.shape
    T = Bl * S
    x0 = x.reshape(T, Dm)
    tgt = loss_target.reshape(T, Dm)
    mem2 = mem.reshape(Bl * MEM_LEN, Dm)

    big_names = [n for n, _ in BIG]
    big_axes = [a for _, a in BIG]
    chip = 2 * lax.axis_index("x") + lax.axis_index("y")

    core = lax.axis_index("c")
    axis_of = dict(BIG)

    gather_groups = [['ffn1_w_up'], ['ffn1_w_down'], ['w_in', 'conv_dw_w'],
                     ['mem_w_kv', 'conv_w_pw', 'att_w_o', 'mem_w_o', 'w_out'], ['ffn2_w_up'], ['ffn2_w_down']]
    gather_names = [n for grp in gather_groups for n in grp]
    gather_axes = [axis_of.get(n, 1) for n in gather_names]
    shards = [jnp.pad(conv_dw_w[0], ((0, 1), (0, 0))) if n == 'conv_dw_w' else wts[n][0].astype(BF16)
              for n in gather_names]
    chip_idx = chip.reshape(1).astype(jnp.int32)
    first, first_token = _gather_start(
        shards[:1], [_place_block(shards[0], gather_axes[0], chip_idx, f"place_{gather_names[0]}")],
        gather_axes[:1], "gather_start_first")
    lands = [_place_block(sh, a, chip_idx, f"place_{n}", after=first_token)
             for sh, a, n in zip(shards[1:], gather_axes[1:], gather_names[1:])]
    rest, gather_token = _gather_start(shards[1:], lands, gather_axes[1:], "gather_start_rest")
    in_flight = dict(zip(gather_names, first + rest))
    full = {}

    def arrived(name, after):
        if name not in full:
            grp = next(grp for grp in gather_groups if name in grp)
            lands = _gather_wait([in_flight[n] for n in grp], [axis_of.get(n, 1) for n in grp], after,
                                 f"gather_wait_{grp[0]}")
            full.update(zip(grp, lands))
        return full[name]

    scattering = {}

    def send(names, grads):
        pairs, token = _scatter_start(grads, [axis_of[n] for n in names], f"scatter_start_{names[0]}")
        scattering.update(zip(names, pairs))
        return token

    final_g = final_norm.reshape(1, Dm)
    bias = _att_bias(att_rel_bias[0])

    x1, ffn1_saved = _ffn_fwd(x0, ffn1_norm, arrived, "ffn1", after=gather_token)
    h = _rms_fwd(x1, mix_norm, "mix_norm")
    w_in_full = arrived('w_in', h)
    dw_full = full['conv_dw_w'][:CONV_KERNEL]
    proj = _mm(h, w_in_full, 'nn', "w_in", BF16)
    proj3 = proj.reshape(Bl, S, proj.shape[1])
    cv, c_act = _conv_fwd(proj3, dw_full, conv_dw_b, conv_ln_g, conv_ln_b)
    o_att = _att_fwd(proj3, bias)
    mem_h = _rms_fwd(mem2, mem_norm, "mem_norm")
    kv = _mm(mem_h, arrived('mem_w_kv', o_att), 'nn', "mem_kv", BF16)
    kv3 = kv.reshape(Bl, MEM_LEN, 2 * MEM_WIDTH)
    o_mem = _mem_fwd(proj3, kv3)
    c_act2, o_att2, o_mem2 = c_act.reshape(T, -1), o_att.reshape(T, -1), o_mem.reshape(T, -1)
    x2, yc, ya, ym, y = _mix_fwd(c_act2, o_att2, o_mem2, proj, b_gate, x1, full['conv_w_pw'], full['att_w_o'],
                                 full['mem_w_o'], full['w_out'])
    x3, ffn2_saved = _ffn_fwd(x2, ffn2_norm, arrived, "ffn2")
    dx3, g_final, loss_vec = _final_fwd_bwd(x3, tgt, final_g)

    g = {}
    dx2, g['ffn2_norm'] = _ffn_bwd(dx3, x2, ffn2_norm, ffn2_saved, "ffn2", send)
    dyc, dya, dym, dgl, g['b_gate'], dc, doa, dom = _mix_bwd(
        dx2, yc, ya, ym, proj, b_gate, full['conv_w_pw'], full['att_w_o'], full['mem_w_o'], full['w_out'])
    token = send(['w_out', 'conv_w_pw', 'att_w_o', 'mem_w_o'],
                 [_mm(y, dx2, 'tn', "w_out_dw", BF16), _mm(c_act2, dyc, 'tn', "conv_pw_dw", BF16),
                  _mm(o_att2, dya, 'tn', "att_o_dw", BF16), _mm(o_mem2, dym, 'tn', "mem_o_dw", BF16)])
    du, g_dw, g['conv_dw_b'], g['conv_ln_g'], g['conv_ln_b'] = _conv_bwd(
        proj3, cv, dc.reshape(Bl, S, -1), dw_full, conv_ln_g, conv_ln_b)
    dq, dk, dv, dbias = _att_bwd(proj3, doa.reshape(Bl, S, -1), bias)
    g['att_rel_bias'] = _rel_bias_grad(dbias)
    dmq, dkv = _mem_bwd(proj3, kv3, dom.reshape(Bl, S, -1))
    dkv2 = dkv.reshape(Bl * MEM_LEN, 2 * MEM_WIDTH)
    g_kv = _mm(mem_h, dkv2, 'tn', "mem_kv_dw", BF16, after=token)
    dmem_h = _mm(dkv2, full['mem_w_kv'], 'nt', "mem_kv_dx", F32)
    _, g['mem_norm'] = _rms_bwd(mem2, mem_norm, dmem_h, dmem_h, "mem_norm_bwd")
    left = jnp.concatenate([du.reshape(T, -1), dq.reshape(T, -1), dk.reshape(T, -1), dv.reshape(T, -1),
                            dmq.reshape(T, -1)], axis=1)
    dproj = lax.dynamic_update_slice(dgl, left, (0, 0))
    token = send(['mem_w_kv', 'w_in'], [g_kv, _mm(h, dproj, 'tn', "w_in_dw", BF16)])
    dx1, g['mix_norm'] = _dx_rms_bwd([dproj], w_in_full, x1, mix_norm, dx2, "w_in_dx", after=token)
    dx0, g['ffn1_norm'] = _ffn_bwd(dx1, x0, ffn1_norm, ffn1_saved, "ffn1", send)
    g['final_norm'] = g_final

    sent, landed = _scatter_wait([scattering[n] for n in big_names], big_axes, dx0)
    halves = []
    half_idx = core.reshape(1).astype(jnp.int32)
    for n, a, own_full, parts in zip(big_names, big_axes, sent, landed):
        R, Cn = _shard_shape(own_full, a)
        start = (core * (R // 2), chip * Cn) if a == 1 else (chip * R + core * (R // 2), 0)
        own = lax.dynamic_slice(own_full, start, (R // 2, Cn))
        halves.append(_sum_partials(own, parts, half_idx, f"sum_{n}"))
    for n, sg in zip(big_names, _exchange_halves(halves)):
        g[n] = sg

    small_shapes = [wts[n].shape for n in SMALL]
    n_small = sum(int(wts[n].size) for n in SMALL)
    n_red = n_small + CONV_KERNEL * CONV_WIDTH
    red = _all_reduce_small(_pack([g[n] for n in SMALL] + [g_dw], -(-n_red // 1024) * 8))
    red_list = _unpack(red, small_shapes + [(CONV_KERNEL, CONV_WIDTH)])
    for n, rg in zip(SMALL, red_list[:-1]):
        g[n] = rg
    dw_cols = conv_dw_w.shape[2]
    g['conv_dw_w'] = lax.dynamic_slice(red_list[-1], (0, chip * dw_cols), (CONV_KERNEL, dw_cols))[None]

    delta, new_m, new_v = {}, {}, {}
    for n in big_names:
        g[n] = g[n][None]
        d, nm, nv = _adamw(wts[n][0], g[n][0], mom1[n][0], mom2[n][0], f"adamw_{n}")
        delta[n], new_m[n], new_v[n] = d[None], nm[None], nv[None]
    rest = SMALL + ['conv_dw_w']
    rest_shapes = [wts[n].shape for n in rest]
    rows = -(-sum(int(wts[n].size) for n in rest) // 1024) * 8
    packed = [_pack([src[n] for n in rest], rows) for src in (wts, g, mom1, mom2)]
    for out, res in zip((delta, new_m, new_v), _adamw(*packed, "adamw_small")):
        for n, a in zip(rest, _unpack(res, rest_shapes)):
            out[n] = a

    loss = lax.psum(loss_vec[0, 0], ("x", "y", "c"))
    grad_x = dx0.reshape(Bl, S, Dm)
    return (loss, grad_x, *[g[n] for n in WEIGHTS], *[delta[n] for n in WEIGHTS],
            *[new_m[n] for n in WEIGHTS], *[new_v[n] for n in WEIGHTS])
```

```python
import jax
import jax.numpy as jnp
from jax import lax
from jax.experimental import pallas as pl
from jax.experimental.pallas import tpu as pltpu

F32 = jnp.float32
BF16 = jnp.bfloat16

D_MODEL = 1024
D_FF = 2816
CHUNK = 64
LEFT_CHUNKS = 8
MAX_REL = 128
N_REL = (CHUNK - 1) + MAX_REL + 1
CONV_WIDTH = 512
CONV_KERNEL = 31
ATT_HEADS = 8
ATT_WIDTH = 512
MEM_LEN = 256
MEM_HEADS = 4
MEM_HEAD_DIM = 128
MEM_WIDTH = 512
EPS = 1e-6
MASK_VALUE = -1e30
ATT_SCALE = 64 ** -0.5
MEM_SCALE = 128 ** -0.5

ADAM_LR = 0.001
ADAM_B1 = 0.9
ADAM_B2 = 0.999
ADAM_EPS = 1e-08
ADAM_WD = 0.01
ADAM_STEP = 10

QB = 256
KW = 3 * QB
CONV_PAD = 32
CONV_TILE = 256

VMEM_LIMIT = 56 << 20

WEIGHTS = ['ffn1_norm', 'ffn1_w_up', 'ffn1_w_down', 'mix_norm', 'mem_norm', 'w_in', 'b_gate', 'conv_dw_w',
           'conv_dw_b', 'conv_ln_g', 'conv_ln_b', 'conv_w_pw', 'att_rel_bias', 'att_w_o', 'mem_w_kv', 'mem_w_o',
           'w_out', 'ffn2_norm', 'ffn2_w_up', 'ffn2_w_down', 'final_norm']
BIG = [('ffn1_w_up', 1), ('ffn1_w_down', 0), ('w_in', 1), ('conv_w_pw', 1), ('att_w_o', 1), ('mem_w_kv', 0),
       ('mem_w_o', 1), ('w_out', 0), ('ffn2_w_up', 1), ('ffn2_w_down', 0)]
SMALL = ['ffn1_norm', 'mix_norm', 'mem_norm', 'b_gate', 'conv_dw_b', 'conv_ln_g', 'conv_ln_b', 'att_rel_bias',
         'ffn2_norm', 'final_norm']
N_CHIPS = 4
N_DEV = 8
MESH = pl.DeviceIdType.MESH


def _pick(n, cands):
    for c in cands:
        if n % c == 0:
            return c
    return n


def _sig(x):
    return 0.5 * jnp.tanh(0.5 * x) + 0.5


def _params(sem=None, vmem=VMEM_LIMIT):
    return pltpu.CompilerParams(dimension_semantics=sem, vmem_limit_bytes=vmem)


def _dot(a, b, mode='nn'):
    dims = {'nn': (((1,), (0,)), ((), ())), 'nt': (((1,), (1,)), ((), ())), 'tn': (((0,), (0,)), ((), ()))}[mode]
    return lax.dot_general(a.astype(BF16), b.astype(BF16), dims, preferred_element_type=F32)


def _mm(a, b, mode, name, out_dtype, res=None, scale=1.0, after=None):
    if mode == 'nn':
        (M, C), (_, N) = a.shape, b.shape
    elif mode == 'nt':
        (M, C), (N, _) = a.shape, b.shape
    else:
        (C, M), (_, N) = a.shape, b.shape
    tm = _pick(M, (1024, 1408, 512, 256, 128))
    tn = _pick(N, (1024, 1408, 512, 256, 128))
    tc = _pick(C, (1024, 1408, 512, 256, 128))
    nk = C // tc
    if mode == 'nn':
        a_spec = pl.BlockSpec((tm, tc), lambda i, j, k: (i, k))
        b_spec = pl.BlockSpec((tc, tn), lambda i, j, k: (k, j))
    elif mode == 'nt':
        a_spec = pl.BlockSpec((tm, tc), lambda i, j, k: (i, k))
        b_spec = pl.BlockSpec((tn, tc), lambda i, j, k: (j, k))
    else:
        a_spec = pl.BlockSpec((tc, tm), lambda i, j, k: (k, i))
        b_spec = pl.BlockSpec((tc, tn), lambda i, j, k: (k, j))
    o_spec = pl.BlockSpec((tm, tn), lambda i, j, k: (i, j))
    has_res = res is not None
    has_after = after is not None

    def body(*refs):
        a_ref, b_ref = refs[:2]
        r_ref = refs[2] if has_res else None
        o_ref, acc_ref = refs[-2:]
        k = pl.program_id(2)

        def finish(acc):
            if scale != 1.0:
                acc = acc * scale
            if r_ref is not None:
                acc = r_ref[...] + acc
            o_ref[...] = acc.astype(o_ref.dtype)

        if nk == 1:
            finish(_dot(a_ref[...], b_ref[...], mode))
        else:
            @pl.when(k == 0)
            def _():
                acc_ref[...] = jnp.zeros_like(acc_ref)

            acc_ref[...] += _dot(a_ref[...], b_ref[...], mode)

            @pl.when(k == nk - 1)
            def _():
                finish(acc_ref[...])

    in_specs = [a_spec, b_spec] + ([o_spec] if has_res else []) + ([ANY] if has_after else [])
    args = (a, b) + ((res,) if has_res else ()) + ((after,) if has_after else ())
    acc_shape = (tm, tn) if nk > 1 else (8, 128)
    return pl.pallas_call(
        body, name=name, grid=(M // tm, N // tn, nk), in_specs=in_specs, out_specs=o_spec,
        out_shape=jax.ShapeDtypeStruct((M, N), out_dtype), scratch_shapes=[pltpu.VMEM(acc_shape, F32)],
        compiler_params=_params(("parallel", "parallel", "arbitrary")))(*args)


def _row_tile(T):
    return _pick(T, (512, 256, 128, 64, 32, 16, 8))


def _rms_fwd(x, g, name, after=None):
    T, Dm = x.shape
    tm = _row_tile(T)

    def body(x_ref, g_ref, *rest):
        o_ref = rest[-1]
        xv = x_ref[...]
        r = lax.rsqrt(jnp.mean(xv * xv, axis=-1, keepdims=True) + EPS)
        o_ref[...] = ((xv * r) * g_ref[...]).astype(o_ref.dtype)

    extra = () if after is None else (after,)
    return pl.pallas_call(
        body, name=name, grid=(T // tm,),
        in_specs=[pl.BlockSpec((tm, Dm), lambda i: (i, 0)), pl.BlockSpec((1, Dm), lambda i: (0, 0))]
        + [ANY] * len(extra),
        out_specs=pl.BlockSpec((tm, Dm), lambda i: (i, 0)), out_shape=jax.ShapeDtypeStruct((T, Dm), BF16),
        compiler_params=_params(("parallel",)))(x, g, *extra)


def _rms_bwd(x, g, dh, dres, name):
    T, Dm = x.shape
    tm = _row_tile(T)

    def body(x_ref, g_ref, dh_ref, dr_ref, dx_ref, dg_ref):
        i = pl.program_id(0)
        xv = x_ref[...]
        r = lax.rsqrt(jnp.mean(xv * xv, axis=-1, keepdims=True) + EPS)
        xr = xv * r
        dh_v = dh_ref[...].astype(F32)
        dyg = dh_v * g_ref[...]
        dx = r * (dyg - xr * jnp.mean(dyg * xr, axis=-1, keepdims=True))
        dx_ref[...] = dr_ref[...] + dx

        @pl.when(i == 0)
        def _():
            dg_ref[...] = jnp.zeros_like(dg_ref)

        dg_ref[...] += jnp.sum(dh_v * xr, axis=0, keepdims=True)

    row = pl.BlockSpec((tm, Dm), lambda i: (i, 0))
    vec = pl.BlockSpec((1, Dm), lambda i: (0, 0))
    return pl.pallas_call(
        body, name=name, grid=(T // tm,), in_specs=[row, vec, row, row], out_specs=[row, vec],
        out_shape=[jax.ShapeDtypeStruct((T, Dm), F32), jax.ShapeDtypeStruct((1, Dm), F32)],
        compiler_params=_params(("arbitrary",)))(x, g, dh, dres)


def _final_fwd_bwd(x3, tgt, g):
    T, Dm = x3.shape
    tm = _row_tile(T)

    def body(x_ref, t_ref, g_ref, dx_ref, dg_ref, loss_ref):
        i = pl.program_id(0)
        xv = x_ref[...]
        gg = g_ref[...]
        r = lax.rsqrt(jnp.mean(xv * xv, axis=-1, keepdims=True) + EPS)
        xr = xv * r
        err = xr * gg - t_ref[...]
        dout = err * (1.0 / Dm)
        dyg = dout * gg
        dx_ref[...] = r * (dyg - xr * jnp.mean(dyg * xr, axis=-1, keepdims=True))

        @pl.when(i == 0)
        def _():
            dg_ref[...] = jnp.zeros_like(dg_ref)
            loss_ref[...] = jnp.zeros_like(loss_ref)

        dg_ref[...] += jnp.sum(dout * xr, axis=0, keepdims=True)
        loss_ref[...] += jnp.zeros_like(loss_ref) + (0.5 / Dm) * jnp.sum(err * err)

    row = pl.BlockSpec((tm, Dm), lambda i: (i, 0))
    vec = pl.BlockSpec((1, Dm), lambda i: (0, 0))
    one = pl.BlockSpec((1, 128), lambda i: (0, 0))
    return pl.pallas_call(
        body, name="final_fwd_bwd", grid=(T // tm,), in_specs=[row, row, vec], out_specs=[row, vec, one],
        out_shape=[jax.ShapeDtypeStruct((T, Dm), F32), jax.ShapeDtypeStruct((1, Dm), F32),
                   jax.ShapeDtypeStruct((1, 128), F32)],
        compiler_params=_params(("arbitrary",)))(x3, tgt, g)


def _ffn_up(h, w_up, name):
    T, K = h.shape
    Fh = w_up.shape[1] // 2
    tm = _pick(T, (512, 256, 128))
    tn = _pick(Fh, (1408, 512, 256, 128))
    nj = Fh // tn

    def body(h_ref, wa_ref, wb_ref, a_ref, b_ref, act_ref):
        hv = h_ref[...]
        a = _dot(hv, wa_ref[...])
        b = _dot(hv, wb_ref[...])
        a_ref[...] = a.astype(BF16)
        b_ref[...] = b.astype(BF16)
        act_ref[...] = (a * _sig(a) * b).astype(BF16)

    out = pl.BlockSpec((tm, tn), lambda i, j: (i, j))
    return pl.pallas_call(
        body, name=name, grid=(T // tm, nj),
        in_specs=[pl.BlockSpec((tm, K), lambda i, j: (i, 0)), pl.BlockSpec((K, tn), lambda i, j: (0, j)),
                  pl.BlockSpec((K, tn), lambda i, j: (0, j + nj))],
        out_specs=[out, out, out], out_shape=[jax.ShapeDtypeStruct((T, Fh), BF16)] * 3,
        compiler_params=_params(("parallel", "parallel")))(h, w_up, w_up)


def _ffn_down_bwd(dout, w_down, a, b, name, after=None):
    T, Dm = dout.shape
    Fh = w_down.shape[0]
    tm = _pick(T, (512, 256, 128))
    tn = _pick(Fh, (1408, 512, 256, 128))

    def body(d_ref, w_ref, a_ref, b_ref, *rest):
        da_ref, db_ref = rest[-2:]
        dact = _dot(d_ref[...], w_ref[...], 'nt') * 0.5
        av = a_ref[...].astype(F32)
        bv = b_ref[...].astype(F32)
        s = _sig(av)
        da_ref[...] = (dact * bv * s * (1.0 + av * (1.0 - s))).astype(BF16)
        db_ref[...] = (dact * av * s).astype(BF16)

    tile = pl.BlockSpec((tm, tn), lambda i, j: (i, j))
    extra = () if after is None else (after,)
    return pl.pallas_call(
        body, name=name, grid=(T // tm, Fh // tn),
        in_specs=[pl.BlockSpec((tm, Dm), lambda i, j: (i, 0)), pl.BlockSpec((tn, Dm), lambda i, j: (j, 0)),
                  tile, tile] + [ANY] * len(extra),
        out_specs=[tile, tile], out_shape=[jax.ShapeDtypeStruct((T, Fh), BF16)] * 2,
        compiler_params=_params(("parallel", "parallel")))(dout, w_down, a, b, *extra)


def _dx_rms_bwd(pieces, w, x, g, dres, name, after=None):
    T, Dm = x.shape
    width = pieces[0].shape[1]
    tm = _pick(T, (1024, 512, 256, 128))
    tc = _pick(width, (1408, 1024, 512, 256, 128))
    per = width // tc
    nk = per * len(pieces)
    npc = len(pieces)
    rows = _pick(tm, (256, 128))

    def body(*refs):
        p_refs = refs[:npc]
        w_ref, x_ref, g_ref, dr_ref = refs[npc:npc + 4]
        dx_ref, dg_ref, acc_ref = refs[-3:]
        i = pl.program_id(0)
        k = pl.program_id(1)
        lhs = p_refs[0][...]
        for p in range(1, npc):
            lhs = jnp.where(k >= p * per, p_refs[p][...], lhs)

        @pl.when(k == 0)
        def _():
            acc_ref[...] = jnp.zeros_like(acc_ref)

        @pl.when((i == 0) & (k == 0))
        def _():
            dg_ref[...] = jnp.zeros_like(dg_ref)

        acc_ref[...] += _dot(lhs, w_ref[...], 'nt')

        @pl.when(k == nk - 1)
        def _():
            def chunk(c, carry):
                rs = pl.ds(pl.multiple_of(c * rows, rows), rows)
                dh = acc_ref[rs, :]
                xv = x_ref[rs, :]
                r = lax.rsqrt(jnp.mean(xv * xv, axis=-1, keepdims=True) + EPS)
                xr = xv * r
                dyg = dh * g_ref[...]
                dx_ref[rs, :] = dr_ref[rs, :] + r * (dyg - xr * jnp.mean(dyg * xr, axis=-1, keepdims=True))
                dg_ref[...] += jnp.sum(dh * xr, axis=0, keepdims=True)
                return carry

            lax.fori_loop(0, tm // rows, chunk, 0)

    def piece_spec(p):
        return pl.BlockSpec((tm, tc), lambda i, k: (i, jnp.clip(k - p * per, 0, per - 1)))

    once = pl.BlockSpec((tm, Dm), lambda i, k: (i, 0), pipeline_mode=pl.Buffered(1))
    row = pl.BlockSpec((tm, Dm), lambda i, k: (i, 0))
    vec = pl.BlockSpec((1, Dm), lambda i, k: (0, 0))
    extra = () if after is None else (after,)
    return pl.pallas_call(
        body, name=name, grid=(T // tm, nk),
        in_specs=[piece_spec(p) for p in range(npc)] + [pl.BlockSpec((Dm, tc), lambda i, k: (0, k)), once, vec, once]
        + [ANY] * len(extra),
        out_specs=[row, vec], out_shape=[jax.ShapeDtypeStruct((T, Dm), F32), jax.ShapeDtypeStruct((1, Dm), F32)],
        scratch_shapes=[pltpu.VMEM((tm, Dm), F32)],
        compiler_params=_params(("arbitrary", "arbitrary")))(*pieces, w, x, g, dres, *extra)


def _dw_pieces(a, pieces, name):
    C, M = a.shape
    width = pieces[0].shape[1]
    npc = len(pieces)
    tm = _pick(M, (1024, 512, 256, 128))
    tn = _pick(width, (1408, 1024, 512, 256, 128))
    tc = _pick(C, (1024, 512, 256, 128))
    per = width // tn
    nk = C // tc

    def body(*refs):
        a_ref = refs[0]
        p_refs = refs[1:1 + npc]
        o_ref, acc_ref = refs[-2:]
        j = pl.program_id(1)
        k = pl.program_id(2)
        rhs = p_refs[0][...]
        for p in range(1, npc):
            rhs = jnp.where(j >= p * per, p_refs[p][...], rhs)

        @pl.when(k == 0)
        def _():
            acc_ref[...] = jnp.zeros_like(acc_ref)

        acc_ref[...] += _dot(a_ref[...], rhs, 'tn')

        @pl.when(k == nk - 1)
        def _():
            o_ref[...] = acc_ref[...].astype(o_ref.dtype)

    def piece_spec(p):
        return pl.BlockSpec((tc, tn), lambda i, j, k: (k, jnp.clip(j - p * per, 0, per - 1)))

    return pl.pallas_call(
        body, name=name, grid=(M // tm, per * npc, nk),
        in_specs=[pl.BlockSpec((tc, tm), lambda i, j, k: (k, i))] + [piece_spec(p) for p in range(npc)],
        out_specs=pl.BlockSpec((tm, tn), lambda i, j, k: (i, j)),
        out_shape=jax.ShapeDtypeStruct((M, width * npc), BF16), scratch_shapes=[pltpu.VMEM((tm, tn), F32)],
        compiler_params=_params(("parallel", "parallel", "arbitrary")))(a, *pieces)


def _mix_fwd(c_act, o_att, o_mem, proj, b_gate, x1, w_pw, w_o, w_mo, w_out):
    T, Dm = x1.shape
    W = c_act.shape[1]
    tm = _pick(T, (256, 128, 64, 32, 16, 8))

    def body(c_ref, oa_ref, om_ref, gl_ref, bg_ref, x1_ref, wpw_ref, wo_ref, wmo_ref, wout_ref,
             x2_ref, yc_ref, ya_ref, ym_ref, y_ref):
        yc = _dot(c_ref[...], wpw_ref[...])
        ya = _dot(oa_ref[...], wo_ref[...])
        ym = _dot(om_ref[...], wmo_ref[...])
        g = _sig(gl_ref[...].astype(F32) + bg_ref[...])
        y = g[:, :Dm] * yc + g[:, Dm:2 * Dm] * ya + g[:, 2 * Dm:] * ym
        x2_ref[...] = x1_ref[...] + _dot(y, wout_ref[...])
        yc_ref[...] = yc.astype(BF16)
        ya_ref[...] = ya.astype(BF16)
        ym_ref[...] = ym.astype(BF16)
        y_ref[...] = y.astype(BF16)

    rowW = pl.BlockSpec((tm, W), lambda i: (i, 0))
    rowD = pl.BlockSpec((tm, Dm), lambda i: (i, 0))
    full = lambda s: pl.BlockSpec(s, lambda i: (0, 0))
    return pl.pallas_call(
        body, name="mix_fwd", grid=(T // tm,),
        in_specs=[rowW, rowW, rowW, pl.BlockSpec((tm, 3 * Dm), lambda i: (i, 1)), full((1, 3 * Dm)), rowD,
                  full((W, Dm)), full((W, Dm)), full((W, Dm)), full((Dm, Dm))],
        out_specs=[rowD] * 5,
        out_shape=[jax.ShapeDtypeStruct((T, Dm), F32)] + [jax.ShapeDtypeStruct((T, Dm), BF16)] * 4,
        compiler_params=_params(("parallel",)))(c_act, o_att, o_mem, proj, b_gate, x1, w_pw, w_o, w_mo, w_out)


def _mix_bwd(dx2, yc, ya, ym, proj, b_gate, w_pw, w_o, w_mo, w_out):
    T, Dm = dx2.shape
    W = w_pw.shape[0]
    tm = _pick(T, (256, 128, 64, 32, 16, 8))

    def body(dx_ref, yc_ref, ya_ref, ym_ref, gl_ref, bg_ref, wpw_ref, wo_ref, wmo_ref, wout_ref,
             dyc_ref, dya_ref, dym_ref, dgl_ref, dbg_ref, dc_ref, doa_ref, dom_ref):
        i = pl.program_id(0)

        @pl.when(i == 0)
        def _():
            dbg_ref[...] = jnp.zeros_like(dbg_ref)

        dy = _dot(dx_ref[...], wout_ref[...], 'nt')
        g = _sig(gl_ref[...].astype(F32) + bg_ref[...])
        branches = ((yc_ref, dyc_ref, wpw_ref, dc_ref), (ya_ref, dya_ref, wo_ref, doa_ref),
                    (ym_ref, dym_ref, wmo_ref, dom_ref))
        for n, (y_ref, dyk_ref, w_ref, dk_ref) in enumerate(branches):
            gk = g[:, n * Dm:(n + 1) * Dm]
            dyk = dy * gk
            dgl = dyk * y_ref[...].astype(F32) * (1.0 - gk)
            dyk_ref[...] = dyk.astype(BF16)
            dgl_ref[:, n * Dm:(n + 1) * Dm] = dgl.astype(BF16)
            dbg_ref[:, n * Dm:(n + 1) * Dm] += jnp.sum(dgl, axis=0, keepdims=True)
            dk_ref[...] = _dot(dyk, w_ref[...], 'nt').astype(BF16)

    rowW = pl.BlockSpec((tm, W), lambda i: (i, 0))
    rowD = pl.BlockSpec((tm, Dm), lambda i: (i, 0))
    row3 = pl.BlockSpec((tm, 3 * Dm), lambda i: (i, 0))
    full = lambda s: pl.BlockSpec(s, lambda i: (0, 0))
    return pl.pallas_call(
        body, name="mix_bwd", grid=(T // tm,),
        in_specs=[rowD, rowD, rowD, rowD, pl.BlockSpec((tm, 3 * Dm), lambda i: (i, 1)), full((1, 3 * Dm)),
                  full((W, Dm)), full((W, Dm)), full((W, Dm)), full((Dm, Dm))],
        out_specs=[rowD, rowD, rowD, pl.BlockSpec((tm, 3 * Dm), lambda i: (i, 1)), full((1, 3 * Dm)),
                   rowW, rowW, rowW],
        out_shape=[jax.ShapeDtypeStruct((T, Dm), BF16)] * 3 + [jax.ShapeDtypeStruct((T, 6 * Dm), BF16),
                                                                jax.ShapeDtypeStruct((1, 3 * Dm), F32)]
        + [jax.ShapeDtypeStruct((T, W), BF16)] * 3,
        compiler_params=_params(("arbitrary",)))(dx2, yc, ya, ym, proj, b_gate, w_pw, w_o, w_mo, w_out)


def _ln_swish(cv, lg, lb):
    mu = jnp.mean(cv, axis=-1, keepdims=True)
    xc = cv - mu
    r = lax.rsqrt(jnp.mean(xc * xc, axis=-1, keepdims=True) + EPS)
    n = xc * r
    l = n * lg + lb
    return r, n, l


def _shift_copies(src, r0, win, shifts):
    win[...] = src[pl.ds(r0, CONV_TILE + CONV_PAD + 8), :]
    for s in range(8):
        shifts[s] = win[s:s + CONV_TILE + CONV_PAD, :]


def _tap(shifts, d):
    return shifts[d % 8, d - d % 8:d - d % 8 + CONV_TILE, :]


def _conv_fwd(proj3, dw_w, dw_b, ln_g, ln_b):
    Bl, S, _ = proj3.shape
    C, K, TS, PAD = CONV_WIDTH, CONV_KERNEL, CONV_TILE, CONV_PAD
    nt = S // TS

    def body(u_ref, w_ref, b_ref, lg_ref, lb_ref, cv_ref, c_ref, vbuf, win, shifts):
        vbuf[0:PAD, :] = jnp.zeros((PAD, C), F32)
        vbuf[S + PAD:S + PAD + 8, :] = jnp.zeros((8, C), F32)

        def glu(t, carry):
            r0 = pl.multiple_of(t * TS, TS)
            u = u_ref[pl.ds(r0, TS), :].astype(F32)
            vbuf[pl.ds(PAD + r0, TS), :] = u[:, :C] * _sig(u[:, C:])
            return carry

        lax.fori_loop(0, nt, glu, 0)

        def conv(t, carry):
            r0 = pl.multiple_of(t * TS, TS)
            _shift_copies(vbuf, r0, win, shifts)
            acc = jnp.zeros((TS, C), F32)
            for j in range(K):
                acc = acc + w_ref[j:j + 1, :] * _tap(shifts, PAD - (K - 1) + j)
            cv = acc + b_ref[...]
            cv_ref[pl.ds(r0, TS), :] = cv
            _, _, l = _ln_swish(cv, lg_ref[...], lb_ref[...])
            c_ref[pl.ds(r0, TS), :] = (l * _sig(l)).astype(BF16)
            return carry

        lax.fori_loop(0, nt, conv, 0)

    vec = pl.BlockSpec((1, C), lambda b: (0, 0))
    return pl.pallas_call(
        body, name="conv_fwd", grid=(Bl,),
        in_specs=[pl.BlockSpec((None, S, 2 * C), lambda b: (b, 0, 0)), pl.BlockSpec((K, C), lambda b: (0, 0)),
                  vec, vec, vec],
        out_specs=[pl.BlockSpec((None, S, C), lambda b: (b, 0, 0))] * 2,
        out_shape=[jax.ShapeDtypeStruct((Bl, S, C), F32), jax.ShapeDtypeStruct((Bl, S, C), BF16)],
        scratch_shapes=[pltpu.VMEM((S + PAD + 8, C), F32), pltpu.VMEM((TS + PAD + 8, C), F32),
                        pltpu.VMEM((8, TS + PAD, C), F32)],
        compiler_params=_params(("parallel",)))(proj3, dw_w, dw_b, ln_g, ln_b)


def _conv_bwd(proj3, cv, dc, dw_w, ln_g, ln_b):
    Bl, S, _ = proj3.shape
    C, K, TS, PAD = CONV_WIDTH, CONV_KERNEL, CONV_TILE, CONV_PAD
    nt = S // TS

    def body(u_ref, cv_ref, dc_ref, w_ref, lg_ref, lb_ref, du_ref, dw_ref, db_ref, dlg_ref, dlb_ref,
             vbuf, gbuf, win, shifts, dwacc):
        b = pl.program_id(0)

        @pl.when(b == 0)
        def _():
            dw_ref[...] = jnp.zeros_like(dw_ref)
            db_ref[...] = jnp.zeros_like(db_ref)
            dlg_ref[...] = jnp.zeros_like(dlg_ref)
            dlb_ref[...] = jnp.zeros_like(dlb_ref)

        vbuf[0:PAD, :] = jnp.zeros((PAD, C), F32)
        vbuf[S + PAD:S + PAD + 8, :] = jnp.zeros((8, C), F32)
        gbuf[S:S + PAD + 8, :] = jnp.zeros((PAD + 8, C), F32)
        dwacc[...] = jnp.zeros_like(dwacc)

        def norm_bwd(t, carry):
            r0 = pl.multiple_of(t * TS, TS)
            u = u_ref[pl.ds(r0, TS), :].astype(F32)
            vbuf[pl.ds(PAD + r0, TS), :] = u[:, :C] * _sig(u[:, C:])
            r, n, l = _ln_swish(cv_ref[pl.ds(r0, TS), :], lg_ref[...], lb_ref[...])
            s = _sig(l)
            dl = dc_ref[pl.ds(r0, TS), :].astype(F32) * s * (1.0 + l * (1.0 - s))
            dlg_ref[...] += jnp.sum(dl * n, axis=0, keepdims=True)
            dlb_ref[...] += jnp.sum(dl, axis=0, keepdims=True)
            dn = dl * lg_ref[...]
            dcv = r * (dn - jnp.mean(dn, axis=-1, keepdims=True) - n * jnp.mean(dn * n, axis=-1, keepdims=True))
            gbuf[pl.ds(r0, TS), :] = dcv
            db_ref[...] += jnp.sum(dcv, axis=0, keepdims=True)
            return carry

        lax.fori_loop(0, nt, norm_bwd, 0)

        def conv_bwd(t, carry):
            r0 = pl.multiple_of(t * TS, TS)
            _shift_copies(gbuf, r0, win, shifts)
            dv = jnp.zeros((TS, C), F32)
            for j in range(K):
                dv = dv + w_ref[j:j + 1, :] * _tap(shifts, K - 1 - j)
            u = u_ref[pl.ds(r0, TS), :].astype(F32)
            a, g = u[:, :C], u[:, C:]
            s = _sig(g)
            du_ref[pl.ds(r0, TS), 0:C] = (dv * s).astype(BF16)
            du_ref[pl.ds(r0, TS), C:2 * C] = (dv * a * s * (1.0 - s)).astype(BF16)
            dcv = gbuf[pl.ds(r0, TS), :]
            _shift_copies(vbuf, r0, win, shifts)
            for j in range(K):
                prod = dcv * _tap(shifts, PAD - (K - 1) + j)
                dwacc[j] += jnp.sum(prod.reshape(TS // 8, 8, C), axis=0)
            return carry

        lax.fori_loop(0, nt, conv_bwd, 0)
        dw_ref[...] += jnp.sum(dwacc[...], axis=1)

    vec = pl.BlockSpec((1, C), lambda b: (0, 0))
    seq = lambda w: pl.BlockSpec((None, S, w), lambda b: (b, 0, 0))
    return pl.pallas_call(
        body, name="conv_bwd", grid=(Bl,),
        in_specs=[seq(2 * C), seq(C), seq(C), pl.BlockSpec((K, C), lambda b: (0, 0)), vec, vec],
        out_specs=[seq(2 * C), pl.BlockSpec((K, C), lambda b: (0, 0)), vec, vec, vec],
        out_shape=[jax.ShapeDtypeStruct((Bl, S, 2 * C), BF16), jax.ShapeDtypeStruct((K, C), F32)]
        + [jax.ShapeDtypeStruct((1, C), F32)] * 3,
        scratch_shapes=[pltpu.VMEM((S + PAD + 8, C), F32), pltpu.VMEM((S + PAD + 8, C), F32),
                        pltpu.VMEM((TS + PAD + 8, C), F32), pltpu.VMEM((8, TS + PAD, C), F32),
                        pltpu.VMEM((K, 8, C), F32)],
        compiler_params=_params(("arbitrary",)))(proj3, cv, dc, dw_w, ln_g, ln_b)


def _att_bias(rel_bias):
    H = rel_bias.shape[0]
    Wd = KW + QB
    c = jnp.arange(Wd + 1)
    by_offset = rel_bias[:, jnp.clip(KW - c, -(CHUNK - 1), MAX_REL) + (CHUNK - 1)]
    flat = jnp.broadcast_to(by_offset[:, None, :], (H, QB, Wd + 1)).reshape(H, QB * (Wd + 1))
    skew = jnp.pad(flat, ((0, 0), (0, (QB + 1) * Wd - QB * (Wd + 1)))).reshape(H, QB + 1, Wd)[:, :QB, QB:]
    qi = jnp.arange(QB)[:, None]
    kj = jnp.arange(KW)[None, :]
    dchunk = ((KW - QB) + qi) // CHUNK - kj // CHUNK
    band = (dchunk >= 0) & (dchunk <= LEFT_CHUNKS)
    return jnp.where(band[None], skew, MASK_VALUE)


def _head_masks():
    lane = lax.broadcasted_iota(jnp.int32, (1, 128), 1)
    return (lane < 64, lane >= 64)


def _att_probs(qh, k2, bias, valid):
    s = _dot(qh, k2, 'nt') * ATT_SCALE + bias
    s = jnp.where(valid, s, MASK_VALUE)
    e = jnp.exp(s - jnp.max(s, axis=-1, keepdims=True))
    return e * (1.0 / jnp.sum(e, axis=-1, keepdims=True))


def _att_specs(S, q_col):
    nb = S // QB
    q_spec = pl.BlockSpec((None, QB, ATT_WIDTH), lambda b, i: (b, jnp.minimum(i, nb - 1), q_col))

    def kv_spec(col, kb):
        return pl.BlockSpec((None, QB, ATT_WIDTH),
                            lambda b, i: (b, jnp.clip(i - 2 + kb, 0, nb - 1), col))

    return q_spec, [kv_spec(3, kb) for kb in range(3)], [kv_spec(4, kb) for kb in range(3)]


def _att_fwd(proj3, bias):
    Bl, S, _ = proj3.shape
    nb = S // QB
    q_spec, k_specs, v_specs = _att_specs(S, 2)

    def body(q_ref, k0, k1, k2r, v0, v1, v2r, bias_ref, o_ref):
        i = pl.program_id(1)
        masks = _head_masks()
        valid = lax.broadcasted_iota(jnp.int32, (QB, KW), 1) >= (2 - i) * QB
        for pr in range(ATT_HEADS // 2):
            ls = slice(128 * pr, 128 * (pr + 1))
            q2 = q_ref[:, ls]
            k2 = jnp.concatenate([k0[:, ls], k1[:, ls], k2r[:, ls]], axis=0)
            v2 = jnp.concatenate([v0[:, ls], v1[:, ls], v2r[:, ls]], axis=0)
            o2 = jnp.zeros((QB, 128), F32)
            for hh in range(2):
                p = _att_probs(jnp.where(masks[hh], q2, 0), k2, bias_ref[2 * pr + hh], valid)
                o2 = o2 + _dot(p, jnp.where(masks[hh], v2, 0))
            o_ref[:, ls] = o2.astype(BF16)

    return pl.pallas_call(
        body, name="att_fwd", grid=(Bl, nb),
        in_specs=[q_spec] + k_specs + v_specs + [pl.BlockSpec((ATT_HEADS, QB, KW), lambda b, i: (0, 0, 0))],
        out_specs=pl.BlockSpec((None, QB, ATT_WIDTH), lambda b, i: (b, i, 0)),
        out_shape=jax.ShapeDtypeStruct((Bl, S, ATT_WIDTH), BF16),
        compiler_params=_params(("parallel", "arbitrary")))(*([proj3] * 7), bias)


def _att_bwd(proj3, do, bias):
    Bl, S, _ = proj3.shape
    nb = S // QB
    q_spec, k_specs, v_specs = _att_specs(S, 2)
    do_spec = pl.BlockSpec((None, QB, ATT_WIDTH), lambda b, i: (b, jnp.minimum(i, nb - 1), 0))
    kv_out = pl.BlockSpec((None, QB, ATT_WIDTH), lambda b, i: (b, jnp.clip(i - 2, 0, nb - 1), 0))
    bias_spec = pl.BlockSpec((ATT_HEADS, QB, KW), lambda b, i: (0, 0, 0))

    def body(q_ref, k0, k1, k2r, v0, v1, v2r, do_ref, bias_ref, dq_ref, dk_ref, dv_ref, db_ref, dkw, dvw):
        b = pl.program_id(0)
        i = pl.program_id(1)

        @pl.when((b == 0) & (i == 0))
        def _():
            db_ref[...] = jnp.zeros_like(db_ref)

        @pl.when(i == 0)
        def _():
            dkw[...] = jnp.zeros_like(dkw)
            dvw[...] = jnp.zeros_like(dvw)

        @pl.when(i < nb)
        def _():
            masks = _head_masks()
            valid = lax.broadcasted_iota(jnp.int32, (QB, KW), 1) >= (2 - i) * QB
            for pr in range(ATT_HEADS // 2):
                ls = slice(128 * pr, 128 * (pr + 1))
                q2 = q_ref[:, ls]
                do2 = do_ref[:, ls]
                k2 = jnp.concatenate([k0[:, ls], k1[:, ls], k2r[:, ls]], axis=0)
                v2 = jnp.concatenate([v0[:, ls], v1[:, ls], v2r[:, ls]], axis=0)
                dq2 = jnp.zeros((QB, 128), F32)
                dk2 = jnp.zeros((KW, 128), F32)
                dv2 = jnp.zeros((KW, 128), F32)
                for hh in range(2):
                    h = 2 * pr + hh
                    qh = jnp.where(masks[hh], q2, 0)
                    doh = jnp.where(masks[hh], do2, 0)
                    p = _att_probs(qh, k2, bias_ref[h], valid)
                    dp = _dot(doh, v2, 'nt')
                    ds = p * (dp - jnp.sum(p * dp, axis=-1, keepdims=True))
                    db_ref[h] += ds
                    dq2 = dq2 + _dot(ds, jnp.where(masks[hh], k2, 0))
                    dk2 = dk2 + _dot(ds, qh, 'tn')
                    dv2 = dv2 + _dot(p, doh, 'tn')
                dq_ref[:, ls] = (dq2 * ATT_SCALE).astype(BF16)
                dkw[:, ls] += dk2 * ATT_SCALE
                dvw[:, ls] += dv2

        dk_ref[...] = dkw[0:QB, :].astype(BF16)
        dv_ref[...] = dvw[0:QB, :].astype(BF16)
        for buf in (dkw, dvw):
            rest = buf[QB:KW, :]
            buf[0:KW - QB, :] = rest
            buf[KW - QB:KW, :] = jnp.zeros((QB, ATT_WIDTH), F32)

    blk = jax.ShapeDtypeStruct((Bl, S, ATT_WIDTH), BF16)
    return pl.pallas_call(
        body, name="att_bwd", grid=(Bl, nb + 2),
        in_specs=[q_spec] + k_specs + v_specs + [do_spec, bias_spec],
        out_specs=[do_spec, kv_out, kv_out, bias_spec],
        out_shape=[blk, blk, blk, jax.ShapeDtypeStruct((ATT_HEADS, QB, KW), F32)],
        scratch_shapes=[pltpu.VMEM((KW, ATT_WIDTH), F32), pltpu.VMEM((KW, ATT_WIDTH), F32)],
        compiler_params=_params(("arbitrary", "arbitrary")))(*([proj3] * 7), do, bias)


def _rel_bias_grad(dbias):
    H = dbias.shape[0]
    Wd = KW + QB
    padded = jnp.pad(dbias, ((0, 0), (0, 1), (QB, 0)))
    skew = padded.reshape(H, (QB + 1) * Wd)[:, :QB * (Wd + 1)].reshape(H, QB, Wd + 1)[:, :, :Wd]
    c = jnp.arange(Wd)[:, None]
    bins = (jnp.clip(KW - c, -(CHUNK - 1), MAX_REL) + (CHUNK - 1) == jnp.arange(N_REL)[None, :]).astype(F32)

    def body(s_ref, bins_ref, o_ref):
        col = jnp.sum(s_ref[...], axis=1)
        o_ref[...] = jnp.dot(col, bins_ref[...], preferred_element_type=F32, precision=lax.Precision.HIGHEST)

    return pl.pallas_call(
        body, name="rel_bias_grad", grid=(1,),
        in_specs=[pl.BlockSpec((H, QB, Wd), lambda i: (0, 0, 0)), pl.BlockSpec((Wd, N_REL), lambda i: (0, 0))],
        out_specs=pl.BlockSpec((H, N_REL), lambda i: (0, 0)), out_shape=jax.ShapeDtypeStruct((H, N_REL), F32),
        compiler_params=_params(("arbitrary",)))(skew, bins)


MEM_TILE = 512


def _mem_probs(qh, kh):
    s = _dot(qh, kh, 'nt') * MEM_SCALE
    e = jnp.exp(s - jnp.max(s, axis=-1, keepdims=True))
    return e * (1.0 / jnp.sum(e, axis=-1, keepdims=True))


def _mem_fwd(proj3, kv3):
    Bl, S, _ = proj3.shape
    tq = _pick(S, (MEM_TILE, 256))
    hd = MEM_HEAD_DIM

    def body(q_ref, kv_ref, o_ref):
        for h in range(MEM_HEADS):
            p = _mem_probs(q_ref[:, h * hd:(h + 1) * hd], kv_ref[:, h * hd:(h + 1) * hd])
            o_ref[:, h * hd:(h + 1) * hd] = _dot(p, kv_ref[:, MEM_WIDTH + h * hd:MEM_WIDTH + (h + 1) * hd]).astype(BF16)

    return pl.pallas_call(
        body, name="mem_fwd", grid=(Bl, S // tq),
        in_specs=[pl.BlockSpec((None, tq, MEM_WIDTH), lambda b, i: (b, i, 5)),
                  pl.BlockSpec((None, MEM_LEN, 2 * MEM_WIDTH), lambda b, i: (b, 0, 0))],
        out_specs=pl.BlockSpec((None, tq, MEM_WIDTH), lambda b, i: (b, i, 0)),
        out_shape=jax.ShapeDtypeStruct((Bl, S, MEM_WIDTH), BF16),
        compiler_params=_params(("parallel", "parallel")))(proj3, kv3)


def _mem_bwd(proj3, kv3, do):
    Bl, S, _ = proj3.shape
    tq = _pick(S, (MEM_TILE, 256))
    hd = MEM_HEAD_DIM

    def body(q_ref, kv_ref, do_ref, dq_ref, dkv_ref):
        i = pl.program_id(1)

        @pl.when(i == 0)
        def _():
            dkv_ref[...] = jnp.zeros_like(dkv_ref)

        for h in range(MEM_HEADS):
            ks = slice(h * hd, (h + 1) * hd)
            vs = slice(MEM_WIDTH + h * hd, MEM_WIDTH + (h + 1) * hd)
            qh, kh, vh, doh = q_ref[:, ks], kv_ref[:, ks], kv_ref[:, vs], do_ref[:, ks]
            p = _mem_probs(qh, kh)
            dp = _dot(doh, vh, 'nt')
            ds = p * (dp - jnp.sum(p * dp, axis=-1, keepdims=True))
            dq_ref[:, ks] = (_dot(ds, kh) * MEM_SCALE).astype(BF16)
            dkv_ref[:, ks] += _dot(ds, qh, 'tn') * MEM_SCALE
            dkv_ref[:, vs] += _dot(p, doh, 'tn')

    return pl.pallas_call(
        body, name="mem_bwd", grid=(Bl, S // tq),
        in_specs=[pl.BlockSpec((None, tq, MEM_WIDTH), lambda b, i: (b, i, 5)),
                  pl.BlockSpec((None, MEM_LEN, 2 * MEM_WIDTH), lambda b, i: (b, 0, 0)),
                  pl.BlockSpec((None, tq, MEM_WIDTH), lambda b, i: (b, i, 0))],
        out_specs=[pl.BlockSpec((None, tq, MEM_WIDTH), lambda b, i: (b, i, 0)),
                   pl.BlockSpec((None, MEM_LEN, 2 * MEM_WIDTH), lambda b, i: (b, 0, 0))],
        out_shape=[jax.ShapeDtypeStruct((Bl, S, MEM_WIDTH), BF16),
                   jax.ShapeDtypeStruct((Bl, MEM_LEN, 2 * MEM_WIDTH), F32)],
        compiler_params=_params(("parallel", "arbitrary")))(proj3, kv3, do)


def _position():
    x, y, c = lax.axis_index("x"), lax.axis_index("y"), lax.axis_index("c")
    return x, y, c, 4 * x + 2 * y + c


def _device(idx):
    return ((idx >> 2) & 1, (idx >> 1) & 1, idx & 1)


def _half_block(ref, axis, shard_shape, k, h):
    R, Cn = shard_shape
    if axis == 1:
        return ref.at[pl.ds(h * (R // 2), R // 2), pl.ds(k * Cn, Cn)]
    return ref.at[pl.ds(k * R + h * (R // 2), R // 2), :]


def _block(ref, axis, shard_shape, k):
    R, Cn = shard_shape
    if axis == 1:
        return ref.at[:, pl.ds(k * Cn, Cn)]
    return ref.at[pl.ds(k * R, R), :]


def _half(ref, h):
    R = ref.shape[0]
    return ref.at[pl.ds(h * (R // 2), R // 2), :]


ANY = pl.BlockSpec(memory_space=pl.ANY)


HBM = pl.BlockSpec(memory_space=pltpu.HBM)
SEM = pl.BlockSpec(memory_space=pltpu.SEMAPHORE)
VMEM_WHOLE = pl.BlockSpec(memory_space=pltpu.VMEM)
EFFECT = pltpu.SideEffectType.DATAFLOW_SIDE_EFFECTING


def _in_hbm(a):
    return pltpu.with_memory_space_constraint(a, pltpu.HBM)


def _split_start(body, name, sources, lands, n_copies):
    n = len(sources)
    out_shape, out_specs = [], []
    for _ in range(n):
        out_shape += [pltpu.SemaphoreType.DMA((n_copies,)), pltpu.SemaphoreType.DMA((n_copies,))]
        out_specs += [SEM, SEM]
    out_shape += [pltpu.HBM(a.shape, a.dtype) for a in list(sources) + list(lands)]
    out_specs += [HBM] * (2 * n)
    out_shape.append(jax.ShapeDtypeStruct((8, 128), F32))
    out_specs.append(VMEM_WHOLE)

    def call_body(*refs):
        srcs, lnds = refs[:n], refs[n:2 * n]
        sems = refs[2 * n:4 * n]
        token = refs[-1]
        body(srcs, lnds, sems[0::2], sems[1::2])
        token[...] = jnp.zeros_like(token)

    res = pl.pallas_call(
        call_body, name=name, in_specs=[HBM] * (2 * n), out_specs=out_specs, out_shape=out_shape,
        input_output_aliases={i: 2 * n + i for i in range(2 * n)},
        compiler_params=pltpu.CompilerParams(has_side_effects=EFFECT))(
            *[_in_hbm(a) for a in list(sources) + list(lands)])
    pairs = [(res[2 * w], res[2 * w + 1], res[2 * n + w], res[3 * n + w]) for w in range(n)]
    return pairs, res[-1]


def _split_wait(body, name, pairs, after):
    n = len(pairs)

    def call_body(*refs):
        srcs, lnds = refs[:n], refs[n:2 * n]
        sems = refs[2 * n:4 * n]
        body(srcs, lnds, sems[0::2], sems[1::2])

    args = [_in_hbm(p[2]) for p in pairs] + [_in_hbm(p[3]) for p in pairs]
    for p in pairs:
        args += [p[0], p[1]]
    res = pl.pallas_call(
        call_body, name=name, in_specs=[HBM] * (2 * n) + [SEM] * (2 * n) + [ANY], out_specs=[HBM] * (2 * n),
        out_shape=[pltpu.HBM(a.shape, a.dtype) for a in args[:2 * n]],
        input_output_aliases={i: i for i in range(2 * n)},
        compiler_params=pltpu.CompilerParams(has_side_effects=EFFECT))(*args, after)
    return res[:n], res[n:]


def _place_block(shard, axis, chip_idx, name, after=None):
    R, Cn = shard.shape
    tr = _pick(R, (256, 176, 128, 64, 32, 16, 8))
    nblk = R // tr

    def body(k_ref, s_ref, *rest):
        rest[-1][...] = s_ref[...]

    if axis == 1:
        out_shape, out_index = (R, 4 * Cn), lambda i, k: (i, k[0])
    else:
        out_shape, out_index = (4 * R, Cn), lambda i, k: (k[0] * nblk + i, 0)
    extra = () if after is None else (after,)
    return pl.pallas_call(
        body, name=name,
        grid_spec=pltpu.PrefetchScalarGridSpec(
            num_scalar_prefetch=1, grid=(nblk,),
            in_specs=[pl.BlockSpec((tr, Cn), lambda i, k: (i, 0))] + [ANY] * len(extra),
            out_specs=pl.BlockSpec((tr, Cn), out_index)),
        out_shape=jax.ShapeDtypeStruct(out_shape, shard.dtype),
        compiler_params=_params(("parallel",)))(chip_idx, shard, *extra)


def _gather_copy(srcs, lnds, send, recv, axes, shapes, w, j, me):
    chip = me >> 1
    return (pltpu.make_async_remote_copy(
        src_ref=srcs[w], dst_ref=_block(lnds[w], axes[w], shapes[w], chip), send_sem=send[w].at[j],
        recv_sem=recv[w].at[j], device_id=_device(me ^ (2 * (j + 1))), device_id_type=MESH),
            pltpu.make_async_remote_copy(
        src_ref=srcs[w], dst_ref=_block(lnds[w], axes[w], shapes[w], chip ^ (j + 1)), send_sem=send[w].at[j],
        recv_sem=recv[w].at[j], device_id=_device(me ^ (2 * (j + 1))), device_id_type=MESH))


def _gather_start(shards, lands, axes, name):
    shapes = [s.shape for s in shards]

    def body(srcs, lnds, send, recv):
        x, y, c, me = _position()
        for w in range(len(shards)):
            for j in range(3):
                _gather_copy(srcs, lnds, send, recv, axes, shapes, w, j, me)[0].start()

    return _split_start(body, name, shards, lands, 3)


def _gather_wait(pairs, axes, after, name):
    shapes = [p[2].shape for p in pairs]

    def body(srcs, lnds, send, recv):
        x, y, c, me = _position()
        for w in range(len(pairs)):
            for j in range(3):
                sent, landed = _gather_copy(srcs, lnds, send, recv, axes, shapes, w, j, me)
                sent.wait_send()
                landed.wait_recv()

    return _split_wait(body, name, pairs, after)[1]


def _shard_shape(grad, axis):
    return (grad.shape[0], grad.shape[1] // 4) if axis == 1 else (grad.shape[0] // 4, grad.shape[1])


def _scatter_copy(srcs, lnds, send, recv, axes, shapes, w, m, me):
    peer = me ^ m
    return pltpu.make_async_remote_copy(
        src_ref=_half_block(srcs[w], axes[w], shapes[w], peer >> 1, peer & 1), dst_ref=lnds[w].at[m - 1],
        send_sem=send[w].at[m - 1], recv_sem=recv[w].at[m - 1], device_id=_device(peer), device_id_type=MESH)


def _scatter_start(grads, axes, name):
    shapes = [_shard_shape(g, a) for g, a in zip(grads, axes)]
    lands = [lax.empty((N_DEV - 1, R // 2, Cn), BF16) for R, Cn in shapes]

    def body(srcs, lnds, send, recv):
        x, y, c, me = _position()
        for w in range(len(grads)):
            for m in range(1, N_DEV):
                _scatter_copy(srcs, lnds, send, recv, axes, shapes, w, m, me).start()

    return _split_start(body, name, grads, lands, N_DEV - 1)


def _scatter_wait(pairs, axes, after):
    shapes = [_shard_shape(p[2], a) for p, a in zip(pairs, axes)]

    def body(srcs, lnds, send, recv):
        x, y, c, me = _position()
        for w in range(len(pairs)):
            for m in range(1, N_DEV):
                cp = _scatter_copy(srcs, lnds, send, recv, axes, shapes, w, m, me)
                cp.wait_send()
                cp.wait_recv()

    return _split_wait(body, "scatter_wait", pairs, after)


def _sum_partials(own, parts, half, name):
    R, Cn = own.shape
    tr = _pick(R, (256, 176, 128, 64, 32, 16, 8))
    nblk = R // tr

    def body(half_ref, own_ref, p_ref, o_ref):
        acc = own_ref[...].astype(F32)
        for d in range(N_DEV - 1):
            acc = acc + p_ref[d].astype(F32)
        o_ref[...] = acc

    return pl.pallas_call(
        body, name=name,
        grid_spec=pltpu.PrefetchScalarGridSpec(
            num_scalar_prefetch=1, grid=(nblk,),
            in_specs=[pl.BlockSpec((tr, Cn), lambda i, hr: (i, 0)),
                      pl.BlockSpec((N_DEV - 1, tr, Cn), lambda i, hr: (0, i, 0))],
            out_specs=pl.BlockSpec((tr, Cn), lambda i, hr: (hr[0] * nblk + i, 0))),
        out_shape=jax.ShapeDtypeStruct((2 * R, Cn), F32),
        compiler_params=_params(("parallel",)))(half, own, parts)


def _exchange_halves(grads):
    n = len(grads)

    def body(*refs):
        outs = refs[n:2 * n]
        send, recv = refs[2 * n:]
        x, y, c, me = _position()

        def copy(w, half):
            rows = _half(outs[w], half)
            return pltpu.make_async_remote_copy(src_ref=rows, dst_ref=rows, send_sem=send.at[w],
                                                recv_sem=recv.at[w], device_id=_device(me ^ 1), device_id_type=MESH)

        for w in range(n):
            copy(w, c).start()
        for w in range(n):
            copy(w, 1 - c).wait_recv()
        for w in range(n):
            copy(w, c).wait_send()

    return pl.pallas_call(
        body, name="exchange_halves", in_specs=[ANY] * n, out_specs=[ANY] * n,
        out_shape=[jax.ShapeDtypeStruct(a.shape, a.dtype) for a in grads],
        input_output_aliases={i: i for i in range(n)},
        scratch_shapes=[pltpu.SemaphoreType.DMA((n,)), pltpu.SemaphoreType.DMA((n,))],
        compiler_params=pltpu.CompilerParams(has_side_effects=True))(*grads)


def _all_reduce_small(vec):
    R, L = vec.shape

    def body(v_ref, o_ref, buf, send, recv):
        x, y, c, me = _position()
        buf[me] = v_ref[...]

        def copy(m, slot):
            return pltpu.make_async_remote_copy(src_ref=v_ref, dst_ref=buf.at[slot], send_sem=send.at[m - 1],
                                                recv_sem=recv.at[m - 1], device_id=_device(me ^ m),
                                                device_id_type=MESH)

        for m in range(1, N_DEV):
            copy(m, me).start()
        for m in range(1, N_DEV):
            copy(m, me ^ m).wait_recv()
        for m in range(1, N_DEV):
            copy(m, me).wait_send()
        acc = buf[0]
        for d in range(1, N_DEV):
            acc = acc + buf[d]
        o_ref[...] = acc

    vm = pl.BlockSpec(memory_space=pltpu.VMEM)
    return pl.pallas_call(
        body, name="all_reduce_small", in_specs=[vm], out_specs=vm, out_shape=jax.ShapeDtypeStruct((R, L), F32),
        scratch_shapes=[pltpu.VMEM((N_DEV, R, L), F32), pltpu.SemaphoreType.DMA((N_DEV - 1,)),
                        pltpu.SemaphoreType.DMA((N_DEV - 1,))],
        compiler_params=pltpu.CompilerParams(has_side_effects=True))(vec)


def _adamw(w, g, m, v, name):
    R, Cn = w.shape
    tr = _pick(R, (256, 176, 128, 64, 40, 32, 16, 8))

    def body(w_ref, g_ref, m_ref, v_ref, d_ref, nm_ref, nv_ref):
        gv = g_ref[...]
        nm = ADAM_B1 * m_ref[...] + (1.0 - ADAM_B1) * gv
        nv = ADAM_B2 * v_ref[...] + (1.0 - ADAM_B2) * (gv * gv)
        m_hat = nm / (1.0 - ADAM_B1 ** ADAM_STEP)
        v_hat = nv / (1.0 - ADAM_B2 ** ADAM_STEP)
        d_ref[...] = -ADAM_LR * (m_hat / (jnp.sqrt(v_hat) + ADAM_EPS) + ADAM_WD * w_ref[...])
        nm_ref[...] = nm
        nv_ref[...] = nv

    spec = pl.BlockSpec((tr, Cn), lambda i: (i, 0))
    return pl.pallas_call(
        body, name=name, grid=(R // tr,), in_specs=[spec] * 4, out_specs=[spec] * 3,
        out_shape=[jax.ShapeDtypeStruct((R, Cn), F32)] * 3, compiler_params=_params(("parallel",)))(w, g, m, v)


def _pack(arrays, rows):
    flat = jnp.concatenate([a.reshape(-1).astype(F32) for a in arrays])
    return jnp.pad(flat, (0, rows * 128 - flat.shape[0])).reshape(rows, 128)


def _unpack(packed, shapes):
    flat = packed.reshape(-1)
    out, off = [], 0
    for s in shapes:
        size = 1
        for d in s:
            size *= d
        out.append(flat[off:off + size].reshape(s))
        off += size
    return out


def _ffn_fwd(x, norm, arrived, tag, after=None):
    h = _rms_fwd(x, norm, f"{tag}_norm", after=after)
    w_up = arrived(f"{tag}_w_up", h)
    a, b, act = _ffn_up(h, w_up, f"{tag}_up")
    w_down = arrived(f"{tag}_w_down", act)
    out = _mm(act, w_down, 'nn', f"{tag}_down", F32, res=x, scale=0.5)
    return out, (h, a, b, act, w_up, w_down)


def _ffn_bwd(dout, x, norm, saved, tag, send):
    h, a, b, act, w_up, w_down = saved
    g_down = _mm(act, dout, 'tn', f"{tag}_down_dw", BF16, scale=0.5)
    token = send([f"{tag}_w_down"], [g_down])
    da, db = _ffn_down_bwd(dout, w_down, a, b, f"{tag}_down_dx", after=token)
    g_up = _dw_pieces(h, [da, db], f"{tag}_up_dw")
    token = send([f"{tag}_w_up"], [g_up])
    return _dx_rms_bwd([da, db], w_up, x, norm, dout, f"{tag}_up_dx", after=token)


def kernel(x, mem, ffn1_norm, ffn1_w_up, ffn1_w_down, mix_norm, mem_norm, w_in, b_gate, conv_dw_w, conv_dw_b, conv_ln_g, conv_ln_b, conv_w_pw, att_rel_bias, att_w_o, mem_w_kv, mem_w_o, w_out, ffn2_norm, ffn2_w_up, ffn2_w_down, final_norm, loss_target, m_ffn1_norm, m_ffn1_w_up, m_ffn1_w_down, m_mix_norm, m_mem_norm, m_w_in, m_b_gate, m_conv_dw_w, m_conv_dw_b, m_conv_ln_g, m_conv_ln_b, m_conv_w_pw, m_att_rel_bias, m_att_w_o, m_mem_w_kv, m_mem_w_o, m_w_out, m_ffn2_norm, m_ffn2_w_up, m_ffn2_w_down, m_final_norm, v_ffn1_norm, v_ffn1_w_up, v_ffn1_w_down, v_mix_norm, v_mem_norm, v_w_in, v_b_gate, v_conv_dw_w, v_conv_dw_b, v_conv_ln_g, v_conv_ln_b, v_conv_w_pw, v_att_rel_bias, v_att_w_o, v_mem_w_kv, v_mem_w_o, v_w_out, v_ffn2_norm, v_ffn2_w_up, v_ffn2_w_down, v_final_norm):
    given = dict(locals())
    wts = {n: given[n] for n in WEIGHTS}
    mom1 = {n: given["m_" + n] for n in WEIGHTS}
    mom2 = {n: given["v_" + n] for n in WEIGHTS}
    Bl, S, Dm = x.shape
    T = Bl * S
    x0 = x.reshape(T, Dm)
    tgt = loss_target.reshape(T, Dm)
    mem2 = mem.reshape(Bl * MEM_LEN, Dm)

    big_names = [n for n, _ in BIG]
    big_axes = [a for _, a in BIG]
    chip = 2 * lax.axis_index("x") + lax.axis_index("y")

    core = lax.axis_index("c")
    axis_of = dict(BIG)

    gather_groups = [['ffn1_w_up'], ['ffn1_w_down'], ['w_in', 'conv_dw_w'],
                     ['mem_w_kv', 'conv_w_pw', 'att_w_o', 'mem_w_o', 'w_out'], ['ffn2_w_up'], ['ffn2_w_down']]
    gather_names = [n for grp in gather_groups for n in grp]
    gather_axes = [axis_of.get(n, 1) for n in gather_names]
    shards = [jnp.pad(conv_dw_w[0], ((0, 1), (0, 0))) if n == 'conv_dw_w' else wts[n][0].astype(BF16)
              for n in gather_names]
    chip_idx = chip.reshape(1).astype(jnp.int32)
    first, first_token = _gather_start(
        shards[:1], [_place_block(shards[0], gather_axes[0], chip_idx, f"place_{gather_names[0]}")],
        gather_axes[:1], "gather_start_first")
    lands = [_place_block(sh, a, chip_idx, f"place_{n}", after=first_token)
             for sh, a, n in zip(shards[1:], gather_axes[1:], gather_names[1:])]
    rest, gather_token = _gather_start(shards[1:], lands, gather_axes[1:], "gather_start_rest")
    in_flight = dict(zip(gather_names, first + rest))
    full = {}

    def arrived(name, after):
        if name not in full:
            grp = next(grp for grp in gather_groups if name in grp)
            lands = _gather_wait([in_flight[n] for n in grp], [axis_of.get(n, 1) for n in grp], after,
                                 f"gather_wait_{grp[0]}")
            full.update(zip(grp, lands))
        return full[name]

    scattering = {}

    def send(names, grads):
        pairs, token = _scatter_start(grads, [axis_of[n] for n in names], f"scatter_start_{names[0]}")
        scattering.update(zip(names, pairs))
        return token

    final_g = final_norm.reshape(1, Dm)
    bias = _att_bias(att_rel_bias[0])

    x1, ffn1_saved = _ffn_fwd(x0, ffn1_norm, arrived, "ffn1", after=gather_token)
    h = _rms_fwd(x1, mix_norm, "mix_norm")
    w_in_full = arrived('w_in', h)
    dw_full = full['conv_dw_w'][:CONV_KERNEL]
    proj = _mm(h, w_in_full, 'nn', "w_in", BF16)
    proj3 = proj.reshape(Bl, S, proj.shape[1])
    cv, c_act = _conv_fwd(proj3, dw_full, conv_dw_b, conv_ln_g, conv_ln_b)
    o_att = _att_fwd(proj3, bias)
    mem_h = _rms_fwd(mem2, mem_norm, "mem_norm")
    kv = _mm(mem_h, arrived('mem_w_kv', o_att), 'nn', "mem_kv", BF16)
    kv3 = kv.reshape(Bl, MEM_LEN, 2 * MEM_WIDTH)
    o_mem = _mem_fwd(proj3, kv3)
    c_act2, o_att2, o_mem2 = c_act.reshape(T, -1), o_att.reshape(T, -1), o_mem.reshape(T, -1)
    x2, yc, ya, ym, y = _mix_fwd(c_act2, o_att2, o_mem2, proj, b_gate, x1, full['conv_w_pw'], full['att_w_o'],
                                 full['mem_w_o'], full['w_out'])
    x3, ffn2_saved = _ffn_fwd(x2, ffn2_norm, arrived, "ffn2")
    dx3, g_final, loss_vec = _final_fwd_bwd(x3, tgt, final_g)

    g = {}
    dx2, g['ffn2_norm'] = _ffn_bwd(dx3, x2, ffn2_norm, ffn2_saved, "ffn2", send)
    dyc, dya, dym, dgl, g['b_gate'], dc, doa, dom = _mix_bwd(
        dx2, yc, ya, ym, proj, b_gate, full['conv_w_pw'], full['att_w_o'], full['mem_w_o'], full['w_out'])
    token = send(['w_out', 'conv_w_pw', 'att_w_o', 'mem_w_o'],
                 [_mm(y, dx2, 'tn', "w_out_dw", BF16), _mm(c_act2, dyc, 'tn', "conv_pw_dw", BF16),
                  _mm(o_att2, dya, 'tn', "att_o_dw", BF16), _mm(o_mem2, dym, 'tn', "mem_o_dw", BF16)])
    du, g_dw, g['conv_dw_b'], g['conv_ln_g'], g['conv_ln_b'] = _conv_bwd(
        proj3, cv, dc.reshape(Bl, S, -1), dw_full, conv_ln_g, conv_ln_b)
    dq, dk, dv, dbias = _att_bwd(proj3, doa.reshape(Bl, S, -1), bias)
    g['att_rel_bias'] = _rel_bias_grad(dbias)
    dmq, dkv = _mem_bwd(proj3, kv3, dom.reshape(Bl, S, -1))
    dkv2 = dkv.reshape(Bl * MEM_LEN, 2 * MEM_WIDTH)
    g_kv = _mm(mem_h, dkv2, 'tn', "mem_kv_dw", BF16, after=token)
    dmem_h = _mm(dkv2, full['mem_w_kv'], 'nt', "mem_kv_dx", F32)
    _, g['mem_norm'] = _rms_bwd(mem2, mem_norm, dmem_h, dmem_h, "mem_norm_bwd")
    left = jnp.concatenate([du.reshape(T, -1), dq.reshape(T, -1), dk.reshape(T, -1), dv.reshape(T, -1),
                            dmq.reshape(T, -1)], axis=1)
    dproj = lax.dynamic_update_slice(dgl, left, (0, 0))
    token = send(['mem_w_kv', 'w_in'], [g_kv, _mm(h, dproj, 'tn', "w_in_dw", BF16)])
    dx1, g['mix_norm'] = _dx_rms_bwd([dproj], w_in_full, x1, mix_norm, dx2, "w_in_dx", after=token)
    dx0, g['ffn1_norm'] = _ffn_bwd(dx1, x0, ffn1_norm, ffn1_saved, "ffn1", send)
    g['final_norm'] = g_final

    sent, landed = _scatter_wait([scattering[n] for n in big_names], big_axes, dx0)
    halves = []
    half_idx = core.reshape(1).astype(jnp.int32)
    for n, a, own_full, parts in zip(big_names, big_axes, sent, landed):
        R, Cn = _shard_shape(own_full, a)
        start = (core * (R // 2), chip * Cn) if a == 1 else (chip * R + core * (R // 2), 0)
        own = lax.dynamic_slice(own_full, start, (R // 2, Cn))
        halves.append(_sum_partials(own, parts, half_idx, f"sum_{n}"))
    for n, sg in zip(big_names, _exchange_halves(halves)):
        g[n] = sg

    small_shapes = [wts[n].shape for n in SMALL]
    n_small = sum(int(wts[n].size) for n in SMALL)
    n_red = n_small + CONV_KERNEL * CONV_WIDTH
    red = _all_reduce_small(_pack([g[n] for n in SMALL] + [g_dw], -(-n_red // 1024) * 8))
    red_list = _unpack(red, small_shapes + [(CONV_KERNEL, CONV_WIDTH)])
    for n, rg in zip(SMALL, red_list[:-1]):
        g[n] = rg
    dw_cols = conv_dw_w.shape[2]
    g['conv_dw_w'] = lax.dynamic_slice(red_list[-1], (0, chip * dw_cols), (CONV_KERNEL, dw_cols))[None]

    delta, new_m, new_v = {}, {}, {}
    for n in big_names:
        g[n] = g[n][None]
        d, nm, nv = _adamw(wts[n][0], g[n][0], mom1[n][0], mom2[n][0], f"adamw_{n}")
        delta[n], new_m[n], new_v[n] = d[None], nm[None], nv[None]
    rest = SMALL + ['conv_dw_w']
    rest_shapes = [wts[n].shape for n in rest]
    rows = -(-sum(int(wts[n].size) for n in rest) // 1024) * 8
    packed = [_pack([src[n] for n in rest], rows) for src in (wts, g, mom1, mom2)]
    for out, res in zip((delta, new_m, new_v), _adamw(*packed, "adamw_small")):
        for n, a in zip(rest, _unpack(res, rest_shapes)):
            out[n] = a

    loss = lax.psum(loss_vec[0, 0], ("x", "y", "c"))
    grad_x = dx0.reshape(Bl, S, Dm)
    return (loss, grad_x, *[g[n] for n in WEIGHTS], *[delta[n] for n in WEIGHTS],
            *[new_m[n] for n in WEIGHTS], *[new_v[n] for n in WEIGHTS])
```

```python
import jax
import jax.numpy as jnp
from jax import lax
from jax.experimental import pallas as pl
from jax.experimental.pallas import tpu as pltpu

F32 = jnp.float32
BF16 = jnp.bfloat16

D_MODEL = 1024
D_FF = 2816
CHUNK = 64
LEFT_CHUNKS = 8
MAX_REL = 128
N_REL = (CHUNK - 1) + MAX_REL + 1
CONV_WIDTH = 512
CONV_KERNEL = 31
ATT_HEADS = 8
ATT_WIDTH = 512
MEM_LEN = 256
MEM_HEADS = 4
MEM_HEAD_DIM = 128
MEM_WIDTH = 512
EPS = 1e-6
MASK_VALUE = -1e30
ATT_SCALE = 64 ** -0.5
MEM_SCALE = 128 ** -0.5

ADAM_LR = 0.001
ADAM_B1 = 0.9
ADAM_B2 = 0.999
ADAM_EPS = 1e-08
ADAM_WD = 0.01
ADAM_STEP = 10

QB = 256
KW = 3 * QB
CONV_PAD = 32
CONV_TILE = 256

VMEM_LIMIT = 56 << 20

WEIGHTS = ['ffn1_norm', 'ffn1_w_up', 'ffn1_w_down', 'mix_norm', 'mem_norm', 'w_in', 'b_gate', 'conv_dw_w',
           'conv_dw_b', 'conv_ln_g', 'conv_ln_b', 'conv_w_pw', 'att_rel_bias', 'att_w_o', 'mem_w_kv', 'mem_w_o',
           'w_out', 'ffn2_norm', 'ffn2_w_up', 'ffn2_w_down', 'final_norm']
BIG = [('ffn1_w_up', 1), ('ffn1_w_down', 0), ('w_in', 1), ('conv_w_pw', 1), ('att_w_o', 1), ('mem_w_kv', 0),
       ('mem_w_o', 1), ('w_out', 0), ('ffn2_w_up', 1), ('ffn2_w_down', 0)]
SMALL = ['ffn1_norm', 'mix_norm', 'mem_norm', 'b_gate', 'conv_dw_b', 'conv_ln_g', 'conv_ln_b', 'att_rel_bias',
         'ffn2_norm', 'final_norm']
N_CHIPS = 4
N_DEV = 8
MESH = pl.DeviceIdType.MESH


def _pick(n, cands):
    for c in cands:
        if n % c == 0:
            return c
    return n


def _sig(x):
    return 0.5 * jnp.tanh(0.5 * x) + 0.5


def _params(sem=None, vmem=VMEM_LIMIT):
    return pltpu.CompilerParams(dimension_semantics=sem, vmem_limit_bytes=vmem)


def _dot(a, b, mode='nn'):
    dims = {'nn': (((1,), (0,)), ((), ())), 'nt': (((1,), (1,)), ((), ())), 'tn': (((0,), (0,)), ((), ()))}[mode]
    return lax.dot_general(a.astype(BF16), b.astype(BF16), dims, preferred_element_type=F32)


def _mm(a, b, mode, name, out_dtype, res=None, scale=1.0, after=None):
    if mode == 'nn':
        (M, C), (_, N) = a.shape, b.shape
    elif mode == 'nt':
        (M, C), (N, _) = a.shape, b.shape
    else:
        (C, M), (_, N) = a.shape, b.shape
    tm = _pick(M, (1024, 1408, 512, 256, 128))
    tn = _pick(N, (1024, 1408, 512, 256, 128))
    tc = _pick(C, (1024, 1408, 512, 256, 128))
    nk = C // tc
    if mode == 'nn':
        a_spec = pl.BlockSpec((tm, tc), lambda i, j, k: (i, k))
        b_spec = pl.BlockSpec((tc, tn), lambda i, j, k: (k, j))
    elif mode == 'nt':
        a_spec = pl.BlockSpec((tm, tc), lambda i, j, k: (i, k))
        b_spec = pl.BlockSpec((tn, tc), lambda i, j, k: (j, k))
    else:
        a_spec = pl.BlockSpec((tc, tm), lambda i, j, k: (k, i))
        b_spec = pl.BlockSpec((tc, tn), lambda i, j, k: (k, j))
    o_spec = pl.BlockSpec((tm, tn), lambda i, j, k: (i, j))
    has_res = res is not None
    has_after = after is not None

    def body(*refs):
        a_ref, b_ref = refs[:2]
        r_ref = refs[2] if has_res else None
        o_ref, acc_ref = refs[-2:]
        k = pl.program_id(2)

        def finish(acc):
            if scale != 1.0:
                acc = acc * scale
            if r_ref is not None:
                acc = r_ref[...] + acc
            o_ref[...] = acc.astype(o_ref.dtype)

        if nk == 1:
            finish(_dot(a_ref[...], b_ref[...], mode))
        else:
            @pl.when(k == 0)
            def _():
                acc_ref[...] = jnp.zeros_like(acc_ref)

            acc_ref[...] += _dot(a_ref[...], b_ref[...], mode)

            @pl.when(k == nk - 1)
            def _():
                finish(acc_ref[...])

    in_specs = [a_spec, b_spec] + ([o_spec] if has_res else []) + ([ANY] if has_after else [])
    args = (a, b) + ((res,) if has_res else ()) + ((after,) if has_after else ())
    acc_shape = (tm, tn) if nk > 1 else (8, 128)
    return pl.pallas_call(
        body, name=name, grid=(M // tm, N // tn, nk), in_specs=in_specs, out_specs=o_spec,
        out_shape=jax.ShapeDtypeStruct((M, N), out_dtype), scratch_shapes=[pltpu.VMEM(acc_shape, F32)],
        compiler_params=_params(("parallel", "parallel", "arbitrary")))(*args)


def _row_tile(T):
    return _pick(T, (512, 256, 128, 64, 32, 16, 8))


def _rms_fwd(x, g, name, after=None):
    T, Dm = x.shape
    tm = _row_tile(T)

    def body(x_ref, g_ref, *rest):
        o_ref = rest[-1]
        xv = x_ref[...]
        r = lax.rsqrt(jnp.mean(xv * xv, axis=-1, keepdims=True) + EPS)
        o_ref[...] = ((xv * r) * g_ref[...]).astype(o_ref.dtype)

    extra = () if after is None else (after,)
    return pl.pallas_call(
        body, name=name, grid=(T // tm,),
        in_specs=[pl.BlockSpec((tm, Dm), lambda i: (i, 0)), pl.BlockSpec((1, Dm), lambda i: (0, 0))]
        + [ANY] * len(extra),
        out_specs=pl.BlockSpec((tm, Dm), lambda i: (i, 0)), out_shape=jax.ShapeDtypeStruct((T, Dm), BF16),
        compiler_params=_params(("parallel",)))(x, g, *extra)


def _rms_bwd(x, g, dh, dres, name):
    T, Dm = x.shape
    tm = _row_tile(T)

    def body(x_ref, g_ref, dh_ref, dr_ref, dx_ref, dg_ref):
        i = pl.program_id(0)
        xv = x_ref[...]
        r = lax.rsqrt(jnp.mean(xv * xv, axis=-1, keepdims=True) + EPS)
        xr = xv * r
        dh_v = dh_ref[...].astype(F32)
        dyg = dh_v * g_ref[...]
        dx = r * (dyg - xr * jnp.mean(dyg * xr, axis=-1, keepdims=True))
        dx_ref[...] = dr_ref[...] + dx

        @pl.when(i == 0)
        def _():
            dg_ref[...] = jnp.zeros_like(dg_ref)

        dg_ref[...] += jnp.sum(dh_v * xr, axis=0, keepdims=True)

    row = pl.BlockSpec((tm, Dm), lambda i: (i, 0))
    vec = pl.BlockSpec((1, Dm), lambda i: (0, 0))
    return pl.pallas_call(
        body, name=name, grid=(T // tm,), in_specs=[row, vec, row, row], out_specs=[row, vec],
        out_shape=[jax.ShapeDtypeStruct((T, Dm), F32), jax.ShapeDtypeStruct((1, Dm), F32)],
        compiler_params=_params(("arbitrary",)))(x, g, dh, dres)


def _final_fwd_bwd(x3, tgt, g):
    T, Dm = x3.shape
    tm = _row_tile(T)

    def body(x_ref, t_ref, g_ref, dx_ref, dg_ref, loss_ref):
        i = pl.program_id(0)
        xv = x_ref[...]
        gg = g_ref[...]
        r = lax.rsqrt(jnp.mean(xv * xv, axis=-1, keepdims=True) + EPS)
        xr = xv * r
        err = xr * gg - t_ref[...]
        dout = err * (1.0 / Dm)
        dyg = dout * gg
        dx_ref[...] = r * (dyg - xr * jnp.mean(dyg * xr, axis=-1, keepdims=True))

        @pl.when(i == 0)
        def _():
            dg_ref[...] = jnp.zeros_like(dg_ref)
            loss_ref[...] = jnp.zeros_like(loss_ref)

        dg_ref[...] += jnp.sum(dout * xr, axis=0, keepdims=True)
        loss_ref[...] += jnp.zeros_like(loss_ref) + (0.5 / Dm) * jnp.sum(err * err)

    row = pl.BlockSpec((tm, Dm), lambda i: (i, 0))
    vec = pl.BlockSpec((1, Dm), lambda i: (0, 0))
    one = pl.BlockSpec((1, 128), lambda i: (0, 0))
    return pl.pallas_call(
        body, name="final_fwd_bwd", grid=(T // tm,), in_specs=[row, row, vec], out_specs=[row, vec, one],
        out_shape=[jax.ShapeDtypeStruct((T, Dm), F32), jax.ShapeDtypeStruct((1, Dm), F32),
                   jax.ShapeDtypeStruct((1, 128), F32)],
        compiler_params=_params(("arbitrary",)))(x3, tgt, g)


def _ffn_up(h, w_up, name):
    T, K = h.shape
    Fh = w_up.shape[1] // 2
    tm = _pick(T, (512, 256, 128))
    tn = _pick(Fh, (1408, 512, 256, 128))
    nj = Fh // tn

    def body(h_ref, wa_ref, wb_ref, a_ref, b_ref, act_ref):
        hv = h_ref[...]
        a = _dot(hv, wa_ref[...])
        b = _dot(hv, wb_ref[...])
        a_ref[...] = a.astype(BF16)
        b_ref[...] = b.astype(BF16)
        act_ref[...] = (a * _sig(a) * b).astype(BF16)

    out = pl.BlockSpec((tm, tn), lambda i, j: (i, j))
    return pl.pallas_call(
        body, name=name, grid=(T // tm, nj),
        in_specs=[pl.BlockSpec((tm, K), lambda i, j: (i, 0)), pl.BlockSpec((K, tn), lambda i, j: (0, j)),
                  pl.BlockSpec((K, tn), lambda i, j: (0, j + nj))],
        out_specs=[out, out, out], out_shape=[jax.ShapeDtypeStruct((T, Fh), BF16)] * 3,
        compiler_params=_params(("parallel", "parallel")))(h, w_up, w_up)


def _ffn_down_bwd(dout, w_down, a, b, name, after=None):
    T, Dm = dout.shape
    Fh = w_down.shape[0]
    tm = _pick(T, (512, 256, 128))
    tn = _pick(Fh, (1408, 512, 256, 128))

    def body(d_ref, w_ref, a_ref, b_ref, *rest):
        da_ref, db_ref = rest[-2:]
        dact = _dot(d_ref[...], w_ref[...], 'nt') * 0.5
        av = a_ref[...].astype(F32)
        bv = b_ref[...].astype(F32)
        s = _sig(av)
        da_ref[...] = (dact * bv * s * (1.0 + av * (1.0 - s))).astype(BF16)
        db_ref[...] = (dact * av * s).astype(BF16)

    tile = pl.BlockSpec((tm, tn), lambda i, j: (i, j))
    extra = () if after is None else (after,)
    return pl.pallas_call(
        body, name=name, grid=(T // tm, Fh // tn),
        in_specs=[pl.BlockSpec((tm, Dm), lambda i, j: (i, 0)), pl.BlockSpec((tn, Dm), lambda i, j: (j, 0)),
                  tile, tile] + [ANY] * len(extra),
        out_specs=[tile, tile], out_shape=[jax.ShapeDtypeStruct((T, Fh), BF16)] * 2,
        compiler_params=_params(("parallel", "parallel")))(dout, w_down, a, b, *extra)


def _dx_rms_bwd(pieces, w, x, g, dres, name, after=None):
    T, Dm = x.shape
    width = pieces[0].shape[1]
    tm = _pick(T, (1024, 512, 256, 128))
    tc = _pick(width, (1408, 1024, 512, 256, 128))
    per = width // tc
    nk = per * len(pieces)
    npc = len(pieces)
    rows = _pick(tm, (256, 128))

    def body(*refs):
        p_refs = refs[:npc]
        w_ref, x_ref, g_ref, dr_ref = refs[npc:npc + 4]
        dx_ref, dg_ref, acc_ref = refs[-3:]
        i = pl.program_id(0)
        k = pl.program_id(1)

        @pl.when(k == 0)
        def _():
            acc_ref[...] = jnp.zeros_like(acc_ref)

        @pl.when((i == 0) & (k == 0))
        def _():
            dg_ref[...] = jnp.zeros_like(dg_ref)

        for p in range(npc):
            @pl.when((k >= p * per) & (k < (p + 1) * per))
            def _(p=p):
                acc_ref[...] += _dot(p_refs[p][...], w_ref[...], 'nt')

        @pl.when(k == nk - 1)
        def _():
            def chunk(c, carry):
                rs = pl.ds(pl.multiple_of(c * rows, rows), rows)
                dh = acc_ref[rs, :]
                xv = x_ref[rs, :]
                r = lax.rsqrt(jnp.mean(xv * xv, axis=-1, keepdims=True) + EPS)
                xr = xv * r
                dyg = dh * g_ref[...]
                dx_ref[rs, :] = dr_ref[rs, :] + r * (dyg - xr * jnp.mean(dyg * xr, axis=-1, keepdims=True))
                dg_ref[...] += jnp.sum(dh * xr, axis=0, keepdims=True)
                return carry

            lax.fori_loop(0, tm // rows, chunk, 0)

    def piece_spec(p):
        return pl.BlockSpec((tm, tc), lambda i, k: (i, jnp.clip(k - p * per, 0, per - 1)))

    once = pl.BlockSpec((tm, Dm), lambda i, k: (i, 0), pipeline_mode=pl.Buffered(1))
    row = pl.BlockSpec((tm, Dm), lambda i, k: (i, 0))
    vec = pl.BlockSpec((1, Dm), lambda i, k: (0, 0))
    extra = () if after is None else (after,)
    return pl.pallas_call(
        body, name=name, grid=(T // tm, nk),
        in_specs=[piece_spec(p) for p in range(npc)] + [pl.BlockSpec((Dm, tc), lambda i, k: (0, k)), once, vec, once]
        + [ANY] * len(extra),
        out_specs=[row, vec], out_shape=[jax.ShapeDtypeStruct((T, Dm), F32), jax.ShapeDtypeStruct((1, Dm), F32)],
        scratch_shapes=[pltpu.VMEM((tm, Dm), F32)],
        compiler_params=_params(("arbitrary", "arbitrary")))(*pieces, w, x, g, dres, *extra)


def _dw_pieces(a, pieces, name):
    C, M = a.shape
    width = pieces[0].shape[1]
    npc = len(pieces)
    tm = _pick(M, (1024, 512, 256, 128))
    tn = _pick(width, (1408, 1024, 512, 256, 128))
    tc = _pick(C, (1024, 512, 256, 128))
    per = width // tn
    nk = C // tc

    def body(*refs):
        a_ref = refs[0]
        p_refs = refs[1:1 + npc]
        o_ref, acc_ref = refs[-2:]
        j = pl.program_id(1)
        k = pl.program_id(2)

        @pl.when(k == 0)
        def _():
            acc_ref[...] = jnp.zeros_like(acc_ref)

        for p in range(npc):
            @pl.when((j >= p * per) & (j < (p + 1) * per))
            def _(p=p):
                acc_ref[...] += _dot(a_ref[...], p_refs[p][...], 'tn')

        @pl.when(k == nk - 1)
        def _():
            o_ref[...] = acc_ref[...].astype(o_ref.dtype)

    def piece_spec(p):
        return pl.BlockSpec((tc, tn), lambda i, j, k: (k, jnp.clip(j - p * per, 0, per - 1)))

    return pl.pallas_call(
        body, name=name, grid=(M // tm, per * npc, nk),
        in_specs=[pl.BlockSpec((tc, tm), lambda i, j, k: (k, i))] + [piece_spec(p) for p in range(npc)],
        out_specs=pl.BlockSpec((tm, tn), lambda i, j, k: (i, j)),
        out_shape=jax.ShapeDtypeStruct((M, width * npc), BF16), scratch_shapes=[pltpu.VMEM((tm, tn), F32)],
        compiler_params=_params(("parallel", "parallel", "arbitrary")))(a, *pieces)


def _mix_fwd(c_act, o_att, o_mem, proj, b_gate, x1, w_pw, w_o, w_mo, w_out):
    T, Dm = x1.shape
    W = c_act.shape[1]
    tm = _pick(T, (256, 128, 64, 32, 16, 8))

    def body(c_ref, oa_ref, om_ref, gl_ref, bg_ref, x1_ref, wpw_ref, wo_ref, wmo_ref, wout_ref,
             x2_ref, yc_ref, ya_ref, ym_ref, y_ref):
        yc = _dot(c_ref[...], wpw_ref[...])
        ya = _dot(oa_ref[...], wo_ref[...])
        ym = _dot(om_ref[...], wmo_ref[...])
        g = _sig(gl_ref[...].astype(F32) + bg_ref[...])
        y = g[:, :Dm] * yc + g[:, Dm:2 * Dm] * ya + g[:, 2 * Dm:] * ym
        x2_ref[...] = x1_ref[...] + _dot(y, wout_ref[...])
        yc_ref[...] = yc.astype(BF16)
        ya_ref[...] = ya.astype(BF16)
        ym_ref[...] = ym.astype(BF16)
        y_ref[...] = y.astype(BF16)

    rowW = pl.BlockSpec((tm, W), lambda i: (i, 0))
    rowD = pl.BlockSpec((tm, Dm), lambda i: (i, 0))
    full = lambda s: pl.BlockSpec(s, lambda i: (0, 0))
    return pl.pallas_call(
        body, name="mix_fwd", grid=(T // tm,),
        in_specs=[rowW, rowW, rowW, pl.BlockSpec((tm, 3 * Dm), lambda i: (i, 1)), full((1, 3 * Dm)), rowD,
                  full((W, Dm)), full((W, Dm)), full((W, Dm)), full((Dm, Dm))],
        out_specs=[rowD] * 5,
        out_shape=[jax.ShapeDtypeStruct((T, Dm), F32)] + [jax.ShapeDtypeStruct((T, Dm), BF16)] * 4,
        compiler_params=_params(("parallel",)))(c_act, o_att, o_mem, proj, b_gate, x1, w_pw, w_o, w_mo, w_out)


def _mix_bwd(dx2, yc, ya, ym, proj, b_gate, w_pw, w_o, w_mo, w_out):
    T, Dm = dx2.shape
    W = w_pw.shape[0]
    tm = _pick(T, (256, 128, 64, 32, 16, 8))

    def body(dx_ref, yc_ref, ya_ref, ym_ref, gl_ref, bg_ref, wpw_ref, wo_ref, wmo_ref, wout_ref,
             dyc_ref, dya_ref, dym_ref, dgl_ref, dbg_ref, dc_ref, doa_ref, dom_ref):
        i = pl.program_id(0)

        @pl.when(i == 0)
        def _():
            dbg_ref[...] = jnp.zeros_like(dbg_ref)

        dy = _dot(dx_ref[...], wout_ref[...], 'nt')
        g = _sig(gl_ref[...].astype(F32) + bg_ref[...])
        branches = ((yc_ref, dyc_ref, wpw_ref, dc_ref), (ya_ref, dya_ref, wo_ref, doa_ref),
                    (ym_ref, dym_ref, wmo_ref, dom_ref))
        for n, (y_ref, dyk_ref, w_ref, dk_ref) in enumerate(branches):
            gk = g[:, n * Dm:(n + 1) * Dm]
            dyk = dy * gk
            dgl = dyk * y_ref[...].astype(F32) * (1.0 - gk)
            dyk_ref[...] = dyk.astype(BF16)
            dgl_ref[:, n * Dm:(n + 1) * Dm] = dgl.astype(BF16)
            dbg_ref[:, n * Dm:(n + 1) * Dm] += jnp.sum(dgl, axis=0, keepdims=True)
            dk_ref[...] = _dot(dyk, w_ref[...], 'nt').astype(BF16)

    rowW = pl.BlockSpec((tm, W), lambda i: (i, 0))
    rowD = pl.BlockSpec((tm, Dm), lambda i: (i, 0))
    row3 = pl.BlockSpec((tm, 3 * Dm), lambda i: (i, 0))
    full = lambda s: pl.BlockSpec(s, lambda i: (0, 0))
    return pl.pallas_call(
        body, name="mix_bwd", grid=(T // tm,),
        in_specs=[rowD, rowD, rowD, rowD, pl.BlockSpec((tm, 3 * Dm), lambda i: (i, 1)), full((1, 3 * Dm)),
                  full((W, Dm)), full((W, Dm)), full((W, Dm)), full((Dm, Dm))],
        out_specs=[rowD, rowD, rowD, pl.BlockSpec((tm, 3 * Dm), lambda i: (i, 1)), full((1, 3 * Dm)),
                   rowW, rowW, rowW],
        out_shape=[jax.ShapeDtypeStruct((T, Dm), BF16)] * 3 + [jax.ShapeDtypeStruct((T, 6 * Dm), BF16),
                                                                jax.ShapeDtypeStruct((1, 3 * Dm), F32)]
        + [jax.ShapeDtypeStruct((T, W), BF16)] * 3,
        compiler_params=_params(("arbitrary",)))(dx2, yc, ya, ym, proj, b_gate, w_pw, w_o, w_mo, w_out)


def _ln_swish(cv, lg, lb):
    mu = jnp.mean(cv, axis=-1, keepdims=True)
    xc = cv - mu
    r = lax.rsqrt(jnp.mean(xc * xc, axis=-1, keepdims=True) + EPS)
    n = xc * r
    l = n * lg + lb
    return r, n, l


def _shift_copies(src, r0, win, shifts):
    win[...] = src[pl.ds(r0, CONV_TILE + CONV_PAD + 8), :]
    for s in range(8):
        shifts[s] = win[s:s + CONV_TILE + CONV_PAD, :]


def _tap(shifts, d):
    return shifts[d % 8, d - d % 8:d - d % 8 + CONV_TILE, :]


def _conv_fwd(proj3, dw_w, dw_b, ln_g, ln_b):
    Bl, S, _ = proj3.shape
    C, K, TS, PAD = CONV_WIDTH, CONV_KERNEL, CONV_TILE, CONV_PAD
    nt = S // TS

    def body(u_ref, w_ref, b_ref, lg_ref, lb_ref, cv_ref, c_ref, vbuf, win, shifts):
        vbuf[0:PAD, :] = jnp.zeros((PAD, C), F32)
        vbuf[S + PAD:S + PAD + 8, :] = jnp.zeros((8, C), F32)

        def glu(t, carry):
            r0 = pl.multiple_of(t * TS, TS)
            u = u_ref[pl.ds(r0, TS), :].astype(F32)
            vbuf[pl.ds(PAD + r0, TS), :] = u[:, :C] * _sig(u[:, C:])
            return carry

        lax.fori_loop(0, nt, glu, 0)

        def conv(t, carry):
            r0 = pl.multiple_of(t * TS, TS)
            _shift_copies(vbuf, r0, win, shifts)
            acc = jnp.zeros((TS, C), F32)
            for j in range(K):
                acc = acc + w_ref[j:j + 1, :] * _tap(shifts, PAD - (K - 1) + j)
            cv = acc + b_ref[...]
            cv_ref[pl.ds(r0, TS), :] = cv
            _, _, l = _ln_swish(cv, lg_ref[...], lb_ref[...])
            c_ref[pl.ds(r0, TS), :] = (l * _sig(l)).astype(BF16)
            return carry

        lax.fori_loop(0, nt, conv, 0)

    vec = pl.BlockSpec((1, C), lambda b: (0, 0))
    return pl.pallas_call(
        body, name="conv_fwd", grid=(Bl,),
        in_specs=[pl.BlockSpec((None, S, 2 * C), lambda b: (b, 0, 0)), pl.BlockSpec((K, C), lambda b: (0, 0)),
                  vec, vec, vec],
        out_specs=[pl.BlockSpec((None, S, C), lambda b: (b, 0, 0))] * 2,
        out_shape=[jax.ShapeDtypeStruct((Bl, S, C), F32), jax.ShapeDtypeStruct((Bl, S, C), BF16)],
        scratch_shapes=[pltpu.VMEM((S + PAD + 8, C), F32), pltpu.VMEM((TS + PAD + 8, C), F32),
                        pltpu.VMEM((8, TS + PAD, C), F32)],
        compiler_params=_params(("parallel",)))(proj3, dw_w, dw_b, ln_g, ln_b)


def _conv_bwd(proj3, cv, dc, dw_w, ln_g, ln_b):
    Bl, S, _ = proj3.shape
    C, K, TS, PAD = CONV_WIDTH, CONV_KERNEL, CONV_TILE, CONV_PAD
    nt = S // TS

    def body(u_ref, cv_ref, dc_ref, w_ref, lg_ref, lb_ref, du_ref, dw_ref, db_ref, dlg_ref, dlb_ref,
             vbuf, gbuf, win, shifts, dwacc):
        b = pl.program_id(0)

        @pl.when(b == 0)
        def _():
            dw_ref[...] = jnp.zeros_like(dw_ref)
            db_ref[...] = jnp.zeros_like(db_ref)
            dlg_ref[...] = jnp.zeros_like(dlg_ref)
            dlb_ref[...] = jnp.zeros_like(dlb_ref)

        vbuf[0:PAD, :] = jnp.zeros((PAD, C), F32)
        vbuf[S + PAD:S + PAD + 8, :] = jnp.zeros((8, C), F32)
        gbuf[S:S + PAD + 8, :] = jnp.zeros((PAD + 8, C), F32)
        dwacc[...] = jnp.zeros_like(dwacc)

        def norm_bwd(t, carry):
            r0 = pl.multiple_of(t * TS, TS)
            u = u_ref[pl.ds(r0, TS), :].astype(F32)
            vbuf[pl.ds(PAD + r0, TS), :] = u[:, :C] * _sig(u[:, C:])
            r, n, l = _ln_swish(cv_ref[pl.ds(r0, TS), :], lg_ref[...], lb_ref[...])
            s = _sig(l)
            dl = dc_ref[pl.ds(r0, TS), :].astype(F32) * s * (1.0 + l * (1.0 - s))
            dlg_ref[...] += jnp.sum(dl * n, axis=0, keepdims=True)
            dlb_ref[...] += jnp.sum(dl, axis=0, keepdims=True)
            dn = dl * lg_ref[...]
            dcv = r * (dn - jnp.mean(dn, axis=-1, keepdims=True) - n * jnp.mean(dn * n, axis=-1, keepdims=True))
            gbuf[pl.ds(r0, TS), :] = dcv
            db_ref[...] += jnp.sum(dcv, axis=0, keepdims=True)
            return carry

        lax.fori_loop(0, nt, norm_bwd, 0)

        def conv_bwd(t, carry):
            r0 = pl.multiple_of(t * TS, TS)
            _shift_copies(gbuf, r0, win, shifts)
            dv = jnp.zeros((TS, C), F32)
            for j in range(K):
                dv = dv + w_ref[j:j + 1, :] * _tap(shifts, K - 1 - j)
            u = u_ref[pl.ds(r0, TS), :].astype(F32)
            a, g = u[:, :C], u[:, C:]
            s = _sig(g)
            du_ref[pl.ds(r0, TS), 0:C] = (dv * s).astype(BF16)
            du_ref[pl.ds(r0, TS), C:2 * C] = (dv * a * s * (1.0 - s)).astype(BF16)
            dcv = gbuf[pl.ds(r0, TS), :]
            _shift_copies(vbuf, r0, win, shifts)
            for j in range(K):
                prod = dcv * _tap(shifts, PAD - (K - 1) + j)
                dwacc[j] += jnp.sum(prod.reshape(TS // 8, 8, C), axis=0)
            return carry

        lax.fori_loop(0, nt, conv_bwd, 0)
        dw_ref[...] += jnp.sum(dwacc[...], axis=1)

    vec = pl.BlockSpec((1, C), lambda b: (0, 0))
    seq = lambda w: pl.BlockSpec((None, S, w), lambda b: (b, 0, 0))
    return pl.pallas_call(
        body, name="conv_bwd", grid=(Bl,),
        in_specs=[seq(2 * C), seq(C), seq(C), pl.BlockSpec((K, C), lambda b: (0, 0)), vec, vec],
        out_specs=[seq(2 * C), pl.BlockSpec((K, C), lambda b: (0, 0)), vec, vec, vec],
        out_shape=[jax.ShapeDtypeStruct((Bl, S, 2 * C), BF16), jax.ShapeDtypeStruct((K, C), F32)]
        + [jax.ShapeDtypeStruct((1, C), F32)] * 3,
        scratch_shapes=[pltpu.VMEM((S + PAD + 8, C), F32), pltpu.VMEM((S + PAD + 8, C), F32),
                        pltpu.VMEM((TS + PAD + 8, C), F32), pltpu.VMEM((8, TS + PAD, C), F32),
                        pltpu.VMEM((K, 8, C), F32)],
        compiler_params=_params(("arbitrary",)))(proj3, cv, dc, dw_w, ln_g, ln_b)


def _att_bias(rel_bias):
    H = rel_bias.shape[0]
    Wd = KW + QB
    c = jnp.arange(Wd + 1)
    by_offset = rel_bias[:, jnp.clip(KW - c, -(CHUNK - 1), MAX_REL) + (CHUNK - 1)]
    flat = jnp.broadcast_to(by_offset[:, None, :], (H, QB, Wd + 1)).reshape(H, QB * (Wd + 1))
    skew = jnp.pad(flat, ((0, 0), (0, (QB + 1) * Wd - QB * (Wd + 1)))).reshape(H, QB + 1, Wd)[:, :QB, QB:]
    qi = jnp.arange(QB)[:, None]
    kj = jnp.arange(KW)[None, :]
    dchunk = ((KW - QB) + qi) // CHUNK - kj // CHUNK
    band = (dchunk >= 0) & (dchunk <= LEFT_CHUNKS)
    return jnp.where(band[None], skew, MASK_VALUE)


def _head_masks():
    lane = lax.broadcasted_iota(jnp.int32, (1, 128), 1)
    return (lane < 64, lane >= 64)


def _att_probs(qh, k2, bias, valid):
    s = _dot(qh, k2, 'nt') * ATT_SCALE + bias
    s = jnp.where(valid, s, MASK_VALUE)
    e = jnp.exp(s - jnp.max(s, axis=-1, keepdims=True))
    return e * (1.0 / jnp.sum(e, axis=-1, keepdims=True))


def _att_specs(S, q_col):
    nb = S // QB
    q_spec = pl.BlockSpec((None, QB, ATT_WIDTH), lambda b, i: (b, jnp.minimum(i, nb - 1), q_col))

    def kv_spec(col, kb):
        return pl.BlockSpec((None, QB, ATT_WIDTH),
                            lambda b, i: (b, jnp.clip(i - 2 + kb, 0, nb - 1), col))

    return q_spec, [kv_spec(3, kb) for kb in range(3)], [kv_spec(4, kb) for kb in range(3)]


def _att_fwd(proj3, bias):
    Bl, S, _ = proj3.shape
    nb = S // QB
    q_spec, k_specs, v_specs = _att_specs(S, 2)

    def body(q_ref, k0, k1, k2r, v0, v1, v2r, bias_ref, o_ref):
        i = pl.program_id(1)
        masks = _head_masks()
        valid = lax.broadcasted_iota(jnp.int32, (QB, KW), 1) >= (2 - i) * QB
        for pr in range(ATT_HEADS // 2):
            ls = slice(128 * pr, 128 * (pr + 1))
            q2 = q_ref[:, ls]
            k2 = jnp.concatenate([k0[:, ls], k1[:, ls], k2r[:, ls]], axis=0)
            v2 = jnp.concatenate([v0[:, ls], v1[:, ls], v2r[:, ls]], axis=0)
            o2 = jnp.zeros((QB, 128), F32)
            for hh in range(2):
                p = _att_probs(jnp.where(masks[hh], q2, 0), k2, bias_ref[2 * pr + hh], valid)
                o2 = o2 + _dot(p, jnp.where(masks[hh], v2, 0))
            o_ref[:, ls] = o2.astype(BF16)

    return pl.pallas_call(
        body, name="att_fwd", grid=(Bl, nb),
        in_specs=[q_spec] + k_specs + v_specs + [pl.BlockSpec((ATT_HEADS, QB, KW), lambda b, i: (0, 0, 0))],
        out_specs=pl.BlockSpec((None, QB, ATT_WIDTH), lambda b, i: (b, i, 0)),
        out_shape=jax.ShapeDtypeStruct((Bl, S, ATT_WIDTH), BF16),
        compiler_params=_params(("parallel", "arbitrary")))(*([proj3] * 7), bias)


def _att_bwd(proj3, do, bias):
    Bl, S, _ = proj3.shape
    nb = S // QB
    q_spec, k_specs, v_specs = _att_specs(S, 2)
    do_spec = pl.BlockSpec((None, QB, ATT_WIDTH), lambda b, i: (b, jnp.minimum(i, nb - 1), 0))
    kv_out = pl.BlockSpec((None, QB, ATT_WIDTH), lambda b, i: (b, jnp.clip(i - 2, 0, nb - 1), 0))
    bias_spec = pl.BlockSpec((ATT_HEADS, QB, KW), lambda b, i: (0, 0, 0))

    def body(q_ref, k0, k1, k2r, v0, v1, v2r, do_ref, bias_ref, dq_ref, dk_ref, dv_ref, db_ref, dkw, dvw):
        b = pl.program_id(0)
        i = pl.program_id(1)

        @pl.when((b == 0) & (i == 0))
        def _():
            db_ref[...] = jnp.zeros_like(db_ref)

        @pl.when(i == 0)
        def _():
            dkw[...] = jnp.zeros_like(dkw)
            dvw[...] = jnp.zeros_like(dvw)

        @pl.when(i < nb)
        def _():
            masks = _head_masks()
            valid = lax.broadcasted_iota(jnp.int32, (QB, KW), 1) >= (2 - i) * QB
            for pr in range(ATT_HEADS // 2):
                ls = slice(128 * pr, 128 * (pr + 1))
                q2 = q_ref[:, ls]
                do2 = do_ref[:, ls]
                k2 = jnp.concatenate([k0[:, ls], k1[:, ls], k2r[:, ls]], axis=0)
                v2 = jnp.concatenate([v0[:, ls], v1[:, ls], v2r[:, ls]], axis=0)
                dq2 = jnp.zeros((QB, 128), F32)
                dk2 = jnp.zeros((KW, 128), F32)
                dv2 = jnp.zeros((KW, 128), F32)
                for hh in range(2):
                    h = 2 * pr + hh
                    qh = jnp.where(masks[hh], q2, 0)
                    doh = jnp.where(masks[hh], do2, 0)
                    p = _att_probs(qh, k2, bias_ref[h], valid)
                    dp = _dot(doh, v2, 'nt')
                    ds = p * (dp - jnp.sum(p * dp, axis=-1, keepdims=True))
                    db_ref[h] += ds
                    dq2 = dq2 + _dot(ds, jnp.where(masks[hh], k2, 0))
                    dk2 = dk2 + _dot(ds, qh, 'tn')
                    dv2 = dv2 + _dot(p, doh, 'tn')
                dq_ref[:, ls] = (dq2 * ATT_SCALE).astype(BF16)
                dkw[:, ls] += dk2 * ATT_SCALE
                dvw[:, ls] += dv2

        dk_ref[...] = dkw[0:QB, :].astype(BF16)
        dv_ref[...] = dvw[0:QB, :].astype(BF16)
        for buf in (dkw, dvw):
            rest = buf[QB:KW, :]
            buf[0:KW - QB, :] = rest
            buf[KW - QB:KW, :] = jnp.zeros((QB, ATT_WIDTH), F32)

    blk = jax.ShapeDtypeStruct((Bl, S, ATT_WIDTH), BF16)
    return pl.pallas_call(
        body, name="att_bwd", grid=(Bl, nb + 2),
        in_specs=[q_spec] + k_specs + v_specs + [do_spec, bias_spec],
        out_specs=[do_spec, kv_out, kv_out, bias_spec],
        out_shape=[blk, blk, blk, jax.ShapeDtypeStruct((ATT_HEADS, QB, KW), F32)],
        scratch_shapes=[pltpu.VMEM((KW, ATT_WIDTH), F32), pltpu.VMEM((KW, ATT_WIDTH), F32)],
        compiler_params=_params(("arbitrary", "arbitrary")))(*([proj3] * 7), do, bias)


def _rel_bias_grad(dbias):
    H = dbias.shape[0]
    Wd = KW + QB
    padded = jnp.pad(dbias, ((0, 0), (0, 1), (QB, 0)))
    skew = padded.reshape(H, (QB + 1) * Wd)[:, :QB * (Wd + 1)].reshape(H, QB, Wd + 1)[:, :, :Wd]
    c = jnp.arange(Wd)[:, None]
    bins = (jnp.clip(KW - c, -(CHUNK - 1), MAX_REL) + (CHUNK - 1) == jnp.arange(N_REL)[None, :]).astype(F32)

    def body(s_ref, bins_ref, o_ref):
        col = jnp.sum(s_ref[...], axis=1)
        o_ref[...] = jnp.dot(col, bins_ref[...], preferred_element_type=F32, precision=lax.Precision.HIGHEST)

    return pl.pallas_call(
        body, name="rel_bias_grad", grid=(1,),
        in_specs=[pl.BlockSpec((H, QB, Wd), lambda i: (0, 0, 0)), pl.BlockSpec((Wd, N_REL), lambda i: (0, 0))],
        out_specs=pl.BlockSpec((H, N_REL), lambda i: (0, 0)), out_shape=jax.ShapeDtypeStruct((H, N_REL), F32),
        compiler_params=_params(("arbitrary",)))(skew, bins)


MEM_TILE = 512


def _mem_probs(qh, kh):
    s = _dot(qh, kh, 'nt') * MEM_SCALE
    e = jnp.exp(s - jnp.max(s, axis=-1, keepdims=True))
    return e * (1.0 / jnp.sum(e, axis=-1, keepdims=True))


def _mem_fwd(proj3, kv3):
    Bl, S, _ = proj3.shape
    tq = _pick(S, (MEM_TILE, 256))
    hd = MEM_HEAD_DIM

    def body(q_ref, kv_ref, o_ref):
        for h in range(MEM_HEADS):
            p = _mem_probs(q_ref[:, h * hd:(h + 1) * hd], kv_ref[:, h * hd:(h + 1) * hd])
            o_ref[:, h * hd:(h + 1) * hd] = _dot(p, kv_ref[:, MEM_WIDTH + h * hd:MEM_WIDTH + (h + 1) * hd]).astype(BF16)

    return pl.pallas_call(
        body, name="mem_fwd", grid=(Bl, S // tq),
        in_specs=[pl.BlockSpec((None, tq, MEM_WIDTH), lambda b, i: (b, i, 5)),
                  pl.BlockSpec((None, MEM_LEN, 2 * MEM_WIDTH), lambda b, i: (b, 0, 0))],
        out_specs=pl.BlockSpec((None, tq, MEM_WIDTH), lambda b, i: (b, i, 0)),
        out_shape=jax.ShapeDtypeStruct((Bl, S, MEM_WIDTH), BF16),
        compiler_params=_params(("parallel", "parallel")))(proj3, kv3)


def _mem_bwd(proj3, kv3, do):
    Bl, S, _ = proj3.shape
    tq = _pick(S, (MEM_TILE, 256))
    hd = MEM_HEAD_DIM

    def body(q_ref, kv_ref, do_ref, dq_ref, dkv_ref):
        i = pl.program_id(1)

        @pl.when(i == 0)
        def _():
            dkv_ref[...] = jnp.zeros_like(dkv_ref)

        for h in range(MEM_HEADS):
            ks = slice(h * hd, (h + 1) * hd)
            vs = slice(MEM_WIDTH + h * hd, MEM_WIDTH + (h + 1) * hd)
            qh, kh, vh, doh = q_ref[:, ks], kv_ref[:, ks], kv_ref[:, vs], do_ref[:, ks]
            p = _mem_probs(qh, kh)
            dp = _dot(doh, vh, 'nt')
            ds = p * (dp - jnp.sum(p * dp, axis=-1, keepdims=True))
            dq_ref[:, ks] = (_dot(ds, kh) * MEM_SCALE).astype(BF16)
            dkv_ref[:, ks] += _dot(ds, qh, 'tn') * MEM_SCALE
            dkv_ref[:, vs] += _dot(p, doh, 'tn')

    return pl.pallas_call(
        body, name="mem_bwd", grid=(Bl, S // tq),
        in_specs=[pl.BlockSpec((None, tq, MEM_WIDTH), lambda b, i: (b, i, 5)),
                  pl.BlockSpec((None, MEM_LEN, 2 * MEM_WIDTH), lambda b, i: (b, 0, 0)),
                  pl.BlockSpec((None, tq, MEM_WIDTH), lambda b, i: (b, i, 0))],
        out_specs=[pl.BlockSpec((None, tq, MEM_WIDTH), lambda b, i: (b, i, 0)),
                   pl.BlockSpec((None, MEM_LEN, 2 * MEM_WIDTH), lambda b, i: (b, 0, 0))],
        out_shape=[jax.ShapeDtypeStruct((Bl, S, MEM_WIDTH), BF16),
                   jax.ShapeDtypeStruct((Bl, MEM_LEN, 2 * MEM_WIDTH), F32)],
        compiler_params=_params(("parallel", "arbitrary")))(proj3, kv3, do)


def _position():
    x, y, c = lax.axis_index("x"), lax.axis_index("y"), lax.axis_index("c")
    return x, y, c, 4 * x + 2 * y + c


def _device(idx):
    return ((idx >> 2) & 1, (idx >> 1) & 1, idx & 1)


def _half_block(ref, axis, shard_shape, k, h):
    R, Cn = shard_shape
    if axis == 1:
        return ref.at[pl.ds(h * (R // 2), R // 2), pl.ds(k * Cn, Cn)]
    return ref.at[pl.ds(k * R + h * (R // 2), R // 2), :]


def _block(ref, axis, shard_shape, k):
    R, Cn = shard_shape
    if axis == 1:
        return ref.at[:, pl.ds(k * Cn, Cn)]
    return ref.at[pl.ds(k * R, R), :]


def _half(ref, h):
    R = ref.shape[0]
    return ref.at[pl.ds(h * (R // 2), R // 2), :]


ANY = pl.BlockSpec(memory_space=pl.ANY)


HBM = pl.BlockSpec(memory_space=pltpu.HBM)
SEM = pl.BlockSpec(memory_space=pltpu.SEMAPHORE)
VMEM_WHOLE = pl.BlockSpec(memory_space=pltpu.VMEM)
EFFECT = pltpu.SideEffectType.DATAFLOW_SIDE_EFFECTING


def _in_hbm(a):
    return pltpu.with_memory_space_constraint(a, pltpu.HBM)


def _split_start(body, name, sources, lands, n_copies):
    n = len(sources)
    out_shape, out_specs = [], []
    for _ in range(n):
        out_shape += [pltpu.SemaphoreType.DMA((n_copies,)), pltpu.SemaphoreType.DMA((n_copies,))]
        out_specs += [SEM, SEM]
    out_shape += [pltpu.HBM(a.shape, a.dtype) for a in list(sources) + list(lands)]
    out_specs += [HBM] * (2 * n)
    out_shape.append(jax.ShapeDtypeStruct((8, 128), F32))
    out_specs.append(VMEM_WHOLE)

    def call_body(*refs):
        srcs, lnds = refs[:n], refs[n:2 * n]
        sems = refs[2 * n:4 * n]
        token = refs[-1]
        body(srcs, lnds, sems[0::2], sems[1::2])
        token[...] = jnp.zeros_like(token)

    res = pl.pallas_call(
        call_body, name=name, in_specs=[HBM] * (2 * n), out_specs=out_specs, out_shape=out_shape,
        input_output_aliases={i: 2 * n + i for i in range(2 * n)},
        compiler_params=pltpu.CompilerParams(has_side_effects=EFFECT))(
            *[_in_hbm(a) for a in list(sources) + list(lands)])
    pairs = [(res[2 * w], res[2 * w + 1], res[2 * n + w], res[3 * n + w]) for w in range(n)]
    return pairs, res[-1]


def _split_wait(body, name, pairs, after):
    n = len(pairs)

    def call_body(*refs):
        srcs, lnds = refs[:n], refs[n:2 * n]
        sems = refs[2 * n:4 * n]
        body(srcs, lnds, sems[0::2], sems[1::2])

    args = [_in_hbm(p[2]) for p in pairs] + [_in_hbm(p[3]) for p in pairs]
    for p in pairs:
        args += [p[0], p[1]]
    res = pl.pallas_call(
        call_body, name=name, in_specs=[HBM] * (2 * n) + [SEM] * (2 * n) + [ANY], out_specs=[HBM] * (2 * n),
        out_shape=[pltpu.HBM(a.shape, a.dtype) for a in args[:2 * n]],
        input_output_aliases={i: i for i in range(2 * n)},
        compiler_params=pltpu.CompilerParams(has_side_effects=EFFECT))(*args, after)
    return res[:n], res[n:]


def _place_block(shard, axis, chip_idx, name, after=None):
    R, Cn = shard.shape
    tr = _pick(R, (256, 176, 128, 64, 32, 16, 8))
    nblk = R // tr

    def body(k_ref, s_ref, *rest):
        rest[-1][...] = s_ref[...]

    if axis == 1:
        out_shape, out_index = (R, 4 * Cn), lambda i, k: (i, k[0])
    else:
        out_shape, out_index = (4 * R, Cn), lambda i, k: (k[0] * nblk + i, 0)
    extra = () if after is None else (after,)
    return pl.pallas_call(
        body, name=name,
        grid_spec=pltpu.PrefetchScalarGridSpec(
            num_scalar_prefetch=1, grid=(nblk,),
            in_specs=[pl.BlockSpec((tr, Cn), lambda i, k: (i, 0))] + [ANY] * len(extra),
            out_specs=pl.BlockSpec((tr, Cn), out_index)),
        out_shape=jax.ShapeDtypeStruct(out_shape, shard.dtype),
        compiler_params=_params(("parallel",)))(chip_idx, shard, *extra)


def _gather_copy(srcs, lnds, send, recv, axes, shapes, w, j, me):
    chip = me >> 1
    return (pltpu.make_async_remote_copy(
        src_ref=srcs[w], dst_ref=_block(lnds[w], axes[w], shapes[w], chip), send_sem=send[w].at[j],
        recv_sem=recv[w].at[j], device_id=_device(me ^ (2 * (j + 1))), device_id_type=MESH),
            pltpu.make_async_remote_copy(
        src_ref=srcs[w], dst_ref=_block(lnds[w], axes[w], shapes[w], chip ^ (j + 1)), send_sem=send[w].at[j],
        recv_sem=recv[w].at[j], device_id=_device(me ^ (2 * (j + 1))), device_id_type=MESH))


def _gather_start(shards, lands, axes, name):
    shapes = [s.shape for s in shards]

    def body(srcs, lnds, send, recv):
        x, y, c, me = _position()
        for w in range(len(shards)):
            for j in range(3):
                _gather_copy(srcs, lnds, send, recv, axes, shapes, w, j, me)[0].start()

    return _split_start(body, name, shards, lands, 3)


def _gather_wait(pairs, axes, after, name):
    shapes = [p[2].shape for p in pairs]

    def body(srcs, lnds, send, recv):
        x, y, c, me = _position()
        for w in range(len(pairs)):
            for j in range(3):
                sent, landed = _gather_copy(srcs, lnds, send, recv, axes, shapes, w, j, me)
                sent.wait_send()
                landed.wait_recv()

    return _split_wait(body, name, pairs, after)[1]


def _shard_shape(grad, axis):
    return (grad.shape[0], grad.shape[1] // 4) if axis == 1 else (grad.shape[0] // 4, grad.shape[1])


def _scatter_copy(srcs, lnds, send, recv, axes, shapes, w, m, me):
    peer = me ^ m
    return pltpu.make_async_remote_copy(
        src_ref=_half_block(srcs[w], axes[w], shapes[w], peer >> 1, peer & 1), dst_ref=lnds[w].at[m - 1],
        send_sem=send[w].at[m - 1], recv_sem=recv[w].at[m - 1], device_id=_device(peer), device_id_type=MESH)


def _scatter_start(grads, axes, name):
    shapes = [_shard_shape(g, a) for g, a in zip(grads, axes)]
    lands = [lax.empty((N_DEV - 1, R // 2, Cn), BF16) for R, Cn in shapes]

    def body(srcs, lnds, send, recv):
        x, y, c, me = _position()
        for w in range(len(grads)):
            for m in range(1, N_DEV):
                _scatter_copy(srcs, lnds, send, recv, axes, shapes, w, m, me).start()

    return _split_start(body, name, grads, lands, N_DEV - 1)


def _scatter_wait(pairs, axes, after):
    shapes = [_shard_shape(p[2], a) for p, a in zip(pairs, axes)]

    def body(srcs, lnds, send, recv):
        x, y, c, me = _position()
        for w in range(len(pairs)):
            for m in range(1, N_DEV):
                cp = _scatter_copy(srcs, lnds, send, recv, axes, shapes, w, m, me)
                cp.wait_send()
                cp.wait_recv()

    return _split_wait(body, "scatter_wait", pairs, after)


def _sum_partials(own, parts, half, name):
    R, Cn = own.shape
    tr = _pick(R, (256, 176, 128, 64, 32, 16, 8))
    nblk = R // tr

    def body(half_ref, own_ref, p_ref, o_ref):
        acc = own_ref[...].astype(F32)
        for d in range(N_DEV - 1):
            acc = acc + p_ref[d].astype(F32)
        o_ref[...] = acc

    return pl.pallas_call(
        body, name=name,
        grid_spec=pltpu.PrefetchScalarGridSpec(
            num_scalar_prefetch=1, grid=(nblk,),
            in_specs=[pl.BlockSpec((tr, Cn), lambda i, hr: (i, 0)),
                      pl.BlockSpec((N_DEV - 1, tr, Cn), lambda i, hr: (0, i, 0))],
            out_specs=pl.BlockSpec((tr, Cn), lambda i, hr: (hr[0] * nblk + i, 0))),
        out_shape=jax.ShapeDtypeStruct((2 * R, Cn), F32),
        compiler_params=_params(("parallel",)))(half, own, parts)


def _exchange_halves(grads):
    n = len(grads)

    def body(*refs):
        outs = refs[n:2 * n]
        send, recv = refs[2 * n:]
        x, y, c, me = _position()

        def copy(w, half):
            rows = _half(outs[w], half)
            return pltpu.make_async_remote_copy(src_ref=rows, dst_ref=rows, send_sem=send.at[w],
                                                recv_sem=recv.at[w], device_id=_device(me ^ 1), device_id_type=MESH)

        for w in range(n):
            copy(w, c).start()
        for w in range(n):
            copy(w, 1 - c).wait_recv()
        for w in range(n):
            copy(w, c).wait_send()

    return pl.pallas_call(
        body, name="exchange_halves", in_specs=[ANY] * n, out_specs=[ANY] * n,
        out_shape=[jax.ShapeDtypeStruct(a.shape, a.dtype) for a in grads],
        input_output_aliases={i: i for i in range(n)},
        scratch_shapes=[pltpu.SemaphoreType.DMA((n,)), pltpu.SemaphoreType.DMA((n,))],
        compiler_params=pltpu.CompilerParams(has_side_effects=True))(*grads)


def _all_reduce_small(vec):
    R, L = vec.shape

    def body(v_ref, o_ref, buf, send, recv):
        x, y, c, me = _position()
        buf[me] = v_ref[...]

        def copy(m, slot):
            return pltpu.make_async_remote_copy(src_ref=v_ref, dst_ref=buf.at[slot], send_sem=send.at[m - 1],
                                                recv_sem=recv.at[m - 1], device_id=_device(me ^ m),
                                                device_id_type=MESH)

        for m in range(1, N_DEV):
            copy(m, me).start()
        for m in range(1, N_DEV):
            copy(m, me ^ m).wait_recv()
        for m in range(1, N_DEV):
            copy(m, me).wait_send()
        acc = buf[0]
        for d in range(1, N_DEV):
            acc = acc + buf[d]
        o_ref[...] = acc

    vm = pl.BlockSpec(memory_space=pltpu.VMEM)
    return pl.pallas_call(
        body, name="all_reduce_small", in_specs=[vm], out_specs=vm, out_shape=jax.ShapeDtypeStruct((R, L), F32),
        scratch_shapes=[pltpu.VMEM((N_DEV, R, L), F32), pltpu.SemaphoreType.DMA((N_DEV - 1,)),
                        pltpu.SemaphoreType.DMA((N_DEV - 1,))],
        compiler_params=pltpu.CompilerParams(has_side_effects=True))(vec)


def _adamw(w, g, m, v, name):
    R, Cn = w.shape
    tr = _pick(R, (256, 176, 128, 64, 40, 32, 16, 8))

    def body(w_ref, g_ref, m_ref, v_ref, d_ref, nm_ref, nv_ref):
        gv = g_ref[...]
        nm = ADAM_B1 * m_ref[...] + (1.0 - ADAM_B1) * gv
        nv = ADAM_B2 * v_ref[...] + (1.0 - ADAM_B2) * (gv * gv)
        m_hat = nm / (1.0 - ADAM_B1 ** ADAM_STEP)
        v_hat = nv / (1.0 - ADAM_B2 ** ADAM_STEP)
        d_ref[...] = -ADAM_LR * (m_hat / (jnp.sqrt(v_hat) + ADAM_EPS) + ADAM_WD * w_ref[...])
        nm_ref[...] = nm
        nv_ref[...] = nv

    spec = pl.BlockSpec((tr, Cn), lambda i: (i, 0))
    return pl.pallas_call(
        body, name=name, grid=(R // tr,), in_specs=[spec] * 4, out_specs=[spec] * 3,
        out_shape=[jax.ShapeDtypeStruct((R, Cn), F32)] * 3, compiler_params=_params(("parallel",)))(w, g, m, v)


def _pack(arrays, rows):
    flat = jnp.concatenate([a.reshape(-1).astype(F32) for a in arrays])
    return jnp.pad(flat, (0, rows * 128 - flat.shape[0])).reshape(rows, 128)


def _unpack(packed, shapes):
    flat = packed.reshape(-1)
    out, off = [], 0
    for s in shapes:
        size = 1
        for d in s:
            size *= d
        out.append(flat[off:off + size].reshape(s))
        off += size
    return out


def _ffn_fwd(x, norm, arrived, tag, after=None):
    h = _rms_fwd(x, norm, f"{tag}_norm", after=after)
    w_up = arrived(f"{tag}_w_up", h)
    a, b, act = _ffn_up(h, w_up, f"{tag}_up")
    w_down = arrived(f"{tag}_w_down", act)
    out = _mm(act, w_down, 'nn', f"{tag}_down", F32, res=x, scale=0.5)
    return out, (h, a, b, act, w_up, w_down)


def _ffn_bwd(dout, x, norm, saved, tag, send):
    h, a, b, act, w_up, w_down = saved
    g_down = _mm(act, dout, 'tn', f"{tag}_down_dw", BF16, scale=0.5)
    token = send([f"{tag}_w_down"], [g_down])
    da, db = _ffn_down_bwd(dout, w_down, a, b, f"{tag}_down_dx", after=token)
    g_up = _dw_pieces(h, [da, db], f"{tag}_up_dw")
    token = send([f"{tag}_w_up"], [g_up])
    return _dx_rms_bwd([da, db], w_up, x, norm, dout, f"{tag}_up_dx", after=token)


def kernel(x, mem, ffn1_norm, ffn1_w_up, ffn1_w_down, mix_norm, mem_norm, w_in, b_gate, conv_dw_w, conv_dw_b, conv_ln_g, conv_ln_b, conv_w_pw, att_rel_bias, att_w_o, mem_w_kv, mem_w_o, w_out, ffn2_norm, ffn2_w_up, ffn2_w_down, final_norm, loss_target, m_ffn1_norm, m_ffn1_w_up, m_ffn1_w_down, m_mix_norm, m_mem_norm, m_w_in, m_b_gate, m_conv_dw_w, m_conv_dw_b, m_conv_ln_g, m_conv_ln_b, m_conv_w_pw, m_att_rel_bias, m_att_w_o, m_mem_w_kv, m_mem_w_o, m_w_out, m_ffn2_norm, m_ffn2_w_up, m_ffn2_w_down, m_final_norm, v_ffn1_norm, v_ffn1_w_up, v_ffn1_w_down, v_mix_norm, v_mem_norm, v_w_in, v_b_gate, v_conv_dw_w, v_conv_dw_b, v_conv_ln_g, v_conv_ln_b, v_conv_w_pw, v_att_rel_bias, v_att_w_o, v_mem_w_kv, v_mem_w_o, v_w_out, v_ffn2_norm, v_ffn2_w_up, v_ffn2_w_down, v_final_norm):
    given = dict(locals())
    wts = {n: given[n] for n in WEIGHTS}
    mom1 = {n: given["m_" + n] for n in WEIGHTS}
    mom2 = {n: given["v_" + n] for n in WEIGHTS}
    Bl, S, Dm = x.shape
    T = Bl * S
    x0 = x.reshape(T, Dm)
    tgt = loss_target.reshape(T, Dm)
    mem2 = mem.reshape(Bl * MEM_LEN, Dm)

    big_names = [n for n, _ in BIG]
    big_axes = [a for _, a in BIG]
    chip = 2 * lax.axis_index("x") + lax.axis_index("y")

    core = lax.axis_index("c")
    axis_of = dict(BIG)

    gather_groups = [['ffn1_w_up'], ['ffn1_w_down'], ['w_in', 'conv_dw_w'],
                     ['mem_w_kv', 'conv_w_pw', 'att_w_o', 'mem_w_o', 'w_out'], ['ffn2_w_up'], ['ffn2_w_down']]
    gather_names = [n for grp in gather_groups for n in grp]
    gather_axes = [axis_of.get(n, 1) for n in gather_names]
    shards = [jnp.pad(conv_dw_w[0], ((0, 1), (0, 0))) if n == 'conv_dw_w' else wts[n][0].astype(BF16)
              for n in gather_names]
    chip_idx = chip.reshape(1).astype(jnp.int32)
    first, first_token = _gather_start(
        shards[:1], [_place_block(shards[0], gather_axes[0], chip_idx, f"place_{gather_names[0]}")],
        gather_axes[:1], "gather_start_first")
    lands = [_place_block(sh, a, chip_idx, f"place_{n}", after=first_token)
             for sh, a, n in zip(shards[1:], gather_axes[1:], gather_names[1:])]
    rest, gather_token = _gather_start(shards[1:], lands, gather_axes[1:], "gather_start_rest")
    in_flight = dict(zip(gather_names, first + rest))
    full = {}

    def arrived(name, after):
        if name not in full:
            grp = next(grp for grp in gather_groups if name in grp)
            lands = _gather_wait([in_flight[n] for n in grp], [axis_of.get(n, 1) for n in grp], after,
                                 f"gather_wait_{grp[0]}")
            full.update(zip(grp, lands))
        return full[name]

    scattering = {}

    def send(names, grads):
        pairs, token = _scatter_start(grads, [axis_of[n] for n in names], f"scatter_start_{names[0]}")
        scattering.update(zip(names, pairs))
        return token

    final_g = final_norm.reshape(1, Dm)
    bias = _att_bias(att_rel_bias[0] + first_token[:1, :1])

    x1, ffn1_saved = _ffn_fwd(x0, ffn1_norm, arrived, "ffn1", after=gather_token)
    h = _rms_fwd(x1, mix_norm, "mix_norm")
    w_in_full = arrived('w_in', h)
    dw_full = full['conv_dw_w'][:CONV_KERNEL]
    proj = _mm(h, w_in_full, 'nn', "w_in", BF16)
    proj3 = proj.reshape(Bl, S, proj.shape[1])
    cv, c_act = _conv_fwd(proj3, dw_full, conv_dw_b, conv_ln_g, conv_ln_b)
    o_att = _att_fwd(proj3, bias)
    mem_h = _rms_fwd(mem2, mem_norm, "mem_norm")
    kv = _mm(mem_h, arrived('mem_w_kv', o_att), 'nn', "mem_kv", BF16)
    kv3 = kv.reshape(Bl, MEM_LEN, 2 * MEM_WIDTH)
    o_mem = _mem_fwd(proj3, kv3)
    c_act2, o_att2, o_mem2 = c_act.reshape(T, -1), o_att.reshape(T, -1), o_mem.reshape(T, -1)
    x2, yc, ya, ym, y = _mix_fwd(c_act2, o_att2, o_mem2, proj, b_gate, x1, full['conv_w_pw'], full['att_w_o'],
                                 full['mem_w_o'], full['w_out'])
    x3, ffn2_saved = _ffn_fwd(x2, ffn2_norm, arrived, "ffn2")
    dx3, g_final, loss_vec = _final_fwd_bwd(x3, tgt, final_g)

    g = {}
    dx2, g['ffn2_norm'] = _ffn_bwd(dx3, x2, ffn2_norm, ffn2_saved, "ffn2", send)
    dyc, dya, dym, dgl, g['b_gate'], dc, doa, dom = _mix_bwd(
        dx2, yc, ya, ym, proj, b_gate, full['conv_w_pw'], full['att_w_o'], full['mem_w_o'], full['w_out'])
    token = send(['w_out', 'conv_w_pw', 'att_w_o', 'mem_w_o'],
                 [_mm(y, dx2, 'tn', "w_out_dw", BF16), _mm(c_act2, dyc, 'tn', "conv_pw_dw", BF16),
                  _mm(o_att2, dya, 'tn', "att_o_dw", BF16), _mm(o_mem2, dym, 'tn', "mem_o_dw", BF16)])
    du, g_dw, g['conv_dw_b'], g['conv_ln_g'], g['conv_ln_b'] = _conv_bwd(
        proj3, cv, dc.reshape(Bl, S, -1), dw_full, conv_ln_g, conv_ln_b)
    dq, dk, dv, dbias = _att_bwd(proj3, doa.reshape(Bl, S, -1), bias)
    g['att_rel_bias'] = _rel_bias_grad(dbias)
    dmq, dkv = _mem_bwd(proj3, kv3, dom.reshape(Bl, S, -1))
    dkv2 = dkv.reshape(Bl * MEM_LEN, 2 * MEM_WIDTH)
    g_kv = _mm(mem_h, dkv2, 'tn', "mem_kv_dw", BF16, after=token)
    dmem_h = _mm(dkv2, full['mem_w_kv'], 'nt', "mem_kv_dx", F32)
    _, g['mem_norm'] = _rms_bwd(mem2, mem_norm, dmem_h, dmem_h, "mem_norm_bwd")
    left = jnp.concatenate([du.reshape(T, -1), dq.reshape(T, -1), dk.reshape(T, -1), dv.reshape(T, -1),
                            dmq.reshape(T, -1)], axis=1)
    dproj = lax.dynamic_update_slice(dgl, left, (0, 0))
    token = send(['mem_w_kv', 'w_in'], [g_kv, _mm(h, dproj, 'tn', "w_in_dw", BF16)])
    dx1, g['mix_norm'] = _dx_rms_bwd([dproj], w_in_full, x1, mix_norm, dx2, "w_in_dx", after=token)
    dx0, g['ffn1_norm'] = _ffn_bwd(dx1, x0, ffn1_norm, ffn1_saved, "ffn1", send)
    g['final_norm'] = g_final

    sent, landed = _scatter_wait([scattering[n] for n in big_names], big_axes, dx0)
    halves = []
    half_idx = core.reshape(1).astype(jnp.int32)
    for n, a, own_full, parts in zip(big_names, big_axes, sent, landed):
        R, Cn = _shard_shape(own_full, a)
        start = (core * (R // 2), chip * Cn) if a == 1 else (chip * R + core * (R // 2), 0)
        own = lax.dynamic_slice(own_full, start, (R // 2, Cn))
        halves.append(_sum_partials(own, parts, half_idx, f"sum_{n}"))
    for n, sg in zip(big_names, _exchange_halves(halves)):
        g[n] = sg

    small_shapes = [wts[n].shape for n in SMALL]
    n_small = sum(int(wts[n].size) for n in SMALL)
    n_red = n_small + CONV_KERNEL * CONV_WIDTH
    red = _all_reduce_small(_pack([g[n] for n in SMALL] + [g_dw], -(-n_red // 1024) * 8))
    red_list = _unpack(red, small_shapes + [(CONV_KERNEL, CONV_WIDTH)])
    for n, rg in zip(SMALL, red_list[:-1]):
        g[n] = rg
    dw_cols = conv_dw_w.shape[2]
    g['conv_dw_w'] = lax.dynamic_slice(red_list[-1], (0, chip * dw_cols), (CONV_KERNEL, dw_cols))[None]

    delta, new_m, new_v = {}, {}, {}
    for n in big_names:
        g[n] = g[n][None]
        d, nm, nv = _adamw(wts[n][0], g[n][0], mom1[n][0], mom2[n][0], f"adamw_{n}")
        delta[n], new_m[n], new_v[n] = d[None], nm[None], nv[None]
    rest = SMALL + ['conv_dw_w']
    rest_shapes = [wts[n].shape for n in rest]
    rows = -(-sum(int(wts[n].size) for n in rest) // 1024) * 8
    packed = [_pack([src[n] for n in rest], rows) for src in (wts, g, mom1, mom2)]
    for out, res in zip((delta, new_m, new_v), _adamw(*packed, "adamw_small")):
        for n, a in zip(rest, _unpack(res, rest_shapes)):
            out[n] = a

    loss = lax.psum(loss_vec[0, 0], ("x", "y", "c"))
    grad_x = dx0.reshape(Bl, S, Dm)
    return (loss, grad_x, *[g[n] for n in WEIGHTS], *[delta[n] for n in WEIGHTS],
            *[new_m[n] for n in WEIGHTS], *[new_v[n] for n in WEIGHTS])
```

```python
import jax
import jax.numpy as jnp
from jax import lax
from jax.experimental import pallas as pl
from jax.experimental.pallas import tpu as pltpu

F32 = jnp.float32
BF16 = jnp.bfloat16

D_MODEL = 1024
D_FF = 2816
CHUNK = 64
LEFT_CHUNKS = 8
MAX_REL = 128
N_REL = (CHUNK - 1) + MAX_REL + 1
CONV_WIDTH = 512
CONV_KERNEL = 31
ATT_HEADS = 8
ATT_WIDTH = 512
MEM_LEN = 256
MEM_HEADS = 4
MEM_HEAD_DIM = 128
MEM_WIDTH = 512
EPS = 1e-6
MASK_VALUE = -1e30
ATT_SCALE = 64 ** -0.5
MEM_SCALE = 128 ** -0.5

ADAM_LR = 0.001
ADAM_B1 = 0.9
ADAM_B2 = 0.999
ADAM_EPS = 1e-08
ADAM_WD = 0.01
ADAM_STEP = 10

QB = 256
KW = 3 * QB
CONV_PAD = 32
CONV_TILE = 256

VMEM_LIMIT = 56 << 20
MXU_COLS = 256

WEIGHTS = ['ffn1_norm', 'ffn1_w_up', 'ffn1_w_down', 'mix_norm', 'mem_norm', 'w_in', 'b_gate', 'conv_dw_w',
           'conv_dw_b', 'conv_ln_g', 'conv_ln_b', 'conv_w_pw', 'att_rel_bias', 'att_w_o', 'mem_w_kv', 'mem_w_o',
           'w_out', 'ffn2_norm', 'ffn2_w_up', 'ffn2_w_down', 'final_norm']
BIG = [('ffn1_w_up', 1), ('ffn1_w_down', 0), ('w_in', 1), ('conv_w_pw', 1), ('att_w_o', 1), ('mem_w_kv', 0),
       ('mem_w_o', 1), ('w_out', 0), ('ffn2_w_up', 1), ('ffn2_w_down', 0)]
SMALL = ['ffn1_norm', 'mix_norm', 'mem_norm', 'b_gate', 'conv_dw_b', 'conv_ln_g', 'conv_ln_b', 'att_rel_bias',
         'ffn2_norm', 'final_norm']
N_CHIPS = 4
N_DEV = 8
MESH = pl.DeviceIdType.MESH


def _pick(n, cands):
    for c in cands:
        if n % c == 0:
            return c
    return n


def _sig(x):
    return 0.5 * jnp.tanh(0.5 * x) + 0.5


def _params(sem=None, vmem=VMEM_LIMIT):
    return pltpu.CompilerParams(dimension_semantics=sem, vmem_limit_bytes=vmem)


def _dot(a, b, mode='nn'):
    dims = {'nn': (((1,), (0,)), ((), ())), 'nt': (((1,), (1,)), ((), ())), 'tn': (((0,), (0,)), ((), ()))}[mode]
    return lax.dot_general(a.astype(BF16), b.astype(BF16), dims, preferred_element_type=F32)


def _mm(a, b, mode, name, out_dtype, res=None, scale=1.0, after=None):
    if mode == 'nn':
        (M, C), (_, N) = a.shape, b.shape
    elif mode == 'nt':
        (M, C), (N, _) = a.shape, b.shape
    else:
        (C, M), (_, N) = a.shape, b.shape
    tm = _pick(M, (1024, 1408, 512, 256, 128))
    tn = _pick(N, (1024, 1408, 512, 256, 128))
    tc = _pick(C, (1024, 1408, 512, 256, 128))
    nk = C // tc
    if mode == 'nn':
        a_spec = pl.BlockSpec((tm, tc), lambda i, j, k: (i, k))
        b_spec = pl.BlockSpec((tc, tn), lambda i, j, k: (k, j))
    elif mode == 'nt':
        a_spec = pl.BlockSpec((tm, tc), lambda i, j, k: (i, k))
        b_spec = pl.BlockSpec((tn, tc), lambda i, j, k: (j, k))
    else:
        a_spec = pl.BlockSpec((tc, tm), lambda i, j, k: (k, i))
        b_spec = pl.BlockSpec((tc, tn), lambda i, j, k: (k, j))
    o_spec = pl.BlockSpec((tm, tn), lambda i, j, k: (i, j))
    has_res = res is not None
    has_after = after is not None

    def body(*refs):
        a_ref, b_ref = refs[:2]
        r_ref = refs[2] if has_res else None
        o_ref, acc_ref = refs[-2:]
        k = pl.program_id(2)

        def finish(acc):
            if scale != 1.0:
                acc = acc * scale
            if r_ref is not None:
                acc = r_ref[...] + acc
            o_ref[...] = acc.astype(o_ref.dtype)

        if nk == 1:
            finish(_dot(a_ref[...], b_ref[...], mode))
        else:
            @pl.when(k == 0)
            def _():
                acc_ref[...] = jnp.zeros_like(acc_ref)

            acc_ref[...] += _dot(a_ref[...], b_ref[...], mode)

            @pl.when(k == nk - 1)
            def _():
                finish(acc_ref[...])

    in_specs = [a_spec, b_spec] + ([o_spec] if has_res else []) + ([ANY] if has_after else [])
    args = (a, b) + ((res,) if has_res else ()) + ((after,) if has_after else ())
    acc_shape = (tm, tn) if nk > 1 else (8, 128)
    return pl.pallas_call(
        body, name=name, grid=(M // tm, N // tn, nk), in_specs=in_specs, out_specs=o_spec,
        out_shape=jax.ShapeDtypeStruct((M, N), out_dtype), scratch_shapes=[pltpu.VMEM(acc_shape, F32)],
        compiler_params=_params(("parallel", "parallel", "arbitrary")))(*args)


def _row_tile(T):
    return _pick(T, (512, 256, 128, 64, 32, 16, 8))


def _rms_fwd(x, g, name, after=None):
    T, Dm = x.shape
    tm = _row_tile(T)

    def body(x_ref, g_ref, *rest):
        o_ref = rest[-1]
        xv = x_ref[...]
        r = lax.rsqrt(jnp.mean(xv * xv, axis=-1, keepdims=True) + EPS)
        o_ref[...] = ((xv * r) * g_ref[...]).astype(o_ref.dtype)

    extra = () if after is None else (after,)
    return pl.pallas_call(
        body, name=name, grid=(T // tm,),
        in_specs=[pl.BlockSpec((tm, Dm), lambda i: (i, 0)), pl.BlockSpec((1, Dm), lambda i: (0, 0))]
        + [ANY] * len(extra),
        out_specs=pl.BlockSpec((tm, Dm), lambda i: (i, 0)), out_shape=jax.ShapeDtypeStruct((T, Dm), BF16),
        compiler_params=_params(("parallel",)))(x, g, *extra)


def _rms_bwd(x, g, dh, dres, name):
    T, Dm = x.shape
    tm = _row_tile(T)

    def body(x_ref, g_ref, dh_ref, dr_ref, dx_ref, dg_ref):
        i = pl.program_id(0)
        xv = x_ref[...]
        r = lax.rsqrt(jnp.mean(xv * xv, axis=-1, keepdims=True) + EPS)
        xr = xv * r
        dh_v = dh_ref[...].astype(F32)
        dyg = dh_v * g_ref[...]
        dx = r * (dyg - xr * jnp.mean(dyg * xr, axis=-1, keepdims=True))
        dx_ref[...] = dr_ref[...] + dx

        @pl.when(i == 0)
        def _():
            dg_ref[...] = jnp.zeros_like(dg_ref)

        dg_ref[...] += jnp.sum(dh_v * xr, axis=0, keepdims=True)

    row = pl.BlockSpec((tm, Dm), lambda i: (i, 0))
    vec = pl.BlockSpec((1, Dm), lambda i: (0, 0))
    return pl.pallas_call(
        body, name=name, grid=(T // tm,), in_specs=[row, vec, row, row], out_specs=[row, vec],
        out_shape=[jax.ShapeDtypeStruct((T, Dm), F32), jax.ShapeDtypeStruct((1, Dm), F32)],
        compiler_params=_params(("arbitrary",)))(x, g, dh, dres)


def _final_fwd_bwd(x3, tgt, g):
    T, Dm = x3.shape
    tm = _row_tile(T)

    def body(x_ref, t_ref, g_ref, dx_ref, dg_ref, loss_ref):
        i = pl.program_id(0)
        xv = x_ref[...]
        gg = g_ref[...]
        r = lax.rsqrt(jnp.mean(xv * xv, axis=-1, keepdims=True) + EPS)
        xr = xv * r
        err = xr * gg - t_ref[...]
        dout = err * (1.0 / Dm)
        dyg = dout * gg
        dx_ref[...] = r * (dyg - xr * jnp.mean(dyg * xr, axis=-1, keepdims=True))

        @pl.when(i == 0)
        def _():
            dg_ref[...] = jnp.zeros_like(dg_ref)
            loss_ref[...] = jnp.zeros_like(loss_ref)

        dg_ref[...] += jnp.sum(dout * xr, axis=0, keepdims=True)
        loss_ref[...] += jnp.zeros_like(loss_ref) + (0.5 / Dm) * jnp.sum(err * err)

    row = pl.BlockSpec((tm, Dm), lambda i: (i, 0))
    vec = pl.BlockSpec((1, Dm), lambda i: (0, 0))
    one = pl.BlockSpec((1, 128), lambda i: (0, 0))
    return pl.pallas_call(
        body, name="final_fwd_bwd", grid=(T // tm,), in_specs=[row, row, vec], out_specs=[row, vec, one],
        out_shape=[jax.ShapeDtypeStruct((T, Dm), F32), jax.ShapeDtypeStruct((1, Dm), F32),
                   jax.ShapeDtypeStruct((1, 128), F32)],
        compiler_params=_params(("arbitrary",)))(x3, tgt, g)


def _ffn_up(h, w_up, name):
    T, K = h.shape
    Fh = w_up.shape[1] // 2
    tm = _pick(T, (1024, 512, 256, 128))
    tn = _pick(Fh, (1408, 512, 256, 128))
    nj = Fh // tn

    def body(h_ref, wa_ref, wb_ref, a_ref, b_ref, act_ref):
        hv = h_ref[...]
        for c0 in range(0, tn, MXU_COLS):
            cs = slice(c0, min(c0 + MXU_COLS, tn))
            a = _dot(hv, wa_ref[:, cs])
            b = _dot(hv, wb_ref[:, cs])
            a_ref[:, cs] = a.astype(BF16)
            b_ref[:, cs] = b.astype(BF16)
            act_ref[:, cs] = (a * _sig(a) * b).astype(BF16)

    out = pl.BlockSpec((tm, tn), lambda i, j: (i, j))
    return pl.pallas_call(
        body, name=name, grid=(T // tm, nj),
        in_specs=[pl.BlockSpec((tm, K), lambda i, j: (i, 0)), pl.BlockSpec((K, tn), lambda i, j: (0, j)),
                  pl.BlockSpec((K, tn), lambda i, j: (0, j + nj))],
        out_specs=[out, out, out], out_shape=[jax.ShapeDtypeStruct((T, Fh), BF16)] * 3,
        compiler_params=_params(("parallel", "parallel")))(h, w_up, w_up)


def _ffn_down_bwd(dout, w_down, a, b, name, after=None):
    T, Dm = dout.shape
    Fh = w_down.shape[0]
    tm = _pick(T, (1024, 512, 256, 128))
    tn = _pick(Fh, (1408, 512, 256, 128))

    def body(d_ref, w_ref, a_ref, b_ref, *rest):
        da_ref, db_ref = rest[-2:]
        dv = d_ref[...].astype(BF16)
        for c0 in range(0, tn, MXU_COLS):
            cs = slice(c0, min(c0 + MXU_COLS, tn))
            dact = _dot(dv, w_ref[cs, :], 'nt') * 0.5
            av = a_ref[:, cs].astype(F32)
            bv = b_ref[:, cs].astype(F32)
            s = _sig(av)
            da_ref[:, cs] = (dact * bv * s * (1.0 + av * (1.0 - s))).astype(BF16)
            db_ref[:, cs] = (dact * av * s).astype(BF16)

    tile = pl.BlockSpec((tm, tn), lambda i, j: (i, j))
    extra = () if after is None else (after,)
    return pl.pallas_call(
        body, name=name, grid=(T // tm, Fh // tn),
        in_specs=[pl.BlockSpec((tm, Dm), lambda i, j: (i, 0)), pl.BlockSpec((tn, Dm), lambda i, j: (j, 0)),
                  tile, tile] + [ANY] * len(extra),
        out_specs=[tile, tile], out_shape=[jax.ShapeDtypeStruct((T, Fh), BF16)] * 2,
        compiler_params=_params(("parallel", "parallel")))(dout, w_down, a, b, *extra)


def _dx_rms_bwd(pieces, w, x, g, dres, name, after=None):
    T, Dm = x.shape
    width = pieces[0].shape[1]
    tm = _pick(T, (1024, 512, 256, 128))
    tc = _pick(width, (1408, 1024, 512, 256, 128))
    per = width // tc
    nk = per * len(pieces)
    npc = len(pieces)
    rows = _pick(tm, (256, 128))

    def body(*refs):
        p_refs = refs[:npc]
        w_ref, x_ref, g_ref, dr_ref = refs[npc:npc + 4]
        dx_ref, dg_ref, acc_ref = refs[-3:]
        i = pl.program_id(0)
        k = pl.program_id(1)

        @pl.when(k == 0)
        def _():
            acc_ref[...] = jnp.zeros_like(acc_ref)

        @pl.when((i == 0) & (k == 0))
        def _():
            dg_ref[...] = jnp.zeros_like(dg_ref)

        for p in range(npc):
            @pl.when((k >= p * per) & (k < (p + 1) * per))
            def _(p=p):
                acc_ref[...] += _dot(p_refs[p][...], w_ref[...], 'nt')

        @pl.when(k == nk - 1)
        def _():
            def chunk(c, carry):
                rs = pl.ds(pl.multiple_of(c * rows, rows), rows)
                dh = acc_ref[rs, :]
                xv = x_ref[rs, :]
                r = lax.rsqrt(jnp.mean(xv * xv, axis=-1, keepdims=True) + EPS)
                xr = xv * r
                dyg = dh * g_ref[...]
                dx_ref[rs, :] = dr_ref[rs, :] + r * (dyg - xr * jnp.mean(dyg * xr, axis=-1, keepdims=True))
                dg_ref[...] += jnp.sum(dh * xr, axis=0, keepdims=True)
                return carry

            lax.fori_loop(0, tm // rows, chunk, 0)

    def piece_spec(p):
        return pl.BlockSpec((tm, tc), lambda i, k: (i, jnp.clip(k - p * per, 0, per - 1)))

    once = pl.BlockSpec((tm, Dm), lambda i, k: (i, 0), pipeline_mode=pl.Buffered(1))
    row = pl.BlockSpec((tm, Dm), lambda i, k: (i, 0))
    vec = pl.BlockSpec((1, Dm), lambda i, k: (0, 0))
    extra = () if after is None else (after,)
    return pl.pallas_call(
        body, name=name, grid=(T // tm, nk),
        in_specs=[piece_spec(p) for p in range(npc)] + [pl.BlockSpec((Dm, tc), lambda i, k: (0, k)), once, vec, once]
        + [ANY] * len(extra),
        out_specs=[row, vec], out_shape=[jax.ShapeDtypeStruct((T, Dm), F32), jax.ShapeDtypeStruct((1, Dm), F32)],
        scratch_shapes=[pltpu.VMEM((tm, Dm), F32)],
        compiler_params=_params(("arbitrary", "arbitrary")))(*pieces, w, x, g, dres, *extra)


def _dw_pieces(a, pieces, name):
    C, M = a.shape
    width = pieces[0].shape[1]
    npc = len(pieces)
    tm = _pick(M, (1024, 512, 256, 128))
    tn = _pick(width, (1408, 1024, 512, 256, 128))
    tc = _pick(C, (1024, 512, 256, 128))
    per = width // tn
    nk = C // tc

    def body(*refs):
        a_ref = refs[0]
        p_refs = refs[1:1 + npc]
        o_ref, acc_ref = refs[-2:]
        j = pl.program_id(1)
        k = pl.program_id(2)

        @pl.when(k == 0)
        def _():
            acc_ref[...] = jnp.zeros_like(acc_ref)

        for p in range(npc):
            @pl.when((j >= p * per) & (j < (p + 1) * per))
            def _(p=p):
                acc_ref[...] += _dot(a_ref[...], p_refs[p][...], 'tn')

        @pl.when(k == nk - 1)
        def _():
            o_ref[...] = acc_ref[...].astype(o_ref.dtype)

    def piece_spec(p):
        return pl.BlockSpec((tc, tn), lambda i, j, k: (k, jnp.clip(j - p * per, 0, per - 1)))

    return pl.pallas_call(
        body, name=name, grid=(M // tm, per * npc, nk),
        in_specs=[pl.BlockSpec((tc, tm), lambda i, j, k: (k, i))] + [piece_spec(p) for p in range(npc)],
        out_specs=pl.BlockSpec((tm, tn), lambda i, j, k: (i, j)),
        out_shape=jax.ShapeDtypeStruct((M, width * npc), BF16), scratch_shapes=[pltpu.VMEM((tm, tn), F32)],
        compiler_params=_params(("parallel", "parallel", "arbitrary")))(a, *pieces)


def _mix_fwd(c_act, o_att, o_mem, proj, b_gate, x1, w_pw, w_o, w_mo, w_out):
    T, Dm = x1.shape
    W = c_act.shape[1]
    tm = _pick(T, (256, 128, 64, 32, 16, 8))

    def body(c_ref, oa_ref, om_ref, gl_ref, bg_ref, x1_ref, wpw_ref, wo_ref, wmo_ref, wout_ref,
             x2_ref, yc_ref, ya_ref, ym_ref, y_ref):
        yc = _dot(c_ref[...], wpw_ref[...])
        ya = _dot(oa_ref[...], wo_ref[...])
        ym = _dot(om_ref[...], wmo_ref[...])
        g = _sig(gl_ref[...].astype(F32) + bg_ref[...])
        y = g[:, :Dm] * yc + g[:, Dm:2 * Dm] * ya + g[:, 2 * Dm:] * ym
        x2_ref[...] = x1_ref[...] + _dot(y, wout_ref[...])
        yc_ref[...] = yc.astype(BF16)
        ya_ref[...] = ya.astype(BF16)
        ym_ref[...] = ym.astype(BF16)
        y_ref[...] = y.astype(BF16)

    rowW = pl.BlockSpec((tm, W), lambda i: (i, 0))
    rowD = pl.BlockSpec((tm, Dm), lambda i: (i, 0))
    full = lambda s: pl.BlockSpec(s, lambda i: (0, 0))
    return pl.pallas_call(
        body, name="mix_fwd", grid=(T // tm,),
        in_specs=[rowW, rowW, rowW, pl.BlockSpec((tm, 3 * Dm), lambda i: (i, 1)), full((1, 3 * Dm)), rowD,
                  full((W, Dm)), full((W, Dm)), full((W, Dm)), full((Dm, Dm))],
        out_specs=[rowD] * 5,
        out_shape=[jax.ShapeDtypeStruct((T, Dm), F32)] + [jax.ShapeDtypeStruct((T, Dm), BF16)] * 4,
        compiler_params=_params(("parallel",)))(c_act, o_att, o_mem, proj, b_gate, x1, w_pw, w_o, w_mo, w_out)


def _mix_bwd(dx2, yc, ya, ym, proj, b_gate, w_pw, w_o, w_mo, w_out):
    T, Dm = dx2.shape
    W = w_pw.shape[0]
    tm = _pick(T, (256, 128, 64, 32, 16, 8))

    def body(dx_ref, yc_ref, ya_ref, ym_ref, gl_ref, bg_ref, wpw_ref, wo_ref, wmo_ref, wout_ref,
             dyc_ref, dya_ref, dym_ref, dgl_ref, dbg_ref, dc_ref, doa_ref, dom_ref):
        i = pl.program_id(0)

        @pl.when(i == 0)
        def _():
            dbg_ref[...] = jnp.zeros_like(dbg_ref)

        dy = _dot(dx_ref[...], wout_ref[...], 'nt')
        g = _sig(gl_ref[...].astype(F32) + bg_ref[...])
        branches = ((yc_ref, dyc_ref, wpw_ref, dc_ref), (ya_ref, dya_ref, wo_ref, doa_ref),
                    (ym_ref, dym_ref, wmo_ref, dom_ref))
        for n, (y_ref, dyk_ref, w_ref, dk_ref) in enumerate(branches):
            gk = g[:, n * Dm:(n + 1) * Dm]
            dyk = dy * gk
            dgl = dyk * y_ref[...].astype(F32) * (1.0 - gk)
            dyk_ref[...] = dyk.astype(BF16)
            dgl_ref[:, n * Dm:(n + 1) * Dm] = dgl.astype(BF16)
            dbg_ref[:, n * Dm:(n + 1) * Dm] += jnp.sum(dgl, axis=0, keepdims=True)
            dk_ref[...] = _dot(dyk, w_ref[...], 'nt').astype(BF16)

    rowW = pl.BlockSpec((tm, W), lambda i: (i, 0))
    rowD = pl.BlockSpec((tm, Dm), lambda i: (i, 0))
    row3 = pl.BlockSpec((tm, 3 * Dm), lambda i: (i, 0))
    full = lambda s: pl.BlockSpec(s, lambda i: (0, 0))
    return pl.pallas_call(
        body, name="mix_bwd", grid=(T // tm,),
        in_specs=[rowD, rowD, rowD, rowD, pl.BlockSpec((tm, 3 * Dm), lambda i: (i, 1)), full((1, 3 * Dm)),
                  full((W, Dm)), full((W, Dm)), full((W, Dm)), full((Dm, Dm))],
        out_specs=[rowD, rowD, rowD, pl.BlockSpec((tm, 3 * Dm), lambda i: (i, 1)), full((1, 3 * Dm)),
                   rowW, rowW, rowW],
        out_shape=[jax.ShapeDtypeStruct((T, Dm), BF16)] * 3 + [jax.ShapeDtypeStruct((T, 6 * Dm), BF16),
                                                                jax.ShapeDtypeStruct((1, 3 * Dm), F32)]
        + [jax.ShapeDtypeStruct((T, W), BF16)] * 3,
        compiler_params=_params(("arbitrary",)))(dx2, yc, ya, ym, proj, b_gate, w_pw, w_o, w_mo, w_out)


def _ln_swish(cv, lg, lb):
    mu = jnp.mean(cv, axis=-1, keepdims=True)
    xc = cv - mu
    r = lax.rsqrt(jnp.mean(xc * xc, axis=-1, keepdims=True) + EPS)
    n = xc * r
    l = n * lg + lb
    return r, n, l


def _shift_copies(src, r0, win, shifts):
    win[...] = src[pl.ds(r0, CONV_TILE + CONV_PAD + 8), :]
    for s in range(8):
        shifts[s] = win[s:s + CONV_TILE + CONV_PAD, :]


def _tap(shifts, d):
    return shifts[d % 8, d - d % 8:d - d % 8 + CONV_TILE, :]


def _conv_fwd(proj3, dw_w, dw_b, ln_g, ln_b):
    Bl, S, _ = proj3.shape
    C, K, TS, PAD = CONV_WIDTH, CONV_KERNEL, CONV_TILE, CONV_PAD
    nt = S // TS

    def body(u_ref, w_ref, b_ref, lg_ref, lb_ref, cv_ref, c_ref, vbuf, win, shifts):
        vbuf[0:PAD, :] = jnp.zeros((PAD, C), F32)
        vbuf[S + PAD:S + PAD + 8, :] = jnp.zeros((8, C), F32)

        def glu(t, carry):
            r0 = pl.multiple_of(t * TS, TS)
            u = u_ref[pl.ds(r0, TS), :].astype(F32)
            vbuf[pl.ds(PAD + r0, TS), :] = u[:, :C] * _sig(u[:, C:])
            return carry

        lax.fori_loop(0, nt, glu, 0)

        def conv(t, carry):
            r0 = pl.multiple_of(t * TS, TS)
            _shift_copies(vbuf, r0, win, shifts)
            acc = jnp.zeros((TS, C), F32)
            for j in range(K):
                acc = acc + w_ref[j:j + 1, :] * _tap(shifts, PAD - (K - 1) + j)
            cv = acc + b_ref[...]
            cv_ref[pl.ds(r0, TS), :] = cv
            _, _, l = _ln_swish(cv, lg_ref[...], lb_ref[...])
            c_ref[pl.ds(r0, TS), :] = (l * _sig(l)).astype(BF16)
            return carry

        lax.fori_loop(0, nt, conv, 0)

    vec = pl.BlockSpec((1, C), lambda b: (0, 0))
    return pl.pallas_call(
        body, name="conv_fwd", grid=(Bl,),
        in_specs=[pl.BlockSpec((None, S, 2 * C), lambda b: (b, 0, 0)), pl.BlockSpec((K, C), lambda b: (0, 0)),
                  vec, vec, vec],
        out_specs=[pl.BlockSpec((None, S, C), lambda b: (b, 0, 0))] * 2,
        out_shape=[jax.ShapeDtypeStruct((Bl, S, C), F32), jax.ShapeDtypeStruct((Bl, S, C), BF16)],
        scratch_shapes=[pltpu.VMEM((S + PAD + 8, C), F32), pltpu.VMEM((TS + PAD + 8, C), F32),
                        pltpu.VMEM((8, TS + PAD, C), F32)],
        compiler_params=_params(("parallel",)))(proj3, dw_w, dw_b, ln_g, ln_b)


def _conv_bwd(proj3, cv, dc, dw_w, ln_g, ln_b):
    Bl, S, _ = proj3.shape
    C, K, TS, PAD = CONV_WIDTH, CONV_KERNEL, CONV_TILE, CONV_PAD
    nt = S // TS

    def body(u_ref, cv_ref, dc_ref, w_ref, lg_ref, lb_ref, du_ref, dw_ref, db_ref, dlg_ref, dlb_ref,
             vbuf, gbuf, win, shifts, dwacc):
        b = pl.program_id(0)

        @pl.when(b == 0)
        def _():
            dw_ref[...] = jnp.zeros_like(dw_ref)
            db_ref[...] = jnp.zeros_like(db_ref)
            dlg_ref[...] = jnp.zeros_like(dlg_ref)
            dlb_ref[...] = jnp.zeros_like(dlb_ref)

        vbuf[0:PAD, :] = jnp.zeros((PAD, C), F32)
        vbuf[S + PAD:S + PAD + 8, :] = jnp.zeros((8, C), F32)
        gbuf[S:S + PAD + 8, :] = jnp.zeros((PAD + 8, C), F32)
        dwacc[...] = jnp.zeros_like(dwacc)

        def norm_bwd(t, carry):
            r0 = pl.multiple_of(t * TS, TS)
            u = u_ref[pl.ds(r0, TS), :].astype(F32)
            vbuf[pl.ds(PAD + r0, TS), :] = u[:, :C] * _sig(u[:, C:])
            r, n, l = _ln_swish(cv_ref[pl.ds(r0, TS), :], lg_ref[...], lb_ref[...])
            s = _sig(l)
            dl = dc_ref[pl.ds(r0, TS), :].astype(F32) * s * (1.0 + l * (1.0 - s))
            dlg_ref[...] += jnp.sum(dl * n, axis=0, keepdims=True)
            dlb_ref[...] += jnp.sum(dl, axis=0, keepdims=True)
            dn = dl * lg_ref[...]
            dcv = r * (dn - jnp.mean(dn, axis=-1, keepdims=True) - n * jnp.mean(dn * n, axis=-1, keepdims=True))
            gbuf[pl.ds(r0, TS), :] = dcv
            db_ref[...] += jnp.sum(dcv, axis=0, keepdims=True)
            return carry

        lax.fori_loop(0, nt, norm_bwd, 0)

        def conv_bwd(t, carry):
            r0 = pl.multiple_of(t * TS, TS)
            _shift_copies(gbuf, r0, win, shifts)
            dv = jnp.zeros((TS, C), F32)
            for j in range(K):
                dv = dv + w_ref[j:j + 1, :] * _tap(shifts, K - 1 - j)
            u = u_ref[pl.ds(r0, TS), :].astype(F32)
            a, g = u[:, :C], u[:, C:]
            s = _sig(g)
            du_ref[pl.ds(r0, TS), 0:C] = (dv * s).astype(BF16)
            du_ref[pl.ds(r0, TS), C:2 * C] = (dv * a * s * (1.0 - s)).astype(BF16)
            dcv = gbuf[pl.ds(r0, TS), :]
            _shift_copies(vbuf, r0, win, shifts)
            for j in range(K):
                prod = dcv * _tap(shifts, PAD - (K - 1) + j)
                dwacc[j] += jnp.sum(prod.reshape(TS // 8, 8, C), axis=0)
            return carry

        lax.fori_loop(0, nt, conv_bwd, 0)
        dw_ref[...] += jnp.sum(dwacc[...], axis=1)

    vec = pl.BlockSpec((1, C), lambda b: (0, 0))
    seq = lambda w: pl.BlockSpec((None, S, w), lambda b: (b, 0, 0))
    return pl.pallas_call(
        body, name="conv_bwd", grid=(Bl,),
        in_specs=[seq(2 * C), seq(C), seq(C), pl.BlockSpec((K, C), lambda b: (0, 0)), vec, vec],
        out_specs=[seq(2 * C), pl.BlockSpec((K, C), lambda b: (0, 0)), vec, vec, vec],
        out_shape=[jax.ShapeDtypeStruct((Bl, S, 2 * C), BF16), jax.ShapeDtypeStruct((K, C), F32)]
        + [jax.ShapeDtypeStruct((1, C), F32)] * 3,
        scratch_shapes=[pltpu.VMEM((S + PAD + 8, C), F32), pltpu.VMEM((S + PAD + 8, C), F32),
                        pltpu.VMEM((TS + PAD + 8, C), F32), pltpu.VMEM((8, TS + PAD, C), F32),
                        pltpu.VMEM((K, 8, C), F32)],
        compiler_params=_params(("arbitrary",)))(proj3, cv, dc, dw_w, ln_g, ln_b)


def _att_bias(rel_bias):
    H = rel_bias.shape[0]
    Wd = KW + QB
    c = jnp.arange(Wd + 1)
    by_offset = rel_bias[:, jnp.clip(KW - c, -(CHUNK - 1), MAX_REL) + (CHUNK - 1)]
    flat = jnp.broadcast_to(by_offset[:, None, :], (H, QB, Wd + 1)).reshape(H, QB * (Wd + 1))
    skew = jnp.pad(flat, ((0, 0), (0, (QB + 1) * Wd - QB * (Wd + 1)))).reshape(H, QB + 1, Wd)[:, :QB, QB:]
    qi = jnp.arange(QB)[:, None]
    kj = jnp.arange(KW)[None, :]
    dchunk = ((KW - QB) + qi) // CHUNK - kj // CHUNK
    band = (dchunk >= 0) & (dchunk <= LEFT_CHUNKS)
    return jnp.where(band[None], skew, MASK_VALUE)


def _head_masks():
    lane = lax.broadcasted_iota(jnp.int32, (1, 128), 1)
    return (lane < 64, lane >= 64)


def _att_probs(qh, k2, bias, valid):
    s = _dot(qh, k2, 'nt') * ATT_SCALE + bias
    s = jnp.where(valid, s, MASK_VALUE)
    e = jnp.exp(s - jnp.max(s, axis=-1, keepdims=True))
    return e * (1.0 / jnp.sum(e, axis=-1, keepdims=True))


def _att_specs(S, q_col):
    nb = S // QB
    q_spec = pl.BlockSpec((None, QB, ATT_WIDTH), lambda b, i: (b, jnp.minimum(i, nb - 1), q_col))

    def kv_spec(col, kb):
        return pl.BlockSpec((None, QB, ATT_WIDTH),
                            lambda b, i: (b, jnp.clip(i - 2 + kb, 0, nb - 1), col))

    return q_spec, [kv_spec(3, kb) for kb in range(3)], [kv_spec(4, kb) for kb in range(3)]


def _att_fwd(proj3, bias):
    Bl, S, _ = proj3.shape
    nb = S // QB
    q_spec, k_specs, v_specs = _att_specs(S, 2)

    def body(q_ref, k0, k1, k2r, v0, v1, v2r, bias_ref, o_ref):
        i = pl.program_id(1)
        masks = _head_masks()
        valid = lax.broadcasted_iota(jnp.int32, (QB, KW), 1) >= (2 - i) * QB
        for pr in range(ATT_HEADS // 2):
            ls = slice(128 * pr, 128 * (pr + 1))
            q2 = q_ref[:, ls]
            k2 = jnp.concatenate([k0[:, ls], k1[:, ls], k2r[:, ls]], axis=0)
            v2 = jnp.concatenate([v0[:, ls], v1[:, ls], v2r[:, ls]], axis=0)
            o2 = jnp.zeros((QB, 128), F32)
            for hh in range(2):
                p = _att_probs(jnp.where(masks[hh], q2, 0), k2, bias_ref[2 * pr + hh], valid)
                o2 = o2 + _dot(p, jnp.where(masks[hh], v2, 0))
            o_ref[:, ls] = o2.astype(BF16)

    return pl.pallas_call(
        body, name="att_fwd", grid=(Bl, nb),
        in_specs=[q_spec] + k_specs + v_specs + [pl.BlockSpec((ATT_HEADS, QB, KW), lambda b, i: (0, 0, 0))],
        out_specs=pl.BlockSpec((None, QB, ATT_WIDTH), lambda b, i: (b, i, 0)),
        out_shape=jax.ShapeDtypeStruct((Bl, S, ATT_WIDTH), BF16),
        compiler_params=_params(("parallel", "arbitrary")))(*([proj3] * 7), bias)


def _att_bwd(proj3, do, bias):
    Bl, S, _ = proj3.shape
    nb = S // QB
    q_spec, k_specs, v_specs = _att_specs(S, 2)
    do_spec = pl.BlockSpec((None, QB, ATT_WIDTH), lambda b, i: (b, jnp.minimum(i, nb - 1), 0))
    kv_out = pl.BlockSpec((None, QB, ATT_WIDTH), lambda b, i: (b, jnp.clip(i - 2, 0, nb - 1), 0))
    bias_spec = pl.BlockSpec((ATT_HEADS, QB, KW), lambda b, i: (0, 0, 0))

    def body(q_ref, k0, k1, k2r, v0, v1, v2r, do_ref, bias_ref, dq_ref, dk_ref, dv_ref, db_ref, dkw, dvw):
        b = pl.program_id(0)
        i = pl.program_id(1)

        @pl.when((b == 0) & (i == 0))
        def _():
            db_ref[...] = jnp.zeros_like(db_ref)

        @pl.when(i == 0)
        def _():
            dkw[...] = jnp.zeros_like(dkw)
            dvw[...] = jnp.zeros_like(dvw)

        @pl.when(i < nb)
        def _():
            masks = _head_masks()
            valid = lax.broadcasted_iota(jnp.int32, (QB, KW), 1) >= (2 - i) * QB
            for pr in range(ATT_HEADS // 2):
                ls = slice(128 * pr, 128 * (pr + 1))
                q2 = q_ref[:, ls]
                do2 = do_ref[:, ls]
                k2 = jnp.concatenate([k0[:, ls], k1[:, ls], k2r[:, ls]], axis=0)
                v2 = jnp.concatenate([v0[:, ls], v1[:, ls], v2r[:, ls]], axis=0)
                dq2 = jnp.zeros((QB, 128), F32)
                dk2 = jnp.zeros((KW, 128), F32)
                dv2 = jnp.zeros((KW, 128), F32)
                for hh in range(2):
                    h = 2 * pr + hh
                    qh = jnp.where(masks[hh], q2, 0)
                    doh = jnp.where(masks[hh], do2, 0)
                    p = _att_probs(qh, k2, bias_ref[h], valid)
                    dp = _dot(doh, v2, 'nt')
                    ds = p * (dp - jnp.sum(p * dp, axis=-1, keepdims=True))
                    db_ref[h] += ds
                    dq2 = dq2 + _dot(ds, jnp.where(masks[hh], k2, 0))
                    dk2 = dk2 + _dot(ds, qh, 'tn')
                    dv2 = dv2 + _dot(p, doh, 'tn')
                dq_ref[:, ls] = (dq2 * ATT_SCALE).astype(BF16)
                dkw[:, ls] += dk2 * ATT_SCALE
                dvw[:, ls] += dv2

        dk_ref[...] = dkw[0:QB, :].astype(BF16)
        dv_ref[...] = dvw[0:QB, :].astype(BF16)
        for buf in (dkw, dvw):
            rest = buf[QB:KW, :]
            buf[0:KW - QB, :] = rest
            buf[KW - QB:KW, :] = jnp.zeros((QB, ATT_WIDTH), F32)

    blk = jax.ShapeDtypeStruct((Bl, S, ATT_WIDTH), BF16)
    return pl.pallas_call(
        body, name="att_bwd", grid=(Bl, nb + 2),
        in_specs=[q_spec] + k_specs + v_specs + [do_spec, bias_spec],
        out_specs=[do_spec, kv_out, kv_out, bias_spec],
        out_shape=[blk, blk, blk, jax.ShapeDtypeStruct((ATT_HEADS, QB, KW), F32)],
        scratch_shapes=[pltpu.VMEM((KW, ATT_WIDTH), F32), pltpu.VMEM((KW, ATT_WIDTH), F32)],
        compiler_params=_params(("arbitrary", "arbitrary")))(*([proj3] * 7), do, bias)


def _rel_bias_grad(dbias):
    H = dbias.shape[0]
    Wd = KW + QB
    padded = jnp.pad(dbias, ((0, 0), (0, 1), (QB, 0)))
    skew = padded.reshape(H, (QB + 1) * Wd)[:, :QB * (Wd + 1)].reshape(H, QB, Wd + 1)[:, :, :Wd]
    c = jnp.arange(Wd)[:, None]
    bins = (jnp.clip(KW - c, -(CHUNK - 1), MAX_REL) + (CHUNK - 1) == jnp.arange(N_REL)[None, :]).astype(F32)

    def body(s_ref, bins_ref, o_ref):
        col = jnp.sum(s_ref[...], axis=1)
        o_ref[...] = jnp.dot(col, bins_ref[...], preferred_element_type=F32, precision=lax.Precision.HIGHEST)

    return pl.pallas_call(
        body, name="rel_bias_grad", grid=(1,),
        in_specs=[pl.BlockSpec((H, QB, Wd), lambda i: (0, 0, 0)), pl.BlockSpec((Wd, N_REL), lambda i: (0, 0))],
        out_specs=pl.BlockSpec((H, N_REL), lambda i: (0, 0)), out_shape=jax.ShapeDtypeStruct((H, N_REL), F32),
        compiler_params=_params(("arbitrary",)))(skew, bins)


MEM_TILE = 512


def _mem_probs(qh, kh):
    s = _dot(qh, kh, 'nt') * MEM_SCALE
    e = jnp.exp(s - jnp.max(s, axis=-1, keepdims=True))
    return e * (1.0 / jnp.sum(e, axis=-1, keepdims=True))


def _mem_fwd(proj3, kv3):
    Bl, S, _ = proj3.shape
    tq = _pick(S, (MEM_TILE, 256))
    hd = MEM_HEAD_DIM

    def body(q_ref, kv_ref, o_ref):
        for h in range(MEM_HEADS):
            p = _mem_probs(q_ref[:, h * hd:(h + 1) * hd], kv_ref[:, h * hd:(h + 1) * hd])
            o_ref[:, h * hd:(h + 1) * hd] = _dot(p, kv_ref[:, MEM_WIDTH + h * hd:MEM_WIDTH + (h + 1) * hd]).astype(BF16)

    return pl.pallas_call(
        body, name="mem_fwd", grid=(Bl, S // tq),
        in_specs=[pl.BlockSpec((None, tq, MEM_WIDTH), lambda b, i: (b, i, 5)),
                  pl.BlockSpec((None, MEM_LEN, 2 * MEM_WIDTH), lambda b, i: (b, 0, 0))],
        out_specs=pl.BlockSpec((None, tq, MEM_WIDTH), lambda b, i: (b, i, 0)),
        out_shape=jax.ShapeDtypeStruct((Bl, S, MEM_WIDTH), BF16),
        compiler_params=_params(("parallel", "parallel")))(proj3, kv3)


def _mem_bwd(proj3, kv3, do):
    Bl, S, _ = proj3.shape
    tq = _pick(S, (MEM_TILE, 256))
    hd = MEM_HEAD_DIM

    def body(q_ref, kv_ref, do_ref, dq_ref, dkv_ref):
        i = pl.program_id(1)

        @pl.when(i == 0)
        def _():
            dkv_ref[...] = jnp.zeros_like(dkv_ref)

        for h in range(MEM_HEADS):
            ks = slice(h * hd, (h + 1) * hd)
            vs = slice(MEM_WIDTH + h * hd, MEM_WIDTH + (h + 1) * hd)
            qh, kh, vh, doh = q_ref[:, ks], kv_ref[:, ks], kv_ref[:, vs], do_ref[:, ks]
            p = _mem_probs(qh, kh)
            dp = _dot(doh, vh, 'nt')
            ds = p * (dp - jnp.sum(p * dp, axis=-1, keepdims=True))
            dq_ref[:, ks] = (_dot(ds, kh) * MEM_SCALE).astype(BF16)
            dkv_ref[:, ks] += _dot(ds, qh, 'tn') * MEM_SCALE
            dkv_ref[:, vs] += _dot(p, doh, 'tn')

    return pl.pallas_call(
        body, name="mem_bwd", grid=(Bl, S // tq),
        in_specs=[pl.BlockSpec((None, tq, MEM_WIDTH), lambda b, i: (b, i, 5)),
                  pl.BlockSpec((None, MEM_LEN, 2 * MEM_WIDTH), lambda b, i: (b, 0, 0)),
                  pl.BlockSpec((None, tq, MEM_WIDTH), lambda b, i: (b, i, 0))],
        out_specs=[pl.BlockSpec((None, tq, MEM_WIDTH), lambda b, i: (b, i, 0)),
                   pl.BlockSpec((None, MEM_LEN, 2 * MEM_WIDTH), lambda b, i: (b, 0, 0))],
        out_shape=[jax.ShapeDtypeStruct((Bl, S, MEM_WIDTH), BF16),
                   jax.ShapeDtypeStruct((Bl, MEM_LEN, 2 * MEM_WIDTH), F32)],
        compiler_params=_params(("parallel", "arbitrary")))(proj3, kv3, do)


def _position():
    x, y, c = lax.axis_index("x"), lax.axis_index("y"), lax.axis_index("c")
    return x, y, c, 4 * x + 2 * y + c


def _device(idx):
    return ((idx >> 2) & 1, (idx >> 1) & 1, idx & 1)


def _half_block(ref, axis, shard_shape, k, h):
    R, Cn = shard_shape
    if axis == 1:
        return ref.at[pl.ds(h * (R // 2), R // 2), pl.ds(k * Cn, Cn)]
    return ref.at[pl.ds(k * R + h * (R // 2), R // 2), :]


def _block(ref, axis, shard_shape, k):
    R, Cn = shard_shape
    if axis == 1:
        return ref.at[:, pl.ds(k * Cn, Cn)]
    return ref.at[pl.ds(k * R, R), :]


def _half(ref, h):
    R = ref.shape[0]
    return ref.at[pl.ds(h * (R // 2), R // 2), :]


ANY = pl.BlockSpec(memory_space=pl.ANY)


HBM = pl.BlockSpec(memory_space=pltpu.HBM)
SEM = pl.BlockSpec(memory_space=pltpu.SEMAPHORE)
VMEM_WHOLE = pl.BlockSpec(memory_space=pltpu.VMEM)
EFFECT = pltpu.SideEffectType.DATAFLOW_SIDE_EFFECTING


def _in_hbm(a):
    return pltpu.with_memory_space_constraint(a, pltpu.HBM)


def _split_start(body, name, sources, lands, n_copies):
    n = len(sources)
    out_shape, out_specs = [], []
    for _ in range(n):
        out_shape += [pltpu.SemaphoreType.DMA((n_copies,)), pltpu.SemaphoreType.DMA((n_copies,))]
        out_specs += [SEM, SEM]
    out_shape += [pltpu.HBM(a.shape, a.dtype) for a in list(sources) + list(lands)]
    out_specs += [HBM] * (2 * n)
    out_shape.append(jax.ShapeDtypeStruct((8, 128), F32))
    out_specs.append(VMEM_WHOLE)

    def call_body(*refs):
        srcs, lnds = refs[:n], refs[n:2 * n]
        sems = refs[2 * n:4 * n]
        token = refs[-1]
        body(srcs, lnds, sems[0::2], sems[1::2])
        token[...] = jnp.zeros_like(token)

    res = pl.pallas_call(
        call_body, name=name, in_specs=[HBM] * (2 * n), out_specs=out_specs, out_shape=out_shape,
        input_output_aliases={i: 2 * n + i for i in range(2 * n)},
        compiler_params=pltpu.CompilerParams(has_side_effects=EFFECT))(
            *[_in_hbm(a) for a in list(sources) + list(lands)])
    pairs = [(res[2 * w], res[2 * w + 1], res[2 * n + w], res[3 * n + w]) for w in range(n)]
    return pairs, res[-1]


def _split_wait(body, name, pairs, after):
    n = len(pairs)

    def call_body(*refs):
        srcs, lnds = refs[:n], refs[n:2 * n]
        sems = refs[2 * n:4 * n]
        body(srcs, lnds, sems[0::2], sems[1::2])

    args = [_in_hbm(p[2]) for p in pairs] + [_in_hbm(p[3]) for p in pairs]
    for p in pairs:
        args += [p[0], p[1]]
    res = pl.pallas_call(
        call_body, name=name, in_specs=[HBM] * (2 * n) + [SEM] * (2 * n) + [ANY], out_specs=[HBM] * (2 * n),
        out_shape=[pltpu.HBM(a.shape, a.dtype) for a in args[:2 * n]],
        input_output_aliases={i: i for i in range(2 * n)},
        compiler_params=pltpu.CompilerParams(has_side_effects=EFFECT))(*args, after)
    return res[:n], res[n:]


def _place_block(shard, axis, chip_idx, name, after=None):
    R, Cn = shard.shape
    tr = _pick(R, (256, 176, 128, 64, 32, 16, 8))
    nblk = R // tr

    def body(k_ref, s_ref, *rest):
        rest[-1][...] = s_ref[...]

    if axis == 1:
        out_shape, out_index = (R, 4 * Cn), lambda i, k: (i, k[0])
    else:
        out_shape, out_index = (4 * R, Cn), lambda i, k: (k[0] * nblk + i, 0)
    extra = () if after is None else (after,)
    return pl.pallas_call(
        body, name=name,
        grid_spec=pltpu.PrefetchScalarGridSpec(
            num_scalar_prefetch=1, grid=(nblk,),
            in_specs=[pl.BlockSpec((tr, Cn), lambda i, k: (i, 0))] + [ANY] * len(extra),
            out_specs=pl.BlockSpec((tr, Cn), out_index)),
        out_shape=jax.ShapeDtypeStruct(out_shape, shard.dtype),
        compiler_params=_params(("parallel",)))(chip_idx, shard, *extra)


def _gather_copy(srcs, lnds, send, recv, axes, shapes, w, j, me):
    chip = me >> 1
    return (pltpu.make_async_remote_copy(
        src_ref=srcs[w], dst_ref=_block(lnds[w], axes[w], shapes[w], chip), send_sem=send[w].at[j],
        recv_sem=recv[w].at[j], device_id=_device(me ^ (2 * (j + 1))), device_id_type=MESH),
            pltpu.make_async_remote_copy(
        src_ref=srcs[w], dst_ref=_block(lnds[w], axes[w], shapes[w], chip ^ (j + 1)), send_sem=send[w].at[j],
        recv_sem=recv[w].at[j], device_id=_device(me ^ (2 * (j + 1))), device_id_type=MESH))


def _gather_start(shards, lands, axes, name):
    shapes = [s.shape for s in shards]

    def body(srcs, lnds, send, recv):
        x, y, c, me = _position()
        for w in range(len(shards)):
            for j in range(3):
                _gather_copy(srcs, lnds, send, recv, axes, shapes, w, j, me)[0].start()

    return _split_start(body, name, shards, lands, 3)


def _gather_wait(pairs, axes, after, name):
    shapes = [p[2].shape for p in pairs]

    def body(srcs, lnds, send, recv):
        x, y, c, me = _position()
        for w in range(len(pairs)):
            for j in range(3):
                sent, landed = _gather_copy(srcs, lnds, send, recv, axes, shapes, w, j, me)
                sent.wait_send()
                landed.wait_recv()

    return _split_wait(body, name, pairs, after)[1]


def _shard_shape(grad, axis):
    return (grad.shape[0], grad.shape[1] // 4) if axis == 1 else (grad.shape[0] // 4, grad.shape[1])


def _scatter_copy(srcs, lnds, send, recv, axes, shapes, w, m, me):
    peer = me ^ m
    return pltpu.make_async_remote_copy(
        src_ref=_half_block(srcs[w], axes[w], shapes[w], peer >> 1, peer & 1), dst_ref=lnds[w].at[m - 1],
        send_sem=send[w].at[m - 1], recv_sem=recv[w].at[m - 1], device_id=_device(peer), device_id_type=MESH)


def _scatter_start(grads, axes, name):
    shapes = [_shard_shape(g, a) for g, a in zip(grads, axes)]
    lands = [lax.empty((N_DEV - 1, R // 2, Cn), BF16) for R, Cn in shapes]

    def body(srcs, lnds, send, recv):
        x, y, c, me = _position()
        for w in range(len(grads)):
            for m in range(1, N_DEV):
                _scatter_copy(srcs, lnds, send, recv, axes, shapes, w, m, me).start()

    return _split_start(body, name, grads, lands, N_DEV - 1)


def _scatter_wait(pairs, axes, after):
    shapes = [_shard_shape(p[2], a) for p, a in zip(pairs, axes)]

    def body(srcs, lnds, send, recv):
        x, y, c, me = _position()
        for w in range(len(pairs)):
            for m in range(1, N_DEV):
                cp = _scatter_copy(srcs, lnds, send, recv, axes, shapes, w, m, me)
                cp.wait_send()
                cp.wait_recv()

    return _split_wait(body, "scatter_wait", pairs, after)


def _sum_partials(own, parts, half, name):
    R, Cn = own.shape
    tr = _pick(R, (256, 176, 128, 64, 32, 16, 8))
    nblk = R // tr

    def body(half_ref, own_ref, p_ref, o_ref):
        acc = own_ref[...].astype(F32)
        for d in range(N_DEV - 1):
            acc = acc + p_ref[d].astype(F32)
        o_ref[...] = acc

    return pl.pallas_call(
        body, name=name,
        grid_spec=pltpu.PrefetchScalarGridSpec(
            num_scalar_prefetch=1, grid=(nblk,),
            in_specs=[pl.BlockSpec((tr, Cn), lambda i, hr: (i, 0)),
                      pl.BlockSpec((N_DEV - 1, tr, Cn), lambda i, hr: (0, i, 0))],
            out_specs=pl.BlockSpec((tr, Cn), lambda i, hr: (hr[0] * nblk + i, 0))),
        out_shape=jax.ShapeDtypeStruct((2 * R, Cn), F32),
        compiler_params=_params(("parallel",)))(half, own, parts)


def _exchange_halves(grads):
    n = len(grads)

    def body(*refs):
        outs = refs[n:2 * n]
        send, recv = refs[2 * n:]
        x, y, c, me = _position()

        def copy(w, half):
            rows = _half(outs[w], half)
            return pltpu.make_async_remote_copy(src_ref=rows, dst_ref=rows, send_sem=send.at[w],
                                                recv_sem=recv.at[w], device_id=_device(me ^ 1), device_id_type=MESH)

        for w in range(n):
            copy(w, c).start()
        for w in range(n):
            copy(w, 1 - c).wait_recv()
        for w in range(n):
            copy(w, c).wait_send()

    return pl.pallas_call(
        body, name="exchange_halves", in_specs=[ANY] * n, out_specs=[ANY] * n,
        out_shape=[jax.ShapeDtypeStruct(a.shape, a.dtype) for a in grads],
        input_output_aliases={i: i for i in range(n)},
        scratch_shapes=[pltpu.SemaphoreType.DMA((n,)), pltpu.SemaphoreType.DMA((n,))],
        compiler_params=pltpu.CompilerParams(has_side_effects=True))(*grads)


def _all_reduce_small(vec):
    R, L = vec.shape

    def body(v_ref, o_ref, buf, send, recv):
        x, y, c, me = _position()
        buf[me] = v_ref[...]

        def copy(m, slot):
            return pltpu.make_async_remote_copy(src_ref=v_ref, dst_ref=buf.at[slot], send_sem=send.at[m - 1],
                                                recv_sem=recv.at[m - 1], device_id=_device(me ^ m),
                                                device_id_type=MESH)

        for m in range(1, N_DEV):
            copy(m, me).start()
        for m in range(1, N_DEV):
            copy(m, me ^ m).wait_recv()
        for m in range(1, N_DEV):
            copy(m, me).wait_send()
        acc = buf[0]
        for d in range(1, N_DEV):
            acc = acc + buf[d]
        o_ref[...] = acc

    vm = pl.BlockSpec(memory_space=pltpu.VMEM)
    return pl.pallas_call(
        body, name="all_reduce_small", in_specs=[vm], out_specs=vm, out_shape=jax.ShapeDtypeStruct((R, L), F32),
        scratch_shapes=[pltpu.VMEM((N_DEV, R, L), F32), pltpu.SemaphoreType.DMA((N_DEV - 1,)),
                        pltpu.SemaphoreType.DMA((N_DEV - 1,))],
        compiler_params=pltpu.CompilerParams(has_side_effects=True))(vec)


def _adamw(w, g, m, v, name):
    R, Cn = w.shape
    tr = _pick(R, (256, 176, 128, 64, 40, 32, 16, 8))

    def body(w_ref, g_ref, m_ref, v_ref, d_ref, nm_ref, nv_ref):
        gv = g_ref[...]
        nm = ADAM_B1 * m_ref[...] + (1.0 - ADAM_B1) * gv
        nv = ADAM_B2 * v_ref[...] + (1.0 - ADAM_B2) * (gv * gv)
        m_hat = nm / (1.0 - ADAM_B1 ** ADAM_STEP)
        v_hat = nv / (1.0 - ADAM_B2 ** ADAM_STEP)
        d_ref[...] = -ADAM_LR * (m_hat / (jnp.sqrt(v_hat) + ADAM_EPS) + ADAM_WD * w_ref[...])
        nm_ref[...] = nm
        nv_ref[...] = nv

    spec = pl.BlockSpec((tr, Cn), lambda i: (i, 0))
    return pl.pallas_call(
        body, name=name, grid=(R // tr,), in_specs=[spec] * 4, out_specs=[spec] * 3,
        out_shape=[jax.ShapeDtypeStruct((R, Cn), F32)] * 3, compiler_params=_params(("parallel",)))(w, g, m, v)


def _pack(arrays, rows):
    flat = jnp.concatenate([a.reshape(-1).astype(F32) for a in arrays])
    return jnp.pad(flat, (0, rows * 128 - flat.shape[0])).reshape(rows, 128)


def _unpack(packed, shapes):
    flat = packed.reshape(-1)
    out, off = [], 0
    for s in shapes:
        size = 1
        for d in s:
            size *= d
        out.append(flat[off:off + size].reshape(s))
        off += size
    return out


def _ffn_fwd(x, norm, arrived, tag, after=None):
    h = _rms_fwd(x, norm, f"{tag}_norm", after=after)
    w_up = arrived(f"{tag}_w_up", h)
    a, b, act = _ffn_up(h, w_up, f"{tag}_up")
    w_down = arrived(f"{tag}_w_down", act)
    out = _mm(act, w_down, 'nn', f"{tag}_down", F32, res=x, scale=0.5)
    return out, (h, a, b, act, w_up, w_down)


def _ffn_bwd(dout, x, norm, saved, tag, send):
    h, a, b, act, w_up, w_down = saved
    g_down = _mm(act, dout, 'tn', f"{tag}_down_dw", BF16, scale=0.5)
    token = send([f"{tag}_w_down"], [g_down])
    da, db = _ffn_down_bwd(dout, w_down, a, b, f"{tag}_down_dx", after=token)
    g_up = _dw_pieces(h, [da, db], f"{tag}_up_dw")
    token = send([f"{tag}_w_up"], [g_up])
    return _dx_rms_bwd([da, db], w_up, x, norm, dout, f"{tag}_up_dx", after=token)


def kernel(x, mem, ffn1_norm, ffn1_w_up, ffn1_w_down, mix_norm, mem_norm, w_in, b_gate, conv_dw_w, conv_dw_b, conv_ln_g, conv_ln_b, conv_w_pw, att_rel_bias, att_w_o, mem_w_kv, mem_w_o, w_out, ffn2_norm, ffn2_w_up, ffn2_w_down, final_norm, loss_target, m_ffn1_norm, m_ffn1_w_up, m_ffn1_w_down, m_mix_norm, m_mem_norm, m_w_in, m_b_gate, m_conv_dw_w, m_conv_dw_b, m_conv_ln_g, m_conv_ln_b, m_conv_w_pw, m_att_rel_bias, m_att_w_o, m_mem_w_kv, m_mem_w_o, m_w_out, m_ffn2_norm, m_ffn2_w_up, m_ffn2_w_down, m_final_norm, v_ffn1_norm, v_ffn1_w_up, v_ffn1_w_down, v_mix_norm, v_mem_norm, v_w_in, v_b_gate, v_conv_dw_w, v_conv_dw_b, v_conv_ln_g, v_conv_ln_b, v_conv_w_pw, v_att_rel_bias, v_att_w_o, v_mem_w_kv, v_mem_w_o, v_w_out, v_ffn2_norm, v_ffn2_w_up, v_ffn2_w_down, v_final_norm):
    given = dict(locals())
    wts = {n: given[n] for n in WEIGHTS}
    mom1 = {n: given["m_" + n] for n in WEIGHTS}
    mom2 = {n: given["v_" + n] for n in WEIGHTS}
    Bl, S, Dm = x.shape
    T = Bl * S
    x0 = x.reshape(T, Dm)
    tgt = loss_target.reshape(T, Dm)
    mem2 = mem.reshape(Bl * MEM_LEN, Dm)

    big_names = [n for n, _ in BIG]
    big_axes = [a for _, a in BIG]
    chip = 2 * lax.axis_index("x") + lax.axis_index("y")

    core = lax.axis_index("c")
    axis_of = dict(BIG)

    gather_groups = [['ffn1_w_up'], ['ffn1_w_down'], ['w_in', 'conv_dw_w'],
                     ['mem_w_kv', 'conv_w_pw', 'att_w_o', 'mem_w_o', 'w_out'], ['ffn2_w_up'], ['ffn2_w_down']]
    gather_names = [n for grp in gather_groups for n in grp]
    gather_axes = [axis_of.get(n, 1) for n in gather_names]
    shards = [jnp.pad(conv_dw_w[0], ((0, 1), (0, 0))) if n == 'conv_dw_w' else wts[n][0].astype(BF16)
              for n in gather_names]
    chip_idx = chip.reshape(1).astype(jnp.int32)
    first, first_token = _gather_start(
        shards[:1], [_place_block(shards[0], gather_axes[0], chip_idx, f"place_{gather_names[0]}")],
        gather_axes[:1], "gather_start_first")
    lands = [_place_block(sh, a, chip_idx, f"place_{n}", after=first_token)
             for sh, a, n in zip(shards[1:], gather_axes[1:], gather_names[1:])]
    rest, gather_token = _gather_start(shards[1:], lands, gather_axes[1:], "gather_start_rest")
    in_flight = dict(zip(gather_names, first + rest))
    full = {}

    def arrived(name, after):
        if name not in full:
            grp = next(grp for grp in gather_groups if name in grp)
            lands = _gather_wait([in_flight[n] for n in grp], [axis_of.get(n, 1) for n in grp], after,
                                 f"gather_wait_{grp[0]}")
            full.update(zip(grp, lands))
        return full[name]

    scattering = {}

    def send(names, grads):
        pairs, token = _scatter_start(grads, [axis_of[n] for n in names], f"scatter_start_{names[0]}")
        scattering.update(zip(names, pairs))
        return token

    final_g = final_norm.reshape(1, Dm)
    bias = _att_bias(att_rel_bias[0] + first_token[:1, :1])

    x1, ffn1_saved = _ffn_fwd(x0, ffn1_norm, arrived, "ffn1", after=gather_token)
    h = _rms_fwd(x1, mix_norm, "mix_norm")
    w_in_full = arrived('w_in', h)
    dw_full = full['conv_dw_w'][:CONV_KERNEL]
    proj = _mm(h, w_in_full, 'nn', "w_in", BF16)
    proj3 = proj.reshape(Bl, S, proj.shape[1])
    cv, c_act = _conv_fwd(proj3, dw_full, conv_dw_b, conv_ln_g, conv_ln_b)
    o_att = _att_fwd(proj3, bias)
    mem_h = _rms_fwd(mem2, mem_norm, "mem_norm")
    kv = _mm(mem_h, arrived('mem_w_kv', o_att), 'nn', "mem_kv", BF16)
    kv3 = kv.reshape(Bl, MEM_LEN, 2 * MEM_WIDTH)
    o_mem = _mem_fwd(proj3, kv3)
    c_act2, o_att2, o_mem2 = c_act.reshape(T, -1), o_att.reshape(T, -1), o_mem.reshape(T, -1)
    x2, yc, ya, ym, y = _mix_fwd(c_act2, o_att2, o_mem2, proj, b_gate, x1, full['conv_w_pw'], full['att_w_o'],
                                 full['mem_w_o'], full['w_out'])
    x3, ffn2_saved = _ffn_fwd(x2, ffn2_norm, arrived, "ffn2")
    dx3, g_final, loss_vec = _final_fwd_bwd(x3, tgt, final_g)

    g = {}
    dx2, g['ffn2_norm'] = _ffn_bwd(dx3, x2, ffn2_norm, ffn2_saved, "ffn2", send)
    dyc, dya, dym, dgl, g['b_gate'], dc, doa, dom = _mix_bwd(
        dx2, yc, ya, ym, proj, b_gate, full['conv_w_pw'], full['att_w_o'], full['mem_w_o'], full['w_out'])
    token = send(['w_out', 'conv_w_pw', 'att_w_o', 'mem_w_o'],
                 [_mm(y, dx2, 'tn', "w_out_dw", BF16), _mm(c_act2, dyc, 'tn', "conv_pw_dw", BF16),
                  _mm(o_att2, dya, 'tn', "att_o_dw", BF16), _mm(o_mem2, dym, 'tn', "mem_o_dw", BF16)])
    du, g_dw, g['conv_dw_b'], g['conv_ln_g'], g['conv_ln_b'] = _conv_bwd(
        proj3, cv, dc.reshape(Bl, S, -1), dw_full, conv_ln_g, conv_ln_b)
    dq, dk, dv, dbias = _att_bwd(proj3, doa.reshape(Bl, S, -1), bias)
    g['att_rel_bias'] = _rel_bias_grad(dbias)
    dmq, dkv = _mem_bwd(proj3, kv3, dom.reshape(Bl, S, -1))
    dkv2 = dkv.reshape(Bl * MEM_LEN, 2 * MEM_WIDTH)
    g_kv = _mm(mem_h, dkv2, 'tn', "mem_kv_dw", BF16, after=token)
    dmem_h = _mm(dkv2, full['mem_w_kv'], 'nt', "mem_kv_dx", F32)
    _, g['mem_norm'] = _rms_bwd(mem2, mem_norm, dmem_h, dmem_h, "mem_norm_bwd")
    left = jnp.concatenate([du.reshape(T, -1), dq.reshape(T, -1), dk.reshape(T, -1), dv.reshape(T, -1),
                            dmq.reshape(T, -1)], axis=1)
    dproj = lax.dynamic_update_slice(dgl, left, (0, 0))
    token = send(['mem_w_kv', 'w_in'], [g_kv, _mm(h, dproj, 'tn', "w_in_dw", BF16)])
    dx1, g['mix_norm'] = _dx_rms_bwd([dproj], w_in_full, x1, mix_norm, dx2, "w_in_dx", after=token)
    dx0, g['ffn1_norm'] = _ffn_bwd(dx1, x0, ffn1_norm, ffn1_saved, "ffn1", send)
    g['final_norm'] = g_final

    sent, landed = _scatter_wait([scattering[n] for n in big_names], big_axes, dx0)
    halves = []
    half_idx = core.reshape(1).astype(jnp.int32)
    for n, a, own_full, parts in zip(big_names, big_axes, sent, landed):
        R, Cn = _shard_shape(own_full, a)
        start = (core * (R // 2), chip * Cn) if a == 1 else (chip * R + core * (R // 2), 0)
        own = lax.dynamic_slice(own_full, start, (R // 2, Cn))
        halves.append(_sum_partials(own, parts, half_idx, f"sum_{n}"))
    for n, sg in zip(big_names, _exchange_halves(halves)):
        g[n] = sg

    small_shapes = [wts[n].shape for n in SMALL]
    n_small = sum(int(wts[n].size) for n in SMALL)
    n_red = n_small + CONV_KERNEL * CONV_WIDTH
    red = _all_reduce_small(_pack([g[n] for n in SMALL] + [g_dw], -(-n_red // 1024) * 8))
    red_list = _unpack(red, small_shapes + [(CONV_KERNEL, CONV_WIDTH)])
    for n, rg in zip(SMALL, red_list[:-1]):
        g[n] = rg
    dw_cols = conv_dw_w.shape[2]
    g['conv_dw_w'] = lax.dynamic_slice(red_list[-1], (0, chip * dw_cols), (CONV_KERNEL, dw_cols))[None]

    delta, new_m, new_v = {}, {}, {}
    for n in big_names:
        g[n] = g[n][None]
        d, nm, nv = _adamw(wts[n][0], g[n][0], mom1[n][0], mom2[n][0], f"adamw_{n}")
        delta[n], new_m[n], new_v[n] = d[None], nm[None], nv[None]
    rest = SMALL + ['conv_dw_w']
    rest_shapes = [wts[n].shape for n in rest]
    rows = -(-sum(int(wts[n].size) for n in rest) // 1024) * 8
    packed = [_pack([src[n] for n in rest], rows) for src in (wts, g, mom1, mom2)]
    for out, res in zip((delta, new_m, new_v), _adamw(*packed, "adamw_small")):
        for n, a in zip(rest, _unpack(res, rest_shapes)):
            out[n] = a

    loss = lax.psum(loss_vec[0, 0], ("x", "y", "c"))
    grad_x = dx0.reshape(Bl, S, Dm)
    return (loss, grad_x, *[g[n] for n in WEIGHTS], *[delta[n] for n in WEIGHTS],
            *[new_m[n] for n in WEIGHTS], *[new_v[n] for n in WEIGHTS])
```

```python
import jax
import jax.numpy as jnp
from jax import lax
from jax.experimental import pallas as pl
from jax.experimental.pallas import tpu as pltpu

F32 = jnp.float32
BF16 = jnp.bfloat16

D_MODEL = 1024
D_FF = 2816
CHUNK = 64
LEFT_CHUNKS = 8
MAX_REL = 128
N_REL = (CHUNK - 1) + MAX_REL + 1
CONV_WIDTH = 512
CONV_KERNEL = 31
ATT_HEADS = 8
ATT_WIDTH = 512
MEM_LEN = 256
MEM_HEADS = 4
MEM_HEAD_DIM = 128
MEM_WIDTH = 512
EPS = 1e-6
MASK_VALUE = -1e30
ATT_SCALE = 64 ** -0.5
MEM_SCALE = 128 ** -0.5

ADAM_LR = 0.001
ADAM_B1 = 0.9
ADAM_B2 = 0.999
ADAM_EPS = 1e-08
ADAM_WD = 0.01
ADAM_STEP = 10

QB = 256
KW = 3 * QB
CONV_PAD = 32
CONV_TILE = 256

VMEM_LIMIT = 56 << 20
MXU_COLS = 256

WEIGHTS = ['ffn1_norm', 'ffn1_w_up', 'ffn1_w_down', 'mix_norm', 'mem_norm', 'w_in', 'b_gate', 'conv_dw_w',
           'conv_dw_b', 'conv_ln_g', 'conv_ln_b', 'conv_w_pw', 'att_rel_bias', 'att_w_o', 'mem_w_kv', 'mem_w_o',
           'w_out', 'ffn2_norm', 'ffn2_w_up', 'ffn2_w_down', 'final_norm']
BIG = [('ffn1_w_up', 1), ('ffn1_w_down', 0), ('w_in', 1), ('conv_w_pw', 1), ('att_w_o', 1), ('mem_w_kv', 0),
       ('mem_w_o', 1), ('w_out', 0), ('ffn2_w_up', 1), ('ffn2_w_down', 0)]
SMALL = ['ffn1_norm', 'mix_norm', 'mem_norm', 'b_gate', 'conv_dw_b', 'conv_ln_g', 'conv_ln_b', 'att_rel_bias',
         'ffn2_norm', 'final_norm']
N_CHIPS = 4
N_DEV = 8
MESH = pl.DeviceIdType.MESH


def _pick(n, cands):
    for c in cands:
        if n % c == 0:
            return c
    return n


def _sig(x):
    return 0.5 * jnp.tanh(0.5 * x) + 0.5


def _params(sem=None, vmem=VMEM_LIMIT):
    return pltpu.CompilerParams(dimension_semantics=sem, vmem_limit_bytes=vmem)


def _dot(a, b, mode='nn'):
    dims = {'nn': (((1,), (0,)), ((), ())), 'nt': (((1,), (1,)), ((), ())), 'tn': (((0,), (0,)), ((), ()))}[mode]
    return lax.dot_general(a.astype(BF16), b.astype(BF16), dims, preferred_element_type=F32)


def _mm(a, b, mode, name, out_dtype, res=None, scale=1.0, after=None):
    if mode == 'nn':
        (M, C), (_, N) = a.shape, b.shape
    elif mode == 'nt':
        (M, C), (N, _) = a.shape, b.shape
    else:
        (C, M), (_, N) = a.shape, b.shape
    tm = _pick(M, (1024, 1408, 512, 256, 128))
    tn = _pick(N, (1024, 1408, 512, 256, 128))
    tc = _pick(C, (1024, 1408, 512, 256, 128))
    nk = C // tc
    if mode == 'nn':
        a_spec = pl.BlockSpec((tm, tc), lambda i, j, k: (i, k))
        b_spec = pl.BlockSpec((tc, tn), lambda i, j, k: (k, j))
    elif mode == 'nt':
        a_spec = pl.BlockSpec((tm, tc), lambda i, j, k: (i, k))
        b_spec = pl.BlockSpec((tn, tc), lambda i, j, k: (j, k))
    else:
        a_spec = pl.BlockSpec((tc, tm), lambda i, j, k: (k, i))
        b_spec = pl.BlockSpec((tc, tn), lambda i, j, k: (k, j))
    o_spec = pl.BlockSpec((tm, tn), lambda i, j, k: (i, j))
    has_res = res is not None
    has_after = after is not None

    def body(*refs):
        a_ref, b_ref = refs[:2]
        r_ref = refs[2] if has_res else None
        o_ref, acc_ref = refs[-2:]
        k = pl.program_id(2)

        def finish(acc):
            if scale != 1.0:
                acc = acc * scale
            if r_ref is not None:
                acc = r_ref[...] + acc
            o_ref[...] = acc.astype(o_ref.dtype)

        if nk == 1:
            finish(_dot(a_ref[...], b_ref[...], mode))
        else:
            @pl.when(k == 0)
            def _():
                acc_ref[...] = jnp.zeros_like(acc_ref)

            acc_ref[...] += _dot(a_ref[...], b_ref[...], mode)

            @pl.when(k == nk - 1)
            def _():
                finish(acc_ref[...])

    in_specs = [a_spec, b_spec] + ([o_spec] if has_res else []) + ([ANY] if has_after else [])
    args = (a, b) + ((res,) if has_res else ()) + ((after,) if has_after else ())
    acc_shape = (tm, tn) if nk > 1 else (8, 128)
    return pl.pallas_call(
        body, name=name, grid=(M // tm, N // tn, nk), in_specs=in_specs, out_specs=o_spec,
        out_shape=jax.ShapeDtypeStruct((M, N), out_dtype), scratch_shapes=[pltpu.VMEM(acc_shape, F32)],
        compiler_params=_params(("parallel", "parallel", "arbitrary")))(*args)


def _row_tile(T):
    return _pick(T, (512, 256, 128, 64, 32, 16, 8))


def _rms_fwd(x, g, name, after=None):
    T, Dm = x.shape
    tm = _row_tile(T)

    def body(x_ref, g_ref, *rest):
        o_ref = rest[-1]
        xv = x_ref[...]
        r = lax.rsqrt(jnp.mean(xv * xv, axis=-1, keepdims=True) + EPS)
        o_ref[...] = ((xv * r) * g_ref[...]).astype(o_ref.dtype)

    extra = () if after is None else (after,)
    return pl.pallas_call(
        body, name=name, grid=(T // tm,),
        in_specs=[pl.BlockSpec((tm, Dm), lambda i: (i, 0)), pl.BlockSpec((1, Dm), lambda i: (0, 0))]
        + [ANY] * len(extra),
        out_specs=pl.BlockSpec((tm, Dm), lambda i: (i, 0)), out_shape=jax.ShapeDtypeStruct((T, Dm), BF16),
        compiler_params=_params(("parallel",)))(x, g, *extra)


def _rms_bwd(x, g, dh, dres, name):
    T, Dm = x.shape
    tm = _row_tile(T)

    def body(x_ref, g_ref, dh_ref, dr_ref, dx_ref, dg_ref):
        i = pl.program_id(0)
        xv = x_ref[...]
        r = lax.rsqrt(jnp.mean(xv * xv, axis=-1, keepdims=True) + EPS)
        xr = xv * r
        dh_v = dh_ref[...].astype(F32)
        dyg = dh_v * g_ref[...]
        dx = r * (dyg - xr * jnp.mean(dyg * xr, axis=-1, keepdims=True))
        dx_ref[...] = dr_ref[...] + dx

        @pl.when(i == 0)
        def _():
            dg_ref[...] = jnp.zeros_like(dg_ref)

        dg_ref[...] += jnp.sum(dh_v * xr, axis=0, keepdims=True)

    row = pl.BlockSpec((tm, Dm), lambda i: (i, 0))
    vec = pl.BlockSpec((1, Dm), lambda i: (0, 0))
    return pl.pallas_call(
        body, name=name, grid=(T // tm,), in_specs=[row, vec, row, row], out_specs=[row, vec],
        out_shape=[jax.ShapeDtypeStruct((T, Dm), F32), jax.ShapeDtypeStruct((1, Dm), F32)],
        compiler_params=_params(("arbitrary",)))(x, g, dh, dres)


def _final_fwd_bwd(x3, tgt, g):
    T, Dm = x3.shape
    tm = _row_tile(T)

    def body(x_ref, t_ref, g_ref, dx_ref, dg_ref, loss_ref):
        i = pl.program_id(0)
        xv = x_ref[...]
        gg = g_ref[...]
        r = lax.rsqrt(jnp.mean(xv * xv, axis=-1, keepdims=True) + EPS)
        xr = xv * r
        err = xr * gg - t_ref[...]
        dout = err * (1.0 / Dm)
        dyg = dout * gg
        dx_ref[...] = r * (dyg - xr * jnp.mean(dyg * xr, axis=-1, keepdims=True))

        @pl.when(i == 0)
        def _():
            dg_ref[...] = jnp.zeros_like(dg_ref)
            loss_ref[...] = jnp.zeros_like(loss_ref)

        dg_ref[...] += jnp.sum(dout * xr, axis=0, keepdims=True)
        loss_ref[...] += jnp.zeros_like(loss_ref) + (0.5 / Dm) * jnp.sum(err * err)

    row = pl.BlockSpec((tm, Dm), lambda i: (i, 0))
    vec = pl.BlockSpec((1, Dm), lambda i: (0, 0))
    one = pl.BlockSpec((1, 128), lambda i: (0, 0))
    return pl.pallas_call(
        body, name="final_fwd_bwd", grid=(T // tm,), in_specs=[row, row, vec], out_specs=[row, vec, one],
        out_shape=[jax.ShapeDtypeStruct((T, Dm), F32), jax.ShapeDtypeStruct((1, Dm), F32),
                   jax.ShapeDtypeStruct((1, 128), F32)],
        compiler_params=_params(("arbitrary",)))(x3, tgt, g)


def _ffn_up(h, w_up, name):
    T, K = h.shape
    Fh = w_up.shape[1] // 2
    tm = _pick(T, (1024, 512, 256, 128))
    tn = _pick(Fh, (1408, 512, 256, 128))
    nj = Fh // tn

    def body(h_ref, wa_ref, wb_ref, a_ref, b_ref, act_ref):
        hv = h_ref[...]
        for c0 in range(0, tn, MXU_COLS):
            cs = slice(c0, min(c0 + MXU_COLS, tn))
            a = _dot(hv, wa_ref[:, cs])
            b = _dot(hv, wb_ref[:, cs])
            a_ref[:, cs] = a.astype(BF16)
            b_ref[:, cs] = b.astype(BF16)
            act_ref[:, cs] = (a * _sig(a) * b).astype(BF16)

    out = pl.BlockSpec((tm, tn), lambda i, j: (i, j))
    return pl.pallas_call(
        body, name=name, grid=(T // tm, nj),
        in_specs=[pl.BlockSpec((tm, K), lambda i, j: (i, 0)), pl.BlockSpec((K, tn), lambda i, j: (0, j)),
                  pl.BlockSpec((K, tn), lambda i, j: (0, j + nj))],
        out_specs=[out, out, out], out_shape=[jax.ShapeDtypeStruct((T, Fh), BF16)] * 3,
        compiler_params=_params(("parallel", "parallel")))(h, w_up, w_up)


def _ffn_down_bwd(dout, w_down, a, b, name, after=None):
    T, Dm = dout.shape
    Fh = w_down.shape[0]
    tm = _pick(T, (1024, 512, 256, 128))
    tn = _pick(Fh, (1408, 512, 256, 128))

    def body(d_ref, w_ref, a_ref, b_ref, *rest):
        da_ref, db_ref = rest[-2:]
        dv = d_ref[...].astype(BF16)
        for c0 in range(0, tn, MXU_COLS):
            cs = slice(c0, min(c0 + MXU_COLS, tn))
            dact = _dot(dv, w_ref[cs, :], 'nt') * 0.5
            av = a_ref[:, cs].astype(F32)
            bv = b_ref[:, cs].astype(F32)
            s = _sig(av)
            da_ref[:, cs] = (dact * bv * s * (1.0 + av * (1.0 - s))).astype(BF16)
            db_ref[:, cs] = (dact * av * s).astype(BF16)

    tile = pl.BlockSpec((tm, tn), lambda i, j: (i, j))
    extra = () if after is None else (after,)
    return pl.pallas_call(
        body, name=name, grid=(T // tm, Fh // tn),
        in_specs=[pl.BlockSpec((tm, Dm), lambda i, j: (i, 0)), pl.BlockSpec((tn, Dm), lambda i, j: (j, 0)),
                  tile, tile] + [ANY] * len(extra),
        out_specs=[tile, tile], out_shape=[jax.ShapeDtypeStruct((T, Fh), BF16)] * 2,
        compiler_params=_params(("parallel", "parallel")))(dout, w_down, a, b, *extra)


def _dx_rms_bwd(pieces, w, x, g, dres, name, after=None):
    T, Dm = x.shape
    width = pieces[0].shape[1]
    tm = _pick(T, (1024, 512, 256, 128))
    tc = _pick(width, (1408, 1024, 512, 256, 128))
    per = width // tc
    nk = per * len(pieces)
    npc = len(pieces)
    rows = _pick(tm, (256, 128))

    def body(*refs):
        p_refs = refs[:npc]
        w_ref, x_hbm, g_ref, dr_hbm = refs[npc:npc + 4]
        dx_ref, dg_ref, acc_ref, x_buf, dr_buf, sems = refs[-6:]
        i = pl.program_id(0)
        k = pl.program_id(1)
        tile = pl.ds(pl.multiple_of(i * tm, tm), tm)
        fetch_x = pltpu.make_async_copy(x_hbm.at[tile, :], x_buf, sems.at[0])
        fetch_dr = pltpu.make_async_copy(dr_hbm.at[tile, :], dr_buf, sems.at[1])

        @pl.when(k == 0)
        def _():
            fetch_x.start()
            fetch_dr.start()
            acc_ref[...] = jnp.zeros_like(acc_ref)

        @pl.when((i == 0) & (k == 0))
        def _():
            dg_ref[...] = jnp.zeros_like(dg_ref)

        for p in range(npc):
            @pl.when((k >= p * per) & (k < (p + 1) * per))
            def _(p=p):
                acc_ref[...] += _dot(p_refs[p][...], w_ref[...], 'nt')

        @pl.when(k == nk - 1)
        def _():
            fetch_x.wait()
            fetch_dr.wait()

            def chunk(c, carry):
                rs = pl.ds(pl.multiple_of(c * rows, rows), rows)
                dh = acc_ref[rs, :]
                xv = x_buf[rs, :]
                r = lax.rsqrt(jnp.mean(xv * xv, axis=-1, keepdims=True) + EPS)
                xr = xv * r
                dyg = dh * g_ref[...]
                dx_ref[rs, :] = dr_buf[rs, :] + r * (dyg - xr * jnp.mean(dyg * xr, axis=-1, keepdims=True))
                dg_ref[...] += jnp.sum(dh * xr, axis=0, keepdims=True)
                return carry

            lax.fori_loop(0, tm // rows, chunk, 0)

    def piece_spec(p):
        return pl.BlockSpec((tm, tc), lambda i, k: (i, jnp.clip(k - p * per, 0, per - 1)))

    row = pl.BlockSpec((tm, Dm), lambda i, k: (i, 0))
    vec = pl.BlockSpec((1, Dm), lambda i, k: (0, 0))
    extra = () if after is None else (after,)
    return pl.pallas_call(
        body, name=name, grid=(T // tm, nk),
        in_specs=[piece_spec(p) for p in range(npc)] + [pl.BlockSpec((Dm, tc), lambda i, k: (0, k)), ANY, vec, ANY]
        + [ANY] * len(extra),
        out_specs=[row, vec], out_shape=[jax.ShapeDtypeStruct((T, Dm), F32), jax.ShapeDtypeStruct((1, Dm), F32)],
        scratch_shapes=[pltpu.VMEM((tm, Dm), F32), pltpu.VMEM((tm, Dm), F32), pltpu.VMEM((tm, Dm), F32),
                        pltpu.SemaphoreType.DMA((2,))],
        compiler_params=_params(("arbitrary", "arbitrary")))(*pieces, w, x, g, dres, *extra)


def _dw_pieces(a, pieces, name):
    C, M = a.shape
    width = pieces[0].shape[1]
    npc = len(pieces)
    tm = _pick(M, (1024, 512, 256, 128))
    tn = _pick(width, (1408, 1024, 512, 256, 128))
    tc = _pick(C, (1024, 512, 256, 128))
    per = width // tn
    nk = C // tc

    def body(*refs):
        a_ref = refs[0]
        p_refs = refs[1:1 + npc]
        o_ref, acc_ref = refs[-2:]
        j = pl.program_id(1)
        k = pl.program_id(2)

        @pl.when(k == 0)
        def _():
            acc_ref[...] = jnp.zeros_like(acc_ref)

        for p in range(npc):
            @pl.when((j >= p * per) & (j < (p + 1) * per))
            def _(p=p):
                acc_ref[...] += _dot(a_ref[...], p_refs[p][...], 'tn')

        @pl.when(k == nk - 1)
        def _():
            o_ref[...] = acc_ref[...].astype(o_ref.dtype)

    def piece_spec(p):
        return pl.BlockSpec((tc, tn), lambda i, j, k: (k, jnp.clip(j - p * per, 0, per - 1)))

    return pl.pallas_call(
        body, name=name, grid=(M // tm, per * npc, nk),
        in_specs=[pl.BlockSpec((tc, tm), lambda i, j, k: (k, i))] + [piece_spec(p) for p in range(npc)],
        out_specs=pl.BlockSpec((tm, tn), lambda i, j, k: (i, j)),
        out_shape=jax.ShapeDtypeStruct((M, width * npc), BF16), scratch_shapes=[pltpu.VMEM((tm, tn), F32)],
        compiler_params=_params(("parallel", "parallel", "arbitrary")))(a, *pieces)


def _mix_fwd(c_act, o_att, o_mem, proj, b_gate, x1, w_pw, w_o, w_mo, w_out):
    T, Dm = x1.shape
    W = c_act.shape[1]
    tm = _pick(T, (256, 128, 64, 32, 16, 8))

    def body(c_ref, oa_ref, om_ref, gl_ref, bg_ref, x1_ref, wpw_ref, wo_ref, wmo_ref, wout_ref,
             x2_ref, yc_ref, ya_ref, ym_ref, y_ref):
        yc = _dot(c_ref[...], wpw_ref[...])
        ya = _dot(oa_ref[...], wo_ref[...])
        ym = _dot(om_ref[...], wmo_ref[...])
        g = _sig(gl_ref[...].astype(F32) + bg_ref[...])
        y = g[:, :Dm] * yc + g[:, Dm:2 * Dm] * ya + g[:, 2 * Dm:] * ym
        x2_ref[...] = x1_ref[...] + _dot(y, wout_ref[...])
        yc_ref[...] = yc.astype(BF16)
        ya_ref[...] = ya.astype(BF16)
        ym_ref[...] = ym.astype(BF16)
        y_ref[...] = y.astype(BF16)

    rowW = pl.BlockSpec((tm, W), lambda i: (i, 0))
    rowD = pl.BlockSpec((tm, Dm), lambda i: (i, 0))
    full = lambda s: pl.BlockSpec(s, lambda i: (0, 0))
    return pl.pallas_call(
        body, name="mix_fwd", grid=(T // tm,),
        in_specs=[rowW, rowW, rowW, pl.BlockSpec((tm, 3 * Dm), lambda i: (i, 1)), full((1, 3 * Dm)), rowD,
                  full((W, Dm)), full((W, Dm)), full((W, Dm)), full((Dm, Dm))],
        out_specs=[rowD] * 5,
        out_shape=[jax.ShapeDtypeStruct((T, Dm), F32)] + [jax.ShapeDtypeStruct((T, Dm), BF16)] * 4,
        compiler_params=_params(("parallel",)))(c_act, o_att, o_mem, proj, b_gate, x1, w_pw, w_o, w_mo, w_out)


def _mix_bwd(dx2, yc, ya, ym, proj, b_gate, w_pw, w_o, w_mo, w_out):
    T, Dm = dx2.shape
    W = w_pw.shape[0]
    tm = _pick(T, (256, 128, 64, 32, 16, 8))

    def body(dx_ref, yc_ref, ya_ref, ym_ref, gl_ref, bg_ref, wpw_ref, wo_ref, wmo_ref, wout_ref,
             dyc_ref, dya_ref, dym_ref, dgl_ref, dbg_ref, dc_ref, doa_ref, dom_ref):
        i = pl.program_id(0)

        @pl.when(i == 0)
        def _():
            dbg_ref[...] = jnp.zeros_like(dbg_ref)

        dy = _dot(dx_ref[...], wout_ref[...], 'nt')
        g = _sig(gl_ref[...].astype(F32) + bg_ref[...])
        branches = ((yc_ref, dyc_ref, wpw_ref, dc_ref), (ya_ref, dya_ref, wo_ref, doa_ref),
                    (ym_ref, dym_ref, wmo_ref, dom_ref))
        for n, (y_ref, dyk_ref, w_ref, dk_ref) in enumerate(branches):
            gk = g[:, n * Dm:(n + 1) * Dm]
            dyk = dy * gk
            dgl = dyk * y_ref[...].astype(F32) * (1.0 - gk)
            dyk_ref[...] = dyk.astype(BF16)
            dgl_ref[:, n * Dm:(n + 1) * Dm] = dgl.astype(BF16)
            dbg_ref[:, n * Dm:(n + 1) * Dm] += jnp.sum(dgl, axis=0, keepdims=True)
            dk_ref[...] = _dot(dyk, w_ref[...], 'nt').astype(BF16)

    rowW = pl.BlockSpec((tm, W), lambda i: (i, 0))
    rowD = pl.BlockSpec((tm, Dm), lambda i: (i, 0))
    row3 = pl.BlockSpec((tm, 3 * Dm), lambda i: (i, 0))
    full = lambda s: pl.BlockSpec(s, lambda i: (0, 0))
    return pl.pallas_call(
        body, name="mix_bwd", grid=(T // tm,),
        in_specs=[rowD, rowD, rowD, rowD, pl.BlockSpec((tm, 3 * Dm), lambda i: (i, 1)), full((1, 3 * Dm)),
                  full((W, Dm)), full((W, Dm)), full((W, Dm)), full((Dm, Dm))],
        out_specs=[rowD, rowD, rowD, pl.BlockSpec((tm, 3 * Dm), lambda i: (i, 1)), full((1, 3 * Dm)),
                   rowW, rowW, rowW],
        out_shape=[jax.ShapeDtypeStruct((T, Dm), BF16)] * 3 + [jax.ShapeDtypeStruct((T, 6 * Dm), BF16),
                                                                jax.ShapeDtypeStruct((1, 3 * Dm), F32)]
        + [jax.ShapeDtypeStruct((T, W), BF16)] * 3,
        compiler_params=_params(("arbitrary",)))(dx2, yc, ya, ym, proj, b_gate, w_pw, w_o, w_mo, w_out)


def _ln_swish(cv, lg, lb):
    mu = jnp.mean(cv, axis=-1, keepdims=True)
    xc = cv - mu
    r = lax.rsqrt(jnp.mean(xc * xc, axis=-1, keepdims=True) + EPS)
    n = xc * r
    l = n * lg + lb
    return r, n, l


def _shift_copies(src, r0, win, shifts):
    win[...] = src[pl.ds(r0, CONV_TILE + CONV_PAD + 8), :]
    for s in range(8):
        shifts[s] = win[s:s + CONV_TILE + CONV_PAD, :]


def _tap(shifts, d):
    return shifts[d % 8, d - d % 8:d - d % 8 + CONV_TILE, :]


def _conv_fwd(proj3, dw_w, dw_b, ln_g, ln_b):
    Bl, S, _ = proj3.shape
    C, K, TS, PAD = CONV_WIDTH, CONV_KERNEL, CONV_TILE, CONV_PAD
    nt = S // TS

    def body(u_ref, w_ref, b_ref, lg_ref, lb_ref, cv_ref, c_ref, vbuf, win, shifts):
        vbuf[0:PAD, :] = jnp.zeros((PAD, C), F32)
        vbuf[S + PAD:S + PAD + 8, :] = jnp.zeros((8, C), F32)

        def glu(t, carry):
            r0 = pl.multiple_of(t * TS, TS)
            u = u_ref[pl.ds(r0, TS), :].astype(F32)
            vbuf[pl.ds(PAD + r0, TS), :] = u[:, :C] * _sig(u[:, C:])
            return carry

        lax.fori_loop(0, nt, glu, 0)

        def conv(t, carry):
            r0 = pl.multiple_of(t * TS, TS)
            _shift_copies(vbuf, r0, win, shifts)
            acc = jnp.zeros((TS, C), F32)
            for j in range(K):
                acc = acc + w_ref[j:j + 1, :] * _tap(shifts, PAD - (K - 1) + j)
            cv = acc + b_ref[...]
            cv_ref[pl.ds(r0, TS), :] = cv
            _, _, l = _ln_swish(cv, lg_ref[...], lb_ref[...])
            c_ref[pl.ds(r0, TS), :] = (l * _sig(l)).astype(BF16)
            return carry

        lax.fori_loop(0, nt, conv, 0)

    vec = pl.BlockSpec((1, C), lambda b: (0, 0))
    return pl.pallas_call(
        body, name="conv_fwd", grid=(Bl,),
        in_specs=[pl.BlockSpec((None, S, 2 * C), lambda b: (b, 0, 0)), pl.BlockSpec((K, C), lambda b: (0, 0)),
                  vec, vec, vec],
        out_specs=[pl.BlockSpec((None, S, C), lambda b: (b, 0, 0))] * 2,
        out_shape=[jax.ShapeDtypeStruct((Bl, S, C), F32), jax.ShapeDtypeStruct((Bl, S, C), BF16)],
        scratch_shapes=[pltpu.VMEM((S + PAD + 8, C), F32), pltpu.VMEM((TS + PAD + 8, C), F32),
                        pltpu.VMEM((8, TS + PAD, C), F32)],
        compiler_params=_params(("parallel",)))(proj3, dw_w, dw_b, ln_g, ln_b)


def _conv_bwd(proj3, cv, dc, dw_w, ln_g, ln_b):
    Bl, S, _ = proj3.shape
    C, K, TS, PAD = CONV_WIDTH, CONV_KERNEL, CONV_TILE, CONV_PAD
    nt = S // TS

    def body(u_ref, cv_ref, dc_ref, w_ref, lg_ref, lb_ref, du_ref, dw_ref, db_ref, dlg_ref, dlb_ref,
             vbuf, gbuf, win, shifts, dwacc):
        b = pl.program_id(0)

        @pl.when(b == 0)
        def _():
            dw_ref[...] = jnp.zeros_like(dw_ref)
            db_ref[...] = jnp.zeros_like(db_ref)
            dlg_ref[...] = jnp.zeros_like(dlg_ref)
            dlb_ref[...] = jnp.zeros_like(dlb_ref)

        vbuf[0:PAD, :] = jnp.zeros((PAD, C), F32)
        vbuf[S + PAD:S + PAD + 8, :] = jnp.zeros((8, C), F32)
        gbuf[S:S + PAD + 8, :] = jnp.zeros((PAD + 8, C), F32)
        dwacc[...] = jnp.zeros_like(dwacc)

        def norm_bwd(t, carry):
            r0 = pl.multiple_of(t * TS, TS)
            u = u_ref[pl.ds(r0, TS), :].astype(F32)
            vbuf[pl.ds(PAD + r0, TS), :] = u[:, :C] * _sig(u[:, C:])
            r, n, l = _ln_swish(cv_ref[pl.ds(r0, TS), :], lg_ref[...], lb_ref[...])
            s = _sig(l)
            dl = dc_ref[pl.ds(r0, TS), :].astype(F32) * s * (1.0 + l * (1.0 - s))
            dlg_ref[...] += jnp.sum(dl * n, axis=0, keepdims=True)
            dlb_ref[...] += jnp.sum(dl, axis=0, keepdims=True)
            dn = dl * lg_ref[...]
            dcv = r * (dn - jnp.mean(dn, axis=-1, keepdims=True) - n * jnp.mean(dn * n, axis=-1, keepdims=True))
            gbuf[pl.ds(r0, TS), :] = dcv
            db_ref[...] += jnp.sum(dcv, axis=0, keepdims=True)
            return carry

        lax.fori_loop(0, nt, norm_bwd, 0)

        def conv_bwd(t, carry):
            r0 = pl.multiple_of(t * TS, TS)
            _shift_copies(gbuf, r0, win, shifts)
            dv = jnp.zeros((TS, C), F32)
            for j in range(K):
                dv = dv + w_ref[j:j + 1, :] * _tap(shifts, K - 1 - j)
            u = u_ref[pl.ds(r0, TS), :].astype(F32)
            a, g = u[:, :C], u[:, C:]
            s = _sig(g)
            du_ref[pl.ds(r0, TS), 0:C] = (dv * s).astype(BF16)
            du_ref[pl.ds(r0, TS), C:2 * C] = (dv * a * s * (1.0 - s)).astype(BF16)
            dcv = gbuf[pl.ds(r0, TS), :]
            _shift_copies(vbuf, r0, win, shifts)
            for j in range(K):
                prod = dcv * _tap(shifts, PAD - (K - 1) + j)
                dwacc[j] += jnp.sum(prod.reshape(TS // 8, 8, C), axis=0)
            return carry

        lax.fori_loop(0, nt, conv_bwd, 0)
        dw_ref[...] += jnp.sum(dwacc[...], axis=1)

    vec = pl.BlockSpec((1, C), lambda b: (0, 0))
    seq = lambda w: pl.BlockSpec((None, S, w), lambda b: (b, 0, 0))
    return pl.pallas_call(
        body, name="conv_bwd", grid=(Bl,),
        in_specs=[seq(2 * C), seq(C), seq(C), pl.BlockSpec((K, C), lambda b: (0, 0)), vec, vec],
        out_specs=[seq(2 * C), pl.BlockSpec((K, C), lambda b: (0, 0)), vec, vec, vec],
        out_shape=[jax.ShapeDtypeStruct((Bl, S, 2 * C), BF16), jax.ShapeDtypeStruct((K, C), F32)]
        + [jax.ShapeDtypeStruct((1, C), F32)] * 3,
        scratch_shapes=[pltpu.VMEM((S + PAD + 8, C), F32), pltpu.VMEM((S + PAD + 8, C), F32),
                        pltpu.VMEM((TS + PAD + 8, C), F32), pltpu.VMEM((8, TS + PAD, C), F32),
                        pltpu.VMEM((K, 8, C), F32)],
        compiler_params=_params(("arbitrary",)))(proj3, cv, dc, dw_w, ln_g, ln_b)


def _att_bias(rel_bias):
    H = rel_bias.shape[0]
    Wd = KW + QB
    c = jnp.arange(Wd + 1)
    by_offset = rel_bias[:, jnp.clip(KW - c, -(CHUNK - 1), MAX_REL) + (CHUNK - 1)]
    flat = jnp.broadcast_to(by_offset[:, None, :], (H, QB, Wd + 1)).reshape(H, QB * (Wd + 1))
    skew = jnp.pad(flat, ((0, 0), (0, (QB + 1) * Wd - QB * (Wd + 1)))).reshape(H, QB + 1, Wd)[:, :QB, QB:]
    qi = jnp.arange(QB)[:, None]
    kj = jnp.arange(KW)[None, :]
    dchunk = ((KW - QB) + qi) // CHUNK - kj // CHUNK
    band = (dchunk >= 0) & (dchunk <= LEFT_CHUNKS)
    return jnp.where(band[None], skew, MASK_VALUE)


def _head_masks():
    lane = lax.broadcasted_iota(jnp.int32, (1, 128), 1)
    return (lane < 64, lane >= 64)


def _att_probs(qh, k2, bias, valid):
    s = _dot(qh, k2, 'nt') * ATT_SCALE + bias
    s = jnp.where(valid, s, MASK_VALUE)
    e = jnp.exp(s - jnp.max(s, axis=-1, keepdims=True))
    return e * (1.0 / jnp.sum(e, axis=-1, keepdims=True))


def _att_specs(S, q_col):
    nb = S // QB
    q_spec = pl.BlockSpec((None, QB, ATT_WIDTH), lambda b, i: (b, jnp.minimum(i, nb - 1), q_col))

    def kv_spec(col, kb):
        return pl.BlockSpec((None, QB, ATT_WIDTH),
                            lambda b, i: (b, jnp.clip(i - 2 + kb, 0, nb - 1), col))

    return q_spec, [kv_spec(3, kb) for kb in range(3)], [kv_spec(4, kb) for kb in range(3)]


def _att_fwd(proj3, bias):
    Bl, S, _ = proj3.shape
    nb = S // QB
    q_spec, k_specs, v_specs = _att_specs(S, 2)

    def body(q_ref, k0, k1, k2r, v0, v1, v2r, bias_ref, o_ref):
        i = pl.program_id(1)
        masks = _head_masks()
        valid = lax.broadcasted_iota(jnp.int32, (QB, KW), 1) >= (2 - i) * QB
        for pr in range(ATT_HEADS // 2):
            ls = slice(128 * pr, 128 * (pr + 1))
            q2 = q_ref[:, ls]
            k2 = jnp.concatenate([k0[:, ls], k1[:, ls], k2r[:, ls]], axis=0)
            v2 = jnp.concatenate([v0[:, ls], v1[:, ls], v2r[:, ls]], axis=0)
            o2 = jnp.zeros((QB, 128), F32)
            for hh in range(2):
                p = _att_probs(jnp.where(masks[hh], q2, 0), k2, bias_ref[2 * pr + hh], valid)
                o2 = o2 + _dot(p, jnp.where(masks[hh], v2, 0))
            o_ref[:, ls] = o2.astype(BF16)

    return pl.pallas_call(
        body, name="att_fwd", grid=(Bl, nb),
        in_specs=[q_spec] + k_specs + v_specs + [pl.BlockSpec((ATT_HEADS, QB, KW), lambda b, i: (0, 0, 0))],
        out_specs=pl.BlockSpec((None, QB, ATT_WIDTH), lambda b, i: (b, i, 0)),
        out_shape=jax.ShapeDtypeStruct((Bl, S, ATT_WIDTH), BF16),
        compiler_params=_params(("parallel", "arbitrary")))(*([proj3] * 7), bias)


def _att_bwd(proj3, do, bias):
    Bl, S, _ = proj3.shape
    nb = S // QB
    q_spec, k_specs, v_specs = _att_specs(S, 2)
    do_spec = pl.BlockSpec((None, QB, ATT_WIDTH), lambda b, i: (b, jnp.minimum(i, nb - 1), 0))
    kv_out = pl.BlockSpec((None, QB, ATT_WIDTH), lambda b, i: (b, jnp.clip(i - 2, 0, nb - 1), 0))
    bias_spec = pl.BlockSpec((ATT_HEADS, QB, KW), lambda b, i: (0, 0, 0))

    def body(q_ref, k0, k1, k2r, v0, v1, v2r, do_ref, bias_ref, dq_ref, dk_ref, dv_ref, db_ref, dkw, dvw):
        b = pl.program_id(0)
        i = pl.program_id(1)

        @pl.when((b == 0) & (i == 0))
        def _():
            db_ref[...] = jnp.zeros_like(db_ref)

        @pl.when(i == 0)
        def _():
            dkw[...] = jnp.zeros_like(dkw)
            dvw[...] = jnp.zeros_like(dvw)

        @pl.when(i < nb)
        def _():
            masks = _head_masks()
            valid = lax.broadcasted_iota(jnp.int32, (QB, KW), 1) >= (2 - i) * QB
            for pr in range(ATT_HEADS // 2):
                ls = slice(128 * pr, 128 * (pr + 1))
                q2 = q_ref[:, ls]
                do2 = do_ref[:, ls]
                k2 = jnp.concatenate([k0[:, ls], k1[:, ls], k2r[:, ls]], axis=0)
                v2 = jnp.concatenate([v0[:, ls], v1[:, ls], v2r[:, ls]], axis=0)
                dq2 = jnp.zeros((QB, 128), F32)
                dk2 = jnp.zeros((KW, 128), F32)
                dv2 = jnp.zeros((KW, 128), F32)
                for hh in range(2):
                    h = 2 * pr + hh
                    qh = jnp.where(masks[hh], q2, 0)
                    doh = jnp.where(masks[hh], do2, 0)
                    p = _att_probs(qh, k2, bias_ref[h], valid)
                    dp = _dot(doh, v2, 'nt')
                    ds = p * (dp - jnp.sum(p * dp, axis=-1, keepdims=True))
                    db_ref[h] += ds
                    dq2 = dq2 + _dot(ds, jnp.where(masks[hh], k2, 0))
                    dk2 = dk2 + _dot(ds, qh, 'tn')
                    dv2 = dv2 + _dot(p, doh, 'tn')
                dq_ref[:, ls] = (dq2 * ATT_SCALE).astype(BF16)
                dkw[:, ls] += dk2 * ATT_SCALE
                dvw[:, ls] += dv2

        dk_ref[...] = dkw[0:QB, :].astype(BF16)
        dv_ref[...] = dvw[0:QB, :].astype(BF16)
        for buf in (dkw, dvw):
            rest = buf[QB:KW, :]
            buf[0:KW - QB, :] = rest
            buf[KW - QB:KW, :] = jnp.zeros((QB, ATT_WIDTH), F32)

    blk = jax.ShapeDtypeStruct((Bl, S, ATT_WIDTH), BF16)
    return pl.pallas_call(
        body, name="att_bwd", grid=(Bl, nb + 2),
        in_specs=[q_spec] + k_specs + v_specs + [do_spec, bias_spec],
        out_specs=[do_spec, kv_out, kv_out, bias_spec],
        out_shape=[blk, blk, blk, jax.ShapeDtypeStruct((ATT_HEADS, QB, KW), F32)],
        scratch_shapes=[pltpu.VMEM((KW, ATT_WIDTH), F32), pltpu.VMEM((KW, ATT_WIDTH), F32)],
        compiler_params=_params(("arbitrary", "arbitrary")))(*([proj3] * 7), do, bias)


def _rel_bias_grad(dbias):
    H = dbias.shape[0]
    Wd = KW + QB
    padded = jnp.pad(dbias, ((0, 0), (0, 1), (QB, 0)))
    skew = padded.reshape(H, (QB + 1) * Wd)[:, :QB * (Wd + 1)].reshape(H, QB, Wd + 1)[:, :, :Wd]
    c = jnp.arange(Wd)[:, None]
    bins = (jnp.clip(KW - c, -(CHUNK - 1), MAX_REL) + (CHUNK - 1) == jnp.arange(N_REL)[None, :]).astype(F32)

    def body(s_ref, bins_ref, o_ref):
        col = jnp.sum(s_ref[...], axis=1)
        o_ref[...] = jnp.dot(col, bins_ref[...], preferred_element_type=F32, precision=lax.Precision.HIGHEST)

    return pl.pallas_call(
        body, name="rel_bias_grad", grid=(1,),
        in_specs=[pl.BlockSpec((H, QB, Wd), lambda i: (0, 0, 0)), pl.BlockSpec((Wd, N_REL), lambda i: (0, 0))],
        out_specs=pl.BlockSpec((H, N_REL), lambda i: (0, 0)), out_shape=jax.ShapeDtypeStruct((H, N_REL), F32),
        compiler_params=_params(("arbitrary",)))(skew, bins)


MEM_TILE = 512


def _mem_probs(qh, kh):
    s = _dot(qh, kh, 'nt') * MEM_SCALE
    e = jnp.exp(s - jnp.max(s, axis=-1, keepdims=True))
    return e * (1.0 / jnp.sum(e, axis=-1, keepdims=True))


def _mem_fwd(proj3, kv3):
    Bl, S, _ = proj3.shape
    tq = _pick(S, (MEM_TILE, 256))
    hd = MEM_HEAD_DIM

    def body(q_ref, kv_ref, o_ref):
        for h in range(MEM_HEADS):
            p = _mem_probs(q_ref[:, h * hd:(h + 1) * hd], kv_ref[:, h * hd:(h + 1) * hd])
            o_ref[:, h * hd:(h + 1) * hd] = _dot(p, kv_ref[:, MEM_WIDTH + h * hd:MEM_WIDTH + (h + 1) * hd]).astype(BF16)

    return pl.pallas_call(
        body, name="mem_fwd", grid=(Bl, S // tq),
        in_specs=[pl.BlockSpec((None, tq, MEM_WIDTH), lambda b, i: (b, i, 5)),
                  pl.BlockSpec((None, MEM_LEN, 2 * MEM_WIDTH), lambda b, i: (b, 0, 0))],
        out_specs=pl.BlockSpec((None, tq, MEM_WIDTH), lambda b, i: (b, i, 0)),
        out_shape=jax.ShapeDtypeStruct((Bl, S, MEM_WIDTH), BF16),
        compiler_params=_params(("parallel", "parallel")))(proj3, kv3)


def _mem_bwd(proj3, kv3, do):
    Bl, S, _ = proj3.shape
    tq = _pick(S, (MEM_TILE, 256))
    hd = MEM_HEAD_DIM

    def body(q_ref, kv_ref, do_ref, dq_ref, dkv_ref):
        i = pl.program_id(1)

        @pl.when(i == 0)
        def _():
            dkv_ref[...] = jnp.zeros_like(dkv_ref)

        for h in range(MEM_HEADS):
            ks = slice(h * hd, (h + 1) * hd)
            vs = slice(MEM_WIDTH + h * hd, MEM_WIDTH + (h + 1) * hd)
            qh, kh, vh, doh = q_ref[:, ks], kv_ref[:, ks], kv_ref[:, vs], do_ref[:, ks]
            p = _mem_probs(qh, kh)
            dp = _dot(doh, vh, 'nt')
            ds = p * (dp - jnp.sum(p * dp, axis=-1, keepdims=True))
            dq_ref[:, ks] = (_dot(ds, kh) * MEM_SCALE).astype(BF16)
            dkv_ref[:, ks] += _dot(ds, qh, 'tn') * MEM_SCALE
            dkv_ref[:, vs] += _dot(p, doh, 'tn')

    return pl.pallas_call(
        body, name="mem_bwd", grid=(Bl, S // tq),
        in_specs=[pl.BlockSpec((None, tq, MEM_WIDTH), lambda b, i: (b, i, 5)),
                  pl.BlockSpec((None, MEM_LEN, 2 * MEM_WIDTH), lambda b, i: (b, 0, 0)),
                  pl.BlockSpec((None, tq, MEM_WIDTH), lambda b, i: (b, i, 0))],
        out_specs=[pl.BlockSpec((None, tq, MEM_WIDTH), lambda b, i: (b, i, 0)),
                   pl.BlockSpec((None, MEM_LEN, 2 * MEM_WIDTH), lambda b, i: (b, 0, 0))],
        out_shape=[jax.ShapeDtypeStruct((Bl, S, MEM_WIDTH), BF16),
                   jax.ShapeDtypeStruct((Bl, MEM_LEN, 2 * MEM_WIDTH), F32)],
        compiler_params=_params(("parallel", "arbitrary")))(proj3, kv3, do)


def _position():
    x, y, c = lax.axis_index("x"), lax.axis_index("y"), lax.axis_index("c")
    return x, y, c, 4 * x + 2 * y + c


def _device(idx):
    return ((idx >> 2) & 1, (idx >> 1) & 1, idx & 1)


def _half_block(ref, axis, shard_shape, k, h):
    R, Cn = shard_shape
    if axis == 1:
        return ref.at[pl.ds(h * (R // 2), R // 2), pl.ds(k * Cn, Cn)]
    return ref.at[pl.ds(k * R + h * (R // 2), R // 2), :]


def _block(ref, axis, shard_shape, k):
    R, Cn = shard_shape
    if axis == 1:
        return ref.at[:, pl.ds(k * Cn, Cn)]
    return ref.at[pl.ds(k * R, R), :]


def _half(ref, h):
    R = ref.shape[0]
    return ref.at[pl.ds(h * (R // 2), R // 2), :]


ANY = pl.BlockSpec(memory_space=pl.ANY)


HBM = pl.BlockSpec(memory_space=pltpu.HBM)
SEM = pl.BlockSpec(memory_space=pltpu.SEMAPHORE)
VMEM_WHOLE = pl.BlockSpec(memory_space=pltpu.VMEM)
EFFECT = pltpu.SideEffectType.DATAFLOW_SIDE_EFFECTING


def _in_hbm(a):
    return pltpu.with_memory_space_constraint(a, pltpu.HBM)


def _split_start(body, name, sources, lands, n_copies):
    n = len(sources)
    out_shape, out_specs = [], []
    for _ in range(n):
        out_shape += [pltpu.SemaphoreType.DMA((n_copies,)), pltpu.SemaphoreType.DMA((n_copies,))]
        out_specs += [SEM, SEM]
    out_shape += [pltpu.HBM(a.shape, a.dtype) for a in list(sources) + list(lands)]
    out_specs += [HBM] * (2 * n)
    out_shape.append(jax.ShapeDtypeStruct((8, 128), F32))
    out_specs.append(VMEM_WHOLE)

    def call_body(*refs):
        srcs, lnds = refs[:n], refs[n:2 * n]
        sems = refs[2 * n:4 * n]
        token = refs[-1]
        body(srcs, lnds, sems[0::2], sems[1::2])
        token[...] = jnp.zeros_like(token)

    res = pl.pallas_call(
        call_body, name=name, in_specs=[HBM] * (2 * n), out_specs=out_specs, out_shape=out_shape,
        input_output_aliases={i: 2 * n + i for i in range(2 * n)},
        compiler_params=pltpu.CompilerParams(has_side_effects=EFFECT))(
            *[_in_hbm(a) for a in list(sources) + list(lands)])
    pairs = [(res[2 * w], res[2 * w + 1], res[2 * n + w], res[3 * n + w]) for w in range(n)]
    return pairs, res[-1]


def _split_wait(body, name, pairs, after):
    n = len(pairs)

    def call_body(*refs):
        srcs, lnds = refs[:n], refs[n:2 * n]
        sems = refs[2 * n:4 * n]
        body(srcs, lnds, sems[0::2], sems[1::2])

    args = [_in_hbm(p[2]) for p in pairs] + [_in_hbm(p[3]) for p in pairs]
    for p in pairs:
        args += [p[0], p[1]]
    res = pl.pallas_call(
        call_body, name=name, in_specs=[HBM] * (2 * n) + [SEM] * (2 * n) + [ANY], out_specs=[HBM] * (2 * n),
        out_shape=[pltpu.HBM(a.shape, a.dtype) for a in args[:2 * n]],
        input_output_aliases={i: i for i in range(2 * n)},
        compiler_params=pltpu.CompilerParams(has_side_effects=EFFECT))(*args, after)
    return res[:n], res[n:]


def _place_block(shard, axis, chip_idx, name, after=None):
    R, Cn = shard.shape
    tr = _pick(R, (256, 176, 128, 64, 32, 16, 8))
    nblk = R // tr

    def body(k_ref, s_ref, *rest):
        rest[-1][...] = s_ref[...]

    if axis == 1:
        out_shape, out_index = (R, 4 * Cn), lambda i, k: (i, k[0])
    else:
        out_shape, out_index = (4 * R, Cn), lambda i, k: (k[0] * nblk + i, 0)
    extra = () if after is None else (after,)
    return pl.pallas_call(
        body, name=name,
        grid_spec=pltpu.PrefetchScalarGridSpec(
            num_scalar_prefetch=1, grid=(nblk,),
            in_specs=[pl.BlockSpec((tr, Cn), lambda i, k: (i, 0))] + [ANY] * len(extra),
            out_specs=pl.BlockSpec((tr, Cn), out_index)),
        out_shape=jax.ShapeDtypeStruct(out_shape, shard.dtype),
        compiler_params=_params(("parallel",)))(chip_idx, shard, *extra)


def _gather_copy(srcs, lnds, send, recv, axes, shapes, w, j, me):
    chip = me >> 1
    return (pltpu.make_async_remote_copy(
        src_ref=srcs[w], dst_ref=_block(lnds[w], axes[w], shapes[w], chip), send_sem=send[w].at[j],
        recv_sem=recv[w].at[j], device_id=_device(me ^ (2 * (j + 1))), device_id_type=MESH),
            pltpu.make_async_remote_copy(
        src_ref=srcs[w], dst_ref=_block(lnds[w], axes[w], shapes[w], chip ^ (j + 1)), send_sem=send[w].at[j],
        recv_sem=recv[w].at[j], device_id=_device(me ^ (2 * (j + 1))), device_id_type=MESH))


def _gather_start(shards, lands, axes, name):
    shapes = [s.shape for s in shards]

    def body(srcs, lnds, send, recv):
        x, y, c, me = _position()
        for w in range(len(shards)):
            for j in range(3):
                _gather_copy(srcs, lnds, send, recv, axes, shapes, w, j, me)[0].start()

    return _split_start(body, name, shards, lands, 3)


def _gather_wait(pairs, axes, after, name):
    shapes = [p[2].shape for p in pairs]

    def body(srcs, lnds, send, recv):
        x, y, c, me = _position()
        for w in range(len(pairs)):
            for j in range(3):
                sent, landed = _gather_copy(srcs, lnds, send, recv, axes, shapes, w, j, me)
                sent.wait_send()
                landed.wait_recv()

    return _split_wait(body, name, pairs, after)[1]


def _shard_shape(grad, axis):
    return (grad.shape[0], grad.shape[1] // 4) if axis == 1 else (grad.shape[0] // 4, grad.shape[1])


def _scatter_copy(srcs, lnds, send, recv, axes, shapes, w, m, me):
    peer = me ^ m
    return pltpu.make_async_remote_copy(
        src_ref=_half_block(srcs[w], axes[w], shapes[w], peer >> 1, peer & 1), dst_ref=lnds[w].at[m - 1],
        send_sem=send[w].at[m - 1], recv_sem=recv[w].at[m - 1], device_id=_device(peer), device_id_type=MESH)


def _scatter_start(grads, axes, name):
    shapes = [_shard_shape(g, a) for g, a in zip(grads, axes)]
    lands = [lax.empty((N_DEV - 1, R // 2, Cn), BF16) for R, Cn in shapes]

    def body(srcs, lnds, send, recv):
        x, y, c, me = _position()
        for w in range(len(grads)):
            for m in range(1, N_DEV):
                _scatter_copy(srcs, lnds, send, recv, axes, shapes, w, m, me).start()

    return _split_start(body, name, grads, lands, N_DEV - 1)


def _scatter_wait(pairs, axes, after):
    shapes = [_shard_shape(p[2], a) for p, a in zip(pairs, axes)]

    def body(srcs, lnds, send, recv):
        x, y, c, me = _position()
        for w in range(len(pairs)):
            for m in range(1, N_DEV):
                cp = _scatter_copy(srcs, lnds, send, recv, axes, shapes, w, m, me)
                cp.wait_send()
                cp.wait_recv()

    return _split_wait(body, "scatter_wait", pairs, after)


def _sum_partials(own, parts, half, name):
    R, Cn = own.shape
    tr = _pick(R, (256, 176, 128, 64, 32, 16, 8))
    nblk = R // tr

    def body(half_ref, own_ref, p_ref, o_ref):
        acc = own_ref[...].astype(F32)
        for d in range(N_DEV - 1):
            acc = acc + p_ref[d].astype(F32)
        o_ref[...] = acc

    return pl.pallas_call(
        body, name=name,
        grid_spec=pltpu.PrefetchScalarGridSpec(
            num_scalar_prefetch=1, grid=(nblk,),
            in_specs=[pl.BlockSpec((tr, Cn), lambda i, hr: (i, 0)),
                      pl.BlockSpec((N_DEV - 1, tr, Cn), lambda i, hr: (0, i, 0))],
            out_specs=pl.BlockSpec((tr, Cn), lambda i, hr: (hr[0] * nblk + i, 0))),
        out_shape=jax.ShapeDtypeStruct((2 * R, Cn), F32),
        compiler_params=_params(("parallel",)))(half, own, parts)


def _exchange_halves(grads):
    n = len(grads)

    def body(*refs):
        outs = refs[n:2 * n]
        send, recv = refs[2 * n:]
        x, y, c, me = _position()

        def copy(w, half):
            rows = _half(outs[w], half)
            return pltpu.make_async_remote_copy(src_ref=rows, dst_ref=rows, send_sem=send.at[w],
                                                recv_sem=recv.at[w], device_id=_device(me ^ 1), device_id_type=MESH)

        for w in range(n):
            copy(w, c).start()
        for w in range(n):
            copy(w, 1 - c).wait_recv()
        for w in range(n):
            copy(w, c).wait_send()

    return pl.pallas_call(
        body, name="exchange_halves", in_specs=[ANY] * n, out_specs=[ANY] * n,
        out_shape=[jax.ShapeDtypeStruct(a.shape, a.dtype) for a in grads],
        input_output_aliases={i: i for i in range(n)},
        scratch_shapes=[pltpu.SemaphoreType.DMA((n,)), pltpu.SemaphoreType.DMA((n,))],
        compiler_params=pltpu.CompilerParams(has_side_effects=True))(*grads)


def _all_reduce_small(vec):
    R, L = vec.shape

    def body(v_ref, o_ref, buf, send, recv):
        x, y, c, me = _position()
        buf[me] = v_ref[...]

        def copy(m, slot):
            return pltpu.make_async_remote_copy(src_ref=v_ref, dst_ref=buf.at[slot], send_sem=send.at[m - 1],
                                                recv_sem=recv.at[m - 1], device_id=_device(me ^ m),
                                                device_id_type=MESH)

        for m in range(1, N_DEV):
            copy(m, me).start()
        for m in range(1, N_DEV):
            copy(m, me ^ m).wait_recv()
        for m in range(1, N_DEV):
            copy(m, me).wait_send()
        acc = buf[0]
        for d in range(1, N_DEV):
            acc = acc + buf[d]
        o_ref[...] = acc

    vm = pl.BlockSpec(memory_space=pltpu.VMEM)
    return pl.pallas_call(
        body, name="all_reduce_small", in_specs=[vm], out_specs=vm, out_shape=jax.ShapeDtypeStruct((R, L), F32),
        scratch_shapes=[pltpu.VMEM((N_DEV, R, L), F32), pltpu.SemaphoreType.DMA((N_DEV - 1,)),
                        pltpu.SemaphoreType.DMA((N_DEV - 1,))],
        compiler_params=pltpu.CompilerParams(has_side_effects=True))(vec)


def _adamw(w, g, m, v, name):
    R, Cn = w.shape
    tr = _pick(R, (256, 176, 128, 64, 40, 32, 16, 8))

    def body(w_ref, g_ref, m_ref, v_ref, d_ref, nm_ref, nv_ref):
        gv = g_ref[...]
        nm = ADAM_B1 * m_ref[...] + (1.0 - ADAM_B1) * gv
        nv = ADAM_B2 * v_ref[...] + (1.0 - ADAM_B2) * (gv * gv)
        m_hat = nm / (1.0 - ADAM_B1 ** ADAM_STEP)
        v_hat = nv / (1.0 - ADAM_B2 ** ADAM_STEP)
        d_ref[...] = -ADAM_LR * (m_hat / (jnp.sqrt(v_hat) + ADAM_EPS) + ADAM_WD * w_ref[...])
        nm_ref[...] = nm
        nv_ref[...] = nv

    spec = pl.BlockSpec((tr, Cn), lambda i: (i, 0))
    return pl.pallas_call(
        body, name=name, grid=(R // tr,), in_specs=[spec] * 4, out_specs=[spec] * 3,
        out_shape=[jax.ShapeDtypeStruct((R, Cn), F32)] * 3, compiler_params=_params(("parallel",)))(w, g, m, v)


def _pack(arrays, rows):
    flat = jnp.concatenate([a.reshape(-1).astype(F32) for a in arrays])
    return jnp.pad(flat, (0, rows * 128 - flat.shape[0])).reshape(rows, 128)


def _unpack(packed, shapes):
    flat = packed.reshape(-1)
    out, off = [], 0
    for s in shapes:
        size = 1
        for d in s:
            size *= d
        out.append(flat[off:off + size].reshape(s))
        off += size
    return out


def _ffn_fwd(x, norm, arrived, tag, after=None):
    h = _rms_fwd(x, norm, f"{tag}_norm", after=after)
    w_up = arrived(f"{tag}_w_up", h)
    a, b, act = _ffn_up(h, w_up, f"{tag}_up")
    w_down = arrived(f"{tag}_w_down", act)
    out = _mm(act, w_down, 'nn', f"{tag}_down", F32, res=x, scale=0.5)
    return out, (h, a, b, act, w_up, w_down)


def _ffn_bwd(dout, x, norm, saved, tag, send):
    h, a, b, act, w_up, w_down = saved
    g_down = _mm(act, dout, 'tn', f"{tag}_down_dw", BF16, scale=0.5)
    token = send([f"{tag}_w_down"], [g_down])
    da, db = _ffn_down_bwd(dout, w_down, a, b, f"{tag}_down_dx", after=token)
    g_up = _dw_pieces(h, [da, db], f"{tag}_up_dw")
    token = send([f"{tag}_w_up"], [g_up])
    return _dx_rms_bwd([da, db], w_up, x, norm, dout, f"{tag}_up_dx", after=token)


def kernel(x, mem, ffn1_norm, ffn1_w_up, ffn1_w_down, mix_norm, mem_norm, w_in, b_gate, conv_dw_w, conv_dw_b, conv_ln_g, conv_ln_b, conv_w_pw, att_rel_bias, att_w_o, mem_w_kv, mem_w_o, w_out, ffn2_norm, ffn2_w_up, ffn2_w_down, final_norm, loss_target, m_ffn1_norm, m_ffn1_w_up, m_ffn1_w_down, m_mix_norm, m_mem_norm, m_w_in, m_b_gate, m_conv_dw_w, m_conv_dw_b, m_conv_ln_g, m_conv_ln_b, m_conv_w_pw, m_att_rel_bias, m_att_w_o, m_mem_w_kv, m_mem_w_o, m_w_out, m_ffn2_norm, m_ffn2_w_up, m_ffn2_w_down, m_final_norm, v_ffn1_norm, v_ffn1_w_up, v_ffn1_w_down, v_mix_norm, v_mem_norm, v_w_in, v_b_gate, v_conv_dw_w, v_conv_dw_b, v_conv_ln_g, v_conv_ln_b, v_conv_w_pw, v_att_rel_bias, v_att_w_o, v_mem_w_kv, v_mem_w_o, v_w_out, v_ffn2_norm, v_ffn2_w_up, v_ffn2_w_down, v_final_norm):
    given = dict(locals())
    wts = {n: given[n] for n in WEIGHTS}
    mom1 = {n: given["m_" + n] for n in WEIGHTS}
    mom2 = {n: given["v_" + n] for n in WEIGHTS}
    Bl, S, Dm = x.shape
    T = Bl * S
    x0 = x.reshape(T, Dm)
    tgt = loss_target.reshape(T, Dm)
    mem2 = mem.reshape(Bl * MEM_LEN, Dm)

    big_names = [n for n, _ in BIG]
    big_axes = [a for _, a in BIG]
    chip = 2 * lax.axis_index("x") + lax.axis_index("y")

    core = lax.axis_index("c")
    axis_of = dict(BIG)

    gather_groups = [['ffn1_w_up'], ['ffn1_w_down'], ['w_in', 'conv_dw_w'],
                     ['mem_w_kv', 'conv_w_pw', 'att_w_o', 'mem_w_o', 'w_out'], ['ffn2_w_up'], ['ffn2_w_down']]
    gather_names = [n for grp in gather_groups for n in grp]
    gather_axes = [axis_of.get(n, 1) for n in gather_names]
    shards = [jnp.pad(conv_dw_w[0], ((0, 1), (0, 0))) if n == 'conv_dw_w' else wts[n][0].astype(BF16)
              for n in gather_names]
    chip_idx = chip.reshape(1).astype(jnp.int32)
    first, first_token = _gather_start(
        shards[:1], [_place_block(shards[0], gather_axes[0], chip_idx, f"place_{gather_names[0]}")],
        gather_axes[:1], "gather_start_first")
    lands = [_place_block(sh, a, chip_idx, f"place_{n}", after=first_token)
             for sh, a, n in zip(shards[1:], gather_axes[1:], gather_names[1:])]
    rest, gather_token = _gather_start(shards[1:], lands, gather_axes[1:], "gather_start_rest")
    in_flight = dict(zip(gather_names, first + rest))
    full = {}

    def arrived(name, after):
        if name not in full:
            grp = next(grp for grp in gather_groups if name in grp)
            lands = _gather_wait([in_flight[n] for n in grp], [axis_of.get(n, 1) for n in grp], after,
                                 f"gather_wait_{grp[0]}")
            full.update(zip(grp, lands))
        return full[name]

    scattering = {}

    def send(names, grads):
        pairs, token = _scatter_start(grads, [axis_of[n] for n in names], f"scatter_start_{names[0]}")
        scattering.update(zip(names, pairs))
        return token

    final_g = final_norm.reshape(1, Dm)
    bias = _att_bias(att_rel_bias[0] + first_token[:1, :1])

    x1, ffn1_saved = _ffn_fwd(x0, ffn1_norm, arrived, "ffn1", after=gather_token)
    h = _rms_fwd(x1, mix_norm, "mix_norm")
    w_in_full = arrived('w_in', h)
    dw_full = full['conv_dw_w'][:CONV_KERNEL]
    proj = _mm(h, w_in_full, 'nn', "w_in", BF16)
    proj3 = proj.reshape(Bl, S, proj.shape[1])
    cv, c_act = _conv_fwd(proj3, dw_full, conv_dw_b, conv_ln_g, conv_ln_b)
    o_att = _att_fwd(proj3, bias)
    mem_h = _rms_fwd(mem2, mem_norm, "mem_norm")
    kv = _mm(mem_h, arrived('mem_w_kv', o_att), 'nn', "mem_kv", BF16)
    kv3 = kv.reshape(Bl, MEM_LEN, 2 * MEM_WIDTH)
    o_mem = _mem_fwd(proj3, kv3)
    c_act2, o_att2, o_mem2 = c_act.reshape(T, -1), o_att.reshape(T, -1), o_mem.reshape(T, -1)
    x2, yc, ya, ym, y = _mix_fwd(c_act2, o_att2, o_mem2, proj, b_gate, x1, full['conv_w_pw'], full['att_w_o'],
                                 full['mem_w_o'], full['w_out'])
    x3, ffn2_saved = _ffn_fwd(x2, ffn2_norm, arrived, "ffn2")
    dx3, g_final, loss_vec = _final_fwd_bwd(x3, tgt, final_g)

    g = {}
    dx2, g['ffn2_norm'] = _ffn_bwd(dx3, x2, ffn2_norm, ffn2_saved, "ffn2", send)
    dyc, dya, dym, dgl, g['b_gate'], dc, doa, dom = _mix_bwd(
        dx2, yc, ya, ym, proj, b_gate, full['conv_w_pw'], full['att_w_o'], full['mem_w_o'], full['w_out'])
    token = send(['w_out', 'conv_w_pw', 'att_w_o', 'mem_w_o'],
                 [_mm(y, dx2, 'tn', "w_out_dw", BF16), _mm(c_act2, dyc, 'tn', "conv_pw_dw", BF16),
                  _mm(o_att2, dya, 'tn', "att_o_dw", BF16), _mm(o_mem2, dym, 'tn', "mem_o_dw", BF16)])
    du, g_dw, g['conv_dw_b'], g['conv_ln_g'], g['conv_ln_b'] = _conv_bwd(
        proj3, cv, dc.reshape(Bl, S, -1), dw_full, conv_ln_g, conv_ln_b)
    dq, dk, dv, dbias = _att_bwd(proj3, doa.reshape(Bl, S, -1), bias)
    g['att_rel_bias'] = _rel_bias_grad(dbias)
    dmq, dkv = _mem_bwd(proj3, kv3, dom.reshape(Bl, S, -1))
    dkv2 = dkv.reshape(Bl * MEM_LEN, 2 * MEM_WIDTH)
    g_kv = _mm(mem_h, dkv2, 'tn', "mem_kv_dw", BF16, after=token)
    dmem_h = _mm(dkv2, full['mem_w_kv'], 'nt', "mem_kv_dx", F32)
    _, g['mem_norm'] = _rms_bwd(mem2, mem_norm, dmem_h, dmem_h, "mem_norm_bwd")
    left = jnp.concatenate([du.reshape(T, -1), dq.reshape(T, -1), dk.reshape(T, -1), dv.reshape(T, -1),
                            dmq.reshape(T, -1)], axis=1)
    dproj = lax.dynamic_update_slice(dgl, left, (0, 0))
    token = send(['mem_w_kv', 'w_in'], [g_kv, _mm(h, dproj, 'tn', "w_in_dw", BF16)])
    dx1, g['mix_norm'] = _dx_rms_bwd([dproj], w_in_full, x1, mix_norm, dx2, "w_in_dx", after=token)
    dx0, g['ffn1_norm'] = _ffn_bwd(dx1, x0, ffn1_norm, ffn1_saved, "ffn1", send)
    g['final_norm'] = g_final

    sent, landed = _scatter_wait([scattering[n] for n in big_names], big_axes, dx0)
    halves = []
    half_idx = core.reshape(1).astype(jnp.int32)
    for n, a, own_full, parts in zip(big_names, big_axes, sent, landed):
        R, Cn = _shard_shape(own_full, a)
        start = (core * (R // 2), chip * Cn) if a == 1 else (chip * R + core * (R // 2), 0)
        own = lax.dynamic_slice(own_full, start, (R // 2, Cn))
        halves.append(_sum_partials(own, parts, half_idx, f"sum_{n}"))
    for n, sg in zip(big_names, _exchange_halves(halves)):
        g[n] = sg

    small_shapes = [wts[n].shape for n in SMALL]
    n_small = sum(int(wts[n].size) for n in SMALL)
    n_red = n_small + CONV_KERNEL * CONV_WIDTH
    red = _all_reduce_small(_pack([g[n] for n in SMALL] + [g_dw], -(-n_red // 1024) * 8))
    red_list = _unpack(red, small_shapes + [(CONV_KERNEL, CONV_WIDTH)])
    for n, rg in zip(SMALL, red_list[:-1]):
        g[n] = rg
    dw_cols = conv_dw_w.shape[2]
    g['conv_dw_w'] = lax.dynamic_slice(red_list[-1], (0, chip * dw_cols), (CONV_KERNEL, dw_cols))[None]

    delta, new_m, new_v = {}, {}, {}
    for n in big_names:
        g[n] = g[n][None]
        d, nm, nv = _adamw(wts[n][0], g[n][0], mom1[n][0], mom2[n][0], f"adamw_{n}")
        delta[n], new_m[n], new_v[n] = d[None], nm[None], nv[None]
    rest = SMALL + ['conv_dw_w']
    rest_shapes = [wts[n].shape for n in rest]
    rows = -(-sum(int(wts[n].size) for n in rest) // 1024) * 8
    packed = [_pack([src[n] for n in rest], rows) for src in (wts, g, mom1, mom2)]
    for out, res in zip((delta, new_m, new_v), _adamw(*packed, "adamw_small")):
        for n, a in zip(rest, _unpack(res, rest_shapes)):
            out[n] = a

    loss = lax.psum(loss_vec[0, 0], ("x", "y", "c"))
    grad_x = dx0.reshape(Bl, S, Dm)
    return (loss, grad_x, *[g[n] for n in WEIGHTS], *[delta[n] for n in WEIGHTS],
            *[new_m[n] for n in WEIGHTS], *[new_v[n] for n in WEIGHTS])
```

```python
import jax
import jax.numpy as jnp
from jax import lax
from jax.experimental import pallas as pl
from jax.experimental.pallas import tpu as pltpu

F32 = jnp.float32
BF16 = jnp.bfloat16

D_MODEL = 1024
D_FF = 2816
CHUNK = 64
LEFT_CHUNKS = 8
MAX_REL = 128
N_REL = (CHUNK - 1) + MAX_REL + 1
CONV_WIDTH = 512
CONV_KERNEL = 31
ATT_HEADS = 8
ATT_WIDTH = 512
MEM_LEN = 256
MEM_HEADS = 4
MEM_HEAD_DIM = 128
MEM_WIDTH = 512
EPS = 1e-6
MASK_VALUE = -1e30
ATT_SCALE = 64 ** -0.5
MEM_SCALE = 128 ** -0.5

ADAM_LR = 0.001
ADAM_B1 = 0.9
ADAM_B2 = 0.999
ADAM_EPS = 1e-08
ADAM_WD = 0.01
ADAM_STEP = 10

QB = 256
KW = 3 * QB
CONV_PAD = 32
CONV_TILE = 256

VMEM_LIMIT = 56 << 20
MXU_COLS = 256

WEIGHTS = ['ffn1_norm', 'ffn1_w_up', 'ffn1_w_down', 'mix_norm', 'mem_norm', 'w_in', 'b_gate', 'conv_dw_w',
           'conv_dw_b', 'conv_ln_g', 'conv_ln_b', 'conv_w_pw', 'att_rel_bias', 'att_w_o', 'mem_w_kv', 'mem_w_o',
           'w_out', 'ffn2_norm', 'ffn2_w_up', 'ffn2_w_down', 'final_norm']
BIG = [('ffn1_w_up', 1), ('ffn1_w_down', 0), ('w_in', 1), ('conv_w_pw', 1), ('att_w_o', 1), ('mem_w_kv', 0),
       ('mem_w_o', 1), ('w_out', 0), ('ffn2_w_up', 1), ('ffn2_w_down', 0)]
SMALL = ['ffn1_norm', 'mix_norm', 'mem_norm', 'b_gate', 'conv_dw_b', 'conv_ln_g', 'conv_ln_b', 'att_rel_bias',
         'ffn2_norm', 'final_norm']
N_CHIPS = 4
N_DEV = 8
MESH = pl.DeviceIdType.MESH


def _pick(n, cands):
    for c in cands:
        if n % c == 0:
            return c
    return n


def _sig(x):
    return 0.5 * jnp.tanh(0.5 * x) + 0.5


def _params(sem=None, vmem=VMEM_LIMIT):
    return pltpu.CompilerParams(dimension_semantics=sem, vmem_limit_bytes=vmem)


def _dot(a, b, mode='nn'):
    dims = {'nn': (((1,), (0,)), ((), ())), 'nt': (((1,), (1,)), ((), ())), 'tn': (((0,), (0,)), ((), ()))}[mode]
    return lax.dot_general(a.astype(BF16), b.astype(BF16), dims, preferred_element_type=F32)


def _mm(a, b, mode, name, out_dtype, res=None, scale=1.0, after=None):
    if mode == 'nn':
        (M, C), (_, N) = a.shape, b.shape
    elif mode == 'nt':
        (M, C), (N, _) = a.shape, b.shape
    else:
        (C, M), (_, N) = a.shape, b.shape
    tm = _pick(M, (1024, 1408, 512, 256, 128))
    tn = _pick(N, (1024, 1408, 512, 256, 128))
    tc = C if C <= 2816 else _pick(C, (1024, 1408, 512, 256, 128))
    nk = C // tc
    if mode == 'nn':
        a_spec = pl.BlockSpec((tm, tc), lambda i, j, k: (i, k))
        b_spec = pl.BlockSpec((tc, tn), lambda i, j, k: (k, j))
    elif mode == 'nt':
        a_spec = pl.BlockSpec((tm, tc), lambda i, j, k: (i, k))
        b_spec = pl.BlockSpec((tn, tc), lambda i, j, k: (j, k))
    else:
        a_spec = pl.BlockSpec((tc, tm), lambda i, j, k: (k, i))
        b_spec = pl.BlockSpec((tc, tn), lambda i, j, k: (k, j))
    o_spec = pl.BlockSpec((tm, tn), lambda i, j, k: (i, j))
    has_res = res is not None
    has_after = after is not None

    def body(*refs):
        a_ref, b_ref = refs[:2]
        r_ref = refs[2] if has_res else None
        o_ref, acc_ref = refs[-2:]
        k = pl.program_id(2)

        def finish(acc):
            if scale != 1.0:
                acc = acc * scale
            if r_ref is not None:
                acc = r_ref[...] + acc
            o_ref[...] = acc.astype(o_ref.dtype)

        if nk == 1:
            finish(_dot(a_ref[...], b_ref[...], mode))
        else:
            @pl.when(k == 0)
            def _():
                acc_ref[...] = jnp.zeros_like(acc_ref)

            acc_ref[...] += _dot(a_ref[...], b_ref[...], mode)

            @pl.when(k == nk - 1)
            def _():
                finish(acc_ref[...])

    in_specs = [a_spec, b_spec] + ([o_spec] if has_res else []) + ([ANY] if has_after else [])
    args = (a, b) + ((res,) if has_res else ()) + ((after,) if has_after else ())
    acc_shape = (tm, tn) if nk > 1 else (8, 128)
    return pl.pallas_call(
        body, name=name, grid=(M // tm, N // tn, nk), in_specs=in_specs, out_specs=o_spec,
        out_shape=jax.ShapeDtypeStruct((M, N), out_dtype), scratch_shapes=[pltpu.VMEM(acc_shape, F32)],
        compiler_params=_params(("parallel", "parallel", "arbitrary")))(*args)


def _row_tile(T):
    return _pick(T, (512, 256, 128, 64, 32, 16, 8))


def _rms_fwd(x, g, name, after=None):
    T, Dm = x.shape
    tm = _row_tile(T)

    def body(x_ref, g_ref, *rest):
        o_ref = rest[-1]
        xv = x_ref[...]
        r = lax.rsqrt(jnp.mean(xv * xv, axis=-1, keepdims=True) + EPS)
        o_ref[...] = ((xv * r) * g_ref[...]).astype(o_ref.dtype)

    extra = () if after is None else (after,)
    return pl.pallas_call(
        body, name=name, grid=(T // tm,),
        in_specs=[pl.BlockSpec((tm, Dm), lambda i: (i, 0)), pl.BlockSpec((1, Dm), lambda i: (0, 0))]
        + [ANY] * len(extra),
        out_specs=pl.BlockSpec((tm, Dm), lambda i: (i, 0)), out_shape=jax.ShapeDtypeStruct((T, Dm), BF16),
        compiler_params=_params(("parallel",)))(x, g, *extra)


def _rms_bwd(x, g, dh, dres, name):
    T, Dm = x.shape
    tm = _row_tile(T)

    def body(x_ref, g_ref, dh_ref, dr_ref, dx_ref, dg_ref):
        i = pl.program_id(0)
        xv = x_ref[...]
        r = lax.rsqrt(jnp.mean(xv * xv, axis=-1, keepdims=True) + EPS)
        xr = xv * r
        dh_v = dh_ref[...].astype(F32)
        dyg = dh_v * g_ref[...]
        dx = r * (dyg - xr * jnp.mean(dyg * xr, axis=-1, keepdims=True))
        dx_ref[...] = dr_ref[...] + dx

        @pl.when(i == 0)
        def _():
            dg_ref[...] = jnp.zeros_like(dg_ref)

        dg_ref[...] += jnp.sum(dh_v * xr, axis=0, keepdims=True)

    row = pl.BlockSpec((tm, Dm), lambda i: (i, 0))
    vec = pl.BlockSpec((1, Dm), lambda i: (0, 0))
    return pl.pallas_call(
        body, name=name, grid=(T // tm,), in_specs=[row, vec, row, row], out_specs=[row, vec],
        out_shape=[jax.ShapeDtypeStruct((T, Dm), F32), jax.ShapeDtypeStruct((1, Dm), F32)],
        compiler_params=_params(("arbitrary",)))(x, g, dh, dres)


def _final_fwd_bwd(x3, tgt, g):
    T, Dm = x3.shape
    tm = _row_tile(T)

    def body(x_ref, t_ref, g_ref, dx_ref, dg_ref, loss_ref):
        i = pl.program_id(0)
        xv = x_ref[...]
        gg = g_ref[...]
        r = lax.rsqrt(jnp.mean(xv * xv, axis=-1, keepdims=True) + EPS)
        xr = xv * r
        err = xr * gg - t_ref[...]
        dout = err * (1.0 / Dm)
        dyg = dout * gg
        dx_ref[...] = r * (dyg - xr * jnp.mean(dyg * xr, axis=-1, keepdims=True))

        @pl.when(i == 0)
        def _():
            dg_ref[...] = jnp.zeros_like(dg_ref)
            loss_ref[...] = jnp.zeros_like(loss_ref)

        dg_ref[...] += jnp.sum(dout * xr, axis=0, keepdims=True)
        loss_ref[...] += jnp.zeros_like(loss_ref) + (0.5 / Dm) * jnp.sum(err * err)

    row = pl.BlockSpec((tm, Dm), lambda i: (i, 0))
    vec = pl.BlockSpec((1, Dm), lambda i: (0, 0))
    one = pl.BlockSpec((1, 128), lambda i: (0, 0))
    return pl.pallas_call(
        body, name="final_fwd_bwd", grid=(T // tm,), in_specs=[row, row, vec], out_specs=[row, vec, one],
        out_shape=[jax.ShapeDtypeStruct((T, Dm), F32), jax.ShapeDtypeStruct((1, Dm), F32),
                   jax.ShapeDtypeStruct((1, 128), F32)],
        compiler_params=_params(("arbitrary",)))(x3, tgt, g)


def _ffn_up(h, w_up, name):
    T, K = h.shape
    Fh = w_up.shape[1] // 2
    tm = _pick(T, (1024, 512, 256, 128))
    tn = _pick(Fh, (1408, 512, 256, 128))
    nj = Fh // tn

    def body(h_ref, wa_ref, wb_ref, a_ref, b_ref, act_ref):
        hv = h_ref[...]
        for c0 in range(0, tn, MXU_COLS):
            cs = slice(c0, min(c0 + MXU_COLS, tn))
            a = _dot(hv, wa_ref[:, cs])
            b = _dot(hv, wb_ref[:, cs])
            a_ref[:, cs] = a.astype(BF16)
            b_ref[:, cs] = b.astype(BF16)
            act_ref[:, cs] = (a * _sig(a) * b).astype(BF16)

    out = pl.BlockSpec((tm, tn), lambda i, j: (i, j))
    return pl.pallas_call(
        body, name=name, grid=(T // tm, nj),
        in_specs=[pl.BlockSpec((tm, K), lambda i, j: (i, 0)), pl.BlockSpec((K, tn), lambda i, j: (0, j)),
                  pl.BlockSpec((K, tn), lambda i, j: (0, j + nj))],
        out_specs=[out, out, out], out_shape=[jax.ShapeDtypeStruct((T, Fh), BF16)] * 3,
        compiler_params=_params(("parallel", "parallel")))(h, w_up, w_up)


def _ffn_down_bwd(dout, w_down, a, b, name, after=None):
    T, Dm = dout.shape
    Fh = w_down.shape[0]
    tm = _pick(T, (1024, 512, 256, 128))
    tn = _pick(Fh, (1408, 512, 256, 128))

    def body(d_ref, w_ref, a_ref, b_ref, *rest):
        da_ref, db_ref = rest[-2:]
        dv = d_ref[...].astype(BF16)
        for c0 in range(0, tn, MXU_COLS):
            cs = slice(c0, min(c0 + MXU_COLS, tn))
            dact = _dot(dv, w_ref[cs, :], 'nt') * 0.5
            av = a_ref[:, cs].astype(F32)
            bv = b_ref[:, cs].astype(F32)
            s = _sig(av)
            da_ref[:, cs] = (dact * bv * s * (1.0 + av * (1.0 - s))).astype(BF16)
            db_ref[:, cs] = (dact * av * s).astype(BF16)

    tile = pl.BlockSpec((tm, tn), lambda i, j: (i, j))
    extra = () if after is None else (after,)
    return pl.pallas_call(
        body, name=name, grid=(T // tm, Fh // tn),
        in_specs=[pl.BlockSpec((tm, Dm), lambda i, j: (i, 0)), pl.BlockSpec((tn, Dm), lambda i, j: (j, 0)),
                  tile, tile] + [ANY] * len(extra),
        out_specs=[tile, tile], out_shape=[jax.ShapeDtypeStruct((T, Fh), BF16)] * 2,
        compiler_params=_params(("parallel", "parallel")))(dout, w_down, a, b, *extra)


def _dx_rms_bwd(pieces, w, x, g, dres, name, after=None):
    T, Dm = x.shape
    width = pieces[0].shape[1]
    tm = _pick(T, (1024, 512, 256, 128))
    tc = _pick(width, (1408, 1024, 512, 256, 128))
    per = width // tc
    nk = per * len(pieces)
    npc = len(pieces)
    rows = _pick(tm, (256, 128))

    def body(*refs):
        p_refs = refs[:npc]
        w_ref, x_hbm, g_ref, dr_hbm = refs[npc:npc + 4]
        dx_ref, dg_ref, acc_ref, x_buf, dr_buf, sems = refs[-6:]
        i = pl.program_id(0)
        k = pl.program_id(1)
        tile = pl.ds(pl.multiple_of(i * tm, tm), tm)
        fetch_x = pltpu.make_async_copy(x_hbm.at[tile, :], x_buf, sems.at[0])
        fetch_dr = pltpu.make_async_copy(dr_hbm.at[tile, :], dr_buf, sems.at[1])

        @pl.when(k == 0)
        def _():
            fetch_x.start()
            fetch_dr.start()
            acc_ref[...] = jnp.zeros_like(acc_ref)

        @pl.when((i == 0) & (k == 0))
        def _():
            dg_ref[...] = jnp.zeros_like(dg_ref)

        for p in range(npc):
            @pl.when((k >= p * per) & (k < (p + 1) * per))
            def _(p=p):
                acc_ref[...] += _dot(p_refs[p][...], w_ref[...], 'nt')

        @pl.when(k == nk - 1)
        def _():
            fetch_x.wait()
            fetch_dr.wait()

            def chunk(c, carry):
                rs = pl.ds(pl.multiple_of(c * rows, rows), rows)
                dh = acc_ref[rs, :]
                xv = x_buf[rs, :]
                r = lax.rsqrt(jnp.mean(xv * xv, axis=-1, keepdims=True) + EPS)
                xr = xv * r
                dyg = dh * g_ref[...]
                dx_ref[rs, :] = dr_buf[rs, :] + r * (dyg - xr * jnp.mean(dyg * xr, axis=-1, keepdims=True))
                dg_ref[...] += jnp.sum(dh * xr, axis=0, keepdims=True)
                return carry

            lax.fori_loop(0, tm // rows, chunk, 0)

    def piece_spec(p):
        return pl.BlockSpec((tm, tc), lambda i, k: (i, jnp.clip(k - p * per, 0, per - 1)))

    row = pl.BlockSpec((tm, Dm), lambda i, k: (i, 0))
    vec = pl.BlockSpec((1, Dm), lambda i, k: (0, 0))
    extra = () if after is None else (after,)
    return pl.pallas_call(
        body, name=name, grid=(T // tm, nk),
        in_specs=[piece_spec(p) for p in range(npc)] + [pl.BlockSpec((Dm, tc), lambda i, k: (0, k)), ANY, vec, ANY]
        + [ANY] * len(extra),
        out_specs=[row, vec], out_shape=[jax.ShapeDtypeStruct((T, Dm), F32), jax.ShapeDtypeStruct((1, Dm), F32)],
        scratch_shapes=[pltpu.VMEM((tm, Dm), F32), pltpu.VMEM((tm, Dm), F32), pltpu.VMEM((tm, Dm), F32),
                        pltpu.SemaphoreType.DMA((2,))],
        compiler_params=_params(("arbitrary", "arbitrary")))(*pieces, w, x, g, dres, *extra)


def _dw_pieces(a, pieces, name):
    C, M = a.shape
    width = pieces[0].shape[1]
    npc = len(pieces)
    tm = _pick(M, (1024, 512, 256, 128))
    tn = _pick(width, (1408, 1024, 512, 256, 128))
    tc = _pick(C, (1024, 512, 256, 128))
    per = width // tn
    nk = C // tc

    def body(*refs):
        a_ref = refs[0]
        p_refs = refs[1:1 + npc]
        o_ref, acc_ref = refs[-2:]
        j = pl.program_id(1)
        k = pl.program_id(2)

        @pl.when(k == 0)
        def _():
            acc_ref[...] = jnp.zeros_like(acc_ref)

        for p in range(npc):
            @pl.when((j >= p * per) & (j < (p + 1) * per))
            def _(p=p):
                acc_ref[...] += _dot(a_ref[...], p_refs[p][...], 'tn')

        @pl.when(k == nk - 1)
        def _():
            o_ref[...] = acc_ref[...].astype(o_ref.dtype)

    def piece_spec(p):
        return pl.BlockSpec((tc, tn), lambda i, j, k: (k, jnp.clip(j - p * per, 0, per - 1)))

    return pl.pallas_call(
        body, name=name, grid=(M // tm, per * npc, nk),
        in_specs=[pl.BlockSpec((tc, tm), lambda i, j, k: (k, i))] + [piece_spec(p) for p in range(npc)],
        out_specs=pl.BlockSpec((tm, tn), lambda i, j, k: (i, j)),
        out_shape=jax.ShapeDtypeStruct((M, width * npc), BF16), scratch_shapes=[pltpu.VMEM((tm, tn), F32)],
        compiler_params=_params(("parallel", "parallel", "arbitrary")))(a, *pieces)


def _mix_fwd(c_act, o_att, o_mem, proj, b_gate, x1, w_pw, w_o, w_mo, w_out):
    T, Dm = x1.shape
    W = c_act.shape[1]
    tm = _pick(T, (256, 128, 64, 32, 16, 8))

    def body(c_ref, oa_ref, om_ref, gl_ref, bg_ref, x1_ref, wpw_ref, wo_ref, wmo_ref, wout_ref,
             x2_ref, yc_ref, ya_ref, ym_ref):
        yc = _dot(c_ref[...], wpw_ref[...])
        ya = _dot(oa_ref[...], wo_ref[...])
        ym = _dot(om_ref[...], wmo_ref[...])
        g = _sig(gl_ref[...].astype(F32) + bg_ref[...])
        y = g[:, :Dm] * yc + g[:, Dm:2 * Dm] * ya + g[:, 2 * Dm:] * ym
        x2_ref[...] = x1_ref[...] + _dot(y, wout_ref[...])
        yc_ref[...] = yc.astype(BF16)
        ya_ref[...] = ya.astype(BF16)
        ym_ref[...] = ym.astype(BF16)

    rowW = pl.BlockSpec((tm, W), lambda i: (i, 0))
    rowD = pl.BlockSpec((tm, Dm), lambda i: (i, 0))
    full = lambda s: pl.BlockSpec(s, lambda i: (0, 0))
    return pl.pallas_call(
        body, name="mix_fwd", grid=(T // tm,),
        in_specs=[rowW, rowW, rowW, pl.BlockSpec((tm, 3 * Dm), lambda i: (i, 1)), full((1, 3 * Dm)), rowD,
                  full((W, Dm)), full((W, Dm)), full((W, Dm)), full((Dm, Dm))],
        out_specs=[rowD] * 4,
        out_shape=[jax.ShapeDtypeStruct((T, Dm), F32)] + [jax.ShapeDtypeStruct((T, Dm), BF16)] * 3,
        compiler_params=_params(("parallel",)))(c_act, o_att, o_mem, proj, b_gate, x1, w_pw, w_o, w_mo, w_out)


def _mix_bwd(dx2, yc, ya, ym, c_act, o_att, o_mem, proj, b_gate, w_pw, w_o, w_mo, w_out):
    T, Dm = dx2.shape
    W = w_pw.shape[0]
    tm = _pick(T, (256, 128, 64, 32, 16, 8))
    nt = T // tm

    def body(dx_ref, yc_ref, ya_ref, ym_ref, c_ref, oa_ref, om_ref, gl_ref, bg_ref, wpw_ref, wo_ref, wmo_ref,
             wout_ref, dgl_ref, dbg_ref, dc_ref, doa_ref, dom_ref, gpw_ref, go_ref, gmo_ref, gout_ref,
             apw, ao, amo, aout):
        i = pl.program_id(0)

        @pl.when(i == 0)
        def _():
            dbg_ref[...] = jnp.zeros_like(dbg_ref)
            for acc in (apw, ao, amo, aout):
                acc[...] = jnp.zeros_like(acc)

        dxv = dx_ref[...].astype(BF16)
        dy = _dot(dxv, wout_ref[...], 'nt')
        g = _sig(gl_ref[...].astype(F32) + bg_ref[...])
        branches = ((yc_ref, c_ref, wpw_ref, dc_ref, apw), (ya_ref, oa_ref, wo_ref, doa_ref, ao),
                    (ym_ref, om_ref, wmo_ref, dom_ref, amo))
        y = jnp.zeros((tm, Dm), F32)
        for n, (y_ref, in_ref, w_ref, dk_ref, acc) in enumerate(branches):
            gk = g[:, n * Dm:(n + 1) * Dm]
            yk = y_ref[...].astype(F32)
            dyk = dy * gk
            dgl = dyk * yk * (1.0 - gk)
            dgl_ref[:, n * Dm:(n + 1) * Dm] = dgl.astype(BF16)
            dbg_ref[:, n * Dm:(n + 1) * Dm] += jnp.sum(dgl, axis=0, keepdims=True)
            dyk = dyk.astype(BF16)
            dk_ref[...] = _dot(dyk, w_ref[...], 'nt').astype(BF16)
            acc[...] += _dot(in_ref[...], dyk, 'tn')
            y = y + gk * yk
        aout[...] += _dot(y, dxv, 'tn')

        @pl.when(i == nt - 1)
        def _():
            for acc, out in ((apw, gpw_ref), (ao, go_ref), (amo, gmo_ref), (aout, gout_ref)):
                out[...] = acc[...].astype(BF16)

    rowW = pl.BlockSpec((tm, W), lambda i: (i, 0))
    rowD = pl.BlockSpec((tm, Dm), lambda i: (i, 0))
    full = lambda s: pl.BlockSpec(s, lambda i: (0, 0))
    return pl.pallas_call(
        body, name="mix_bwd", grid=(nt,),
        in_specs=[rowD, rowD, rowD, rowD, rowW, rowW, rowW, pl.BlockSpec((tm, 3 * Dm), lambda i: (i, 1)),
                  full((1, 3 * Dm)), full((W, Dm)), full((W, Dm)), full((W, Dm)), full((Dm, Dm))],
        out_specs=[pl.BlockSpec((tm, 3 * Dm), lambda i: (i, 1)), full((1, 3 * Dm)), rowW, rowW, rowW,
                   full((W, Dm)), full((W, Dm)), full((W, Dm)), full((Dm, Dm))],
        out_shape=[jax.ShapeDtypeStruct((T, 6 * Dm), BF16), jax.ShapeDtypeStruct((1, 3 * Dm), F32)]
        + [jax.ShapeDtypeStruct((T, W), BF16)] * 3 + [jax.ShapeDtypeStruct((W, Dm), BF16)] * 3
        + [jax.ShapeDtypeStruct((Dm, Dm), BF16)],
        scratch_shapes=[pltpu.VMEM((W, Dm), F32)] * 3 + [pltpu.VMEM((Dm, Dm), F32)],
        compiler_params=_params(("arbitrary",)))(dx2, yc, ya, ym, c_act, o_att, o_mem, proj, b_gate, w_pw, w_o,
                                                 w_mo, w_out)


def _ln_swish(cv, lg, lb):
    mu = jnp.mean(cv, axis=-1, keepdims=True)
    xc = cv - mu
    r = lax.rsqrt(jnp.mean(xc * xc, axis=-1, keepdims=True) + EPS)
    n = xc * r
    l = n * lg + lb
    return r, n, l


def _shift_copies(src, r0, win, shifts):
    win[...] = src[pl.ds(r0, CONV_TILE + CONV_PAD + 8), :]
    for s in range(8):
        shifts[s] = win[s:s + CONV_TILE + CONV_PAD, :]


def _tap(shifts, d):
    return shifts[d % 8, d - d % 8:d - d % 8 + CONV_TILE, :]


def _conv_fwd(proj3, dw_w, dw_b, ln_g, ln_b):
    Bl, S, _ = proj3.shape
    C, K, TS, PAD = CONV_WIDTH, CONV_KERNEL, CONV_TILE, CONV_PAD
    nt = S // TS

    def body(u_ref, w_ref, b_ref, lg_ref, lb_ref, cv_ref, c_ref, vbuf, win, shifts):
        vbuf[0:PAD, :] = jnp.zeros((PAD, C), F32)
        vbuf[S + PAD:S + PAD + 8, :] = jnp.zeros((8, C), F32)

        def glu(t, carry):
            r0 = pl.multiple_of(t * TS, TS)
            u = u_ref[pl.ds(r0, TS), :].astype(F32)
            vbuf[pl.ds(PAD + r0, TS), :] = u[:, :C] * _sig(u[:, C:])
            return carry

        lax.fori_loop(0, nt, glu, 0)

        def conv(t, carry):
            r0 = pl.multiple_of(t * TS, TS)
            _shift_copies(vbuf, r0, win, shifts)
            acc = jnp.zeros((TS, C), F32)
            for j in range(K):
                acc = acc + w_ref[j:j + 1, :] * _tap(shifts, PAD - (K - 1) + j)
            cv = acc + b_ref[...]
            cv_ref[pl.ds(r0, TS), :] = cv
            _, _, l = _ln_swish(cv, lg_ref[...], lb_ref[...])
            c_ref[pl.ds(r0, TS), :] = (l * _sig(l)).astype(BF16)
            return carry

        lax.fori_loop(0, nt, conv, 0)

    vec = pl.BlockSpec((1, C), lambda b: (0, 0))
    return pl.pallas_call(
        body, name="conv_fwd", grid=(Bl,),
        in_specs=[pl.BlockSpec((None, S, 2 * C), lambda b: (b, 0, 0)), pl.BlockSpec((K, C), lambda b: (0, 0)),
                  vec, vec, vec],
        out_specs=[pl.BlockSpec((None, S, C), lambda b: (b, 0, 0))] * 2,
        out_shape=[jax.ShapeDtypeStruct((Bl, S, C), F32), jax.ShapeDtypeStruct((Bl, S, C), BF16)],
        scratch_shapes=[pltpu.VMEM((S + PAD + 8, C), F32), pltpu.VMEM((TS + PAD + 8, C), F32),
                        pltpu.VMEM((8, TS + PAD, C), F32)],
        compiler_params=_params(("parallel",)))(proj3, dw_w, dw_b, ln_g, ln_b)


def _conv_bwd(proj3, cv, dc, dw_w, ln_g, ln_b):
    Bl, S, _ = proj3.shape
    C, K, TS, PAD = CONV_WIDTH, CONV_KERNEL, CONV_TILE, CONV_PAD
    nt = S // TS

    def body(u_ref, cv_ref, dc_ref, w_ref, lg_ref, lb_ref, du_ref, dw_ref, db_ref, dlg_ref, dlb_ref,
             vbuf, gbuf, win, shifts, dwacc):
        b = pl.program_id(0)

        @pl.when(b == 0)
        def _():
            dw_ref[...] = jnp.zeros_like(dw_ref)
            db_ref[...] = jnp.zeros_like(db_ref)
            dlg_ref[...] = jnp.zeros_like(dlg_ref)
            dlb_ref[...] = jnp.zeros_like(dlb_ref)

        vbuf[0:PAD, :] = jnp.zeros((PAD, C), F32)
        vbuf[S + PAD:S + PAD + 8, :] = jnp.zeros((8, C), F32)
        gbuf[S:S + PAD + 8, :] = jnp.zeros((PAD + 8, C), F32)
        dwacc[...] = jnp.zeros_like(dwacc)

        def norm_bwd(t, carry):
            r0 = pl.multiple_of(t * TS, TS)
            u = u_ref[pl.ds(r0, TS), :].astype(F32)
            vbuf[pl.ds(PAD + r0, TS), :] = u[:, :C] * _sig(u[:, C:])
            r, n, l = _ln_swish(cv_ref[pl.ds(r0, TS), :], lg_ref[...], lb_ref[...])
            s = _sig(l)
            dl = dc_ref[pl.ds(r0, TS), :].astype(F32) * s * (1.0 + l * (1.0 - s))
            dlg_ref[...] += jnp.sum(dl * n, axis=0, keepdims=True)
            dlb_ref[...] += jnp.sum(dl, axis=0, keepdims=True)
            dn = dl * lg_ref[...]
            dcv = r * (dn - jnp.mean(dn, axis=-1, keepdims=True) - n * jnp.mean(dn * n, axis=-1, keepdims=True))
            gbuf[pl.ds(r0, TS), :] = dcv
            db_ref[...] += jnp.sum(dcv, axis=0, keepdims=True)
            return carry

        lax.fori_loop(0, nt, norm_bwd, 0)

        def conv_bwd(t, carry):
            r0 = pl.multiple_of(t * TS, TS)
            _shift_copies(gbuf, r0, win, shifts)
            dv = jnp.zeros((TS, C), F32)
            for j in range(K):
                dv = dv + w_ref[j:j + 1, :] * _tap(shifts, K - 1 - j)
            u = u_ref[pl.ds(r0, TS), :].astype(F32)
            a, g = u[:, :C], u[:, C:]
            s = _sig(g)
            du_ref[pl.ds(r0, TS), 0:C] = (dv * s).astype(BF16)
            du_ref[pl.ds(r0, TS), C:2 * C] = (dv * a * s * (1.0 - s)).astype(BF16)
            dcv = gbuf[pl.ds(r0, TS), :]
            _shift_copies(vbuf, r0, win, shifts)
            for j in range(K):
                prod = dcv * _tap(shifts, PAD - (K - 1) + j)
                dwacc[j] += jnp.sum(prod.reshape(TS // 8, 8, C), axis=0)
            return carry

        lax.fori_loop(0, nt, conv_bwd, 0)
        dw_ref[...] += jnp.sum(dwacc[...], axis=1)

    vec = pl.BlockSpec((1, C), lambda b: (0, 0))
    seq = lambda w: pl.BlockSpec((None, S, w), lambda b: (b, 0, 0))
    return pl.pallas_call(
        body, name="conv_bwd", grid=(Bl,),
        in_specs=[seq(2 * C), seq(C), seq(C), pl.BlockSpec((K, C), lambda b: (0, 0)), vec, vec],
        out_specs=[seq(2 * C), pl.BlockSpec((K, C), lambda b: (0, 0)), vec, vec, vec],
        out_shape=[jax.ShapeDtypeStruct((Bl, S, 2 * C), BF16), jax.ShapeDtypeStruct((K, C), F32)]
        + [jax.ShapeDtypeStruct((1, C), F32)] * 3,
        scratch_shapes=[pltpu.VMEM((S + PAD + 8, C), F32), pltpu.VMEM((S + PAD + 8, C), F32),
                        pltpu.VMEM((TS + PAD + 8, C), F32), pltpu.VMEM((8, TS + PAD, C), F32),
                        pltpu.VMEM((K, 8, C), F32)],
        compiler_params=_params(("arbitrary",)))(proj3, cv, dc, dw_w, ln_g, ln_b)


def _att_bias(rel_bias):
    H = rel_bias.shape[0]
    Wd = KW + QB
    c = jnp.arange(Wd + 1)
    by_offset = rel_bias[:, jnp.clip(KW - c, -(CHUNK - 1), MAX_REL) + (CHUNK - 1)]
    flat = jnp.broadcast_to(by_offset[:, None, :], (H, QB, Wd + 1)).reshape(H, QB * (Wd + 1))
    skew = jnp.pad(flat, ((0, 0), (0, (QB + 1) * Wd - QB * (Wd + 1)))).reshape(H, QB + 1, Wd)[:, :QB, QB:]
    qi = jnp.arange(QB)[:, None]
    kj = jnp.arange(KW)[None, :]
    dchunk = ((KW - QB) + qi) // CHUNK - kj // CHUNK
    band = (dchunk >= 0) & (dchunk <= LEFT_CHUNKS)
    return jnp.where(band[None], skew, MASK_VALUE)


def _head_masks():
    lane = lax.broadcasted_iota(jnp.int32, (1, 128), 1)
    return (lane < 64, lane >= 64)


def _att_probs(qh, k2, bias, valid):
    s = _dot(qh, k2, 'nt') * ATT_SCALE + bias
    s = jnp.where(valid, s, MASK_VALUE)
    e = jnp.exp(s - jnp.max(s, axis=-1, keepdims=True))
    return e * (1.0 / jnp.sum(e, axis=-1, keepdims=True))


def _att_specs(S, q_col):
    nb = S // QB
    q_spec = pl.BlockSpec((None, QB, ATT_WIDTH), lambda b, i: (b, jnp.minimum(i, nb - 1), q_col))

    def kv_spec(col, kb):
        return pl.BlockSpec((None, QB, ATT_WIDTH),
                            lambda b, i: (b, jnp.clip(i - 2 + kb, 0, nb - 1), col))

    return q_spec, [kv_spec(3, kb) for kb in range(3)], [kv_spec(4, kb) for kb in range(3)]


def _att_fwd(proj3, bias):
    Bl, S, _ = proj3.shape
    nb = S // QB
    q_spec, k_specs, v_specs = _att_specs(S, 2)

    def body(q_ref, k0, k1, k2r, v0, v1, v2r, bias_ref, o_ref):
        i = pl.program_id(1)
        masks = _head_masks()
        valid = lax.broadcasted_iota(jnp.int32, (QB, KW), 1) >= (2 - i) * QB
        for pr in range(ATT_HEADS // 2):
            ls = slice(128 * pr, 128 * (pr + 1))
            q2 = q_ref[:, ls]
            k2 = jnp.concatenate([k0[:, ls], k1[:, ls], k2r[:, ls]], axis=0)
            v2 = jnp.concatenate([v0[:, ls], v1[:, ls], v2r[:, ls]], axis=0)
            o2 = jnp.zeros((QB, 128), F32)
            for hh in range(2):
                p = _att_probs(jnp.where(masks[hh], q2, 0), k2, bias_ref[2 * pr + hh], valid)
                o2 = o2 + _dot(p, jnp.where(masks[hh], v2, 0))
            o_ref[:, ls] = o2.astype(BF16)

    return pl.pallas_call(
        body, name="att_fwd", grid=(Bl, nb),
        in_specs=[q_spec] + k_specs + v_specs + [pl.BlockSpec((ATT_HEADS, QB, KW), lambda b, i: (0, 0, 0))],
        out_specs=pl.BlockSpec((None, QB, ATT_WIDTH), lambda b, i: (b, i, 0)),
        out_shape=jax.ShapeDtypeStruct((Bl, S, ATT_WIDTH), BF16),
        compiler_params=_params(("parallel", "arbitrary")))(*([proj3] * 7), bias)


def _att_bwd(proj3, do, bias):
    Bl, S, _ = proj3.shape
    nb = S // QB
    q_spec, k_specs, v_specs = _att_specs(S, 2)
    do_spec = pl.BlockSpec((None, QB, ATT_WIDTH), lambda b, i: (b, jnp.minimum(i, nb - 1), 0))
    kv_out = pl.BlockSpec((None, QB, ATT_WIDTH), lambda b, i: (b, jnp.clip(i - 2, 0, nb - 1), 0))
    bias_spec = pl.BlockSpec((ATT_HEADS, QB, KW), lambda b, i: (0, 0, 0))

    def body(q_ref, k0, k1, k2r, v0, v1, v2r, do_ref, bias_ref, dq_ref, dk_ref, dv_ref, db_ref, dkw, dvw):
        b = pl.program_id(0)
        i = pl.program_id(1)

        @pl.when((b == 0) & (i == 0))
        def _():
            db_ref[...] = jnp.zeros_like(db_ref)

        @pl.when(i == 0)
        def _():
            dkw[...] = jnp.zeros_like(dkw)
            dvw[...] = jnp.zeros_like(dvw)

        @pl.when(i < nb)
        def _():
            masks = _head_masks()
            valid = lax.broadcasted_iota(jnp.int32, (QB, KW), 1) >= (2 - i) * QB
            for pr in range(ATT_HEADS // 2):
                ls = slice(128 * pr, 128 * (pr + 1))
                q2 = q_ref[:, ls]
                do2 = do_ref[:, ls]
                k2 = jnp.concatenate([k0[:, ls], k1[:, ls], k2r[:, ls]], axis=0)
                v2 = jnp.concatenate([v0[:, ls], v1[:, ls], v2r[:, ls]], axis=0)
                dq2 = jnp.zeros((QB, 128), F32)
                dk2 = jnp.zeros((KW, 128), F32)
                dv2 = jnp.zeros((KW, 128), F32)
                for hh in range(2):
                    h = 2 * pr + hh
                    qh = jnp.where(masks[hh], q2, 0)
                    doh = jnp.where(masks[hh], do2, 0)
                    p = _att_probs(qh, k2, bias_ref[h], valid)
                    dp = _dot(doh, v2, 'nt')
                    ds = p * (dp - jnp.sum(p * dp, axis=-1, keepdims=True))
                    db_ref[h] += ds
                    dq2 = dq2 + _dot(ds, jnp.where(masks[hh], k2, 0))
                    dk2 = dk2 + _dot(ds, qh, 'tn')
                    dv2 = dv2 + _dot(p, doh, 'tn')
                dq_ref[:, ls] = (dq2 * ATT_SCALE).astype(BF16)
                dkw[:, ls] += dk2 * ATT_SCALE
                dvw[:, ls] += dv2

        dk_ref[...] = dkw[0:QB, :].astype(BF16)
        dv_ref[...] = dvw[0:QB, :].astype(BF16)
        for buf in (dkw, dvw):
            rest = buf[QB:KW, :]
            buf[0:KW - QB, :] = rest
            buf[KW - QB:KW, :] = jnp.zeros((QB, ATT_WIDTH), F32)

    blk = jax.ShapeDtypeStruct((Bl, S, ATT_WIDTH), BF16)
    return pl.pallas_call(
        body, name="att_bwd", grid=(Bl, nb + 2),
        in_specs=[q_spec] + k_specs + v_specs + [do_spec, bias_spec],
        out_specs=[do_spec, kv_out, kv_out, bias_spec],
        out_shape=[blk, blk, blk, jax.ShapeDtypeStruct((ATT_HEADS, QB, KW), F32)],
        scratch_shapes=[pltpu.VMEM((KW, ATT_WIDTH), F32), pltpu.VMEM((KW, ATT_WIDTH), F32)],
        compiler_params=_params(("arbitrary", "arbitrary")))(*([proj3] * 7), do, bias)


def _rel_bias_grad(dbias):
    H = dbias.shape[0]
    Wd = KW + QB
    padded = jnp.pad(dbias, ((0, 0), (0, 1), (QB, 0)))
    skew = padded.reshape(H, (QB + 1) * Wd)[:, :QB * (Wd + 1)].reshape(H, QB, Wd + 1)[:, :, :Wd]
    c = jnp.arange(Wd)[:, None]
    bins = (jnp.clip(KW - c, -(CHUNK - 1), MAX_REL) + (CHUNK - 1) == jnp.arange(N_REL)[None, :]).astype(F32)

    def body(s_ref, bins_ref, o_ref):
        col = jnp.sum(s_ref[...], axis=1)
        o_ref[...] = jnp.dot(col, bins_ref[...], preferred_element_type=F32, precision=lax.Precision.HIGHEST)

    return pl.pallas_call(
        body, name="rel_bias_grad", grid=(1,),
        in_specs=[pl.BlockSpec((H, QB, Wd), lambda i: (0, 0, 0)), pl.BlockSpec((Wd, N_REL), lambda i: (0, 0))],
        out_specs=pl.BlockSpec((H, N_REL), lambda i: (0, 0)), out_shape=jax.ShapeDtypeStruct((H, N_REL), F32),
        compiler_params=_params(("arbitrary",)))(skew, bins)


MEM_TILE = 512


def _mem_probs(qh, kh):
    s = _dot(qh, kh, 'nt') * MEM_SCALE
    e = jnp.exp(s - jnp.max(s, axis=-1, keepdims=True))
    return e * (1.0 / jnp.sum(e, axis=-1, keepdims=True))


def _mem_fwd(proj3, kv3):
    Bl, S, _ = proj3.shape
    tq = _pick(S, (MEM_TILE, 256))
    hd = MEM_HEAD_DIM

    def body(q_ref, kv_ref, o_ref):
        for h in range(MEM_HEADS):
            p = _mem_probs(q_ref[:, h * hd:(h + 1) * hd], kv_ref[:, h * hd:(h + 1) * hd])
            o_ref[:, h * hd:(h + 1) * hd] = _dot(p, kv_ref[:, MEM_WIDTH + h * hd:MEM_WIDTH + (h + 1) * hd]).astype(BF16)

    return pl.pallas_call(
        body, name="mem_fwd", grid=(Bl, S // tq),
        in_specs=[pl.BlockSpec((None, tq, MEM_WIDTH), lambda b, i: (b, i, 5)),
                  pl.BlockSpec((None, MEM_LEN, 2 * MEM_WIDTH), lambda b, i: (b, 0, 0))],
        out_specs=pl.BlockSpec((None, tq, MEM_WIDTH), lambda b, i: (b, i, 0)),
        out_shape=jax.ShapeDtypeStruct((Bl, S, MEM_WIDTH), BF16),
        compiler_params=_params(("parallel", "parallel")))(proj3, kv3)


def _mem_bwd(proj3, kv3, do):
    Bl, S, _ = proj3.shape
    tq = _pick(S, (MEM_TILE, 256))
    hd = MEM_HEAD_DIM

    def body(q_ref, kv_ref, do_ref, dq_ref, dkv_ref):
        i = pl.program_id(1)

        @pl.when(i == 0)
        def _():
            dkv_ref[...] = jnp.zeros_like(dkv_ref)

        for h in range(MEM_HEADS):
            ks = slice(h * hd, (h + 1) * hd)
            vs = slice(MEM_WIDTH + h * hd, MEM_WIDTH + (h + 1) * hd)
            qh, kh, vh, doh = q_ref[:, ks], kv_ref[:, ks], kv_ref[:, vs], do_ref[:, ks]
            p = _mem_probs(qh, kh)
            dp = _dot(doh, vh, 'nt')
            ds = p * (dp - jnp.sum(p * dp, axis=-1, keepdims=True))
            dq_ref[:, ks] = (_dot(ds, kh) * MEM_SCALE).astype(BF16)
            dkv_ref[:, ks] += _dot(ds, qh, 'tn') * MEM_SCALE
            dkv_ref[:, vs] += _dot(p, doh, 'tn')

    return pl.pallas_call(
        body, name="mem_bwd", grid=(Bl, S // tq),
        in_specs=[pl.BlockSpec((None, tq, MEM_WIDTH), lambda b, i: (b, i, 5)),
                  pl.BlockSpec((None, MEM_LEN, 2 * MEM_WIDTH), lambda b, i: (b, 0, 0)),
                  pl.BlockSpec((None, tq, MEM_WIDTH), lambda b, i: (b, i, 0))],
        out_specs=[pl.BlockSpec((None, tq, MEM_WIDTH), lambda b, i: (b, i, 0)),
                   pl.BlockSpec((None, MEM_LEN, 2 * MEM_WIDTH), lambda b, i: (b, 0, 0))],
        out_shape=[jax.ShapeDtypeStruct((Bl, S, MEM_WIDTH), BF16),
                   jax.ShapeDtypeStruct((Bl, MEM_LEN, 2 * MEM_WIDTH), F32)],
        compiler_params=_params(("parallel", "arbitrary")))(proj3, kv3, do)


def _position():
    x, y, c = lax.axis_index("x"), lax.axis_index("y"), lax.axis_index("c")
    return x, y, c, 4 * x + 2 * y + c


def _device(idx):
    return ((idx >> 2) & 1, (idx >> 1) & 1, idx & 1)


def _half_block(ref, axis, shard_shape, k, h):
    R, Cn = shard_shape
    if axis == 1:
        return ref.at[pl.ds(h * (R // 2), R // 2), pl.ds(k * Cn, Cn)]
    return ref.at[pl.ds(k * R + h * (R // 2), R // 2), :]


def _block(ref, axis, shard_shape, k):
    R, Cn = shard_shape
    if axis == 1:
        return ref.at[:, pl.ds(k * Cn, Cn)]
    return ref.at[pl.ds(k * R, R), :]


def _half(ref, h):
    R = ref.shape[0]
    return ref.at[pl.ds(h * (R // 2), R // 2), :]


ANY = pl.BlockSpec(memory_space=pl.ANY)


HBM = pl.BlockSpec(memory_space=pltpu.HBM)
SEM = pl.BlockSpec(memory_space=pltpu.SEMAPHORE)
VMEM_WHOLE = pl.BlockSpec(memory_space=pltpu.VMEM)
EFFECT = pltpu.SideEffectType.DATAFLOW_SIDE_EFFECTING


def _in_hbm(a):
    return pltpu.with_memory_space_constraint(a, pltpu.HBM)


def _split_start(body, name, sources, lands, n_copies):
    n = len(sources)
    out_shape, out_specs = [], []
    for _ in range(n):
        out_shape += [pltpu.SemaphoreType.DMA((n_copies,)), pltpu.SemaphoreType.DMA((n_copies,))]
        out_specs += [SEM, SEM]
    out_shape += [pltpu.HBM(a.shape, a.dtype) for a in list(sources) + list(lands)]
    out_specs += [HBM] * (2 * n)
    out_shape.append(jax.ShapeDtypeStruct((8, 128), F32))
    out_specs.append(VMEM_WHOLE)

    def call_body(*refs):
        srcs, lnds = refs[:n], refs[n:2 * n]
        sems = refs[2 * n:4 * n]
        token = refs[-1]
        body(srcs, lnds, sems[0::2], sems[1::2])
        token[...] = jnp.zeros_like(token)

    res = pl.pallas_call(
        call_body, name=name, in_specs=[HBM] * (2 * n), out_specs=out_specs, out_shape=out_shape,
        input_output_aliases={i: 2 * n + i for i in range(2 * n)},
        compiler_params=pltpu.CompilerParams(has_side_effects=EFFECT))(
            *[_in_hbm(a) for a in list(sources) + list(lands)])
    pairs = [(res[2 * w], res[2 * w + 1], res[2 * n + w], res[3 * n + w]) for w in range(n)]
    return pairs, res[-1]


def _split_wait(body, name, pairs, after):
    n = len(pairs)

    def call_body(*refs):
        srcs, lnds = refs[:n], refs[n:2 * n]
        sems = refs[2 * n:4 * n]
        body(srcs, lnds, sems[0::2], sems[1::2])

    args = [_in_hbm(p[2]) for p in pairs] + [_in_hbm(p[3]) for p in pairs]
    for p in pairs:
        args += [p[0], p[1]]
    res = pl.pallas_call(
        call_body, name=name, in_specs=[HBM] * (2 * n) + [SEM] * (2 * n) + [ANY], out_specs=[HBM] * (2 * n),
        out_shape=[pltpu.HBM(a.shape, a.dtype) for a in args[:2 * n]],
        input_output_aliases={i: i for i in range(2 * n)},
        compiler_params=pltpu.CompilerParams(has_side_effects=EFFECT))(*args, after)
    return res[:n], res[n:]


def _place_block(shard, axis, chip_idx, name, after=None):
    R, Cn = shard.shape
    tr = _pick(R, (256, 176, 128, 64, 32, 16, 8))
    nblk = R // tr

    def body(k_ref, s_ref, *rest):
        rest[-1][...] = s_ref[...]

    if axis == 1:
        out_shape, out_index = (R, 4 * Cn), lambda i, k: (i, k[0])
    else:
        out_shape, out_index = (4 * R, Cn), lambda i, k: (k[0] * nblk + i, 0)
    extra = () if after is None else (after,)
    return pl.pallas_call(
        body, name=name,
        grid_spec=pltpu.PrefetchScalarGridSpec(
            num_scalar_prefetch=1, grid=(nblk,),
            in_specs=[pl.BlockSpec((tr, Cn), lambda i, k: (i, 0))] + [ANY] * len(extra),
            out_specs=pl.BlockSpec((tr, Cn), out_index)),
        out_shape=jax.ShapeDtypeStruct(out_shape, shard.dtype),
        compiler_params=_params(("parallel",)))(chip_idx, shard, *extra)


def _gather_copy(srcs, lnds, send, recv, axes, shapes, w, j, me):
    chip = me >> 1
    return (pltpu.make_async_remote_copy(
        src_ref=srcs[w], dst_ref=_block(lnds[w], axes[w], shapes[w], chip), send_sem=send[w].at[j],
        recv_sem=recv[w].at[j], device_id=_device(me ^ (2 * (j + 1))), device_id_type=MESH),
            pltpu.make_async_remote_copy(
        src_ref=srcs[w], dst_ref=_block(lnds[w], axes[w], shapes[w], chip ^ (j + 1)), send_sem=send[w].at[j],
        recv_sem=recv[w].at[j], device_id=_device(me ^ (2 * (j + 1))), device_id_type=MESH))


def _gather_start(shards, lands, axes, name):
    shapes = [s.shape for s in shards]

    def body(srcs, lnds, send, recv):
        x, y, c, me = _position()
        for w in range(len(shards)):
            for j in range(3):
                _gather_copy(srcs, lnds, send, recv, axes, shapes, w, j, me)[0].start()

    return _split_start(body, name, shards, lands, 3)


def _gather_wait(pairs, axes, after, name):
    shapes = [p[2].shape for p in pairs]

    def body(srcs, lnds, send, recv):
        x, y, c, me = _position()
        for w in range(len(pairs)):
            for j in range(3):
                sent, landed = _gather_copy(srcs, lnds, send, recv, axes, shapes, w, j, me)
                sent.wait_send()
                landed.wait_recv()

    return _split_wait(body, name, pairs, after)[1]


def _shard_shape(grad, axis):
    return (grad.shape[0], grad.shape[1] // 4) if axis == 1 else (grad.shape[0] // 4, grad.shape[1])


def _scatter_copy(srcs, lnds, send, recv, axes, shapes, w, m, me):
    peer = me ^ m
    return pltpu.make_async_remote_copy(
        src_ref=_half_block(srcs[w], axes[w], shapes[w], peer >> 1, peer & 1), dst_ref=lnds[w].at[m - 1],
        send_sem=send[w].at[m - 1], recv_sem=recv[w].at[m - 1], device_id=_device(peer), device_id_type=MESH)


def _scatter_start(grads, axes, name):
    shapes = [_shard_shape(g, a) for g, a in zip(grads, axes)]
    lands = [lax.empty((N_DEV - 1, R // 2, Cn), BF16) for R, Cn in shapes]

    def body(srcs, lnds, send, recv):
        x, y, c, me = _position()
        for w in range(len(grads)):
            for m in range(1, N_DEV):
                _scatter_copy(srcs, lnds, send, recv, axes, shapes, w, m, me).start()

    return _split_start(body, name, grads, lands, N_DEV - 1)


def _scatter_wait(pairs, axes, after):
    shapes = [_shard_shape(p[2], a) for p, a in zip(pairs, axes)]

    def body(srcs, lnds, send, recv):
        x, y, c, me = _position()
        for w in range(len(pairs)):
            for m in range(1, N_DEV):
                cp = _scatter_copy(srcs, lnds, send, recv, axes, shapes, w, m, me)
                cp.wait_send()
                cp.wait_recv()

    return _split_wait(body, "scatter_wait", pairs, after)


def _sum_partials(own, parts, half, name):
    R, Cn = own.shape
    tr = _pick(R, (256, 176, 128, 64, 32, 16, 8))
    nblk = R // tr

    def body(half_ref, own_ref, p_ref, o_ref):
        acc = own_ref[...].astype(F32)
        for d in range(N_DEV - 1):
            acc = acc + p_ref[d].astype(F32)
        o_ref[...] = acc

    return pl.pallas_call(
        body, name=name,
        grid_spec=pltpu.PrefetchScalarGridSpec(
            num_scalar_prefetch=1, grid=(nblk,),
            in_specs=[pl.BlockSpec((tr, Cn), lambda i, hr: (i, 0)),
                      pl.BlockSpec((N_DEV - 1, tr, Cn), lambda i, hr: (0, i, 0))],
            out_specs=pl.BlockSpec((tr, Cn), lambda i, hr: (hr[0] * nblk + i, 0))),
        out_shape=jax.ShapeDtypeStruct((2 * R, Cn), F32),
        compiler_params=_params(("parallel",)))(half, own, parts)


def _exchange_halves(grads):
    n = len(grads)

    def body(*refs):
        outs = refs[n:2 * n]
        send, recv = refs[2 * n:]
        x, y, c, me = _position()

        def copy(w, half):
            rows = _half(outs[w], half)
            return pltpu.make_async_remote_copy(src_ref=rows, dst_ref=rows, send_sem=send.at[w],
                                                recv_sem=recv.at[w], device_id=_device(me ^ 1), device_id_type=MESH)

        for w in range(n):
            copy(w, c).start()
        for w in range(n):
            copy(w, 1 - c).wait_recv()
        for w in range(n):
            copy(w, c).wait_send()

    return pl.pallas_call(
        body, name="exchange_halves", in_specs=[ANY] * n, out_specs=[ANY] * n,
        out_shape=[jax.ShapeDtypeStruct(a.shape, a.dtype) for a in grads],
        input_output_aliases={i: i for i in range(n)},
        scratch_shapes=[pltpu.SemaphoreType.DMA((n,)), pltpu.SemaphoreType.DMA((n,))],
        compiler_params=pltpu.CompilerParams(has_side_effects=True))(*grads)


def _all_reduce_small(vec):
    R, L = vec.shape

    def body(v_ref, o_ref, buf, send, recv):
        x, y, c, me = _position()
        buf[me] = v_ref[...]

        def copy(m, slot):
            return pltpu.make_async_remote_copy(src_ref=v_ref, dst_ref=buf.at[slot], send_sem=send.at[m - 1],
                                                recv_sem=recv.at[m - 1], device_id=_device(me ^ m),
                                                device_id_type=MESH)

        for m in range(1, N_DEV):
            copy(m, me).start()
        for m in range(1, N_DEV):
            copy(m, me ^ m).wait_recv()
        for m in range(1, N_DEV):
            copy(m, me).wait_send()
        acc = buf[0]
        for d in range(1, N_DEV):
            acc = acc + buf[d]
        o_ref[...] = acc

    vm = pl.BlockSpec(memory_space=pltpu.VMEM)
    return pl.pallas_call(
        body, name="all_reduce_small", in_specs=[vm], out_specs=vm, out_shape=jax.ShapeDtypeStruct((R, L), F32),
        scratch_shapes=[pltpu.VMEM((N_DEV, R, L), F32), pltpu.SemaphoreType.DMA((N_DEV - 1,)),
                        pltpu.SemaphoreType.DMA((N_DEV - 1,))],
        compiler_params=pltpu.CompilerParams(has_side_effects=True))(vec)


def _adamw(w, g, m, v, name):
    R, Cn = w.shape
    tr = _pick(R, (256, 176, 128, 64, 40, 32, 16, 8))

    def body(w_ref, g_ref, m_ref, v_ref, d_ref, nm_ref, nv_ref):
        gv = g_ref[...]
        nm = ADAM_B1 * m_ref[...] + (1.0 - ADAM_B1) * gv
        nv = ADAM_B2 * v_ref[...] + (1.0 - ADAM_B2) * (gv * gv)
        m_hat = nm / (1.0 - ADAM_B1 ** ADAM_STEP)
        v_hat = nv / (1.0 - ADAM_B2 ** ADAM_STEP)
        d_ref[...] = -ADAM_LR * (m_hat / (jnp.sqrt(v_hat) + ADAM_EPS) + ADAM_WD * w_ref[...])
        nm_ref[...] = nm
        nv_ref[...] = nv

    spec = pl.BlockSpec((tr, Cn), lambda i: (i, 0))
    return pl.pallas_call(
        body, name=name, grid=(R // tr,), in_specs=[spec] * 4, out_specs=[spec] * 3,
        out_shape=[jax.ShapeDtypeStruct((R, Cn), F32)] * 3, compiler_params=_params(("parallel",)))(w, g, m, v)


def _pack(arrays, rows):
    flat = jnp.concatenate([a.reshape(-1).astype(F32) for a in arrays])
    return jnp.pad(flat, (0, rows * 128 - flat.shape[0])).reshape(rows, 128)


def _unpack(packed, shapes):
    flat = packed.reshape(-1)
    out, off = [], 0
    for s in shapes:
        size = 1
        for d in s:
            size *= d
        out.append(flat[off:off + size].reshape(s))
        off += size
    return out


def _ffn_fwd(x, norm, arrived, tag, after=None):
    h = _rms_fwd(x, norm, f"{tag}_norm", after=after)
    w_up = arrived(f"{tag}_w_up", h)
    a, b, act = _ffn_up(h, w_up, f"{tag}_up")
    w_down = arrived(f"{tag}_w_down", act)
    out = _mm(act, w_down, 'nn', f"{tag}_down", F32, res=x, scale=0.5)
    return out, (h, a, b, act, w_up, w_down)


def _ffn_bwd(dout, x, norm, saved, tag, send):
    h, a, b, act, w_up, w_down = saved
    g_down = _mm(act, dout, 'tn', f"{tag}_down_dw", BF16, scale=0.5)
    token = send([f"{tag}_w_down"], [g_down])
    da, db = _ffn_down_bwd(dout, w_down, a, b, f"{tag}_down_dx", after=token)
    g_up = _dw_pieces(h, [da, db], f"{tag}_up_dw")
    token = send([f"{tag}_w_up"], [g_up])
    return _dx_rms_bwd([da, db], w_up, x, norm, dout, f"{tag}_up_dx", after=token)


def kernel(x, mem, ffn1_norm, ffn1_w_up, ffn1_w_down, mix_norm, mem_norm, w_in, b_gate, conv_dw_w, conv_dw_b, conv_ln_g, conv_ln_b, conv_w_pw, att_rel_bias, att_w_o, mem_w_kv, mem_w_o, w_out, ffn2_norm, ffn2_w_up, ffn2_w_down, final_norm, loss_target, m_ffn1_norm, m_ffn1_w_up, m_ffn1_w_down, m_mix_norm, m_mem_norm, m_w_in, m_b_gate, m_conv_dw_w, m_conv_dw_b, m_conv_ln_g, m_conv_ln_b, m_conv_w_pw, m_att_rel_bias, m_att_w_o, m_mem_w_kv, m_mem_w_o, m_w_out, m_ffn2_norm, m_ffn2_w_up, m_ffn2_w_down, m_final_norm, v_ffn1_norm, v_ffn1_w_up, v_ffn1_w_down, v_mix_norm, v_mem_norm, v_w_in, v_b_gate, v_conv_dw_w, v_conv_dw_b, v_conv_ln_g, v_conv_ln_b, v_conv_w_pw, v_att_rel_bias, v_att_w_o, v_mem_w_kv, v_mem_w_o, v_w_out, v_ffn2_norm, v_ffn2_w_up, v_ffn2_w_down, v_final_norm):
    given = dict(locals())
    wts = {n: given[n] for n in WEIGHTS}
    mom1 = {n: given["m_" + n] for n in WEIGHTS}
    mom2 = {n: given["v_" + n] for n in WEIGHTS}
    Bl, S, Dm = x.shape
    T = Bl * S
    x0 = x.reshape(T, Dm)
    tgt = loss_target.reshape(T, Dm)
    mem2 = mem.reshape(Bl * MEM_LEN, Dm)

    big_names = [n for n, _ in BIG]
    big_axes = [a for _, a in BIG]
    chip = 2 * lax.axis_index("x") + lax.axis_index("y")

    core = lax.axis_index("c")
    axis_of = dict(BIG)

    gather_groups = [['ffn1_w_up'], ['ffn1_w_down'], ['w_in', 'conv_dw_w'],
                     ['mem_w_kv', 'conv_w_pw', 'att_w_o', 'mem_w_o', 'w_out'], ['ffn2_w_up'], ['ffn2_w_down']]
    gather_names = [n for grp in gather_groups for n in grp]
    gather_axes = [axis_of.get(n, 1) for n in gather_names]
    shards = [jnp.pad(conv_dw_w[0], ((0, 1), (0, 0))) if n == 'conv_dw_w' else wts[n][0].astype(BF16)
              for n in gather_names]
    chip_idx = chip.reshape(1).astype(jnp.int32)
    first, first_token = _gather_start(
        shards[:1], [_place_block(shards[0], gather_axes[0], chip_idx, f"place_{gather_names[0]}")],
        gather_axes[:1], "gather_start_first")
    lands = [_place_block(sh, a, chip_idx, f"place_{n}", after=first_token)
             for sh, a, n in zip(shards[1:], gather_axes[1:], gather_names[1:])]
    rest, gather_token = _gather_start(shards[1:], lands, gather_axes[1:], "gather_start_rest")
    in_flight = dict(zip(gather_names, first + rest))
    full = {}

    def arrived(name, after):
        if name not in full:
            grp = next(grp for grp in gather_groups if name in grp)
            lands = _gather_wait([in_flight[n] for n in grp], [axis_of.get(n, 1) for n in grp], after,
                                 f"gather_wait_{grp[0]}")
            full.update(zip(grp, lands))
        return full[name]

    scattering = {}

    def send(names, grads):
        pairs, token = _scatter_start(grads, [axis_of[n] for n in names], f"scatter_start_{names[0]}")
        scattering.update(zip(names, pairs))
        return token

    final_g = final_norm.reshape(1, Dm)
    bias = _att_bias(att_rel_bias[0] + first_token[:1, :1])

    x1, ffn1_saved = _ffn_fwd(x0, ffn1_norm, arrived, "ffn1", after=gather_token)
    h = _rms_fwd(x1, mix_norm, "mix_norm")
    w_in_full = arrived('w_in', h)
    dw_full = full['conv_dw_w'][:CONV_KERNEL]
    proj = _mm(h, w_in_full, 'nn', "w_in", BF16)
    proj3 = proj.reshape(Bl, S, proj.shape[1])
    cv, c_act = _conv_fwd(proj3, dw_full, conv_dw_b, conv_ln_g, conv_ln_b)
    o_att = _att_fwd(proj3, bias)
    mem_h = _rms_fwd(mem2, mem_norm, "mem_norm")
    kv = _mm(mem_h, arrived('mem_w_kv', o_att), 'nn', "mem_kv", BF16)
    kv3 = kv.reshape(Bl, MEM_LEN, 2 * MEM_WIDTH)
    o_mem = _mem_fwd(proj3, kv3)
    c_act2, o_att2, o_mem2 = c_act.reshape(T, -1), o_att.reshape(T, -1), o_mem.reshape(T, -1)
    x2, yc, ya, ym = _mix_fwd(c_act2, o_att2, o_mem2, proj, b_gate, x1, full['conv_w_pw'], full['att_w_o'],
                              full['mem_w_o'], full['w_out'])
    x3, ffn2_saved = _ffn_fwd(x2, ffn2_norm, arrived, "ffn2")
    dx3, g_final, loss_vec = _final_fwd_bwd(x3, tgt, final_g)

    g = {}
    dx2, g['ffn2_norm'] = _ffn_bwd(dx3, x2, ffn2_norm, ffn2_saved, "ffn2", send)
    dgl, g['b_gate'], dc, doa, dom, g_pw, g_o, g_mo, g_out = _mix_bwd(
        dx2, yc, ya, ym, c_act2, o_att2, o_mem2, proj, b_gate, full['conv_w_pw'], full['att_w_o'],
        full['mem_w_o'], full['w_out'])
    token = send(['w_out', 'conv_w_pw', 'att_w_o', 'mem_w_o'], [g_out, g_pw, g_o, g_mo])
    du, g_dw, g['conv_dw_b'], g['conv_ln_g'], g['conv_ln_b'] = _conv_bwd(
        proj3, cv, dc.reshape(Bl, S, -1), dw_full, conv_ln_g, conv_ln_b)
    dq, dk, dv, dbias = _att_bwd(proj3, doa.reshape(Bl, S, -1), bias)
    g['att_rel_bias'] = _rel_bias_grad(dbias)
    dmq, dkv = _mem_bwd(proj3, kv3, dom.reshape(Bl, S, -1))
    dkv2 = dkv.reshape(Bl * MEM_LEN, 2 * MEM_WIDTH)
    g_kv = _mm(mem_h, dkv2, 'tn', "mem_kv_dw", BF16, after=token)
    dmem_h = _mm(dkv2, full['mem_w_kv'], 'nt', "mem_kv_dx", F32)
    _, g['mem_norm'] = _rms_bwd(mem2, mem_norm, dmem_h, dmem_h, "mem_norm_bwd")
    left = jnp.concatenate([du.reshape(T, -1), dq.reshape(T, -1), dk.reshape(T, -1), dv.reshape(T, -1),
                            dmq.reshape(T, -1)], axis=1)
    dproj = lax.dynamic_update_slice(dgl, left, (0, 0))
    token = send(['mem_w_kv', 'w_in'], [g_kv, _mm(h, dproj, 'tn', "w_in_dw", BF16)])
    dx1, g['mix_norm'] = _dx_rms_bwd([dproj], w_in_full, x1, mix_norm, dx2, "w_in_dx", after=token)
    dx0, g['ffn1_norm'] = _ffn_bwd(dx1, x0, ffn1_norm, ffn1_saved, "ffn1", send)
    g['final_norm'] = g_final

    sent, landed = _scatter_wait([scattering[n] for n in big_names], big_axes, dx0)
    halves = []
    half_idx = core.reshape(1).astype(jnp.int32)
    for n, a, own_full, parts in zip(big_names, big_axes, sent, landed):
        R, Cn = _shard_shape(own_full, a)
        start = (core * (R // 2), chip * Cn) if a == 1 else (chip * R + core * (R // 2), 0)
        own = lax.dynamic_slice(own_full, start, (R // 2, Cn))
        halves.append(_sum_partials(own, parts, half_idx, f"sum_{n}"))
    for n, sg in zip(big_names, _exchange_halves(halves)):
        g[n] = sg

    small_shapes = [wts[n].shape for n in SMALL]
    n_small = sum(int(wts[n].size) for n in SMALL)
    n_red = n_small + CONV_KERNEL * CONV_WIDTH
    red = _all_reduce_small(_pack([g[n] for n in SMALL] + [g_dw], -(-n_red // 1024) * 8))
    red_list = _unpack(red, small_shapes + [(CONV_KERNEL, CONV_WIDTH)])
    for n, rg in zip(SMALL, red_list[:-1]):
        g[n] = rg
    dw_cols = conv_dw_w.shape[2]
    g['conv_dw_w'] = lax.dynamic_slice(red_list[-1], (0, chip * dw_cols), (CONV_KERNEL, dw_cols))[None]

    delta, new_m, new_v = {}, {}, {}
    for n in big_names:
        g[n] = g[n][None]
        d, nm, nv = _adamw(wts[n][0], g[n][0], mom1[n][0], mom2[n][0], f"adamw_{n}")
        delta[n], new_m[n], new_v[n] = d[None], nm[None], nv[None]
    rest = SMALL + ['conv_dw_w']
    rest_shapes = [wts[n].shape for n in rest]
    rows = -(-sum(int(wts[n].size) for n in rest) // 1024) * 8
    packed = [_pack([src[n] for n in rest], rows) for src in (wts, g, mom1, mom2)]
    for out, res in zip((delta, new_m, new_v), _adamw(*packed, "adamw_small")):
        for n, a in zip(rest, _unpack(res, rest_shapes)):
            out[n] = a

    loss = lax.psum(loss_vec[0, 0], ("x", "y", "c"))
    grad_x = dx0.reshape(Bl, S, Dm)
    return (loss, grad_x, *[g[n] for n in WEIGHTS], *[delta[n] for n in WEIGHTS],
            *[new_m[n] for n in WEIGHTS], *[new_v[n] for n in WEIGHTS])
```

```python
import jax
import jax.numpy as jnp
from jax import lax
from jax.experimental import pallas as pl
from jax.experimental.pallas import tpu as pltpu

F32 = jnp.float32
BF16 = jnp.bfloat16

D_MODEL = 1024
D_FF = 2816
CHUNK = 64
LEFT_CHUNKS = 8
MAX_REL = 128
N_REL = (CHUNK - 1) + MAX_REL + 1
CONV_WIDTH = 512
CONV_KERNEL = 31
ATT_HEADS = 8
ATT_WIDTH = 512
MEM_LEN = 256
MEM_HEADS = 4
MEM_HEAD_DIM = 128
MEM_WIDTH = 512
EPS = 1e-6
MASK_VALUE = -1e30
ATT_SCALE = 64 ** -0.5
MEM_SCALE = 128 ** -0.5

ADAM_LR = 0.001
ADAM_B1 = 0.9
ADAM_B2 = 0.999
ADAM_EPS = 1e-08
ADAM_WD = 0.01
ADAM_STEP = 10

QB = 256
KW = 3 * QB
CONV_PAD = 32
CONV_TILE = 256

VMEM_LIMIT = 56 << 20
MXU_COLS = 256

WEIGHTS = ['ffn1_norm', 'ffn1_w_up', 'ffn1_w_down', 'mix_norm', 'mem_norm', 'w_in', 'b_gate', 'conv_dw_w',
           'conv_dw_b', 'conv_ln_g', 'conv_ln_b', 'conv_w_pw', 'att_rel_bias', 'att_w_o', 'mem_w_kv', 'mem_w_o',
           'w_out', 'ffn2_norm', 'ffn2_w_up', 'ffn2_w_down', 'final_norm']
BIG = [('ffn1_w_up', 1), ('ffn1_w_down', 0), ('w_in', 1), ('conv_w_pw', 1), ('att_w_o', 1), ('mem_w_kv', 0),
       ('mem_w_o', 1), ('w_out', 0), ('ffn2_w_up', 1), ('ffn2_w_down', 0)]
SMALL = ['ffn1_norm', 'mix_norm', 'mem_norm', 'b_gate', 'conv_dw_b', 'conv_ln_g', 'conv_ln_b', 'att_rel_bias',
         'ffn2_norm', 'final_norm']
N_CHIPS = 4
N_DEV = 8
MESH = pl.DeviceIdType.MESH


def _pick(n, cands):
    for c in cands:
        if n % c == 0:
            return c
    return n


def _sig(x):
    return 0.5 * jnp.tanh(0.5 * x) + 0.5


def _params(sem=None, vmem=VMEM_LIMIT):
    return pltpu.CompilerParams(dimension_semantics=sem, vmem_limit_bytes=vmem)


def _dot(a, b, mode='nn'):
    dims = {'nn': (((1,), (0,)), ((), ())), 'nt': (((1,), (1,)), ((), ())), 'tn': (((0,), (0,)), ((), ()))}[mode]
    return lax.dot_general(a.astype(BF16), b.astype(BF16), dims, preferred_element_type=F32)


def _mm(a, b, mode, name, out_dtype, res=None, scale=1.0, after=None):
    if mode == 'nn':
        (M, C), (_, N) = a.shape, b.shape
    elif mode == 'nt':
        (M, C), (N, _) = a.shape, b.shape
    else:
        (C, M), (_, N) = a.shape, b.shape
    tm = _pick(M, (1024, 1408, 512, 256, 128))
    tn = _pick(N, (1024, 1408, 512, 256, 128))
    tc = C if C <= 2816 else _pick(C, (1024, 1408, 512, 256, 128))
    nk = C // tc
    if mode == 'nn':
        a_spec = pl.BlockSpec((tm, tc), lambda i, j, k: (i, k))
        b_spec = pl.BlockSpec((tc, tn), lambda i, j, k: (k, j))
    elif mode == 'nt':
        a_spec = pl.BlockSpec((tm, tc), lambda i, j, k: (i, k))
        b_spec = pl.BlockSpec((tn, tc), lambda i, j, k: (j, k))
    else:
        a_spec = pl.BlockSpec((tc, tm), lambda i, j, k: (k, i))
        b_spec = pl.BlockSpec((tc, tn), lambda i, j, k: (k, j))
    o_spec = pl.BlockSpec((tm, tn), lambda i, j, k: (i, j))
    has_res = res is not None
    has_after = after is not None

    def body(*refs):
        a_ref, b_ref = refs[:2]
        r_ref = refs[2] if has_res else None
        o_ref, acc_ref = refs[-2:]
        k = pl.program_id(2)

        def finish(acc):
            if scale != 1.0:
                acc = acc * scale
            if r_ref is not None:
                acc = r_ref[...] + acc
            o_ref[...] = acc.astype(o_ref.dtype)

        if nk == 1:
            finish(_dot(a_ref[...], b_ref[...], mode))
        else:
            @pl.when(k == 0)
            def _():
                acc_ref[...] = jnp.zeros_like(acc_ref)

            acc_ref[...] += _dot(a_ref[...], b_ref[...], mode)

            @pl.when(k == nk - 1)
            def _():
                finish(acc_ref[...])

    in_specs = [a_spec, b_spec] + ([o_spec] if has_res else []) + ([ANY] if has_after else [])
    args = (a, b) + ((res,) if has_res else ()) + ((after,) if has_after else ())
    acc_shape = (tm, tn) if nk > 1 else (8, 128)
    return pl.pallas_call(
        body, name=name, grid=(M // tm, N // tn, nk), in_specs=in_specs, out_specs=o_spec,
        out_shape=jax.ShapeDtypeStruct((M, N), out_dtype), scratch_shapes=[pltpu.VMEM(acc_shape, F32)],
        compiler_params=_params(("parallel", "parallel", "arbitrary")))(*args)


def _row_tile(T):
    return _pick(T, (512, 256, 128, 64, 32, 16, 8))


def _rms_fwd(x, g, name, after=None):
    T, Dm = x.shape
    tm = _row_tile(T)

    def body(x_ref, g_ref, *rest):
        o_ref = rest[-1]
        xv = x_ref[...]
        r = lax.rsqrt(jnp.mean(xv * xv, axis=-1, keepdims=True) + EPS)
        o_ref[...] = ((xv * r) * g_ref[...]).astype(o_ref.dtype)

    extra = () if after is None else (after,)
    return pl.pallas_call(
        body, name=name, grid=(T // tm,),
        in_specs=[pl.BlockSpec((tm, Dm), lambda i: (i, 0)), pl.BlockSpec((1, Dm), lambda i: (0, 0))]
        + [ANY] * len(extra),
        out_specs=pl.BlockSpec((tm, Dm), lambda i: (i, 0)), out_shape=jax.ShapeDtypeStruct((T, Dm), BF16),
        compiler_params=_params(("parallel",)))(x, g, *extra)


def _rms_bwd(x, g, dh, dres, name):
    T, Dm = x.shape
    tm = _row_tile(T)

    def body(x_ref, g_ref, dh_ref, dr_ref, dx_ref, dg_ref):
        i = pl.program_id(0)
        xv = x_ref[...]
        r = lax.rsqrt(jnp.mean(xv * xv, axis=-1, keepdims=True) + EPS)
        xr = xv * r
        dh_v = dh_ref[...].astype(F32)
        dyg = dh_v * g_ref[...]
        dx = r * (dyg - xr * jnp.mean(dyg * xr, axis=-1, keepdims=True))
        dx_ref[...] = dr_ref[...] + dx

        @pl.when(i == 0)
        def _():
            dg_ref[...] = jnp.zeros_like(dg_ref)

        dg_ref[...] += jnp.sum(dh_v * xr, axis=0, keepdims=True)

    row = pl.BlockSpec((tm, Dm), lambda i: (i, 0))
    vec = pl.BlockSpec((1, Dm), lambda i: (0, 0))
    return pl.pallas_call(
        body, name=name, grid=(T // tm,), in_specs=[row, vec, row, row], out_specs=[row, vec],
        out_shape=[jax.ShapeDtypeStruct((T, Dm), F32), jax.ShapeDtypeStruct((1, Dm), F32)],
        compiler_params=_params(("arbitrary",)))(x, g, dh, dres)


def _final_fwd_bwd(x3, tgt, g):
    T, Dm = x3.shape
    tm = _row_tile(T)

    def body(x_ref, t_ref, g_ref, dx_ref, dg_ref, loss_ref):
        i = pl.program_id(0)
        xv = x_ref[...]
        gg = g_ref[...]
        r = lax.rsqrt(jnp.mean(xv * xv, axis=-1, keepdims=True) + EPS)
        xr = xv * r
        err = xr * gg - t_ref[...]
        dout = err * (1.0 / Dm)
        dyg = dout * gg
        dx_ref[...] = r * (dyg - xr * jnp.mean(dyg * xr, axis=-1, keepdims=True))

        @pl.when(i == 0)
        def _():
            dg_ref[...] = jnp.zeros_like(dg_ref)
            loss_ref[...] = jnp.zeros_like(loss_ref)

        dg_ref[...] += jnp.sum(dout * xr, axis=0, keepdims=True)
        loss_ref[...] += jnp.zeros_like(loss_ref) + (0.5 / Dm) * jnp.sum(err * err)

    row = pl.BlockSpec((tm, Dm), lambda i: (i, 0))
    vec = pl.BlockSpec((1, Dm), lambda i: (0, 0))
    one = pl.BlockSpec((1, 128), lambda i: (0, 0))
    return pl.pallas_call(
        body, name="final_fwd_bwd", grid=(T // tm,), in_specs=[row, row, vec], out_specs=[row, vec, one],
        out_shape=[jax.ShapeDtypeStruct((T, Dm), F32), jax.ShapeDtypeStruct((1, Dm), F32),
                   jax.ShapeDtypeStruct((1, 128), F32)],
        compiler_params=_params(("arbitrary",)))(x3, tgt, g)


def _ffn_up(h, w_up, name):
    T, K = h.shape
    Fh = w_up.shape[1] // 2
    tm = _pick(T, (1024, 512, 256, 128))
    tn = _pick(Fh, (1408, 512, 256, 128))
    nj = Fh // tn

    def body(h_ref, wa_ref, wb_ref, a_ref, b_ref, act_ref):
        hv = h_ref[...]
        for c0 in range(0, tn, MXU_COLS):
            cs = slice(c0, min(c0 + MXU_COLS, tn))
            a = _dot(hv, wa_ref[:, cs])
            b = _dot(hv, wb_ref[:, cs])
            a_ref[:, cs] = a.astype(BF16)
            b_ref[:, cs] = b.astype(BF16)
            act_ref[:, cs] = (a * _sig(a) * b).astype(BF16)

    out = pl.BlockSpec((tm, tn), lambda i, j: (i, j))
    return pl.pallas_call(
        body, name=name, grid=(T // tm, nj),
        in_specs=[pl.BlockSpec((tm, K), lambda i, j: (i, 0)), pl.BlockSpec((K, tn), lambda i, j: (0, j)),
                  pl.BlockSpec((K, tn), lambda i, j: (0, j + nj))],
        out_specs=[out, out, out], out_shape=[jax.ShapeDtypeStruct((T, Fh), BF16)] * 3,
        compiler_params=_params(("parallel", "parallel")))(h, w_up, w_up)


def _ffn_down_bwd(dout, w_down, a, b, name, after=None):
    T, Dm = dout.shape
    Fh = w_down.shape[0]
    tm = _pick(T, (1024, 512, 256, 128))
    tn = _pick(Fh, (1408, 512, 256, 128))

    def body(d_ref, w_ref, a_ref, b_ref, *rest):
        da_ref, db_ref = rest[-2:]
        dv = d_ref[...].astype(BF16)
        for c0 in range(0, tn, MXU_COLS):
            cs = slice(c0, min(c0 + MXU_COLS, tn))
            dact = _dot(dv, w_ref[cs, :], 'nt') * 0.5
            av = a_ref[:, cs].astype(F32)
            bv = b_ref[:, cs].astype(F32)
            s = _sig(av)
            da_ref[:, cs] = (dact * bv * s * (1.0 + av * (1.0 - s))).astype(BF16)
            db_ref[:, cs] = (dact * av * s).astype(BF16)

    tile = pl.BlockSpec((tm, tn), lambda i, j: (i, j))
    extra = () if after is None else (after,)
    return pl.pallas_call(
        body, name=name, grid=(T // tm, Fh // tn),
        in_specs=[pl.BlockSpec((tm, Dm), lambda i, j: (i, 0)), pl.BlockSpec((tn, Dm), lambda i, j: (j, 0)),
                  tile, tile] + [ANY] * len(extra),
        out_specs=[tile, tile], out_shape=[jax.ShapeDtypeStruct((T, Fh), BF16)] * 2,
        compiler_params=_params(("parallel", "parallel")))(dout, w_down, a, b, *extra)


def _dx_rms_bwd(pieces, w, x, g, dres, name, after=None):
    T, Dm = x.shape
    width = pieces[0].shape[1]
    tm = _pick(T, (1024, 512, 256, 128))
    tc = _pick(width, (1408, 1024, 512, 256, 128))
    per = width // tc
    nk = per * len(pieces)
    npc = len(pieces)
    rows = _pick(tm, (256, 128))

    def body(*refs):
        p_refs = refs[:npc]
        w_ref, x_hbm, g_ref, dr_hbm = refs[npc:npc + 4]
        dx_ref, dg_ref, acc_ref, x_buf, dr_buf, sems = refs[-6:]
        i = pl.program_id(0)
        k = pl.program_id(1)
        tile = pl.ds(pl.multiple_of(i * tm, tm), tm)
        fetch_x = pltpu.make_async_copy(x_hbm.at[tile, :], x_buf, sems.at[0])
        fetch_dr = pltpu.make_async_copy(dr_hbm.at[tile, :], dr_buf, sems.at[1])

        @pl.when(k == 0)
        def _():
            fetch_x.start()
            fetch_dr.start()
            acc_ref[...] = jnp.zeros_like(acc_ref)

        @pl.when((i == 0) & (k == 0))
        def _():
            dg_ref[...] = jnp.zeros_like(dg_ref)

        for p in range(npc):
            @pl.when((k >= p * per) & (k < (p + 1) * per))
            def _(p=p):
                acc_ref[...] += _dot(p_refs[p][...], w_ref[...], 'nt')

        @pl.when(k == nk - 1)
        def _():
            fetch_x.wait()
            fetch_dr.wait()

            def chunk(c, carry):
                rs = pl.ds(pl.multiple_of(c * rows, rows), rows)
                dh = acc_ref[rs, :]
                xv = x_buf[rs, :]
                r = lax.rsqrt(jnp.mean(xv * xv, axis=-1, keepdims=True) + EPS)
                xr = xv * r
                dyg = dh * g_ref[...]
                dx_ref[rs, :] = dr_buf[rs, :] + r * (dyg - xr * jnp.mean(dyg * xr, axis=-1, keepdims=True))
                dg_ref[...] += jnp.sum(dh * xr, axis=0, keepdims=True)
                return carry

            lax.fori_loop(0, tm // rows, chunk, 0)

    def piece_spec(p):
        return pl.BlockSpec((tm, tc), lambda i, k: (i, jnp.clip(k - p * per, 0, per - 1)))

    row = pl.BlockSpec((tm, Dm), lambda i, k: (i, 0))
    vec = pl.BlockSpec((1, Dm), lambda i, k: (0, 0))
    extra = () if after is None else (after,)
    return pl.pallas_call(
        body, name=name, grid=(T // tm, nk),
        in_specs=[piece_spec(p) for p in range(npc)] + [pl.BlockSpec((Dm, tc), lambda i, k: (0, k)), ANY, vec, ANY]
        + [ANY] * len(extra),
        out_specs=[row, vec], out_shape=[jax.ShapeDtypeStruct((T, Dm), F32), jax.ShapeDtypeStruct((1, Dm), F32)],
        scratch_shapes=[pltpu.VMEM((tm, Dm), F32), pltpu.VMEM((tm, Dm), F32), pltpu.VMEM((tm, Dm), F32),
                        pltpu.SemaphoreType.DMA((2,))],
        compiler_params=_params(("arbitrary", "arbitrary")))(*pieces, w, x, g, dres, *extra)


def _dw_pieces(a, pieces, name):
    C, M = a.shape
    width = pieces[0].shape[1]
    npc = len(pieces)
    tm = _pick(M, (1024, 512, 256, 128))
    tn = _pick(width, (1408, 1024, 512, 256, 128))
    tc = _pick(C, (1024, 512, 256, 128))
    per = width // tn
    nk = C // tc

    def body(*refs):
        a_ref = refs[0]
        p_refs = refs[1:1 + npc]
        o_ref, acc_ref = refs[-2:]
        j = pl.program_id(1)
        k = pl.program_id(2)

        @pl.when(k == 0)
        def _():
            acc_ref[...] = jnp.zeros_like(acc_ref)

        for p in range(npc):
            @pl.when((j >= p * per) & (j < (p + 1) * per))
            def _(p=p):
                acc_ref[...] += _dot(a_ref[...], p_refs[p][...], 'tn')

        @pl.when(k == nk - 1)
        def _():
            o_ref[...] = acc_ref[...].astype(o_ref.dtype)

    def piece_spec(p):
        return pl.BlockSpec((tc, tn), lambda i, j, k: (k, jnp.clip(j - p * per, 0, per - 1)))

    return pl.pallas_call(
        body, name=name, grid=(M // tm, per * npc, nk),
        in_specs=[pl.BlockSpec((tc, tm), lambda i, j, k: (k, i))] + [piece_spec(p) for p in range(npc)],
        out_specs=pl.BlockSpec((tm, tn), lambda i, j, k: (i, j)),
        out_shape=jax.ShapeDtypeStruct((M, width * npc), BF16), scratch_shapes=[pltpu.VMEM((tm, tn), F32)],
        compiler_params=_params(("parallel", "parallel", "arbitrary")))(a, *pieces)


def _mix_fwd(c_act, o_att, o_mem, proj, b_gate, x1, w_pw, w_o, w_mo, w_out):
    T, Dm = x1.shape
    W = c_act.shape[1]
    tm = _pick(T, (256, 128, 64, 32, 16, 8))

    def body(c_ref, oa_ref, om_ref, gl_ref, bg_ref, x1_ref, wpw_ref, wo_ref, wmo_ref, wout_ref,
             x2_ref, yc_ref, ya_ref, ym_ref):
        yc = _dot(c_ref[...], wpw_ref[...])
        ya = _dot(oa_ref[...], wo_ref[...])
        ym = _dot(om_ref[...], wmo_ref[...])
        g = _sig(gl_ref[...].astype(F32) + bg_ref[...])
        y = g[:, :Dm] * yc + g[:, Dm:2 * Dm] * ya + g[:, 2 * Dm:] * ym
        x2_ref[...] = x1_ref[...] + _dot(y, wout_ref[...])
        yc_ref[...] = yc.astype(BF16)
        ya_ref[...] = ya.astype(BF16)
        ym_ref[...] = ym.astype(BF16)

    rowW = pl.BlockSpec((tm, W), lambda i: (i, 0))
    rowD = pl.BlockSpec((tm, Dm), lambda i: (i, 0))
    full = lambda s: pl.BlockSpec(s, lambda i: (0, 0))
    return pl.pallas_call(
        body, name="mix_fwd", grid=(T // tm,),
        in_specs=[rowW, rowW, rowW, pl.BlockSpec((tm, 3 * Dm), lambda i: (i, 1)), full((1, 3 * Dm)), rowD,
                  full((W, Dm)), full((W, Dm)), full((W, Dm)), full((Dm, Dm))],
        out_specs=[rowD] * 4,
        out_shape=[jax.ShapeDtypeStruct((T, Dm), F32)] + [jax.ShapeDtypeStruct((T, Dm), BF16)] * 3,
        compiler_params=_params(("parallel",)))(c_act, o_att, o_mem, proj, b_gate, x1, w_pw, w_o, w_mo, w_out)


def _mix_bwd(dx2, yc, ya, ym, c_act, o_att, o_mem, proj, b_gate, w_pw, w_o, w_mo, w_out):
    T, Dm = dx2.shape
    W = w_pw.shape[0]
    tm = _pick(T, (256, 128, 64, 32, 16, 8))
    nt = T // tm

    def body(dx_ref, yc_ref, ya_ref, ym_ref, c_ref, oa_ref, om_ref, gl_ref, bg_ref, wpw_ref, wo_ref, wmo_ref,
             wout_ref, dgl_ref, dbg_ref, dc_ref, doa_ref, dom_ref, gpw_ref, go_ref, gmo_ref, gout_ref,
             apw, ao, amo, aout):
        i = pl.program_id(0)

        @pl.when(i == 0)
        def _():
            dbg_ref[...] = jnp.zeros_like(dbg_ref)
            for acc in (apw, ao, amo, aout):
                acc[...] = jnp.zeros_like(acc)

        dxv = dx_ref[...].astype(BF16)
        dy = _dot(dxv, wout_ref[...], 'nt')
        g = _sig(gl_ref[...].astype(F32) + bg_ref[...])
        branches = ((yc_ref, c_ref, wpw_ref, dc_ref, apw), (ya_ref, oa_ref, wo_ref, doa_ref, ao),
                    (ym_ref, om_ref, wmo_ref, dom_ref, amo))
        y = jnp.zeros((tm, Dm), F32)
        for n, (y_ref, in_ref, w_ref, dk_ref, acc) in enumerate(branches):
            gk = g[:, n * Dm:(n + 1) * Dm]
            yk = y_ref[...].astype(F32)
            dyk = dy * gk
            dgl = dyk * yk * (1.0 - gk)
            dgl_ref[:, n * Dm:(n + 1) * Dm] = dgl.astype(BF16)
            dbg_ref[:, n * Dm:(n + 1) * Dm] += jnp.sum(dgl, axis=0, keepdims=True)
            dyk = dyk.astype(BF16)
            dk_ref[...] = _dot(dyk, w_ref[...], 'nt').astype(BF16)
            acc[...] += _dot(in_ref[...], dyk, 'tn')
            y = y + gk * yk
        aout[...] += _dot(y, dxv, 'tn')

        @pl.when(i == nt - 1)
        def _():
            for acc, out in ((apw, gpw_ref), (ao, go_ref), (amo, gmo_ref), (aout, gout_ref)):
                out[...] = acc[...].astype(BF16)

    rowW = pl.BlockSpec((tm, W), lambda i: (i, 0))
    rowD = pl.BlockSpec((tm, Dm), lambda i: (i, 0))
    full = lambda s: pl.BlockSpec(s, lambda i: (0, 0))
    return pl.pallas_call(
        body, name="mix_bwd", grid=(nt,),
        in_specs=[rowD, rowD, rowD, rowD, rowW, rowW, rowW, pl.BlockSpec((tm, 3 * Dm), lambda i: (i, 1)),
                  full((1, 3 * Dm)), full((W, Dm)), full((W, Dm)), full((W, Dm)), full((Dm, Dm))],
        out_specs=[pl.BlockSpec((tm, 3 * Dm), lambda i: (i, 1)), full((1, 3 * Dm)), rowW, rowW, rowW,
                   full((W, Dm)), full((W, Dm)), full((W, Dm)), full((Dm, Dm))],
        out_shape=[jax.ShapeDtypeStruct((T, 6 * Dm), BF16), jax.ShapeDtypeStruct((1, 3 * Dm), F32)]
        + [jax.ShapeDtypeStruct((T, W), BF16)] * 3 + [jax.ShapeDtypeStruct((W, Dm), BF16)] * 3
        + [jax.ShapeDtypeStruct((Dm, Dm), BF16)],
        scratch_shapes=[pltpu.VMEM((W, Dm), F32)] * 3 + [pltpu.VMEM((Dm, Dm), F32)],
        compiler_params=_params(("arbitrary",)))(dx2, yc, ya, ym, c_act, o_att, o_mem, proj, b_gate, w_pw, w_o,
                                                 w_mo, w_out)


def _ln_swish(cv, lg, lb):
    mu = jnp.mean(cv, axis=-1, keepdims=True)
    xc = cv - mu
    r = lax.rsqrt(jnp.mean(xc * xc, axis=-1, keepdims=True) + EPS)
    n = xc * r
    l = n * lg + lb
    return r, n, l


def _shift_copies(src, r0, win, shifts):
    win[...] = src[pl.ds(r0, CONV_TILE + CONV_PAD + 8), :]
    for s in range(8):
        shifts[s] = win[s:s + CONV_TILE + CONV_PAD, :]


def _tap(shifts, d):
    return shifts[d % 8, d - d % 8:d - d % 8 + CONV_TILE, :]


def _conv_fwd(proj3, dw_w, dw_b, ln_g, ln_b):
    Bl, S, _ = proj3.shape
    C, K, TS, PAD = CONV_WIDTH, CONV_KERNEL, CONV_TILE, CONV_PAD
    nt = S // TS

    def body(u_ref, w_ref, b_ref, lg_ref, lb_ref, cv_ref, c_ref, vbuf, win, shifts):
        vbuf[0:PAD, :] = jnp.zeros((PAD, C), F32)
        vbuf[S + PAD:S + PAD + 8, :] = jnp.zeros((8, C), F32)

        def glu(t, carry):
            r0 = pl.multiple_of(t * TS, TS)
            u = u_ref[pl.ds(r0, TS), :].astype(F32)
            vbuf[pl.ds(PAD + r0, TS), :] = u[:, :C] * _sig(u[:, C:])
            return carry

        lax.fori_loop(0, nt, glu, 0)

        def conv(t, carry):
            r0 = pl.multiple_of(t * TS, TS)
            _shift_copies(vbuf, r0, win, shifts)
            acc = jnp.zeros((TS, C), F32)
            for j in range(K):
                acc = acc + w_ref[j:j + 1, :] * _tap(shifts, PAD - (K - 1) + j)
            cv = acc + b_ref[...]
            cv_ref[pl.ds(r0, TS), :] = cv
            _, _, l = _ln_swish(cv, lg_ref[...], lb_ref[...])
            c_ref[pl.ds(r0, TS), :] = (l * _sig(l)).astype(BF16)
            return carry

        lax.fori_loop(0, nt, conv, 0)

    vec = pl.BlockSpec((1, C), lambda b: (0, 0))
    return pl.pallas_call(
        body, name="conv_fwd", grid=(Bl,),
        in_specs=[pl.BlockSpec((None, S, 2 * C), lambda b: (b, 0, 0)), pl.BlockSpec((K, C), lambda b: (0, 0)),
                  vec, vec, vec],
        out_specs=[pl.BlockSpec((None, S, C), lambda b: (b, 0, 0))] * 2,
        out_shape=[jax.ShapeDtypeStruct((Bl, S, C), F32), jax.ShapeDtypeStruct((Bl, S, C), BF16)],
        scratch_shapes=[pltpu.VMEM((S + PAD + 8, C), F32), pltpu.VMEM((TS + PAD + 8, C), F32),
                        pltpu.VMEM((8, TS + PAD, C), F32)],
        compiler_params=_params(("parallel",)))(proj3, dw_w, dw_b, ln_g, ln_b)


def _conv_bwd(proj3, cv, dc, dw_w, ln_g, ln_b, dproj3):
    Bl, S, _ = proj3.shape
    C, K, TS, PAD = CONV_WIDTH, CONV_KERNEL, CONV_TILE, CONV_PAD
    nt = S // TS

    def body(u_ref, cv_ref, dc_ref, w_ref, lg_ref, lb_ref, through_ref, du_ref, dw_ref, db_ref, dlg_ref, dlb_ref,
             vbuf, gbuf, win, shifts, dwacc):
        b = pl.program_id(0)

        @pl.when(b == 0)
        def _():
            dw_ref[...] = jnp.zeros_like(dw_ref)
            db_ref[...] = jnp.zeros_like(db_ref)
            dlg_ref[...] = jnp.zeros_like(dlg_ref)
            dlb_ref[...] = jnp.zeros_like(dlb_ref)

        vbuf[0:PAD, :] = jnp.zeros((PAD, C), F32)
        vbuf[S + PAD:S + PAD + 8, :] = jnp.zeros((8, C), F32)
        gbuf[S:S + PAD + 8, :] = jnp.zeros((PAD + 8, C), F32)
        dwacc[...] = jnp.zeros_like(dwacc)

        def norm_bwd(t, carry):
            r0 = pl.multiple_of(t * TS, TS)
            u = u_ref[pl.ds(r0, TS), :].astype(F32)
            vbuf[pl.ds(PAD + r0, TS), :] = u[:, :C] * _sig(u[:, C:])
            r, n, l = _ln_swish(cv_ref[pl.ds(r0, TS), :], lg_ref[...], lb_ref[...])
            s = _sig(l)
            dl = dc_ref[pl.ds(r0, TS), :].astype(F32) * s * (1.0 + l * (1.0 - s))
            dlg_ref[...] += jnp.sum(dl * n, axis=0, keepdims=True)
            dlb_ref[...] += jnp.sum(dl, axis=0, keepdims=True)
            dn = dl * lg_ref[...]
            dcv = r * (dn - jnp.mean(dn, axis=-1, keepdims=True) - n * jnp.mean(dn * n, axis=-1, keepdims=True))
            gbuf[pl.ds(r0, TS), :] = dcv
            db_ref[...] += jnp.sum(dcv, axis=0, keepdims=True)
            return carry

        lax.fori_loop(0, nt, norm_bwd, 0)

        def conv_bwd(t, carry):
            r0 = pl.multiple_of(t * TS, TS)
            _shift_copies(gbuf, r0, win, shifts)
            dv = jnp.zeros((TS, C), F32)
            for j in range(K):
                dv = dv + w_ref[j:j + 1, :] * _tap(shifts, K - 1 - j)
            u = u_ref[pl.ds(r0, TS), :].astype(F32)
            a, g = u[:, :C], u[:, C:]
            s = _sig(g)
            du_ref[pl.ds(r0, TS), 0:C] = (dv * s).astype(BF16)
            du_ref[pl.ds(r0, TS), C:2 * C] = (dv * a * s * (1.0 - s)).astype(BF16)
            dcv = gbuf[pl.ds(r0, TS), :]
            _shift_copies(vbuf, r0, win, shifts)
            for j in range(K):
                prod = dcv * _tap(shifts, PAD - (K - 1) + j)
                dwacc[j] += jnp.sum(prod.reshape(TS // 8, 8, C), axis=0)
            return carry

        lax.fori_loop(0, nt, conv_bwd, 0)
        dw_ref[...] += jnp.sum(dwacc[...], axis=1)

    vec = pl.BlockSpec((1, C), lambda b: (0, 0))
    seq = lambda w: pl.BlockSpec((None, S, w), lambda b: (b, 0, 0))
    return pl.pallas_call(
        body, name="conv_bwd", grid=(Bl,),
        in_specs=[seq(2 * C), seq(C), seq(C), pl.BlockSpec((K, C), lambda b: (0, 0)), vec, vec, ANY],
        out_specs=[seq(2 * C), pl.BlockSpec((K, C), lambda b: (0, 0)), vec, vec, vec],
        out_shape=[jax.ShapeDtypeStruct(dproj3.shape, BF16), jax.ShapeDtypeStruct((K, C), F32)]
        + [jax.ShapeDtypeStruct((1, C), F32)] * 3,
        input_output_aliases={6: 0},
        scratch_shapes=[pltpu.VMEM((S + PAD + 8, C), F32), pltpu.VMEM((S + PAD + 8, C), F32),
                        pltpu.VMEM((TS + PAD + 8, C), F32), pltpu.VMEM((8, TS + PAD, C), F32),
                        pltpu.VMEM((K, 8, C), F32)],
        compiler_params=_params(("arbitrary",)))(proj3, cv, dc, dw_w, ln_g, ln_b, dproj3)


def _att_bias(rel_bias):
    H = rel_bias.shape[0]
    Wd = KW + QB
    c = jnp.arange(Wd + 1)
    by_offset = rel_bias[:, jnp.clip(KW - c, -(CHUNK - 1), MAX_REL) + (CHUNK - 1)]
    flat = jnp.broadcast_to(by_offset[:, None, :], (H, QB, Wd + 1)).reshape(H, QB * (Wd + 1))
    skew = jnp.pad(flat, ((0, 0), (0, (QB + 1) * Wd - QB * (Wd + 1)))).reshape(H, QB + 1, Wd)[:, :QB, QB:]
    qi = jnp.arange(QB)[:, None]
    kj = jnp.arange(KW)[None, :]
    dchunk = ((KW - QB) + qi) // CHUNK - kj // CHUNK
    band = (dchunk >= 0) & (dchunk <= LEFT_CHUNKS)
    return jnp.where(band[None], skew, MASK_VALUE)


def _head_masks():
    lane = lax.broadcasted_iota(jnp.int32, (1, 128), 1)
    return (lane < 64, lane >= 64)


def _att_probs(qh, k2, bias, valid):
    s = _dot(qh, k2, 'nt') * ATT_SCALE + bias
    s = jnp.where(valid, s, MASK_VALUE)
    e = jnp.exp(s - jnp.max(s, axis=-1, keepdims=True))
    return e * (1.0 / jnp.sum(e, axis=-1, keepdims=True))


def _att_specs(S, q_col):
    nb = S // QB
    q_spec = pl.BlockSpec((None, QB, ATT_WIDTH), lambda b, i: (b, jnp.minimum(i, nb - 1), q_col))

    def kv_spec(col, kb):
        return pl.BlockSpec((None, QB, ATT_WIDTH),
                            lambda b, i: (b, jnp.clip(i - 2 + kb, 0, nb - 1), col))

    return q_spec, [kv_spec(3, kb) for kb in range(3)], [kv_spec(4, kb) for kb in range(3)]


def _att_fwd(proj3, bias):
    Bl, S, _ = proj3.shape
    nb = S // QB
    q_spec, k_specs, v_specs = _att_specs(S, 2)

    def body(q_ref, k0, k1, k2r, v0, v1, v2r, bias_ref, o_ref):
        i = pl.program_id(1)
        masks = _head_masks()
        valid = lax.broadcasted_iota(jnp.int32, (QB, KW), 1) >= (2 - i) * QB
        for pr in range(ATT_HEADS // 2):
            ls = slice(128 * pr, 128 * (pr + 1))
            q2 = q_ref[:, ls]
            k2 = jnp.concatenate([k0[:, ls], k1[:, ls], k2r[:, ls]], axis=0)
            v2 = jnp.concatenate([v0[:, ls], v1[:, ls], v2r[:, ls]], axis=0)
            o2 = jnp.zeros((QB, 128), F32)
            for hh in range(2):
                p = _att_probs(jnp.where(masks[hh], q2, 0), k2, bias_ref[2 * pr + hh], valid)
                o2 = o2 + _dot(p, jnp.where(masks[hh], v2, 0))
            o_ref[:, ls] = o2.astype(BF16)

    return pl.pallas_call(
        body, name="att_fwd", grid=(Bl, nb),
        in_specs=[q_spec] + k_specs + v_specs + [pl.BlockSpec((ATT_HEADS, QB, KW), lambda b, i: (0, 0, 0))],
        out_specs=pl.BlockSpec((None, QB, ATT_WIDTH), lambda b, i: (b, i, 0)),
        out_shape=jax.ShapeDtypeStruct((Bl, S, ATT_WIDTH), BF16),
        compiler_params=_params(("parallel", "arbitrary")))(*([proj3] * 7), bias)


def _att_bwd(proj3, do, bias, dproj3):
    Bl, S, _ = proj3.shape
    nb = S // QB
    q_spec, k_specs, v_specs = _att_specs(S, 2)
    do_spec = pl.BlockSpec((None, QB, ATT_WIDTH), lambda b, i: (b, jnp.minimum(i, nb - 1), 0))
    kv_out = pl.BlockSpec((None, QB, ATT_WIDTH), lambda b, i: (b, jnp.clip(i - 2, 0, nb - 1), 0))
    bias_spec = pl.BlockSpec((ATT_HEADS, QB, KW), lambda b, i: (0, 0, 0))

    def body(q_ref, k0, k1, k2r, v0, v1, v2r, do_ref, bias_ref, through_ref, dq_ref, dk_ref, dv_ref, db_ref,
             dkw, dvw):
        b = pl.program_id(0)
        i = pl.program_id(1)

        @pl.when((b == 0) & (i == 0))
        def _():
            db_ref[...] = jnp.zeros_like(db_ref)

        @pl.when(i == 0)
        def _():
            dkw[...] = jnp.zeros_like(dkw)
            dvw[...] = jnp.zeros_like(dvw)

        @pl.when(i < nb)
        def _():
            masks = _head_masks()
            valid = lax.broadcasted_iota(jnp.int32, (QB, KW), 1) >= (2 - i) * QB
            for pr in range(ATT_HEADS // 2):
                ls = slice(128 * pr, 128 * (pr + 1))
                q2 = q_ref[:, ls]
                do2 = do_ref[:, ls]
                k2 = jnp.concatenate([k0[:, ls], k1[:, ls], k2r[:, ls]], axis=0)
                v2 = jnp.concatenate([v0[:, ls], v1[:, ls], v2r[:, ls]], axis=0)
                dq2 = jnp.zeros((QB, 128), F32)
                dk2 = jnp.zeros((KW, 128), F32)
                dv2 = jnp.zeros((KW, 128), F32)
                for hh in range(2):
                    h = 2 * pr + hh
                    qh = jnp.where(masks[hh], q2, 0)
                    doh = jnp.where(masks[hh], do2, 0)
                    p = _att_probs(qh, k2, bias_ref[h], valid)
                    dp = _dot(doh, v2, 'nt')
                    ds = p * (dp - jnp.sum(p * dp, axis=-1, keepdims=True))
                    db_ref[h] += ds
                    dq2 = dq2 + _dot(ds, jnp.where(masks[hh], k2, 0))
                    dk2 = dk2 + _dot(ds, qh, 'tn')
                    dv2 = dv2 + _dot(p, doh, 'tn')
                dq_ref[:, ls] = (dq2 * ATT_SCALE).astype(BF16)
                dkw[:, ls] += dk2 * ATT_SCALE
                dvw[:, ls] += dv2

        dk_ref[...] = dkw[0:QB, :].astype(BF16)
        dv_ref[...] = dvw[0:QB, :].astype(BF16)
        for buf in (dkw, dvw):
            rest = buf[QB:KW, :]
            buf[0:KW - QB, :] = rest
            buf[KW - QB:KW, :] = jnp.zeros((QB, ATT_WIDTH), F32)

    blk = jax.ShapeDtypeStruct((Bl, S, ATT_WIDTH), BF16)
    return pl.pallas_call(
        body, name="att_bwd", grid=(Bl, nb + 2),
        in_specs=[q_spec] + k_specs + v_specs + [do_spec, bias_spec, ANY],
        out_specs=[q_spec, kv_out, kv_out, bias_spec],
        out_shape=[jax.ShapeDtypeStruct(dproj3.shape, BF16), blk, blk,
                   jax.ShapeDtypeStruct((ATT_HEADS, QB, KW), F32)],
        input_output_aliases={9: 0},
        scratch_shapes=[pltpu.VMEM((KW, ATT_WIDTH), F32), pltpu.VMEM((KW, ATT_WIDTH), F32)],
        compiler_params=_params(("arbitrary", "arbitrary")))(*([proj3] * 7), do, bias, dproj3)


def _rel_bias_grad(dbias):
    H = dbias.shape[0]
    Wd = KW + QB
    padded = jnp.pad(dbias, ((0, 0), (0, 1), (QB, 0)))
    skew = padded.reshape(H, (QB + 1) * Wd)[:, :QB * (Wd + 1)].reshape(H, QB, Wd + 1)[:, :, :Wd]
    c = jnp.arange(Wd)[:, None]
    bins = (jnp.clip(KW - c, -(CHUNK - 1), MAX_REL) + (CHUNK - 1) == jnp.arange(N_REL)[None, :]).astype(F32)

    def body(s_ref, bins_ref, o_ref):
        col = jnp.sum(s_ref[...], axis=1)
        o_ref[...] = jnp.dot(col, bins_ref[...], preferred_element_type=F32, precision=lax.Precision.HIGHEST)

    return pl.pallas_call(
        body, name="rel_bias_grad", grid=(1,),
        in_specs=[pl.BlockSpec((H, QB, Wd), lambda i: (0, 0, 0)), pl.BlockSpec((Wd, N_REL), lambda i: (0, 0))],
        out_specs=pl.BlockSpec((H, N_REL), lambda i: (0, 0)), out_shape=jax.ShapeDtypeStruct((H, N_REL), F32),
        compiler_params=_params(("arbitrary",)))(skew, bins)


MEM_TILE = 512


def _mem_probs(qh, kh):
    s = _dot(qh, kh, 'nt') * MEM_SCALE
    e = jnp.exp(s - jnp.max(s, axis=-1, keepdims=True))
    return e * (1.0 / jnp.sum(e, axis=-1, keepdims=True))


def _mem_fwd(proj3, kv3):
    Bl, S, _ = proj3.shape
    tq = _pick(S, (MEM_TILE, 256))
    hd = MEM_HEAD_DIM

    def body(q_ref, kv_ref, o_ref):
        for h in range(MEM_HEADS):
            p = _mem_probs(q_ref[:, h * hd:(h + 1) * hd], kv_ref[:, h * hd:(h + 1) * hd])
            o_ref[:, h * hd:(h + 1) * hd] = _dot(p, kv_ref[:, MEM_WIDTH + h * hd:MEM_WIDTH + (h + 1) * hd]).astype(BF16)

    return pl.pallas_call(
        body, name="mem_fwd", grid=(Bl, S // tq),
        in_specs=[pl.BlockSpec((None, tq, MEM_WIDTH), lambda b, i: (b, i, 5)),
                  pl.BlockSpec((None, MEM_LEN, 2 * MEM_WIDTH), lambda b, i: (b, 0, 0))],
        out_specs=pl.BlockSpec((None, tq, MEM_WIDTH), lambda b, i: (b, i, 0)),
        out_shape=jax.ShapeDtypeStruct((Bl, S, MEM_WIDTH), BF16),
        compiler_params=_params(("parallel", "parallel")))(proj3, kv3)


def _mem_bwd(proj3, kv3, do, dproj3):
    Bl, S, _ = proj3.shape
    tq = _pick(S, (MEM_TILE, 256))
    hd = MEM_HEAD_DIM

    def body(q_ref, kv_ref, do_ref, through_ref, dq_ref, dkv_ref):
        i = pl.program_id(1)

        @pl.when(i == 0)
        def _():
            dkv_ref[...] = jnp.zeros_like(dkv_ref)

        for h in range(MEM_HEADS):
            ks = slice(h * hd, (h + 1) * hd)
            vs = slice(MEM_WIDTH + h * hd, MEM_WIDTH + (h + 1) * hd)
            qh, kh, vh, doh = q_ref[:, ks], kv_ref[:, ks], kv_ref[:, vs], do_ref[:, ks]
            p = _mem_probs(qh, kh)
            dp = _dot(doh, vh, 'nt')
            ds = p * (dp - jnp.sum(p * dp, axis=-1, keepdims=True))
            dq_ref[:, ks] = (_dot(ds, kh) * MEM_SCALE).astype(BF16)
            dkv_ref[:, ks] += _dot(ds, qh, 'tn') * MEM_SCALE
            dkv_ref[:, vs] += _dot(p, doh, 'tn')

    return pl.pallas_call(
        body, name="mem_bwd", grid=(Bl, S // tq),
        in_specs=[pl.BlockSpec((None, tq, MEM_WIDTH), lambda b, i: (b, i, 5)),
                  pl.BlockSpec((None, MEM_LEN, 2 * MEM_WIDTH), lambda b, i: (b, 0, 0)),
                  pl.BlockSpec((None, tq, MEM_WIDTH), lambda b, i: (b, i, 0)), ANY],
        out_specs=[pl.BlockSpec((None, tq, MEM_WIDTH), lambda b, i: (b, i, 5)),
                   pl.BlockSpec((None, MEM_LEN, 2 * MEM_WIDTH), lambda b, i: (b, 0, 0))],
        out_shape=[jax.ShapeDtypeStruct(dproj3.shape, BF16),
                   jax.ShapeDtypeStruct((Bl, MEM_LEN, 2 * MEM_WIDTH), F32)],
        input_output_aliases={3: 0},
        compiler_params=_params(("parallel", "arbitrary")))(proj3, kv3, do, dproj3)


def _position():
    x, y, c = lax.axis_index("x"), lax.axis_index("y"), lax.axis_index("c")
    return x, y, c, 4 * x + 2 * y + c


def _device(idx):
    return ((idx >> 2) & 1, (idx >> 1) & 1, idx & 1)


def _half_block(ref, axis, shard_shape, k, h):
    R, Cn = shard_shape
    if axis == 1:
        return ref.at[pl.ds(h * (R // 2), R // 2), pl.ds(k * Cn, Cn)]
    return ref.at[pl.ds(k * R + h * (R // 2), R // 2), :]


def _block(ref, axis, shard_shape, k):
    R, Cn = shard_shape
    if axis == 1:
        return ref.at[:, pl.ds(k * Cn, Cn)]
    return ref.at[pl.ds(k * R, R), :]


def _half(ref, h):
    R = ref.shape[0]
    return ref.at[pl.ds(h * (R // 2), R // 2), :]


ANY = pl.BlockSpec(memory_space=pl.ANY)


HBM = pl.BlockSpec(memory_space=pltpu.HBM)
SEM = pl.BlockSpec(memory_space=pltpu.SEMAPHORE)
VMEM_WHOLE = pl.BlockSpec(memory_space=pltpu.VMEM)
EFFECT = pltpu.SideEffectType.DATAFLOW_SIDE_EFFECTING


def _in_hbm(a):
    return pltpu.with_memory_space_constraint(a, pltpu.HBM)


def _split_start(body, name, sources, lands, n_copies):
    n = len(sources)
    out_shape, out_specs = [], []
    for _ in range(n):
        out_shape += [pltpu.SemaphoreType.DMA((n_copies,)), pltpu.SemaphoreType.DMA((n_copies,))]
        out_specs += [SEM, SEM]
    out_shape += [pltpu.HBM(a.shape, a.dtype) for a in list(sources) + list(lands)]
    out_specs += [HBM] * (2 * n)
    out_shape.append(jax.ShapeDtypeStruct((8, 128), F32))
    out_specs.append(VMEM_WHOLE)

    def call_body(*refs):
        srcs, lnds = refs[:n], refs[n:2 * n]
        sems = refs[2 * n:4 * n]
        token = refs[-1]
        body(srcs, lnds, sems[0::2], sems[1::2])
        token[...] = jnp.zeros_like(token)

    res = pl.pallas_call(
        call_body, name=name, in_specs=[HBM] * (2 * n), out_specs=out_specs, out_shape=out_shape,
        input_output_aliases={i: 2 * n + i for i in range(2 * n)},
        compiler_params=pltpu.CompilerParams(has_side_effects=EFFECT))(
            *[_in_hbm(a) for a in list(sources) + list(lands)])
    pairs = [(res[2 * w], res[2 * w + 1], res[2 * n + w], res[3 * n + w]) for w in range(n)]
    return pairs, res[-1]


def _split_wait(body, name, pairs, after):
    n = len(pairs)

    def call_body(*refs):
        srcs, lnds = refs[:n], refs[n:2 * n]
        sems = refs[2 * n:4 * n]
        body(srcs, lnds, sems[0::2], sems[1::2])

    args = [_in_hbm(p[2]) for p in pairs] + [_in_hbm(p[3]) for p in pairs]
    for p in pairs:
        args += [p[0], p[1]]
    res = pl.pallas_call(
        call_body, name=name, in_specs=[HBM] * (2 * n) + [SEM] * (2 * n) + [ANY], out_specs=[HBM] * (2 * n),
        out_shape=[pltpu.HBM(a.shape, a.dtype) for a in args[:2 * n]],
        input_output_aliases={i: i for i in range(2 * n)},
        compiler_params=pltpu.CompilerParams(has_side_effects=EFFECT))(*args, after)
    return res[:n], res[n:]


def _place_block(shard, axis, chip_idx, name, after=None):
    R, Cn = shard.shape
    tr = _pick(R, (256, 176, 128, 64, 32, 16, 8))
    nblk = R // tr

    def body(k_ref, s_ref, *rest):
        rest[-1][...] = s_ref[...]

    if axis == 1:
        out_shape, out_index = (R, 4 * Cn), lambda i, k: (i, k[0])
    else:
        out_shape, out_index = (4 * R, Cn), lambda i, k: (k[0] * nblk + i, 0)
    extra = () if after is None else (after,)
    return pl.pallas_call(
        body, name=name,
        grid_spec=pltpu.PrefetchScalarGridSpec(
            num_scalar_prefetch=1, grid=(nblk,),
            in_specs=[pl.BlockSpec((tr, Cn), lambda i, k: (i, 0))] + [ANY] * len(extra),
            out_specs=pl.BlockSpec((tr, Cn), out_index)),
        out_shape=jax.ShapeDtypeStruct(out_shape, shard.dtype),
        compiler_params=_params(("parallel",)))(chip_idx, shard, *extra)


def _gather_copy(srcs, lnds, send, recv, axes, shapes, w, j, me):
    chip = me >> 1
    return (pltpu.make_async_remote_copy(
        src_ref=srcs[w], dst_ref=_block(lnds[w], axes[w], shapes[w], chip), send_sem=send[w].at[j],
        recv_sem=recv[w].at[j], device_id=_device(me ^ (2 * (j + 1))), device_id_type=MESH),
            pltpu.make_async_remote_copy(
        src_ref=srcs[w], dst_ref=_block(lnds[w], axes[w], shapes[w], chip ^ (j + 1)), send_sem=send[w].at[j],
        recv_sem=recv[w].at[j], device_id=_device(me ^ (2 * (j + 1))), device_id_type=MESH))


def _gather_start(shards, lands, axes, name):
    shapes = [s.shape for s in shards]

    def body(srcs, lnds, send, recv):
        x, y, c, me = _position()
        for w in range(len(shards)):
            for j in range(3):
                _gather_copy(srcs, lnds, send, recv, axes, shapes, w, j, me)[0].start()

    return _split_start(body, name, shards, lands, 3)


def _gather_wait(pairs, axes, after, name):
    shapes = [p[2].shape for p in pairs]

    def body(srcs, lnds, send, recv):
        x, y, c, me = _position()
        for w in range(len(pairs)):
            for j in range(3):
                sent, landed = _gather_copy(srcs, lnds, send, recv, axes, shapes, w, j, me)
                sent.wait_send()
                landed.wait_recv()

    return _split_wait(body, name, pairs, after)[1]


def _shard_shape(grad, axis):
    return (grad.shape[0], grad.shape[1] // 4) if axis == 1 else (grad.shape[0] // 4, grad.shape[1])


def _scatter_copy(srcs, lnds, send, recv, axes, shapes, w, m, me):
    peer = me ^ m
    return pltpu.make_async_remote_copy(
        src_ref=_half_block(srcs[w], axes[w], shapes[w], peer >> 1, peer & 1), dst_ref=lnds[w].at[m - 1],
        send_sem=send[w].at[m - 1], recv_sem=recv[w].at[m - 1], device_id=_device(peer), device_id_type=MESH)


def _scatter_start(grads, axes, name):
    shapes = [_shard_shape(g, a) for g, a in zip(grads, axes)]
    lands = [lax.empty((N_DEV - 1, R // 2, Cn), BF16) for R, Cn in shapes]

    def body(srcs, lnds, send, recv):
        x, y, c, me = _position()
        for w in range(len(grads)):
            for m in range(1, N_DEV):
                _scatter_copy(srcs, lnds, send, recv, axes, shapes, w, m, me).start()

    return _split_start(body, name, grads, lands, N_DEV - 1)


def _scatter_wait(pairs, axes, after):
    shapes = [_shard_shape(p[2], a) for p, a in zip(pairs, axes)]

    def body(srcs, lnds, send, recv):
        x, y, c, me = _position()
        for w in range(len(pairs)):
            for m in range(1, N_DEV):
                cp = _scatter_copy(srcs, lnds, send, recv, axes, shapes, w, m, me)
                cp.wait_send()
                cp.wait_recv()

    return _split_wait(body, "scatter_wait", pairs, after)


def _sum_partials(own, parts, half, name):
    R, Cn = own.shape
    tr = _pick(R, (256, 176, 128, 64, 32, 16, 8))
    nblk = R // tr

    def body(half_ref, own_ref, p_ref, o_ref):
        acc = own_ref[...].astype(F32)
        for d in range(N_DEV - 1):
            acc = acc + p_ref[d].astype(F32)
        o_ref[...] = acc

    return pl.pallas_call(
        body, name=name,
        grid_spec=pltpu.PrefetchScalarGridSpec(
            num_scalar_prefetch=1, grid=(nblk,),
            in_specs=[pl.BlockSpec((tr, Cn), lambda i, hr: (i, 0)),
                      pl.BlockSpec((N_DEV - 1, tr, Cn), lambda i, hr: (0, i, 0))],
            out_specs=pl.BlockSpec((tr, Cn), lambda i, hr: (hr[0] * nblk + i, 0))),
        out_shape=jax.ShapeDtypeStruct((2 * R, Cn), F32),
        compiler_params=_params(("parallel",)))(half, own, parts)


def _exchange_halves(grads):
    n = len(grads)

    def body(*refs):
        outs = refs[n:2 * n]
        send, recv = refs[2 * n:]
        x, y, c, me = _position()

        def copy(w, half):
            rows = _half(outs[w], half)
            return pltpu.make_async_remote_copy(src_ref=rows, dst_ref=rows, send_sem=send.at[w],
                                                recv_sem=recv.at[w], device_id=_device(me ^ 1), device_id_type=MESH)

        for w in range(n):
            copy(w, c).start()
        for w in range(n):
            copy(w, 1 - c).wait_recv()
        for w in range(n):
            copy(w, c).wait_send()

    return pl.pallas_call(
        body, name="exchange_halves", in_specs=[ANY] * n, out_specs=[ANY] * n,
        out_shape=[jax.ShapeDtypeStruct(a.shape, a.dtype) for a in grads],
        input_output_aliases={i: i for i in range(n)},
        scratch_shapes=[pltpu.SemaphoreType.DMA((n,)), pltpu.SemaphoreType.DMA((n,))],
        compiler_params=pltpu.CompilerParams(has_side_effects=True))(*grads)


def _all_reduce_small(vec):
    R, L = vec.shape

    def body(v_ref, o_ref, buf, send, recv):
        x, y, c, me = _position()
        buf[me] = v_ref[...]

        def copy(m, slot):
            return pltpu.make_async_remote_copy(src_ref=v_ref, dst_ref=buf.at[slot], send_sem=send.at[m - 1],
                                                recv_sem=recv.at[m - 1], device_id=_device(me ^ m),
                                                device_id_type=MESH)

        for m in range(1, N_DEV):
            copy(m, me).start()
        for m in range(1, N_DEV):
            copy(m, me ^ m).wait_recv()
        for m in range(1, N_DEV):
            copy(m, me).wait_send()
        acc = buf[0]
        for d in range(1, N_DEV):
            acc = acc + buf[d]
        o_ref[...] = acc

    vm = pl.BlockSpec(memory_space=pltpu.VMEM)
    return pl.pallas_call(
        body, name="all_reduce_small", in_specs=[vm], out_specs=vm, out_shape=jax.ShapeDtypeStruct((R, L), F32),
        scratch_shapes=[pltpu.VMEM((N_DEV, R, L), F32), pltpu.SemaphoreType.DMA((N_DEV - 1,)),
                        pltpu.SemaphoreType.DMA((N_DEV - 1,))],
        compiler_params=pltpu.CompilerParams(has_side_effects=True))(vec)


def _adamw(w, g, m, v, name):
    R, Cn = w.shape
    tr = _pick(R, (256, 176, 128, 64, 40, 32, 16, 8))

    def body(w_ref, g_ref, m_ref, v_ref, d_ref, nm_ref, nv_ref):
        gv = g_ref[...]
        nm = ADAM_B1 * m_ref[...] + (1.0 - ADAM_B1) * gv
        nv = ADAM_B2 * v_ref[...] + (1.0 - ADAM_B2) * (gv * gv)
        m_hat = nm / (1.0 - ADAM_B1 ** ADAM_STEP)
        v_hat = nv / (1.0 - ADAM_B2 ** ADAM_STEP)
        d_ref[...] = -ADAM_LR * (m_hat / (jnp.sqrt(v_hat) + ADAM_EPS) + ADAM_WD * w_ref[...])
        nm_ref[...] = nm
        nv_ref[...] = nv

    spec = pl.BlockSpec((tr, Cn), lambda i: (i, 0))
    return pl.pallas_call(
        body, name=name, grid=(R // tr,), in_specs=[spec] * 4, out_specs=[spec] * 3,
        out_shape=[jax.ShapeDtypeStruct((R, Cn), F32)] * 3, compiler_params=_params(("parallel",)))(w, g, m, v)


def _pack(arrays, rows):
    flat = jnp.concatenate([a.reshape(-1).astype(F32) for a in arrays])
    return jnp.pad(flat, (0, rows * 128 - flat.shape[0])).reshape(rows, 128)


def _unpack(packed, shapes):
    flat = packed.reshape(-1)
    out, off = [], 0
    for s in shapes:
        size = 1
        for d in s:
            size *= d
        out.append(flat[off:off + size].reshape(s))
        off += size
    return out


def _ffn_fwd(x, norm, arrived, tag, after=None):
    h = _rms_fwd(x, norm, f"{tag}_norm", after=after)
    w_up = arrived(f"{tag}_w_up", h)
    a, b, act = _ffn_up(h, w_up, f"{tag}_up")
    w_down = arrived(f"{tag}_w_down", act)
    out = _mm(act, w_down, 'nn', f"{tag}_down", F32, res=x, scale=0.5)
    return out, (h, a, b, act, w_up, w_down)


def _ffn_bwd(dout, x, norm, saved, tag, send):
    h, a, b, act, w_up, w_down = saved
    g_down = _mm(act, dout, 'tn', f"{tag}_down_dw", BF16, scale=0.5)
    token = send([f"{tag}_w_down"], [g_down])
    da, db = _ffn_down_bwd(dout, w_down, a, b, f"{tag}_down_dx", after=token)
    g_up = _dw_pieces(h, [da, db], f"{tag}_up_dw")
    token = send([f"{tag}_w_up"], [g_up])
    return _dx_rms_bwd([da, db], w_up, x, norm, dout, f"{tag}_up_dx", after=token)


def kernel(x, mem, ffn1_norm, ffn1_w_up, ffn1_w_down, mix_norm, mem_norm, w_in, b_gate, conv_dw_w, conv_dw_b, conv_ln_g, conv_ln_b, conv_w_pw, att_rel_bias, att_w_o, mem_w_kv, mem_w_o, w_out, ffn2_norm, ffn2_w_up, ffn2_w_down, final_norm, loss_target, m_ffn1_norm, m_ffn1_w_up, m_ffn1_w_down, m_mix_norm, m_mem_norm, m_w_in, m_b_gate, m_conv_dw_w, m_conv_dw_b, m_conv_ln_g, m_conv_ln_b, m_conv_w_pw, m_att_rel_bias, m_att_w_o, m_mem_w_kv, m_mem_w_o, m_w_out, m_ffn2_norm, m_ffn2_w_up, m_ffn2_w_down, m_final_norm, v_ffn1_norm, v_ffn1_w_up, v_ffn1_w_down, v_mix_norm, v_mem_norm, v_w_in, v_b_gate, v_conv_dw_w, v_conv_dw_b, v_conv_ln_g, v_conv_ln_b, v_conv_w_pw, v_att_rel_bias, v_att_w_o, v_mem_w_kv, v_mem_w_o, v_w_out, v_ffn2_norm, v_ffn2_w_up, v_ffn2_w_down, v_final_norm):
    given = dict(locals())
    wts = {n: given[n] for n in WEIGHTS}
    mom1 = {n: given["m_" + n] for n in WEIGHTS}
    mom2 = {n: given["v_" + n] for n in WEIGHTS}
    Bl, S, Dm = x.shape
    T = Bl * S
    x0 = x.reshape(T, Dm)
    tgt = loss_target.reshape(T, Dm)
    mem2 = mem.reshape(Bl * MEM_LEN, Dm)

    big_names = [n for n, _ in BIG]
    big_axes = [a for _, a in BIG]
    chip = 2 * lax.axis_index("x") + lax.axis_index("y")

    core = lax.axis_index("c")
    axis_of = dict(BIG)

    gather_groups = [['ffn1_w_up'], ['ffn1_w_down'], ['w_in', 'conv_dw_w'],
                     ['mem_w_kv', 'conv_w_pw', 'att_w_o', 'mem_w_o', 'w_out'], ['ffn2_w_up'], ['ffn2_w_down']]
    gather_names = [n for grp in gather_groups for n in grp]
    gather_axes = [axis_of.get(n, 1) for n in gather_names]
    shards = [jnp.pad(conv_dw_w[0], ((0, 1), (0, 0))) if n == 'conv_dw_w' else wts[n][0].astype(BF16)
              for n in gather_names]
    chip_idx = chip.reshape(1).astype(jnp.int32)
    first, first_token = _gather_start(
        shards[:1], [_place_block(shards[0], gather_axes[0], chip_idx, f"place_{gather_names[0]}")],
        gather_axes[:1], "gather_start_first")
    lands = [_place_block(sh, a, chip_idx, f"place_{n}", after=first_token)
             for sh, a, n in zip(shards[1:], gather_axes[1:], gather_names[1:])]
    rest, gather_token = _gather_start(shards[1:], lands, gather_axes[1:], "gather_start_rest")
    in_flight = dict(zip(gather_names, first + rest))
    full = {}

    def arrived(name, after):
        if name not in full:
            grp = next(grp for grp in gather_groups if name in grp)
            lands = _gather_wait([in_flight[n] for n in grp], [axis_of.get(n, 1) for n in grp], after,
                                 f"gather_wait_{grp[0]}")
            full.update(zip(grp, lands))
        return full[name]

    scattering = {}

    def send(names, grads):
        pairs, token = _scatter_start(grads, [axis_of[n] for n in names], f"scatter_start_{names[0]}")
        scattering.update(zip(names, pairs))
        return token

    final_g = final_norm.reshape(1, Dm)
    bias = _att_bias(att_rel_bias[0] + first_token[:1, :1])

    x1, ffn1_saved = _ffn_fwd(x0, ffn1_norm, arrived, "ffn1", after=gather_token)
    h = _rms_fwd(x1, mix_norm, "mix_norm")
    w_in_full = arrived('w_in', h)
    dw_full = full['conv_dw_w'][:CONV_KERNEL]
    proj = _mm(h, w_in_full, 'nn', "w_in", BF16)
    proj3 = proj.reshape(Bl, S, proj.shape[1])
    cv, c_act = _conv_fwd(proj3, dw_full, conv_dw_b, conv_ln_g, conv_ln_b)
    o_att = _att_fwd(proj3, bias)
    mem_h = _rms_fwd(mem2, mem_norm, "mem_norm")
    kv = _mm(mem_h, arrived('mem_w_kv', o_att), 'nn', "mem_kv", BF16)
    kv3 = kv.reshape(Bl, MEM_LEN, 2 * MEM_WIDTH)
    o_mem = _mem_fwd(proj3, kv3)
    c_act2, o_att2, o_mem2 = c_act.reshape(T, -1), o_att.reshape(T, -1), o_mem.reshape(T, -1)
    x2, yc, ya, ym = _mix_fwd(c_act2, o_att2, o_mem2, proj, b_gate, x1, full['conv_w_pw'], full['att_w_o'],
                              full['mem_w_o'], full['w_out'])
    x3, ffn2_saved = _ffn_fwd(x2, ffn2_norm, arrived, "ffn2")
    dx3, g_final, loss_vec = _final_fwd_bwd(x3, tgt, final_g)

    g = {}
    dx2, g['ffn2_norm'] = _ffn_bwd(dx3, x2, ffn2_norm, ffn2_saved, "ffn2", send)
    dgl, g['b_gate'], dc, doa, dom, g_pw, g_o, g_mo, g_out = _mix_bwd(
        dx2, yc, ya, ym, c_act2, o_att2, o_mem2, proj, b_gate, full['conv_w_pw'], full['att_w_o'],
        full['mem_w_o'], full['w_out'])
    token = send(['w_out', 'conv_w_pw', 'att_w_o', 'mem_w_o'], [g_out, g_pw, g_o, g_mo])
    dproj3 = dgl.reshape(Bl, S, -1)
    dproj3, g_dw, g['conv_dw_b'], g['conv_ln_g'], g['conv_ln_b'] = _conv_bwd(
        proj3, cv, dc.reshape(Bl, S, -1), dw_full, conv_ln_g, conv_ln_b, dproj3)
    dproj3, dk, dv, dbias = _att_bwd(proj3, doa.reshape(Bl, S, -1), bias, dproj3)
    g['att_rel_bias'] = _rel_bias_grad(dbias)
    dproj3, dkv = _mem_bwd(proj3, kv3, dom.reshape(Bl, S, -1), dproj3)
    dkv2 = dkv.reshape(Bl * MEM_LEN, 2 * MEM_WIDTH)
    g_kv = _mm(mem_h, dkv2, 'tn', "mem_kv_dw", BF16, after=token)
    dmem_h = _mm(dkv2, full['mem_w_kv'], 'nt', "mem_kv_dx", F32)
    _, g['mem_norm'] = _rms_bwd(mem2, mem_norm, dmem_h, dmem_h, "mem_norm_bwd")
    dkdv = jnp.concatenate([dk.reshape(T, -1), dv.reshape(T, -1)], axis=1)
    dproj = lax.dynamic_update_slice(dproj3.reshape(T, -1), dkdv, (0, 2 * CONV_WIDTH + ATT_WIDTH))
    token = send(['mem_w_kv', 'w_in'], [g_kv, _mm(h, dproj, 'tn', "w_in_dw", BF16)])
    dx1, g['mix_norm'] = _dx_rms_bwd([dproj], w_in_full, x1, mix_norm, dx2, "w_in_dx", after=token)
    dx0, g['ffn1_norm'] = _ffn_bwd(dx1, x0, ffn1_norm, ffn1_saved, "ffn1", send)
    g['final_norm'] = g_final

    sent, landed = _scatter_wait([scattering[n] for n in big_names], big_axes, dx0)
    halves = []
    half_idx = core.reshape(1).astype(jnp.int32)
    for n, a, own_full, parts in zip(big_names, big_axes, sent, landed):
        R, Cn = _shard_shape(own_full, a)
        start = (core * (R // 2), chip * Cn) if a == 1 else (chip * R + core * (R // 2), 0)
        own = lax.dynamic_slice(own_full, start, (R // 2, Cn))
        halves.append(_sum_partials(own, parts, half_idx, f"sum_{n}"))
    for n, sg in zip(big_names, _exchange_halves(halves)):
        g[n] = sg

    small_shapes = [wts[n].shape for n in SMALL]
    n_small = sum(int(wts[n].size) for n in SMALL)
    n_red = n_small + CONV_KERNEL * CONV_WIDTH
    red = _all_reduce_small(_pack([g[n] for n in SMALL] + [g_dw], -(-n_red // 1024) * 8))
    red_list = _unpack(red, small_shapes + [(CONV_KERNEL, CONV_WIDTH)])
    for n, rg in zip(SMALL, red_list[:-1]):
        g[n] = rg
    dw_cols = conv_dw_w.shape[2]
    g['conv_dw_w'] = lax.dynamic_slice(red_list[-1], (0, chip * dw_cols), (CONV_KERNEL, dw_cols))[None]

    delta, new_m, new_v = {}, {}, {}
    for n in big_names:
        g[n] = g[n][None]
        d, nm, nv = _adamw(wts[n][0], g[n][0], mom1[n][0], mom2[n][0], f"adamw_{n}")
        delta[n], new_m[n], new_v[n] = d[None], nm[None], nv[None]
    rest = SMALL + ['conv_dw_w']
    rest_shapes = [wts[n].shape for n in rest]
    rows = -(-sum(int(wts[n].size) for n in rest) // 1024) * 8
    packed = [_pack([src[n] for n in rest], rows) for src in (wts, g, mom1, mom2)]
    for out, res in zip((delta, new_m, new_v), _adamw(*packed, "adamw_small")):
        for n, a in zip(rest, _unpack(res, rest_shapes)):
            out[n] = a

    loss = lax.psum(loss_vec[0, 0], ("x", "y", "c"))
    grad_x = dx0.reshape(Bl, S, Dm)
    return (loss, grad_x, *[g[n] for n in WEIGHTS], *[delta[n] for n in WEIGHTS],
            *[new_m[n] for n in WEIGHTS], *[new_v[n] for n in WEIGHTS])
```

```python
import jax
import jax.numpy as jnp
from jax import lax
from jax.experimental import pallas as pl
from jax.experimental.pallas import tpu as pltpu

F32 = jnp.float32
BF16 = jnp.bfloat16

D_MODEL = 1024
D_FF = 2816
CHUNK = 64
LEFT_CHUNKS = 8
MAX_REL = 128
N_REL = (CHUNK - 1) + MAX_REL + 1
CONV_WIDTH = 512
CONV_KERNEL = 31
ATT_HEADS = 8
ATT_WIDTH = 512
MEM_LEN = 256
MEM_HEADS = 4
MEM_HEAD_DIM = 128
MEM_WIDTH = 512
EPS = 1e-6
MASK_VALUE = -1e30
ATT_SCALE = 64 ** -0.5
MEM_SCALE = 128 ** -0.5

ADAM_LR = 0.001
ADAM_B1 = 0.9
ADAM_B2 = 0.999
ADAM_EPS = 1e-08
ADAM_WD = 0.01
ADAM_STEP = 10

QB = 256
KW = 3 * QB
CONV_PAD = 32
CONV_TILE = 256

VMEM_LIMIT = 56 << 20
MXU_COLS = 256

WEIGHTS = ['ffn1_norm', 'ffn1_w_up', 'ffn1_w_down', 'mix_norm', 'mem_norm', 'w_in', 'b_gate', 'conv_dw_w',
           'conv_dw_b', 'conv_ln_g', 'conv_ln_b', 'conv_w_pw', 'att_rel_bias', 'att_w_o', 'mem_w_kv', 'mem_w_o',
           'w_out', 'ffn2_norm', 'ffn2_w_up', 'ffn2_w_down', 'final_norm']
BIG = [('ffn1_w_up', 1), ('ffn1_w_down', 0), ('w_in', 1), ('conv_w_pw', 1), ('att_w_o', 1), ('mem_w_kv', 0),
       ('mem_w_o', 1), ('w_out', 0), ('ffn2_w_up', 1), ('ffn2_w_down', 0)]
SMALL = ['ffn1_norm', 'mix_norm', 'mem_norm', 'b_gate', 'conv_dw_b', 'conv_ln_g', 'conv_ln_b', 'att_rel_bias',
         'ffn2_norm', 'final_norm']
N_CHIPS = 4
N_DEV = 8
MESH = pl.DeviceIdType.MESH


def _pick(n, cands):
    for c in cands:
        if n % c == 0:
            return c
    return n


def _sig(x):
    return 0.5 * jnp.tanh(0.5 * x) + 0.5


def _params(sem=None, vmem=VMEM_LIMIT):
    return pltpu.CompilerParams(dimension_semantics=sem, vmem_limit_bytes=vmem)


def _dot(a, b, mode='nn'):
    dims = {'nn': (((1,), (0,)), ((), ())), 'nt': (((1,), (1,)), ((), ())), 'tn': (((0,), (0,)), ((), ()))}[mode]
    return lax.dot_general(a.astype(BF16), b.astype(BF16), dims, preferred_element_type=F32)


def _mm(a, b, mode, name, out_dtype, res=None, scale=1.0, after=None):
    if mode == 'nn':
        (M, C), (_, N) = a.shape, b.shape
    elif mode == 'nt':
        (M, C), (N, _) = a.shape, b.shape
    else:
        (C, M), (_, N) = a.shape, b.shape
    tm = _pick(M, (1024, 1408, 512, 256, 128))
    tn = _pick(N, (1024, 1408, 512, 256, 128))
    tc = C if C <= 2816 else _pick(C, (2048, 1024, 1408, 512, 256, 128))
    nk = C // tc
    if mode == 'nn':
        a_spec = pl.BlockSpec((tm, tc), lambda i, j, k: (i, k))
        b_spec = pl.BlockSpec((tc, tn), lambda i, j, k: (k, j))
    elif mode == 'nt':
        a_spec = pl.BlockSpec((tm, tc), lambda i, j, k: (i, k))
        b_spec = pl.BlockSpec((tn, tc), lambda i, j, k: (j, k))
    else:
        a_spec = pl.BlockSpec((tc, tm), lambda i, j, k: (k, i))
        b_spec = pl.BlockSpec((tc, tn), lambda i, j, k: (k, j))
    o_spec = pl.BlockSpec((tm, tn), lambda i, j, k: (i, j))
    has_res = res is not None
    has_after = after is not None

    def body(*refs):
        a_ref, b_ref = refs[:2]
        r_ref = refs[2] if has_res else None
        o_ref, acc_ref = refs[-2:]
        k = pl.program_id(2)

        def finish(acc):
            if scale != 1.0:
                acc = acc * scale
            if r_ref is not None:
                acc = r_ref[...] + acc
            o_ref[...] = acc.astype(o_ref.dtype)

        if nk == 1:
            finish(_dot(a_ref[...], b_ref[...], mode))
        else:
            @pl.when(k == 0)
            def _():
                acc_ref[...] = jnp.zeros_like(acc_ref)

            acc_ref[...] += _dot(a_ref[...], b_ref[...], mode)

            @pl.when(k == nk - 1)
            def _():
                finish(acc_ref[...])

    in_specs = [a_spec, b_spec] + ([o_spec] if has_res else []) + ([ANY] if has_after else [])
    args = (a, b) + ((res,) if has_res else ()) + ((after,) if has_after else ())
    acc_shape = (tm, tn) if nk > 1 else (8, 128)
    return pl.pallas_call(
        body, name=name, grid=(M // tm, N // tn, nk), in_specs=in_specs, out_specs=o_spec,
        out_shape=jax.ShapeDtypeStruct((M, N), out_dtype), scratch_shapes=[pltpu.VMEM(acc_shape, F32)],
        compiler_params=_params(("parallel", "parallel", "arbitrary")))(*args)


def _row_tile(T):
    return _pick(T, (512, 256, 128, 64, 32, 16, 8))


def _rms_fwd(x, g, name, after=None):
    T, Dm = x.shape
    tm = _row_tile(T)

    def body(x_ref, g_ref, *rest):
        o_ref = rest[-1]
        xv = x_ref[...]
        r = lax.rsqrt(jnp.mean(xv * xv, axis=-1, keepdims=True) + EPS)
        o_ref[...] = ((xv * r) * g_ref[...]).astype(o_ref.dtype)

    extra = () if after is None else (after,)
    return pl.pallas_call(
        body, name=name, grid=(T // tm,),
        in_specs=[pl.BlockSpec((tm, Dm), lambda i: (i, 0)), pl.BlockSpec((1, Dm), lambda i: (0, 0))]
        + [ANY] * len(extra),
        out_specs=pl.BlockSpec((tm, Dm), lambda i: (i, 0)), out_shape=jax.ShapeDtypeStruct((T, Dm), BF16),
        compiler_params=_params(("parallel",)))(x, g, *extra)


def _rms_bwd(x, g, dh, dres, name):
    T, Dm = x.shape
    tm = _row_tile(T)

    def body(x_ref, g_ref, dh_ref, dr_ref, dx_ref, dg_ref):
        i = pl.program_id(0)
        xv = x_ref[...]
        r = lax.rsqrt(jnp.mean(xv * xv, axis=-1, keepdims=True) + EPS)
        xr = xv * r
        dh_v = dh_ref[...].astype(F32)
        dyg = dh_v * g_ref[...]
        dx = r * (dyg - xr * jnp.mean(dyg * xr, axis=-1, keepdims=True))
        dx_ref[...] = dr_ref[...] + dx

        @pl.when(i == 0)
        def _():
            dg_ref[...] = jnp.zeros_like(dg_ref)

        dg_ref[...] += jnp.sum(dh_v * xr, axis=0, keepdims=True)

    row = pl.BlockSpec((tm, Dm), lambda i: (i, 0))
    vec = pl.BlockSpec((1, Dm), lambda i: (0, 0))
    return pl.pallas_call(
        body, name=name, grid=(T // tm,), in_specs=[row, vec, row, row], out_specs=[row, vec],
        out_shape=[jax.ShapeDtypeStruct((T, Dm), F32), jax.ShapeDtypeStruct((1, Dm), F32)],
        compiler_params=_params(("arbitrary",)))(x, g, dh, dres)


def _final_fwd_bwd(x3, tgt, g):
    T, Dm = x3.shape
    tm = _row_tile(T)

    def body(x_ref, t_ref, g_ref, dx_ref, dg_ref, loss_ref):
        i = pl.program_id(0)
        xv = x_ref[...]
        gg = g_ref[...]
        r = lax.rsqrt(jnp.mean(xv * xv, axis=-1, keepdims=True) + EPS)
        xr = xv * r
        err = xr * gg - t_ref[...]
        dout = err * (1.0 / Dm)
        dyg = dout * gg
        dx_ref[...] = r * (dyg - xr * jnp.mean(dyg * xr, axis=-1, keepdims=True))

        @pl.when(i == 0)
        def _():
            dg_ref[...] = jnp.zeros_like(dg_ref)
            loss_ref[...] = jnp.zeros_like(loss_ref)

        dg_ref[...] += jnp.sum(dout * xr, axis=0, keepdims=True)
        loss_ref[...] += jnp.zeros_like(loss_ref) + (0.5 / Dm) * jnp.sum(err * err)

    row = pl.BlockSpec((tm, Dm), lambda i: (i, 0))
    vec = pl.BlockSpec((1, Dm), lambda i: (0, 0))
    one = pl.BlockSpec((1, 128), lambda i: (0, 0))
    return pl.pallas_call(
        body, name="final_fwd_bwd", grid=(T // tm,), in_specs=[row, row, vec], out_specs=[row, vec, one],
        out_shape=[jax.ShapeDtypeStruct((T, Dm), F32), jax.ShapeDtypeStruct((1, Dm), F32),
                   jax.ShapeDtypeStruct((1, 128), F32)],
        compiler_params=_params(("arbitrary",)))(x3, tgt, g)


def _ffn_up(h, w_up, name):
    T, K = h.shape
    Fh = w_up.shape[1] // 2
    tm = _pick(T, (1024, 512, 256, 128))
    tn = _pick(Fh, (1408, 512, 256, 128))
    nj = Fh // tn

    def body(h_ref, wa_ref, wb_ref, a_ref, b_ref, act_ref):
        hv = h_ref[...]
        for c0 in range(0, tn, MXU_COLS):
            cs = slice(c0, min(c0 + MXU_COLS, tn))
            a = _dot(hv, wa_ref[:, cs])
            b = _dot(hv, wb_ref[:, cs])
            a_ref[:, cs] = a.astype(BF16)
            b_ref[:, cs] = b.astype(BF16)
            act_ref[:, cs] = (a * _sig(a) * b).astype(BF16)

    out = pl.BlockSpec((tm, tn), lambda i, j: (i, j))
    return pl.pallas_call(
        body, name=name, grid=(T // tm, nj),
        in_specs=[pl.BlockSpec((tm, K), lambda i, j: (i, 0)), pl.BlockSpec((K, tn), lambda i, j: (0, j)),
                  pl.BlockSpec((K, tn), lambda i, j: (0, j + nj))],
        out_specs=[out, out, out], out_shape=[jax.ShapeDtypeStruct((T, Fh), BF16)] * 3,
        compiler_params=_params(("parallel", "parallel")))(h, w_up, w_up)


def _ffn_down_bwd(dout, w_down, a, b, name, after=None):
    T, Dm = dout.shape
    Fh = w_down.shape[0]
    tm = _pick(T, (1024, 512, 256, 128))
    tn = _pick(Fh, (1408, 512, 256, 128))

    def body(d_ref, w_ref, a_ref, b_ref, *rest):
        da_ref, db_ref = rest[-2:]
        dv = d_ref[...].astype(BF16)
        for c0 in range(0, tn, MXU_COLS):
            cs = slice(c0, min(c0 + MXU_COLS, tn))
            dact = _dot(dv, w_ref[cs, :], 'nt') * 0.5
            av = a_ref[:, cs].astype(F32)
            bv = b_ref[:, cs].astype(F32)
            s = _sig(av)
            da_ref[:, cs] = (dact * bv * s * (1.0 + av * (1.0 - s))).astype(BF16)
            db_ref[:, cs] = (dact * av * s).astype(BF16)

    tile = pl.BlockSpec((tm, tn), lambda i, j: (i, j))
    extra = () if after is None else (after,)
    return pl.pallas_call(
        body, name=name, grid=(T // tm, Fh // tn),
        in_specs=[pl.BlockSpec((tm, Dm), lambda i, j: (i, 0)), pl.BlockSpec((tn, Dm), lambda i, j: (j, 0)),
                  tile, tile] + [ANY] * len(extra),
        out_specs=[tile, tile], out_shape=[jax.ShapeDtypeStruct((T, Fh), BF16)] * 2,
        compiler_params=_params(("parallel", "parallel")))(dout, w_down, a, b, *extra)


def _dx_rms_bwd(pieces, w, x, g, dres, name, after=None):
    T, Dm = x.shape
    width = pieces[0].shape[1]
    tm = _pick(T, (1024, 512, 256, 128))
    tc = _pick(width, (1408, 2048, 1024, 512, 256, 128))
    per = width // tc
    nk = per * len(pieces)
    npc = len(pieces)
    rows = _pick(tm, (256, 128))

    def body(*refs):
        p_refs = refs[:npc]
        w_ref, x_hbm, g_ref, dr_hbm = refs[npc:npc + 4]
        dx_ref, dg_ref, acc_ref, x_buf, dr_buf, sems = refs[-6:]
        i = pl.program_id(0)
        k = pl.program_id(1)
        tile = pl.ds(pl.multiple_of(i * tm, tm), tm)
        fetch_x = pltpu.make_async_copy(x_hbm.at[tile, :], x_buf, sems.at[0])
        fetch_dr = pltpu.make_async_copy(dr_hbm.at[tile, :], dr_buf, sems.at[1])

        @pl.when(k == 0)
        def _():
            fetch_x.start()
            fetch_dr.start()
            acc_ref[...] = jnp.zeros_like(acc_ref)

        @pl.when((i == 0) & (k == 0))
        def _():
            dg_ref[...] = jnp.zeros_like(dg_ref)

        for p in range(npc):
            @pl.when((k >= p * per) & (k < (p + 1) * per))
            def _(p=p):
                acc_ref[...] += _dot(p_refs[p][...], w_ref[...], 'nt')

        @pl.when(k == nk - 1)
        def _():
            fetch_x.wait()
            fetch_dr.wait()

            def chunk(c, carry):
                rs = pl.ds(pl.multiple_of(c * rows, rows), rows)
                dh = acc_ref[rs, :]
                xv = x_buf[rs, :]
                r = lax.rsqrt(jnp.mean(xv * xv, axis=-1, keepdims=True) + EPS)
                xr = xv * r
                dyg = dh * g_ref[...]
                dx_ref[rs, :] = dr_buf[rs, :] + r * (dyg - xr * jnp.mean(dyg * xr, axis=-1, keepdims=True))
                dg_ref[...] += jnp.sum(dh * xr, axis=0, keepdims=True)
                return carry

            lax.fori_loop(0, tm // rows, chunk, 0)

    def piece_spec(p):
        return pl.BlockSpec((tm, tc), lambda i, k: (i, jnp.clip(k - p * per, 0, per - 1)))

    row = pl.BlockSpec((tm, Dm), lambda i, k: (i, 0))
    vec = pl.BlockSpec((1, Dm), lambda i, k: (0, 0))
    extra = () if after is None else (after,)
    return pl.pallas_call(
        body, name=name, grid=(T // tm, nk),
        in_specs=[piece_spec(p) for p in range(npc)] + [pl.BlockSpec((Dm, tc), lambda i, k: (0, k)), ANY, vec, ANY]
        + [ANY] * len(extra),
        out_specs=[row, vec], out_shape=[jax.ShapeDtypeStruct((T, Dm), F32), jax.ShapeDtypeStruct((1, Dm), F32)],
        scratch_shapes=[pltpu.VMEM((tm, Dm), F32), pltpu.VMEM((tm, Dm), F32), pltpu.VMEM((tm, Dm), F32),
                        pltpu.SemaphoreType.DMA((2,))],
        compiler_params=_params(("arbitrary", "arbitrary")))(*pieces, w, x, g, dres, *extra)


def _dw_pieces(a, pieces, name):
    C, M = a.shape
    width = pieces[0].shape[1]
    npc = len(pieces)
    tm = _pick(M, (1024, 512, 256, 128))
    tn = _pick(width, (1408, 1024, 512, 256, 128))
    tc = _pick(C, (1024, 512, 256, 128))
    per = width // tn
    nk = C // tc

    def body(*refs):
        a_ref = refs[0]
        p_refs = refs[1:1 + npc]
        o_ref, acc_ref = refs[-2:]
        j = pl.program_id(1)
        k = pl.program_id(2)

        @pl.when(k == 0)
        def _():
            acc_ref[...] = jnp.zeros_like(acc_ref)

        for p in range(npc):
            @pl.when((j >= p * per) & (j < (p + 1) * per))
            def _(p=p):
                acc_ref[...] += _dot(a_ref[...], p_refs[p][...], 'tn')

        @pl.when(k == nk - 1)
        def _():
            o_ref[...] = acc_ref[...].astype(o_ref.dtype)

    def piece_spec(p):
        return pl.BlockSpec((tc, tn), lambda i, j, k: (k, jnp.clip(j - p * per, 0, per - 1)))

    return pl.pallas_call(
        body, name=name, grid=(M // tm, per * npc, nk),
        in_specs=[pl.BlockSpec((tc, tm), lambda i, j, k: (k, i))] + [piece_spec(p) for p in range(npc)],
        out_specs=pl.BlockSpec((tm, tn), lambda i, j, k: (i, j)),
        out_shape=jax.ShapeDtypeStruct((M, width * npc), BF16), scratch_shapes=[pltpu.VMEM((tm, tn), F32)],
        compiler_params=_params(("parallel", "parallel", "arbitrary")))(a, *pieces)


def _mix_fwd(c_act, o_att, o_mem, proj, b_gate, x1, w_pw, w_o, w_mo, w_out):
    T, Dm = x1.shape
    W = c_act.shape[1]
    tm = _pick(T, (256, 128, 64, 32, 16, 8))

    def body(c_ref, oa_ref, om_ref, gl_ref, bg_ref, x1_ref, wpw_ref, wo_ref, wmo_ref, wout_ref,
             x2_ref, yc_ref, ya_ref, ym_ref):
        yc = _dot(c_ref[...], wpw_ref[...])
        ya = _dot(oa_ref[...], wo_ref[...])
        ym = _dot(om_ref[...], wmo_ref[...])
        g = _sig(gl_ref[...].astype(F32) + bg_ref[...])
        y = g[:, :Dm] * yc + g[:, Dm:2 * Dm] * ya + g[:, 2 * Dm:] * ym
        x2_ref[...] = x1_ref[...] + _dot(y, wout_ref[...])
        yc_ref[...] = yc.astype(BF16)
        ya_ref[...] = ya.astype(BF16)
        ym_ref[...] = ym.astype(BF16)

    rowW = pl.BlockSpec((tm, W), lambda i: (i, 0))
    rowD = pl.BlockSpec((tm, Dm), lambda i: (i, 0))
    full = lambda s: pl.BlockSpec(s, lambda i: (0, 0))
    return pl.pallas_call(
        body, name="mix_fwd", grid=(T // tm,),
        in_specs=[rowW, rowW, rowW, pl.BlockSpec((tm, 3 * Dm), lambda i: (i, 1)), full((1, 3 * Dm)), rowD,
                  full((W, Dm)), full((W, Dm)), full((W, Dm)), full((Dm, Dm))],
        out_specs=[rowD] * 4,
        out_shape=[jax.ShapeDtypeStruct((T, Dm), F32)] + [jax.ShapeDtypeStruct((T, Dm), BF16)] * 3,
        compiler_params=_params(("parallel",)))(c_act, o_att, o_mem, proj, b_gate, x1, w_pw, w_o, w_mo, w_out)


def _mix_bwd(dx2, yc, ya, ym, c_act, o_att, o_mem, proj, b_gate, w_pw, w_o, w_mo, w_out):
    T, Dm = dx2.shape
    W = w_pw.shape[0]
    tm = _pick(T, (256, 128, 64, 32, 16, 8))
    nt = T // tm

    def body(dx_ref, yc_ref, ya_ref, ym_ref, c_ref, oa_ref, om_ref, gl_ref, bg_ref, wpw_ref, wo_ref, wmo_ref,
             wout_ref, dgl_ref, dbg_ref, dc_ref, doa_ref, dom_ref, gpw_ref, go_ref, gmo_ref, gout_ref,
             apw, ao, amo, aout):
        i = pl.program_id(0)

        @pl.when(i == 0)
        def _():
            dbg_ref[...] = jnp.zeros_like(dbg_ref)
            for acc in (apw, ao, amo, aout):
                acc[...] = jnp.zeros_like(acc)

        dxv = dx_ref[...].astype(BF16)
        dy = _dot(dxv, wout_ref[...], 'nt')
        g = _sig(gl_ref[...].astype(F32) + bg_ref[...])
        branches = ((yc_ref, c_ref, wpw_ref, dc_ref, apw), (ya_ref, oa_ref, wo_ref, doa_ref, ao),
                    (ym_ref, om_ref, wmo_ref, dom_ref, amo))
        y = jnp.zeros((tm, Dm), F32)
        for n, (y_ref, in_ref, w_ref, dk_ref, acc) in enumerate(branches):
            gk = g[:, n * Dm:(n + 1) * Dm]
            yk = y_ref[...].astype(F32)
            dyk = dy * gk
            dgl = dyk * yk * (1.0 - gk)
            dgl_ref[:, n * Dm:(n + 1) * Dm] = dgl.astype(BF16)
            dbg_ref[:, n * Dm:(n + 1) * Dm] += jnp.sum(dgl, axis=0, keepdims=True)
            dyk = dyk.astype(BF16)
            dk_ref[...] = _dot(dyk, w_ref[...], 'nt').astype(BF16)
            acc[...] += _dot(in_ref[...], dyk, 'tn')
            y = y + gk * yk
        aout[...] += _dot(y, dxv, 'tn')

        @pl.when(i == nt - 1)
        def _():
            for acc, out in ((apw, gpw_ref), (ao, go_ref), (amo, gmo_ref), (aout, gout_ref)):
                out[...] = acc[...].astype(BF16)

    rowW = pl.BlockSpec((tm, W), lambda i: (i, 0))
    rowD = pl.BlockSpec((tm, Dm), lambda i: (i, 0))
    full = lambda s: pl.BlockSpec(s, lambda i: (0, 0))
    return pl.pallas_call(
        body, name="mix_bwd", grid=(nt,),
        in_specs=[rowD, rowD, rowD, rowD, rowW, rowW, rowW, pl.BlockSpec((tm, 3 * Dm), lambda i: (i, 1)),
                  full((1, 3 * Dm)), full((W, Dm)), full((W, Dm)), full((W, Dm)), full((Dm, Dm))],
        out_specs=[pl.BlockSpec((tm, 3 * Dm), lambda i: (i, 1)), full((1, 3 * Dm)), rowW, rowW, rowW,
                   full((W, Dm)), full((W, Dm)), full((W, Dm)), full((Dm, Dm))],
        out_shape=[jax.ShapeDtypeStruct((T, 6 * Dm), BF16), jax.ShapeDtypeStruct((1, 3 * Dm), F32)]
        + [jax.ShapeDtypeStruct((T, W), BF16)] * 3 + [jax.ShapeDtypeStruct((W, Dm), BF16)] * 3
        + [jax.ShapeDtypeStruct((Dm, Dm), BF16)],
        scratch_shapes=[pltpu.VMEM((W, Dm), F32)] * 3 + [pltpu.VMEM((Dm, Dm), F32)],
        compiler_params=_params(("arbitrary",)))(dx2, yc, ya, ym, c_act, o_att, o_mem, proj, b_gate, w_pw, w_o,
                                                 w_mo, w_out)


def _ln_swish(cv, lg, lb):
    mu = jnp.mean(cv, axis=-1, keepdims=True)
    xc = cv - mu
    r = lax.rsqrt(jnp.mean(xc * xc, axis=-1, keepdims=True) + EPS)
    n = xc * r
    l = n * lg + lb
    return r, n, l


def _shift_copies(src, r0, win, shifts):
    win[...] = src[pl.ds(r0, CONV_TILE + CONV_PAD + 8), :]
    for s in range(8):
        shifts[s] = win[s:s + CONV_TILE + CONV_PAD, :]


def _tap(shifts, d):
    return shifts[d % 8, d - d % 8:d - d % 8 + CONV_TILE, :]


def _conv_fwd(proj3, dw_w, dw_b, ln_g, ln_b):
    Bl, S, _ = proj3.shape
    C, K, TS, PAD = CONV_WIDTH, CONV_KERNEL, CONV_TILE, CONV_PAD
    nt = S // TS

    def body(u_ref, w_ref, b_ref, lg_ref, lb_ref, cv_ref, c_ref, vbuf, win, shifts):
        vbuf[0:PAD, :] = jnp.zeros((PAD, C), F32)
        vbuf[S + PAD:S + PAD + 8, :] = jnp.zeros((8, C), F32)

        def glu(t, carry):
            r0 = pl.multiple_of(t * TS, TS)
            u = u_ref[pl.ds(r0, TS), :].astype(F32)
            vbuf[pl.ds(PAD + r0, TS), :] = u[:, :C] * _sig(u[:, C:])
            return carry

        lax.fori_loop(0, nt, glu, 0)

        def conv(t, carry):
            r0 = pl.multiple_of(t * TS, TS)
            _shift_copies(vbuf, r0, win, shifts)
            acc = jnp.zeros((TS, C), F32)
            for j in range(K):
                acc = acc + w_ref[j:j + 1, :] * _tap(shifts, PAD - (K - 1) + j)
            cv = acc + b_ref[...]
            cv_ref[pl.ds(r0, TS), :] = cv
            _, _, l = _ln_swish(cv, lg_ref[...], lb_ref[...])
            c_ref[pl.ds(r0, TS), :] = (l * _sig(l)).astype(BF16)
            return carry

        lax.fori_loop(0, nt, conv, 0)

    vec = pl.BlockSpec((1, C), lambda b: (0, 0))
    return pl.pallas_call(
        body, name="conv_fwd", grid=(Bl,),
        in_specs=[pl.BlockSpec((None, S, 2 * C), lambda b: (b, 0, 0)), pl.BlockSpec((K, C), lambda b: (0, 0)),
                  vec, vec, vec],
        out_specs=[pl.BlockSpec((None, S, C), lambda b: (b, 0, 0))] * 2,
        out_shape=[jax.ShapeDtypeStruct((Bl, S, C), F32), jax.ShapeDtypeStruct((Bl, S, C), BF16)],
        scratch_shapes=[pltpu.VMEM((S + PAD + 8, C), F32), pltpu.VMEM((TS + PAD + 8, C), F32),
                        pltpu.VMEM((8, TS + PAD, C), F32)],
        compiler_params=_params(("parallel",)))(proj3, dw_w, dw_b, ln_g, ln_b)


def _conv_bwd(proj3, cv, dc, dw_w, ln_g, ln_b, dproj3):
    Bl, S, _ = proj3.shape
    C, K, TS, PAD = CONV_WIDTH, CONV_KERNEL, CONV_TILE, CONV_PAD
    nt = S // TS

    def body(u_ref, cv_ref, dc_ref, w_ref, lg_ref, lb_ref, through_ref, du_ref, dw_ref, db_ref, dlg_ref, dlb_ref,
             vbuf, gbuf, win, shifts, dwacc):
        b = pl.program_id(0)

        @pl.when(b == 0)
        def _():
            dw_ref[...] = jnp.zeros_like(dw_ref)
            db_ref[...] = jnp.zeros_like(db_ref)
            dlg_ref[...] = jnp.zeros_like(dlg_ref)
            dlb_ref[...] = jnp.zeros_like(dlb_ref)

        vbuf[0:PAD, :] = jnp.zeros((PAD, C), F32)
        vbuf[S + PAD:S + PAD + 8, :] = jnp.zeros((8, C), F32)
        gbuf[S:S + PAD + 8, :] = jnp.zeros((PAD + 8, C), F32)
        dwacc[...] = jnp.zeros_like(dwacc)

        def norm_bwd(t, carry):
            r0 = pl.multiple_of(t * TS, TS)
            u = u_ref[pl.ds(r0, TS), :].astype(F32)
            vbuf[pl.ds(PAD + r0, TS), :] = u[:, :C] * _sig(u[:, C:])
            r, n, l = _ln_swish(cv_ref[pl.ds(r0, TS), :], lg_ref[...], lb_ref[...])
            s = _sig(l)
            dl = dc_ref[pl.ds(r0, TS), :].astype(F32) * s * (1.0 + l * (1.0 - s))
            dlg_ref[...] += jnp.sum(dl * n, axis=0, keepdims=True)
            dlb_ref[...] += jnp.sum(dl, axis=0, keepdims=True)
            dn = dl * lg_ref[...]
            dcv = r * (dn - jnp.mean(dn, axis=-1, keepdims=True) - n * jnp.mean(dn * n, axis=-1, keepdims=True))
            gbuf[pl.ds(r0, TS), :] = dcv
            db_ref[...] += jnp.sum(dcv, axis=0, keepdims=True)
            return carry

        lax.fori_loop(0, nt, norm_bwd, 0)

        def conv_bwd(t, carry):
            r0 = pl.multiple_of(t * TS, TS)
            _shift_copies(gbuf, r0, win, shifts)
            dv = jnp.zeros((TS, C), F32)
            for j in range(K):
                dv = dv + w_ref[j:j + 1, :] * _tap(shifts, K - 1 - j)
            u = u_ref[pl.ds(r0, TS), :].astype(F32)
            a, g = u[:, :C], u[:, C:]
            s = _sig(g)
            du_ref[pl.ds(r0, TS), 0:C] = (dv * s).astype(BF16)
            du_ref[pl.ds(r0, TS), C:2 * C] = (dv * a * s * (1.0 - s)).astype(BF16)
            dcv = gbuf[pl.ds(r0, TS), :]
            _shift_copies(vbuf, r0, win, shifts)
            for j in range(K):
                prod = dcv * _tap(shifts, PAD - (K - 1) + j)
                dwacc[j] += jnp.sum(prod.reshape(TS // 8, 8, C), axis=0)
            return carry

        lax.fori_loop(0, nt, conv_bwd, 0)
        dw_ref[...] += jnp.sum(dwacc[...], axis=1)

    vec = pl.BlockSpec((1, C), lambda b: (0, 0))
    seq = lambda w: pl.BlockSpec((None, S, w), lambda b: (b, 0, 0))
    return pl.pallas_call(
        body, name="conv_bwd", grid=(Bl,),
        in_specs=[seq(2 * C), seq(C), seq(C), pl.BlockSpec((K, C), lambda b: (0, 0)), vec, vec, ANY],
        out_specs=[seq(2 * C), pl.BlockSpec((K, C), lambda b: (0, 0)), vec, vec, vec],
        out_shape=[jax.ShapeDtypeStruct(dproj3.shape, BF16), jax.ShapeDtypeStruct((K, C), F32)]
        + [jax.ShapeDtypeStruct((1, C), F32)] * 3,
        input_output_aliases={6: 0},
        scratch_shapes=[pltpu.VMEM((S + PAD + 8, C), F32), pltpu.VMEM((S + PAD + 8, C), F32),
                        pltpu.VMEM((TS + PAD + 8, C), F32), pltpu.VMEM((8, TS + PAD, C), F32),
                        pltpu.VMEM((K, 8, C), F32)],
        compiler_params=_params(("arbitrary",)))(proj3, cv, dc, dw_w, ln_g, ln_b, dproj3)


def _att_bias(rel_bias):
    H = rel_bias.shape[0]
    Wd = KW + QB
    c = jnp.arange(Wd + 1)
    by_offset = rel_bias[:, jnp.clip(KW - c, -(CHUNK - 1), MAX_REL) + (CHUNK - 1)]
    flat = jnp.broadcast_to(by_offset[:, None, :], (H, QB, Wd + 1)).reshape(H, QB * (Wd + 1))
    skew = jnp.pad(flat, ((0, 0), (0, (QB + 1) * Wd - QB * (Wd + 1)))).reshape(H, QB + 1, Wd)[:, :QB, QB:]
    qi = jnp.arange(QB)[:, None]
    kj = jnp.arange(KW)[None, :]
    dchunk = ((KW - QB) + qi) // CHUNK - kj // CHUNK
    band = (dchunk >= 0) & (dchunk <= LEFT_CHUNKS)
    return jnp.where(band[None], skew, MASK_VALUE)


def _head_masks():
    lane = lax.broadcasted_iota(jnp.int32, (1, 128), 1)
    return (lane < 64, lane >= 64)


def _att_probs(qh, k2, bias, valid):
    s = _dot(qh, k2, 'nt') * ATT_SCALE + bias
    s = jnp.where(valid, s, MASK_VALUE)
    e = jnp.exp(s - jnp.max(s, axis=-1, keepdims=True))
    return e * (1.0 / jnp.sum(e, axis=-1, keepdims=True))


def _att_specs(S, q_col):
    nb = S // QB
    q_spec = pl.BlockSpec((None, QB, ATT_WIDTH), lambda b, i: (b, jnp.minimum(i, nb - 1), q_col))

    def kv_spec(col, kb):
        return pl.BlockSpec((None, QB, ATT_WIDTH),
                            lambda b, i: (b, jnp.clip(i - 2 + kb, 0, nb - 1), col))

    return q_spec, [kv_spec(3, kb) for kb in range(3)], [kv_spec(4, kb) for kb in range(3)]


def _att_fwd(proj3, bias):
    Bl, S, _ = proj3.shape
    nb = S // QB
    q_spec, k_specs, v_specs = _att_specs(S, 2)

    def body(q_ref, k0, k1, k2r, v0, v1, v2r, bias_ref, o_ref):
        i = pl.program_id(1)
        masks = _head_masks()
        valid = lax.broadcasted_iota(jnp.int32, (QB, KW), 1) >= (2 - i) * QB
        for pr in range(ATT_HEADS // 2):
            ls = slice(128 * pr, 128 * (pr + 1))
            q2 = q_ref[:, ls]
            k2 = jnp.concatenate([k0[:, ls], k1[:, ls], k2r[:, ls]], axis=0)
            v2 = jnp.concatenate([v0[:, ls], v1[:, ls], v2r[:, ls]], axis=0)
            o2 = jnp.zeros((QB, 128), F32)
            for hh in range(2):
                p = _att_probs(jnp.where(masks[hh], q2, 0), k2, bias_ref[2 * pr + hh], valid)
                o2 = o2 + _dot(p, jnp.where(masks[hh], v2, 0))
            o_ref[:, ls] = o2.astype(BF16)

    return pl.pallas_call(
        body, name="att_fwd", grid=(Bl, nb),
        in_specs=[q_spec] + k_specs + v_specs + [pl.BlockSpec((ATT_HEADS, QB, KW), lambda b, i: (0, 0, 0))],
        out_specs=pl.BlockSpec((None, QB, ATT_WIDTH), lambda b, i: (b, i, 0)),
        out_shape=jax.ShapeDtypeStruct((Bl, S, ATT_WIDTH), BF16),
        compiler_params=_params(("parallel", "arbitrary")))(*([proj3] * 7), bias)


def _att_bwd(proj3, do, bias, dproj3):
    Bl, S, _ = proj3.shape
    nb = S // QB
    q_spec, k_specs, v_specs = _att_specs(S, 2)
    do_spec = pl.BlockSpec((None, QB, ATT_WIDTH), lambda b, i: (b, jnp.minimum(i, nb - 1), 0))
    kv_out = pl.BlockSpec((None, QB, ATT_WIDTH), lambda b, i: (b, jnp.clip(i - 2, 0, nb - 1), 0))
    bias_spec = pl.BlockSpec((ATT_HEADS, QB, KW), lambda b, i: (0, 0, 0))

    def body(q_ref, k0, k1, k2r, v0, v1, v2r, do_ref, bias_ref, through_ref, dq_ref, dk_ref, dv_ref, db_ref,
             dkw, dvw):
        b = pl.program_id(0)
        i = pl.program_id(1)

        @pl.when((b == 0) & (i == 0))
        def _():
            db_ref[...] = jnp.zeros_like(db_ref)

        @pl.when(i == 0)
        def _():
            dkw[...] = jnp.zeros_like(dkw)
            dvw[...] = jnp.zeros_like(dvw)

        @pl.when(i < nb)
        def _():
            masks = _head_masks()
            valid = lax.broadcasted_iota(jnp.int32, (QB, KW), 1) >= (2 - i) * QB
            for pr in range(ATT_HEADS // 2):
                ls = slice(128 * pr, 128 * (pr + 1))
                q2 = q_ref[:, ls]
                do2 = do_ref[:, ls]
                k2 = jnp.concatenate([k0[:, ls], k1[:, ls], k2r[:, ls]], axis=0)
                v2 = jnp.concatenate([v0[:, ls], v1[:, ls], v2r[:, ls]], axis=0)
                dq2 = jnp.zeros((QB, 128), F32)
                dk2 = jnp.zeros((KW, 128), F32)
                dv2 = jnp.zeros((KW, 128), F32)
                for hh in range(2):
                    h = 2 * pr + hh
                    qh = jnp.where(masks[hh], q2, 0)
                    doh = jnp.where(masks[hh], do2, 0)
                    p = _att_probs(qh, k2, bias_ref[h], valid)
                    dp = _dot(doh, v2, 'nt')
                    ds = p * (dp - jnp.sum(p * dp, axis=-1, keepdims=True))
                    db_ref[h] += ds
                    dq2 = dq2 + _dot(ds, jnp.where(masks[hh], k2, 0))
                    dk2 = dk2 + _dot(ds, qh, 'tn')
                    dv2 = dv2 + _dot(p, doh, 'tn')
                dq_ref[:, ls] = (dq2 * ATT_SCALE).astype(BF16)
                dkw[:, ls] += dk2 * ATT_SCALE
                dvw[:, ls] += dv2

        dk_ref[...] = dkw[0:QB, :].astype(BF16)
        dv_ref[...] = dvw[0:QB, :].astype(BF16)
        for buf in (dkw, dvw):
            rest = buf[QB:KW, :]
            buf[0:KW - QB, :] = rest
            buf[KW - QB:KW, :] = jnp.zeros((QB, ATT_WIDTH), F32)

    blk = jax.ShapeDtypeStruct((Bl, S, ATT_WIDTH), BF16)
    return pl.pallas_call(
        body, name="att_bwd", grid=(Bl, nb + 2),
        in_specs=[q_spec] + k_specs + v_specs + [do_spec, bias_spec, ANY],
        out_specs=[q_spec, kv_out, kv_out, bias_spec],
        out_shape=[jax.ShapeDtypeStruct(dproj3.shape, BF16), blk, blk,
                   jax.ShapeDtypeStruct((ATT_HEADS, QB, KW), F32)],
        input_output_aliases={9: 0},
        scratch_shapes=[pltpu.VMEM((KW, ATT_WIDTH), F32), pltpu.VMEM((KW, ATT_WIDTH), F32)],
        compiler_params=_params(("arbitrary", "arbitrary")))(*([proj3] * 7), do, bias, dproj3)


def _rel_bias_grad(dbias):
    H = dbias.shape[0]
    Wd = KW + QB
    padded = jnp.pad(dbias, ((0, 0), (0, 1), (QB, 0)))
    skew = padded.reshape(H, (QB + 1) * Wd)[:, :QB * (Wd + 1)].reshape(H, QB, Wd + 1)[:, :, :Wd]
    c = jnp.arange(Wd)[:, None]
    bins = (jnp.clip(KW - c, -(CHUNK - 1), MAX_REL) + (CHUNK - 1) == jnp.arange(N_REL)[None, :]).astype(F32)

    def body(s_ref, bins_ref, o_ref):
        col = jnp.sum(s_ref[...], axis=1)
        o_ref[...] = jnp.dot(col, bins_ref[...], preferred_element_type=F32, precision=lax.Precision.HIGHEST)

    return pl.pallas_call(
        body, name="rel_bias_grad", grid=(1,),
        in_specs=[pl.BlockSpec((H, QB, Wd), lambda i: (0, 0, 0)), pl.BlockSpec((Wd, N_REL), lambda i: (0, 0))],
        out_specs=pl.BlockSpec((H, N_REL), lambda i: (0, 0)), out_shape=jax.ShapeDtypeStruct((H, N_REL), F32),
        compiler_params=_params(("arbitrary",)))(skew, bins)


MEM_TILE = 512


def _mem_probs(qh, kh):
    s = _dot(qh, kh, 'nt') * MEM_SCALE
    e = jnp.exp(s - jnp.max(s, axis=-1, keepdims=True))
    return e * (1.0 / jnp.sum(e, axis=-1, keepdims=True))


def _mem_fwd(proj3, kv3):
    Bl, S, _ = proj3.shape
    tq = _pick(S, (MEM_TILE, 256))
    hd = MEM_HEAD_DIM

    def body(q_ref, kv_ref, o_ref):
        for h in range(MEM_HEADS):
            p = _mem_probs(q_ref[:, h * hd:(h + 1) * hd], kv_ref[:, h * hd:(h + 1) * hd])
            o_ref[:, h * hd:(h + 1) * hd] = _dot(p, kv_ref[:, MEM_WIDTH + h * hd:MEM_WIDTH + (h + 1) * hd]).astype(BF16)

    return pl.pallas_call(
        body, name="mem_fwd", grid=(Bl, S // tq),
        in_specs=[pl.BlockSpec((None, tq, MEM_WIDTH), lambda b, i: (b, i, 5)),
                  pl.BlockSpec((None, MEM_LEN, 2 * MEM_WIDTH), lambda b, i: (b, 0, 0))],
        out_specs=pl.BlockSpec((None, tq, MEM_WIDTH), lambda b, i: (b, i, 0)),
        out_shape=jax.ShapeDtypeStruct((Bl, S, MEM_WIDTH), BF16),
        compiler_params=_params(("parallel", "parallel")))(proj3, kv3)


def _mem_bwd(proj3, kv3, do, dproj3):
    Bl, S, _ = proj3.shape
    tq = _pick(S, (MEM_TILE, 256))
    hd = MEM_HEAD_DIM

    def body(q_ref, kv_ref, do_ref, through_ref, dq_ref, dkv_ref):
        i = pl.program_id(1)

        @pl.when(i == 0)
        def _():
            dkv_ref[...] = jnp.zeros_like(dkv_ref)

        for h in range(MEM_HEADS):
            ks = slice(h * hd, (h + 1) * hd)
            vs = slice(MEM_WIDTH + h * hd, MEM_WIDTH + (h + 1) * hd)
            qh, kh, vh, doh = q_ref[:, ks], kv_ref[:, ks], kv_ref[:, vs], do_ref[:, ks]
            p = _mem_probs(qh, kh)
            dp = _dot(doh, vh, 'nt')
            ds = p * (dp - jnp.sum(p * dp, axis=-1, keepdims=True))
            dq_ref[:, ks] = (_dot(ds, kh) * MEM_SCALE).astype(BF16)
            dkv_ref[:, ks] += _dot(ds, qh, 'tn') * MEM_SCALE
            dkv_ref[:, vs] += _dot(p, doh, 'tn')

    return pl.pallas_call(
        body, name="mem_bwd", grid=(Bl, S // tq),
        in_specs=[pl.BlockSpec((None, tq, MEM_WIDTH), lambda b, i: (b, i, 5)),
                  pl.BlockSpec((None, MEM_LEN, 2 * MEM_WIDTH), lambda b, i: (b, 0, 0)),
                  pl.BlockSpec((None, tq, MEM_WIDTH), lambda b, i: (b, i, 0)), ANY],
        out_specs=[pl.BlockSpec((None, tq, MEM_WIDTH), lambda b, i: (b, i, 5)),
                   pl.BlockSpec((None, MEM_LEN, 2 * MEM_WIDTH), lambda b, i: (b, 0, 0))],
        out_shape=[jax.ShapeDtypeStruct(dproj3.shape, BF16),
                   jax.ShapeDtypeStruct((Bl, MEM_LEN, 2 * MEM_WIDTH), F32)],
        input_output_aliases={3: 0},
        compiler_params=_params(("parallel", "arbitrary")))(proj3, kv3, do, dproj3)


def _position():
    x, y, c = lax.axis_index("x"), lax.axis_index("y"), lax.axis_index("c")
    return x, y, c, 4 * x + 2 * y + c


def _device(idx):
    return ((idx >> 2) & 1, (idx >> 1) & 1, idx & 1)


def _half_block(ref, axis, shard_shape, k, h):
    R, Cn = shard_shape
    if axis == 1:
        return ref.at[pl.ds(h * (R // 2), R // 2), pl.ds(k * Cn, Cn)]
    return ref.at[pl.ds(k * R + h * (R // 2), R // 2), :]


def _block(ref, axis, shard_shape, k):
    R, Cn = shard_shape
    if axis == 1:
        return ref.at[:, pl.ds(k * Cn, Cn)]
    return ref.at[pl.ds(k * R, R), :]


def _half(ref, h):
    R = ref.shape[0]
    return ref.at[pl.ds(h * (R // 2), R // 2), :]


ANY = pl.BlockSpec(memory_space=pl.ANY)


HBM = pl.BlockSpec(memory_space=pltpu.HBM)
SEM = pl.BlockSpec(memory_space=pltpu.SEMAPHORE)
VMEM_WHOLE = pl.BlockSpec(memory_space=pltpu.VMEM)
EFFECT = pltpu.SideEffectType.DATAFLOW_SIDE_EFFECTING


def _in_hbm(a):
    return pltpu.with_memory_space_constraint(a, pltpu.HBM)


def _split_start(body, name, sources, lands, n_copies):
    n = len(sources)
    out_shape, out_specs = [], []
    for _ in range(n):
        out_shape += [pltpu.SemaphoreType.DMA((n_copies,)), pltpu.SemaphoreType.DMA((n_copies,))]
        out_specs += [SEM, SEM]
    out_shape += [pltpu.HBM(a.shape, a.dtype) for a in list(sources) + list(lands)]
    out_specs += [HBM] * (2 * n)
    out_shape.append(jax.ShapeDtypeStruct((8, 128), F32))
    out_specs.append(VMEM_WHOLE)

    def call_body(*refs):
        srcs, lnds = refs[:n], refs[n:2 * n]
        sems = refs[2 * n:4 * n]
        token = refs[-1]
        body(srcs, lnds, sems[0::2], sems[1::2])
        token[...] = jnp.zeros_like(token)

    res = pl.pallas_call(
        call_body, name=name, in_specs=[HBM] * (2 * n), out_specs=out_specs, out_shape=out_shape,
        input_output_aliases={i: 2 * n + i for i in range(2 * n)},
        compiler_params=pltpu.CompilerParams(has_side_effects=EFFECT))(
            *[_in_hbm(a) for a in list(sources) + list(lands)])
    pairs = [(res[2 * w], res[2 * w + 1], res[2 * n + w], res[3 * n + w]) for w in range(n)]
    return pairs, res[-1]


def _split_wait(body, name, pairs, after):
    n = len(pairs)

    def call_body(*refs):
        srcs, lnds = refs[:n], refs[n:2 * n]
        sems = refs[2 * n:4 * n]
        body(srcs, lnds, sems[0::2], sems[1::2])

    args = [_in_hbm(p[2]) for p in pairs] + [_in_hbm(p[3]) for p in pairs]
    for p in pairs:
        args += [p[0], p[1]]
    res = pl.pallas_call(
        call_body, name=name, in_specs=[HBM] * (2 * n) + [SEM] * (2 * n) + [ANY], out_specs=[HBM] * (2 * n),
        out_shape=[pltpu.HBM(a.shape, a.dtype) for a in args[:2 * n]],
        input_output_aliases={i: i for i in range(2 * n)},
        compiler_params=pltpu.CompilerParams(has_side_effects=EFFECT))(*args, after)
    return res[:n], res[n:]


def _place_block(shard, axis, chip_idx, name, after=None):
    R, Cn = shard.shape
    tr = _pick(R, (256, 176, 128, 64, 32, 16, 8))
    nblk = R // tr

    def body(k_ref, s_ref, *rest):
        rest[-1][...] = s_ref[...]

    if axis == 1:
        out_shape, out_index = (R, 4 * Cn), lambda i, k: (i, k[0])
    else:
        out_shape, out_index = (4 * R, Cn), lambda i, k: (k[0] * nblk + i, 0)
    extra = () if after is None else (after,)
    return pl.pallas_call(
        body, name=name,
        grid_spec=pltpu.PrefetchScalarGridSpec(
            num_scalar_prefetch=1, grid=(nblk,),
            in_specs=[pl.BlockSpec((tr, Cn), lambda i, k: (i, 0))] + [ANY] * len(extra),
            out_specs=pl.BlockSpec((tr, Cn), out_index)),
        out_shape=jax.ShapeDtypeStruct(out_shape, shard.dtype),
        compiler_params=_params(("parallel",)))(chip_idx, shard, *extra)


def _gather_copy(srcs, lnds, send, recv, axes, shapes, w, j, me):
    chip = me >> 1
    return (pltpu.make_async_remote_copy(
        src_ref=srcs[w], dst_ref=_block(lnds[w], axes[w], shapes[w], chip), send_sem=send[w].at[j],
        recv_sem=recv[w].at[j], device_id=_device(me ^ (2 * (j + 1))), device_id_type=MESH),
            pltpu.make_async_remote_copy(
        src_ref=srcs[w], dst_ref=_block(lnds[w], axes[w], shapes[w], chip ^ (j + 1)), send_sem=send[w].at[j],
        recv_sem=recv[w].at[j], device_id=_device(me ^ (2 * (j + 1))), device_id_type=MESH))


def _gather_start(shards, lands, axes, name):
    shapes = [s.shape for s in shards]

    def body(srcs, lnds, send, recv):
        x, y, c, me = _position()
        for w in range(len(shards)):
            for j in range(3):
                _gather_copy(srcs, lnds, send, recv, axes, shapes, w, j, me)[0].start()

    return _split_start(body, name, shards, lands, 3)


def _gather_wait(pairs, axes, after, name):
    shapes = [p[2].shape for p in pairs]

    def body(srcs, lnds, send, recv):
        x, y, c, me = _position()
        for w in range(len(pairs)):
            for j in range(3):
                sent, landed = _gather_copy(srcs, lnds, send, recv, axes, shapes, w, j, me)
                sent.wait_send()
                landed.wait_recv()

    return _split_wait(body, name, pairs, after)[1]


def _shard_shape(grad, axis):
    return (grad.shape[0], grad.shape[1] // 4) if axis == 1 else (grad.shape[0] // 4, grad.shape[1])


def _scatter_copy(srcs, lnds, send, recv, axes, shapes, w, m, me):
    peer = me ^ m
    return pltpu.make_async_remote_copy(
        src_ref=_half_block(srcs[w], axes[w], shapes[w], peer >> 1, peer & 1), dst_ref=lnds[w].at[m - 1],
        send_sem=send[w].at[m - 1], recv_sem=recv[w].at[m - 1], device_id=_device(peer), device_id_type=MESH)


def _scatter_start(grads, axes, name):
    shapes = [_shard_shape(g, a) for g, a in zip(grads, axes)]
    lands = [lax.empty((N_DEV - 1, R // 2, Cn), BF16) for R, Cn in shapes]

    def body(srcs, lnds, send, recv):
        x, y, c, me = _position()
        for w in range(len(grads)):
            for m in range(1, N_DEV):
                _scatter_copy(srcs, lnds, send, recv, axes, shapes, w, m, me).start()

    return _split_start(body, name, grads, lands, N_DEV - 1)


def _scatter_wait(pairs, axes, after):
    shapes = [_shard_shape(p[2], a) for p, a in zip(pairs, axes)]

    def body(srcs, lnds, send, recv):
        x, y, c, me = _position()
        for w in range(len(pairs)):
            for m in range(1, N_DEV):
                cp = _scatter_copy(srcs, lnds, send, recv, axes, shapes, w, m, me)
                cp.wait_send()
                cp.wait_recv()

    return _split_wait(body, "scatter_wait", pairs, after)


def _sum_partials(own, parts, half, name):
    R, Cn = own.shape
    tr = _pick(R, (256, 176, 128, 64, 32, 16, 8))
    nblk = R // tr

    def body(half_ref, own_ref, p_ref, o_ref):
        acc = own_ref[...].astype(F32)
        for d in range(N_DEV - 1):
            acc = acc + p_ref[d].astype(F32)
        o_ref[...] = acc

    return pl.pallas_call(
        body, name=name,
        grid_spec=pltpu.PrefetchScalarGridSpec(
            num_scalar_prefetch=1, grid=(nblk,),
            in_specs=[pl.BlockSpec((tr, Cn), lambda i, hr: (i, 0)),
                      pl.BlockSpec((N_DEV - 1, tr, Cn), lambda i, hr: (0, i, 0))],
            out_specs=pl.BlockSpec((tr, Cn), lambda i, hr: (hr[0] * nblk + i, 0))),
        out_shape=jax.ShapeDtypeStruct((2 * R, Cn), F32),
        compiler_params=_params(("parallel",)))(half, own, parts)


def _exchange_halves(grads):
    n = len(grads)

    def body(*refs):
        outs = refs[n:2 * n]
        send, recv = refs[2 * n:]
        x, y, c, me = _position()

        def copy(w, half):
            rows = _half(outs[w], half)
            return pltpu.make_async_remote_copy(src_ref=rows, dst_ref=rows, send_sem=send.at[w],
                                                recv_sem=recv.at[w], device_id=_device(me ^ 1), device_id_type=MESH)

        for w in range(n):
            copy(w, c).start()
        for w in range(n):
            copy(w, 1 - c).wait_recv()
        for w in range(n):
            copy(w, c).wait_send()

    return pl.pallas_call(
        body, name="exchange_halves", in_specs=[ANY] * n, out_specs=[ANY] * n,
        out_shape=[jax.ShapeDtypeStruct(a.shape, a.dtype) for a in grads],
        input_output_aliases={i: i for i in range(n)},
        scratch_shapes=[pltpu.SemaphoreType.DMA((n,)), pltpu.SemaphoreType.DMA((n,))],
        compiler_params=pltpu.CompilerParams(has_side_effects=True))(*grads)


def _all_reduce_small(vec):
    R, L = vec.shape

    def body(v_ref, o_ref, buf, send, recv):
        x, y, c, me = _position()
        buf[me] = v_ref[...]

        def copy(m, slot):
            return pltpu.make_async_remote_copy(src_ref=v_ref, dst_ref=buf.at[slot], send_sem=send.at[m - 1],
                                                recv_sem=recv.at[m - 1], device_id=_device(me ^ m),
                                                device_id_type=MESH)

        for m in range(1, N_DEV):
            copy(m, me).start()
        for m in range(1, N_DEV):
            copy(m, me ^ m).wait_recv()
        for m in range(1, N_DEV):
            copy(m, me).wait_send()
        acc = buf[0]
        for d in range(1, N_DEV):
            acc = acc + buf[d]
        o_ref[...] = acc

    vm = pl.BlockSpec(memory_space=pltpu.VMEM)
    return pl.pallas_call(
        body, name="all_reduce_small", in_specs=[vm], out_specs=vm, out_shape=jax.ShapeDtypeStruct((R, L), F32),
        scratch_shapes=[pltpu.VMEM((N_DEV, R, L), F32), pltpu.SemaphoreType.DMA((N_DEV - 1,)),
                        pltpu.SemaphoreType.DMA((N_DEV - 1,))],
        compiler_params=pltpu.CompilerParams(has_side_effects=True))(vec)


def _adamw(w, g, m, v, name):
    R, Cn = w.shape
    tr = _pick(R, (256, 176, 128, 64, 40, 32, 16, 8))

    def body(w_ref, g_ref, m_ref, v_ref, d_ref, nm_ref, nv_ref):
        gv = g_ref[...]
        nm = ADAM_B1 * m_ref[...] + (1.0 - ADAM_B1) * gv
        nv = ADAM_B2 * v_ref[...] + (1.0 - ADAM_B2) * (gv * gv)
        m_hat = nm / (1.0 - ADAM_B1 ** ADAM_STEP)
        v_hat = nv / (1.0 - ADAM_B2 ** ADAM_STEP)
        d_ref[...] = -ADAM_LR * (m_hat / (jnp.sqrt(v_hat) + ADAM_EPS) + ADAM_WD * w_ref[...])
        nm_ref[...] = nm
        nv_ref[...] = nv

    spec = pl.BlockSpec((tr, Cn), lambda i: (i, 0))
    return pl.pallas_call(
        body, name=name, grid=(R // tr,), in_specs=[spec] * 4, out_specs=[spec] * 3,
        out_shape=[jax.ShapeDtypeStruct((R, Cn), F32)] * 3, compiler_params=_params(("parallel",)))(w, g, m, v)


def _pack(arrays, rows):
    flat = jnp.concatenate([a.reshape(-1).astype(F32) for a in arrays])
    return jnp.pad(flat, (0, rows * 128 - flat.shape[0])).reshape(rows, 128)


def _unpack(packed, shapes):
    flat = packed.reshape(-1)
    out, off = [], 0
    for s in shapes:
        size = 1
        for d in s:
            size *= d
        out.append(flat[off:off + size].reshape(s))
        off += size
    return out


def _ffn_fwd(x, norm, arrived, tag, after=None):
    h = _rms_fwd(x, norm, f"{tag}_norm", after=after)
    w_up = arrived(f"{tag}_w_up", h)
    a, b, act = _ffn_up(h, w_up, f"{tag}_up")
    w_down = arrived(f"{tag}_w_down", act)
    out = _mm(act, w_down, 'nn', f"{tag}_down", F32, res=x, scale=0.5)
    return out, (h, a, b, act, w_up, w_down)


def _ffn_bwd(dout, x, norm, saved, tag, send):
    h, a, b, act, w_up, w_down = saved
    g_down = _mm(act, dout, 'tn', f"{tag}_down_dw", BF16, scale=0.5)
    token = send([f"{tag}_w_down"], [g_down])
    da, db = _ffn_down_bwd(dout, w_down, a, b, f"{tag}_down_dx", after=token)
    g_up = _dw_pieces(h, [da, db], f"{tag}_up_dw")
    token = send([f"{tag}_w_up"], [g_up])
    return _dx_rms_bwd([da, db], w_up, x, norm, dout, f"{tag}_up_dx", after=token)


def kernel(x, mem, ffn1_norm, ffn1_w_up, ffn1_w_down, mix_norm, mem_norm, w_in, b_gate, conv_dw_w, conv_dw_b, conv_ln_g, conv_ln_b, conv_w_pw, att_rel_bias, att_w_o, mem_w_kv, mem_w_o, w_out, ffn2_norm, ffn2_w_up, ffn2_w_down, final_norm, loss_target, m_ffn1_norm, m_ffn1_w_up, m_ffn1_w_down, m_mix_norm, m_mem_norm, m_w_in, m_b_gate, m_conv_dw_w, m_conv_dw_b, m_conv_ln_g, m_conv_ln_b, m_conv_w_pw, m_att_rel_bias, m_att_w_o, m_mem_w_kv, m_mem_w_o, m_w_out, m_ffn2_norm, m_ffn2_w_up, m_ffn2_w_down, m_final_norm, v_ffn1_norm, v_ffn1_w_up, v_ffn1_w_down, v_mix_norm, v_mem_norm, v_w_in, v_b_gate, v_conv_dw_w, v_conv_dw_b, v_conv_ln_g, v_conv_ln_b, v_conv_w_pw, v_att_rel_bias, v_att_w_o, v_mem_w_kv, v_mem_w_o, v_w_out, v_ffn2_norm, v_ffn2_w_up, v_ffn2_w_down, v_final_norm):
    given = dict(locals())
    wts = {n: given[n] for n in WEIGHTS}
    mom1 = {n: given["m_" + n] for n in WEIGHTS}
    mom2 = {n: given["v_" + n] for n in WEIGHTS}
    Bl, S, Dm = x.shape
    T = Bl * S
    x0 = x.reshape(T, Dm)
    tgt = loss_target.reshape(T, Dm)
    mem2 = mem.reshape(Bl * MEM_LEN, Dm)

    big_names = [n for n, _ in BIG]
    big_axes = [a for _, a in BIG]
    chip = 2 * lax.axis_index("x") + lax.axis_index("y")

    core = lax.axis_index("c")
    axis_of = dict(BIG)

    gather_groups = [['ffn1_w_up'], ['ffn1_w_down'], ['w_in', 'conv_dw_w'],
                     ['mem_w_kv', 'conv_w_pw', 'att_w_o', 'mem_w_o', 'w_out'], ['ffn2_w_up'], ['ffn2_w_down']]
    gather_names = [n for grp in gather_groups for n in grp]
    gather_axes = [axis_of.get(n, 1) for n in gather_names]
    shards = [jnp.pad(conv_dw_w[0], ((0, 1), (0, 0))) if n == 'conv_dw_w' else wts[n][0].astype(BF16)
              for n in gather_names]
    chip_idx = chip.reshape(1).astype(jnp.int32)
    first, first_token = _gather_start(
        shards[:1], [_place_block(shards[0], gather_axes[0], chip_idx, f"place_{gather_names[0]}")],
        gather_axes[:1], "gather_start_first")
    lands = [_place_block(sh, a, chip_idx, f"place_{n}", after=first_token)
             for sh, a, n in zip(shards[1:], gather_axes[1:], gather_names[1:])]
    rest, gather_token = _gather_start(shards[1:], lands, gather_axes[1:], "gather_start_rest")
    in_flight = dict(zip(gather_names, first + rest))
    full = {}

    def arrived(name, after):
        if name not in full:
            grp = next(grp for grp in gather_groups if name in grp)
            lands = _gather_wait([in_flight[n] for n in grp], [axis_of.get(n, 1) for n in grp], after,
                                 f"gather_wait_{grp[0]}")
            full.update(zip(grp, lands))
        return full[name]

    scattering = {}

    def send(names, grads):
        pairs, token = _scatter_start(grads, [axis_of[n] for n in names], f"scatter_start_{names[0]}")
        scattering.update(zip(names, pairs))
        return token

    final_g = final_norm.reshape(1, Dm)
    bias = _att_bias(att_rel_bias[0] + first_token[:1, :1])

    x1, ffn1_saved = _ffn_fwd(x0, ffn1_norm, arrived, "ffn1", after=gather_token)
    h = _rms_fwd(x1, mix_norm, "mix_norm")
    w_in_full = arrived('w_in', h)
    dw_full = full['conv_dw_w'][:CONV_KERNEL]
    proj = _mm(h, w_in_full, 'nn', "w_in", BF16)
    proj3 = proj.reshape(Bl, S, proj.shape[1])
    cv, c_act = _conv_fwd(proj3, dw_full, conv_dw_b, conv_ln_g, conv_ln_b)
    o_att = _att_fwd(proj3, bias)
    mem_h = _rms_fwd(mem2, mem_norm, "mem_norm")
    kv = _mm(mem_h, arrived('mem_w_kv', o_att), 'nn', "mem_kv", BF16)
    kv3 = kv.reshape(Bl, MEM_LEN, 2 * MEM_WIDTH)
    o_mem = _mem_fwd(proj3, kv3)
    c_act2, o_att2, o_mem2 = c_act.reshape(T, -1), o_att.reshape(T, -1), o_mem.reshape(T, -1)
    x2, yc, ya, ym = _mix_fwd(c_act2, o_att2, o_mem2, proj, b_gate, x1, full['conv_w_pw'], full['att_w_o'],
                              full['mem_w_o'], full['w_out'])
    x3, ffn2_saved = _ffn_fwd(x2, ffn2_norm, arrived, "ffn2")
    dx3, g_final, loss_vec = _final_fwd_bwd(x3, tgt, final_g)

    g = {}
    dx2, g['ffn2_norm'] = _ffn_bwd(dx3, x2, ffn2_norm, ffn2_saved, "ffn2", send)
    dgl, g['b_gate'], dc, doa, dom, g_pw, g_o, g_mo, g_out = _mix_bwd(
        dx2, yc, ya, ym, c_act2, o_att2, o_mem2, proj, b_gate, full['conv_w_pw'], full['att_w_o'],
        full['mem_w_o'], full['w_out'])
    token = send(['w_out', 'conv_w_pw', 'att_w_o', 'mem_w_o'], [g_out, g_pw, g_o, g_mo])
    dproj3 = dgl.reshape(Bl, S, -1)
    dproj3, g_dw, g['conv_dw_b'], g['conv_ln_g'], g['conv_ln_b'] = _conv_bwd(
        proj3, cv, dc.reshape(Bl, S, -1), dw_full, conv_ln_g, conv_ln_b, dproj3)
    dproj3, dk, dv, dbias = _att_bwd(proj3, doa.reshape(Bl, S, -1), bias, dproj3)
    g['att_rel_bias'] = _rel_bias_grad(dbias)
    dproj3, dkv = _mem_bwd(proj3, kv3, dom.reshape(Bl, S, -1), dproj3)
    dkv2 = dkv.reshape(Bl * MEM_LEN, 2 * MEM_WIDTH)
    g_kv = _mm(mem_h, dkv2, 'tn', "mem_kv_dw", BF16, after=token)
    dmem_h = _mm(dkv2, full['mem_w_kv'], 'nt', "mem_kv_dx", F32)
    _, g['mem_norm'] = _rms_bwd(mem2, mem_norm, dmem_h, dmem_h, "mem_norm_bwd")
    dkdv = jnp.concatenate([dk.reshape(T, -1), dv.reshape(T, -1)], axis=1)
    dproj = lax.dynamic_update_slice(dproj3.reshape(T, -1), dkdv, (0, 2 * CONV_WIDTH + ATT_WIDTH))
    token = send(['mem_w_kv', 'w_in'], [g_kv, _mm(h, dproj, 'tn', "w_in_dw", BF16)])
    dx1, g['mix_norm'] = _dx_rms_bwd([dproj], w_in_full, x1, mix_norm, dx2, "w_in_dx", after=token)
    dx0, g['ffn1_norm'] = _ffn_bwd(dx1, x0, ffn1_norm, ffn1_saved, "ffn1", send)
    g['final_norm'] = g_final

    sent, landed = _scatter_wait([scattering[n] for n in big_names], big_axes, dx0)
    halves = []
    half_idx = core.reshape(1).astype(jnp.int32)
    for n, a, own_full, parts in zip(big_names, big_axes, sent, landed):
        R, Cn = _shard_shape(own_full, a)
        start = (core * (R // 2), chip * Cn) if a == 1 else (chip * R + core * (R // 2), 0)
        own = lax.dynamic_slice(own_full, start, (R // 2, Cn))
        halves.append(_sum_partials(own, parts, half_idx, f"sum_{n}"))
    for n, sg in zip(big_names, _exchange_halves(halves)):
        g[n] = sg

    small_shapes = [wts[n].shape for n in SMALL]
    n_small = sum(int(wts[n].size) for n in SMALL)
    n_red = n_small + CONV_KERNEL * CONV_WIDTH + 1
    red = _all_reduce_small(_pack([g[n] for n in SMALL] + [g_dw, loss_vec[0, :1]], -(-n_red // 1024) * 8))
    red_list = _unpack(red, small_shapes + [(CONV_KERNEL, CONV_WIDTH), ()])
    for n, rg in zip(SMALL, red_list[:-2]):
        g[n] = rg
    loss = red_list[-1]
    dw_cols = conv_dw_w.shape[2]
    g['conv_dw_w'] = lax.dynamic_slice(red_list[-2], (0, chip * dw_cols), (CONV_KERNEL, dw_cols))[None]

    delta, new_m, new_v = {}, {}, {}
    for n in big_names:
        g[n] = g[n][None]
        d, nm, nv = _adamw(wts[n][0], g[n][0], mom1[n][0], mom2[n][0], f"adamw_{n}")
        delta[n], new_m[n], new_v[n] = d[None], nm[None], nv[None]
    rest = SMALL + ['conv_dw_w']
    rest_shapes = [wts[n].shape for n in rest]
    rows = -(-sum(int(wts[n].size) for n in rest) // 1024) * 8
    packed = [_pack([src[n] for n in rest], rows) for src in (wts, g, mom1, mom2)]
    for out, res in zip((delta, new_m, new_v), _adamw(*packed, "adamw_small")):
        for n, a in zip(rest, _unpack(res, rest_shapes)):
            out[n] = a

    grad_x = dx0.reshape(Bl, S, Dm)
    return (loss, grad_x, *[g[n] for n in WEIGHTS], *[delta[n] for n in WEIGHTS],
            *[new_m[n] for n in WEIGHTS], *[new_v[n] for n in WEIGHTS])
```

```python
import jax
import jax.numpy as jnp
from jax import lax
from jax.experimental import pallas as pl
from jax.experimental.pallas import tpu as pltpu

F32 = jnp.float32
BF16 = jnp.bfloat16

D_MODEL = 1024
D_FF = 2816
CHUNK = 64
LEFT_CHUNKS = 8
MAX_REL = 128
N_REL = (CHUNK - 1) + MAX_REL + 1
CONV_WIDTH = 512
CONV_KERNEL = 31
ATT_HEADS = 8
ATT_WIDTH = 512
MEM_LEN = 256
MEM_HEADS = 4
MEM_HEAD_DIM = 128
MEM_WIDTH = 512
EPS = 1e-6
MASK_VALUE = -1e30
ATT_SCALE = 64 ** -0.5
MEM_SCALE = 128 ** -0.5

ADAM_LR = 0.001
ADAM_B1 = 0.9
ADAM_B2 = 0.999
ADAM_EPS = 1e-08
ADAM_WD = 0.01
ADAM_STEP = 10

QB = 256
KW = 3 * QB
CONV_PAD = 32
CONV_TILE = 256

VMEM_LIMIT = 56 << 20
MXU_COLS = 256

WEIGHTS = ['ffn1_norm', 'ffn1_w_up', 'ffn1_w_down', 'mix_norm', 'mem_norm', 'w_in', 'b_gate', 'conv_dw_w',
           'conv_dw_b', 'conv_ln_g', 'conv_ln_b', 'conv_w_pw', 'att_rel_bias', 'att_w_o', 'mem_w_kv', 'mem_w_o',
           'w_out', 'ffn2_norm', 'ffn2_w_up', 'ffn2_w_down', 'final_norm']
BIG = [('ffn1_w_up', 1), ('ffn1_w_down', 0), ('w_in', 1), ('conv_w_pw', 1), ('att_w_o', 1), ('mem_w_kv', 0),
       ('mem_w_o', 1), ('w_out', 0), ('ffn2_w_up', 1), ('ffn2_w_down', 0)]
SMALL = ['ffn1_norm', 'mix_norm', 'mem_norm', 'b_gate', 'conv_dw_b', 'conv_ln_g', 'conv_ln_b', 'att_rel_bias',
         'ffn2_norm', 'final_norm']
N_CHIPS = 4
N_DEV = 8
MESH = pl.DeviceIdType.MESH


def _pick(n, cands):
    for c in cands:
        if n % c == 0:
            return c
    return n


def _sig(x):
    return 0.5 * jnp.tanh(0.5 * x) + 0.5


def _params(sem=None, vmem=VMEM_LIMIT):
    return pltpu.CompilerParams(dimension_semantics=sem, vmem_limit_bytes=vmem)


def _dot(a, b, mode='nn'):
    dims = {'nn': (((1,), (0,)), ((), ())), 'nt': (((1,), (1,)), ((), ())), 'tn': (((0,), (0,)), ((), ()))}[mode]
    return lax.dot_general(a.astype(BF16), b.astype(BF16), dims, preferred_element_type=F32)


def _mm(a, b, mode, name, out_dtype, res=None, scale=1.0, after=None):
    if mode == 'nn':
        (M, C), (_, N) = a.shape, b.shape
    elif mode == 'nt':
        (M, C), (N, _) = a.shape, b.shape
    else:
        (C, M), (_, N) = a.shape, b.shape
    tm = _pick(M, (1024, 1408, 512, 256, 128))
    tn = _pick(N, (1024, 1408, 512, 256, 128))
    tc = C if C <= 2816 else _pick(C, (2048, 1024, 1408, 512, 256, 128))
    nk = C // tc
    if mode == 'nn':
        a_spec = pl.BlockSpec((tm, tc), lambda i, j, k: (i, k))
        b_spec = pl.BlockSpec((tc, tn), lambda i, j, k: (k, j))
    elif mode == 'nt':
        a_spec = pl.BlockSpec((tm, tc), lambda i, j, k: (i, k))
        b_spec = pl.BlockSpec((tn, tc), lambda i, j, k: (j, k))
    else:
        a_spec = pl.BlockSpec((tc, tm), lambda i, j, k: (k, i))
        b_spec = pl.BlockSpec((tc, tn), lambda i, j, k: (k, j))
    o_spec = pl.BlockSpec((tm, tn), lambda i, j, k: (i, j))
    has_res = res is not None
    has_after = after is not None

    def body(*refs):
        a_ref, b_ref = refs[:2]
        r_ref = refs[2] if has_res else None
        o_ref, acc_ref = refs[-2:]
        k = pl.program_id(2)

        def finish(acc):
            if scale != 1.0:
                acc = acc * scale
            if r_ref is not None:
                acc = r_ref[...] + acc
            o_ref[...] = acc.astype(o_ref.dtype)

        if nk == 1:
            finish(_dot(a_ref[...], b_ref[...], mode))
        else:
            @pl.when(k == 0)
            def _():
                acc_ref[...] = jnp.zeros_like(acc_ref)

            acc_ref[...] += _dot(a_ref[...], b_ref[...], mode)

            @pl.when(k == nk - 1)
            def _():
                finish(acc_ref[...])

    in_specs = [a_spec, b_spec] + ([o_spec] if has_res else []) + ([ANY] if has_after else [])
    args = (a, b) + ((res,) if has_res else ()) + ((after,) if has_after else ())
    acc_shape = (tm, tn) if nk > 1 else (8, 128)
    return pl.pallas_call(
        body, name=name, grid=(M // tm, N // tn, nk), in_specs=in_specs, out_specs=o_spec,
        out_shape=jax.ShapeDtypeStruct((M, N), out_dtype), scratch_shapes=[pltpu.VMEM(acc_shape, F32)],
        compiler_params=_params(("parallel", "parallel", "arbitrary")))(*args)


def _row_tile(T):
    return _pick(T, (512, 256, 128, 64, 32, 16, 8))


def _rms_fwd(x, g, name, after=None):
    T, Dm = x.shape
    tm = _row_tile(T)

    def body(x_ref, g_ref, *rest):
        o_ref = rest[-1]
        xv = x_ref[...]
        r = lax.rsqrt(jnp.mean(xv * xv, axis=-1, keepdims=True) + EPS)
        o_ref[...] = ((xv * r) * g_ref[...]).astype(o_ref.dtype)

    extra = () if after is None else (after,)
    return pl.pallas_call(
        body, name=name, grid=(T // tm,),
        in_specs=[pl.BlockSpec((tm, Dm), lambda i: (i, 0)), pl.BlockSpec((1, Dm), lambda i: (0, 0))]
        + [ANY] * len(extra),
        out_specs=pl.BlockSpec((tm, Dm), lambda i: (i, 0)), out_shape=jax.ShapeDtypeStruct((T, Dm), BF16),
        compiler_params=_params(("parallel",)))(x, g, *extra)


def _rms_bwd(x, g, dh, dres, name):
    T, Dm = x.shape
    tm = _row_tile(T)

    def body(x_ref, g_ref, dh_ref, dr_ref, dx_ref, dg_ref):
        i = pl.program_id(0)
        xv = x_ref[...]
        r = lax.rsqrt(jnp.mean(xv * xv, axis=-1, keepdims=True) + EPS)
        xr = xv * r
        dh_v = dh_ref[...].astype(F32)
        dyg = dh_v * g_ref[...]
        dx = r * (dyg - xr * jnp.mean(dyg * xr, axis=-1, keepdims=True))
        dx_ref[...] = dr_ref[...] + dx

        @pl.when(i == 0)
        def _():
            dg_ref[...] = jnp.zeros_like(dg_ref)

        dg_ref[...] += jnp.sum(dh_v * xr, axis=0, keepdims=True)

    row = pl.BlockSpec((tm, Dm), lambda i: (i, 0))
    vec = pl.BlockSpec((1, Dm), lambda i: (0, 0))
    return pl.pallas_call(
        body, name=name, grid=(T // tm,), in_specs=[row, vec, row, row], out_specs=[row, vec],
        out_shape=[jax.ShapeDtypeStruct((T, Dm), F32), jax.ShapeDtypeStruct((1, Dm), F32)],
        compiler_params=_params(("arbitrary",)))(x, g, dh, dres)


def _final_fwd_bwd(x3, tgt, g):
    T, Dm = x3.shape
    tm = _row_tile(T)

    def body(x_ref, t_ref, g_ref, dx_ref, dg_ref, loss_ref):
        i = pl.program_id(0)
        xv = x_ref[...]
        gg = g_ref[...]
        r = lax.rsqrt(jnp.mean(xv * xv, axis=-1, keepdims=True) + EPS)
        xr = xv * r
        err = xr * gg - t_ref[...]
        dout = err * (1.0 / Dm)
        dyg = dout * gg
        dx_ref[...] = r * (dyg - xr * jnp.mean(dyg * xr, axis=-1, keepdims=True))

        @pl.when(i == 0)
        def _():
            dg_ref[...] = jnp.zeros_like(dg_ref)
            loss_ref[...] = jnp.zeros_like(loss_ref)

        dg_ref[...] += jnp.sum(dout * xr, axis=0, keepdims=True)
        loss_ref[...] += jnp.zeros_like(loss_ref) + (0.5 / Dm) * jnp.sum(err * err)

    row = pl.BlockSpec((tm, Dm), lambda i: (i, 0))
    vec = pl.BlockSpec((1, Dm), lambda i: (0, 0))
    one = pl.BlockSpec((1, 128), lambda i: (0, 0))
    return pl.pallas_call(
        body, name="final_fwd_bwd", grid=(T // tm,), in_specs=[row, row, vec], out_specs=[row, vec, one],
        out_shape=[jax.ShapeDtypeStruct((T, Dm), F32), jax.ShapeDtypeStruct((1, Dm), F32),
                   jax.ShapeDtypeStruct((1, 128), F32)],
        compiler_params=_params(("arbitrary",)))(x3, tgt, g)


def _ffn_up(h, w_up, name):
    T, K = h.shape
    Fh = w_up.shape[1] // 2
    tm = _pick(T, (1024, 512, 256, 128))
    tn = _pick(Fh, (1408, 512, 256, 128))
    nj = Fh // tn

    def body(h_ref, wa_ref, wb_ref, a_ref, b_ref, act_ref):
        hv = h_ref[...]
        for c0 in range(0, tn, MXU_COLS):
            cs = slice(c0, min(c0 + MXU_COLS, tn))
            a = _dot(hv, wa_ref[:, cs])
            b = _dot(hv, wb_ref[:, cs])
            a_ref[:, cs] = a.astype(BF16)
            b_ref[:, cs] = b.astype(BF16)
            act_ref[:, cs] = (a * _sig(a) * b).astype(BF16)

    out = pl.BlockSpec((tm, tn), lambda i, j: (i, j))
    return pl.pallas_call(
        body, name=name, grid=(T // tm, nj),
        in_specs=[pl.BlockSpec((tm, K), lambda i, j: (i, 0)), pl.BlockSpec((K, tn), lambda i, j: (0, j)),
                  pl.BlockSpec((K, tn), lambda i, j: (0, j + nj))],
        out_specs=[out, out, out], out_shape=[jax.ShapeDtypeStruct((T, Fh), BF16)] * 3,
        compiler_params=_params(("parallel", "parallel")))(h, w_up, w_up)


def _ffn_down_bwd(dout, w_down, a, b, name, after=None):
    T, Dm = dout.shape
    Fh = w_down.shape[0]
    tm = _pick(T, (1024, 512, 256, 128))
    tn = _pick(Fh, (1408, 512, 256, 128))

    def body(d_ref, w_ref, a_ref, b_ref, *rest):
        da_ref, db_ref = rest[-2:]
        dv = (d_ref[...] * 0.5).astype(BF16)
        for c0 in range(0, tn, MXU_COLS):
            cs = slice(c0, min(c0 + MXU_COLS, tn))
            dact = _dot(dv, w_ref[cs, :], 'nt')
            av = a_ref[:, cs].astype(F32)
            bv = b_ref[:, cs].astype(F32)
            s = _sig(av)
            da_ref[:, cs] = (dact * bv * s * (1.0 + av * (1.0 - s))).astype(BF16)
            db_ref[:, cs] = (dact * av * s).astype(BF16)

    tile = pl.BlockSpec((tm, tn), lambda i, j: (i, j))
    extra = () if after is None else (after,)
    return pl.pallas_call(
        body, name=name, grid=(T // tm, Fh // tn),
        in_specs=[pl.BlockSpec((tm, Dm), lambda i, j: (i, 0)), pl.BlockSpec((tn, Dm), lambda i, j: (j, 0)),
                  tile, tile] + [ANY] * len(extra),
        out_specs=[tile, tile], out_shape=[jax.ShapeDtypeStruct((T, Fh), BF16)] * 2,
        compiler_params=_params(("parallel", "parallel")))(dout, w_down, a, b, *extra)


def _dx_rms_bwd(pieces, w, x, g, dres, name, after=None):
    T, Dm = x.shape
    width = pieces[0].shape[1]
    tm = _pick(T, (1024, 512, 256, 128))
    tc = _pick(width, (1408, 2048, 1024, 512, 256, 128))
    per = width // tc
    nk = per * len(pieces)
    npc = len(pieces)
    rows = _pick(tm, (256, 128))

    def body(*refs):
        p_refs = refs[:npc]
        w_ref, x_hbm, g_ref, dr_hbm = refs[npc:npc + 4]
        dx_ref, dg_ref, acc_ref, x_buf, dr_buf, sems = refs[-6:]
        i = pl.program_id(0)
        k = pl.program_id(1)
        tile = pl.ds(pl.multiple_of(i * tm, tm), tm)
        fetch_x = pltpu.make_async_copy(x_hbm.at[tile, :], x_buf, sems.at[0])
        fetch_dr = pltpu.make_async_copy(dr_hbm.at[tile, :], dr_buf, sems.at[1])

        @pl.when(k == 0)
        def _():
            fetch_x.start()
            fetch_dr.start()
            acc_ref[...] = jnp.zeros_like(acc_ref)

        @pl.when((i == 0) & (k == 0))
        def _():
            dg_ref[...] = jnp.zeros_like(dg_ref)

        for p in range(npc):
            @pl.when((k >= p * per) & (k < (p + 1) * per))
            def _(p=p):
                acc_ref[...] += _dot(p_refs[p][...], w_ref[...], 'nt')

        @pl.when(k == nk - 1)
        def _():
            fetch_x.wait()
            fetch_dr.wait()

            def chunk(c, carry):
                rs = pl.ds(pl.multiple_of(c * rows, rows), rows)
                dh = acc_ref[rs, :]
                xv = x_buf[rs, :]
                r = lax.rsqrt(jnp.mean(xv * xv, axis=-1, keepdims=True) + EPS)
                xr = xv * r
                dyg = dh * g_ref[...]
                dx_ref[rs, :] = dr_buf[rs, :] + r * (dyg - xr * jnp.mean(dyg * xr, axis=-1, keepdims=True))
                dg_ref[...] += jnp.sum(dh * xr, axis=0, keepdims=True)
                return carry

            lax.fori_loop(0, tm // rows, chunk, 0)

    def piece_spec(p):
        return pl.BlockSpec((tm, tc), lambda i, k: (i, jnp.clip(k - p * per, 0, per - 1)))

    row = pl.BlockSpec((tm, Dm), lambda i, k: (i, 0))
    vec = pl.BlockSpec((1, Dm), lambda i, k: (0, 0))
    extra = () if after is None else (after,)
    return pl.pallas_call(
        body, name=name, grid=(T // tm, nk),
        in_specs=[piece_spec(p) for p in range(npc)] + [pl.BlockSpec((Dm, tc), lambda i, k: (0, k)), ANY, vec, ANY]
        + [ANY] * len(extra),
        out_specs=[row, vec], out_shape=[jax.ShapeDtypeStruct((T, Dm), F32), jax.ShapeDtypeStruct((1, Dm), F32)],
        scratch_shapes=[pltpu.VMEM((tm, Dm), F32), pltpu.VMEM((tm, Dm), F32), pltpu.VMEM((tm, Dm), F32),
                        pltpu.SemaphoreType.DMA((2,))],
        compiler_params=_params(("arbitrary", "arbitrary")))(*pieces, w, x, g, dres, *extra)


def _dw_pieces(a, pieces, name):
    C, M = a.shape
    width = pieces[0].shape[1]
    npc = len(pieces)
    tm = _pick(M, (1024, 512, 256, 128))
    tn = _pick(width, (1408, 1024, 512, 256, 128))
    tc = _pick(C, (2048, 1024, 512, 256, 128))
    per = width // tn
    nk = C // tc

    def body(*refs):
        a_ref = refs[0]
        p_refs = refs[1:1 + npc]
        o_ref, acc_ref = refs[-2:]
        j = pl.program_id(1)
        k = pl.program_id(2)

        @pl.when(k == 0)
        def _():
            acc_ref[...] = jnp.zeros_like(acc_ref)

        for p in range(npc):
            @pl.when((j >= p * per) & (j < (p + 1) * per))
            def _(p=p):
                acc_ref[...] += _dot(a_ref[...], p_refs[p][...], 'tn')

        @pl.when(k == nk - 1)
        def _():
            o_ref[...] = acc_ref[...].astype(o_ref.dtype)

    def piece_spec(p):
        return pl.BlockSpec((tc, tn), lambda i, j, k: (k, jnp.clip(j - p * per, 0, per - 1)))

    return pl.pallas_call(
        body, name=name, grid=(M // tm, per * npc, nk),
        in_specs=[pl.BlockSpec((tc, tm), lambda i, j, k: (k, i))] + [piece_spec(p) for p in range(npc)],
        out_specs=pl.BlockSpec((tm, tn), lambda i, j, k: (i, j)),
        out_shape=jax.ShapeDtypeStruct((M, width * npc), BF16), scratch_shapes=[pltpu.VMEM((tm, tn), F32)],
        compiler_params=_params(("parallel", "parallel", "arbitrary")))(a, *pieces)


def _mix_fwd(c_act, o_att, o_mem, proj, b_gate, x1, w_pw, w_o, w_mo, w_out):
    T, Dm = x1.shape
    W = c_act.shape[1]
    tm = _pick(T, (256, 128, 64, 32, 16, 8))

    def body(c_ref, oa_ref, om_ref, gl_ref, bg_ref, x1_ref, wpw_ref, wo_ref, wmo_ref, wout_ref,
             x2_ref, yc_ref, ya_ref, ym_ref):
        yc = _dot(c_ref[...], wpw_ref[...])
        ya = _dot(oa_ref[...], wo_ref[...])
        ym = _dot(om_ref[...], wmo_ref[...])
        g = _sig(gl_ref[...].astype(F32) + bg_ref[...])
        y = g[:, :Dm] * yc + g[:, Dm:2 * Dm] * ya + g[:, 2 * Dm:] * ym
        x2_ref[...] = x1_ref[...] + _dot(y, wout_ref[...])
        yc_ref[...] = yc.astype(BF16)
        ya_ref[...] = ya.astype(BF16)
        ym_ref[...] = ym.astype(BF16)

    rowW = pl.BlockSpec((tm, W), lambda i: (i, 0))
    rowD = pl.BlockSpec((tm, Dm), lambda i: (i, 0))
    full = lambda s: pl.BlockSpec(s, lambda i: (0, 0))
    return pl.pallas_call(
        body, name="mix_fwd", grid=(T // tm,),
        in_specs=[rowW, rowW, rowW, pl.BlockSpec((tm, 3 * Dm), lambda i: (i, 1)), full((1, 3 * Dm)), rowD,
                  full((W, Dm)), full((W, Dm)), full((W, Dm)), full((Dm, Dm))],
        out_specs=[rowD] * 4,
        out_shape=[jax.ShapeDtypeStruct((T, Dm), F32)] + [jax.ShapeDtypeStruct((T, Dm), BF16)] * 3,
        compiler_params=_params(("parallel",)))(c_act, o_att, o_mem, proj, b_gate, x1, w_pw, w_o, w_mo, w_out)


def _mix_bwd(dx2, yc, ya, ym, c_act, o_att, o_mem, proj, b_gate, w_pw, w_o, w_mo, w_out):
    T, Dm = dx2.shape
    W = w_pw.shape[0]
    tm = _pick(T, (256, 128, 64, 32, 16, 8))
    nt = T // tm

    def body(dx_ref, yc_ref, ya_ref, ym_ref, c_ref, oa_ref, om_ref, gl_ref, bg_ref, wpw_ref, wo_ref, wmo_ref,
             wout_ref, dgl_ref, dbg_ref, dc_ref, doa_ref, dom_ref, gpw_ref, go_ref, gmo_ref, gout_ref,
             apw, ao, amo, aout):
        i = pl.program_id(0)

        @pl.when(i == 0)
        def _():
            dbg_ref[...] = jnp.zeros_like(dbg_ref)
            for acc in (apw, ao, amo, aout):
                acc[...] = jnp.zeros_like(acc)

        dxv = dx_ref[...].astype(BF16)
        dy = _dot(dxv, wout_ref[...], 'nt')
        g = _sig(gl_ref[...].astype(F32) + bg_ref[...])
        branches = ((yc_ref, c_ref, wpw_ref, dc_ref, apw), (ya_ref, oa_ref, wo_ref, doa_ref, ao),
                    (ym_ref, om_ref, wmo_ref, dom_ref, amo))
        y = jnp.zeros((tm, Dm), F32)
        for n, (y_ref, in_ref, w_ref, dk_ref, acc) in enumerate(branches):
            gk = g[:, n * Dm:(n + 1) * Dm]
            yk = y_ref[...].astype(F32)
            dyk = dy * gk
            dgl = dyk * yk * (1.0 - gk)
            dgl_ref[:, n * Dm:(n + 1) * Dm] = dgl.astype(BF16)
            dbg_ref[:, n * Dm:(n + 1) * Dm] += jnp.sum(dgl, axis=0, keepdims=True)
            dyk = dyk.astype(BF16)
            dk_ref[...] = _dot(dyk, w_ref[...], 'nt').astype(BF16)
            acc[...] += _dot(in_ref[...], dyk, 'tn')
            y = y + gk * yk
        aout[...] += _dot(y, dxv, 'tn')

        @pl.when(i == nt - 1)
        def _():
            for acc, out in ((apw, gpw_ref), (ao, go_ref), (amo, gmo_ref), (aout, gout_ref)):
                out[...] = acc[...].astype(BF16)

    rowW = pl.BlockSpec((tm, W), lambda i: (i, 0))
    rowD = pl.BlockSpec((tm, Dm), lambda i: (i, 0))
    full = lambda s: pl.BlockSpec(s, lambda i: (0, 0))
    return pl.pallas_call(
        body, name="mix_bwd", grid=(nt,),
        in_specs=[rowD, rowD, rowD, rowD, rowW, rowW, rowW, pl.BlockSpec((tm, 3 * Dm), lambda i: (i, 1)),
                  full((1, 3 * Dm)), full((W, Dm)), full((W, Dm)), full((W, Dm)), full((Dm, Dm))],
        out_specs=[pl.BlockSpec((tm, 3 * Dm), lambda i: (i, 1)), full((1, 3 * Dm)), rowW, rowW, rowW,
                   full((W, Dm)), full((W, Dm)), full((W, Dm)), full((Dm, Dm))],
        out_shape=[jax.ShapeDtypeStruct((T, 6 * Dm), BF16), jax.ShapeDtypeStruct((1, 3 * Dm), F32)]
        + [jax.ShapeDtypeStruct((T, W), BF16)] * 3 + [jax.ShapeDtypeStruct((W, Dm), BF16)] * 3
        + [jax.ShapeDtypeStruct((Dm, Dm), BF16)],
        scratch_shapes=[pltpu.VMEM((W, Dm), F32)] * 3 + [pltpu.VMEM((Dm, Dm), F32)],
        compiler_params=_params(("arbitrary",)))(dx2, yc, ya, ym, c_act, o_att, o_mem, proj, b_gate, w_pw, w_o,
                                                 w_mo, w_out)


def _ln_swish(cv, lg, lb):
    mu = jnp.mean(cv, axis=-1, keepdims=True)
    xc = cv - mu
    r = lax.rsqrt(jnp.mean(xc * xc, axis=-1, keepdims=True) + EPS)
    n = xc * r
    l = n * lg + lb
    return r, n, l


def _shift_copies(src, r0, win, shifts):
    win[...] = src[pl.ds(r0, CONV_TILE + CONV_PAD + 8), :]
    for s in range(8):
        shifts[s] = win[s:s + CONV_TILE + CONV_PAD, :]


def _tap(shifts, d):
    return shifts[d % 8, d - d % 8:d - d % 8 + CONV_TILE, :]


def _conv_fwd(proj3, dw_w, dw_b, ln_g, ln_b):
    Bl, S, _ = proj3.shape
    C, K, TS, PAD = CONV_WIDTH, CONV_KERNEL, CONV_TILE, CONV_PAD
    nt = S // TS

    def body(u_ref, w_ref, b_ref, lg_ref, lb_ref, cv_ref, c_ref, vbuf, win, shifts):
        vbuf[0:PAD, :] = jnp.zeros((PAD, C), F32)
        vbuf[S + PAD:S + PAD + 8, :] = jnp.zeros((8, C), F32)

        def glu(t, carry):
            r0 = pl.multiple_of(t * TS, TS)
            u = u_ref[pl.ds(r0, TS), :].astype(F32)
            vbuf[pl.ds(PAD + r0, TS), :] = u[:, :C] * _sig(u[:, C:])
            return carry

        lax.fori_loop(0, nt, glu, 0)

        def conv(t, carry):
            r0 = pl.multiple_of(t * TS, TS)
            _shift_copies(vbuf, r0, win, shifts)
            acc = jnp.zeros((TS, C), F32)
            for j in range(K):
                acc = acc + w_ref[j:j + 1, :] * _tap(shifts, PAD - (K - 1) + j)
            cv = acc + b_ref[...]
            cv_ref[pl.ds(r0, TS), :] = cv
            _, _, l = _ln_swish(cv, lg_ref[...], lb_ref[...])
            c_ref[pl.ds(r0, TS), :] = (l * _sig(l)).astype(BF16)
            return carry

        lax.fori_loop(0, nt, conv, 0)

    vec = pl.BlockSpec((1, C), lambda b: (0, 0))
    return pl.pallas_call(
        body, name="conv_fwd", grid=(Bl,),
        in_specs=[pl.BlockSpec((None, S, 2 * C), lambda b: (b, 0, 0)), pl.BlockSpec((K, C), lambda b: (0, 0)),
                  vec, vec, vec],
        out_specs=[pl.BlockSpec((None, S, C), lambda b: (b, 0, 0))] * 2,
        out_shape=[jax.ShapeDtypeStruct((Bl, S, C), F32), jax.ShapeDtypeStruct((Bl, S, C), BF16)],
        scratch_shapes=[pltpu.VMEM((S + PAD + 8, C), F32), pltpu.VMEM((TS + PAD + 8, C), F32),
                        pltpu.VMEM((8, TS + PAD, C), F32)],
        compiler_params=_params(("parallel",)))(proj3, dw_w, dw_b, ln_g, ln_b)


def _conv_bwd(proj3, cv, dc, dw_w, ln_g, ln_b, dproj3):
    Bl, S, _ = proj3.shape
    C, K, TS, PAD = CONV_WIDTH, CONV_KERNEL, CONV_TILE, CONV_PAD
    nt = S // TS

    def body(u_ref, cv_ref, dc_ref, w_ref, lg_ref, lb_ref, through_ref, du_ref, dw_ref, db_ref, dlg_ref, dlb_ref,
             vbuf, gbuf, win, shifts, dwacc):
        b = pl.program_id(0)

        @pl.when(b == 0)
        def _():
            dw_ref[...] = jnp.zeros_like(dw_ref)
            db_ref[...] = jnp.zeros_like(db_ref)
            dlg_ref[...] = jnp.zeros_like(dlg_ref)
            dlb_ref[...] = jnp.zeros_like(dlb_ref)

        vbuf[0:PAD, :] = jnp.zeros((PAD, C), F32)
        vbuf[S + PAD:S + PAD + 8, :] = jnp.zeros((8, C), F32)
        gbuf[S:S + PAD + 8, :] = jnp.zeros((PAD + 8, C), F32)
        dwacc[...] = jnp.zeros_like(dwacc)

        def norm_bwd(t, carry):
            r0 = pl.multiple_of(t * TS, TS)
            u = u_ref[pl.ds(r0, TS), :].astype(F32)
            vbuf[pl.ds(PAD + r0, TS), :] = u[:, :C] * _sig(u[:, C:])
            r, n, l = _ln_swish(cv_ref[pl.ds(r0, TS), :], lg_ref[...], lb_ref[...])
            s = _sig(l)
            dl = dc_ref[pl.ds(r0, TS), :].astype(F32) * s * (1.0 + l * (1.0 - s))
            dlg_ref[...] += jnp.sum(dl * n, axis=0, keepdims=True)
            dlb_ref[...] += jnp.sum(dl, axis=0, keepdims=True)
            dn = dl * lg_ref[...]
            dcv = r * (dn - jnp.mean(dn, axis=-1, keepdims=True) - n * jnp.mean(dn * n, axis=-1, keepdims=True))
            gbuf[pl.ds(r0, TS), :] = dcv
            db_ref[...] += jnp.sum(dcv, axis=0, keepdims=True)
            return carry

        lax.fori_loop(0, nt, norm_bwd, 0)

        def conv_bwd(t, carry):
            r0 = pl.multiple_of(t * TS, TS)
            _shift_copies(gbuf, r0, win, shifts)
            dv = jnp.zeros((TS, C), F32)
            for j in range(K):
                dv = dv + w_ref[j:j + 1, :] * _tap(shifts, K - 1 - j)
            u = u_ref[pl.ds(r0, TS), :].astype(F32)
            a, g = u[:, :C], u[:, C:]
            s = _sig(g)
            du_ref[pl.ds(r0, TS), 0:C] = (dv * s).astype(BF16)
            du_ref[pl.ds(r0, TS), C:2 * C] = (dv * a * s * (1.0 - s)).astype(BF16)
            dcv = gbuf[pl.ds(r0, TS), :]
            _shift_copies(vbuf, r0, win, shifts)
            for j in range(K):
                prod = dcv * _tap(shifts, PAD - (K - 1) + j)
                dwacc[j] += jnp.sum(prod.reshape(TS // 8, 8, C), axis=0)
            return carry

        lax.fori_loop(0, nt, conv_bwd, 0)
        dw_ref[...] += jnp.sum(dwacc[...], axis=1)

    vec = pl.BlockSpec((1, C), lambda b: (0, 0))
    seq = lambda w: pl.BlockSpec((None, S, w), lambda b: (b, 0, 0))
    return pl.pallas_call(
        body, name="conv_bwd", grid=(Bl,),
        in_specs=[seq(2 * C), seq(C), seq(C), pl.BlockSpec((K, C), lambda b: (0, 0)), vec, vec, ANY],
        out_specs=[seq(2 * C), pl.BlockSpec((K, C), lambda b: (0, 0)), vec, vec, vec],
        out_shape=[jax.ShapeDtypeStruct(dproj3.shape, BF16), jax.ShapeDtypeStruct((K, C), F32)]
        + [jax.ShapeDtypeStruct((1, C), F32)] * 3,
        input_output_aliases={6: 0},
        scratch_shapes=[pltpu.VMEM((S + PAD + 8, C), F32), pltpu.VMEM((S + PAD + 8, C), F32),
                        pltpu.VMEM((TS + PAD + 8, C), F32), pltpu.VMEM((8, TS + PAD, C), F32),
                        pltpu.VMEM((K, 8, C), F32)],
        compiler_params=_params(("arbitrary",)))(proj3, cv, dc, dw_w, ln_g, ln_b, dproj3)


def _att_bias(rel_bias):
    H = rel_bias.shape[0]
    Wd = KW + QB
    c = jnp.arange(Wd + 1)
    by_offset = rel_bias[:, jnp.clip(KW - c, -(CHUNK - 1), MAX_REL) + (CHUNK - 1)]
    flat = jnp.broadcast_to(by_offset[:, None, :], (H, QB, Wd + 1)).reshape(H, QB * (Wd + 1))
    skew = jnp.pad(flat, ((0, 0), (0, (QB + 1) * Wd - QB * (Wd + 1)))).reshape(H, QB + 1, Wd)[:, :QB, QB:]
    qi = jnp.arange(QB)[:, None]
    kj = jnp.arange(KW)[None, :]
    dchunk = ((KW - QB) + qi) // CHUNK - kj // CHUNK
    band = (dchunk >= 0) & (dchunk <= LEFT_CHUNKS)
    return jnp.where(band[None], skew, MASK_VALUE)


def _head_masks():
    lane = lax.broadcasted_iota(jnp.int32, (1, 128), 1)
    return (lane < 64, lane >= 64)


def _att_probs(qh, k2, bias, valid):
    s = _dot(qh, k2, 'nt') * ATT_SCALE + bias
    s = jnp.where(valid, s, MASK_VALUE)
    e = jnp.exp(s - jnp.max(s, axis=-1, keepdims=True))
    return e * (1.0 / jnp.sum(e, axis=-1, keepdims=True))


def _att_specs(S, q_col):
    nb = S // QB
    q_spec = pl.BlockSpec((None, QB, ATT_WIDTH), lambda b, i: (b, jnp.minimum(i, nb - 1), q_col))

    def kv_spec(col, kb):
        return pl.BlockSpec((None, QB, ATT_WIDTH),
                            lambda b, i: (b, jnp.clip(i - 2 + kb, 0, nb - 1), col))

    return q_spec, [kv_spec(3, kb) for kb in range(3)], [kv_spec(4, kb) for kb in range(3)]


def _att_fwd(proj3, bias):
    Bl, S, _ = proj3.shape
    nb = S // QB
    q_spec, k_specs, v_specs = _att_specs(S, 2)

    def body(q_ref, k0, k1, k2r, v0, v1, v2r, bias_ref, o_ref):
        i = pl.program_id(1)
        masks = _head_masks()
        valid = lax.broadcasted_iota(jnp.int32, (QB, KW), 1) >= (2 - i) * QB
        for pr in range(ATT_HEADS // 2):
            ls = slice(128 * pr, 128 * (pr + 1))
            q2 = q_ref[:, ls]
            k2 = jnp.concatenate([k0[:, ls], k1[:, ls], k2r[:, ls]], axis=0)
            v2 = jnp.concatenate([v0[:, ls], v1[:, ls], v2r[:, ls]], axis=0)
            o2 = jnp.zeros((QB, 128), F32)
            for hh in range(2):
                p = _att_probs(jnp.where(masks[hh], q2, 0), k2, bias_ref[2 * pr + hh], valid)
                o2 = o2 + _dot(p, jnp.where(masks[hh], v2, 0))
            o_ref[:, ls] = o2.astype(BF16)

    return pl.pallas_call(
        body, name="att_fwd", grid=(Bl, nb),
        in_specs=[q_spec] + k_specs + v_specs + [pl.BlockSpec((ATT_HEADS, QB, KW), lambda b, i: (0, 0, 0))],
        out_specs=pl.BlockSpec((None, QB, ATT_WIDTH), lambda b, i: (b, i, 0)),
        out_shape=jax.ShapeDtypeStruct((Bl, S, ATT_WIDTH), BF16),
        compiler_params=_params(("parallel", "arbitrary")))(*([proj3] * 7), bias)


def _att_bwd(proj3, do, bias, dproj3):
    Bl, S, _ = proj3.shape
    nb = S // QB
    q_spec, k_specs, v_specs = _att_specs(S, 2)
    do_spec = pl.BlockSpec((None, QB, ATT_WIDTH), lambda b, i: (b, jnp.minimum(i, nb - 1), 0))
    kv_out = pl.BlockSpec((None, QB, ATT_WIDTH), lambda b, i: (b, jnp.clip(i - 2, 0, nb - 1), 0))
    bias_spec = pl.BlockSpec((ATT_HEADS, QB, KW), lambda b, i: (0, 0, 0))

    def body(q_ref, k0, k1, k2r, v0, v1, v2r, do_ref, bias_ref, through_ref, dq_ref, dk_ref, dv_ref, db_ref,
             dkw, dvw):
        b = pl.program_id(0)
        i = pl.program_id(1)

        @pl.when((b == 0) & (i == 0))
        def _():
            db_ref[...] = jnp.zeros_like(db_ref)

        @pl.when(i == 0)
        def _():
            dkw[...] = jnp.zeros_like(dkw)
            dvw[...] = jnp.zeros_like(dvw)

        @pl.when(i < nb)
        def _():
            masks = _head_masks()
            valid = lax.broadcasted_iota(jnp.int32, (QB, KW), 1) >= (2 - i) * QB
            for pr in range(ATT_HEADS // 2):
                ls = slice(128 * pr, 128 * (pr + 1))
                q2 = q_ref[:, ls]
                do2 = do_ref[:, ls]
                k2 = jnp.concatenate([k0[:, ls], k1[:, ls], k2r[:, ls]], axis=0)
                v2 = jnp.concatenate([v0[:, ls], v1[:, ls], v2r[:, ls]], axis=0)
                dq2 = jnp.zeros((QB, 128), F32)
                dk2 = jnp.zeros((KW, 128), F32)
                dv2 = jnp.zeros((KW, 128), F32)
                for hh in range(2):
                    h = 2 * pr + hh
                    qh = jnp.where(masks[hh], q2, 0)
                    doh = jnp.where(masks[hh], do2, 0)
                    p = _att_probs(qh, k2, bias_ref[h], valid)
                    dp = _dot(doh, v2, 'nt')
                    ds = p * (dp - jnp.sum(p * dp, axis=-1, keepdims=True))
                    db_ref[h] += ds
                    dq2 = dq2 + _dot(ds, jnp.where(masks[hh], k2, 0))
                    dk2 = dk2 + _dot(ds, qh, 'tn')
                    dv2 = dv2 + _dot(p, doh, 'tn')
                dq_ref[:, ls] = (dq2 * ATT_SCALE).astype(BF16)
                dkw[:, ls] += dk2 * ATT_SCALE
                dvw[:, ls] += dv2

        dk_ref[...] = dkw[0:QB, :].astype(BF16)
        dv_ref[...] = dvw[0:QB, :].astype(BF16)
        for buf in (dkw, dvw):
            rest = buf[QB:KW, :]
            buf[0:KW - QB, :] = rest
            buf[KW - QB:KW, :] = jnp.zeros((QB, ATT_WIDTH), F32)

    blk = jax.ShapeDtypeStruct((Bl, S, ATT_WIDTH), BF16)
    return pl.pallas_call(
        body, name="att_bwd", grid=(Bl, nb + 2),
        in_specs=[q_spec] + k_specs + v_specs + [do_spec, bias_spec, ANY],
        out_specs=[q_spec, kv_out, kv_out, bias_spec],
        out_shape=[jax.ShapeDtypeStruct(dproj3.shape, BF16), blk, blk,
                   jax.ShapeDtypeStruct((ATT_HEADS, QB, KW), F32)],
        input_output_aliases={9: 0},
        scratch_shapes=[pltpu.VMEM((KW, ATT_WIDTH), F32), pltpu.VMEM((KW, ATT_WIDTH), F32)],
        compiler_params=_params(("arbitrary", "arbitrary")))(*([proj3] * 7), do, bias, dproj3)


def _rel_bias_grad(dbias):
    H = dbias.shape[0]
    Wd = KW + QB
    padded = jnp.pad(dbias, ((0, 0), (0, 1), (QB, 0)))
    skew = padded.reshape(H, (QB + 1) * Wd)[:, :QB * (Wd + 1)].reshape(H, QB, Wd + 1)[:, :, :Wd]
    c = jnp.arange(Wd)[:, None]
    bins = (jnp.clip(KW - c, -(CHUNK - 1), MAX_REL) + (CHUNK - 1) == jnp.arange(N_REL)[None, :]).astype(F32)

    def body(s_ref, bins_ref, o_ref):
        col = jnp.sum(s_ref[...], axis=1)
        o_ref[...] = jnp.dot(col, bins_ref[...], preferred_element_type=F32, precision=lax.Precision.HIGHEST)

    return pl.pallas_call(
        body, name="rel_bias_grad", grid=(1,),
        in_specs=[pl.BlockSpec((H, QB, Wd), lambda i: (0, 0, 0)), pl.BlockSpec((Wd, N_REL), lambda i: (0, 0))],
        out_specs=pl.BlockSpec((H, N_REL), lambda i: (0, 0)), out_shape=jax.ShapeDtypeStruct((H, N_REL), F32),
        compiler_params=_params(("arbitrary",)))(skew, bins)


MEM_TILE = 512


def _mem_probs(qh, kh):
    s = _dot(qh, kh, 'nt') * MEM_SCALE
    e = jnp.exp(s - jnp.max(s, axis=-1, keepdims=True))
    return e * (1.0 / jnp.sum(e, axis=-1, keepdims=True))


def _mem_fwd(proj3, kv3):
    Bl, S, _ = proj3.shape
    tq = _pick(S, (MEM_TILE, 256))
    hd = MEM_HEAD_DIM

    def body(q_ref, kv_ref, o_ref):
        for h in range(MEM_HEADS):
            p = _mem_probs(q_ref[:, h * hd:(h + 1) * hd], kv_ref[:, h * hd:(h + 1) * hd])
            o_ref[:, h * hd:(h + 1) * hd] = _dot(p, kv_ref[:, MEM_WIDTH + h * hd:MEM_WIDTH + (h + 1) * hd]).astype(BF16)

    return pl.pallas_call(
        body, name="mem_fwd", grid=(Bl, S // tq),
        in_specs=[pl.BlockSpec((None, tq, MEM_WIDTH), lambda b, i: (b, i, 5)),
                  pl.BlockSpec((None, MEM_LEN, 2 * MEM_WIDTH), lambda b, i: (b, 0, 0))],
        out_specs=pl.BlockSpec((None, tq, MEM_WIDTH), lambda b, i: (b, i, 0)),
        out_shape=jax.ShapeDtypeStruct((Bl, S, MEM_WIDTH), BF16),
        compiler_params=_params(("parallel", "parallel")))(proj3, kv3)


def _mem_bwd(proj3, kv3, do, dproj3):
    Bl, S, _ = proj3.shape
    tq = _pick(S, (MEM_TILE, 256))
    hd = MEM_HEAD_DIM

    def body(q_ref, kv_ref, do_ref, through_ref, dq_ref, dkv_ref):
        i = pl.program_id(1)

        @pl.when(i == 0)
        def _():
            dkv_ref[...] = jnp.zeros_like(dkv_ref)

        for h in range(MEM_HEADS):
            ks = slice(h * hd, (h + 1) * hd)
            vs = slice(MEM_WIDTH + h * hd, MEM_WIDTH + (h + 1) * hd)
            qh, kh, vh, doh = q_ref[:, ks], kv_ref[:, ks], kv_ref[:, vs], do_ref[:, ks]
            p = _mem_probs(qh, kh)
            dp = _dot(doh, vh, 'nt')
            ds = p * (dp - jnp.sum(p * dp, axis=-1, keepdims=True))
            dq_ref[:, ks] = (_dot(ds, kh) * MEM_SCALE).astype(BF16)
            dkv_ref[:, ks] += _dot(ds, qh, 'tn') * MEM_SCALE
            dkv_ref[:, vs] += _dot(p, doh, 'tn')

    return pl.pallas_call(
        body, name="mem_bwd", grid=(Bl, S // tq),
        in_specs=[pl.BlockSpec((None, tq, MEM_WIDTH), lambda b, i: (b, i, 5)),
                  pl.BlockSpec((None, MEM_LEN, 2 * MEM_WIDTH), lambda b, i: (b, 0, 0)),
                  pl.BlockSpec((None, tq, MEM_WIDTH), lambda b, i: (b, i, 0)), ANY],
        out_specs=[pl.BlockSpec((None, tq, MEM_WIDTH), lambda b, i: (b, i, 5)),
                   pl.BlockSpec((None, MEM_LEN, 2 * MEM_WIDTH), lambda b, i: (b, 0, 0))],
        out_shape=[jax.ShapeDtypeStruct(dproj3.shape, BF16),
                   jax.ShapeDtypeStruct((Bl, MEM_LEN, 2 * MEM_WIDTH), F32)],
        input_output_aliases={3: 0},
        compiler_params=_params(("parallel", "arbitrary")))(proj3, kv3, do, dproj3)


def _position():
    x, y, c = lax.axis_index("x"), lax.axis_index("y"), lax.axis_index("c")
    return x, y, c, 4 * x + 2 * y + c


def _device(idx):
    return ((idx >> 2) & 1, (idx >> 1) & 1, idx & 1)


def _half_block(ref, axis, shard_shape, k, h):
    R, Cn = shard_shape
    if axis == 1:
        return ref.at[pl.ds(h * (R // 2), R // 2), pl.ds(k * Cn, Cn)]
    return ref.at[pl.ds(k * R + h * (R // 2), R // 2), :]


def _block(ref, axis, shard_shape, k):
    R, Cn = shard_shape
    if axis == 1:
        return ref.at[:, pl.ds(k * Cn, Cn)]
    return ref.at[pl.ds(k * R, R), :]


def _half(ref, h):
    R = ref.shape[0]
    return ref.at[pl.ds(h * (R // 2), R // 2), :]


ANY = pl.BlockSpec(memory_space=pl.ANY)


HBM = pl.BlockSpec(memory_space=pltpu.HBM)
SEM = pl.BlockSpec(memory_space=pltpu.SEMAPHORE)
VMEM_WHOLE = pl.BlockSpec(memory_space=pltpu.VMEM)
EFFECT = pltpu.SideEffectType.DATAFLOW_SIDE_EFFECTING


def _in_hbm(a):
    return pltpu.with_memory_space_constraint(a, pltpu.HBM)


def _split_start(body, name, sources, lands, n_copies):
    n = len(sources)
    out_shape, out_specs = [], []
    for _ in range(n):
        out_shape += [pltpu.SemaphoreType.DMA((n_copies,)), pltpu.SemaphoreType.DMA((n_copies,))]
        out_specs += [SEM, SEM]
    out_shape += [pltpu.HBM(a.shape, a.dtype) for a in list(sources) + list(lands)]
    out_specs += [HBM] * (2 * n)
    out_shape.append(jax.ShapeDtypeStruct((8, 128), F32))
    out_specs.append(VMEM_WHOLE)

    def call_body(*refs):
        srcs, lnds = refs[:n], refs[n:2 * n]
        sems = refs[2 * n:4 * n]
        token = refs[-1]
        body(srcs, lnds, sems[0::2], sems[1::2])
        token[...] = jnp.zeros_like(token)

    res = pl.pallas_call(
        call_body, name=name, in_specs=[HBM] * (2 * n), out_specs=out_specs, out_shape=out_shape,
        input_output_aliases={i: 2 * n + i for i in range(2 * n)},
        compiler_params=pltpu.CompilerParams(has_side_effects=EFFECT))(
            *[_in_hbm(a) for a in list(sources) + list(lands)])
    pairs = [(res[2 * w], res[2 * w + 1], res[2 * n + w], res[3 * n + w]) for w in range(n)]
    return pairs, res[-1]


def _split_wait(body, name, pairs, after):
    n = len(pairs)

    def call_body(*refs):
        srcs, lnds = refs[:n], refs[n:2 * n]
        sems = refs[2 * n:4 * n]
        body(srcs, lnds, sems[0::2], sems[1::2])

    args = [_in_hbm(p[2]) for p in pairs] + [_in_hbm(p[3]) for p in pairs]
    for p in pairs:
        args += [p[0], p[1]]
    res = pl.pallas_call(
        call_body, name=name, in_specs=[HBM] * (2 * n) + [SEM] * (2 * n) + [ANY], out_specs=[HBM] * (2 * n),
        out_shape=[pltpu.HBM(a.shape, a.dtype) for a in args[:2 * n]],
        input_output_aliases={i: i for i in range(2 * n)},
        compiler_params=pltpu.CompilerParams(has_side_effects=EFFECT))(*args, after)
    return res[:n], res[n:]


def _place_block(shard, axis, chip_idx, name, after=None):
    R, Cn = shard.shape
    tr = _pick(R, (256, 176, 128, 64, 32, 16, 8))
    nblk = R // tr

    def body(k_ref, s_ref, *rest):
        rest[-1][...] = s_ref[...]

    if axis == 1:
        out_shape, out_index = (R, 4 * Cn), lambda i, k: (i, k[0])
    else:
        out_shape, out_index = (4 * R, Cn), lambda i, k: (k[0] * nblk + i, 0)
    extra = () if after is None else (after,)
    return pl.pallas_call(
        body, name=name,
        grid_spec=pltpu.PrefetchScalarGridSpec(
            num_scalar_prefetch=1, grid=(nblk,),
            in_specs=[pl.BlockSpec((tr, Cn), lambda i, k: (i, 0))] + [ANY] * len(extra),
            out_specs=pl.BlockSpec((tr, Cn), out_index)),
        out_shape=jax.ShapeDtypeStruct(out_shape, shard.dtype),
        compiler_params=_params(("parallel",)))(chip_idx, shard, *extra)


def _gather_copy(srcs, lnds, send, recv, axes, shapes, w, j, me):
    chip = me >> 1
    return (pltpu.make_async_remote_copy(
        src_ref=srcs[w], dst_ref=_block(lnds[w], axes[w], shapes[w], chip), send_sem=send[w].at[j],
        recv_sem=recv[w].at[j], device_id=_device(me ^ (2 * (j + 1))), device_id_type=MESH),
            pltpu.make_async_remote_copy(
        src_ref=srcs[w], dst_ref=_block(lnds[w], axes[w], shapes[w], chip ^ (j + 1)), send_sem=send[w].at[j],
        recv_sem=recv[w].at[j], device_id=_device(me ^ (2 * (j + 1))), device_id_type=MESH))


def _gather_start(shards, lands, axes, name):
    shapes = [s.shape for s in shards]

    def body(srcs, lnds, send, recv):
        x, y, c, me = _position()
        for w in range(len(shards)):
            for j in range(3):
                _gather_copy(srcs, lnds, send, recv, axes, shapes, w, j, me)[0].start()

    return _split_start(body, name, shards, lands, 3)


def _gather_wait(pairs, axes, after, name):
    shapes = [p[2].shape for p in pairs]

    def body(srcs, lnds, send, recv):
        x, y, c, me = _position()
        for w in range(len(pairs)):
            for j in range(3):
                sent, landed = _gather_copy(srcs, lnds, send, recv, axes, shapes, w, j, me)
                sent.wait_send()
                landed.wait_recv()

    return _split_wait(body, name, pairs, after)[1]


def _shard_shape(grad, axis):
    return (grad.shape[0], grad.shape[1] // 4) if axis == 1 else (grad.shape[0] // 4, grad.shape[1])


def _scatter_copy(srcs, lnds, send, recv, axes, shapes, w, m, me):
    peer = me ^ m
    return pltpu.make_async_remote_copy(
        src_ref=_half_block(srcs[w], axes[w], shapes[w], peer >> 1, peer & 1), dst_ref=lnds[w].at[m - 1],
        send_sem=send[w].at[m - 1], recv_sem=recv[w].at[m - 1], device_id=_device(peer), device_id_type=MESH)


def _scatter_start(grads, axes, name):
    shapes = [_shard_shape(g, a) for g, a in zip(grads, axes)]
    lands = [lax.empty((N_DEV - 1, R // 2, Cn), BF16) for R, Cn in shapes]

    def body(srcs, lnds, send, recv):
        x, y, c, me = _position()
        for w in range(len(grads)):
            for m in range(1, N_DEV):
                _scatter_copy(srcs, lnds, send, recv, axes, shapes, w, m, me).start()

    return _split_start(body, name, grads, lands, N_DEV - 1)


def _scatter_wait(pairs, axes, after):
    shapes = [_shard_shape(p[2], a) for p, a in zip(pairs, axes)]

    def body(srcs, lnds, send, recv):
        x, y, c, me = _position()
        for w in range(len(pairs)):
            for m in range(1, N_DEV):
                cp = _scatter_copy(srcs, lnds, send, recv, axes, shapes, w, m, me)
                cp.wait_send()
                cp.wait_recv()

    return _split_wait(body, "scatter_wait", pairs, after)


def _sum_partials(own, parts, half, name):
    R, Cn = own.shape
    tr = _pick(R, (256, 176, 128, 64, 32, 16, 8))
    nblk = R // tr

    def body(half_ref, own_ref, p_ref, o_ref):
        acc = own_ref[...].astype(F32)
        for d in range(N_DEV - 1):
            acc = acc + p_ref[d].astype(F32)
        o_ref[...] = acc

    return pl.pallas_call(
        body, name=name,
        grid_spec=pltpu.PrefetchScalarGridSpec(
            num_scalar_prefetch=1, grid=(nblk,),
            in_specs=[pl.BlockSpec((tr, Cn), lambda i, hr: (i, 0)),
                      pl.BlockSpec((N_DEV - 1, tr, Cn), lambda i, hr: (0, i, 0))],
            out_specs=pl.BlockSpec((tr, Cn), lambda i, hr: (hr[0] * nblk + i, 0))),
        out_shape=jax.ShapeDtypeStruct((2 * R, Cn), F32),
        compiler_params=_params(("parallel",)))(half, own, parts)


def _exchange_halves(grads):
    n = len(grads)

    def body(*refs):
        outs = refs[n:2 * n]
        send, recv = refs[2 * n:]
        x, y, c, me = _position()

        def copy(w, half):
            rows = _half(outs[w], half)
            return pltpu.make_async_remote_copy(src_ref=rows, dst_ref=rows, send_sem=send.at[w],
                                                recv_sem=recv.at[w], device_id=_device(me ^ 1), device_id_type=MESH)

        for w in range(n):
            copy(w, c).start()
        for w in range(n):
            copy(w, 1 - c).wait_recv()
        for w in range(n):
            copy(w, c).wait_send()

    return pl.pallas_call(
        body, name="exchange_halves", in_specs=[ANY] * n, out_specs=[ANY] * n,
        out_shape=[jax.ShapeDtypeStruct(a.shape, a.dtype) for a in grads],
        input_output_aliases={i: i for i in range(n)},
        scratch_shapes=[pltpu.SemaphoreType.DMA((n,)), pltpu.SemaphoreType.DMA((n,))],
        compiler_params=pltpu.CompilerParams(has_side_effects=True))(*grads)


def _all_reduce_small(vec):
    R, L = vec.shape

    def body(v_ref, o_ref, buf, send, recv):
        x, y, c, me = _position()
        buf[me] = v_ref[...]

        def copy(m, slot):
            return pltpu.make_async_remote_copy(src_ref=v_ref, dst_ref=buf.at[slot], send_sem=send.at[m - 1],
                                                recv_sem=recv.at[m - 1], device_id=_device(me ^ m),
                                                device_id_type=MESH)

        for m in range(1, N_DEV):
            copy(m, me).start()
        for m in range(1, N_DEV):
            copy(m, me ^ m).wait_recv()
        for m in range(1, N_DEV):
            copy(m, me).wait_send()
        acc = buf[0]
        for d in range(1, N_DEV):
            acc = acc + buf[d]
        o_ref[...] = acc

    vm = pl.BlockSpec(memory_space=pltpu.VMEM)
    return pl.pallas_call(
        body, name="all_reduce_small", in_specs=[vm], out_specs=vm, out_shape=jax.ShapeDtypeStruct((R, L), F32),
        scratch_shapes=[pltpu.VMEM((N_DEV, R, L), F32), pltpu.SemaphoreType.DMA((N_DEV - 1,)),
                        pltpu.SemaphoreType.DMA((N_DEV - 1,))],
        compiler_params=pltpu.CompilerParams(has_side_effects=True))(vec)


def _adamw(w, g, m, v, name):
    R, Cn = w.shape
    tr = _pick(R, (256, 176, 128, 64, 40, 32, 16, 8))

    def body(w_ref, g_ref, m_ref, v_ref, d_ref, nm_ref, nv_ref):
        gv = g_ref[...]
        nm = ADAM_B1 * m_ref[...] + (1.0 - ADAM_B1) * gv
        nv = ADAM_B2 * v_ref[...] + (1.0 - ADAM_B2) * (gv * gv)
        m_hat = nm / (1.0 - ADAM_B1 ** ADAM_STEP)
        v_hat = nv / (1.0 - ADAM_B2 ** ADAM_STEP)
        d_ref[...] = -ADAM_LR * (m_hat / (jnp.sqrt(v_hat) + ADAM_EPS) + ADAM_WD * w_ref[...])
        nm_ref[...] = nm
        nv_ref[...] = nv

    spec = pl.BlockSpec((tr, Cn), lambda i: (i, 0))
    return pl.pallas_call(
        body, name=name, grid=(R // tr,), in_specs=[spec] * 4, out_specs=[spec] * 3,
        out_shape=[jax.ShapeDtypeStruct((R, Cn), F32)] * 3, compiler_params=_params(("parallel",)))(w, g, m, v)


def _pack(arrays, rows):
    flat = jnp.concatenate([a.reshape(-1).astype(F32) for a in arrays])
    return jnp.pad(flat, (0, rows * 128 - flat.shape[0])).reshape(rows, 128)


def _unpack(packed, shapes):
    flat = packed.reshape(-1)
    out, off = [], 0
    for s in shapes:
        size = 1
        for d in s:
            size *= d
        out.append(flat[off:off + size].reshape(s))
        off += size
    return out


def _ffn_fwd(x, norm, arrived, tag, after=None):
    h = _rms_fwd(x, norm, f"{tag}_norm", after=after)
    w_up = arrived(f"{tag}_w_up", h)
    a, b, act = _ffn_up(h, w_up, f"{tag}_up")
    w_down = arrived(f"{tag}_w_down", act)
    out = _mm(act, w_down, 'nn', f"{tag}_down", F32, res=x, scale=0.5)
    return out, (h, a, b, act, w_up, w_down)


def _ffn_bwd(dout, x, norm, saved, tag, send):
    h, a, b, act, w_up, w_down = saved
    g_down = _mm(act, dout, 'tn', f"{tag}_down_dw", BF16, scale=0.5)
    token = send([f"{tag}_w_down"], [g_down])
    da, db = _ffn_down_bwd(dout, w_down, a, b, f"{tag}_down_dx", after=token)
    g_up = _dw_pieces(h, [da, db], f"{tag}_up_dw")
    token = send([f"{tag}_w_up"], [g_up])
    return _dx_rms_bwd([da, db], w_up, x, norm, dout, f"{tag}_up_dx", after=token)


def kernel(x, mem, ffn1_norm, ffn1_w_up, ffn1_w_down, mix_norm, mem_norm, w_in, b_gate, conv_dw_w, conv_dw_b, conv_ln_g, conv_ln_b, conv_w_pw, att_rel_bias, att_w_o, mem_w_kv, mem_w_o, w_out, ffn2_norm, ffn2_w_up, ffn2_w_down, final_norm, loss_target, m_ffn1_norm, m_ffn1_w_up, m_ffn1_w_down, m_mix_norm, m_mem_norm, m_w_in, m_b_gate, m_conv_dw_w, m_conv_dw_b, m_conv_ln_g, m_conv_ln_b, m_conv_w_pw, m_att_rel_bias, m_att_w_o, m_mem_w_kv, m_mem_w_o, m_w_out, m_ffn2_norm, m_ffn2_w_up, m_ffn2_w_down, m_final_norm, v_ffn1_norm, v_ffn1_w_up, v_ffn1_w_down, v_mix_norm, v_mem_norm, v_w_in, v_b_gate, v_conv_dw_w, v_conv_dw_b, v_conv_ln_g, v_conv_ln_b, v_conv_w_pw, v_att_rel_bias, v_att_w_o, v_mem_w_kv, v_mem_w_o, v_w_out, v_ffn2_norm, v_ffn2_w_up, v_ffn2_w_down, v_final_norm):
    given = dict(locals())
    wts = {n: given[n] for n in WEIGHTS}
    mom1 = {n: given["m_" + n] for n in WEIGHTS}
    mom2 = {n: given["v_" + n] for n in WEIGHTS}
    Bl, S, Dm = x.shape
    T = Bl * S
    x0 = x.reshape(T, Dm)
    tgt = loss_target.reshape(T, Dm)
    mem2 = mem.reshape(Bl * MEM_LEN, Dm)

    big_names = [n for n, _ in BIG]
    big_axes = [a for _, a in BIG]
    chip = 2 * lax.axis_index("x") + lax.axis_index("y")

    core = lax.axis_index("c")
    axis_of = dict(BIG)

    gather_groups = [['ffn1_w_up'], ['ffn1_w_down'], ['w_in', 'conv_dw_w'],
                     ['mem_w_kv', 'conv_w_pw', 'att_w_o', 'mem_w_o', 'w_out'], ['ffn2_w_up'], ['ffn2_w_down']]
    gather_names = [n for grp in gather_groups for n in grp]
    gather_axes = [axis_of.get(n, 1) for n in gather_names]
    shards = [jnp.pad(conv_dw_w[0], ((0, 1), (0, 0))) if n == 'conv_dw_w' else wts[n][0].astype(BF16)
              for n in gather_names]
    chip_idx = chip.reshape(1).astype(jnp.int32)
    first, first_token = _gather_start(
        shards[:1], [_place_block(shards[0], gather_axes[0], chip_idx, f"place_{gather_names[0]}")],
        gather_axes[:1], "gather_start_first")
    lands = [_place_block(sh, a, chip_idx, f"place_{n}", after=first_token)
             for sh, a, n in zip(shards[1:], gather_axes[1:], gather_names[1:])]
    rest, gather_token = _gather_start(shards[1:], lands, gather_axes[1:], "gather_start_rest")
    in_flight = dict(zip(gather_names, first + rest))
    full = {}

    def arrived(name, after):
        if name not in full:
            grp = next(grp for grp in gather_groups if name in grp)
            lands = _gather_wait([in_flight[n] for n in grp], [axis_of.get(n, 1) for n in grp], after,
                                 f"gather_wait_{grp[0]}")
            full.update(zip(grp, lands))
        return full[name]

    scattering = {}

    def send(names, grads):
        pairs, token = _scatter_start(grads, [axis_of[n] for n in names], f"scatter_start_{names[0]}")
        scattering.update(zip(names, pairs))
        return token

    final_g = final_norm.reshape(1, Dm)
    bias = _att_bias(att_rel_bias[0] + first_token[:1, :1])

    x1, ffn1_saved = _ffn_fwd(x0, ffn1_norm, arrived, "ffn1", after=gather_token)
    h = _rms_fwd(x1, mix_norm, "mix_norm")
    w_in_full = arrived('w_in', h)
    dw_full = full['conv_dw_w'][:CONV_KERNEL]
    proj = _mm(h, w_in_full, 'nn', "w_in", BF16)
    proj3 = proj.reshape(Bl, S, proj.shape[1])
    cv, c_act = _conv_fwd(proj3, dw_full, conv_dw_b, conv_ln_g, conv_ln_b)
    o_att = _att_fwd(proj3, bias)
    mem_h = _rms_fwd(mem2, mem_norm, "mem_norm")
    kv = _mm(mem_h, arrived('mem_w_kv', o_att), 'nn', "mem_kv", BF16)
    kv3 = kv.reshape(Bl, MEM_LEN, 2 * MEM_WIDTH)
    o_mem = _mem_fwd(proj3, kv3)
    c_act2, o_att2, o_mem2 = c_act.reshape(T, -1), o_att.reshape(T, -1), o_mem.reshape(T, -1)
    x2, yc, ya, ym = _mix_fwd(c_act2, o_att2, o_mem2, proj, b_gate, x1, full['conv_w_pw'], full['att_w_o'],
                              full['mem_w_o'], full['w_out'])
    x3, ffn2_saved = _ffn_fwd(x2, ffn2_norm, arrived, "ffn2")
    dx3, g_final, loss_vec = _final_fwd_bwd(x3, tgt, final_g)

    g = {}
    dx2, g['ffn2_norm'] = _ffn_bwd(dx3, x2, ffn2_norm, ffn2_saved, "ffn2", send)
    dgl, g['b_gate'], dc, doa, dom, g_pw, g_o, g_mo, g_out = _mix_bwd(
        dx2, yc, ya, ym, c_act2, o_att2, o_mem2, proj, b_gate, full['conv_w_pw'], full['att_w_o'],
        full['mem_w_o'], full['w_out'])
    token = send(['w_out', 'conv_w_pw', 'att_w_o', 'mem_w_o'], [g_out, g_pw, g_o, g_mo])
    dproj3 = dgl.reshape(Bl, S, -1)
    dproj3, g_dw, g['conv_dw_b'], g['conv_ln_g'], g['conv_ln_b'] = _conv_bwd(
        proj3, cv, dc.reshape(Bl, S, -1), dw_full, conv_ln_g, conv_ln_b, dproj3)
    dproj3, dk, dv, dbias = _att_bwd(proj3, doa.reshape(Bl, S, -1), bias, dproj3)
    g['att_rel_bias'] = _rel_bias_grad(dbias)
    dproj3, dkv = _mem_bwd(proj3, kv3, dom.reshape(Bl, S, -1), dproj3)
    dkv2 = dkv.reshape(Bl * MEM_LEN, 2 * MEM_WIDTH)
    g_kv = _mm(mem_h, dkv2, 'tn', "mem_kv_dw", BF16, after=token)
    dmem_h = _mm(dkv2, full['mem_w_kv'], 'nt', "mem_kv_dx", F32)
    _, g['mem_norm'] = _rms_bwd(mem2, mem_norm, dmem_h, dmem_h, "mem_norm_bwd")
    dkdv = jnp.concatenate([dk.reshape(T, -1), dv.reshape(T, -1)], axis=1)
    dproj = lax.dynamic_update_slice(dproj3.reshape(T, -1), dkdv, (0, 2 * CONV_WIDTH + ATT_WIDTH))
    token = send(['mem_w_kv', 'w_in'], [g_kv, _mm(h, dproj, 'tn', "w_in_dw", BF16)])
    dx1, g['mix_norm'] = _dx_rms_bwd([dproj], w_in_full, x1, mix_norm, dx2, "w_in_dx", after=token)
    dx0, g['ffn1_norm'] = _ffn_bwd(dx1, x0, ffn1_norm, ffn1_saved, "ffn1", send)
    g['final_norm'] = g_final

    sent, landed = _scatter_wait([scattering[n] for n in big_names], big_axes, dx0)
    halves = []
    half_idx = core.reshape(1).astype(jnp.int32)
    for n, a, own_full, parts in zip(big_names, big_axes, sent, landed):
        R, Cn = _shard_shape(own_full, a)
        start = (core * (R // 2), chip * Cn) if a == 1 else (chip * R + core * (R // 2), 0)
        own = lax.dynamic_slice(own_full, start, (R // 2, Cn))
        halves.append(_sum_partials(own, parts, half_idx, f"sum_{n}"))
    for n, sg in zip(big_names, _exchange_halves(halves)):
        g[n] = sg

    small_shapes = [wts[n].shape for n in SMALL]
    n_small = sum(int(wts[n].size) for n in SMALL)
    n_red = n_small + CONV_KERNEL * CONV_WIDTH + 1
    red = _all_reduce_small(_pack([g[n] for n in SMALL] + [g_dw, loss_vec[0, :1]], -(-n_red // 1024) * 8))
    red_list = _unpack(red, small_shapes + [(CONV_KERNEL, CONV_WIDTH), ()])
    for n, rg in zip(SMALL, red_list[:-2]):
        g[n] = rg
    loss = red_list[-1]
    dw_cols = conv_dw_w.shape[2]
    g['conv_dw_w'] = lax.dynamic_slice(red_list[-2], (0, chip * dw_cols), (CONV_KERNEL, dw_cols))[None]

    delta, new_m, new_v = {}, {}, {}
    for n in big_names:
        g[n] = g[n][None]
        d, nm, nv = _adamw(wts[n][0], g[n][0], mom1[n][0], mom2[n][0], f"adamw_{n}")
        delta[n], new_m[n], new_v[n] = d[None], nm[None], nv[None]
    rest = SMALL + ['conv_dw_w']
    rest_shapes = [wts[n].shape for n in rest]
    rows = -(-sum(int(wts[n].size) for n in rest) // 1024) * 8
    packed = [_pack([src[n] for n in rest], rows) for src in (wts, g, mom1, mom2)]
    for out, res in zip((delta, new_m, new_v), _adamw(*packed, "adamw_small")):
        for n, a in zip(rest, _unpack(res, rest_shapes)):
            out[n] = a

    grad_x = dx0.reshape(Bl, S, Dm)
    return (loss, grad_x, *[g[n] for n in WEIGHTS], *[delta[n] for n in WEIGHTS],
            *[new_m[n] for n in WEIGHTS], *[new_v[n] for n in WEIGHTS])
```

```python
import jax
import jax.numpy as jnp
from jax import lax
from jax.experimental import pallas as pl
from jax.experimental.pallas import tpu as pltpu

F32 = jnp.float32
BF16 = jnp.bfloat16

D_MODEL = 1024
D_FF = 2816
CHUNK = 64
LEFT_CHUNKS = 8
MAX_REL = 128
N_REL = (CHUNK - 1) + MAX_REL + 1
CONV_WIDTH = 512
CONV_KERNEL = 31
ATT_HEADS = 8
ATT_WIDTH = 512
MEM_LEN = 256
MEM_HEADS = 4
MEM_HEAD_DIM = 128
MEM_WIDTH = 512
EPS = 1e-6
MASK_VALUE = -1e30
ATT_SCALE = 64 ** -0.5
MEM_SCALE = 128 ** -0.5

ADAM_LR = 0.001
ADAM_B1 = 0.9
ADAM_B2 = 0.999
ADAM_EPS = 1e-08
ADAM_WD = 0.01
ADAM_STEP = 10

QB = 256
KW = 3 * QB
CONV_PAD = 32
CONV_TILE = 256

VMEM_LIMIT = 56 << 20
MXU_COLS = 256

WEIGHTS = ['ffn1_norm', 'ffn1_w_up', 'ffn1_w_down', 'mix_norm', 'mem_norm', 'w_in', 'b_gate', 'conv_dw_w',
           'conv_dw_b', 'conv_ln_g', 'conv_ln_b', 'conv_w_pw', 'att_rel_bias', 'att_w_o', 'mem_w_kv', 'mem_w_o',
           'w_out', 'ffn2_norm', 'ffn2_w_up', 'ffn2_w_down', 'final_norm']
BIG = [('ffn1_w_up', 1), ('ffn1_w_down', 0), ('w_in', 1), ('conv_w_pw', 1), ('att_w_o', 1), ('mem_w_kv', 0),
       ('mem_w_o', 1), ('w_out', 0), ('ffn2_w_up', 1), ('ffn2_w_down', 0)]
SMALL = ['ffn1_norm', 'mix_norm', 'mem_norm', 'b_gate', 'conv_dw_b', 'conv_ln_g', 'conv_ln_b', 'att_rel_bias',
         'ffn2_norm', 'final_norm']
N_CHIPS = 4
N_DEV = 8
MESH = pl.DeviceIdType.MESH


def _pick(n, cands):
    for c in cands:
        if n % c == 0:
            return c
    return n


def _sig(x):
    return 0.5 * jnp.tanh(0.5 * x) + 0.5


def _params(sem=None, vmem=VMEM_LIMIT):
    return pltpu.CompilerParams(dimension_semantics=sem, vmem_limit_bytes=vmem)


def _dot(a, b, mode='nn'):
    dims = {'nn': (((1,), (0,)), ((), ())), 'nt': (((1,), (1,)), ((), ())), 'tn': (((0,), (0,)), ((), ()))}[mode]
    return lax.dot_general(a.astype(BF16), b.astype(BF16), dims, preferred_element_type=F32)


def _mm(a, b, mode, name, out_dtype, res=None, scale=1.0, after=None):
    if mode == 'nn':
        (M, C), (_, N) = a.shape, b.shape
    elif mode == 'nt':
        (M, C), (N, _) = a.shape, b.shape
    else:
        (C, M), (_, N) = a.shape, b.shape
    tm = _pick(M, (1024, 1408, 512, 256, 128))
    tn = _pick(N, (1024, 1408, 512, 256, 128))
    tc = C if C <= 2816 else _pick(C, (2048, 1024, 1408, 512, 256, 128))
    nk = C // tc
    if mode == 'nn':
        a_spec = pl.BlockSpec((tm, tc), lambda i, j, k: (i, k))
        b_spec = pl.BlockSpec((tc, tn), lambda i, j, k: (k, j))
    elif mode == 'nt':
        a_spec = pl.BlockSpec((tm, tc), lambda i, j, k: (i, k))
        b_spec = pl.BlockSpec((tn, tc), lambda i, j, k: (j, k))
    else:
        a_spec = pl.BlockSpec((tc, tm), lambda i, j, k: (k, i))
        b_spec = pl.BlockSpec((tc, tn), lambda i, j, k: (k, j))
    o_spec = pl.BlockSpec((tm, tn), lambda i, j, k: (i, j))
    has_res = res is not None
    has_after = after is not None

    def body(*refs):
        a_ref, b_ref = refs[:2]
        r_ref = refs[2] if has_res else None
        o_ref, acc_ref = refs[-2:]
        k = pl.program_id(2)

        def finish(acc):
            if scale != 1.0:
                acc = acc * scale
            if r_ref is not None:
                acc = r_ref[...] + acc
            o_ref[...] = acc.astype(o_ref.dtype)

        if nk == 1:
            finish(_dot(a_ref[...], b_ref[...], mode))
        else:
            @pl.when(k == 0)
            def _():
                acc_ref[...] = jnp.zeros_like(acc_ref)

            acc_ref[...] += _dot(a_ref[...], b_ref[...], mode)

            @pl.when(k == nk - 1)
            def _():
                finish(acc_ref[...])

    in_specs = [a_spec, b_spec] + ([o_spec] if has_res else []) + ([ANY] if has_after else [])
    args = (a, b) + ((res,) if has_res else ()) + ((after,) if has_after else ())
    acc_shape = (tm, tn) if nk > 1 else (8, 128)
    return pl.pallas_call(
        body, name=name, grid=(M // tm, N // tn, nk), in_specs=in_specs, out_specs=o_spec,
        out_shape=jax.ShapeDtypeStruct((M, N), out_dtype), scratch_shapes=[pltpu.VMEM(acc_shape, F32)],
        compiler_params=_params(("parallel", "parallel", "arbitrary")))(*args)


def _row_tile(T):
    return _pick(T, (512, 256, 128, 64, 32, 16, 8))


def _rms_fwd(x, g, name, after=None):
    T, Dm = x.shape
    tm = _row_tile(T)

    def body(x_ref, g_ref, *rest):
        o_ref = rest[-1]
        xv = x_ref[...]
        r = lax.rsqrt(jnp.mean(xv * xv, axis=-1, keepdims=True) + EPS)
        o_ref[...] = ((xv * r) * g_ref[...]).astype(o_ref.dtype)

    extra = () if after is None else (after,)
    return pl.pallas_call(
        body, name=name, grid=(T // tm,),
        in_specs=[pl.BlockSpec((tm, Dm), lambda i: (i, 0)), pl.BlockSpec((1, Dm), lambda i: (0, 0))]
        + [ANY] * len(extra),
        out_specs=pl.BlockSpec((tm, Dm), lambda i: (i, 0)), out_shape=jax.ShapeDtypeStruct((T, Dm), BF16),
        compiler_params=_params(("parallel",)))(x, g, *extra)


def _rms_bwd(x, g, dh, dres, name):
    T, Dm = x.shape
    tm = _row_tile(T)

    def body(x_ref, g_ref, dh_ref, dr_ref, dx_ref, dg_ref):
        i = pl.program_id(0)
        xv = x_ref[...]
        r = lax.rsqrt(jnp.mean(xv * xv, axis=-1, keepdims=True) + EPS)
        xr = xv * r
        dh_v = dh_ref[...].astype(F32)
        dyg = dh_v * g_ref[...]
        dx = r * (dyg - xr * jnp.mean(dyg * xr, axis=-1, keepdims=True))
        dx_ref[...] = dr_ref[...] + dx

        @pl.when(i == 0)
        def _():
            dg_ref[...] = jnp.zeros_like(dg_ref)

        dg_ref[...] += jnp.sum(dh_v * xr, axis=0, keepdims=True)

    row = pl.BlockSpec((tm, Dm), lambda i: (i, 0))
    vec = pl.BlockSpec((1, Dm), lambda i: (0, 0))
    return pl.pallas_call(
        body, name=name, grid=(T // tm,), in_specs=[row, vec, row, row], out_specs=[row, vec],
        out_shape=[jax.ShapeDtypeStruct((T, Dm), F32), jax.ShapeDtypeStruct((1, Dm), F32)],
        compiler_params=_params(("arbitrary",)))(x, g, dh, dres)


def _final_fwd_bwd(x3, tgt, g):
    T, Dm = x3.shape
    tm = _row_tile(T)

    def body(x_ref, t_ref, g_ref, dx_ref, dg_ref, loss_ref):
        i = pl.program_id(0)
        xv = x_ref[...]
        gg = g_ref[...]
        r = lax.rsqrt(jnp.mean(xv * xv, axis=-1, keepdims=True) + EPS)
        xr = xv * r
        err = xr * gg - t_ref[...]
        dout = err * (1.0 / Dm)
        dyg = dout * gg
        dx_ref[...] = r * (dyg - xr * jnp.mean(dyg * xr, axis=-1, keepdims=True))

        @pl.when(i == 0)
        def _():
            dg_ref[...] = jnp.zeros_like(dg_ref)
            loss_ref[...] = jnp.zeros_like(loss_ref)

        dg_ref[...] += jnp.sum(dout * xr, axis=0, keepdims=True)
        loss_ref[...] += jnp.zeros_like(loss_ref) + (0.5 / Dm) * jnp.sum(err * err)

    row = pl.BlockSpec((tm, Dm), lambda i: (i, 0))
    vec = pl.BlockSpec((1, Dm), lambda i: (0, 0))
    one = pl.BlockSpec((1, 128), lambda i: (0, 0))
    return pl.pallas_call(
        body, name="final_fwd_bwd", grid=(T // tm,), in_specs=[row, row, vec], out_specs=[row, vec, one],
        out_shape=[jax.ShapeDtypeStruct((T, Dm), F32), jax.ShapeDtypeStruct((1, Dm), F32),
                   jax.ShapeDtypeStruct((1, 128), F32)],
        compiler_params=_params(("arbitrary",)))(x3, tgt, g)


def _ffn_up(h, w_up, name):
    T, K = h.shape
    Fh = w_up.shape[1] // 2
    tm = _pick(T, (1024, 512, 256, 128))
    tn = _pick(Fh, (1408, 512, 256, 128))
    nj = Fh // tn

    def body(h_ref, wa_ref, wb_ref, a_ref, b_ref, act_ref):
        hv = h_ref[...]
        for c0 in range(0, tn, MXU_COLS):
            cs = slice(c0, min(c0 + MXU_COLS, tn))
            a = _dot(hv, wa_ref[:, cs])
            b = _dot(hv, wb_ref[:, cs])
            a_ref[:, cs] = a.astype(BF16)
            b_ref[:, cs] = b.astype(BF16)
            act_ref[:, cs] = (a * _sig(a) * b).astype(BF16)

    out = pl.BlockSpec((tm, tn), lambda i, j: (i, j))
    return pl.pallas_call(
        body, name=name, grid=(T // tm, nj),
        in_specs=[pl.BlockSpec((tm, K), lambda i, j: (i, 0)), pl.BlockSpec((K, tn), lambda i, j: (0, j)),
                  pl.BlockSpec((K, tn), lambda i, j: (0, j + nj))],
        out_specs=[out, out, out], out_shape=[jax.ShapeDtypeStruct((T, Fh), BF16)] * 3,
        compiler_params=_params(("parallel", "parallel")))(h, w_up, w_up)


def _ffn_down(act, w_down, x, name, next_g=None):
    T, Fh = act.shape
    Dm = w_down.shape[1]
    tm = _pick(T, (1024, 512, 256, 128))
    rows = _pick(tm, (256, 128))
    with_norm = next_g is not None

    def body(a_ref, w_ref, x_ref, *rest):
        o_ref = rest[-2] if with_norm else rest[-1]
        o_ref[...] = x_ref[...] + 0.5 * _dot(a_ref[...], w_ref[...])
        if with_norm:
            g_ref, h_ref = rest[0], rest[-1]

            def chunk(c, carry):
                rs = pl.ds(pl.multiple_of(c * rows, rows), rows)
                ov = o_ref[rs, :]
                r = lax.rsqrt(jnp.mean(ov * ov, axis=-1, keepdims=True) + EPS)
                h_ref[rs, :] = ((ov * r) * g_ref[...]).astype(BF16)
                return carry

            lax.fori_loop(0, tm // rows, chunk, 0)

    row = pl.BlockSpec((tm, Dm), lambda i: (i, 0))
    in_specs = [pl.BlockSpec((tm, Fh), lambda i: (i, 0)), pl.BlockSpec((Fh, Dm), lambda i: (0, 0)), row]
    out_specs, out_shape = [row], [jax.ShapeDtypeStruct((T, Dm), F32)]
    args = (act, w_down, x)
    if with_norm:
        in_specs.append(pl.BlockSpec((1, Dm), lambda i: (0, 0)))
        out_specs.append(row)
        out_shape.append(jax.ShapeDtypeStruct((T, Dm), BF16))
        args += (next_g,)
    res = pl.pallas_call(body, name=name, grid=(T // tm,), in_specs=in_specs, out_specs=out_specs,
                         out_shape=out_shape, compiler_params=_params(("parallel",)))(*args)
    return (res[0], res[1]) if with_norm else (res[0], None)


def _ffn_down_bwd(dout, w_down, a, b, name, after=None):
    T, Dm = dout.shape
    Fh = w_down.shape[0]
    tm = _pick(T, (1024, 512, 256, 128))
    tn = _pick(Fh, (1408, 512, 256, 128))

    def body(d_ref, w_ref, a_ref, b_ref, *rest):
        da_ref, db_ref = rest[-2:]
        dv = (d_ref[...] * 0.5).astype(BF16)
        for c0 in range(0, tn, MXU_COLS):
            cs = slice(c0, min(c0 + MXU_COLS, tn))
            dact = _dot(dv, w_ref[cs, :], 'nt')
            av = a_ref[:, cs].astype(F32)
            bv = b_ref[:, cs].astype(F32)
            s = _sig(av)
            da_ref[:, cs] = (dact * bv * s * (1.0 + av * (1.0 - s))).astype(BF16)
            db_ref[:, cs] = (dact * av * s).astype(BF16)

    tile = pl.BlockSpec((tm, tn), lambda i, j: (i, j))
    extra = () if after is None else (after,)
    return pl.pallas_call(
        body, name=name, grid=(T // tm, Fh // tn),
        in_specs=[pl.BlockSpec((tm, Dm), lambda i, j: (i, 0)), pl.BlockSpec((tn, Dm), lambda i, j: (j, 0)),
                  tile, tile] + [ANY] * len(extra),
        out_specs=[tile, tile], out_shape=[jax.ShapeDtypeStruct((T, Fh), BF16)] * 2,
        compiler_params=_params(("parallel", "parallel")))(dout, w_down, a, b, *extra)


def _dx_rms_bwd(pieces, w, x, g, dres, name, after=None):
    T, Dm = x.shape
    width = pieces[0].shape[1]
    tm = _pick(T, (1024, 512, 256, 128))
    tc = _pick(width, (1408, 2048, 1024, 512, 256, 128))
    per = width // tc
    nk = per * len(pieces)
    npc = len(pieces)
    rows = _pick(tm, (256, 128))

    def body(*refs):
        p_refs = refs[:npc]
        w_ref, x_hbm, g_ref, dr_hbm = refs[npc:npc + 4]
        dx_ref, dg_ref, acc_ref, x_buf, dr_buf, sems = refs[-6:]
        i = pl.program_id(0)
        k = pl.program_id(1)
        tile = pl.ds(pl.multiple_of(i * tm, tm), tm)
        fetch_x = pltpu.make_async_copy(x_hbm.at[tile, :], x_buf, sems.at[0])
        fetch_dr = pltpu.make_async_copy(dr_hbm.at[tile, :], dr_buf, sems.at[1])

        @pl.when(k == 0)
        def _():
            fetch_x.start()
            fetch_dr.start()
            acc_ref[...] = jnp.zeros_like(acc_ref)

        @pl.when((i == 0) & (k == 0))
        def _():
            dg_ref[...] = jnp.zeros_like(dg_ref)

        for p in range(npc):
            @pl.when((k >= p * per) & (k < (p + 1) * per))
            def _(p=p):
                acc_ref[...] += _dot(p_refs[p][...], w_ref[...], 'nt')

        @pl.when(k == nk - 1)
        def _():
            fetch_x.wait()
            fetch_dr.wait()

            def chunk(c, carry):
                rs = pl.ds(pl.multiple_of(c * rows, rows), rows)
                dh = acc_ref[rs, :]
                xv = x_buf[rs, :]
                r = lax.rsqrt(jnp.mean(xv * xv, axis=-1, keepdims=True) + EPS)
                xr = xv * r
                dyg = dh * g_ref[...]
                dx_ref[rs, :] = dr_buf[rs, :] + r * (dyg - xr * jnp.mean(dyg * xr, axis=-1, keepdims=True))
                dg_ref[...] += jnp.sum(dh * xr, axis=0, keepdims=True)
                return carry

            lax.fori_loop(0, tm // rows, chunk, 0)

    def piece_spec(p):
        return pl.BlockSpec((tm, tc), lambda i, k: (i, jnp.clip(k - p * per, 0, per - 1)))

    row = pl.BlockSpec((tm, Dm), lambda i, k: (i, 0))
    vec = pl.BlockSpec((1, Dm), lambda i, k: (0, 0))
    extra = () if after is None else (after,)
    return pl.pallas_call(
        body, name=name, grid=(T // tm, nk),
        in_specs=[piece_spec(p) for p in range(npc)] + [pl.BlockSpec((Dm, tc), lambda i, k: (0, k)), ANY, vec, ANY]
        + [ANY] * len(extra),
        out_specs=[row, vec], out_shape=[jax.ShapeDtypeStruct((T, Dm), F32), jax.ShapeDtypeStruct((1, Dm), F32)],
        scratch_shapes=[pltpu.VMEM((tm, Dm), F32), pltpu.VMEM((tm, Dm), F32), pltpu.VMEM((tm, Dm), F32),
                        pltpu.SemaphoreType.DMA((2,))],
        compiler_params=_params(("arbitrary", "arbitrary")))(*pieces, w, x, g, dres, *extra)


def _dw_pieces(a, pieces, name):
    C, M = a.shape
    width = pieces[0].shape[1]
    npc = len(pieces)
    tm = _pick(M, (1024, 512, 256, 128))
    tn = _pick(width, (1408, 1024, 512, 256, 128))
    tc = _pick(C, (2048, 1024, 512, 256, 128))
    per = width // tn
    nk = C // tc

    def body(*refs):
        a_ref = refs[0]
        p_refs = refs[1:1 + npc]
        o_ref, acc_ref = refs[-2:]
        j = pl.program_id(1)
        k = pl.program_id(2)

        @pl.when(k == 0)
        def _():
            acc_ref[...] = jnp.zeros_like(acc_ref)

        for p in range(npc):
            @pl.when((j >= p * per) & (j < (p + 1) * per))
            def _(p=p):
                acc_ref[...] += _dot(a_ref[...], p_refs[p][...], 'tn')

        @pl.when(k == nk - 1)
        def _():
            o_ref[...] = acc_ref[...].astype(o_ref.dtype)

    def piece_spec(p):
        return pl.BlockSpec((tc, tn), lambda i, j, k: (k, jnp.clip(j - p * per, 0, per - 1)))

    return pl.pallas_call(
        body, name=name, grid=(M // tm, per * npc, nk),
        in_specs=[pl.BlockSpec((tc, tm), lambda i, j, k: (k, i))] + [piece_spec(p) for p in range(npc)],
        out_specs=pl.BlockSpec((tm, tn), lambda i, j, k: (i, j)),
        out_shape=jax.ShapeDtypeStruct((M, width * npc), BF16), scratch_shapes=[pltpu.VMEM((tm, tn), F32)],
        compiler_params=_params(("parallel", "parallel", "arbitrary")))(a, *pieces)


def _mix_fwd(c_act, o_att, o_mem, proj, b_gate, x1, w_pw, w_o, w_mo, w_out, next_g):
    T, Dm = x1.shape
    W = c_act.shape[1]
    tm = _pick(T, (256, 128, 64, 32, 16, 8))

    def body(c_ref, oa_ref, om_ref, gl_ref, bg_ref, x1_ref, wpw_ref, wo_ref, wmo_ref, wout_ref, ng_ref,
             x2_ref, yc_ref, ya_ref, ym_ref, h_ref):
        yc = _dot(c_ref[...], wpw_ref[...])
        ya = _dot(oa_ref[...], wo_ref[...])
        ym = _dot(om_ref[...], wmo_ref[...])
        g = _sig(gl_ref[...].astype(F32) + bg_ref[...])
        y = g[:, :Dm] * yc + g[:, Dm:2 * Dm] * ya + g[:, 2 * Dm:] * ym
        x2 = x1_ref[...] + _dot(y, wout_ref[...])
        x2_ref[...] = x2
        r = lax.rsqrt(jnp.mean(x2 * x2, axis=-1, keepdims=True) + EPS)
        h_ref[...] = ((x2 * r) * ng_ref[...]).astype(BF16)
        yc_ref[...] = yc.astype(BF16)
        ya_ref[...] = ya.astype(BF16)
        ym_ref[...] = ym.astype(BF16)

    rowW = pl.BlockSpec((tm, W), lambda i: (i, 0))
    rowD = pl.BlockSpec((tm, Dm), lambda i: (i, 0))
    full = lambda s: pl.BlockSpec(s, lambda i: (0, 0))
    return pl.pallas_call(
        body, name="mix_fwd", grid=(T // tm,),
        in_specs=[rowW, rowW, rowW, pl.BlockSpec((tm, 3 * Dm), lambda i: (i, 1)), full((1, 3 * Dm)), rowD,
                  full((W, Dm)), full((W, Dm)), full((W, Dm)), full((Dm, Dm)), full((1, Dm))],
        out_specs=[rowD] * 5,
        out_shape=[jax.ShapeDtypeStruct((T, Dm), F32)] + [jax.ShapeDtypeStruct((T, Dm), BF16)] * 4,
        compiler_params=_params(("parallel",)))(c_act, o_att, o_mem, proj, b_gate, x1, w_pw, w_o, w_mo, w_out,
                                                next_g)


def _mix_bwd(dx2, yc, ya, ym, c_act, o_att, o_mem, proj, b_gate, w_pw, w_o, w_mo, w_out):
    T, Dm = dx2.shape
    W = w_pw.shape[0]
    tm = _pick(T, (256, 128, 64, 32, 16, 8))
    nt = T // tm

    def body(dx_ref, yc_ref, ya_ref, ym_ref, c_ref, oa_ref, om_ref, gl_ref, bg_ref, wpw_ref, wo_ref, wmo_ref,
             wout_ref, dgl_ref, dbg_ref, dc_ref, doa_ref, dom_ref, gpw_ref, go_ref, gmo_ref, gout_ref,
             apw, ao, amo, aout):
        i = pl.program_id(0)

        @pl.when(i == 0)
        def _():
            dbg_ref[...] = jnp.zeros_like(dbg_ref)
            for acc in (apw, ao, amo, aout):
                acc[...] = jnp.zeros_like(acc)

        dxv = dx_ref[...].astype(BF16)
        dy = _dot(dxv, wout_ref[...], 'nt')
        g = _sig(gl_ref[...].astype(F32) + bg_ref[...])
        branches = ((yc_ref, c_ref, wpw_ref, dc_ref, apw), (ya_ref, oa_ref, wo_ref, doa_ref, ao),
                    (ym_ref, om_ref, wmo_ref, dom_ref, amo))
        y = jnp.zeros((tm, Dm), F32)
        for n, (y_ref, in_ref, w_ref, dk_ref, acc) in enumerate(branches):
            gk = g[:, n * Dm:(n + 1) * Dm]
            yk = y_ref[...].astype(F32)
            dyk = dy * gk
            dgl = dyk * yk * (1.0 - gk)
            dgl_ref[:, n * Dm:(n + 1) * Dm] = dgl.astype(BF16)
            dbg_ref[:, n * Dm:(n + 1) * Dm] += jnp.sum(dgl, axis=0, keepdims=True)
            dyk = dyk.astype(BF16)
            dk_ref[...] = _dot(dyk, w_ref[...], 'nt').astype(BF16)
            acc[...] += _dot(in_ref[...], dyk, 'tn')
            y = y + gk * yk
        aout[...] += _dot(y, dxv, 'tn')

        @pl.when(i == nt - 1)
        def _():
            for acc, out in ((apw, gpw_ref), (ao, go_ref), (amo, gmo_ref), (aout, gout_ref)):
                out[...] = acc[...].astype(BF16)

    rowW = pl.BlockSpec((tm, W), lambda i: (i, 0))
    rowD = pl.BlockSpec((tm, Dm), lambda i: (i, 0))
    full = lambda s: pl.BlockSpec(s, lambda i: (0, 0))
    return pl.pallas_call(
        body, name="mix_bwd", grid=(nt,),
        in_specs=[rowD, rowD, rowD, rowD, rowW, rowW, rowW, pl.BlockSpec((tm, 3 * Dm), lambda i: (i, 1)),
                  full((1, 3 * Dm)), full((W, Dm)), full((W, Dm)), full((W, Dm)), full((Dm, Dm))],
        out_specs=[pl.BlockSpec((tm, 3 * Dm), lambda i: (i, 1)), full((1, 3 * Dm)), rowW, rowW, rowW,
                   full((W, Dm)), full((W, Dm)), full((W, Dm)), full((Dm, Dm))],
        out_shape=[jax.ShapeDtypeStruct((T, 6 * Dm), BF16), jax.ShapeDtypeStruct((1, 3 * Dm), F32)]
        + [jax.ShapeDtypeStruct((T, W), BF16)] * 3 + [jax.ShapeDtypeStruct((W, Dm), BF16)] * 3
        + [jax.ShapeDtypeStruct((Dm, Dm), BF16)],
        scratch_shapes=[pltpu.VMEM((W, Dm), F32)] * 3 + [pltpu.VMEM((Dm, Dm), F32)],
        compiler_params=_params(("arbitrary",)))(dx2, yc, ya, ym, c_act, o_att, o_mem, proj, b_gate, w_pw, w_o,
                                                 w_mo, w_out)


def _ln_swish(cv, lg, lb):
    mu = jnp.mean(cv, axis=-1, keepdims=True)
    xc = cv - mu
    r = lax.rsqrt(jnp.mean(xc * xc, axis=-1, keepdims=True) + EPS)
    n = xc * r
    l = n * lg + lb
    return r, n, l


def _shift_copies(src, r0, win, shifts):
    win[...] = src[pl.ds(r0, CONV_TILE + CONV_PAD + 8), :]
    for s in range(8):
        shifts[s] = win[s:s + CONV_TILE + CONV_PAD, :]


def _tap(shifts, d):
    return shifts[d % 8, d - d % 8:d - d % 8 + CONV_TILE, :]


def _conv_fwd(proj3, dw_w, dw_b, ln_g, ln_b):
    Bl, S, _ = proj3.shape
    C, K, TS, PAD = CONV_WIDTH, CONV_KERNEL, CONV_TILE, CONV_PAD
    nt = S // TS

    def body(u_ref, w_ref, b_ref, lg_ref, lb_ref, cv_ref, c_ref, vbuf, win, shifts):
        vbuf[0:PAD, :] = jnp.zeros((PAD, C), F32)
        vbuf[S + PAD:S + PAD + 8, :] = jnp.zeros((8, C), F32)

        def glu(t, carry):
            r0 = pl.multiple_of(t * TS, TS)
            u = u_ref[pl.ds(r0, TS), :].astype(F32)
            vbuf[pl.ds(PAD + r0, TS), :] = u[:, :C] * _sig(u[:, C:])
            return carry

        lax.fori_loop(0, nt, glu, 0)

        def conv(t, carry):
            r0 = pl.multiple_of(t * TS, TS)
            _shift_copies(vbuf, r0, win, shifts)
            acc = jnp.zeros((TS, C), F32)
            for j in range(K):
                acc = acc + w_ref[j:j + 1, :] * _tap(shifts, PAD - (K - 1) + j)
            cv = acc + b_ref[...]
            cv_ref[pl.ds(r0, TS), :] = cv
            _, _, l = _ln_swish(cv, lg_ref[...], lb_ref[...])
            c_ref[pl.ds(r0, TS), :] = (l * _sig(l)).astype(BF16)
            return carry

        lax.fori_loop(0, nt, conv, 0)

    vec = pl.BlockSpec((1, C), lambda b: (0, 0))
    return pl.pallas_call(
        body, name="conv_fwd", grid=(Bl,),
        in_specs=[pl.BlockSpec((None, S, 2 * C), lambda b: (b, 0, 0)), pl.BlockSpec((K, C), lambda b: (0, 0)),
                  vec, vec, vec],
        out_specs=[pl.BlockSpec((None, S, C), lambda b: (b, 0, 0))] * 2,
        out_shape=[jax.ShapeDtypeStruct((Bl, S, C), F32), jax.ShapeDtypeStruct((Bl, S, C), BF16)],
        scratch_shapes=[pltpu.VMEM((S + PAD + 8, C), F32), pltpu.VMEM((TS + PAD + 8, C), F32),
                        pltpu.VMEM((8, TS + PAD, C), F32)],
        compiler_params=_params(("parallel",)))(proj3, dw_w, dw_b, ln_g, ln_b)


def _conv_bwd(proj3, cv, dc, dw_w, ln_g, ln_b, dproj3):
    Bl, S, _ = proj3.shape
    C, K, TS, PAD = CONV_WIDTH, CONV_KERNEL, CONV_TILE, CONV_PAD
    nt = S // TS

    def body(u_ref, cv_ref, dc_ref, w_ref, lg_ref, lb_ref, through_ref, du_ref, dw_ref, db_ref, dlg_ref, dlb_ref,
             vbuf, gbuf, win, shifts, dwacc):
        b = pl.program_id(0)

        @pl.when(b == 0)
        def _():
            dw_ref[...] = jnp.zeros_like(dw_ref)
            db_ref[...] = jnp.zeros_like(db_ref)
            dlg_ref[...] = jnp.zeros_like(dlg_ref)
            dlb_ref[...] = jnp.zeros_like(dlb_ref)

        vbuf[0:PAD, :] = jnp.zeros((PAD, C), F32)
        vbuf[S + PAD:S + PAD + 8, :] = jnp.zeros((8, C), F32)
        gbuf[S:S + PAD + 8, :] = jnp.zeros((PAD + 8, C), F32)
        dwacc[...] = jnp.zeros_like(dwacc)

        def norm_bwd(t, carry):
            r0 = pl.multiple_of(t * TS, TS)
            u = u_ref[pl.ds(r0, TS), :].astype(F32)
            vbuf[pl.ds(PAD + r0, TS), :] = u[:, :C] * _sig(u[:, C:])
            r, n, l = _ln_swish(cv_ref[pl.ds(r0, TS), :], lg_ref[...], lb_ref[...])
            s = _sig(l)
            dl = dc_ref[pl.ds(r0, TS), :].astype(F32) * s * (1.0 + l * (1.0 - s))
            dlg_ref[...] += jnp.sum(dl * n, axis=0, keepdims=True)
            dlb_ref[...] += jnp.sum(dl, axis=0, keepdims=True)
            dn = dl * lg_ref[...]
            dcv = r * (dn - jnp.mean(dn, axis=-1, keepdims=True) - n * jnp.mean(dn * n, axis=-1, keepdims=True))
            gbuf[pl.ds(r0, TS), :] = dcv
            db_ref[...] += jnp.sum(dcv, axis=0, keepdims=True)
            return carry

        lax.fori_loop(0, nt, norm_bwd, 0)

        def conv_bwd(t, carry):
            r0 = pl.multiple_of(t * TS, TS)
            _shift_copies(gbuf, r0, win, shifts)
            dv = jnp.zeros((TS, C), F32)
            for j in range(K):
                dv = dv + w_ref[j:j + 1, :] * _tap(shifts, K - 1 - j)
            u = u_ref[pl.ds(r0, TS), :].astype(F32)
            a, g = u[:, :C], u[:, C:]
            s = _sig(g)
            du_ref[pl.ds(r0, TS), 0:C] = (dv * s).astype(BF16)
            du_ref[pl.ds(r0, TS), C:2 * C] = (dv * a * s * (1.0 - s)).astype(BF16)
            dcv = gbuf[pl.ds(r0, TS), :]
            _shift_copies(vbuf, r0, win, shifts)
            for j in range(K):
                prod = dcv * _tap(shifts, PAD - (K - 1) + j)
                dwacc[j] += jnp.sum(prod.reshape(TS // 8, 8, C), axis=0)
            return carry

        lax.fori_loop(0, nt, conv_bwd, 0)
        dw_ref[...] += jnp.sum(dwacc[...], axis=1)

    vec = pl.BlockSpec((1, C), lambda b: (0, 0))
    seq = lambda w: pl.BlockSpec((None, S, w), lambda b: (b, 0, 0))
    return pl.pallas_call(
        body, name="conv_bwd", grid=(Bl,),
        in_specs=[seq(2 * C), seq(C), seq(C), pl.BlockSpec((K, C), lambda b: (0, 0)), vec, vec, ANY],
        out_specs=[seq(2 * C), pl.BlockSpec((K, C), lambda b: (0, 0)), vec, vec, vec],
        out_shape=[jax.ShapeDtypeStruct(dproj3.shape, BF16), jax.ShapeDtypeStruct((K, C), F32)]
        + [jax.ShapeDtypeStruct((1, C), F32)] * 3,
        input_output_aliases={6: 0},
        scratch_shapes=[pltpu.VMEM((S + PAD + 8, C), F32), pltpu.VMEM((S + PAD + 8, C), F32),
                        pltpu.VMEM((TS + PAD + 8, C), F32), pltpu.VMEM((8, TS + PAD, C), F32),
                        pltpu.VMEM((K, 8, C), F32)],
        compiler_params=_params(("arbitrary",)))(proj3, cv, dc, dw_w, ln_g, ln_b, dproj3)


def _att_bias(rel_bias):
    H = rel_bias.shape[0]
    Wd = KW + QB
    c = jnp.arange(Wd + 1)
    by_offset = rel_bias[:, jnp.clip(KW - c, -(CHUNK - 1), MAX_REL) + (CHUNK - 1)]
    flat = jnp.broadcast_to(by_offset[:, None, :], (H, QB, Wd + 1)).reshape(H, QB * (Wd + 1))
    skew = jnp.pad(flat, ((0, 0), (0, (QB + 1) * Wd - QB * (Wd + 1)))).reshape(H, QB + 1, Wd)[:, :QB, QB:]
    qi = jnp.arange(QB)[:, None]
    kj = jnp.arange(KW)[None, :]
    dchunk = ((KW - QB) + qi) // CHUNK - kj // CHUNK
    band = (dchunk >= 0) & (dchunk <= LEFT_CHUNKS)
    return jnp.where(band[None], skew, MASK_VALUE)


def _head_masks():
    lane = lax.broadcasted_iota(jnp.int32, (1, 128), 1)
    return (lane < 64, lane >= 64)


def _att_probs(qh, k2, bias, valid):
    s = _dot(qh, k2, 'nt') * ATT_SCALE + bias
    s = jnp.where(valid, s, MASK_VALUE)
    e = jnp.exp(s - jnp.max(s, axis=-1, keepdims=True))
    return e * (1.0 / jnp.sum(e, axis=-1, keepdims=True))


def _att_specs(S, q_col):
    nb = S // QB
    q_spec = pl.BlockSpec((None, QB, ATT_WIDTH), lambda b, i: (b, jnp.minimum(i, nb - 1), q_col))

    def kv_spec(col, kb):
        return pl.BlockSpec((None, QB, ATT_WIDTH),
                            lambda b, i: (b, jnp.clip(i - 2 + kb, 0, nb - 1), col))

    return q_spec, [kv_spec(3, kb) for kb in range(3)], [kv_spec(4, kb) for kb in range(3)]


def _att_fwd(proj3, bias):
    Bl, S, _ = proj3.shape
    nb = S // QB
    q_spec, k_specs, v_specs = _att_specs(S, 2)

    def body(q_ref, k0, k1, k2r, v0, v1, v2r, bias_ref, o_ref):
        i = pl.program_id(1)
        masks = _head_masks()
        valid = lax.broadcasted_iota(jnp.int32, (QB, KW), 1) >= (2 - i) * QB
        for pr in range(ATT_HEADS // 2):
            ls = slice(128 * pr, 128 * (pr + 1))
            q2 = q_ref[:, ls]
            k2 = jnp.concatenate([k0[:, ls], k1[:, ls], k2r[:, ls]], axis=0)
            v2 = jnp.concatenate([v0[:, ls], v1[:, ls], v2r[:, ls]], axis=0)
            o2 = jnp.zeros((QB, 128), F32)
            for hh in range(2):
                p = _att_probs(jnp.where(masks[hh], q2, 0), k2, bias_ref[2 * pr + hh], valid)
                o2 = o2 + _dot(p, jnp.where(masks[hh], v2, 0))
            o_ref[:, ls] = o2.astype(BF16)

    return pl.pallas_call(
        body, name="att_fwd", grid=(Bl, nb),
        in_specs=[q_spec] + k_specs + v_specs + [pl.BlockSpec((ATT_HEADS, QB, KW), lambda b, i: (0, 0, 0))],
        out_specs=pl.BlockSpec((None, QB, ATT_WIDTH), lambda b, i: (b, i, 0)),
        out_shape=jax.ShapeDtypeStruct((Bl, S, ATT_WIDTH), BF16),
        compiler_params=_params(("parallel", "arbitrary")))(*([proj3] * 7), bias)


def _att_bwd(proj3, do, bias, dproj3):
    Bl, S, _ = proj3.shape
    nb = S // QB
    q_spec, k_specs, v_specs = _att_specs(S, 2)
    do_spec = pl.BlockSpec((None, QB, ATT_WIDTH), lambda b, i: (b, jnp.minimum(i, nb - 1), 0))
    kv_out = pl.BlockSpec((None, QB, ATT_WIDTH), lambda b, i: (b, jnp.clip(i - 2, 0, nb - 1), 0))
    bias_spec = pl.BlockSpec((ATT_HEADS, QB, KW), lambda b, i: (0, 0, 0))

    def body(q_ref, k0, k1, k2r, v0, v1, v2r, do_ref, bias_ref, through_ref, dq_ref, dk_ref, dv_ref, db_ref,
             dkw, dvw):
        b = pl.program_id(0)
        i = pl.program_id(1)

        @pl.when((b == 0) & (i == 0))
        def _():
            db_ref[...] = jnp.zeros_like(db_ref)

        @pl.when(i == 0)
        def _():
            dkw[...] = jnp.zeros_like(dkw)
            dvw[...] = jnp.zeros_like(dvw)

        @pl.when(i < nb)
        def _():
            masks = _head_masks()
            valid = lax.broadcasted_iota(jnp.int32, (QB, KW), 1) >= (2 - i) * QB
            for pr in range(ATT_HEADS // 2):
                ls = slice(128 * pr, 128 * (pr + 1))
                q2 = q_ref[:, ls]
                do2 = do_ref[:, ls]
                k2 = jnp.concatenate([k0[:, ls], k1[:, ls], k2r[:, ls]], axis=0)
                v2 = jnp.concatenate([v0[:, ls], v1[:, ls], v2r[:, ls]], axis=0)
                dq2 = jnp.zeros((QB, 128), F32)
                dk2 = jnp.zeros((KW, 128), F32)
                dv2 = jnp.zeros((KW, 128), F32)
                for hh in range(2):
                    h = 2 * pr + hh
                    qh = jnp.where(masks[hh], q2, 0)
                    doh = jnp.where(masks[hh], do2, 0)
                    p = _att_probs(qh, k2, bias_ref[h], valid)
                    dp = _dot(doh, v2, 'nt')
                    ds = p * (dp - jnp.sum(p * dp, axis=-1, keepdims=True))
                    db_ref[h] += ds
                    dq2 = dq2 + _dot(ds, jnp.where(masks[hh], k2, 0))
                    dk2 = dk2 + _dot(ds, qh, 'tn')
                    dv2 = dv2 + _dot(p, doh, 'tn')
                dq_ref[:, ls] = (dq2 * ATT_SCALE).astype(BF16)
                dkw[:, ls] += dk2 * ATT_SCALE
                dvw[:, ls] += dv2

        dk_ref[...] = dkw[0:QB, :].astype(BF16)
        dv_ref[...] = dvw[0:QB, :].astype(BF16)
        for buf in (dkw, dvw):
            rest = buf[QB:KW, :]
            buf[0:KW - QB, :] = rest
            buf[KW - QB:KW, :] = jnp.zeros((QB, ATT_WIDTH), F32)

    blk = jax.ShapeDtypeStruct((Bl, S, ATT_WIDTH), BF16)
    return pl.pallas_call(
        body, name="att_bwd", grid=(Bl, nb + 2),
        in_specs=[q_spec] + k_specs + v_specs + [do_spec, bias_spec, ANY],
        out_specs=[q_spec, kv_out, kv_out, bias_spec],
        out_shape=[jax.ShapeDtypeStruct(dproj3.shape, BF16), blk, blk,
                   jax.ShapeDtypeStruct((ATT_HEADS, QB, KW), F32)],
        input_output_aliases={9: 0},
        scratch_shapes=[pltpu.VMEM((KW, ATT_WIDTH), F32), pltpu.VMEM((KW, ATT_WIDTH), F32)],
        compiler_params=_params(("arbitrary", "arbitrary")))(*([proj3] * 7), do, bias, dproj3)


def _rel_bias_grad(dbias):
    H = dbias.shape[0]
    Wd = KW + QB
    padded = jnp.pad(dbias, ((0, 0), (0, 1), (QB, 0)))
    skew = padded.reshape(H, (QB + 1) * Wd)[:, :QB * (Wd + 1)].reshape(H, QB, Wd + 1)[:, :, :Wd]
    c = jnp.arange(Wd)[:, None]
    bins = (jnp.clip(KW - c, -(CHUNK - 1), MAX_REL) + (CHUNK - 1) == jnp.arange(N_REL)[None, :]).astype(F32)

    def body(s_ref, bins_ref, o_ref):
        col = jnp.sum(s_ref[...], axis=1)
        o_ref[...] = jnp.dot(col, bins_ref[...], preferred_element_type=F32, precision=lax.Precision.HIGHEST)

    return pl.pallas_call(
        body, name="rel_bias_grad", grid=(1,),
        in_specs=[pl.BlockSpec((H, QB, Wd), lambda i: (0, 0, 0)), pl.BlockSpec((Wd, N_REL), lambda i: (0, 0))],
        out_specs=pl.BlockSpec((H, N_REL), lambda i: (0, 0)), out_shape=jax.ShapeDtypeStruct((H, N_REL), F32),
        compiler_params=_params(("arbitrary",)))(skew, bins)


MEM_TILE = 512


def _mem_probs(qh, kh):
    s = _dot(qh, kh, 'nt') * MEM_SCALE
    e = jnp.exp(s - jnp.max(s, axis=-1, keepdims=True))
    return e * (1.0 / jnp.sum(e, axis=-1, keepdims=True))


def _mem_fwd(proj3, kv3):
    Bl, S, _ = proj3.shape
    tq = _pick(S, (MEM_TILE, 256))
    hd = MEM_HEAD_DIM

    def body(q_ref, kv_ref, o_ref):
        for h in range(MEM_HEADS):
            p = _mem_probs(q_ref[:, h * hd:(h + 1) * hd], kv_ref[:, h * hd:(h + 1) * hd])
            o_ref[:, h * hd:(h + 1) * hd] = _dot(p, kv_ref[:, MEM_WIDTH + h * hd:MEM_WIDTH + (h + 1) * hd]).astype(BF16)

    return pl.pallas_call(
        body, name="mem_fwd", grid=(Bl, S // tq),
        in_specs=[pl.BlockSpec((None, tq, MEM_WIDTH), lambda b, i: (b, i, 5)),
                  pl.BlockSpec((None, MEM_LEN, 2 * MEM_WIDTH), lambda b, i: (b, 0, 0))],
        out_specs=pl.BlockSpec((None, tq, MEM_WIDTH), lambda b, i: (b, i, 0)),
        out_shape=jax.ShapeDtypeStruct((Bl, S, MEM_WIDTH), BF16),
        compiler_params=_params(("parallel", "parallel")))(proj3, kv3)


def _mem_bwd(proj3, kv3, do, dproj3):
    Bl, S, _ = proj3.shape
    tq = _pick(S, (MEM_TILE, 256))
    hd = MEM_HEAD_DIM

    def body(q_ref, kv_ref, do_ref, through_ref, dq_ref, dkv_ref):
        i = pl.program_id(1)

        @pl.when(i == 0)
        def _():
            dkv_ref[...] = jnp.zeros_like(dkv_ref)

        for h in range(MEM_HEADS):
            ks = slice(h * hd, (h + 1) * hd)
            vs = slice(MEM_WIDTH + h * hd, MEM_WIDTH + (h + 1) * hd)
            qh, kh, vh, doh = q_ref[:, ks], kv_ref[:, ks], kv_ref[:, vs], do_ref[:, ks]
            p = _mem_probs(qh, kh)
            dp = _dot(doh, vh, 'nt')
            ds = p * (dp - jnp.sum(p * dp, axis=-1, keepdims=True))
            dq_ref[:, ks] = (_dot(ds, kh) * MEM_SCALE).astype(BF16)
            dkv_ref[:, ks] += _dot(ds, qh, 'tn') * MEM_SCALE
            dkv_ref[:, vs] += _dot(p, doh, 'tn')

    return pl.pallas_call(
        body, name="mem_bwd", grid=(Bl, S // tq),
        in_specs=[pl.BlockSpec((None, tq, MEM_WIDTH), lambda b, i: (b, i, 5)),
                  pl.BlockSpec((None, MEM_LEN, 2 * MEM_WIDTH), lambda b, i: (b, 0, 0)),
                  pl.BlockSpec((None, tq, MEM_WIDTH), lambda b, i: (b, i, 0)), ANY],
        out_specs=[pl.BlockSpec((None, tq, MEM_WIDTH), lambda b, i: (b, i, 5)),
                   pl.BlockSpec((None, MEM_LEN, 2 * MEM_WIDTH), lambda b, i: (b, 0, 0))],
        out_shape=[jax.ShapeDtypeStruct(dproj3.shape, BF16),
                   jax.ShapeDtypeStruct((Bl, MEM_LEN, 2 * MEM_WIDTH), F32)],
        input_output_aliases={3: 0},
        compiler_params=_params(("parallel", "arbitrary")))(proj3, kv3, do, dproj3)


def _position():
    x, y, c = lax.axis_index("x"), lax.axis_index("y"), lax.axis_index("c")
    return x, y, c, 4 * x + 2 * y + c


def _device(idx):
    return ((idx >> 2) & 1, (idx >> 1) & 1, idx & 1)


def _half_block(ref, axis, shard_shape, k, h):
    R, Cn = shard_shape
    if axis == 1:
        return ref.at[pl.ds(h * (R // 2), R // 2), pl.ds(k * Cn, Cn)]
    return ref.at[pl.ds(k * R + h * (R // 2), R // 2), :]


def _block(ref, axis, shard_shape, k):
    R, Cn = shard_shape
    if axis == 1:
        return ref.at[:, pl.ds(k * Cn, Cn)]
    return ref.at[pl.ds(k * R, R), :]


def _half(ref, h):
    R = ref.shape[0]
    return ref.at[pl.ds(h * (R // 2), R // 2), :]


ANY = pl.BlockSpec(memory_space=pl.ANY)


HBM = pl.BlockSpec(memory_space=pltpu.HBM)
SEM = pl.BlockSpec(memory_space=pltpu.SEMAPHORE)
VMEM_WHOLE = pl.BlockSpec(memory_space=pltpu.VMEM)
EFFECT = pltpu.SideEffectType.DATAFLOW_SIDE_EFFECTING


def _in_hbm(a):
    return pltpu.with_memory_space_constraint(a, pltpu.HBM)


def _split_start(body, name, sources, lands, n_copies):
    n = len(sources)
    out_shape, out_specs = [], []
    for _ in range(n):
        out_shape += [pltpu.SemaphoreType.DMA((n_copies,)), pltpu.SemaphoreType.DMA((n_copies,))]
        out_specs += [SEM, SEM]
    out_shape += [pltpu.HBM(a.shape, a.dtype) for a in list(sources) + list(lands)]
    out_specs += [HBM] * (2 * n)
    out_shape.append(jax.ShapeDtypeStruct((8, 128), F32))
    out_specs.append(VMEM_WHOLE)

    def call_body(*refs):
        srcs, lnds = refs[:n], refs[n:2 * n]
        sems = refs[2 * n:4 * n]
        token = refs[-1]
        body(srcs, lnds, sems[0::2], sems[1::2])
        token[...] = jnp.zeros_like(token)

    res = pl.pallas_call(
        call_body, name=name, in_specs=[HBM] * (2 * n), out_specs=out_specs, out_shape=out_shape,
        input_output_aliases={i: 2 * n + i for i in range(2 * n)},
        compiler_params=pltpu.CompilerParams(has_side_effects=EFFECT))(
            *[_in_hbm(a) for a in list(sources) + list(lands)])
    pairs = [(res[2 * w], res[2 * w + 1], res[2 * n + w], res[3 * n + w]) for w in range(n)]
    return pairs, res[-1]


def _split_wait(body, name, pairs, after):
    n = len(pairs)

    def call_body(*refs):
        srcs, lnds = refs[:n], refs[n:2 * n]
        sems = refs[2 * n:4 * n]
        body(srcs, lnds, sems[0::2], sems[1::2])

    args = [_in_hbm(p[2]) for p in pairs] + [_in_hbm(p[3]) for p in pairs]
    for p in pairs:
        args += [p[0], p[1]]
    res = pl.pallas_call(
        call_body, name=name, in_specs=[HBM] * (2 * n) + [SEM] * (2 * n) + [ANY], out_specs=[HBM] * (2 * n),
        out_shape=[pltpu.HBM(a.shape, a.dtype) for a in args[:2 * n]],
        input_output_aliases={i: i for i in range(2 * n)},
        compiler_params=pltpu.CompilerParams(has_side_effects=EFFECT))(*args, after)
    return res[:n], res[n:]


def _place_block(shard, axis, chip_idx, name, after=None):
    R, Cn = shard.shape
    tr = _pick(R, (256, 176, 128, 64, 32, 16, 8))
    nblk = R // tr

    def body(k_ref, s_ref, *rest):
        rest[-1][...] = s_ref[...]

    if axis == 1:
        out_shape, out_index = (R, 4 * Cn), lambda i, k: (i, k[0])
    else:
        out_shape, out_index = (4 * R, Cn), lambda i, k: (k[0] * nblk + i, 0)
    extra = () if after is None else (after,)
    return pl.pallas_call(
        body, name=name,
        grid_spec=pltpu.PrefetchScalarGridSpec(
            num_scalar_prefetch=1, grid=(nblk,),
            in_specs=[pl.BlockSpec((tr, Cn), lambda i, k: (i, 0))] + [ANY] * len(extra),
            out_specs=pl.BlockSpec((tr, Cn), out_index)),
        out_shape=jax.ShapeDtypeStruct(out_shape, shard.dtype),
        compiler_params=_params(("parallel",)))(chip_idx, shard, *extra)


def _gather_copy(srcs, lnds, send, recv, axes, shapes, w, j, me):
    chip = me >> 1
    return (pltpu.make_async_remote_copy(
        src_ref=srcs[w], dst_ref=_block(lnds[w], axes[w], shapes[w], chip), send_sem=send[w].at[j],
        recv_sem=recv[w].at[j], device_id=_device(me ^ (2 * (j + 1))), device_id_type=MESH),
            pltpu.make_async_remote_copy(
        src_ref=srcs[w], dst_ref=_block(lnds[w], axes[w], shapes[w], chip ^ (j + 1)), send_sem=send[w].at[j],
        recv_sem=recv[w].at[j], device_id=_device(me ^ (2 * (j + 1))), device_id_type=MESH))


def _gather_start(shards, lands, axes, name):
    shapes = [s.shape for s in shards]

    def body(srcs, lnds, send, recv):
        x, y, c, me = _position()
        for w in range(len(shards)):
            for j in range(3):
                _gather_copy(srcs, lnds, send, recv, axes, shapes, w, j, me)[0].start()

    return _split_start(body, name, shards, lands, 3)


def _gather_wait(pairs, axes, after, name):
    shapes = [p[2].shape for p in pairs]

    def body(srcs, lnds, send, recv):
        x, y, c, me = _position()
        for w in range(len(pairs)):
            for j in range(3):
                sent, landed = _gather_copy(srcs, lnds, send, recv, axes, shapes, w, j, me)
                sent.wait_send()
                landed.wait_recv()

    return _split_wait(body, name, pairs, after)[1]


def _shard_shape(grad, axis):
    return (grad.shape[0], grad.shape[1] // 4) if axis == 1 else (grad.shape[0] // 4, grad.shape[1])


def _scatter_copy(srcs, lnds, send, recv, axes, shapes, w, m, me):
    peer = me ^ m
    return pltpu.make_async_remote_copy(
        src_ref=_half_block(srcs[w], axes[w], shapes[w], peer >> 1, peer & 1), dst_ref=lnds[w].at[m - 1],
        send_sem=send[w].at[m - 1], recv_sem=recv[w].at[m - 1], device_id=_device(peer), device_id_type=MESH)


def _scatter_start(grads, axes, name):
    shapes = [_shard_shape(g, a) for g, a in zip(grads, axes)]
    lands = [lax.empty((N_DEV - 1, R // 2, Cn), BF16) for R, Cn in shapes]

    def body(srcs, lnds, send, recv):
        x, y, c, me = _position()
        for w in range(len(grads)):
            for m in range(1, N_DEV):
                _scatter_copy(srcs, lnds, send, recv, axes, shapes, w, m, me).start()

    return _split_start(body, name, grads, lands, N_DEV - 1)


def _scatter_wait(pairs, axes, after):
    shapes = [_shard_shape(p[2], a) for p, a in zip(pairs, axes)]

    def body(srcs, lnds, send, recv):
        x, y, c, me = _position()
        for w in range(len(pairs)):
            for m in range(1, N_DEV):
                cp = _scatter_copy(srcs, lnds, send, recv, axes, shapes, w, m, me)
                cp.wait_send()
                cp.wait_recv()

    return _split_wait(body, "scatter_wait", pairs, after)


def _sum_partials(own, parts, half, name):
    R, Cn = own.shape
    tr = _pick(R, (256, 176, 128, 64, 32, 16, 8))
    nblk = R // tr

    def body(half_ref, own_ref, p_ref, o_ref):
        acc = own_ref[...].astype(F32)
        for d in range(N_DEV - 1):
            acc = acc + p_ref[d].astype(F32)
        o_ref[...] = acc

    return pl.pallas_call(
        body, name=name,
        grid_spec=pltpu.PrefetchScalarGridSpec(
            num_scalar_prefetch=1, grid=(nblk,),
            in_specs=[pl.BlockSpec((tr, Cn), lambda i, hr: (i, 0)),
                      pl.BlockSpec((N_DEV - 1, tr, Cn), lambda i, hr: (0, i, 0))],
            out_specs=pl.BlockSpec((tr, Cn), lambda i, hr: (hr[0] * nblk + i, 0))),
        out_shape=jax.ShapeDtypeStruct((2 * R, Cn), F32),
        compiler_params=_params(("parallel",)))(half, own, parts)


def _exchange_halves(grads):
    n = len(grads)

    def body(*refs):
        outs = refs[n:2 * n]
        send, recv = refs[2 * n:]
        x, y, c, me = _position()

        def copy(w, half):
            rows = _half(outs[w], half)
            return pltpu.make_async_remote_copy(src_ref=rows, dst_ref=rows, send_sem=send.at[w],
                                                recv_sem=recv.at[w], device_id=_device(me ^ 1), device_id_type=MESH)

        for w in range(n):
            copy(w, c).start()
        for w in range(n):
            copy(w, 1 - c).wait_recv()
        for w in range(n):
            copy(w, c).wait_send()

    return pl.pallas_call(
        body, name="exchange_halves", in_specs=[ANY] * n, out_specs=[ANY] * n,
        out_shape=[jax.ShapeDtypeStruct(a.shape, a.dtype) for a in grads],
        input_output_aliases={i: i for i in range(n)},
        scratch_shapes=[pltpu.SemaphoreType.DMA((n,)), pltpu.SemaphoreType.DMA((n,))],
        compiler_params=pltpu.CompilerParams(has_side_effects=True))(*grads)


def _all_reduce_small(vec):
    R, L = vec.shape

    def body(v_ref, o_ref, buf, send, recv):
        x, y, c, me = _position()
        buf[me] = v_ref[...]

        def copy(m, slot):
            return pltpu.make_async_remote_copy(src_ref=v_ref, dst_ref=buf.at[slot], send_sem=send.at[m - 1],
                                                recv_sem=recv.at[m - 1], device_id=_device(me ^ m),
                                                device_id_type=MESH)

        for m in range(1, N_DEV):
            copy(m, me).start()
        for m in range(1, N_DEV):
            copy(m, me ^ m).wait_recv()
        for m in range(1, N_DEV):
            copy(m, me).wait_send()
        acc = buf[0]
        for d in range(1, N_DEV):
            acc = acc + buf[d]
        o_ref[...] = acc

    vm = pl.BlockSpec(memory_space=pltpu.VMEM)
    return pl.pallas_call(
        body, name="all_reduce_small", in_specs=[vm], out_specs=vm, out_shape=jax.ShapeDtypeStruct((R, L), F32),
        scratch_shapes=[pltpu.VMEM((N_DEV, R, L), F32), pltpu.SemaphoreType.DMA((N_DEV - 1,)),
                        pltpu.SemaphoreType.DMA((N_DEV - 1,))],
        compiler_params=pltpu.CompilerParams(has_side_effects=True))(vec)


def _adamw(w, g, m, v, name):
    R, Cn = w.shape
    tr = _pick(R, (256, 176, 128, 64, 40, 32, 16, 8))

    def body(w_ref, g_ref, m_ref, v_ref, d_ref, nm_ref, nv_ref):
        gv = g_ref[...]
        nm = ADAM_B1 * m_ref[...] + (1.0 - ADAM_B1) * gv
        nv = ADAM_B2 * v_ref[...] + (1.0 - ADAM_B2) * (gv * gv)
        m_hat = nm / (1.0 - ADAM_B1 ** ADAM_STEP)
        v_hat = nv / (1.0 - ADAM_B2 ** ADAM_STEP)
        d_ref[...] = -ADAM_LR * (m_hat / (jnp.sqrt(v_hat) + ADAM_EPS) + ADAM_WD * w_ref[...])
        nm_ref[...] = nm
        nv_ref[...] = nv

    spec = pl.BlockSpec((tr, Cn), lambda i: (i, 0))
    return pl.pallas_call(
        body, name=name, grid=(R // tr,), in_specs=[spec] * 4, out_specs=[spec] * 3,
        out_shape=[jax.ShapeDtypeStruct((R, Cn), F32)] * 3, compiler_params=_params(("parallel",)))(w, g, m, v)


def _pack(arrays, rows):
    flat = jnp.concatenate([a.reshape(-1).astype(F32) for a in arrays])
    return jnp.pad(flat, (0, rows * 128 - flat.shape[0])).reshape(rows, 128)


def _unpack(packed, shapes):
    flat = packed.reshape(-1)
    out, off = [], 0
    for s in shapes:
        size = 1
        for d in s:
            size *= d
        out.append(flat[off:off + size].reshape(s))
        off += size
    return out


def _ffn_fwd(x, h, arrived, tag, next_g=None):
    w_up = arrived(f"{tag}_w_up", h)
    a, b, act = _ffn_up(h, w_up, f"{tag}_up")
    w_down = arrived(f"{tag}_w_down", act)
    out, h_next = _ffn_down(act, w_down, x, f"{tag}_down", next_g)
    return out, h_next, (h, a, b, act, w_up, w_down)


def _ffn_bwd(dout, x, norm, saved, tag, send):
    h, a, b, act, w_up, w_down = saved
    g_down = _mm(act, dout, 'tn', f"{tag}_down_dw", BF16, scale=0.5)
    token = send([f"{tag}_w_down"], [g_down])
    da, db = _ffn_down_bwd(dout, w_down, a, b, f"{tag}_down_dx", after=token)
    g_up = _dw_pieces(h, [da, db], f"{tag}_up_dw")
    token = send([f"{tag}_w_up"], [g_up])
    return _dx_rms_bwd([da, db], w_up, x, norm, dout, f"{tag}_up_dx", after=token)


def kernel(x, mem, ffn1_norm, ffn1_w_up, ffn1_w_down, mix_norm, mem_norm, w_in, b_gate, conv_dw_w, conv_dw_b, conv_ln_g, conv_ln_b, conv_w_pw, att_rel_bias, att_w_o, mem_w_kv, mem_w_o, w_out, ffn2_norm, ffn2_w_up, ffn2_w_down, final_norm, loss_target, m_ffn1_norm, m_ffn1_w_up, m_ffn1_w_down, m_mix_norm, m_mem_norm, m_w_in, m_b_gate, m_conv_dw_w, m_conv_dw_b, m_conv_ln_g, m_conv_ln_b, m_conv_w_pw, m_att_rel_bias, m_att_w_o, m_mem_w_kv, m_mem_w_o, m_w_out, m_ffn2_norm, m_ffn2_w_up, m_ffn2_w_down, m_final_norm, v_ffn1_norm, v_ffn1_w_up, v_ffn1_w_down, v_mix_norm, v_mem_norm, v_w_in, v_b_gate, v_conv_dw_w, v_conv_dw_b, v_conv_ln_g, v_conv_ln_b, v_conv_w_pw, v_att_rel_bias, v_att_w_o, v_mem_w_kv, v_mem_w_o, v_w_out, v_ffn2_norm, v_ffn2_w_up, v_ffn2_w_down, v_final_norm):
    given = dict(locals())
    wts = {n: given[n] for n in WEIGHTS}
    mom1 = {n: given["m_" + n] for n in WEIGHTS}
    mom2 = {n: given["v_" + n] for n in WEIGHTS}
    Bl, S, Dm = x.shape
    T = Bl * S
    x0 = x.reshape(T, Dm)
    tgt = loss_target.reshape(T, Dm)
    mem2 = mem.reshape(Bl * MEM_LEN, Dm)

    big_names = [n for n, _ in BIG]
    big_axes = [a for _, a in BIG]
    chip = 2 * lax.axis_index("x") + lax.axis_index("y")

    core = lax.axis_index("c")
    axis_of = dict(BIG)

    gather_groups = [['ffn1_w_up'], ['ffn1_w_down'], ['w_in', 'conv_dw_w'],
                     ['mem_w_kv', 'conv_w_pw', 'att_w_o', 'mem_w_o', 'w_out'], ['ffn2_w_up'], ['ffn2_w_down']]
    gather_names = [n for grp in gather_groups for n in grp]
    gather_axes = [axis_of.get(n, 1) for n in gather_names]
    shards = [jnp.pad(conv_dw_w[0], ((0, 1), (0, 0))) if n == 'conv_dw_w' else wts[n][0].astype(BF16)
              for n in gather_names]
    chip_idx = chip.reshape(1).astype(jnp.int32)
    first, first_token = _gather_start(
        shards[:1], [_place_block(shards[0], gather_axes[0], chip_idx, f"place_{gather_names[0]}")],
        gather_axes[:1], "gather_start_first")
    lands = [_place_block(sh, a, chip_idx, f"place_{n}", after=first_token)
             for sh, a, n in zip(shards[1:], gather_axes[1:], gather_names[1:])]
    rest, gather_token = _gather_start(shards[1:], lands, gather_axes[1:], "gather_start_rest")
    in_flight = dict(zip(gather_names, first + rest))
    full = {}

    def arrived(name, after):
        if name not in full:
            grp = next(grp for grp in gather_groups if name in grp)
            lands = _gather_wait([in_flight[n] for n in grp], [axis_of.get(n, 1) for n in grp], after,
                                 f"gather_wait_{grp[0]}")
            full.update(zip(grp, lands))
        return full[name]

    scattering = {}

    def send(names, grads):
        pairs, token = _scatter_start(grads, [axis_of[n] for n in names], f"scatter_start_{names[0]}")
        scattering.update(zip(names, pairs))
        return token

    final_g = final_norm.reshape(1, Dm)
    bias = _att_bias(att_rel_bias[0] + first_token[:1, :1])

    x1, h, ffn1_saved = _ffn_fwd(x0, _rms_fwd(x0, ffn1_norm, "ffn1_norm", after=gather_token), arrived, "ffn1",
                                 next_g=mix_norm)
    w_in_full = arrived('w_in', h)
    dw_full = full['conv_dw_w'][:CONV_KERNEL]
    proj = _mm(h, w_in_full, 'nn', "w_in", BF16)
    proj3 = proj.reshape(Bl, S, proj.shape[1])
    cv, c_act = _conv_fwd(proj3, dw_full, conv_dw_b, conv_ln_g, conv_ln_b)
    o_att = _att_fwd(proj3, bias)
    mem_h = _rms_fwd(mem2, mem_norm, "mem_norm")
    kv = _mm(mem_h, arrived('mem_w_kv', o_att), 'nn', "mem_kv", BF16)
    kv3 = kv.reshape(Bl, MEM_LEN, 2 * MEM_WIDTH)
    o_mem = _mem_fwd(proj3, kv3)
    c_act2, o_att2, o_mem2 = c_act.reshape(T, -1), o_att.reshape(T, -1), o_mem.reshape(T, -1)
    x2, yc, ya, ym, h2 = _mix_fwd(c_act2, o_att2, o_mem2, proj, b_gate, x1, full['conv_w_pw'], full['att_w_o'],
                                  full['mem_w_o'], full['w_out'], ffn2_norm)
    x3, _, ffn2_saved = _ffn_fwd(x2, h2, arrived, "ffn2")
    dx3, g_final, loss_vec = _final_fwd_bwd(x3, tgt, final_g)

    g = {}
    dx2, g['ffn2_norm'] = _ffn_bwd(dx3, x2, ffn2_norm, ffn2_saved, "ffn2", send)
    dgl, g['b_gate'], dc, doa, dom, g_pw, g_o, g_mo, g_out = _mix_bwd(
        dx2, yc, ya, ym, c_act2, o_att2, o_mem2, proj, b_gate, full['conv_w_pw'], full['att_w_o'],
        full['mem_w_o'], full['w_out'])
    token = send(['w_out', 'conv_w_pw', 'att_w_o', 'mem_w_o'], [g_out, g_pw, g_o, g_mo])
    dproj3 = dgl.reshape(Bl, S, -1)
    dproj3, g_dw, g['conv_dw_b'], g['conv_ln_g'], g['conv_ln_b'] = _conv_bwd(
        proj3, cv, dc.reshape(Bl, S, -1), dw_full, conv_ln_g, conv_ln_b, dproj3)
    dproj3, dk, dv, dbias = _att_bwd(proj3, doa.reshape(Bl, S, -1), bias, dproj3)
    g['att_rel_bias'] = _rel_bias_grad(dbias)
    dproj3, dkv = _mem_bwd(proj3, kv3, dom.reshape(Bl, S, -1), dproj3)
    dkv2 = dkv.reshape(Bl * MEM_LEN, 2 * MEM_WIDTH)
    g_kv = _mm(mem_h, dkv2, 'tn', "mem_kv_dw", BF16, after=token)
    dmem_h = _mm(dkv2, full['mem_w_kv'], 'nt', "mem_kv_dx", F32)
    _, g['mem_norm'] = _rms_bwd(mem2, mem_norm, dmem_h, dmem_h, "mem_norm_bwd")
    dkdv = jnp.concatenate([dk.reshape(T, -1), dv.reshape(T, -1)], axis=1)
    dproj = lax.dynamic_update_slice(dproj3.reshape(T, -1), dkdv, (0, 2 * CONV_WIDTH + ATT_WIDTH))
    token = send(['mem_w_kv', 'w_in'], [g_kv, _mm(h, dproj, 'tn', "w_in_dw", BF16)])
    dx1, g['mix_norm'] = _dx_rms_bwd([dproj], w_in_full, x1, mix_norm, dx2, "w_in_dx", after=token)
    dx0, g['ffn1_norm'] = _ffn_bwd(dx1, x0, ffn1_norm, ffn1_saved, "ffn1", send)
    g['final_norm'] = g_final

    sent, landed = _scatter_wait([scattering[n] for n in big_names], big_axes, dx0)
    halves = []
    half_idx = core.reshape(1).astype(jnp.int32)
    for n, a, own_full, parts in zip(big_names, big_axes, sent, landed):
        R, Cn = _shard_shape(own_full, a)
        start = (core * (R // 2), chip * Cn) if a == 1 else (chip * R + core * (R // 2), 0)
        own = lax.dynamic_slice(own_full, start, (R // 2, Cn))
        halves.append(_sum_partials(own, parts, half_idx, f"sum_{n}"))
    for n, sg in zip(big_names, _exchange_halves(halves)):
        g[n] = sg

    small_shapes = [wts[n].shape for n in SMALL]
    n_small = sum(int(wts[n].size) for n in SMALL)
    n_red = n_small + CONV_KERNEL * CONV_WIDTH + 1
    red = _all_reduce_small(_pack([g[n] for n in SMALL] + [g_dw, loss_vec[0, :1]], -(-n_red // 1024) * 8))
    red_list = _unpack(red, small_shapes + [(CONV_KERNEL, CONV_WIDTH), ()])
    for n, rg in zip(SMALL, red_list[:-2]):
        g[n] = rg
    loss = red_list[-1]
    dw_cols = conv_dw_w.shape[2]
    g['conv_dw_w'] = lax.dynamic_slice(red_list[-2], (0, chip * dw_cols), (CONV_KERNEL, dw_cols))[None]

    delta, new_m, new_v = {}, {}, {}
    for n in big_names:
        g[n] = g[n][None]
        d, nm, nv = _adamw(wts[n][0], g[n][0], mom1[n][0], mom2[n][0], f"adamw_{n}")
        delta[n], new_m[n], new_v[n] = d[None], nm[None], nv[None]
    rest = SMALL + ['conv_dw_w']
    rest_shapes = [wts[n].shape for n in rest]
    rows = -(-sum(int(wts[n].size) for n in rest) // 1024) * 8
    packed = [_pack([src[n] for n in rest], rows) for src in (wts, g, mom1, mom2)]
    for out, res in zip((delta, new_m, new_v), _adamw(*packed, "adamw_small")):
        for n, a in zip(rest, _unpack(res, rest_shapes)):
            out[n] = a

    grad_x = dx0.reshape(Bl, S, Dm)
    return (loss, grad_x, *[g[n] for n in WEIGHTS], *[delta[n] for n in WEIGHTS],
            *[new_m[n] for n in WEIGHTS], *[new_v[n] for n in WEIGHTS])
```

```python
import jax
import jax.numpy as jnp
from jax import lax
from jax.experimental import pallas as pl
from jax.experimental.pallas import tpu as pltpu

F32 = jnp.float32
BF16 = jnp.bfloat16

CHUNK = 64
LEFT_CHUNKS = 8
MAX_REL = 128
N_REL = (CHUNK - 1) + MAX_REL + 1
CONV_WIDTH = 512
CONV_KERNEL = 31
ATT_HEADS = 8
ATT_WIDTH = 512
MEM_LEN = 256
MEM_HEADS = 4
MEM_HEAD_DIM = 128
MEM_WIDTH = 512
EPS = 1e-6
MASK_VALUE = -1e30
ATT_SCALE = 64 ** -0.5
MEM_SCALE = 128 ** -0.5

ADAM_LR = 0.001
ADAM_B1 = 0.9
ADAM_B2 = 0.999
ADAM_EPS = 1e-08
ADAM_WD = 0.01
ADAM_STEP = 10

QB = 256
KW = 3 * QB
CONV_PAD = 32
CONV_TILE = 256

VMEM_LIMIT = 56 << 20
MXU_COLS = 256

WEIGHTS = ['ffn1_norm', 'ffn1_w_up', 'ffn1_w_down', 'mix_norm', 'mem_norm', 'w_in', 'b_gate', 'conv_dw_w',
           'conv_dw_b', 'conv_ln_g', 'conv_ln_b', 'conv_w_pw', 'att_rel_bias', 'att_w_o', 'mem_w_kv', 'mem_w_o',
           'w_out', 'ffn2_norm', 'ffn2_w_up', 'ffn2_w_down', 'final_norm']
BIG = [('ffn1_w_up', 1), ('ffn1_w_down', 0), ('w_in', 1), ('conv_w_pw', 1), ('att_w_o', 1), ('mem_w_kv', 0),
       ('mem_w_o', 1), ('w_out', 0), ('ffn2_w_up', 1), ('ffn2_w_down', 0)]
SMALL = ['ffn1_norm', 'mix_norm', 'mem_norm', 'b_gate', 'conv_dw_b', 'conv_ln_g', 'conv_ln_b', 'att_rel_bias',
         'ffn2_norm', 'final_norm']
N_DEV = 8
MESH = pl.DeviceIdType.MESH


def _pick(n, cands):
    for c in cands:
        if n % c == 0:
            return c
    return n


def _sig(x):
    return 0.5 * jnp.tanh(0.5 * x) + 0.5


def _params(sem=None, vmem=VMEM_LIMIT):
    return pltpu.CompilerParams(dimension_semantics=sem, vmem_limit_bytes=vmem)


def _dot(a, b, mode='nn'):
    dims = {'nn': (((1,), (0,)), ((), ())), 'nt': (((1,), (1,)), ((), ())), 'tn': (((0,), (0,)), ((), ()))}[mode]
    return lax.dot_general(a.astype(BF16), b.astype(BF16), dims, preferred_element_type=F32)


def _mm(a, b, mode, name, out_dtype, res=None, scale=1.0, after=None):
    if mode == 'nn':
        (M, C), (_, N) = a.shape, b.shape
    elif mode == 'nt':
        (M, C), (N, _) = a.shape, b.shape
    else:
        (C, M), (_, N) = a.shape, b.shape
    tm = _pick(M, (1024, 1408, 512, 256, 128))
    tn = _pick(N, (1024, 1408, 512, 256, 128))
    tc = C if C <= 2816 else _pick(C, (2048, 1024, 1408, 512, 256, 128))
    nk = C // tc
    if mode == 'nn':
        a_spec = pl.BlockSpec((tm, tc), lambda i, j, k: (i, k))
        b_spec = pl.BlockSpec((tc, tn), lambda i, j, k: (k, j))
    elif mode == 'nt':
        a_spec = pl.BlockSpec((tm, tc), lambda i, j, k: (i, k))
        b_spec = pl.BlockSpec((tn, tc), lambda i, j, k: (j, k))
    else:
        a_spec = pl.BlockSpec((tc, tm), lambda i, j, k: (k, i))
        b_spec = pl.BlockSpec((tc, tn), lambda i, j, k: (k, j))
    o_spec = pl.BlockSpec((tm, tn), lambda i, j, k: (i, j))
    has_res = res is not None
    has_after = after is not None

    def body(*refs):
        a_ref, b_ref = refs[:2]
        r_ref = refs[2] if has_res else None
        o_ref, acc_ref = refs[-2:]
        k = pl.program_id(2)

        def finish(acc):
            if scale != 1.0:
                acc = acc * scale
            if r_ref is not None:
                acc = r_ref[...] + acc
            o_ref[...] = acc.astype(o_ref.dtype)

        if nk == 1:
            finish(_dot(a_ref[...], b_ref[...], mode))
        else:
            @pl.when(k == 0)
            def _():
                acc_ref[...] = jnp.zeros_like(acc_ref)

            acc_ref[...] += _dot(a_ref[...], b_ref[...], mode)

            @pl.when(k == nk - 1)
            def _():
                finish(acc_ref[...])

    in_specs = [a_spec, b_spec] + ([o_spec] if has_res else []) + ([ANY] if has_after else [])
    args = (a, b) + ((res,) if has_res else ()) + ((after,) if has_after else ())
    acc_shape = (tm, tn) if nk > 1 else (8, 128)
    return pl.pallas_call(
        body, name=name, grid=(M // tm, N // tn, nk), in_specs=in_specs, out_specs=o_spec,
        out_shape=jax.ShapeDtypeStruct((M, N), out_dtype), scratch_shapes=[pltpu.VMEM(acc_shape, F32)],
        compiler_params=_params(("parallel", "parallel", "arbitrary")))(*args)


def _row_tile(T):
    return _pick(T, (512, 256, 128, 64, 32, 16, 8))


def _rms_fwd(x, g, name, after=None):
    T, Dm = x.shape
    tm = _row_tile(T)

    def body(x_ref, g_ref, *rest):
        o_ref = rest[-1]
        xv = x_ref[...]
        r = lax.rsqrt(jnp.mean(xv * xv, axis=-1, keepdims=True) + EPS)
        o_ref[...] = ((xv * r) * g_ref[...]).astype(o_ref.dtype)

    extra = () if after is None else (after,)
    return pl.pallas_call(
        body, name=name, grid=(T // tm,),
        in_specs=[pl.BlockSpec((tm, Dm), lambda i: (i, 0)), pl.BlockSpec((1, Dm), lambda i: (0, 0))]
        + [ANY] * len(extra),
        out_specs=pl.BlockSpec((tm, Dm), lambda i: (i, 0)), out_shape=jax.ShapeDtypeStruct((T, Dm), BF16),
        compiler_params=_params(("parallel",)))(x, g, *extra)


def _rms_bwd(x, g, dh, dres, name):
    T, Dm = x.shape
    tm = _row_tile(T)

    def body(x_ref, g_ref, dh_ref, dr_ref, dx_ref, dg_ref):
        i = pl.program_id(0)
        xv = x_ref[...]
        r = lax.rsqrt(jnp.mean(xv * xv, axis=-1, keepdims=True) + EPS)
        xr = xv * r
        dh_v = dh_ref[...].astype(F32)
        dyg = dh_v * g_ref[...]
        dx = r * (dyg - xr * jnp.mean(dyg * xr, axis=-1, keepdims=True))
        dx_ref[...] = dr_ref[...] + dx

        @pl.when(i == 0)
        def _():
            dg_ref[...] = jnp.zeros_like(dg_ref)

        dg_ref[...] += jnp.sum(dh_v * xr, axis=0, keepdims=True)

    row = pl.BlockSpec((tm, Dm), lambda i: (i, 0))
    vec = pl.BlockSpec((1, Dm), lambda i: (0, 0))
    return pl.pallas_call(
        body, name=name, grid=(T // tm,), in_specs=[row, vec, row, row], out_specs=[row, vec],
        out_shape=[jax.ShapeDtypeStruct((T, Dm), F32), jax.ShapeDtypeStruct((1, Dm), F32)],
        compiler_params=_params(("arbitrary",)))(x, g, dh, dres)


def _final_fwd_bwd(x3, tgt, g):
    T, Dm = x3.shape
    tm = _row_tile(T)

    def body(x_ref, t_ref, g_ref, dx_ref, dg_ref, loss_ref):
        i = pl.program_id(0)
        xv = x_ref[...]
        gg = g_ref[...]
        r = lax.rsqrt(jnp.mean(xv * xv, axis=-1, keepdims=True) + EPS)
        xr = xv * r
        err = xr * gg - t_ref[...]
        dout = err * (1.0 / Dm)
        dyg = dout * gg
        dx_ref[...] = r * (dyg - xr * jnp.mean(dyg * xr, axis=-1, keepdims=True))

        @pl.when(i == 0)
        def _():
            dg_ref[...] = jnp.zeros_like(dg_ref)
            loss_ref[...] = jnp.zeros_like(loss_ref)

        dg_ref[...] += jnp.sum(dout * xr, axis=0, keepdims=True)
        loss_ref[...] += jnp.zeros_like(loss_ref) + (0.5 / Dm) * jnp.sum(err * err)

    row = pl.BlockSpec((tm, Dm), lambda i: (i, 0))
    vec = pl.BlockSpec((1, Dm), lambda i: (0, 0))
    one = pl.BlockSpec((1, 128), lambda i: (0, 0))
    return pl.pallas_call(
        body, name="final_fwd_bwd", grid=(T // tm,), in_specs=[row, row, vec], out_specs=[row, vec, one],
        out_shape=[jax.ShapeDtypeStruct((T, Dm), F32), jax.ShapeDtypeStruct((1, Dm), F32),
                   jax.ShapeDtypeStruct((1, 128), F32)],
        compiler_params=_params(("arbitrary",)))(x3, tgt, g)


def _ffn_up(h, w_up, name):
    T, K = h.shape
    Fh = w_up.shape[1] // 2
    tm = _pick(T, (1024, 512, 256, 128))
    tn = _pick(Fh, (1408, 512, 256, 128))
    nj = Fh // tn

    def body(h_ref, wa_ref, wb_ref, a_ref, b_ref, act_ref):
        hv = h_ref[...]
        for c0 in range(0, tn, MXU_COLS):
            cs = slice(c0, min(c0 + MXU_COLS, tn))
            a = _dot(hv, wa_ref[:, cs])
            b = _dot(hv, wb_ref[:, cs])
            a_ref[:, cs] = a.astype(BF16)
            b_ref[:, cs] = b.astype(BF16)
            act_ref[:, cs] = (a * _sig(a) * b).astype(BF16)

    out = pl.BlockSpec((tm, tn), lambda i, j: (i, j))
    return pl.pallas_call(
        body, name=name, grid=(T // tm, nj),
        in_specs=[pl.BlockSpec((tm, K), lambda i, j: (i, 0)), pl.BlockSpec((K, tn), lambda i, j: (0, j)),
                  pl.BlockSpec((K, tn), lambda i, j: (0, j + nj))],
        out_specs=[out, out, out], out_shape=[jax.ShapeDtypeStruct((T, Fh), BF16)] * 3,
        compiler_params=_params(("parallel", "parallel")))(h, w_up, w_up)


def _ffn_down(act, w_down, x, name, next_g=None):
    T, Fh = act.shape
    Dm = w_down.shape[1]
    tm = _pick(T, (1024, 512, 256, 128))
    rows = _pick(tm, (256, 128))
    with_norm = next_g is not None

    def body(a_ref, w_ref, x_ref, *rest):
        o_ref = rest[-2] if with_norm else rest[-1]
        o_ref[...] = x_ref[...] + 0.5 * _dot(a_ref[...], w_ref[...])
        if with_norm:
            g_ref, h_ref = rest[0], rest[-1]

            def chunk(c, carry):
                rs = pl.ds(pl.multiple_of(c * rows, rows), rows)
                ov = o_ref[rs, :]
                r = lax.rsqrt(jnp.mean(ov * ov, axis=-1, keepdims=True) + EPS)
                h_ref[rs, :] = ((ov * r) * g_ref[...]).astype(BF16)
                return carry

            lax.fori_loop(0, tm // rows, chunk, 0)

    row = pl.BlockSpec((tm, Dm), lambda i: (i, 0))
    in_specs = [pl.BlockSpec((tm, Fh), lambda i: (i, 0)), pl.BlockSpec((Fh, Dm), lambda i: (0, 0)), row]
    out_specs, out_shape = [row], [jax.ShapeDtypeStruct((T, Dm), F32)]
    args = (act, w_down, x)
    if with_norm:
        in_specs.append(pl.BlockSpec((1, Dm), lambda i: (0, 0)))
        out_specs.append(row)
        out_shape.append(jax.ShapeDtypeStruct((T, Dm), BF16))
        args += (next_g,)
    res = pl.pallas_call(body, name=name, grid=(T // tm,), in_specs=in_specs, out_specs=out_specs,
                         out_shape=out_shape, compiler_params=_params(("parallel",)))(*args)
    return (res[0], res[1]) if with_norm else (res[0], None)


def _ffn_down_bwd(dout, w_down, a, b, name, after=None):
    T, Dm = dout.shape
    Fh = w_down.shape[0]
    tm = _pick(T, (1024, 512, 256, 128))
    tn = _pick(Fh, (1408, 512, 256, 128))

    def body(d_ref, w_ref, a_ref, b_ref, *rest):
        da_ref, db_ref = rest[-2:]
        dv = (d_ref[...] * 0.5).astype(BF16)
        for c0 in range(0, tn, MXU_COLS):
            cs = slice(c0, min(c0 + MXU_COLS, tn))
            dact = _dot(dv, w_ref[cs, :], 'nt')
            av = a_ref[:, cs].astype(F32)
            bv = b_ref[:, cs].astype(F32)
            s = _sig(av)
            da_ref[:, cs] = (dact * bv * s * (1.0 + av * (1.0 - s))).astype(BF16)
            db_ref[:, cs] = (dact * av * s).astype(BF16)

    tile = pl.BlockSpec((tm, tn), lambda i, j: (i, j))
    extra = () if after is None else (after,)
    return pl.pallas_call(
        body, name=name, grid=(T // tm, Fh // tn),
        in_specs=[pl.BlockSpec((tm, Dm), lambda i, j: (i, 0)), pl.BlockSpec((tn, Dm), lambda i, j: (j, 0)),
                  tile, tile] + [ANY] * len(extra),
        out_specs=[tile, tile], out_shape=[jax.ShapeDtypeStruct((T, Fh), BF16)] * 2,
        compiler_params=_params(("parallel", "parallel")))(dout, w_down, a, b, *extra)


def _dx_rms_bwd(pieces, w, x, g, dres, name, after=None):
    T, Dm = x.shape
    width = pieces[0].shape[1]
    tm = _pick(T, (1024, 512, 256, 128))
    tc = _pick(width, (1408, 2048, 1024, 512, 256, 128))
    per = width // tc
    nk = per * len(pieces)
    npc = len(pieces)
    rows = _pick(tm, (256, 128))

    def body(*refs):
        p_refs = refs[:npc]
        w_ref, x_hbm, g_ref, dr_hbm = refs[npc:npc + 4]
        dx_ref, dg_ref, acc_ref, x_buf, dr_buf, sems = refs[-6:]
        i = pl.program_id(0)
        k = pl.program_id(1)
        tile = pl.ds(pl.multiple_of(i * tm, tm), tm)
        fetch_x = pltpu.make_async_copy(x_hbm.at[tile, :], x_buf, sems.at[0])
        fetch_dr = pltpu.make_async_copy(dr_hbm.at[tile, :], dr_buf, sems.at[1])

        @pl.when(k == 0)
        def _():
            fetch_x.start()
            fetch_dr.start()
            acc_ref[...] = jnp.zeros_like(acc_ref)

        @pl.when((i == 0) & (k == 0))
        def _():
            dg_ref[...] = jnp.zeros_like(dg_ref)

        for p in range(npc):
            @pl.when((k >= p * per) & (k < (p + 1) * per))
            def _(p=p):
                acc_ref[...] += _dot(p_refs[p][...], w_ref[...], 'nt')

        @pl.when(k == nk - 1)
        def _():
            fetch_x.wait()
            fetch_dr.wait()

            def chunk(c, carry):
                rs = pl.ds(pl.multiple_of(c * rows, rows), rows)
                dh = acc_ref[rs, :]
                xv = x_buf[rs, :]
                r = lax.rsqrt(jnp.mean(xv * xv, axis=-1, keepdims=True) + EPS)
                xr = xv * r
                dyg = dh * g_ref[...]
                dx_ref[rs, :] = dr_buf[rs, :] + r * (dyg - xr * jnp.mean(dyg * xr, axis=-1, keepdims=True))
                dg_ref[...] += jnp.sum(dh * xr, axis=0, keepdims=True)
                return carry

            lax.fori_loop(0, tm // rows, chunk, 0)

    def piece_spec(p):
        return pl.BlockSpec((tm, tc), lambda i, k: (i, jnp.clip(k - p * per, 0, per - 1)))

    row = pl.BlockSpec((tm, Dm), lambda i, k: (i, 0))
    vec = pl.BlockSpec((1, Dm), lambda i, k: (0, 0))
    extra = () if after is None else (after,)
    return pl.pallas_call(
        body, name=name, grid=(T // tm, nk),
        in_specs=[piece_spec(p) for p in range(npc)] + [pl.BlockSpec((Dm, tc), lambda i, k: (0, k)), ANY, vec, ANY]
        + [ANY] * len(extra),
        out_specs=[row, vec], out_shape=[jax.ShapeDtypeStruct((T, Dm), F32), jax.ShapeDtypeStruct((1, Dm), F32)],
        scratch_shapes=[pltpu.VMEM((tm, Dm), F32), pltpu.VMEM((tm, Dm), F32), pltpu.VMEM((tm, Dm), F32),
                        pltpu.SemaphoreType.DMA((2,))],
        compiler_params=_params(("arbitrary", "arbitrary")))(*pieces, w, x, g, dres, *extra)


def _dw_pieces(a, pieces, name):
    C, M = a.shape
    width = pieces[0].shape[1]
    npc = len(pieces)
    tm = _pick(M, (1024, 512, 256, 128))
    tn = _pick(width, (1408, 1024, 512, 256, 128))
    tc = _pick(C, (2048, 1024, 512, 256, 128))
    per = width // tn
    nk = C // tc

    def body(*refs):
        a_ref = refs[0]
        p_refs = refs[1:1 + npc]
        o_ref, acc_ref = refs[-2:]
        j = pl.program_id(1)
        k = pl.program_id(2)

        @pl.when(k == 0)
        def _():
            acc_ref[...] = jnp.zeros_like(acc_ref)

        for p in range(npc):
            @pl.when((j >= p * per) & (j < (p + 1) * per))
            def _(p=p):
                acc_ref[...] += _dot(a_ref[...], p_refs[p][...], 'tn')

        @pl.when(k == nk - 1)
        def _():
            o_ref[...] = acc_ref[...].astype(o_ref.dtype)

    def piece_spec(p):
        return pl.BlockSpec((tc, tn), lambda i, j, k: (k, jnp.clip(j - p * per, 0, per - 1)))

    return pl.pallas_call(
        body, name=name, grid=(M // tm, per * npc, nk),
        in_specs=[pl.BlockSpec((tc, tm), lambda i, j, k: (k, i))] + [piece_spec(p) for p in range(npc)],
        out_specs=pl.BlockSpec((tm, tn), lambda i, j, k: (i, j)),
        out_shape=jax.ShapeDtypeStruct((M, width * npc), BF16), scratch_shapes=[pltpu.VMEM((tm, tn), F32)],
        compiler_params=_params(("parallel", "parallel", "arbitrary")))(a, *pieces)


def _mix_fwd(c_act, o_att, o_mem, proj, b_gate, x1, w_pw, w_o, w_mo, w_out, next_g):
    T, Dm = x1.shape
    W = c_act.shape[1]
    tm = _pick(T, (256, 128, 64, 32, 16, 8))

    def body(c_ref, oa_ref, om_ref, gl_ref, bg_ref, x1_ref, wpw_ref, wo_ref, wmo_ref, wout_ref, ng_ref,
             x2_ref, yc_ref, ya_ref, ym_ref, h_ref):
        yc = _dot(c_ref[...], wpw_ref[...])
        ya = _dot(oa_ref[...], wo_ref[...])
        ym = _dot(om_ref[...], wmo_ref[...])
        g = _sig(gl_ref[...].astype(F32) + bg_ref[...])
        y = g[:, :Dm] * yc + g[:, Dm:2 * Dm] * ya + g[:, 2 * Dm:] * ym
        x2 = x1_ref[...] + _dot(y, wout_ref[...])
        x2_ref[...] = x2
        r = lax.rsqrt(jnp.mean(x2 * x2, axis=-1, keepdims=True) + EPS)
        h_ref[...] = ((x2 * r) * ng_ref[...]).astype(BF16)
        yc_ref[...] = yc.astype(BF16)
        ya_ref[...] = ya.astype(BF16)
        ym_ref[...] = ym.astype(BF16)

    rowW = pl.BlockSpec((tm, W), lambda i: (i, 0))
    rowD = pl.BlockSpec((tm, Dm), lambda i: (i, 0))
    full = lambda s: pl.BlockSpec(s, lambda i: (0, 0))
    return pl.pallas_call(
        body, name="mix_fwd", grid=(T // tm,),
        in_specs=[rowW, rowW, rowW, pl.BlockSpec((tm, 3 * Dm), lambda i: (i, 1)), full((1, 3 * Dm)), rowD,
                  full((W, Dm)), full((W, Dm)), full((W, Dm)), full((Dm, Dm)), full((1, Dm))],
        out_specs=[rowD] * 5,
        out_shape=[jax.ShapeDtypeStruct((T, Dm), F32)] + [jax.ShapeDtypeStruct((T, Dm), BF16)] * 4,
        compiler_params=_params(("parallel",)))(c_act, o_att, o_mem, proj, b_gate, x1, w_pw, w_o, w_mo, w_out,
                                                next_g)


def _mix_bwd(dx2, yc, ya, ym, c_act, o_att, o_mem, proj, b_gate, w_pw, w_o, w_mo, w_out):
    T, Dm = dx2.shape
    W = w_pw.shape[0]
    tm = _pick(T, (256, 128, 64, 32, 16, 8))
    nt = T // tm

    def body(dx_ref, yc_ref, ya_ref, ym_ref, c_ref, oa_ref, om_ref, gl_ref, bg_ref, wpw_ref, wo_ref, wmo_ref,
             wout_ref, dgl_ref, dbg_ref, dc_ref, doa_ref, dom_ref, gpw_ref, go_ref, gmo_ref, gout_ref,
             apw, ao, amo, aout):
        i = pl.program_id(0)

        @pl.when(i == 0)
        def _():
            dbg_ref[...] = jnp.zeros_like(dbg_ref)
            for acc in (apw, ao, amo, aout):
                acc[...] = jnp.zeros_like(acc)

        dxv = dx_ref[...].astype(BF16)
        dy = _dot(dxv, wout_ref[...], 'nt')
        g = _sig(gl_ref[...].astype(F32) + bg_ref[...])
        branches = ((yc_ref, c_ref, wpw_ref, dc_ref, apw), (ya_ref, oa_ref, wo_ref, doa_ref, ao),
                    (ym_ref, om_ref, wmo_ref, dom_ref, amo))
        y = jnp.zeros((tm, Dm), F32)
        for n, (y_ref, in_ref, w_ref, dk_ref, acc) in enumerate(branches):
            gk = g[:, n * Dm:(n + 1) * Dm]
            yk = y_ref[...].astype(F32)
            dyk = dy * gk
            dgl = dyk * yk * (1.0 - gk)
            dgl_ref[:, n * Dm:(n + 1) * Dm] = dgl.astype(BF16)
            dbg_ref[:, n * Dm:(n + 1) * Dm] += jnp.sum(dgl, axis=0, keepdims=True)
            dyk = dyk.astype(BF16)
            dk_ref[...] = _dot(dyk, w_ref[...], 'nt').astype(BF16)
            acc[...] += _dot(in_ref[...], dyk, 'tn')
            y = y + gk * yk
        aout[...] += _dot(y, dxv, 'tn')

        @pl.when(i == nt - 1)
        def _():
            for acc, out in ((apw, gpw_ref), (ao, go_ref), (amo, gmo_ref), (aout, gout_ref)):
                out[...] = acc[...].astype(BF16)

    rowW = pl.BlockSpec((tm, W), lambda i: (i, 0))
    rowD = pl.BlockSpec((tm, Dm), lambda i: (i, 0))
    full = lambda s: pl.BlockSpec(s, lambda i: (0, 0))
    return pl.pallas_call(
        body, name="mix_bwd", grid=(nt,),
        in_specs=[rowD, rowD, rowD, rowD, rowW, rowW, rowW, pl.BlockSpec((tm, 3 * Dm), lambda i: (i, 1)),
                  full((1, 3 * Dm)), full((W, Dm)), full((W, Dm)), full((W, Dm)), full((Dm, Dm))],
        out_specs=[pl.BlockSpec((tm, 3 * Dm), lambda i: (i, 1)), full((1, 3 * Dm)), rowW, rowW, rowW,
                   full((W, Dm)), full((W, Dm)), full((W, Dm)), full((Dm, Dm))],
        out_shape=[jax.ShapeDtypeStruct((T, 6 * Dm), BF16), jax.ShapeDtypeStruct((1, 3 * Dm), F32)]
        + [jax.ShapeDtypeStruct((T, W), BF16)] * 3 + [jax.ShapeDtypeStruct((W, Dm), BF16)] * 3
        + [jax.ShapeDtypeStruct((Dm, Dm), BF16)],
        scratch_shapes=[pltpu.VMEM((W, Dm), F32)] * 3 + [pltpu.VMEM((Dm, Dm), F32)],
        compiler_params=_params(("arbitrary",)))(dx2, yc, ya, ym, c_act, o_att, o_mem, proj, b_gate, w_pw, w_o,
                                                 w_mo, w_out)


def _ln_swish(cv, lg, lb):
    mu = jnp.mean(cv, axis=-1, keepdims=True)
    xc = cv - mu
    r = lax.rsqrt(jnp.mean(xc * xc, axis=-1, keepdims=True) + EPS)
    n = xc * r
    l = n * lg + lb
    return r, n, l


def _shift_copies(src, r0, win, shifts):
    win[...] = src[pl.ds(r0, CONV_TILE + CONV_PAD + 8), :]
    for s in range(8):
        shifts[s] = win[s:s + CONV_TILE + CONV_PAD, :]


def _tap(shifts, d):
    return shifts[d % 8, d - d % 8:d - d % 8 + CONV_TILE, :]


def _conv_fwd(proj3, dw_w, dw_b, ln_g, ln_b):
    Bl, S, _ = proj3.shape
    C, K, TS, PAD = CONV_WIDTH, CONV_KERNEL, CONV_TILE, CONV_PAD
    nt = S // TS

    def body(u_ref, w_ref, b_ref, lg_ref, lb_ref, cv_ref, c_ref, vbuf, win, shifts):
        vbuf[0:PAD, :] = jnp.zeros((PAD, C), F32)
        vbuf[S + PAD:S + PAD + 8, :] = jnp.zeros((8, C), F32)

        def glu(t, carry):
            r0 = pl.multiple_of(t * TS, TS)
            u = u_ref[pl.ds(r0, TS), :].astype(F32)
            vbuf[pl.ds(PAD + r0, TS), :] = u[:, :C] * _sig(u[:, C:])
            return carry

        lax.fori_loop(0, nt, glu, 0)

        def conv(t, carry):
            r0 = pl.multiple_of(t * TS, TS)
            _shift_copies(vbuf, r0, win, shifts)
            acc = jnp.zeros((TS, C), F32)
            for j in range(K):
                acc = acc + w_ref[j:j + 1, :] * _tap(shifts, PAD - (K - 1) + j)
            cv = acc + b_ref[...]
            cv_ref[pl.ds(r0, TS), :] = cv
            _, _, l = _ln_swish(cv, lg_ref[...], lb_ref[...])
            c_ref[pl.ds(r0, TS), :] = (l * _sig(l)).astype(BF16)
            return carry

        lax.fori_loop(0, nt, conv, 0)

    vec = pl.BlockSpec((1, C), lambda b: (0, 0))
    return pl.pallas_call(
        body, name="conv_fwd", grid=(Bl,),
        in_specs=[pl.BlockSpec((None, S, 2 * C), lambda b: (b, 0, 0)), pl.BlockSpec((K, C), lambda b: (0, 0)),
                  vec, vec, vec],
        out_specs=[pl.BlockSpec((None, S, C), lambda b: (b, 0, 0))] * 2,
        out_shape=[jax.ShapeDtypeStruct((Bl, S, C), F32), jax.ShapeDtypeStruct((Bl, S, C), BF16)],
        scratch_shapes=[pltpu.VMEM((S + PAD + 8, C), F32), pltpu.VMEM((TS + PAD + 8, C), F32),
                        pltpu.VMEM((8, TS + PAD, C), F32)],
        compiler_params=_params(("parallel",)))(proj3, dw_w, dw_b, ln_g, ln_b)


def _conv_bwd(proj3, cv, dc, dw_w, ln_g, ln_b, dproj3):
    Bl, S, _ = proj3.shape
    C, K, TS, PAD = CONV_WIDTH, CONV_KERNEL, CONV_TILE, CONV_PAD
    nt = S // TS

    def body(u_ref, cv_ref, dc_ref, w_ref, lg_ref, lb_ref, through_ref, du_ref, dw_ref, db_ref, dlg_ref, dlb_ref,
             vbuf, gbuf, win, shifts, dwacc):
        b = pl.program_id(0)

        @pl.when(b == 0)
        def _():
            dw_ref[...] = jnp.zeros_like(dw_ref)
            db_ref[...] = jnp.zeros_like(db_ref)
            dlg_ref[...] = jnp.zeros_like(dlg_ref)
            dlb_ref[...] = jnp.zeros_like(dlb_ref)

        vbuf[0:PAD, :] = jnp.zeros((PAD, C), F32)
        vbuf[S + PAD:S + PAD + 8, :] = jnp.zeros((8, C), F32)
        gbuf[S:S + PAD + 8, :] = jnp.zeros((PAD + 8, C), F32)
        dwacc[...] = jnp.zeros_like(dwacc)

        def norm_bwd(t, carry):
            r0 = pl.multiple_of(t * TS, TS)
            u = u_ref[pl.ds(r0, TS), :].astype(F32)
            vbuf[pl.ds(PAD + r0, TS), :] = u[:, :C] * _sig(u[:, C:])
            r, n, l = _ln_swish(cv_ref[pl.ds(r0, TS), :], lg_ref[...], lb_ref[...])
            s = _sig(l)
            dl = dc_ref[pl.ds(r0, TS), :].astype(F32) * s * (1.0 + l * (1.0 - s))
            dlg_ref[...] += jnp.sum(dl * n, axis=0, keepdims=True)
            dlb_ref[...] += jnp.sum(dl, axis=0, keepdims=True)
            dn = dl * lg_ref[...]
            dcv = r * (dn - jnp.mean(dn, axis=-1, keepdims=True) - n * jnp.mean(dn * n, axis=-1, keepdims=True))
            gbuf[pl.ds(r0, TS), :] = dcv
            db_ref[...] += jnp.sum(dcv, axis=0, keepdims=True)
            return carry

        lax.fori_loop(0, nt, norm_bwd, 0)

        def conv_bwd(t, carry):
            r0 = pl.multiple_of(t * TS, TS)
            _shift_copies(gbuf, r0, win, shifts)
            dv = jnp.zeros((TS, C), F32)
            for j in range(K):
                dv = dv + w_ref[j:j + 1, :] * _tap(shifts, K - 1 - j)
            u = u_ref[pl.ds(r0, TS), :].astype(F32)
            a, g = u[:, :C], u[:, C:]
            s = _sig(g)
            du_ref[pl.ds(r0, TS), 0:C] = (dv * s).astype(BF16)
            du_ref[pl.ds(r0, TS), C:2 * C] = (dv * a * s * (1.0 - s)).astype(BF16)
            dcv = gbuf[pl.ds(r0, TS), :]
            _shift_copies(vbuf, r0, win, shifts)
            for j in range(K):
                prod = dcv * _tap(shifts, PAD - (K - 1) + j)
                dwacc[j] += jnp.sum(prod.reshape(TS // 8, 8, C), axis=0)
            return carry

        lax.fori_loop(0, nt, conv_bwd, 0)
        dw_ref[...] += jnp.sum(dwacc[...], axis=1)

    vec = pl.BlockSpec((1, C), lambda b: (0, 0))
    seq = lambda w: pl.BlockSpec((None, S, w), lambda b: (b, 0, 0))
    return pl.pallas_call(
        body, name="conv_bwd", grid=(Bl,),
        in_specs=[seq(2 * C), seq(C), seq(C), pl.BlockSpec((K, C), lambda b: (0, 0)), vec, vec, ANY],
        out_specs=[seq(2 * C), pl.BlockSpec((K, C), lambda b: (0, 0)), vec, vec, vec],
        out_shape=[jax.ShapeDtypeStruct(dproj3.shape, BF16), jax.ShapeDtypeStruct((K, C), F32)]
        + [jax.ShapeDtypeStruct((1, C), F32)] * 3,
        input_output_aliases={6: 0},
        scratch_shapes=[pltpu.VMEM((S + PAD + 8, C), F32), pltpu.VMEM((S + PAD + 8, C), F32),
                        pltpu.VMEM((TS + PAD + 8, C), F32), pltpu.VMEM((8, TS + PAD, C), F32),
                        pltpu.VMEM((K, 8, C), F32)],
        compiler_params=_params(("arbitrary",)))(proj3, cv, dc, dw_w, ln_g, ln_b, dproj3)


def _att_bias(rel_bias):
    H = rel_bias.shape[0]
    Wd = KW + QB
    c = jnp.arange(Wd)
    by_offset = rel_bias[:, jnp.clip(KW - c, -(CHUNK - 1), MAX_REL) + (CHUNK - 1)].reshape(H, 1, Wd)

    def body(t_ref, o_ref):
        rows = jnp.broadcast_to(t_ref[...], (QB, Wd))
        skew = pltpu.roll(rows, 0, 1, stride=1, stride_axis=0)[:, QB:]
        qi = lax.broadcasted_iota(jnp.int32, (QB, KW), 0)
        kj = lax.broadcasted_iota(jnp.int32, (QB, KW), 1)
        dchunk = ((KW - QB) + qi) // CHUNK - kj // CHUNK
        o_ref[...] = jnp.where((dchunk >= 0) & (dchunk <= LEFT_CHUNKS), skew, MASK_VALUE)

    return pl.pallas_call(
        body, name="att_bias", grid=(H,), in_specs=[pl.BlockSpec((None, 1, Wd), lambda h: (h, 0, 0))],
        out_specs=pl.BlockSpec((None, QB, KW), lambda h: (h, 0, 0)),
        out_shape=jax.ShapeDtypeStruct((H, QB, KW), F32), compiler_params=_params(("parallel",)))(by_offset)


def _head_masks():
    lane = lax.broadcasted_iota(jnp.int32, (1, 128), 1)
    return (lane < 64, lane >= 64)


def _att_probs(qh, k2, bias, valid):
    s = _dot(qh, k2, 'nt') * ATT_SCALE + bias
    s = jnp.where(valid, s, MASK_VALUE)
    e = jnp.exp(s - jnp.max(s, axis=-1, keepdims=True))
    return e * (1.0 / jnp.sum(e, axis=-1, keepdims=True))


def _att_specs(S, q_col):
    nb = S // QB
    q_spec = pl.BlockSpec((None, QB, ATT_WIDTH), lambda b, i: (b, jnp.minimum(i, nb - 1), q_col))

    def kv_spec(col, kb):
        return pl.BlockSpec((None, QB, ATT_WIDTH),
                            lambda b, i: (b, jnp.clip(i - 2 + kb, 0, nb - 1), col))

    return q_spec, [kv_spec(3, kb) for kb in range(3)], [kv_spec(4, kb) for kb in range(3)]


def _att_fwd(proj3, bias):
    Bl, S, _ = proj3.shape
    nb = S // QB
    q_spec, k_specs, v_specs = _att_specs(S, 2)

    def body(q_ref, k0, k1, k2r, v0, v1, v2r, bias_ref, o_ref):
        i = pl.program_id(1)
        masks = _head_masks()
        valid = lax.broadcasted_iota(jnp.int32, (QB, KW), 1) >= (2 - i) * QB
        for pr in range(ATT_HEADS // 2):
            ls = slice(128 * pr, 128 * (pr + 1))
            q2 = q_ref[:, ls]
            k2 = jnp.concatenate([k0[:, ls], k1[:, ls], k2r[:, ls]], axis=0)
            v2 = jnp.concatenate([v0[:, ls], v1[:, ls], v2r[:, ls]], axis=0)
            o2 = jnp.zeros((QB, 128), F32)
            for hh in range(2):
                p = _att_probs(jnp.where(masks[hh], q2, 0), k2, bias_ref[2 * pr + hh], valid)
                o2 = o2 + _dot(p, jnp.where(masks[hh], v2, 0))
            o_ref[:, ls] = o2.astype(BF16)

    return pl.pallas_call(
        body, name="att_fwd", grid=(Bl, nb),
        in_specs=[q_spec] + k_specs + v_specs + [pl.BlockSpec((ATT_HEADS, QB, KW), lambda b, i: (0, 0, 0))],
        out_specs=pl.BlockSpec((None, QB, ATT_WIDTH), lambda b, i: (b, i, 0)),
        out_shape=jax.ShapeDtypeStruct((Bl, S, ATT_WIDTH), BF16),
        compiler_params=_params(("parallel", "arbitrary")))(*([proj3] * 7), bias)


def _att_bwd(proj3, do, bias, dproj3):
    Bl, S, _ = proj3.shape
    nb = S // QB
    q_spec, k_specs, v_specs = _att_specs(S, 2)
    do_spec = pl.BlockSpec((None, QB, ATT_WIDTH), lambda b, i: (b, jnp.minimum(i, nb - 1), 0))
    kv_out = pl.BlockSpec((None, QB, ATT_WIDTH), lambda b, i: (b, jnp.clip(i - 2, 0, nb - 1), 0))
    bias_spec = pl.BlockSpec((ATT_HEADS, QB, KW), lambda b, i: (0, 0, 0))

    def body(q_ref, k0, k1, k2r, v0, v1, v2r, do_ref, bias_ref, through_ref, dq_ref, dk_ref, dv_ref, db_ref,
             dkw, dvw):
        b = pl.program_id(0)
        i = pl.program_id(1)

        @pl.when((b == 0) & (i == 0))
        def _():
            db_ref[...] = jnp.zeros_like(db_ref)

        @pl.when(i == 0)
        def _():
            dkw[...] = jnp.zeros_like(dkw)
            dvw[...] = jnp.zeros_like(dvw)

        @pl.when(i < nb)
        def _():
            masks = _head_masks()
            valid = lax.broadcasted_iota(jnp.int32, (QB, KW), 1) >= (2 - i) * QB
            for pr in range(ATT_HEADS // 2):
                ls = slice(128 * pr, 128 * (pr + 1))
                q2 = q_ref[:, ls]
                do2 = do_ref[:, ls]
                k2 = jnp.concatenate([k0[:, ls], k1[:, ls], k2r[:, ls]], axis=0)
                v2 = jnp.concatenate([v0[:, ls], v1[:, ls], v2r[:, ls]], axis=0)
                dq2 = jnp.zeros((QB, 128), F32)
                dk2 = jnp.zeros((KW, 128), F32)
                dv2 = jnp.zeros((KW, 128), F32)
                for hh in range(2):
                    h = 2 * pr + hh
                    qh = jnp.where(masks[hh], q2, 0)
                    doh = jnp.where(masks[hh], do2, 0)
                    p = _att_probs(qh, k2, bias_ref[h], valid)
                    dp = _dot(doh, v2, 'nt')
                    ds = p * (dp - jnp.sum(p * dp, axis=-1, keepdims=True))
                    db_ref[h] += ds
                    dq2 = dq2 + _dot(ds, jnp.where(masks[hh], k2, 0))
                    dk2 = dk2 + _dot(ds, qh, 'tn')
                    dv2 = dv2 + _dot(p, doh, 'tn')
                dq_ref[:, ls] = (dq2 * ATT_SCALE).astype(BF16)
                dkw[:, ls] += dk2 * ATT_SCALE
                dvw[:, ls] += dv2

        dk_ref[...] = dkw[0:QB, :].astype(BF16)
        dv_ref[...] = dvw[0:QB, :].astype(BF16)
        for buf in (dkw, dvw):
            rest = buf[QB:KW, :]
            buf[0:KW - QB, :] = rest
            buf[KW - QB:KW, :] = jnp.zeros((QB, ATT_WIDTH), F32)

    blk = jax.ShapeDtypeStruct((Bl, S, ATT_WIDTH), BF16)
    return pl.pallas_call(
        body, name="att_bwd", grid=(Bl, nb + 2),
        in_specs=[q_spec] + k_specs + v_specs + [do_spec, bias_spec, ANY],
        out_specs=[q_spec, kv_out, kv_out, bias_spec],
        out_shape=[jax.ShapeDtypeStruct(dproj3.shape, BF16), blk, blk,
                   jax.ShapeDtypeStruct((ATT_HEADS, QB, KW), F32)],
        input_output_aliases={9: 0},
        scratch_shapes=[pltpu.VMEM((KW, ATT_WIDTH), F32), pltpu.VMEM((KW, ATT_WIDTH), F32)],
        compiler_params=_params(("arbitrary", "arbitrary")))(*([proj3] * 7), do, bias, dproj3)


def _rel_bias_grad(dbias):
    H = dbias.shape[0]
    Wd = KW + QB
    padded = jnp.pad(dbias, ((0, 0), (0, 1), (QB, 0)))
    skew = padded.reshape(H, (QB + 1) * Wd)[:, :QB * (Wd + 1)].reshape(H, QB, Wd + 1)[:, :, :Wd]
    c = jnp.arange(Wd)[:, None]
    bins = (jnp.clip(KW - c, -(CHUNK - 1), MAX_REL) + (CHUNK - 1) == jnp.arange(N_REL)[None, :]).astype(F32)

    def body(s_ref, bins_ref, o_ref):
        col = jnp.sum(s_ref[...], axis=1)
        o_ref[...] = jnp.dot(col, bins_ref[...], preferred_element_type=F32, precision=lax.Precision.HIGHEST)

    return pl.pallas_call(
        body, name="rel_bias_grad", grid=(1,),
        in_specs=[pl.BlockSpec((H, QB, Wd), lambda i: (0, 0, 0)), pl.BlockSpec((Wd, N_REL), lambda i: (0, 0))],
        out_specs=pl.BlockSpec((H, N_REL), lambda i: (0, 0)), out_shape=jax.ShapeDtypeStruct((H, N_REL), F32),
        compiler_params=_params(("arbitrary",)))(skew, bins)


MEM_TILE = 512


def _mem_probs(qh, kh):
    s = _dot(qh, kh, 'nt') * MEM_SCALE
    e = jnp.exp(s - jnp.max(s, axis=-1, keepdims=True))
    return e * (1.0 / jnp.sum(e, axis=-1, keepdims=True))


def _mem_fwd(proj3, kv3):
    Bl, S, _ = proj3.shape
    tq = _pick(S, (MEM_TILE, 256))
    hd = MEM_HEAD_DIM

    def body(q_ref, kv_ref, o_ref):
        for h in range(MEM_HEADS):
            p = _mem_probs(q_ref[:, h * hd:(h + 1) * hd], kv_ref[:, h * hd:(h + 1) * hd])
            o_ref[:, h * hd:(h + 1) * hd] = _dot(p, kv_ref[:, MEM_WIDTH + h * hd:MEM_WIDTH + (h + 1) * hd]).astype(BF16)

    return pl.pallas_call(
        body, name="mem_fwd", grid=(Bl, S // tq),
        in_specs=[pl.BlockSpec((None, tq, MEM_WIDTH), lambda b, i: (b, i, 5)),
                  pl.BlockSpec((None, MEM_LEN, 2 * MEM_WIDTH), lambda b, i: (b, 0, 0))],
        out_specs=pl.BlockSpec((None, tq, MEM_WIDTH), lambda b, i: (b, i, 0)),
        out_shape=jax.ShapeDtypeStruct((Bl, S, MEM_WIDTH), BF16),
        compiler_params=_params(("parallel", "parallel")))(proj3, kv3)


def _mem_bwd(proj3, kv3, do, dproj3):
    Bl, S, _ = proj3.shape
    tq = _pick(S, (MEM_TILE, 256))
    hd = MEM_HEAD_DIM

    def body(q_ref, kv_ref, do_ref, through_ref, dq_ref, dkv_ref):
        i = pl.program_id(1)

        @pl.when(i == 0)
        def _():
            dkv_ref[...] = jnp.zeros_like(dkv_ref)

        for h in range(MEM_HEADS):
            ks = slice(h * hd, (h + 1) * hd)
            vs = slice(MEM_WIDTH + h * hd, MEM_WIDTH + (h + 1) * hd)
            qh, kh, vh, doh = q_ref[:, ks], kv_ref[:, ks], kv_ref[:, vs], do_ref[:, ks]
            p = _mem_probs(qh, kh)
            dp = _dot(doh, vh, 'nt')
            ds = p * (dp - jnp.sum(p * dp, axis=-1, keepdims=True))
            dq_ref[:, ks] = (_dot(ds, kh) * MEM_SCALE).astype(BF16)
            dkv_ref[:, ks] += _dot(ds, qh, 'tn') * MEM_SCALE
            dkv_ref[:, vs] += _dot(p, doh, 'tn')

    return pl.pallas_call(
        body, name="mem_bwd", grid=(Bl, S // tq),
        in_specs=[pl.BlockSpec((None, tq, MEM_WIDTH), lambda b, i: (b, i, 5)),
                  pl.BlockSpec((None, MEM_LEN, 2 * MEM_WIDTH), lambda b, i: (b, 0, 0)),
                  pl.BlockSpec((None, tq, MEM_WIDTH), lambda b, i: (b, i, 0)), ANY],
        out_specs=[pl.BlockSpec((None, tq, MEM_WIDTH), lambda b, i: (b, i, 5)),
                   pl.BlockSpec((None, MEM_LEN, 2 * MEM_WIDTH), lambda b, i: (b, 0, 0))],
        out_shape=[jax.ShapeDtypeStruct(dproj3.shape, BF16),
                   jax.ShapeDtypeStruct((Bl, MEM_LEN, 2 * MEM_WIDTH), F32)],
        input_output_aliases={3: 0},
        compiler_params=_params(("parallel", "arbitrary")))(proj3, kv3, do, dproj3)


def _position():
    x, y, c = lax.axis_index("x"), lax.axis_index("y"), lax.axis_index("c")
    return x, y, c, 4 * x + 2 * y + c


def _device(idx):
    return ((idx >> 2) & 1, (idx >> 1) & 1, idx & 1)


def _half_block(ref, axis, shard_shape, k, h):
    R, Cn = shard_shape
    if axis == 1:
        return ref.at[pl.ds(h * (R // 2), R // 2), pl.ds(k * Cn, Cn)]
    return ref.at[pl.ds(k * R + h * (R // 2), R // 2), :]


def _block(ref, axis, shard_shape, k):
    R, Cn = shard_shape
    if axis == 1:
        return ref.at[:, pl.ds(k * Cn, Cn)]
    return ref.at[pl.ds(k * R, R), :]


def _half(ref, h):
    R = ref.shape[0]
    return ref.at[pl.ds(h * (R // 2), R // 2), :]


ANY = pl.BlockSpec(memory_space=pl.ANY)


HBM = pl.BlockSpec(memory_space=pltpu.HBM)
SEM = pl.BlockSpec(memory_space=pltpu.SEMAPHORE)
VMEM_WHOLE = pl.BlockSpec(memory_space=pltpu.VMEM)
EFFECT = pltpu.SideEffectType.DATAFLOW_SIDE_EFFECTING


def _in_hbm(a):
    return pltpu.with_memory_space_constraint(a, pltpu.HBM)


def _split_start(body, name, sources, lands, n_copies):
    n = len(sources)
    out_shape, out_specs = [], []
    for _ in range(n):
        out_shape += [pltpu.SemaphoreType.DMA((n_copies,)), pltpu.SemaphoreType.DMA((n_copies,))]
        out_specs += [SEM, SEM]
    out_shape += [pltpu.HBM(a.shape, a.dtype) for a in list(sources) + list(lands)]
    out_specs += [HBM] * (2 * n)
    out_shape.append(jax.ShapeDtypeStruct((8, 128), F32))
    out_specs.append(VMEM_WHOLE)

    def call_body(*refs):
        srcs, lnds = refs[:n], refs[n:2 * n]
        sems = refs[2 * n:4 * n]
        token = refs[-1]
        body(srcs, lnds, sems[0::2], sems[1::2])
        token[...] = jnp.zeros_like(token)

    res = pl.pallas_call(
        call_body, name=name, in_specs=[HBM] * (2 * n), out_specs=out_specs, out_shape=out_shape,
        input_output_aliases={i: 2 * n + i for i in range(2 * n)},
        compiler_params=pltpu.CompilerParams(has_side_effects=EFFECT))(
            *[_in_hbm(a) for a in list(sources) + list(lands)])
    pairs = [(res[2 * w], res[2 * w + 1], res[2 * n + w], res[3 * n + w]) for w in range(n)]
    return pairs, res[-1]


def _split_wait(body, name, pairs, after):
    n = len(pairs)

    def call_body(*refs):
        srcs, lnds = refs[:n], refs[n:2 * n]
        sems = refs[2 * n:4 * n]
        body(srcs, lnds, sems[0::2], sems[1::2])

    args = [_in_hbm(p[2]) for p in pairs] + [_in_hbm(p[3]) for p in pairs]
    for p in pairs:
        args += [p[0], p[1]]
    res = pl.pallas_call(
        call_body, name=name, in_specs=[HBM] * (2 * n) + [SEM] * (2 * n) + [ANY], out_specs=[HBM] * (2 * n),
        out_shape=[pltpu.HBM(a.shape, a.dtype) for a in args[:2 * n]],
        input_output_aliases={i: i for i in range(2 * n)},
        compiler_params=pltpu.CompilerParams(has_side_effects=EFFECT))(*args, after)
    return res[:n], res[n:]


def _place_block(shard, axis, chip_idx, name, after=None):
    R, Cn = shard.shape
    tr = _pick(R, (256, 176, 128, 64, 32, 16, 8))
    nblk = R // tr

    def body(k_ref, s_ref, *rest):
        rest[-1][...] = s_ref[...]

    if axis == 1:
        out_shape, out_index = (R, 4 * Cn), lambda i, k: (i, k[0])
    else:
        out_shape, out_index = (4 * R, Cn), lambda i, k: (k[0] * nblk + i, 0)
    extra = () if after is None else (after,)
    return pl.pallas_call(
        body, name=name,
        grid_spec=pltpu.PrefetchScalarGridSpec(
            num_scalar_prefetch=1, grid=(nblk,),
            in_specs=[pl.BlockSpec((tr, Cn), lambda i, k: (i, 0))] + [ANY] * len(extra),
            out_specs=pl.BlockSpec((tr, Cn), out_index)),
        out_shape=jax.ShapeDtypeStruct(out_shape, shard.dtype),
        compiler_params=_params(("parallel",)))(chip_idx, shard, *extra)


def _gather_copy(srcs, lnds, send, recv, axes, shapes, w, j, me):
    chip = me >> 1
    return (pltpu.make_async_remote_copy(
        src_ref=srcs[w], dst_ref=_block(lnds[w], axes[w], shapes[w], chip), send_sem=send[w].at[j],
        recv_sem=recv[w].at[j], device_id=_device(me ^ (2 * (j + 1))), device_id_type=MESH),
            pltpu.make_async_remote_copy(
        src_ref=srcs[w], dst_ref=_block(lnds[w], axes[w], shapes[w], chip ^ (j + 1)), send_sem=send[w].at[j],
        recv_sem=recv[w].at[j], device_id=_device(me ^ (2 * (j + 1))), device_id_type=MESH))


def _gather_start(shards, lands, axes, name):
    shapes = [s.shape for s in shards]

    def body(srcs, lnds, send, recv):
        x, y, c, me = _position()
        for w in range(len(shards)):
            for j in range(3):
                _gather_copy(srcs, lnds, send, recv, axes, shapes, w, j, me)[0].start()

    return _split_start(body, name, shards, lands, 3)


def _gather_wait(pairs, axes, after, name):
    shapes = [p[2].shape for p in pairs]

    def body(srcs, lnds, send, recv):
        x, y, c, me = _position()
        for w in range(len(pairs)):
            for j in range(3):
                sent, landed = _gather_copy(srcs, lnds, send, recv, axes, shapes, w, j, me)
                sent.wait_send()
                landed.wait_recv()

    return _split_wait(body, name, pairs, after)[1]


def _shard_shape(grad, axis):
    return (grad.shape[0], grad.shape[1] // 4) if axis == 1 else (grad.shape[0] // 4, grad.shape[1])


def _scatter_copy(srcs, lnds, send, recv, axes, shapes, w, m, me):
    peer = me ^ m
    return pltpu.make_async_remote_copy(
        src_ref=_half_block(srcs[w], axes[w], shapes[w], peer >> 1, peer & 1), dst_ref=lnds[w].at[m - 1],
        send_sem=send[w].at[m - 1], recv_sem=recv[w].at[m - 1], device_id=_device(peer), device_id_type=MESH)


def _scatter_start(grads, axes, name):
    shapes = [_shard_shape(g, a) for g, a in zip(grads, axes)]
    lands = [lax.empty((N_DEV - 1, R // 2, Cn), BF16) for R, Cn in shapes]

    def body(srcs, lnds, send, recv):
        x, y, c, me = _position()
        for w in range(len(grads)):
            for m in range(1, N_DEV):
                _scatter_copy(srcs, lnds, send, recv, axes, shapes, w, m, me).start()

    return _split_start(body, name, grads, lands, N_DEV - 1)


def _scatter_wait(pairs, axes, after):
    shapes = [_shard_shape(p[2], a) for p, a in zip(pairs, axes)]

    def body(srcs, lnds, send, recv):
        x, y, c, me = _position()
        for w in range(len(pairs)):
            for m in range(1, N_DEV):
                cp = _scatter_copy(srcs, lnds, send, recv, axes, shapes, w, m, me)
                cp.wait_send()
                cp.wait_recv()

    return _split_wait(body, "scatter_wait", pairs, after)


def _sum_partials(own, parts, half, name):
    R, Cn = own.shape
    tr = _pick(R, (256, 176, 128, 64, 32, 16, 8))
    nblk = R // tr

    def body(half_ref, own_ref, p_ref, o_ref):
        acc = own_ref[...].astype(F32)
        for d in range(N_DEV - 1):
            acc = acc + p_ref[d].astype(F32)
        o_ref[...] = acc

    return pl.pallas_call(
        body, name=name,
        grid_spec=pltpu.PrefetchScalarGridSpec(
            num_scalar_prefetch=1, grid=(nblk,),
            in_specs=[pl.BlockSpec((tr, Cn), lambda i, hr: (i, 0)),
                      pl.BlockSpec((N_DEV - 1, tr, Cn), lambda i, hr: (0, i, 0))],
            out_specs=pl.BlockSpec((tr, Cn), lambda i, hr: (hr[0] * nblk + i, 0))),
        out_shape=jax.ShapeDtypeStruct((2 * R, Cn), F32),
        compiler_params=_params(("parallel",)))(half, own, parts)


def _exchange_halves(grads):
    n = len(grads)

    def body(*refs):
        outs = refs[n:2 * n]
        send, recv = refs[2 * n:]
        x, y, c, me = _position()

        def copy(w, half):
            rows = _half(outs[w], half)
            return pltpu.make_async_remote_copy(src_ref=rows, dst_ref=rows, send_sem=send.at[w],
                                                recv_sem=recv.at[w], device_id=_device(me ^ 1), device_id_type=MESH)

        for w in range(n):
            copy(w, c).start()
        for w in range(n):
            copy(w, 1 - c).wait_recv()
        for w in range(n):
            copy(w, c).wait_send()

    return pl.pallas_call(
        body, name="exchange_halves", in_specs=[ANY] * n, out_specs=[ANY] * n,
        out_shape=[jax.ShapeDtypeStruct(a.shape, a.dtype) for a in grads],
        input_output_aliases={i: i for i in range(n)},
        scratch_shapes=[pltpu.SemaphoreType.DMA((n,)), pltpu.SemaphoreType.DMA((n,))],
        compiler_params=pltpu.CompilerParams(has_side_effects=True))(*grads)


def _all_reduce_small(vec):
    R, L = vec.shape

    def body(v_ref, o_ref, buf, send, recv):
        x, y, c, me = _position()
        buf[me] = v_ref[...]

        def copy(m, slot):
            return pltpu.make_async_remote_copy(src_ref=v_ref, dst_ref=buf.at[slot], send_sem=send.at[m - 1],
                                                recv_sem=recv.at[m - 1], device_id=_device(me ^ m),
                                                device_id_type=MESH)

        for m in range(1, N_DEV):
            copy(m, me).start()
        for m in range(1, N_DEV):
            copy(m, me ^ m).wait_recv()
        for m in range(1, N_DEV):
            copy(m, me).wait_send()
        acc = buf[0]
        for d in range(1, N_DEV):
            acc = acc + buf[d]
        o_ref[...] = acc

    vm = pl.BlockSpec(memory_space=pltpu.VMEM)
    return pl.pallas_call(
        body, name="all_reduce_small", in_specs=[vm], out_specs=vm, out_shape=jax.ShapeDtypeStruct((R, L), F32),
        scratch_shapes=[pltpu.VMEM((N_DEV, R, L), F32), pltpu.SemaphoreType.DMA((N_DEV - 1,)),
                        pltpu.SemaphoreType.DMA((N_DEV - 1,))],
        compiler_params=pltpu.CompilerParams(has_side_effects=True))(vec)


def _adamw(w, g, m, v, name):
    R, Cn = w.shape
    tr = _pick(R, (256, 176, 128, 64, 40, 32, 16, 8))

    def body(w_ref, g_ref, m_ref, v_ref, d_ref, nm_ref, nv_ref):
        gv = g_ref[...]
        nm = ADAM_B1 * m_ref[...] + (1.0 - ADAM_B1) * gv
        nv = ADAM_B2 * v_ref[...] + (1.0 - ADAM_B2) * (gv * gv)
        m_hat = nm / (1.0 - ADAM_B1 ** ADAM_STEP)
        v_hat = nv / (1.0 - ADAM_B2 ** ADAM_STEP)
        d_ref[...] = -ADAM_LR * (m_hat / (jnp.sqrt(v_hat) + ADAM_EPS) + ADAM_WD * w_ref[...])
        nm_ref[...] = nm
        nv_ref[...] = nv

    spec = pl.BlockSpec((tr, Cn), lambda i: (i, 0))
    return pl.pallas_call(
        body, name=name, grid=(R // tr,), in_specs=[spec] * 4, out_specs=[spec] * 3,
        out_shape=[jax.ShapeDtypeStruct((R, Cn), F32)] * 3, compiler_params=_params(("parallel",)))(w, g, m, v)


def _pack(arrays, rows):
    flat = jnp.concatenate([a.reshape(-1).astype(F32) for a in arrays])
    return jnp.pad(flat, (0, rows * 128 - flat.shape[0])).reshape(rows, 128)


def _unpack(packed, shapes):
    flat = packed.reshape(-1)
    out, off = [], 0
    for s in shapes:
        size = 1
        for d in s:
            size *= d
        out.append(flat[off:off + size].reshape(s))
        off += size
    return out


def _ffn_fwd(x, h, arrived, tag, next_g=None):
    w_up = arrived(f"{tag}_w_up", h)
    a, b, act = _ffn_up(h, w_up, f"{tag}_up")
    w_down = arrived(f"{tag}_w_down", act)
    out, h_next = _ffn_down(act, w_down, x, f"{tag}_down", next_g)
    return out, h_next, (h, a, b, act, w_up, w_down)


def _ffn_bwd(dout, x, norm, saved, tag, send):
    h, a, b, act, w_up, w_down = saved
    g_down = _mm(act, dout, 'tn', f"{tag}_down_dw", BF16, scale=0.5)
    token = send([f"{tag}_w_down"], [g_down])
    da, db = _ffn_down_bwd(dout, w_down, a, b, f"{tag}_down_dx", after=token)
    g_up = _dw_pieces(h, [da, db], f"{tag}_up_dw")
    token = send([f"{tag}_w_up"], [g_up])
    return _dx_rms_bwd([da, db], w_up, x, norm, dout, f"{tag}_up_dx", after=token)


def kernel(x, mem, ffn1_norm, ffn1_w_up, ffn1_w_down, mix_norm, mem_norm, w_in, b_gate, conv_dw_w, conv_dw_b, conv_ln_g, conv_ln_b, conv_w_pw, att_rel_bias, att_w_o, mem_w_kv, mem_w_o, w_out, ffn2_norm, ffn2_w_up, ffn2_w_down, final_norm, loss_target, m_ffn1_norm, m_ffn1_w_up, m_ffn1_w_down, m_mix_norm, m_mem_norm, m_w_in, m_b_gate, m_conv_dw_w, m_conv_dw_b, m_conv_ln_g, m_conv_ln_b, m_conv_w_pw, m_att_rel_bias, m_att_w_o, m_mem_w_kv, m_mem_w_o, m_w_out, m_ffn2_norm, m_ffn2_w_up, m_ffn2_w_down, m_final_norm, v_ffn1_norm, v_ffn1_w_up, v_ffn1_w_down, v_mix_norm, v_mem_norm, v_w_in, v_b_gate, v_conv_dw_w, v_conv_dw_b, v_conv_ln_g, v_conv_ln_b, v_conv_w_pw, v_att_rel_bias, v_att_w_o, v_mem_w_kv, v_mem_w_o, v_w_out, v_ffn2_norm, v_ffn2_w_up, v_ffn2_w_down, v_final_norm):
    given = dict(locals())
    wts = {n: given[n] for n in WEIGHTS}
    mom1 = {n: given["m_" + n] for n in WEIGHTS}
    mom2 = {n: given["v_" + n] for n in WEIGHTS}
    Bl, S, Dm = x.shape
    T = Bl * S
    x0 = x.reshape(T, Dm)
    tgt = loss_target.reshape(T, Dm)
    mem2 = mem.reshape(Bl * MEM_LEN, Dm)

    big_names = [n for n, _ in BIG]
    big_axes = [a for _, a in BIG]
    chip = 2 * lax.axis_index("x") + lax.axis_index("y")

    core = lax.axis_index("c")
    axis_of = dict(BIG)

    gather_groups = [['ffn1_w_up'], ['ffn1_w_down'], ['w_in', 'conv_dw_w'],
                     ['mem_w_kv', 'conv_w_pw', 'att_w_o', 'mem_w_o', 'w_out'], ['ffn2_w_up'], ['ffn2_w_down']]
    gather_names = [n for grp in gather_groups for n in grp]
    gather_axes = [axis_of.get(n, 1) for n in gather_names]
    shards = [jnp.pad(conv_dw_w[0], ((0, 1), (0, 0))) if n == 'conv_dw_w' else wts[n][0].astype(BF16)
              for n in gather_names]
    chip_idx = chip.reshape(1).astype(jnp.int32)
    first, first_token = _gather_start(
        shards[:1], [_place_block(shards[0], gather_axes[0], chip_idx, f"place_{gather_names[0]}")],
        gather_axes[:1], "gather_start_first")
    lands = [_place_block(sh, a, chip_idx, f"place_{n}", after=first_token)
             for sh, a, n in zip(shards[1:], gather_axes[1:], gather_names[1:])]
    rest, gather_token = _gather_start(shards[1:], lands, gather_axes[1:], "gather_start_rest")
    in_flight = dict(zip(gather_names, first + rest))
    full = {}

    def arrived(name, after):
        if name not in full:
            grp = next(grp for grp in gather_groups if name in grp)
            lands = _gather_wait([in_flight[n] for n in grp], [axis_of.get(n, 1) for n in grp], after,
                                 f"gather_wait_{grp[0]}")
            full.update(zip(grp, lands))
        return full[name]

    scattering = {}

    def send(names, grads):
        pairs, token = _scatter_start(grads, [axis_of[n] for n in names], f"scatter_start_{names[0]}")
        scattering.update(zip(names, pairs))
        return token

    final_g = final_norm.reshape(1, Dm)
    bias = _att_bias(att_rel_bias[0] + first_token[:1, :1])

    x1, h, ffn1_saved = _ffn_fwd(x0, _rms_fwd(x0, ffn1_norm, "ffn1_norm", after=gather_token), arrived, "ffn1",
                                 next_g=mix_norm)
    w_in_full = arrived('w_in', h)
    dw_full = full['conv_dw_w'][:CONV_KERNEL]
    proj = _mm(h, w_in_full, 'nn', "w_in", BF16)
    proj3 = proj.reshape(Bl, S, proj.shape[1])
    cv, c_act = _conv_fwd(proj3, dw_full, conv_dw_b, conv_ln_g, conv_ln_b)
    o_att = _att_fwd(proj3, bias)
    mem_h = _rms_fwd(mem2, mem_norm, "mem_norm")
    kv = _mm(mem_h, arrived('mem_w_kv', o_att), 'nn', "mem_kv", BF16)
    kv3 = kv.reshape(Bl, MEM_LEN, 2 * MEM_WIDTH)
    o_mem = _mem_fwd(proj3, kv3)
    c_act2, o_att2, o_mem2 = c_act.reshape(T, -1), o_att.reshape(T, -1), o_mem.reshape(T, -1)
    x2, yc, ya, ym, h2 = _mix_fwd(c_act2, o_att2, o_mem2, proj, b_gate, x1, full['conv_w_pw'], full['att_w_o'],
                                  full['mem_w_o'], full['w_out'], ffn2_norm)
    x3, _, ffn2_saved = _ffn_fwd(x2, h2, arrived, "ffn2")
    dx3, g_final, loss_vec = _final_fwd_bwd(x3, tgt, final_g)

    g = {}
    dx2, g['ffn2_norm'] = _ffn_bwd(dx3, x2, ffn2_norm, ffn2_saved, "ffn2", send)
    dgl, g['b_gate'], dc, doa, dom, g_pw, g_o, g_mo, g_out = _mix_bwd(
        dx2, yc, ya, ym, c_act2, o_att2, o_mem2, proj, b_gate, full['conv_w_pw'], full['att_w_o'],
        full['mem_w_o'], full['w_out'])
    token = send(['w_out', 'conv_w_pw', 'att_w_o', 'mem_w_o'], [g_out, g_pw, g_o, g_mo])
    dproj3 = dgl.reshape(Bl, S, -1)
    dproj3, g_dw, g['conv_dw_b'], g['conv_ln_g'], g['conv_ln_b'] = _conv_bwd(
        proj3, cv, dc.reshape(Bl, S, -1), dw_full, conv_ln_g, conv_ln_b, dproj3)
    dproj3, dk, dv, dbias = _att_bwd(proj3, doa.reshape(Bl, S, -1), bias, dproj3)
    g['att_rel_bias'] = _rel_bias_grad(dbias)
    dproj3, dkv = _mem_bwd(proj3, kv3, dom.reshape(Bl, S, -1), dproj3)
    dkv2 = dkv.reshape(Bl * MEM_LEN, 2 * MEM_WIDTH)
    g_kv = _mm(mem_h, dkv2, 'tn', "mem_kv_dw", BF16, after=token)
    dmem_h = _mm(dkv2, full['mem_w_kv'], 'nt', "mem_kv_dx", F32)
    _, g['mem_norm'] = _rms_bwd(mem2, mem_norm, dmem_h, dmem_h, "mem_norm_bwd")
    dkdv = jnp.concatenate([dk.reshape(T, -1), dv.reshape(T, -1)], axis=1)
    dproj = lax.dynamic_update_slice(dproj3.reshape(T, -1), dkdv, (0, 2 * CONV_WIDTH + ATT_WIDTH))
    token = send(['mem_w_kv', 'w_in'], [g_kv, _mm(h, dproj, 'tn', "w_in_dw", BF16)])
    dx1, g['mix_norm'] = _dx_rms_bwd([dproj], w_in_full, x1, mix_norm, dx2, "w_in_dx", after=token)
    dx0, g['ffn1_norm'] = _ffn_bwd(dx1, x0, ffn1_norm, ffn1_saved, "ffn1", send)
    g['final_norm'] = g_final

    sent, landed = _scatter_wait([scattering[n] for n in big_names], big_axes, dx0)
    halves = []
    half_idx = core.reshape(1).astype(jnp.int32)
    for n, a, own_full, parts in zip(big_names, big_axes, sent, landed):
        R, Cn = _shard_shape(own_full, a)
        start = (core * (R // 2), chip * Cn) if a == 1 else (chip * R + core * (R // 2), 0)
        own = lax.dynamic_slice(own_full, start, (R // 2, Cn))
        halves.append(_sum_partials(own, parts, half_idx, f"sum_{n}"))
    for n, sg in zip(big_names, _exchange_halves(halves)):
        g[n] = sg

    small_shapes = [wts[n].shape for n in SMALL]
    n_small = sum(int(wts[n].size) for n in SMALL)
    n_red = n_small + CONV_KERNEL * CONV_WIDTH + 1
    red = _all_reduce_small(_pack([g[n] for n in SMALL] + [g_dw, loss_vec[0, :1]], -(-n_red // 1024) * 8))
    red_list = _unpack(red, small_shapes + [(CONV_KERNEL, CONV_WIDTH), ()])
    for n, rg in zip(SMALL, red_list[:-2]):
        g[n] = rg
    loss = red_list[-1]
    dw_cols = conv_dw_w.shape[2]
    g['conv_dw_w'] = lax.dynamic_slice(red_list[-2], (0, chip * dw_cols), (CONV_KERNEL, dw_cols))[None]

    delta, new_m, new_v = {}, {}, {}
    for n in big_names:
        g[n] = g[n][None]
        d, nm, nv = _adamw(wts[n][0], g[n][0], mom1[n][0], mom2[n][0], f"adamw_{n}")
        delta[n], new_m[n], new_v[n] = d[None], nm[None], nv[None]
    rest = SMALL + ['conv_dw_w']
    rest_shapes = [wts[n].shape for n in rest]
    rows = -(-sum(int(wts[n].size) for n in rest) // 1024) * 8
    packed = [_pack([src[n] for n in rest], rows) for src in (wts, g, mom1, mom2)]
    for out, res in zip((delta, new_m, new_v), _adamw(*packed, "adamw_small")):
        for n, a in zip(rest, _unpack(res, rest_shapes)):
            out[n] = a

    grad_x = dx0.reshape(Bl, S, Dm)
    return (loss, grad_x, *[g[n] for n in WEIGHTS], *[delta[n] for n in WEIGHTS],
            *[new_m[n] for n in WEIGHTS], *[new_v[n] for n in WEIGHTS])
```

```python
import jax
import jax.numpy as jnp
from jax import lax
from jax.experimental import pallas as pl
from jax.experimental.pallas import tpu as pltpu

F32 = jnp.float32
BF16 = jnp.bfloat16

CHUNK = 64
LEFT_CHUNKS = 8
MAX_REL = 128
N_REL = (CHUNK - 1) + MAX_REL + 1
CONV_WIDTH = 512
CONV_KERNEL = 31
ATT_HEADS = 8
ATT_WIDTH = 512
MEM_LEN = 256
MEM_HEADS = 4
MEM_HEAD_DIM = 128
MEM_WIDTH = 512
EPS = 1e-6
MASK_VALUE = -1e30
ATT_SCALE = 64 ** -0.5
MEM_SCALE = 128 ** -0.5

ADAM_LR = 0.001
ADAM_B1 = 0.9
ADAM_B2 = 0.999
ADAM_EPS = 1e-08
ADAM_WD = 0.01
ADAM_STEP = 10

QB = 256
KW = 3 * QB
CONV_PAD = 32
CONV_TILE = 256

VMEM_LIMIT = 56 << 20
MXU_COLS = 256

WEIGHTS = ['ffn1_norm', 'ffn1_w_up', 'ffn1_w_down', 'mix_norm', 'mem_norm', 'w_in', 'b_gate', 'conv_dw_w',
           'conv_dw_b', 'conv_ln_g', 'conv_ln_b', 'conv_w_pw', 'att_rel_bias', 'att_w_o', 'mem_w_kv', 'mem_w_o',
           'w_out', 'ffn2_norm', 'ffn2_w_up', 'ffn2_w_down', 'final_norm']
BIG = [('ffn1_w_up', 1), ('ffn1_w_down', 0), ('w_in', 1), ('conv_w_pw', 1), ('att_w_o', 1), ('mem_w_kv', 0),
       ('mem_w_o', 1), ('w_out', 0), ('ffn2_w_up', 1), ('ffn2_w_down', 0)]
SMALL = ['ffn1_norm', 'mix_norm', 'mem_norm', 'b_gate', 'conv_dw_b', 'conv_ln_g', 'conv_ln_b', 'att_rel_bias',
         'ffn2_norm', 'final_norm']
N_DEV = 8
MESH = pl.DeviceIdType.MESH


def _pick(n, cands):
    for c in cands:
        if n % c == 0:
            return c
    return n


def _sig(x):
    return 0.5 * jnp.tanh(0.5 * x) + 0.5


def _params(sem=None, vmem=VMEM_LIMIT):
    return pltpu.CompilerParams(dimension_semantics=sem, vmem_limit_bytes=vmem)


def _dot(a, b, mode='nn'):
    dims = {'nn': (((1,), (0,)), ((), ())), 'nt': (((1,), (1,)), ((), ())), 'tn': (((0,), (0,)), ((), ()))}[mode]
    return lax.dot_general(a.astype(BF16), b.astype(BF16), dims, preferred_element_type=F32)


def _mm(a, b, mode, name, out_dtype, res=None, scale=1.0, after=None):
    if mode == 'nn':
        (M, C), (_, N) = a.shape, b.shape
    elif mode == 'nt':
        (M, C), (N, _) = a.shape, b.shape
    else:
        (C, M), (_, N) = a.shape, b.shape
    tm = _pick(M, (1024, 1408, 512, 256, 128))
    tn = _pick(N, (1024, 1408, 512, 256, 128))
    tc = C if C <= 2816 else _pick(C, (2048, 1024, 1408, 512, 256, 128))
    nk = C // tc
    if mode == 'nn':
        a_spec = pl.BlockSpec((tm, tc), lambda i, j, k: (i, k))
        b_spec = pl.BlockSpec((tc, tn), lambda i, j, k: (k, j))
    elif mode == 'nt':
        a_spec = pl.BlockSpec((tm, tc), lambda i, j, k: (i, k))
        b_spec = pl.BlockSpec((tn, tc), lambda i, j, k: (j, k))
    else:
        a_spec = pl.BlockSpec((tc, tm), lambda i, j, k: (k, i))
        b_spec = pl.BlockSpec((tc, tn), lambda i, j, k: (k, j))
    o_spec = pl.BlockSpec((tm, tn), lambda i, j, k: (i, j))
    has_res = res is not None
    has_after = after is not None

    def body(*refs):
        a_ref, b_ref = refs[:2]
        r_ref = refs[2] if has_res else None
        o_ref, acc_ref = refs[-2:]
        k = pl.program_id(2)

        def finish(acc):
            if scale != 1.0:
                acc = acc * scale
            if r_ref is not None:
                acc = r_ref[...] + acc
            o_ref[...] = acc.astype(o_ref.dtype)

        if nk == 1:
            finish(_dot(a_ref[...], b_ref[...], mode))
        else:
            @pl.when(k == 0)
            def _():
                acc_ref[...] = jnp.zeros_like(acc_ref)

            acc_ref[...] += _dot(a_ref[...], b_ref[...], mode)

            @pl.when(k == nk - 1)
            def _():
                finish(acc_ref[...])

    in_specs = [a_spec, b_spec] + ([o_spec] if has_res else []) + ([ANY] if has_after else [])
    args = (a, b) + ((res,) if has_res else ()) + ((after,) if has_after else ())
    acc_shape = (tm, tn) if nk > 1 else (8, 128)
    return pl.pallas_call(
        body, name=name, grid=(M // tm, N // tn, nk), in_specs=in_specs, out_specs=o_spec,
        out_shape=jax.ShapeDtypeStruct((M, N), out_dtype), scratch_shapes=[pltpu.VMEM(acc_shape, F32)],
        compiler_params=_params(("parallel", "parallel", "arbitrary")))(*args)


def _row_tile(T):
    return _pick(T, (512, 256, 128, 64, 32, 16, 8))


def _rms_fwd(x, g, name, after=None):
    T, Dm = x.shape
    tm = _row_tile(T)

    def body(x_ref, g_ref, *rest):
        o_ref = rest[-1]
        xv = x_ref[...]
        r = lax.rsqrt(jnp.mean(xv * xv, axis=-1, keepdims=True) + EPS)
        o_ref[...] = ((xv * r) * g_ref[...]).astype(o_ref.dtype)

    extra = () if after is None else (after,)
    return pl.pallas_call(
        body, name=name, grid=(T // tm,),
        in_specs=[pl.BlockSpec((tm, Dm), lambda i: (i, 0)), pl.BlockSpec((1, Dm), lambda i: (0, 0))]
        + [ANY] * len(extra),
        out_specs=pl.BlockSpec((tm, Dm), lambda i: (i, 0)), out_shape=jax.ShapeDtypeStruct((T, Dm), BF16),
        compiler_params=_params(("parallel",)))(x, g, *extra)


def _rms_bwd(x, g, dh, dres, name):
    T, Dm = x.shape
    tm = _row_tile(T)

    def body(x_ref, g_ref, dh_ref, dr_ref, dx_ref, dg_ref):
        i = pl.program_id(0)
        xv = x_ref[...]
        r = lax.rsqrt(jnp.mean(xv * xv, axis=-1, keepdims=True) + EPS)
        xr = xv * r
        dh_v = dh_ref[...].astype(F32)
        dyg = dh_v * g_ref[...]
        dx = r * (dyg - xr * jnp.mean(dyg * xr, axis=-1, keepdims=True))
        dx_ref[...] = dr_ref[...] + dx

        @pl.when(i == 0)
        def _():
            dg_ref[...] = jnp.zeros_like(dg_ref)

        dg_ref[...] += jnp.sum(dh_v * xr, axis=0, keepdims=True)

    row = pl.BlockSpec((tm, Dm), lambda i: (i, 0))
    vec = pl.BlockSpec((1, Dm), lambda i: (0, 0))
    return pl.pallas_call(
        body, name=name, grid=(T // tm,), in_specs=[row, vec, row, row], out_specs=[row, vec],
        out_shape=[jax.ShapeDtypeStruct((T, Dm), F32), jax.ShapeDtypeStruct((1, Dm), F32)],
        compiler_params=_params(("arbitrary",)))(x, g, dh, dres)


def _final_fwd_bwd(x3, tgt, g):
    T, Dm = x3.shape
    tm = _row_tile(T)

    def body(x_ref, t_ref, g_ref, dx_ref, dg_ref, loss_ref):
        i = pl.program_id(0)
        xv = x_ref[...]
        gg = g_ref[...]
        r = lax.rsqrt(jnp.mean(xv * xv, axis=-1, keepdims=True) + EPS)
        xr = xv * r
        err = xr * gg - t_ref[...]
        dout = err * (1.0 / Dm)
        dyg = dout * gg
        dx_ref[...] = r * (dyg - xr * jnp.mean(dyg * xr, axis=-1, keepdims=True))

        @pl.when(i == 0)
        def _():
            dg_ref[...] = jnp.zeros_like(dg_ref)
            loss_ref[...] = jnp.zeros_like(loss_ref)

        dg_ref[...] += jnp.sum(dout * xr, axis=0, keepdims=True)
        loss_ref[...] += jnp.zeros_like(loss_ref) + (0.5 / Dm) * jnp.sum(err * err)

    row = pl.BlockSpec((tm, Dm), lambda i: (i, 0))
    vec = pl.BlockSpec((1, Dm), lambda i: (0, 0))
    one = pl.BlockSpec((1, 128), lambda i: (0, 0))
    return pl.pallas_call(
        body, name="final_fwd_bwd", grid=(T // tm,), in_specs=[row, row, vec], out_specs=[row, vec, one],
        out_shape=[jax.ShapeDtypeStruct((T, Dm), F32), jax.ShapeDtypeStruct((1, Dm), F32),
                   jax.ShapeDtypeStruct((1, 128), F32)],
        compiler_params=_params(("arbitrary",)))(x3, tgt, g)


def _ffn_up(h, w_up, name):
    T, K = h.shape
    Fh = w_up.shape[1] // 2
    tm = _pick(T, (1024, 512, 256, 128))
    tn = _pick(Fh, (1408, 512, 256, 128))
    nj = Fh // tn

    def body(h_ref, wa_ref, wb_ref, a_ref, b_ref, act_ref):
        hv = h_ref[...]
        for c0 in range(0, tn, MXU_COLS):
            cs = slice(c0, min(c0 + MXU_COLS, tn))
            a = _dot(hv, wa_ref[:, cs])
            b = _dot(hv, wb_ref[:, cs])
            a_ref[:, cs] = a.astype(BF16)
            b_ref[:, cs] = b.astype(BF16)
            act_ref[:, cs] = (a * _sig(a) * b).astype(BF16)

    out = pl.BlockSpec((tm, tn), lambda i, j: (i, j))
    return pl.pallas_call(
        body, name=name, grid=(T // tm, nj),
        in_specs=[pl.BlockSpec((tm, K), lambda i, j: (i, 0)), pl.BlockSpec((K, tn), lambda i, j: (0, j)),
                  pl.BlockSpec((K, tn), lambda i, j: (0, j + nj))],
        out_specs=[out, out, out], out_shape=[jax.ShapeDtypeStruct((T, Fh), BF16)] * 3,
        compiler_params=_params(("parallel", "parallel")))(h, w_up, w_up)


def _ffn_down(act, w_down, x, name, next_g=None):
    T, Fh = act.shape
    Dm = w_down.shape[1]
    tm = _pick(T, (1024, 512, 256, 128))
    rows = _pick(tm, (256, 128))
    with_norm = next_g is not None

    def body(a_ref, w_ref, x_ref, *rest):
        o_ref = rest[-2] if with_norm else rest[-1]
        o_ref[...] = x_ref[...] + 0.5 * _dot(a_ref[...], w_ref[...])
        if with_norm:
            g_ref, h_ref = rest[0], rest[-1]

            def chunk(c, carry):
                rs = pl.ds(pl.multiple_of(c * rows, rows), rows)
                ov = o_ref[rs, :]
                r = lax.rsqrt(jnp.mean(ov * ov, axis=-1, keepdims=True) + EPS)
                h_ref[rs, :] = ((ov * r) * g_ref[...]).astype(BF16)
                return carry

            lax.fori_loop(0, tm // rows, chunk, 0)

    row = pl.BlockSpec((tm, Dm), lambda i: (i, 0))
    in_specs = [pl.BlockSpec((tm, Fh), lambda i: (i, 0)), pl.BlockSpec((Fh, Dm), lambda i: (0, 0)), row]
    out_specs, out_shape = [row], [jax.ShapeDtypeStruct((T, Dm), F32)]
    args = (act, w_down, x)
    if with_norm:
        in_specs.append(pl.BlockSpec((1, Dm), lambda i: (0, 0)))
        out_specs.append(row)
        out_shape.append(jax.ShapeDtypeStruct((T, Dm), BF16))
        args += (next_g,)
    res = pl.pallas_call(body, name=name, grid=(T // tm,), in_specs=in_specs, out_specs=out_specs,
                         out_shape=out_shape, compiler_params=_params(("parallel",)))(*args)
    return (res[0], res[1]) if with_norm else (res[0], None)


def _ffn_down_bwd(dout, w_down, a, b, name, after=None):
    T, Dm = dout.shape
    Fh = w_down.shape[0]
    tm = _pick(T, (1024, 512, 256, 128))
    tn = _pick(Fh, (1408, 512, 256, 128))

    def body(d_ref, w_ref, a_ref, b_ref, *rest):
        da_ref, db_ref = rest[-2:]
        dv = (d_ref[...] * 0.5).astype(BF16)
        for c0 in range(0, tn, MXU_COLS):
            cs = slice(c0, min(c0 + MXU_COLS, tn))
            dact = _dot(dv, w_ref[cs, :], 'nt')
            av = a_ref[:, cs].astype(F32)
            bv = b_ref[:, cs].astype(F32)
            s = _sig(av)
            da_ref[:, cs] = (dact * bv * s * (1.0 + av * (1.0 - s))).astype(BF16)
            db_ref[:, cs] = (dact * av * s).astype(BF16)

    tile = pl.BlockSpec((tm, tn), lambda i, j: (i, j))
    extra = () if after is None else (after,)
    return pl.pallas_call(
        body, name=name, grid=(T // tm, Fh // tn),
        in_specs=[pl.BlockSpec((tm, Dm), lambda i, j: (i, 0)), pl.BlockSpec((tn, Dm), lambda i, j: (j, 0)),
                  tile, tile] + [ANY] * len(extra),
        out_specs=[tile, tile], out_shape=[jax.ShapeDtypeStruct((T, Fh), BF16)] * 2,
        compiler_params=_params(("parallel", "parallel")))(dout, w_down, a, b, *extra)


def _dx_rms_bwd(pieces, w, x, g, dres, name, after=None):
    T, Dm = x.shape
    width = pieces[0].shape[1]
    tm = _pick(T, (1024, 512, 256, 128))
    tc = _pick(width, (1408, 2048, 1024, 512, 256, 128))
    per = width // tc
    nk = per * len(pieces)
    npc = len(pieces)
    rows = _pick(tm, (256, 128))

    def body(*refs):
        p_refs = refs[:npc]
        w_ref, x_hbm, g_ref, dr_hbm = refs[npc:npc + 4]
        dx_ref, dg_ref, acc_ref, x_buf, dr_buf, sems = refs[-6:]
        i = pl.program_id(0)
        k = pl.program_id(1)
        tile = pl.ds(pl.multiple_of(i * tm, tm), tm)
        fetch_x = pltpu.make_async_copy(x_hbm.at[tile, :], x_buf, sems.at[0])
        fetch_dr = pltpu.make_async_copy(dr_hbm.at[tile, :], dr_buf, sems.at[1])

        @pl.when(k == 0)
        def _():
            fetch_x.start()
            fetch_dr.start()
            acc_ref[...] = jnp.zeros_like(acc_ref)

        @pl.when((i == 0) & (k == 0))
        def _():
            dg_ref[...] = jnp.zeros_like(dg_ref)

        for p in range(npc):
            @pl.when((k >= p * per) & (k < (p + 1) * per))
            def _(p=p):
                acc_ref[...] += _dot(p_refs[p][...], w_ref[...], 'nt')

        @pl.when(k == nk - 1)
        def _():
            fetch_x.wait()
            fetch_dr.wait()

            def chunk(c, carry):
                rs = pl.ds(pl.multiple_of(c * rows, rows), rows)
                dh = acc_ref[rs, :]
                xv = x_buf[rs, :]
                r = lax.rsqrt(jnp.mean(xv * xv, axis=-1, keepdims=True) + EPS)
                xr = xv * r
                dyg = dh * g_ref[...]
                dx_ref[rs, :] = dr_buf[rs, :] + r * (dyg - xr * jnp.mean(dyg * xr, axis=-1, keepdims=True))
                dg_ref[...] += jnp.sum(dh * xr, axis=0, keepdims=True)
                return carry

            lax.fori_loop(0, tm // rows, chunk, 0)

    def piece_spec(p):
        return pl.BlockSpec((tm, tc), lambda i, k: (i, jnp.clip(k - p * per, 0, per - 1)))

    row = pl.BlockSpec((tm, Dm), lambda i, k: (i, 0))
    vec = pl.BlockSpec((1, Dm), lambda i, k: (0, 0))
    extra = () if after is None else (after,)
    return pl.pallas_call(
        body, name=name, grid=(T // tm, nk),
        in_specs=[piece_spec(p) for p in range(npc)] + [pl.BlockSpec((Dm, tc), lambda i, k: (0, k)), ANY, vec, ANY]
        + [ANY] * len(extra),
        out_specs=[row, vec], out_shape=[jax.ShapeDtypeStruct((T, Dm), F32), jax.ShapeDtypeStruct((1, Dm), F32)],
        scratch_shapes=[pltpu.VMEM((tm, Dm), F32), pltpu.VMEM((tm, Dm), F32), pltpu.VMEM((tm, Dm), F32),
                        pltpu.SemaphoreType.DMA((2,))],
        compiler_params=_params(("arbitrary", "arbitrary")))(*pieces, w, x, g, dres, *extra)


def _dw_pieces(a, pieces, name):
    C, M = a.shape
    width = pieces[0].shape[1]
    npc = len(pieces)
    tm = _pick(M, (1024, 512, 256, 128))
    tn = _pick(width, (1408, 1024, 512, 256, 128))
    tc = _pick(C, (2048, 1024, 512, 256, 128))
    per = width // tn
    nk = C // tc

    def body(*refs):
        a_ref = refs[0]
        p_refs = refs[1:1 + npc]
        o_ref, acc_ref = refs[-2:]
        j = pl.program_id(1)
        k = pl.program_id(2)

        @pl.when(k == 0)
        def _():
            acc_ref[...] = jnp.zeros_like(acc_ref)

        for p in range(npc):
            @pl.when((j >= p * per) & (j < (p + 1) * per))
            def _(p=p):
                acc_ref[...] += _dot(a_ref[...], p_refs[p][...], 'tn')

        @pl.when(k == nk - 1)
        def _():
            o_ref[...] = acc_ref[...].astype(o_ref.dtype)

    def piece_spec(p):
        return pl.BlockSpec((tc, tn), lambda i, j, k: (k, jnp.clip(j - p * per, 0, per - 1)))

    return pl.pallas_call(
        body, name=name, grid=(M // tm, per * npc, nk),
        in_specs=[pl.BlockSpec((tc, tm), lambda i, j, k: (k, i))] + [piece_spec(p) for p in range(npc)],
        out_specs=pl.BlockSpec((tm, tn), lambda i, j, k: (i, j)),
        out_shape=jax.ShapeDtypeStruct((M, width * npc), BF16), scratch_shapes=[pltpu.VMEM((tm, tn), F32)],
        compiler_params=_params(("parallel", "parallel", "arbitrary")))(a, *pieces)


def _mix_fwd(c_act, o_att, o_mem, proj, b_gate, x1, w_pw, w_o, w_mo, w_out, next_g):
    T, Dm = x1.shape
    W = c_act.shape[1]
    tm = _pick(T, (256, 128, 64, 32, 16, 8))

    def body(c_ref, oa_ref, om_ref, gl_ref, bg_ref, x1_ref, wpw_ref, wo_ref, wmo_ref, wout_ref, ng_ref,
             x2_ref, yc_ref, ya_ref, ym_ref, h_ref):
        yc = _dot(c_ref[...], wpw_ref[...])
        ya = _dot(oa_ref[...], wo_ref[...])
        ym = _dot(om_ref[...], wmo_ref[...])
        g = _sig(gl_ref[...].astype(F32) + bg_ref[...])
        y = g[:, :Dm] * yc + g[:, Dm:2 * Dm] * ya + g[:, 2 * Dm:] * ym
        x2 = x1_ref[...] + _dot(y, wout_ref[...])
        x2_ref[...] = x2
        r = lax.rsqrt(jnp.mean(x2 * x2, axis=-1, keepdims=True) + EPS)
        h_ref[...] = ((x2 * r) * ng_ref[...]).astype(BF16)
        yc_ref[...] = yc.astype(BF16)
        ya_ref[...] = ya.astype(BF16)
        ym_ref[...] = ym.astype(BF16)

    rowW = pl.BlockSpec((tm, W), lambda i: (i, 0))
    rowD = pl.BlockSpec((tm, Dm), lambda i: (i, 0))
    full = lambda s: pl.BlockSpec(s, lambda i: (0, 0))
    return pl.pallas_call(
        body, name="mix_fwd", grid=(T // tm,),
        in_specs=[rowW, rowW, rowW, pl.BlockSpec((tm, 3 * Dm), lambda i: (i, 1)), full((1, 3 * Dm)), rowD,
                  full((W, Dm)), full((W, Dm)), full((W, Dm)), full((Dm, Dm)), full((1, Dm))],
        out_specs=[rowD] * 5,
        out_shape=[jax.ShapeDtypeStruct((T, Dm), F32)] + [jax.ShapeDtypeStruct((T, Dm), BF16)] * 4,
        compiler_params=_params(("parallel",)))(c_act, o_att, o_mem, proj, b_gate, x1, w_pw, w_o, w_mo, w_out,
                                                next_g)


def _mix_bwd(dx2, yc, ya, ym, c_act, o_att, o_mem, proj, b_gate, w_pw, w_o, w_mo, w_out):
    T, Dm = dx2.shape
    W = w_pw.shape[0]
    tm = _pick(T, (256, 128, 64, 32, 16, 8))
    nt = T // tm

    def body(dx_ref, yc_ref, ya_ref, ym_ref, c_ref, oa_ref, om_ref, gl_ref, bg_ref, wpw_ref, wo_ref, wmo_ref,
             wout_ref, dgl_ref, dbg_ref, dc_ref, doa_ref, dom_ref, gpw_ref, go_ref, gmo_ref, gout_ref,
             apw, ao, amo, aout):
        i = pl.program_id(0)

        @pl.when(i == 0)
        def _():
            dbg_ref[...] = jnp.zeros_like(dbg_ref)
            for acc in (apw, ao, amo, aout):
                acc[...] = jnp.zeros_like(acc)

        dxv = dx_ref[...].astype(BF16)
        dy = _dot(dxv, wout_ref[...], 'nt')
        g = _sig(gl_ref[...].astype(F32) + bg_ref[...])
        branches = ((yc_ref, c_ref, wpw_ref, dc_ref, apw), (ya_ref, oa_ref, wo_ref, doa_ref, ao),
                    (ym_ref, om_ref, wmo_ref, dom_ref, amo))
        y = jnp.zeros((tm, Dm), F32)
        for n, (y_ref, in_ref, w_ref, dk_ref, acc) in enumerate(branches):
            gk = g[:, n * Dm:(n + 1) * Dm]
            yk = y_ref[...].astype(F32)
            dyk = dy * gk
            dgl = dyk * yk * (1.0 - gk)
            dgl_ref[:, n * Dm:(n + 1) * Dm] = dgl.astype(BF16)
            dbg_ref[:, n * Dm:(n + 1) * Dm] += jnp.sum(dgl, axis=0, keepdims=True)
            dyk = dyk.astype(BF16)
            dk_ref[...] = _dot(dyk, w_ref[...], 'nt').astype(BF16)
            acc[...] += _dot(in_ref[...], dyk, 'tn')
            y = y + gk * yk
        aout[...] += _dot(y, dxv, 'tn')

        @pl.when(i == nt - 1)
        def _():
            for acc, out in ((apw, gpw_ref), (ao, go_ref), (amo, gmo_ref), (aout, gout_ref)):
                out[...] = acc[...].astype(BF16)

    rowW = pl.BlockSpec((tm, W), lambda i: (i, 0))
    rowD = pl.BlockSpec((tm, Dm), lambda i: (i, 0))
    full = lambda s: pl.BlockSpec(s, lambda i: (0, 0))
    return pl.pallas_call(
        body, name="mix_bwd", grid=(nt,),
        in_specs=[rowD, rowD, rowD, rowD, rowW, rowW, rowW, pl.BlockSpec((tm, 3 * Dm), lambda i: (i, 1)),
                  full((1, 3 * Dm)), full((W, Dm)), full((W, Dm)), full((W, Dm)), full((Dm, Dm))],
        out_specs=[pl.BlockSpec((tm, 3 * Dm), lambda i: (i, 1)), full((1, 3 * Dm)), rowW, rowW, rowW,
                   full((W, Dm)), full((W, Dm)), full((W, Dm)), full((Dm, Dm))],
        out_shape=[jax.ShapeDtypeStruct((T, 6 * Dm), BF16), jax.ShapeDtypeStruct((1, 3 * Dm), F32)]
        + [jax.ShapeDtypeStruct((T, W), BF16)] * 3 + [jax.ShapeDtypeStruct((W, Dm), BF16)] * 3
        + [jax.ShapeDtypeStruct((Dm, Dm), BF16)],
        scratch_shapes=[pltpu.VMEM((W, Dm), F32)] * 3 + [pltpu.VMEM((Dm, Dm), F32)],
        compiler_params=_params(("arbitrary",)))(dx2, yc, ya, ym, c_act, o_att, o_mem, proj, b_gate, w_pw, w_o,
                                                 w_mo, w_out)


def _ln_swish(cv, lg, lb):
    mu = jnp.mean(cv, axis=-1, keepdims=True)
    xc = cv - mu
    r = lax.rsqrt(jnp.mean(xc * xc, axis=-1, keepdims=True) + EPS)
    n = xc * r
    l = n * lg + lb
    return r, n, l


def _shift_copies(src, r0, win, shifts):
    win[...] = src[pl.ds(r0, CONV_TILE + CONV_PAD + 8), :]
    for s in range(8):
        shifts[s] = win[s:s + CONV_TILE + CONV_PAD, :]


def _tap(shifts, d):
    return shifts[d % 8, d - d % 8:d - d % 8 + CONV_TILE, :]


def _conv_fwd(proj3, dw_w, dw_b, ln_g, ln_b):
    Bl, S, _ = proj3.shape
    C, K, TS, PAD = CONV_WIDTH, CONV_KERNEL, CONV_TILE, CONV_PAD
    nt = S // TS

    def body(u_ref, w_ref, b_ref, lg_ref, lb_ref, cv_ref, c_ref, vbuf, win, shifts):
        vbuf[0:PAD, :] = jnp.zeros((PAD, C), F32)
        vbuf[S + PAD:S + PAD + 8, :] = jnp.zeros((8, C), F32)

        def glu(t, carry):
            r0 = pl.multiple_of(t * TS, TS)
            u = u_ref[pl.ds(r0, TS), :].astype(F32)
            vbuf[pl.ds(PAD + r0, TS), :] = u[:, :C] * _sig(u[:, C:])
            return carry

        lax.fori_loop(0, nt, glu, 0)

        def conv(t, carry):
            r0 = pl.multiple_of(t * TS, TS)
            _shift_copies(vbuf, r0, win, shifts)
            acc = jnp.zeros((TS, C), F32)
            for j in range(K):
                acc = acc + w_ref[j:j + 1, :] * _tap(shifts, PAD - (K - 1) + j)
            cv = acc + b_ref[...]
            cv_ref[pl.ds(r0, TS), :] = cv
            _, _, l = _ln_swish(cv, lg_ref[...], lb_ref[...])
            c_ref[pl.ds(r0, TS), :] = (l * _sig(l)).astype(BF16)
            return carry

        lax.fori_loop(0, nt, conv, 0)

    vec = pl.BlockSpec((1, C), lambda b: (0, 0))
    return pl.pallas_call(
        body, name="conv_fwd", grid=(Bl,),
        in_specs=[pl.BlockSpec((None, S, 2 * C), lambda b: (b, 0, 0)), pl.BlockSpec((K, C), lambda b: (0, 0)),
                  vec, vec, vec],
        out_specs=[pl.BlockSpec((None, S, C), lambda b: (b, 0, 0))] * 2,
        out_shape=[jax.ShapeDtypeStruct((Bl, S, C), F32), jax.ShapeDtypeStruct((Bl, S, C), BF16)],
        scratch_shapes=[pltpu.VMEM((S + PAD + 8, C), F32), pltpu.VMEM((TS + PAD + 8, C), F32),
                        pltpu.VMEM((8, TS + PAD, C), F32)],
        compiler_params=_params(("parallel",)))(proj3, dw_w, dw_b, ln_g, ln_b)


def _conv_bwd(proj3, cv, dc, dw_w, ln_g, ln_b, dproj3):
    Bl, S, _ = proj3.shape
    C, K, TS, PAD = CONV_WIDTH, CONV_KERNEL, CONV_TILE, CONV_PAD
    nt = S // TS

    def body(u_ref, cv_ref, dc_ref, w_ref, lg_ref, lb_ref, through_ref, du_ref, dw_ref, db_ref, dlg_ref, dlb_ref,
             vbuf, gbuf, win, shifts, dwacc):
        b = pl.program_id(0)

        @pl.when(b == 0)
        def _():
            dw_ref[...] = jnp.zeros_like(dw_ref)
            db_ref[...] = jnp.zeros_like(db_ref)
            dlg_ref[...] = jnp.zeros_like(dlg_ref)
            dlb_ref[...] = jnp.zeros_like(dlb_ref)

        vbuf[0:PAD, :] = jnp.zeros((PAD, C), F32)
        vbuf[S + PAD:S + PAD + 8, :] = jnp.zeros((8, C), F32)
        gbuf[S:S + PAD + 8, :] = jnp.zeros((PAD + 8, C), F32)
        dwacc[...] = jnp.zeros_like(dwacc)

        def norm_bwd(t, carry):
            r0 = pl.multiple_of(t * TS, TS)
            u = u_ref[pl.ds(r0, TS), :].astype(F32)
            vbuf[pl.ds(PAD + r0, TS), :] = u[:, :C] * _sig(u[:, C:])
            r, n, l = _ln_swish(cv_ref[pl.ds(r0, TS), :], lg_ref[...], lb_ref[...])
            s = _sig(l)
            dl = dc_ref[pl.ds(r0, TS), :].astype(F32) * s * (1.0 + l * (1.0 - s))
            dlg_ref[...] += jnp.sum(dl * n, axis=0, keepdims=True)
            dlb_ref[...] += jnp.sum(dl, axis=0, keepdims=True)
            dn = dl * lg_ref[...]
            dcv = r * (dn - jnp.mean(dn, axis=-1, keepdims=True) - n * jnp.mean(dn * n, axis=-1, keepdims=True))
            gbuf[pl.ds(r0, TS), :] = dcv
            db_ref[...] += jnp.sum(dcv, axis=0, keepdims=True)
            return carry

        lax.fori_loop(0, nt, norm_bwd, 0)

        def conv_bwd(t, carry):
            r0 = pl.multiple_of(t * TS, TS)
            _shift_copies(gbuf, r0, win, shifts)
            dv = jnp.zeros((TS, C), F32)
            for j in range(K):
                dv = dv + w_ref[j:j + 1, :] * _tap(shifts, K - 1 - j)
            u = u_ref[pl.ds(r0, TS), :].astype(F32)
            a, g = u[:, :C], u[:, C:]
            s = _sig(g)
            du_ref[pl.ds(r0, TS), 0:C] = (dv * s).astype(BF16)
            du_ref[pl.ds(r0, TS), C:2 * C] = (dv * a * s * (1.0 - s)).astype(BF16)
            dcv = gbuf[pl.ds(r0, TS), :]
            _shift_copies(vbuf, r0, win, shifts)
            for j in range(K):
                prod = dcv * _tap(shifts, PAD - (K - 1) + j)
                dwacc[j] += jnp.sum(prod.reshape(TS // 8, 8, C), axis=0)
            return carry

        lax.fori_loop(0, nt, conv_bwd, 0)
        dw_ref[...] += jnp.sum(dwacc[...], axis=1)

    vec = pl.BlockSpec((1, C), lambda b: (0, 0))
    seq = lambda w: pl.BlockSpec((None, S, w), lambda b: (b, 0, 0))
    return pl.pallas_call(
        body, name="conv_bwd", grid=(Bl,),
        in_specs=[seq(2 * C), seq(C), seq(C), pl.BlockSpec((K, C), lambda b: (0, 0)), vec, vec, ANY],
        out_specs=[seq(2 * C), pl.BlockSpec((K, C), lambda b: (0, 0)), vec, vec, vec],
        out_shape=[jax.ShapeDtypeStruct(dproj3.shape, BF16), jax.ShapeDtypeStruct((K, C), F32)]
        + [jax.ShapeDtypeStruct((1, C), F32)] * 3,
        input_output_aliases={6: 0},
        scratch_shapes=[pltpu.VMEM((S + PAD + 8, C), F32), pltpu.VMEM((S + PAD + 8, C), F32),
                        pltpu.VMEM((TS + PAD + 8, C), F32), pltpu.VMEM((8, TS + PAD, C), F32),
                        pltpu.VMEM((K, 8, C), F32)],
        compiler_params=_params(("arbitrary",)))(proj3, cv, dc, dw_w, ln_g, ln_b, dproj3)


def _att_bias(rel_bias):
    H = rel_bias.shape[0]
    Wd = KW + QB
    c = jnp.arange(Wd)
    by_offset = rel_bias[:, jnp.clip(KW - c, -(CHUNK - 1), MAX_REL) + (CHUNK - 1)].reshape(H, 1, Wd)

    def body(t_ref, o_ref):
        rows = jnp.broadcast_to(t_ref[...], (QB, Wd))
        skew = pltpu.roll(rows, 0, 1, stride=1, stride_axis=0)[:, QB:]
        qi = lax.broadcasted_iota(jnp.int32, (QB, KW), 0)
        kj = lax.broadcasted_iota(jnp.int32, (QB, KW), 1)
        dchunk = ((KW - QB) + qi) // CHUNK - kj // CHUNK
        o_ref[...] = jnp.where((dchunk >= 0) & (dchunk <= LEFT_CHUNKS), skew, MASK_VALUE)

    return pl.pallas_call(
        body, name="att_bias", grid=(H,), in_specs=[pl.BlockSpec((None, 1, Wd), lambda h: (h, 0, 0))],
        out_specs=pl.BlockSpec((None, QB, KW), lambda h: (h, 0, 0)),
        out_shape=jax.ShapeDtypeStruct((H, QB, KW), F32), compiler_params=_params(("parallel",)))(by_offset)


def _head_masks():
    lane = lax.broadcasted_iota(jnp.int32, (1, 128), 1)
    return (lane < 64, lane >= 64)


def _att_probs(qh, k2, bias, valid):
    s = _dot(qh, k2, 'nt') * ATT_SCALE + bias
    s = jnp.where(valid, s, MASK_VALUE)
    e = jnp.exp(s - jnp.max(s, axis=-1, keepdims=True))
    return e * (1.0 / jnp.sum(e, axis=-1, keepdims=True))


def _att_specs(S, q_col):
    nb = S // QB
    q_spec = pl.BlockSpec((None, QB, ATT_WIDTH), lambda b, i: (b, jnp.minimum(i, nb - 1), q_col))

    def kv_spec(col, kb):
        return pl.BlockSpec((None, QB, ATT_WIDTH),
                            lambda b, i: (b, jnp.clip(i - 2 + kb, 0, nb - 1), col))

    return q_spec, [kv_spec(3, kb) for kb in range(3)], [kv_spec(4, kb) for kb in range(3)]


def _att_fwd(proj3, bias):
    Bl, S, _ = proj3.shape
    nb = S // QB
    q_spec, k_specs, v_specs = _att_specs(S, 2)

    def body(q_ref, k0, k1, k2r, v0, v1, v2r, bias_ref, o_ref):
        i = pl.program_id(1)
        masks = _head_masks()
        valid = lax.broadcasted_iota(jnp.int32, (QB, KW), 1) >= (2 - i) * QB
        for pr in range(ATT_HEADS // 2):
            ls = slice(128 * pr, 128 * (pr + 1))
            q2 = q_ref[:, ls]
            k2 = jnp.concatenate([k0[:, ls], k1[:, ls], k2r[:, ls]], axis=0)
            v2 = jnp.concatenate([v0[:, ls], v1[:, ls], v2r[:, ls]], axis=0)
            o2 = jnp.zeros((QB, 128), F32)
            for hh in range(2):
                p = _att_probs(jnp.where(masks[hh], q2, 0), k2, bias_ref[2 * pr + hh], valid)
                o2 = o2 + _dot(p, jnp.where(masks[hh], v2, 0))
            o_ref[:, ls] = o2.astype(BF16)

    return pl.pallas_call(
        body, name="att_fwd", grid=(Bl, nb),
        in_specs=[q_spec] + k_specs + v_specs + [pl.BlockSpec((ATT_HEADS, QB, KW), lambda b, i: (0, 0, 0))],
        out_specs=pl.BlockSpec((None, QB, ATT_WIDTH), lambda b, i: (b, i, 0)),
        out_shape=jax.ShapeDtypeStruct((Bl, S, ATT_WIDTH), BF16),
        compiler_params=_params(("parallel", "arbitrary")))(*([proj3] * 7), bias)


def _att_bwd(proj3, do, bias, dproj3):
    Bl, S, _ = proj3.shape
    nb = S // QB
    q_spec, k_specs, v_specs = _att_specs(S, 2)
    do_spec = pl.BlockSpec((None, QB, ATT_WIDTH), lambda b, i: (b, jnp.minimum(i, nb - 1), 0))
    kv_out = pl.BlockSpec((None, QB, ATT_WIDTH), lambda b, i: (b, jnp.clip(i - 2, 0, nb - 1), 0))
    bias_spec = pl.BlockSpec((ATT_HEADS, QB, KW), lambda b, i: (0, 0, 0))

    def body(q_ref, k0, k1, k2r, v0, v1, v2r, do_ref, bias_ref, through_ref, dq_ref, dk_ref, dv_ref, db_ref,
             dkw, dvw):
        b = pl.program_id(0)
        i = pl.program_id(1)

        @pl.when((b == 0) & (i == 0))
        def _():
            db_ref[...] = jnp.zeros_like(db_ref)

        @pl.when(i == 0)
        def _():
            dkw[...] = jnp.zeros_like(dkw)
            dvw[...] = jnp.zeros_like(dvw)

        @pl.when(i < nb)
        def _():
            masks = _head_masks()
            valid = lax.broadcasted_iota(jnp.int32, (QB, KW), 1) >= (2 - i) * QB
            for pr in range(ATT_HEADS // 2):
                ls = slice(128 * pr, 128 * (pr + 1))
                q2 = q_ref[:, ls]
                do2 = do_ref[:, ls]
                k2 = jnp.concatenate([k0[:, ls], k1[:, ls], k2r[:, ls]], axis=0)
                v2 = jnp.concatenate([v0[:, ls], v1[:, ls], v2r[:, ls]], axis=0)
                dq2 = jnp.zeros((QB, 128), F32)
                dk2 = jnp.zeros((KW, 128), F32)
                dv2 = jnp.zeros((KW, 128), F32)
                for hh in range(2):
                    h = 2 * pr + hh
                    qh = jnp.where(masks[hh], q2, 0)
                    doh = jnp.where(masks[hh], do2, 0)
                    p = _att_probs(qh, k2, bias_ref[h], valid)
                    dp = _dot(doh, v2, 'nt')
                    ds = p * (dp - jnp.sum(p * dp, axis=-1, keepdims=True))
                    db_ref[h] += ds
                    dq2 = dq2 + _dot(ds, jnp.where(masks[hh], k2, 0))
                    dk2 = dk2 + _dot(ds, qh, 'tn')
                    dv2 = dv2 + _dot(p, doh, 'tn')
                dq_ref[:, ls] = (dq2 * ATT_SCALE).astype(BF16)
                dkw[:, ls] += dk2 * ATT_SCALE
                dvw[:, ls] += dv2

        dk_ref[...] = dkw[0:QB, :].astype(BF16)
        dv_ref[...] = dvw[0:QB, :].astype(BF16)
        for buf in (dkw, dvw):
            rest = buf[QB:KW, :]
            buf[0:KW - QB, :] = rest
            buf[KW - QB:KW, :] = jnp.zeros((QB, ATT_WIDTH), F32)

    blk = jax.ShapeDtypeStruct((Bl, S, ATT_WIDTH), BF16)
    return pl.pallas_call(
        body, name="att_bwd", grid=(Bl, nb + 2),
        in_specs=[q_spec] + k_specs + v_specs + [do_spec, bias_spec, ANY],
        out_specs=[q_spec, kv_out, kv_out, bias_spec],
        out_shape=[jax.ShapeDtypeStruct(dproj3.shape, BF16), blk, blk,
                   jax.ShapeDtypeStruct((ATT_HEADS, QB, KW), F32)],
        input_output_aliases={9: 0},
        scratch_shapes=[pltpu.VMEM((KW, ATT_WIDTH), F32), pltpu.VMEM((KW, ATT_WIDTH), F32)],
        compiler_params=_params(("arbitrary", "arbitrary")))(*([proj3] * 7), do, bias, dproj3)


def _rel_bias_grad(dbias):
    H = dbias.shape[0]
    Wd = KW + QB
    c = jnp.arange(Wd)[:, None] + 1
    bins = (jnp.clip(KW - c, -(CHUNK - 1), MAX_REL) + (CHUNK - 1) == jnp.arange(N_REL)[None, :]).astype(F32)

    def body(d_ref, bins_ref, o_ref, cols):
        rows = lax.broadcasted_iota(jnp.int32, (QB, QB), 0)
        lanes = lax.broadcasted_iota(jnp.int32, (QB, QB), 1)
        reverse = (rows + lanes == QB - 1).astype(F32)
        for h in range(H):
            flipped = jnp.dot(reverse, d_ref[h], preferred_element_type=F32, precision=lax.Precision.HIGHEST)
            wide = jnp.concatenate([flipped, jnp.zeros((QB, QB), F32)], axis=1)
            skew = pltpu.roll(wide, 0, 1, stride=1, stride_axis=0)
            cols[h:h + 1, :] = jnp.sum(skew, axis=0, keepdims=True)
        o_ref[...] = jnp.dot(cols[...], bins_ref[...], preferred_element_type=F32, precision=lax.Precision.HIGHEST)

    return pl.pallas_call(
        body, name="rel_bias_grad", grid=(1,),
        in_specs=[pl.BlockSpec((H, QB, KW), lambda i: (0, 0, 0)), pl.BlockSpec((Wd, N_REL), lambda i: (0, 0))],
        out_specs=pl.BlockSpec((H, N_REL), lambda i: (0, 0)), out_shape=jax.ShapeDtypeStruct((H, N_REL), F32),
        scratch_shapes=[pltpu.VMEM((H, Wd), F32)],
        compiler_params=_params(("arbitrary",)))(dbias, bins)


MEM_TILE = 512


def _mem_probs(qh, kh):
    s = _dot(qh, kh, 'nt') * MEM_SCALE
    e = jnp.exp(s - jnp.max(s, axis=-1, keepdims=True))
    return e * (1.0 / jnp.sum(e, axis=-1, keepdims=True))


def _mem_fwd(proj3, kv3):
    Bl, S, _ = proj3.shape
    tq = _pick(S, (MEM_TILE, 256))
    hd = MEM_HEAD_DIM

    def body(q_ref, kv_ref, o_ref):
        for h in range(MEM_HEADS):
            p = _mem_probs(q_ref[:, h * hd:(h + 1) * hd], kv_ref[:, h * hd:(h + 1) * hd])
            o_ref[:, h * hd:(h + 1) * hd] = _dot(p, kv_ref[:, MEM_WIDTH + h * hd:MEM_WIDTH + (h + 1) * hd]).astype(BF16)

    return pl.pallas_call(
        body, name="mem_fwd", grid=(Bl, S // tq),
        in_specs=[pl.BlockSpec((None, tq, MEM_WIDTH), lambda b, i: (b, i, 5)),
                  pl.BlockSpec((None, MEM_LEN, 2 * MEM_WIDTH), lambda b, i: (b, 0, 0))],
        out_specs=pl.BlockSpec((None, tq, MEM_WIDTH), lambda b, i: (b, i, 0)),
        out_shape=jax.ShapeDtypeStruct((Bl, S, MEM_WIDTH), BF16),
        compiler_params=_params(("parallel", "parallel")))(proj3, kv3)


def _mem_bwd(proj3, kv3, do, dproj3):
    Bl, S, _ = proj3.shape
    tq = _pick(S, (MEM_TILE, 256))
    hd = MEM_HEAD_DIM

    def body(q_ref, kv_ref, do_ref, through_ref, dq_ref, dkv_ref):
        i = pl.program_id(1)

        @pl.when(i == 0)
        def _():
            dkv_ref[...] = jnp.zeros_like(dkv_ref)

        for h in range(MEM_HEADS):
            ks = slice(h * hd, (h + 1) * hd)
            vs = slice(MEM_WIDTH + h * hd, MEM_WIDTH + (h + 1) * hd)
            qh, kh, vh, doh = q_ref[:, ks], kv_ref[:, ks], kv_ref[:, vs], do_ref[:, ks]
            p = _mem_probs(qh, kh)
            dp = _dot(doh, vh, 'nt')
            ds = p * (dp - jnp.sum(p * dp, axis=-1, keepdims=True))
            dq_ref[:, ks] = (_dot(ds, kh) * MEM_SCALE).astype(BF16)
            dkv_ref[:, ks] += _dot(ds, qh, 'tn') * MEM_SCALE
            dkv_ref[:, vs] += _dot(p, doh, 'tn')

    return pl.pallas_call(
        body, name="mem_bwd", grid=(Bl, S // tq),
        in_specs=[pl.BlockSpec((None, tq, MEM_WIDTH), lambda b, i: (b, i, 5)),
                  pl.BlockSpec((None, MEM_LEN, 2 * MEM_WIDTH), lambda b, i: (b, 0, 0)),
                  pl.BlockSpec((None, tq, MEM_WIDTH), lambda b, i: (b, i, 0)), ANY],
        out_specs=[pl.BlockSpec((None, tq, MEM_WIDTH), lambda b, i: (b, i, 5)),
                   pl.BlockSpec((None, MEM_LEN, 2 * MEM_WIDTH), lambda b, i: (b, 0, 0))],
        out_shape=[jax.ShapeDtypeStruct(dproj3.shape, BF16),
                   jax.ShapeDtypeStruct((Bl, MEM_LEN, 2 * MEM_WIDTH), F32)],
        input_output_aliases={3: 0},
        compiler_params=_params(("parallel", "arbitrary")))(proj3, kv3, do, dproj3)


def _position():
    x, y, c = lax.axis_index("x"), lax.axis_index("y"), lax.axis_index("c")
    return x, y, c, 4 * x + 2 * y + c


def _device(idx):
    return ((idx >> 2) & 1, (idx >> 1) & 1, idx & 1)


def _half_block(ref, axis, shard_shape, k, h):
    R, Cn = shard_shape
    if axis == 1:
        return ref.at[pl.ds(h * (R // 2), R // 2), pl.ds(k * Cn, Cn)]
    return ref.at[pl.ds(k * R + h * (R // 2), R // 2), :]


def _block(ref, axis, shard_shape, k):
    R, Cn = shard_shape
    if axis == 1:
        return ref.at[:, pl.ds(k * Cn, Cn)]
    return ref.at[pl.ds(k * R, R), :]


def _half(ref, h):
    R = ref.shape[0]
    return ref.at[pl.ds(h * (R // 2), R // 2), :]


ANY = pl.BlockSpec(memory_space=pl.ANY)


HBM = pl.BlockSpec(memory_space=pltpu.HBM)
SEM = pl.BlockSpec(memory_space=pltpu.SEMAPHORE)
VMEM_WHOLE = pl.BlockSpec(memory_space=pltpu.VMEM)
EFFECT = pltpu.SideEffectType.DATAFLOW_SIDE_EFFECTING


def _in_hbm(a):
    return pltpu.with_memory_space_constraint(a, pltpu.HBM)


def _split_start(body, name, sources, lands, n_copies):
    n = len(sources)
    out_shape, out_specs = [], []
    for _ in range(n):
        out_shape += [pltpu.SemaphoreType.DMA((n_copies,)), pltpu.SemaphoreType.DMA((n_copies,))]
        out_specs += [SEM, SEM]
    out_shape += [pltpu.HBM(a.shape, a.dtype) for a in list(sources) + list(lands)]
    out_specs += [HBM] * (2 * n)
    out_shape.append(jax.ShapeDtypeStruct((8, 128), F32))
    out_specs.append(VMEM_WHOLE)

    def call_body(*refs):
        srcs, lnds = refs[:n], refs[n:2 * n]
        sems = refs[2 * n:4 * n]
        token = refs[-1]
        body(srcs, lnds, sems[0::2], sems[1::2])
        token[...] = jnp.zeros_like(token)

    res = pl.pallas_call(
        call_body, name=name, in_specs=[HBM] * (2 * n), out_specs=out_specs, out_shape=out_shape,
        input_output_aliases={i: 2 * n + i for i in range(2 * n)},
        compiler_params=pltpu.CompilerParams(has_side_effects=EFFECT))(
            *[_in_hbm(a) for a in list(sources) + list(lands)])
    pairs = [(res[2 * w], res[2 * w + 1], res[2 * n + w], res[3 * n + w]) for w in range(n)]
    return pairs, res[-1]


def _split_wait(body, name, pairs, after):
    n = len(pairs)

    def call_body(*refs):
        srcs, lnds = refs[:n], refs[n:2 * n]
        sems = refs[2 * n:4 * n]
        body(srcs, lnds, sems[0::2], sems[1::2])

    args = [_in_hbm(p[2]) for p in pairs] + [_in_hbm(p[3]) for p in pairs]
    for p in pairs:
        args += [p[0], p[1]]
    res = pl.pallas_call(
        call_body, name=name, in_specs=[HBM] * (2 * n) + [SEM] * (2 * n) + [ANY], out_specs=[HBM] * (2 * n),
        out_shape=[pltpu.HBM(a.shape, a.dtype) for a in args[:2 * n]],
        input_output_aliases={i: i for i in range(2 * n)},
        compiler_params=pltpu.CompilerParams(has_side_effects=EFFECT))(*args, after)
    return res[:n], res[n:]


def _place_block(shard, axis, chip_idx, name, after=None):
    R, Cn = shard.shape
    tr = _pick(R, (256, 176, 128, 64, 32, 16, 8))
    nblk = R // tr

    def body(k_ref, s_ref, *rest):
        rest[-1][...] = s_ref[...]

    if axis == 1:
        out_shape, out_index = (R, 4 * Cn), lambda i, k: (i, k[0])
    else:
        out_shape, out_index = (4 * R, Cn), lambda i, k: (k[0] * nblk + i, 0)
    extra = () if after is None else (after,)
    return pl.pallas_call(
        body, name=name,
        grid_spec=pltpu.PrefetchScalarGridSpec(
            num_scalar_prefetch=1, grid=(nblk,),
            in_specs=[pl.BlockSpec((tr, Cn), lambda i, k: (i, 0))] + [ANY] * len(extra),
            out_specs=pl.BlockSpec((tr, Cn), out_index)),
        out_shape=jax.ShapeDtypeStruct(out_shape, shard.dtype),
        compiler_params=_params(("parallel",)))(chip_idx, shard, *extra)


def _gather_copy(srcs, lnds, send, recv, axes, shapes, w, j, me):
    chip = me >> 1
    return (pltpu.make_async_remote_copy(
        src_ref=srcs[w], dst_ref=_block(lnds[w], axes[w], shapes[w], chip), send_sem=send[w].at[j],
        recv_sem=recv[w].at[j], device_id=_device(me ^ (2 * (j + 1))), device_id_type=MESH),
            pltpu.make_async_remote_copy(
        src_ref=srcs[w], dst_ref=_block(lnds[w], axes[w], shapes[w], chip ^ (j + 1)), send_sem=send[w].at[j],
        recv_sem=recv[w].at[j], device_id=_device(me ^ (2 * (j + 1))), device_id_type=MESH))


def _gather_start(shards, lands, axes, name):
    shapes = [s.shape for s in shards]

    def body(srcs, lnds, send, recv):
        x, y, c, me = _position()
        for w in range(len(shards)):
            for j in range(3):
                _gather_copy(srcs, lnds, send, recv, axes, shapes, w, j, me)[0].start()

    return _split_start(body, name, shards, lands, 3)


def _gather_wait(pairs, axes, after, name):
    shapes = [p[2].shape for p in pairs]

    def body(srcs, lnds, send, recv):
        x, y, c, me = _position()
        for w in range(len(pairs)):
            for j in range(3):
                sent, landed = _gather_copy(srcs, lnds, send, recv, axes, shapes, w, j, me)
                sent.wait_send()
                landed.wait_recv()

    return _split_wait(body, name, pairs, after)[1]


def _shard_shape(grad, axis):
    return (grad.shape[0], grad.shape[1] // 4) if axis == 1 else (grad.shape[0] // 4, grad.shape[1])


def _scatter_copy(srcs, lnds, send, recv, axes, shapes, w, m, me):
    peer = me ^ m
    return pltpu.make_async_remote_copy(
        src_ref=_half_block(srcs[w], axes[w], shapes[w], peer >> 1, peer & 1), dst_ref=lnds[w].at[m - 1],
        send_sem=send[w].at[m - 1], recv_sem=recv[w].at[m - 1], device_id=_device(peer), device_id_type=MESH)


def _scatter_start(grads, axes, name):
    shapes = [_shard_shape(g, a) for g, a in zip(grads, axes)]
    lands = [lax.empty((N_DEV - 1, R // 2, Cn), BF16) for R, Cn in shapes]

    def body(srcs, lnds, send, recv):
        x, y, c, me = _position()
        for w in range(len(grads)):
            for m in range(1, N_DEV):
                _scatter_copy(srcs, lnds, send, recv, axes, shapes, w, m, me).start()

    return _split_start(body, name, grads, lands, N_DEV - 1)


def _scatter_wait(pairs, axes, after):
    shapes = [_shard_shape(p[2], a) for p, a in zip(pairs, axes)]

    def body(srcs, lnds, send, recv):
        x, y, c, me = _position()
        for w in range(len(pairs)):
            for m in range(1, N_DEV):
                cp = _scatter_copy(srcs, lnds, send, recv, axes, shapes, w, m, me)
                cp.wait_send()
                cp.wait_recv()

    return _split_wait(body, "scatter_wait", pairs, after)


def _sum_partials(own, parts, half, name):
    R, Cn = own.shape
    tr = _pick(R, (256, 176, 128, 64, 32, 16, 8))
    nblk = R // tr

    def body(half_ref, own_ref, p_ref, o_ref):
        acc = own_ref[...].astype(F32)
        for d in range(N_DEV - 1):
            acc = acc + p_ref[d].astype(F32)
        o_ref[...] = acc

    return pl.pallas_call(
        body, name=name,
        grid_spec=pltpu.PrefetchScalarGridSpec(
            num_scalar_prefetch=1, grid=(nblk,),
            in_specs=[pl.BlockSpec((tr, Cn), lambda i, hr: (i, 0)),
                      pl.BlockSpec((N_DEV - 1, tr, Cn), lambda i, hr: (0, i, 0))],
            out_specs=pl.BlockSpec((tr, Cn), lambda i, hr: (hr[0] * nblk + i, 0))),
        out_shape=jax.ShapeDtypeStruct((2 * R, Cn), F32),
        compiler_params=_params(("parallel",)))(half, own, parts)


def _exchange_halves(grads):
    n = len(grads)

    def body(*refs):
        outs = refs[n:2 * n]
        send, recv = refs[2 * n:]
        x, y, c, me = _position()

        def copy(w, half):
            rows = _half(outs[w], half)
            return pltpu.make_async_remote_copy(src_ref=rows, dst_ref=rows, send_sem=send.at[w],
                                                recv_sem=recv.at[w], device_id=_device(me ^ 1), device_id_type=MESH)

        for w in range(n):
            copy(w, c).start()
        for w in range(n):
            copy(w, 1 - c).wait_recv()
        for w in range(n):
            copy(w, c).wait_send()

    return pl.pallas_call(
        body, name="exchange_halves", in_specs=[ANY] * n, out_specs=[ANY] * n,
        out_shape=[jax.ShapeDtypeStruct(a.shape, a.dtype) for a in grads],
        input_output_aliases={i: i for i in range(n)},
        scratch_shapes=[pltpu.SemaphoreType.DMA((n,)), pltpu.SemaphoreType.DMA((n,))],
        compiler_params=pltpu.CompilerParams(has_side_effects=True))(*grads)


def _all_reduce_small(vec):
    R, L = vec.shape

    def body(v_ref, o_ref, buf, send, recv):
        x, y, c, me = _position()
        buf[me] = v_ref[...]

        def copy(m, slot):
            return pltpu.make_async_remote_copy(src_ref=v_ref, dst_ref=buf.at[slot], send_sem=send.at[m - 1],
                                                recv_sem=recv.at[m - 1], device_id=_device(me ^ m),
                                                device_id_type=MESH)

        for m in range(1, N_DEV):
            copy(m, me).start()
        for m in range(1, N_DEV):
            copy(m, me ^ m).wait_recv()
        for m in range(1, N_DEV):
            copy(m, me).wait_send()
        acc = buf[0]
        for d in range(1, N_DEV):
            acc = acc + buf[d]
        o_ref[...] = acc

    vm = pl.BlockSpec(memory_space=pltpu.VMEM)
    return pl.pallas_call(
        body, name="all_reduce_small", in_specs=[vm], out_specs=vm, out_shape=jax.ShapeDtypeStruct((R, L), F32),
        scratch_shapes=[pltpu.VMEM((N_DEV, R, L), F32), pltpu.SemaphoreType.DMA((N_DEV - 1,)),
                        pltpu.SemaphoreType.DMA((N_DEV - 1,))],
        compiler_params=pltpu.CompilerParams(has_side_effects=True))(vec)


def _adamw(w, g, m, v, name):
    R, Cn = w.shape
    tr = _pick(R, (256, 176, 128, 64, 40, 32, 16, 8))

    def body(w_ref, g_ref, m_ref, v_ref, d_ref, nm_ref, nv_ref):
        gv = g_ref[...]
        nm = ADAM_B1 * m_ref[...] + (1.0 - ADAM_B1) * gv
        nv = ADAM_B2 * v_ref[...] + (1.0 - ADAM_B2) * (gv * gv)
        m_hat = nm / (1.0 - ADAM_B1 ** ADAM_STEP)
        v_hat = nv / (1.0 - ADAM_B2 ** ADAM_STEP)
        d_ref[...] = -ADAM_LR * (m_hat / (jnp.sqrt(v_hat) + ADAM_EPS) + ADAM_WD * w_ref[...])
        nm_ref[...] = nm
        nv_ref[...] = nv

    spec = pl.BlockSpec((tr, Cn), lambda i: (i, 0))
    return pl.pallas_call(
        body, name=name, grid=(R // tr,), in_specs=[spec] * 4, out_specs=[spec] * 3,
        out_shape=[jax.ShapeDtypeStruct((R, Cn), F32)] * 3, compiler_params=_params(("parallel",)))(w, g, m, v)


def _pack(arrays, rows):
    flat = jnp.concatenate([a.reshape(-1).astype(F32) for a in arrays])
    return jnp.pad(flat, (0, rows * 128 - flat.shape[0])).reshape(rows, 128)


def _unpack(packed, shapes):
    flat = packed.reshape(-1)
    out, off = [], 0
    for s in shapes:
        size = 1
        for d in s:
            size *= d
        out.append(flat[off:off + size].reshape(s))
        off += size
    return out


def _ffn_fwd(x, h, arrived, tag, next_g=None):
    w_up = arrived(f"{tag}_w_up", h)
    a, b, act = _ffn_up(h, w_up, f"{tag}_up")
    w_down = arrived(f"{tag}_w_down", act)
    out, h_next = _ffn_down(act, w_down, x, f"{tag}_down", next_g)
    return out, h_next, (h, a, b, act, w_up, w_down)


def _ffn_bwd(dout, x, norm, saved, tag, send):
    h, a, b, act, w_up, w_down = saved
    g_down = _mm(act, dout, 'tn', f"{tag}_down_dw", BF16, scale=0.5)
    token = send([f"{tag}_w_down"], [g_down])
    da, db = _ffn_down_bwd(dout, w_down, a, b, f"{tag}_down_dx", after=token)
    g_up = _dw_pieces(h, [da, db], f"{tag}_up_dw")
    token = send([f"{tag}_w_up"], [g_up])
    return _dx_rms_bwd([da, db], w_up, x, norm, dout, f"{tag}_up_dx", after=token)


def kernel(x, mem, ffn1_norm, ffn1_w_up, ffn1_w_down, mix_norm, mem_norm, w_in, b_gate, conv_dw_w, conv_dw_b, conv_ln_g, conv_ln_b, conv_w_pw, att_rel_bias, att_w_o, mem_w_kv, mem_w_o, w_out, ffn2_norm, ffn2_w_up, ffn2_w_down, final_norm, loss_target, m_ffn1_norm, m_ffn1_w_up, m_ffn1_w_down, m_mix_norm, m_mem_norm, m_w_in, m_b_gate, m_conv_dw_w, m_conv_dw_b, m_conv_ln_g, m_conv_ln_b, m_conv_w_pw, m_att_rel_bias, m_att_w_o, m_mem_w_kv, m_mem_w_o, m_w_out, m_ffn2_norm, m_ffn2_w_up, m_ffn2_w_down, m_final_norm, v_ffn1_norm, v_ffn1_w_up, v_ffn1_w_down, v_mix_norm, v_mem_norm, v_w_in, v_b_gate, v_conv_dw_w, v_conv_dw_b, v_conv_ln_g, v_conv_ln_b, v_conv_w_pw, v_att_rel_bias, v_att_w_o, v_mem_w_kv, v_mem_w_o, v_w_out, v_ffn2_norm, v_ffn2_w_up, v_ffn2_w_down, v_final_norm):
    given = dict(locals())
    wts = {n: given[n] for n in WEIGHTS}
    mom1 = {n: given["m_" + n] for n in WEIGHTS}
    mom2 = {n: given["v_" + n] for n in WEIGHTS}
    Bl, S, Dm = x.shape
    T = Bl * S
    x0 = x.reshape(T, Dm)
    tgt = loss_target.reshape(T, Dm)
    mem2 = mem.reshape(Bl * MEM_LEN, Dm)

    big_names = [n for n, _ in BIG]
    big_axes = [a for _, a in BIG]
    chip = 2 * lax.axis_index("x") + lax.axis_index("y")

    core = lax.axis_index("c")
    axis_of = dict(BIG)

    gather_groups = [['ffn1_w_up'], ['ffn1_w_down'], ['w_in', 'conv_dw_w'],
                     ['mem_w_kv', 'conv_w_pw', 'att_w_o', 'mem_w_o', 'w_out'], ['ffn2_w_up'], ['ffn2_w_down']]
    gather_names = [n for grp in gather_groups for n in grp]
    gather_axes = [axis_of.get(n, 1) for n in gather_names]
    shards = [jnp.pad(conv_dw_w[0], ((0, 1), (0, 0))) if n == 'conv_dw_w' else wts[n][0].astype(BF16)
              for n in gather_names]
    chip_idx = chip.reshape(1).astype(jnp.int32)
    first, first_token = _gather_start(
        shards[:1], [_place_block(shards[0], gather_axes[0], chip_idx, f"place_{gather_names[0]}")],
        gather_axes[:1], "gather_start_first")
    lands = [_place_block(sh, a, chip_idx, f"place_{n}", after=first_token)
             for sh, a, n in zip(shards[1:], gather_axes[1:], gather_names[1:])]
    rest, gather_token = _gather_start(shards[1:], lands, gather_axes[1:], "gather_start_rest")
    in_flight = dict(zip(gather_names, first + rest))
    full = {}

    def arrived(name, after):
        if name not in full:
            grp = next(grp for grp in gather_groups if name in grp)
            lands = _gather_wait([in_flight[n] for n in grp], [axis_of.get(n, 1) for n in grp], after,
                                 f"gather_wait_{grp[0]}")
            full.update(zip(grp, lands))
        return full[name]

    scattering = {}

    def send(names, grads):
        pairs, token = _scatter_start(grads, [axis_of[n] for n in names], f"scatter_start_{names[0]}")
        scattering.update(zip(names, pairs))
        return token

    final_g = final_norm.reshape(1, Dm)
    bias = _att_bias(att_rel_bias[0] + first_token[:1, :1])

    x1, h, ffn1_saved = _ffn_fwd(x0, _rms_fwd(x0, ffn1_norm, "ffn1_norm", after=gather_token), arrived, "ffn1",
                                 next_g=mix_norm)
    w_in_full = arrived('w_in', h)
    dw_full = full['conv_dw_w'][:CONV_KERNEL]
    proj = _mm(h, w_in_full, 'nn', "w_in", BF16)
    proj3 = proj.reshape(Bl, S, proj.shape[1])
    cv, c_act = _conv_fwd(proj3, dw_full, conv_dw_b, conv_ln_g, conv_ln_b)
    o_att = _att_fwd(proj3, bias)
    mem_h = _rms_fwd(mem2, mem_norm, "mem_norm")
    kv = _mm(mem_h, arrived('mem_w_kv', o_att), 'nn', "mem_kv", BF16)
    kv3 = kv.reshape(Bl, MEM_LEN, 2 * MEM_WIDTH)
    o_mem = _mem_fwd(proj3, kv3)
    c_act2, o_att2, o_mem2 = c_act.reshape(T, -1), o_att.reshape(T, -1), o_mem.reshape(T, -1)
    x2, yc, ya, ym, h2 = _mix_fwd(c_act2, o_att2, o_mem2, proj, b_gate, x1, full['conv_w_pw'], full['att_w_o'],
                                  full['mem_w_o'], full['w_out'], ffn2_norm)
    x3, _, ffn2_saved = _ffn_fwd(x2, h2, arrived, "ffn2")
    dx3, g_final, loss_vec = _final_fwd_bwd(x3, tgt, final_g)

    g = {}
    dx2, g['ffn2_norm'] = _ffn_bwd(dx3, x2, ffn2_norm, ffn2_saved, "ffn2", send)
    dgl, g['b_gate'], dc, doa, dom, g_pw, g_o, g_mo, g_out = _mix_bwd(
        dx2, yc, ya, ym, c_act2, o_att2, o_mem2, proj, b_gate, full['conv_w_pw'], full['att_w_o'],
        full['mem_w_o'], full['w_out'])
    token = send(['w_out', 'conv_w_pw', 'att_w_o', 'mem_w_o'], [g_out, g_pw, g_o, g_mo])
    dproj3 = dgl.reshape(Bl, S, -1)
    dproj3, g_dw, g['conv_dw_b'], g['conv_ln_g'], g['conv_ln_b'] = _conv_bwd(
        proj3, cv, dc.reshape(Bl, S, -1), dw_full, conv_ln_g, conv_ln_b, dproj3)
    dproj3, dk, dv, dbias = _att_bwd(proj3, doa.reshape(Bl, S, -1), bias, dproj3)
    g['att_rel_bias'] = _rel_bias_grad(dbias)
    dproj3, dkv = _mem_bwd(proj3, kv3, dom.reshape(Bl, S, -1), dproj3)
    dkv2 = dkv.reshape(Bl * MEM_LEN, 2 * MEM_WIDTH)
    g_kv = _mm(mem_h, dkv2, 'tn', "mem_kv_dw", BF16, after=token)
    dmem_h = _mm(dkv2, full['mem_w_kv'], 'nt', "mem_kv_dx", F32)
    _, g['mem_norm'] = _rms_bwd(mem2, mem_norm, dmem_h, dmem_h, "mem_norm_bwd")
    dkdv = jnp.concatenate([dk.reshape(T, -1), dv.reshape(T, -1)], axis=1)
    dproj = lax.dynamic_update_slice(dproj3.reshape(T, -1), dkdv, (0, 2 * CONV_WIDTH + ATT_WIDTH))
    token = send(['mem_w_kv', 'w_in'], [g_kv, _mm(h, dproj, 'tn', "w_in_dw", BF16)])
    dx1, g['mix_norm'] = _dx_rms_bwd([dproj], w_in_full, x1, mix_norm, dx2, "w_in_dx", after=token)
    dx0, g['ffn1_norm'] = _ffn_bwd(dx1, x0, ffn1_norm, ffn1_saved, "ffn1", send)
    g['final_norm'] = g_final

    sent, landed = _scatter_wait([scattering[n] for n in big_names], big_axes, dx0)
    halves = []
    half_idx = core.reshape(1).astype(jnp.int32)
    for n, a, own_full, parts in zip(big_names, big_axes, sent, landed):
        R, Cn = _shard_shape(own_full, a)
        start = (core * (R // 2), chip * Cn) if a == 1 else (chip * R + core * (R // 2), 0)
        own = lax.dynamic_slice(own_full, start, (R // 2, Cn))
        halves.append(_sum_partials(own, parts, half_idx, f"sum_{n}"))
    for n, sg in zip(big_names, _exchange_halves(halves)):
        g[n] = sg

    small_shapes = [wts[n].shape for n in SMALL]
    n_small = sum(int(wts[n].size) for n in SMALL)
    n_red = n_small + CONV_KERNEL * CONV_WIDTH + 1
    red = _all_reduce_small(_pack([g[n] for n in SMALL] + [g_dw, loss_vec[0, :1]], -(-n_red // 1024) * 8))
    red_list = _unpack(red, small_shapes + [(CONV_KERNEL, CONV_WIDTH), ()])
    for n, rg in zip(SMALL, red_list[:-2]):
        g[n] = rg
    loss = red_list[-1]
    dw_cols = conv_dw_w.shape[2]
    g['conv_dw_w'] = lax.dynamic_slice(red_list[-2], (0, chip * dw_cols), (CONV_KERNEL, dw_cols))[None]

    delta, new_m, new_v = {}, {}, {}
    for n in big_names:
        g[n] = g[n][None]
        d, nm, nv = _adamw(wts[n][0], g[n][0], mom1[n][0], mom2[n][0], f"adamw_{n}")
        delta[n], new_m[n], new_v[n] = d[None], nm[None], nv[None]
    rest = SMALL + ['conv_dw_w']
    rest_shapes = [wts[n].shape for n in rest]
    rows = -(-sum(int(wts[n].size) for n in rest) // 1024) * 8
    packed = [_pack([src[n] for n in rest], rows) for src in (wts, g, mom1, mom2)]
    for out, res in zip((delta, new_m, new_v), _adamw(*packed, "adamw_small")):
        for n, a in zip(rest, _unpack(res, rest_shapes)):
            out[n] = a

    grad_x = dx0.reshape(Bl, S, Dm)
    return (loss, grad_x, *[g[n] for n in WEIGHTS], *[delta[n] for n in WEIGHTS],
            *[new_m[n] for n in WEIGHTS], *[new_v[n] for n in WEIGHTS])
```

```python
import jax
import jax.numpy as jnp
from jax import lax
from jax.experimental import pallas as pl
from jax.experimental.pallas import tpu as pltpu

F32 = jnp.float32
BF16 = jnp.bfloat16

CHUNK = 64
LEFT_CHUNKS = 8
MAX_REL = 128
N_REL = (CHUNK - 1) + MAX_REL + 1
CONV_WIDTH = 512
CONV_KERNEL = 31
ATT_HEADS = 8
ATT_WIDTH = 512
MEM_LEN = 256
MEM_HEADS = 4
MEM_HEAD_DIM = 128
MEM_WIDTH = 512
EPS = 1e-6
MASK_VALUE = -1e30
ATT_SCALE = 64 ** -0.5
MEM_SCALE = 128 ** -0.5

ADAM_LR = 0.001
ADAM_B1 = 0.9
ADAM_B2 = 0.999
ADAM_EPS = 1e-08
ADAM_WD = 0.01
ADAM_STEP = 10

QB = 256
KW = 3 * QB
CONV_PAD = 32
CONV_TILE = 256

VMEM_LIMIT = 56 << 20
MXU_COLS = 256

WEIGHTS = ['ffn1_norm', 'ffn1_w_up', 'ffn1_w_down', 'mix_norm', 'mem_norm', 'w_in', 'b_gate', 'conv_dw_w',
           'conv_dw_b', 'conv_ln_g', 'conv_ln_b', 'conv_w_pw', 'att_rel_bias', 'att_w_o', 'mem_w_kv', 'mem_w_o',
           'w_out', 'ffn2_norm', 'ffn2_w_up', 'ffn2_w_down', 'final_norm']
BIG = [('ffn1_w_up', 1), ('ffn1_w_down', 0), ('w_in', 1), ('conv_w_pw', 1), ('att_w_o', 1), ('mem_w_kv', 0),
       ('mem_w_o', 1), ('w_out', 0), ('ffn2_w_up', 1), ('ffn2_w_down', 0)]
SMALL = ['ffn1_norm', 'mix_norm', 'mem_norm', 'b_gate', 'conv_dw_b', 'conv_ln_g', 'conv_ln_b', 'att_rel_bias',
         'ffn2_norm', 'final_norm']
N_DEV = 8
MESH = pl.DeviceIdType.MESH


def _pick(n, cands):
    for c in cands:
        if n % c == 0:
            return c
    return n


def _sig(x):
    return 0.5 * jnp.tanh(0.5 * x) + 0.5


def _params(sem=None, vmem=VMEM_LIMIT):
    return pltpu.CompilerParams(dimension_semantics=sem, vmem_limit_bytes=vmem)


def _dot(a, b, mode='nn'):
    dims = {'nn': (((1,), (0,)), ((), ())), 'nt': (((1,), (1,)), ((), ())), 'tn': (((0,), (0,)), ((), ()))}[mode]
    return lax.dot_general(a.astype(BF16), b.astype(BF16), dims, preferred_element_type=F32)


def _mm(a, b, mode, name, out_dtype, res=None, scale=1.0, after=None):
    if mode == 'nn':
        (M, C), (_, N) = a.shape, b.shape
    elif mode == 'nt':
        (M, C), (N, _) = a.shape, b.shape
    else:
        (C, M), (_, N) = a.shape, b.shape
    tm = _pick(M, (1024, 1408, 512, 256, 128))
    tn = _pick(N, (1024, 1408, 512, 256, 128))
    tc = C if C <= 2816 else _pick(C, (2048, 1024, 1408, 512, 256, 128))
    nk = C // tc
    if mode == 'nn':
        a_spec = pl.BlockSpec((tm, tc), lambda i, j, k: (i, k))
        b_spec = pl.BlockSpec((tc, tn), lambda i, j, k: (k, j))
    elif mode == 'nt':
        a_spec = pl.BlockSpec((tm, tc), lambda i, j, k: (i, k))
        b_spec = pl.BlockSpec((tn, tc), lambda i, j, k: (j, k))
    else:
        a_spec = pl.BlockSpec((tc, tm), lambda i, j, k: (k, i))
        b_spec = pl.BlockSpec((tc, tn), lambda i, j, k: (k, j))
    o_spec = pl.BlockSpec((tm, tn), lambda i, j, k: (i, j))
    has_res = res is not None
    has_after = after is not None

    def body(*refs):
        a_ref, b_ref = refs[:2]
        r_ref = refs[2] if has_res else None
        o_ref, acc_ref = refs[-2:]
        k = pl.program_id(2)

        def finish(acc):
            if scale != 1.0:
                acc = acc * scale
            if r_ref is not None:
                acc = r_ref[...] + acc
            o_ref[...] = acc.astype(o_ref.dtype)

        if nk == 1:
            finish(_dot(a_ref[...], b_ref[...], mode))
        else:
            @pl.when(k == 0)
            def _():
                acc_ref[...] = jnp.zeros_like(acc_ref)

            acc_ref[...] += _dot(a_ref[...], b_ref[...], mode)

            @pl.when(k == nk - 1)
            def _():
                finish(acc_ref[...])

    in_specs = [a_spec, b_spec] + ([o_spec] if has_res else []) + ([ANY] if has_after else [])
    args = (a, b) + ((res,) if has_res else ()) + ((after,) if has_after else ())
    acc_shape = (tm, tn) if nk > 1 else (8, 128)
    return pl.pallas_call(
        body, name=name, grid=(M // tm, N // tn, nk), in_specs=in_specs, out_specs=o_spec,
        out_shape=jax.ShapeDtypeStruct((M, N), out_dtype), scratch_shapes=[pltpu.VMEM(acc_shape, F32)],
        compiler_params=_params(("parallel", "parallel", "arbitrary")))(*args)


def _row_tile(T):
    return _pick(T, (512, 256, 128, 64, 32, 16, 8))


def _rms_fwd(x, g, name, after=None):
    T, Dm = x.shape
    tm = _row_tile(T)

    def body(x_ref, g_ref, *rest):
        o_ref = rest[-1]
        xv = x_ref[...]
        r = lax.rsqrt(jnp.mean(xv * xv, axis=-1, keepdims=True) + EPS)
        o_ref[...] = ((xv * r) * g_ref[...]).astype(o_ref.dtype)

    extra = () if after is None else (after,)
    return pl.pallas_call(
        body, name=name, grid=(T // tm,),
        in_specs=[pl.BlockSpec((tm, Dm), lambda i: (i, 0)), pl.BlockSpec((1, Dm), lambda i: (0, 0))]
        + [ANY] * len(extra),
        out_specs=pl.BlockSpec((tm, Dm), lambda i: (i, 0)), out_shape=jax.ShapeDtypeStruct((T, Dm), BF16),
        compiler_params=_params(("parallel",)))(x, g, *extra)


def _rms_bwd(x, g, dh, dres, name):
    T, Dm = x.shape
    tm = _row_tile(T)

    def body(x_ref, g_ref, dh_ref, dr_ref, dx_ref, dg_ref):
        i = pl.program_id(0)
        xv = x_ref[...]
        r = lax.rsqrt(jnp.mean(xv * xv, axis=-1, keepdims=True) + EPS)
        xr = xv * r
        dh_v = dh_ref[...].astype(F32)
        dyg = dh_v * g_ref[...]
        dx = r * (dyg - xr * jnp.mean(dyg * xr, axis=-1, keepdims=True))
        dx_ref[...] = dr_ref[...] + dx

        @pl.when(i == 0)
        def _():
            dg_ref[...] = jnp.zeros_like(dg_ref)

        dg_ref[...] += jnp.sum(dh_v * xr, axis=0, keepdims=True)

    row = pl.BlockSpec((tm, Dm), lambda i: (i, 0))
    vec = pl.BlockSpec((1, Dm), lambda i: (0, 0))
    return pl.pallas_call(
        body, name=name, grid=(T // tm,), in_specs=[row, vec, row, row], out_specs=[row, vec],
        out_shape=[jax.ShapeDtypeStruct((T, Dm), F32), jax.ShapeDtypeStruct((1, Dm), F32)],
        compiler_params=_params(("arbitrary",)))(x, g, dh, dres)


def _final_fwd_bwd(x3, tgt, g):
    T, Dm = x3.shape
    tm = _row_tile(T)

    def body(x_ref, t_ref, g_ref, dx_ref, dg_ref, loss_ref):
        i = pl.program_id(0)
        xv = x_ref[...]
        gg = g_ref[...]
        r = lax.rsqrt(jnp.mean(xv * xv, axis=-1, keepdims=True) + EPS)
        xr = xv * r
        err = xr * gg - t_ref[...]
        dout = err * (1.0 / Dm)
        dyg = dout * gg
        dx_ref[...] = r * (dyg - xr * jnp.mean(dyg * xr, axis=-1, keepdims=True))

        @pl.when(i == 0)
        def _():
            dg_ref[...] = jnp.zeros_like(dg_ref)
            loss_ref[...] = jnp.zeros_like(loss_ref)

        dg_ref[...] += jnp.sum(dout * xr, axis=0, keepdims=True)
        loss_ref[...] += jnp.zeros_like(loss_ref) + (0.5 / Dm) * jnp.sum(err * err)

    row = pl.BlockSpec((tm, Dm), lambda i: (i, 0))
    vec = pl.BlockSpec((1, Dm), lambda i: (0, 0))
    one = pl.BlockSpec((1, 128), lambda i: (0, 0))
    return pl.pallas_call(
        body, name="final_fwd_bwd", grid=(T // tm,), in_specs=[row, row, vec], out_specs=[row, vec, one],
        out_shape=[jax.ShapeDtypeStruct((T, Dm), F32), jax.ShapeDtypeStruct((1, Dm), F32),
                   jax.ShapeDtypeStruct((1, 128), F32)],
        compiler_params=_params(("arbitrary",)))(x3, tgt, g)


def _ffn_up(h, w_up, name):
    T, K = h.shape
    Fh = w_up.shape[1] // 2
    tm = _pick(T, (1024, 512, 256, 128))
    tn = _pick(Fh, (1408, 512, 256, 128))
    nj = Fh // tn

    def body(h_ref, wa_ref, wb_ref, a_ref, b_ref, act_ref):
        hv = h_ref[...]
        for c0 in range(0, tn, MXU_COLS):
            cs = slice(c0, min(c0 + MXU_COLS, tn))
            a = _dot(hv, wa_ref[:, cs])
            b = _dot(hv, wb_ref[:, cs])
            a_ref[:, cs] = a.astype(BF16)
            b_ref[:, cs] = b.astype(BF16)
            act_ref[:, cs] = (a * _sig(a) * b).astype(BF16)

    out = pl.BlockSpec((tm, tn), lambda i, j: (i, j))
    return pl.pallas_call(
        body, name=name, grid=(T // tm, nj),
        in_specs=[pl.BlockSpec((tm, K), lambda i, j: (i, 0)), pl.BlockSpec((K, tn), lambda i, j: (0, j)),
                  pl.BlockSpec((K, tn), lambda i, j: (0, j + nj))],
        out_specs=[out, out, out], out_shape=[jax.ShapeDtypeStruct((T, Fh), BF16)] * 3,
        compiler_params=_params(("parallel", "parallel")))(h, w_up, w_up)


def _ffn_down(act, w_down, x, name, next_g=None):
    T, Fh = act.shape
    Dm = w_down.shape[1]
    tm = _pick(T, (1024, 512, 256, 128))
    rows = _pick(tm, (256, 128))
    with_norm = next_g is not None

    def body(a_ref, w_ref, x_ref, *rest):
        o_ref = rest[-2] if with_norm else rest[-1]
        o_ref[...] = x_ref[...] + 0.5 * _dot(a_ref[...], w_ref[...])
        if with_norm:
            g_ref, h_ref = rest[0], rest[-1]

            def chunk(c, carry):
                rs = pl.ds(pl.multiple_of(c * rows, rows), rows)
                ov = o_ref[rs, :]
                r = lax.rsqrt(jnp.mean(ov * ov, axis=-1, keepdims=True) + EPS)
                h_ref[rs, :] = ((ov * r) * g_ref[...]).astype(BF16)
                return carry

            lax.fori_loop(0, tm // rows, chunk, 0)

    row = pl.BlockSpec((tm, Dm), lambda i: (i, 0))
    in_specs = [pl.BlockSpec((tm, Fh), lambda i: (i, 0)), pl.BlockSpec((Fh, Dm), lambda i: (0, 0)), row]
    out_specs, out_shape = [row], [jax.ShapeDtypeStruct((T, Dm), F32)]
    args = (act, w_down, x)
    if with_norm:
        in_specs.append(pl.BlockSpec((1, Dm), lambda i: (0, 0)))
        out_specs.append(row)
        out_shape.append(jax.ShapeDtypeStruct((T, Dm), BF16))
        args += (next_g,)
    res = pl.pallas_call(body, name=name, grid=(T // tm,), in_specs=in_specs, out_specs=out_specs,
                         out_shape=out_shape, compiler_params=_params(("parallel",)))(*args)
    return (res[0], res[1]) if with_norm else (res[0], None)


def _ffn_down_bwd(dout, w_down, a, b, name, after=None):
    T, Dm = dout.shape
    Fh = w_down.shape[0]
    tm = _pick(T, (1024, 512, 256, 128))
    tn = _pick(Fh, (1408, 512, 256, 128))

    def body(d_ref, w_ref, a_ref, b_ref, *rest):
        da_ref, db_ref = rest[-2:]
        dv = (d_ref[...] * 0.5).astype(BF16)
        for c0 in range(0, tn, MXU_COLS):
            cs = slice(c0, min(c0 + MXU_COLS, tn))
            dact = _dot(dv, w_ref[cs, :], 'nt')
            av = a_ref[:, cs].astype(F32)
            bv = b_ref[:, cs].astype(F32)
            s = _sig(av)
            da_ref[:, cs] = (dact * bv * s * (1.0 + av * (1.0 - s))).astype(BF16)
            db_ref[:, cs] = (dact * av * s).astype(BF16)

    tile = pl.BlockSpec((tm, tn), lambda i, j: (i, j))
    extra = () if after is None else (after,)
    return pl.pallas_call(
        body, name=name, grid=(T // tm, Fh // tn),
        in_specs=[pl.BlockSpec((tm, Dm), lambda i, j: (i, 0)), pl.BlockSpec((tn, Dm), lambda i, j: (j, 0)),
                  tile, tile] + [ANY] * len(extra),
        out_specs=[tile, tile], out_shape=[jax.ShapeDtypeStruct((T, Fh), BF16)] * 2,
        compiler_params=_params(("parallel", "parallel")))(dout, w_down, a, b, *extra)


def _dx_rms_bwd(pieces, w, x, g, dres, name, after=None):
    T, Dm = x.shape
    width = pieces[0].shape[1]
    tm = _pick(T, (1024, 512, 256, 128))
    tc = _pick(width, (1408, 2048, 1024, 512, 256, 128))
    per = width // tc
    nk = per * len(pieces)
    npc = len(pieces)
    rows = _pick(tm, (256, 128))

    def body(*refs):
        p_refs = refs[:npc]
        w_ref, x_hbm, g_ref, dr_hbm = refs[npc:npc + 4]
        dx_ref, dg_ref, acc_ref, x_buf, dr_buf, sems = refs[-6:]
        i = pl.program_id(0)
        k = pl.program_id(1)
        tile = pl.ds(pl.multiple_of(i * tm, tm), tm)
        fetch_x = pltpu.make_async_copy(x_hbm.at[tile, :], x_buf, sems.at[0])
        fetch_dr = pltpu.make_async_copy(dr_hbm.at[tile, :], dr_buf, sems.at[1])

        @pl.when(k == 0)
        def _():
            fetch_x.start()
            fetch_dr.start()
            acc_ref[...] = jnp.zeros_like(acc_ref)

        @pl.when((i == 0) & (k == 0))
        def _():
            dg_ref[...] = jnp.zeros_like(dg_ref)

        for p in range(npc):
            @pl.when((k >= p * per) & (k < (p + 1) * per))
            def _(p=p):
                acc_ref[...] += _dot(p_refs[p][...], w_ref[...], 'nt')

        @pl.when(k == nk - 1)
        def _():
            fetch_x.wait()
            fetch_dr.wait()

            def chunk(c, carry):
                rs = pl.ds(pl.multiple_of(c * rows, rows), rows)
                dh = acc_ref[rs, :]
                xv = x_buf[rs, :]
                r = lax.rsqrt(jnp.mean(xv * xv, axis=-1, keepdims=True) + EPS)
                xr = xv * r
                dyg = dh * g_ref[...]
                dx_ref[rs, :] = dr_buf[rs, :] + r * (dyg - xr * jnp.mean(dyg * xr, axis=-1, keepdims=True))
                dg_ref[...] += jnp.sum(dh * xr, axis=0, keepdims=True)
                return carry

            lax.fori_loop(0, tm // rows, chunk, 0)

    def piece_spec(p):
        return pl.BlockSpec((tm, tc), lambda i, k: (i, jnp.clip(k - p * per, 0, per - 1)))

    row = pl.BlockSpec((tm, Dm), lambda i, k: (i, 0))
    vec = pl.BlockSpec((1, Dm), lambda i, k: (0, 0))
    extra = () if after is None else (after,)
    return pl.pallas_call(
        body, name=name, grid=(T // tm, nk),
        in_specs=[piece_spec(p) for p in range(npc)] + [pl.BlockSpec((Dm, tc), lambda i, k: (0, k)), ANY, vec, ANY]
        + [ANY] * len(extra),
        out_specs=[row, vec], out_shape=[jax.ShapeDtypeStruct((T, Dm), F32), jax.ShapeDtypeStruct((1, Dm), F32)],
        scratch_shapes=[pltpu.VMEM((tm, Dm), F32), pltpu.VMEM((tm, Dm), F32), pltpu.VMEM((tm, Dm), F32),
                        pltpu.SemaphoreType.DMA((2,))],
        compiler_params=_params(("arbitrary", "arbitrary")))(*pieces, w, x, g, dres, *extra)


def _dw_pieces(a, pieces, name):
    C, M = a.shape
    width = pieces[0].shape[1]
    npc = len(pieces)
    tm = _pick(M, (1024, 512, 256, 128))
    tn = _pick(width, (1408, 1024, 512, 256, 128))
    tc = _pick(C, (2048, 1024, 512, 256, 128))
    per = width // tn
    nk = C // tc

    def body(*refs):
        a_ref = refs[0]
        p_refs = refs[1:1 + npc]
        o_ref, acc_ref = refs[-2:]
        j = pl.program_id(1)
        k = pl.program_id(2)

        @pl.when(k == 0)
        def _():
            acc_ref[...] = jnp.zeros_like(acc_ref)

        for p in range(npc):
            @pl.when((j >= p * per) & (j < (p + 1) * per))
            def _(p=p):
                acc_ref[...] += _dot(a_ref[...], p_refs[p][...], 'tn')

        @pl.when(k == nk - 1)
        def _():
            o_ref[...] = acc_ref[...].astype(o_ref.dtype)

    def piece_spec(p):
        return pl.BlockSpec((tc, tn), lambda i, j, k: (k, jnp.clip(j - p * per, 0, per - 1)))

    return pl.pallas_call(
        body, name=name, grid=(M // tm, per * npc, nk),
        in_specs=[pl.BlockSpec((tc, tm), lambda i, j, k: (k, i))] + [piece_spec(p) for p in range(npc)],
        out_specs=pl.BlockSpec((tm, tn), lambda i, j, k: (i, j)),
        out_shape=jax.ShapeDtypeStruct((M, width * npc), BF16), scratch_shapes=[pltpu.VMEM((tm, tn), F32)],
        compiler_params=_params(("parallel", "parallel", "arbitrary")))(a, *pieces)


def _mix_fwd(c_act, o_att, o_mem, proj, b_gate, x1, w_pw, w_o, w_mo, w_out, next_g):
    T, Dm = x1.shape
    W = c_act.shape[1]
    tm = _pick(T, (256, 128, 64, 32, 16, 8))

    def body(c_ref, oa_ref, om_ref, gl_ref, bg_ref, x1_ref, wpw_ref, wo_ref, wmo_ref, wout_ref, ng_ref,
             x2_ref, yc_ref, ya_ref, ym_ref, h_ref):
        yc = _dot(c_ref[...], wpw_ref[...])
        ya = _dot(oa_ref[...], wo_ref[...])
        ym = _dot(om_ref[...], wmo_ref[...])
        g = _sig(gl_ref[...].astype(F32) + bg_ref[...])
        y = g[:, :Dm] * yc + g[:, Dm:2 * Dm] * ya + g[:, 2 * Dm:] * ym
        x2 = x1_ref[...] + _dot(y, wout_ref[...])
        x2_ref[...] = x2
        r = lax.rsqrt(jnp.mean(x2 * x2, axis=-1, keepdims=True) + EPS)
        h_ref[...] = ((x2 * r) * ng_ref[...]).astype(BF16)
        yc_ref[...] = yc.astype(BF16)
        ya_ref[...] = ya.astype(BF16)
        ym_ref[...] = ym.astype(BF16)

    rowW = pl.BlockSpec((tm, W), lambda i: (i, 0))
    rowD = pl.BlockSpec((tm, Dm), lambda i: (i, 0))
    full = lambda s: pl.BlockSpec(s, lambda i: (0, 0))
    return pl.pallas_call(
        body, name="mix_fwd", grid=(T // tm,),
        in_specs=[rowW, rowW, rowW, pl.BlockSpec((tm, 3 * Dm), lambda i: (i, 1)), full((1, 3 * Dm)), rowD,
                  full((W, Dm)), full((W, Dm)), full((W, Dm)), full((Dm, Dm)), full((1, Dm))],
        out_specs=[rowD] * 5,
        out_shape=[jax.ShapeDtypeStruct((T, Dm), F32)] + [jax.ShapeDtypeStruct((T, Dm), BF16)] * 4,
        compiler_params=_params(("parallel",)))(c_act, o_att, o_mem, proj, b_gate, x1, w_pw, w_o, w_mo, w_out,
                                                next_g)


def _mix_bwd(dx2, yc, ya, ym, c_act, o_att, o_mem, proj, b_gate, w_pw, w_o, w_mo, w_out):
    T, Dm = dx2.shape
    W = w_pw.shape[0]
    tm = _pick(T, (256, 128, 64, 32, 16, 8))
    nt = T // tm

    def body(dx_ref, yc_ref, ya_ref, ym_ref, c_ref, oa_ref, om_ref, gl_ref, bg_ref, wpw_ref, wo_ref, wmo_ref,
             wout_ref, dgl_ref, dbg_ref, dc_ref, doa_ref, dom_ref, gpw_ref, go_ref, gmo_ref, gout_ref,
             apw, ao, amo, aout):
        i = pl.program_id(0)

        @pl.when(i == 0)
        def _():
            dbg_ref[...] = jnp.zeros_like(dbg_ref)
            for acc in (apw, ao, amo, aout):
                acc[...] = jnp.zeros_like(acc)

        dxv = dx_ref[...].astype(BF16)
        dy = _dot(dxv, wout_ref[...], 'nt')
        g = _sig(gl_ref[...].astype(F32) + bg_ref[...])
        branches = ((yc_ref, c_ref, wpw_ref, dc_ref, apw), (ya_ref, oa_ref, wo_ref, doa_ref, ao),
                    (ym_ref, om_ref, wmo_ref, dom_ref, amo))
        y = jnp.zeros((tm, Dm), F32)
        for n, (y_ref, in_ref, w_ref, dk_ref, acc) in enumerate(branches):
            gk = g[:, n * Dm:(n + 1) * Dm]
            yk = y_ref[...].astype(F32)
            dyk = dy * gk
            dgl = dyk * yk * (1.0 - gk)
            dgl_ref[:, n * Dm:(n + 1) * Dm] = dgl.astype(BF16)
            dbg_ref[:, n * Dm:(n + 1) * Dm] += jnp.sum(dgl, axis=0, keepdims=True)
            dyk = dyk.astype(BF16)
            dk_ref[...] = _dot(dyk, w_ref[...], 'nt').astype(BF16)
            acc[...] += _dot(in_ref[...], dyk, 'tn')
            y = y + gk * yk
        aout[...] += _dot(y, dxv, 'tn')

        @pl.when(i == nt - 1)
        def _():
            for acc, out in ((apw, gpw_ref), (ao, go_ref), (amo, gmo_ref), (aout, gout_ref)):
                out[...] = acc[...].astype(BF16)

    rowW = pl.BlockSpec((tm, W), lambda i: (i, 0))
    rowD = pl.BlockSpec((tm, Dm), lambda i: (i, 0))
    full = lambda s: pl.BlockSpec(s, lambda i: (0, 0))
    return pl.pallas_call(
        body, name="mix_bwd", grid=(nt,),
        in_specs=[rowD, rowD, rowD, rowD, rowW, rowW, rowW, pl.BlockSpec((tm, 3 * Dm), lambda i: (i, 1)),
                  full((1, 3 * Dm)), full((W, Dm)), full((W, Dm)), full((W, Dm)), full((Dm, Dm))],
        out_specs=[pl.BlockSpec((tm, 3 * Dm), lambda i: (i, 1)), full((1, 3 * Dm)), rowW, rowW, rowW,
                   full((W, Dm)), full((W, Dm)), full((W, Dm)), full((Dm, Dm))],
        out_shape=[jax.ShapeDtypeStruct((T, 6 * Dm), BF16), jax.ShapeDtypeStruct((1, 3 * Dm), F32)]
        + [jax.ShapeDtypeStruct((T, W), BF16)] * 3 + [jax.ShapeDtypeStruct((W, Dm), BF16)] * 3
        + [jax.ShapeDtypeStruct((Dm, Dm), BF16)],
        scratch_shapes=[pltpu.VMEM((W, Dm), F32)] * 3 + [pltpu.VMEM((Dm, Dm), F32)],
        compiler_params=_params(("arbitrary",)))(dx2, yc, ya, ym, c_act, o_att, o_mem, proj, b_gate, w_pw, w_o,
                                                 w_mo, w_out)


def _ln_swish(cv, lg, lb):
    mu = jnp.mean(cv, axis=-1, keepdims=True)
    xc = cv - mu
    r = lax.rsqrt(jnp.mean(xc * xc, axis=-1, keepdims=True) + EPS)
    n = xc * r
    l = n * lg + lb
    return r, n, l


def _shift_copies(src, r0, win, shifts):
    win[...] = src[pl.ds(r0, CONV_TILE + CONV_PAD + 8), :]
    for s in range(8):
        shifts[s] = win[s:s + CONV_TILE + CONV_PAD, :]


def _tap(shifts, d):
    return shifts[d % 8, d - d % 8:d - d % 8 + CONV_TILE, :]


def _conv_fwd(proj3, dw_w, dw_b, ln_g, ln_b):
    Bl, S, _ = proj3.shape
    C, K, TS, PAD = CONV_WIDTH, CONV_KERNEL, CONV_TILE, CONV_PAD
    nt = S // TS

    def body(u_ref, w_ref, b_ref, lg_ref, lb_ref, cv_ref, c_ref, vbuf, win, shifts):
        vbuf[0:PAD, :] = jnp.zeros((PAD, C), F32)
        vbuf[S + PAD:S + PAD + 8, :] = jnp.zeros((8, C), F32)

        def glu(t, carry):
            r0 = pl.multiple_of(t * TS, TS)
            u = u_ref[pl.ds(r0, TS), :].astype(F32)
            vbuf[pl.ds(PAD + r0, TS), :] = u[:, :C] * _sig(u[:, C:])
            return carry

        lax.fori_loop(0, nt, glu, 0)

        def conv(t, carry):
            r0 = pl.multiple_of(t * TS, TS)
            _shift_copies(vbuf, r0, win, shifts)
            acc = jnp.zeros((TS, C), F32)
            for j in range(K):
                acc = acc + w_ref[j:j + 1, :] * _tap(shifts, PAD - (K - 1) + j)
            cv = acc + b_ref[...]
            cv_ref[pl.ds(r0, TS), :] = cv
            _, _, l = _ln_swish(cv, lg_ref[...], lb_ref[...])
            c_ref[pl.ds(r0, TS), :] = (l * _sig(l)).astype(BF16)
            return carry

        lax.fori_loop(0, nt, conv, 0)

    vec = pl.BlockSpec((1, C), lambda b: (0, 0))
    return pl.pallas_call(
        body, name="conv_fwd", grid=(Bl,),
        in_specs=[pl.BlockSpec((None, S, 2 * C), lambda b: (b, 0, 0)), pl.BlockSpec((K, C), lambda b: (0, 0)),
                  vec, vec, vec],
        out_specs=[pl.BlockSpec((None, S, C), lambda b: (b, 0, 0))] * 2,
        out_shape=[jax.ShapeDtypeStruct((Bl, S, C), F32), jax.ShapeDtypeStruct((Bl, S, C), BF16)],
        scratch_shapes=[pltpu.VMEM((S + PAD + 8, C), F32), pltpu.VMEM((TS + PAD + 8, C), F32),
                        pltpu.VMEM((8, TS + PAD, C), F32)],
        compiler_params=_params(("parallel",)))(proj3, dw_w, dw_b, ln_g, ln_b)


def _conv_bwd(proj3, cv, dc, dw_w, ln_g, ln_b, dproj3):
    Bl, S, _ = proj3.shape
    C, K, TS, PAD = CONV_WIDTH, CONV_KERNEL, CONV_TILE, CONV_PAD
    nt = S // TS

    def body(u_ref, cv_ref, dc_ref, w_ref, lg_ref, lb_ref, through_ref, du_ref, dw_ref, db_ref, dlg_ref, dlb_ref,
             vbuf, gbuf, win, shifts, dwacc):
        b = pl.program_id(0)

        @pl.when(b == 0)
        def _():
            dw_ref[...] = jnp.zeros_like(dw_ref)
            db_ref[...] = jnp.zeros_like(db_ref)
            dlg_ref[...] = jnp.zeros_like(dlg_ref)
            dlb_ref[...] = jnp.zeros_like(dlb_ref)

        vbuf[0:PAD, :] = jnp.zeros((PAD, C), F32)
        vbuf[S + PAD:S + PAD + 8, :] = jnp.zeros((8, C), F32)
        gbuf[S:S + PAD + 8, :] = jnp.zeros((PAD + 8, C), F32)
        dwacc[...] = jnp.zeros_like(dwacc)

        def norm_bwd(t, carry):
            r0 = pl.multiple_of(t * TS, TS)
            u = u_ref[pl.ds(r0, TS), :].astype(F32)
            vbuf[pl.ds(PAD + r0, TS), :] = u[:, :C] * _sig(u[:, C:])
            r, n, l = _ln_swish(cv_ref[pl.ds(r0, TS), :], lg_ref[...], lb_ref[...])
            s = _sig(l)
            dl = dc_ref[pl.ds(r0, TS), :].astype(F32) * s * (1.0 + l * (1.0 - s))
            dlg_ref[...] += jnp.sum(dl * n, axis=0, keepdims=True)
            dlb_ref[...] += jnp.sum(dl, axis=0, keepdims=True)
            dn = dl * lg_ref[...]
            dcv = r * (dn - jnp.mean(dn, axis=-1, keepdims=True) - n * jnp.mean(dn * n, axis=-1, keepdims=True))
            gbuf[pl.ds(r0, TS), :] = dcv
            db_ref[...] += jnp.sum(dcv, axis=0, keepdims=True)
            return carry

        lax.fori_loop(0, nt, norm_bwd, 0)

        def conv_bwd(t, carry):
            r0 = pl.multiple_of(t * TS, TS)
            _shift_copies(gbuf, r0, win, shifts)
            dv = jnp.zeros((TS, C), F32)
            for j in range(K):
                dv = dv + w_ref[j:j + 1, :] * _tap(shifts, K - 1 - j)
            u = u_ref[pl.ds(r0, TS), :].astype(F32)
            a, g = u[:, :C], u[:, C:]
            s = _sig(g)
            du_ref[pl.ds(r0, TS), 0:C] = (dv * s).astype(BF16)
            du_ref[pl.ds(r0, TS), C:2 * C] = (dv * a * s * (1.0 - s)).astype(BF16)
            dcv = gbuf[pl.ds(r0, TS), :]
            _shift_copies(vbuf, r0, win, shifts)
            for j in range(K):
                prod = dcv * _tap(shifts, PAD - (K - 1) + j)
                dwacc[j] += jnp.sum(prod.reshape(TS // 8, 8, C), axis=0)
            return carry

        lax.fori_loop(0, nt, conv_bwd, 0)
        dw_ref[...] += jnp.sum(dwacc[...], axis=1)

    vec = pl.BlockSpec((1, C), lambda b: (0, 0))
    seq = lambda w: pl.BlockSpec((None, S, w), lambda b: (b, 0, 0))
    return pl.pallas_call(
        body, name="conv_bwd", grid=(Bl,),
        in_specs=[seq(2 * C), seq(C), seq(C), pl.BlockSpec((K, C), lambda b: (0, 0)), vec, vec, ANY],
        out_specs=[seq(2 * C), pl.BlockSpec((K, C), lambda b: (0, 0)), vec, vec, vec],
        out_shape=[jax.ShapeDtypeStruct(dproj3.shape, BF16), jax.ShapeDtypeStruct((K, C), F32)]
        + [jax.ShapeDtypeStruct((1, C), F32)] * 3,
        input_output_aliases={6: 0},
        scratch_shapes=[pltpu.VMEM((S + PAD + 8, C), F32), pltpu.VMEM((S + PAD + 8, C), F32),
                        pltpu.VMEM((TS + PAD + 8, C), F32), pltpu.VMEM((8, TS + PAD, C), F32),
                        pltpu.VMEM((K, 8, C), F32)],
        compiler_params=_params(("arbitrary",)))(proj3, cv, dc, dw_w, ln_g, ln_b, dproj3)


def _att_bias(rel_bias):
    H = rel_bias.shape[0]
    Wd = KW + QB
    c = jnp.arange(Wd)
    by_offset = rel_bias[:, jnp.clip(KW - c, -(CHUNK - 1), MAX_REL) + (CHUNK - 1)].reshape(H, 1, Wd)

    def body(t_ref, o_ref):
        rows = jnp.broadcast_to(t_ref[...], (QB, Wd))
        skew = pltpu.roll(rows, 0, 1, stride=1, stride_axis=0)[:, QB:]
        qi = lax.broadcasted_iota(jnp.int32, (QB, KW), 0)
        kj = lax.broadcasted_iota(jnp.int32, (QB, KW), 1)
        dchunk = ((KW - QB) + qi) // CHUNK - kj // CHUNK
        o_ref[...] = jnp.where((dchunk >= 0) & (dchunk <= LEFT_CHUNKS), skew, MASK_VALUE)

    return pl.pallas_call(
        body, name="att_bias", grid=(H,), in_specs=[pl.BlockSpec((None, 1, Wd), lambda h: (h, 0, 0))],
        out_specs=pl.BlockSpec((None, QB, KW), lambda h: (h, 0, 0)),
        out_shape=jax.ShapeDtypeStruct((H, QB, KW), F32), compiler_params=_params(("parallel",)))(by_offset)


def _head_masks():
    lane = lax.broadcasted_iota(jnp.int32, (1, 128), 1)
    return (lane < 64, lane >= 64)


def _att_probs(qh, k2, bias, valid):
    s = _dot(qh, k2, 'nt') * ATT_SCALE + bias
    s = jnp.where(valid, s, MASK_VALUE)
    e = jnp.exp(s - jnp.max(s, axis=-1, keepdims=True))
    return e * (1.0 / jnp.sum(e, axis=-1, keepdims=True))


def _att_specs(S, q_col):
    nb = S // QB
    q_spec = pl.BlockSpec((None, QB, ATT_WIDTH), lambda b, i: (b, jnp.minimum(i, nb - 1), q_col))

    def kv_spec(col, kb):
        return pl.BlockSpec((None, QB, ATT_WIDTH),
                            lambda b, i: (b, jnp.clip(i - 2 + kb, 0, nb - 1), col))

    return q_spec, [kv_spec(3, kb) for kb in range(3)], [kv_spec(4, kb) for kb in range(3)]


def _att_fwd(proj3, bias):
    Bl, S, _ = proj3.shape
    nb = S // QB
    q_spec, k_specs, v_specs = _att_specs(S, 2)

    def body(q_ref, k0, k1, k2r, v0, v1, v2r, bias_ref, o_ref):
        i = pl.program_id(1)
        masks = _head_masks()
        valid = lax.broadcasted_iota(jnp.int32, (QB, KW), 1) >= (2 - i) * QB
        for pr in range(ATT_HEADS // 2):
            ls = slice(128 * pr, 128 * (pr + 1))
            q2 = q_ref[:, ls]
            k2 = jnp.concatenate([k0[:, ls], k1[:, ls], k2r[:, ls]], axis=0)
            v2 = jnp.concatenate([v0[:, ls], v1[:, ls], v2r[:, ls]], axis=0)
            o2 = jnp.zeros((QB, 128), F32)
            for hh in range(2):
                p = _att_probs(jnp.where(masks[hh], q2, 0), k2, bias_ref[2 * pr + hh], valid)
                o2 = o2 + _dot(p, jnp.where(masks[hh], v2, 0))
            o_ref[:, ls] = o2.astype(BF16)

    return pl.pallas_call(
        body, name="att_fwd", grid=(Bl, nb),
        in_specs=[q_spec] + k_specs + v_specs + [pl.BlockSpec((ATT_HEADS, QB, KW), lambda b, i: (0, 0, 0))],
        out_specs=pl.BlockSpec((None, QB, ATT_WIDTH), lambda b, i: (b, i, 0)),
        out_shape=jax.ShapeDtypeStruct((Bl, S, ATT_WIDTH), BF16),
        compiler_params=_params(("parallel", "arbitrary")))(*([proj3] * 7), bias)


def _att_bwd(proj3, do, bias, dproj3):
    Bl, S, _ = proj3.shape
    nb = S // QB
    q_spec, k_specs, v_specs = _att_specs(S, 2)
    do_spec = pl.BlockSpec((None, QB, ATT_WIDTH), lambda b, i: (b, jnp.minimum(i, nb - 1), 0))
    kv_out = pl.BlockSpec((None, QB, ATT_WIDTH), lambda b, i: (b, jnp.clip(i - 2, 0, nb - 1), 0))
    bias_spec = pl.BlockSpec((ATT_HEADS, QB, KW), lambda b, i: (0, 0, 0))

    def body(q_ref, k0, k1, k2r, v0, v1, v2r, do_ref, bias_ref, through_ref, dq_ref, dk_ref, dv_ref, db_ref,
             dkw, dvw):
        b = pl.program_id(0)
        i = pl.program_id(1)

        @pl.when((b == 0) & (i == 0))
        def _():
            db_ref[...] = jnp.zeros_like(db_ref)

        @pl.when(i == 0)
        def _():
            dkw[...] = jnp.zeros_like(dkw)
            dvw[...] = jnp.zeros_like(dvw)

        @pl.when(i < nb)
        def _():
            masks = _head_masks()
            valid = lax.broadcasted_iota(jnp.int32, (QB, KW), 1) >= (2 - i) * QB
            for pr in range(ATT_HEADS // 2):
                ls = slice(128 * pr, 128 * (pr + 1))
                q2 = q_ref[:, ls]
                do2 = do_ref[:, ls]
                k2 = jnp.concatenate([k0[:, ls], k1[:, ls], k2r[:, ls]], axis=0)
                v2 = jnp.concatenate([v0[:, ls], v1[:, ls], v2r[:, ls]], axis=0)
                dq2 = jnp.zeros((QB, 128), F32)
                dk2 = jnp.zeros((KW, 128), F32)
                dv2 = jnp.zeros((KW, 128), F32)
                for hh in range(2):
                    h = 2 * pr + hh
                    qh = jnp.where(masks[hh], q2, 0)
                    doh = jnp.where(masks[hh], do2, 0)
                    p = _att_probs(qh, k2, bias_ref[h], valid)
                    dp = _dot(doh, v2, 'nt')
                    ds = p * (dp - jnp.sum(p * dp, axis=-1, keepdims=True))
                    db_ref[h] += ds
                    dq2 = dq2 + _dot(ds, jnp.where(masks[hh], k2, 0))
                    dk2 = dk2 + _dot(ds, qh, 'tn')
                    dv2 = dv2 + _dot(p, doh, 'tn')
                dq_ref[:, ls] = (dq2 * ATT_SCALE).astype(BF16)
                dkw[:, ls] += dk2 * ATT_SCALE
                dvw[:, ls] += dv2

        dk_ref[...] = dkw[0:QB, :].astype(BF16)
        dv_ref[...] = dvw[0:QB, :].astype(BF16)
        for buf in (dkw, dvw):
            rest = buf[QB:KW, :]
            buf[0:KW - QB, :] = rest
            buf[KW - QB:KW, :] = jnp.zeros((QB, ATT_WIDTH), F32)

    blk = jax.ShapeDtypeStruct((Bl, S, ATT_WIDTH), BF16)
    return pl.pallas_call(
        body, name="att_bwd", grid=(Bl, nb + 2),
        in_specs=[q_spec] + k_specs + v_specs + [do_spec, bias_spec, ANY],
        out_specs=[q_spec, kv_out, kv_out, bias_spec],
        out_shape=[jax.ShapeDtypeStruct(dproj3.shape, BF16), blk, blk,
                   jax.ShapeDtypeStruct((ATT_HEADS, QB, KW), F32)],
        input_output_aliases={9: 0},
        scratch_shapes=[pltpu.VMEM((KW, ATT_WIDTH), F32), pltpu.VMEM((KW, ATT_WIDTH), F32)],
        compiler_params=_params(("arbitrary", "arbitrary")))(*([proj3] * 7), do, bias, dproj3)


def _rel_bias_grad(dbias):
    H = dbias.shape[0]
    Wd = KW + QB
    c = jnp.arange(Wd)[:, None] + 1
    bins = (jnp.clip(KW - c, -(CHUNK - 1), MAX_REL) + (CHUNK - 1) == jnp.arange(N_REL)[None, :]).astype(F32)

    def body(d_ref, bins_ref, o_ref, cols):
        rows = lax.broadcasted_iota(jnp.int32, (QB, QB), 0)
        lanes = lax.broadcasted_iota(jnp.int32, (QB, QB), 1)
        reverse = (rows + lanes == QB - 1).astype(F32)
        for h in range(H):
            flipped = jnp.dot(reverse, d_ref[h], preferred_element_type=F32, precision=lax.Precision.HIGHEST)
            wide = jnp.concatenate([flipped, jnp.zeros((QB, QB), F32)], axis=1)
            skew = pltpu.roll(wide, 0, 1, stride=1, stride_axis=0)
            cols[h:h + 1, :] = jnp.sum(skew, axis=0, keepdims=True)
        o_ref[...] = jnp.dot(cols[...], bins_ref[...], preferred_element_type=F32, precision=lax.Precision.HIGHEST)

    return pl.pallas_call(
        body, name="rel_bias_grad", grid=(1,),
        in_specs=[pl.BlockSpec((H, QB, KW), lambda i: (0, 0, 0)), pl.BlockSpec((Wd, N_REL), lambda i: (0, 0))],
        out_specs=pl.BlockSpec((H, N_REL), lambda i: (0, 0)), out_shape=jax.ShapeDtypeStruct((H, N_REL), F32),
        scratch_shapes=[pltpu.VMEM((H, Wd), F32)],
        compiler_params=_params(("arbitrary",)))(dbias, bins)


MEM_TILE = 512


def _mem_probs(qh, kh):
    s = _dot(qh, kh, 'nt') * MEM_SCALE
    e = jnp.exp(s - jnp.max(s, axis=-1, keepdims=True))
    return e * (1.0 / jnp.sum(e, axis=-1, keepdims=True))


def _mem_fwd(proj3, kv3):
    Bl, S, _ = proj3.shape
    tq = _pick(S, (MEM_TILE, 256))
    hd = MEM_HEAD_DIM

    def body(q_ref, kv_ref, o_ref):
        for h in range(MEM_HEADS):
            p = _mem_probs(q_ref[:, h * hd:(h + 1) * hd], kv_ref[:, h * hd:(h + 1) * hd])
            o_ref[:, h * hd:(h + 1) * hd] = _dot(p, kv_ref[:, MEM_WIDTH + h * hd:MEM_WIDTH + (h + 1) * hd]).astype(BF16)

    return pl.pallas_call(
        body, name="mem_fwd", grid=(Bl, S // tq),
        in_specs=[pl.BlockSpec((None, tq, MEM_WIDTH), lambda b, i: (b, i, 5)),
                  pl.BlockSpec((None, MEM_LEN, 2 * MEM_WIDTH), lambda b, i: (b, 0, 0))],
        out_specs=pl.BlockSpec((None, tq, MEM_WIDTH), lambda b, i: (b, i, 0)),
        out_shape=jax.ShapeDtypeStruct((Bl, S, MEM_WIDTH), BF16),
        compiler_params=_params(("parallel", "parallel")))(proj3, kv3)


def _mem_bwd(proj3, kv3, do, dproj3):
    Bl, S, _ = proj3.shape
    tq = _pick(S, (MEM_TILE, 256))
    hd = MEM_HEAD_DIM

    def body(q_ref, kv_ref, do_ref, through_ref, dq_ref, dkv_ref):
        i = pl.program_id(1)

        @pl.when(i == 0)
        def _():
            dkv_ref[...] = jnp.zeros_like(dkv_ref)

        for h in range(MEM_HEADS):
            ks = slice(h * hd, (h + 1) * hd)
            vs = slice(MEM_WIDTH + h * hd, MEM_WIDTH + (h + 1) * hd)
            qh, kh, vh, doh = q_ref[:, ks], kv_ref[:, ks], kv_ref[:, vs], do_ref[:, ks]
            p = _mem_probs(qh, kh)
            dp = _dot(doh, vh, 'nt')
            ds = p * (dp - jnp.sum(p * dp, axis=-1, keepdims=True))
            dq_ref[:, ks] = (_dot(ds, kh) * MEM_SCALE).astype(BF16)
            dkv_ref[:, ks] += _dot(ds, qh, 'tn') * MEM_SCALE
            dkv_ref[:, vs] += _dot(p, doh, 'tn')

    return pl.pallas_call(
        body, name="mem_bwd", grid=(Bl, S // tq),
        in_specs=[pl.BlockSpec((None, tq, MEM_WIDTH), lambda b, i: (b, i, 5)),
                  pl.BlockSpec((None, MEM_LEN, 2 * MEM_WIDTH), lambda b, i: (b, 0, 0)),
                  pl.BlockSpec((None, tq, MEM_WIDTH), lambda b, i: (b, i, 0)), ANY],
        out_specs=[pl.BlockSpec((None, tq, MEM_WIDTH), lambda b, i: (b, i, 5)),
                   pl.BlockSpec((None, MEM_LEN, 2 * MEM_WIDTH), lambda b, i: (b, 0, 0))],
        out_shape=[jax.ShapeDtypeStruct(dproj3.shape, BF16),
                   jax.ShapeDtypeStruct((Bl, MEM_LEN, 2 * MEM_WIDTH), F32)],
        input_output_aliases={3: 0},
        compiler_params=_params(("parallel", "arbitrary")))(proj3, kv3, do, dproj3)


def _position():
    x, y, c = lax.axis_index("x"), lax.axis_index("y"), lax.axis_index("c")
    return x, y, c, 4 * x + 2 * y + c


def _device(idx):
    return ((idx >> 2) & 1, (idx >> 1) & 1, idx & 1)


def _half_block(ref, axis, shard_shape, k, h):
    R, Cn = shard_shape
    if axis == 1:
        return ref.at[pl.ds(h * (R // 2), R // 2), pl.ds(k * Cn, Cn)]
    return ref.at[pl.ds(k * R + h * (R // 2), R // 2), :]


def _block(ref, axis, shard_shape, k):
    R, Cn = shard_shape
    if axis == 1:
        return ref.at[:, pl.ds(k * Cn, Cn)]
    return ref.at[pl.ds(k * R, R), :]


def _half(ref, h):
    R = ref.shape[0]
    return ref.at[pl.ds(h * (R // 2), R // 2), :]


ANY = pl.BlockSpec(memory_space=pl.ANY)


HBM = pl.BlockSpec(memory_space=pltpu.HBM)
SEM = pl.BlockSpec(memory_space=pltpu.SEMAPHORE)
VMEM_WHOLE = pl.BlockSpec(memory_space=pltpu.VMEM)
EFFECT = pltpu.SideEffectType.DATAFLOW_SIDE_EFFECTING


def _in_hbm(a):
    return pltpu.with_memory_space_constraint(a, pltpu.HBM)


def _split_start(body, name, sources, lands, n_copies):
    n = len(sources)
    out_shape, out_specs = [], []
    for _ in range(n):
        out_shape += [pltpu.SemaphoreType.DMA((n_copies,)), pltpu.SemaphoreType.DMA((n_copies,))]
        out_specs += [SEM, SEM]
    out_shape += [pltpu.HBM(a.shape, a.dtype) for a in list(sources) + list(lands)]
    out_specs += [HBM] * (2 * n)
    out_shape.append(jax.ShapeDtypeStruct((8, 128), F32))
    out_specs.append(VMEM_WHOLE)

    def call_body(*refs):
        srcs, lnds = refs[:n], refs[n:2 * n]
        sems = refs[2 * n:4 * n]
        token = refs[-1]
        body(srcs, lnds, sems[0::2], sems[1::2])
        token[...] = jnp.zeros_like(token)

    res = pl.pallas_call(
        call_body, name=name, in_specs=[HBM] * (2 * n), out_specs=out_specs, out_shape=out_shape,
        input_output_aliases={i: 2 * n + i for i in range(2 * n)},
        compiler_params=pltpu.CompilerParams(has_side_effects=EFFECT))(
            *[_in_hbm(a) for a in list(sources) + list(lands)])
    pairs = [(res[2 * w], res[2 * w + 1], res[2 * n + w], res[3 * n + w]) for w in range(n)]
    return pairs, res[-1]


def _split_wait(body, name, pairs, after):
    n = len(pairs)

    def call_body(*refs):
        srcs, lnds = refs[:n], refs[n:2 * n]
        sems = refs[2 * n:4 * n]
        body(srcs, lnds, sems[0::2], sems[1::2])

    args = [_in_hbm(p[2]) for p in pairs] + [_in_hbm(p[3]) for p in pairs]
    for p in pairs:
        args += [p[0], p[1]]
    res = pl.pallas_call(
        call_body, name=name, in_specs=[HBM] * (2 * n) + [SEM] * (2 * n) + [ANY], out_specs=[HBM] * (2 * n),
        out_shape=[pltpu.HBM(a.shape, a.dtype) for a in args[:2 * n]],
        input_output_aliases={i: i for i in range(2 * n)},
        compiler_params=pltpu.CompilerParams(has_side_effects=EFFECT))(*args, after)
    return res[:n], res[n:]


def _place_block(shard, axis, chip_idx, name, dtype, after=None):
    R, Cn = shard.shape
    tr = _pick(R, (256, 176, 128, 64, 32, 16, 8))
    nblk = R // tr

    def body(k_ref, s_ref, *rest):
        v = s_ref[...].astype(dtype)
        rest[-2][...] = v
        rest[-1][...] = v

    if axis == 1:
        out_shape, out_index = (R, 4 * Cn), lambda i, k: (i, k[0])
    else:
        out_shape, out_index = (4 * R, Cn), lambda i, k: (k[0] * nblk + i, 0)
    extra = () if after is None else (after,)
    return pl.pallas_call(
        body, name=name,
        grid_spec=pltpu.PrefetchScalarGridSpec(
            num_scalar_prefetch=1, grid=(nblk,),
            in_specs=[pl.BlockSpec((tr, Cn), lambda i, k: (i, 0))] + [ANY] * len(extra),
            out_specs=[pl.BlockSpec((tr, Cn), lambda i, k: (i, 0)), pl.BlockSpec((tr, Cn), out_index)]),
        out_shape=[jax.ShapeDtypeStruct((R, Cn), dtype), jax.ShapeDtypeStruct(out_shape, dtype)],
        compiler_params=_params(("parallel",)))(chip_idx, shard, *extra)


def _gather_copy(srcs, lnds, send, recv, axes, shapes, w, j, me):
    chip = me >> 1
    return (pltpu.make_async_remote_copy(
        src_ref=srcs[w], dst_ref=_block(lnds[w], axes[w], shapes[w], chip), send_sem=send[w].at[j],
        recv_sem=recv[w].at[j], device_id=_device(me ^ (2 * (j + 1))), device_id_type=MESH),
            pltpu.make_async_remote_copy(
        src_ref=srcs[w], dst_ref=_block(lnds[w], axes[w], shapes[w], chip ^ (j + 1)), send_sem=send[w].at[j],
        recv_sem=recv[w].at[j], device_id=_device(me ^ (2 * (j + 1))), device_id_type=MESH))


def _gather_start(shards, lands, axes, name):
    shapes = [s.shape for s in shards]

    def body(srcs, lnds, send, recv):
        x, y, c, me = _position()
        for w in range(len(shards)):
            for j in range(3):
                _gather_copy(srcs, lnds, send, recv, axes, shapes, w, j, me)[0].start()

    return _split_start(body, name, shards, lands, 3)


def _gather_wait(pairs, axes, after, name):
    shapes = [p[2].shape for p in pairs]

    def body(srcs, lnds, send, recv):
        x, y, c, me = _position()
        for w in range(len(pairs)):
            for j in range(3):
                sent, landed = _gather_copy(srcs, lnds, send, recv, axes, shapes, w, j, me)
                sent.wait_send()
                landed.wait_recv()

    return _split_wait(body, name, pairs, after)[1]


def _shard_shape(grad, axis):
    return (grad.shape[0], grad.shape[1] // 4) if axis == 1 else (grad.shape[0] // 4, grad.shape[1])


def _scatter_copy(srcs, lnds, send, recv, axes, shapes, w, m, me):
    peer = me ^ m
    return pltpu.make_async_remote_copy(
        src_ref=_half_block(srcs[w], axes[w], shapes[w], peer >> 1, peer & 1), dst_ref=lnds[w].at[m - 1],
        send_sem=send[w].at[m - 1], recv_sem=recv[w].at[m - 1], device_id=_device(peer), device_id_type=MESH)


def _scatter_start(grads, axes, name):
    shapes = [_shard_shape(g, a) for g, a in zip(grads, axes)]
    lands = [lax.empty((N_DEV - 1, R // 2, Cn), BF16) for R, Cn in shapes]

    def body(srcs, lnds, send, recv):
        x, y, c, me = _position()
        for w in range(len(grads)):
            for m in range(1, N_DEV):
                _scatter_copy(srcs, lnds, send, recv, axes, shapes, w, m, me).start()

    return _split_start(body, name, grads, lands, N_DEV - 1)


def _scatter_wait(pairs, axes, after):
    shapes = [_shard_shape(p[2], a) for p, a in zip(pairs, axes)]

    def body(srcs, lnds, send, recv):
        x, y, c, me = _position()
        for w in range(len(pairs)):
            for m in range(1, N_DEV):
                cp = _scatter_copy(srcs, lnds, send, recv, axes, shapes, w, m, me)
                cp.wait_send()
                cp.wait_recv()

    return _split_wait(body, "scatter_wait", pairs, after)


def _sum_partials(own, parts, half, name):
    R, Cn = own.shape
    tr = _pick(R, (256, 176, 128, 64, 32, 16, 8))
    nblk = R // tr

    def body(half_ref, own_ref, p_ref, o_ref):
        acc = own_ref[...].astype(F32)
        for d in range(N_DEV - 1):
            acc = acc + p_ref[d].astype(F32)
        o_ref[...] = acc

    return pl.pallas_call(
        body, name=name,
        grid_spec=pltpu.PrefetchScalarGridSpec(
            num_scalar_prefetch=1, grid=(nblk,),
            in_specs=[pl.BlockSpec((tr, Cn), lambda i, hr: (i, 0)),
                      pl.BlockSpec((N_DEV - 1, tr, Cn), lambda i, hr: (0, i, 0))],
            out_specs=pl.BlockSpec((tr, Cn), lambda i, hr: (hr[0] * nblk + i, 0))),
        out_shape=jax.ShapeDtypeStruct((2 * R, Cn), F32),
        compiler_params=_params(("parallel",)))(half, own, parts)


def _exchange_halves(grads):
    n = len(grads)

    def body(*refs):
        outs = refs[n:2 * n]
        send, recv = refs[2 * n:]
        x, y, c, me = _position()

        def copy(w, half):
            rows = _half(outs[w], half)
            return pltpu.make_async_remote_copy(src_ref=rows, dst_ref=rows, send_sem=send.at[w],
                                                recv_sem=recv.at[w], device_id=_device(me ^ 1), device_id_type=MESH)

        for w in range(n):
            copy(w, c).start()
        for w in range(n):
            copy(w, 1 - c).wait_recv()
        for w in range(n):
            copy(w, c).wait_send()

    return pl.pallas_call(
        body, name="exchange_halves", in_specs=[ANY] * n, out_specs=[ANY] * n,
        out_shape=[jax.ShapeDtypeStruct(a.shape, a.dtype) for a in grads],
        input_output_aliases={i: i for i in range(n)},
        scratch_shapes=[pltpu.SemaphoreType.DMA((n,)), pltpu.SemaphoreType.DMA((n,))],
        compiler_params=pltpu.CompilerParams(has_side_effects=True))(*grads)


def _all_reduce_small(vec):
    R, L = vec.shape

    def body(v_ref, o_ref, buf, send, recv):
        x, y, c, me = _position()
        buf[me] = v_ref[...]

        def copy(m, slot):
            return pltpu.make_async_remote_copy(src_ref=v_ref, dst_ref=buf.at[slot], send_sem=send.at[m - 1],
                                                recv_sem=recv.at[m - 1], device_id=_device(me ^ m),
                                                device_id_type=MESH)

        for m in range(1, N_DEV):
            copy(m, me).start()
        for m in range(1, N_DEV):
            copy(m, me ^ m).wait_recv()
        for m in range(1, N_DEV):
            copy(m, me).wait_send()
        acc = buf[0]
        for d in range(1, N_DEV):
            acc = acc + buf[d]
        o_ref[...] = acc

    vm = pl.BlockSpec(memory_space=pltpu.VMEM)
    return pl.pallas_call(
        body, name="all_reduce_small", in_specs=[vm], out_specs=vm, out_shape=jax.ShapeDtypeStruct((R, L), F32),
        scratch_shapes=[pltpu.VMEM((N_DEV, R, L), F32), pltpu.SemaphoreType.DMA((N_DEV - 1,)),
                        pltpu.SemaphoreType.DMA((N_DEV - 1,))],
        compiler_params=pltpu.CompilerParams(has_side_effects=True))(vec)


def _adamw(w, g, m, v, name):
    R, Cn = w.shape
    tr = _pick(R, (256, 176, 128, 64, 40, 32, 16, 8))

    def body(w_ref, g_ref, m_ref, v_ref, d_ref, nm_ref, nv_ref):
        gv = g_ref[...]
        nm = ADAM_B1 * m_ref[...] + (1.0 - ADAM_B1) * gv
        nv = ADAM_B2 * v_ref[...] + (1.0 - ADAM_B2) * (gv * gv)
        m_hat = nm / (1.0 - ADAM_B1 ** ADAM_STEP)
        v_hat = nv / (1.0 - ADAM_B2 ** ADAM_STEP)
        d_ref[...] = -ADAM_LR * (m_hat / (jnp.sqrt(v_hat) + ADAM_EPS) + ADAM_WD * w_ref[...])
        nm_ref[...] = nm
        nv_ref[...] = nv

    spec = pl.BlockSpec((tr, Cn), lambda i: (i, 0))
    return pl.pallas_call(
        body, name=name, grid=(R // tr,), in_specs=[spec] * 4, out_specs=[spec] * 3,
        out_shape=[jax.ShapeDtypeStruct((R, Cn), F32)] * 3, compiler_params=_params(("parallel",)))(w, g, m, v)


def _pack(arrays, rows):
    flat = jnp.concatenate([a.reshape(-1).astype(F32) for a in arrays])
    return jnp.pad(flat, (0, rows * 128 - flat.shape[0])).reshape(rows, 128)


def _unpack(packed, shapes):
    flat = packed.reshape(-1)
    out, off = [], 0
    for s in shapes:
        size = 1
        for d in s:
            size *= d
        out.append(flat[off:off + size].reshape(s))
        off += size
    return out


def _ffn_fwd(x, h, arrived, tag, next_g=None):
    w_up = arrived(f"{tag}_w_up", h)
    a, b, act = _ffn_up(h, w_up, f"{tag}_up")
    w_down = arrived(f"{tag}_w_down", act)
    out, h_next = _ffn_down(act, w_down, x, f"{tag}_down", next_g)
    return out, h_next, (h, a, b, act, w_up, w_down)


def _ffn_bwd(dout, x, norm, saved, tag, send):
    h, a, b, act, w_up, w_down = saved
    g_down = _mm(act, dout, 'tn', f"{tag}_down_dw", BF16, scale=0.5)
    token = send([f"{tag}_w_down"], [g_down])
    da, db = _ffn_down_bwd(dout, w_down, a, b, f"{tag}_down_dx", after=token)
    g_up = _dw_pieces(h, [da, db], f"{tag}_up_dw")
    token = send([f"{tag}_w_up"], [g_up])
    return _dx_rms_bwd([da, db], w_up, x, norm, dout, f"{tag}_up_dx", after=token)


def kernel(x, mem, ffn1_norm, ffn1_w_up, ffn1_w_down, mix_norm, mem_norm, w_in, b_gate, conv_dw_w, conv_dw_b, conv_ln_g, conv_ln_b, conv_w_pw, att_rel_bias, att_w_o, mem_w_kv, mem_w_o, w_out, ffn2_norm, ffn2_w_up, ffn2_w_down, final_norm, loss_target, m_ffn1_norm, m_ffn1_w_up, m_ffn1_w_down, m_mix_norm, m_mem_norm, m_w_in, m_b_gate, m_conv_dw_w, m_conv_dw_b, m_conv_ln_g, m_conv_ln_b, m_conv_w_pw, m_att_rel_bias, m_att_w_o, m_mem_w_kv, m_mem_w_o, m_w_out, m_ffn2_norm, m_ffn2_w_up, m_ffn2_w_down, m_final_norm, v_ffn1_norm, v_ffn1_w_up, v_ffn1_w_down, v_mix_norm, v_mem_norm, v_w_in, v_b_gate, v_conv_dw_w, v_conv_dw_b, v_conv_ln_g, v_conv_ln_b, v_conv_w_pw, v_att_rel_bias, v_att_w_o, v_mem_w_kv, v_mem_w_o, v_w_out, v_ffn2_norm, v_ffn2_w_up, v_ffn2_w_down, v_final_norm):
    given = dict(locals())
    wts = {n: given[n] for n in WEIGHTS}
    mom1 = {n: given["m_" + n] for n in WEIGHTS}
    mom2 = {n: given["v_" + n] for n in WEIGHTS}
    Bl, S, Dm = x.shape
    T = Bl * S
    x0 = x.reshape(T, Dm)
    tgt = loss_target.reshape(T, Dm)
    mem2 = mem.reshape(Bl * MEM_LEN, Dm)

    big_names = [n for n, _ in BIG]
    big_axes = [a for _, a in BIG]
    chip = 2 * lax.axis_index("x") + lax.axis_index("y")

    core = lax.axis_index("c")
    axis_of = dict(BIG)

    gather_groups = [['ffn1_w_up'], ['ffn1_w_down'], ['w_in', 'conv_dw_w'],
                     ['mem_w_kv', 'conv_w_pw', 'att_w_o', 'mem_w_o', 'w_out'], ['ffn2_w_up'], ['ffn2_w_down']]
    gather_names = [n for grp in gather_groups for n in grp]
    gather_axes = [axis_of.get(n, 1) for n in gather_names]
    raw = [jnp.pad(conv_dw_w[0], ((0, 1), (0, 0))) if n == 'conv_dw_w' else wts[n][0] for n in gather_names]
    kinds = [F32 if n == 'conv_dw_w' else BF16 for n in gather_names]
    chip_idx = chip.reshape(1).astype(jnp.int32)
    shard0, land0 = _place_block(raw[0], gather_axes[0], chip_idx, f"place_{gather_names[0]}", kinds[0])
    first, first_token = _gather_start([shard0], [land0], gather_axes[:1], "gather_start_first")
    placed = [_place_block(r, a, chip_idx, f"place_{n}", kd, after=first_token)
              for r, a, n, kd in zip(raw[1:], gather_axes[1:], gather_names[1:], kinds[1:])]
    rest, gather_token = _gather_start([p[0] for p in placed], [p[1] for p in placed], gather_axes[1:],
                                       "gather_start_rest")
    in_flight = dict(zip(gather_names, first + rest))
    full = {}

    def arrived(name, after):
        if name not in full:
            grp = next(grp for grp in gather_groups if name in grp)
            lands = _gather_wait([in_flight[n] for n in grp], [axis_of.get(n, 1) for n in grp], after,
                                 f"gather_wait_{grp[0]}")
            full.update(zip(grp, lands))
        return full[name]

    scattering = {}

    def send(names, grads):
        pairs, token = _scatter_start(grads, [axis_of[n] for n in names], f"scatter_start_{names[0]}")
        scattering.update(zip(names, pairs))
        return token

    final_g = final_norm.reshape(1, Dm)
    bias = _att_bias(att_rel_bias[0] + first_token[:1, :1])

    x1, h, ffn1_saved = _ffn_fwd(x0, _rms_fwd(x0, ffn1_norm, "ffn1_norm", after=gather_token), arrived, "ffn1",
                                 next_g=mix_norm)
    w_in_full = arrived('w_in', h)
    dw_full = full['conv_dw_w'][:CONV_KERNEL]
    proj = _mm(h, w_in_full, 'nn', "w_in", BF16)
    proj3 = proj.reshape(Bl, S, proj.shape[1])
    cv, c_act = _conv_fwd(proj3, dw_full, conv_dw_b, conv_ln_g, conv_ln_b)
    o_att = _att_fwd(proj3, bias)
    mem_h = _rms_fwd(mem2, mem_norm, "mem_norm")
    kv = _mm(mem_h, arrived('mem_w_kv', o_att), 'nn', "mem_kv", BF16)
    kv3 = kv.reshape(Bl, MEM_LEN, 2 * MEM_WIDTH)
    o_mem = _mem_fwd(proj3, kv3)
    c_act2, o_att2, o_mem2 = c_act.reshape(T, -1), o_att.reshape(T, -1), o_mem.reshape(T, -1)
    x2, yc, ya, ym, h2 = _mix_fwd(c_act2, o_att2, o_mem2, proj, b_gate, x1, full['conv_w_pw'], full['att_w_o'],
                                  full['mem_w_o'], full['w_out'], ffn2_norm)
    x3, _, ffn2_saved = _ffn_fwd(x2, h2, arrived, "ffn2")
    dx3, g_final, loss_vec = _final_fwd_bwd(x3, tgt, final_g)

    g = {}
    dx2, g['ffn2_norm'] = _ffn_bwd(dx3, x2, ffn2_norm, ffn2_saved, "ffn2", send)
    dgl, g['b_gate'], dc, doa, dom, g_pw, g_o, g_mo, g_out = _mix_bwd(
        dx2, yc, ya, ym, c_act2, o_att2, o_mem2, proj, b_gate, full['conv_w_pw'], full['att_w_o'],
        full['mem_w_o'], full['w_out'])
    token = send(['w_out', 'conv_w_pw', 'att_w_o', 'mem_w_o'], [g_out, g_pw, g_o, g_mo])
    dproj3 = dgl.reshape(Bl, S, -1)
    dproj3, g_dw, g['conv_dw_b'], g['conv_ln_g'], g['conv_ln_b'] = _conv_bwd(
        proj3, cv, dc.reshape(Bl, S, -1), dw_full, conv_ln_g, conv_ln_b, dproj3)
    dproj3, dk, dv, dbias = _att_bwd(proj3, doa.reshape(Bl, S, -1), bias, dproj3)
    g['att_rel_bias'] = _rel_bias_grad(dbias)
    dproj3, dkv = _mem_bwd(proj3, kv3, dom.reshape(Bl, S, -1), dproj3)
    dkv2 = dkv.reshape(Bl * MEM_LEN, 2 * MEM_WIDTH)
    g_kv = _mm(mem_h, dkv2, 'tn', "mem_kv_dw", BF16, after=token)
    dmem_h = _mm(dkv2, full['mem_w_kv'], 'nt', "mem_kv_dx", F32)
    _, g['mem_norm'] = _rms_bwd(mem2, mem_norm, dmem_h, dmem_h, "mem_norm_bwd")
    dkdv = jnp.concatenate([dk.reshape(T, -1), dv.reshape(T, -1)], axis=1)
    dproj = lax.dynamic_update_slice(dproj3.reshape(T, -1), dkdv, (0, 2 * CONV_WIDTH + ATT_WIDTH))
    token = send(['mem_w_kv', 'w_in'], [g_kv, _mm(h, dproj, 'tn', "w_in_dw", BF16)])
    dx1, g['mix_norm'] = _dx_rms_bwd([dproj], w_in_full, x1, mix_norm, dx2, "w_in_dx", after=token)
    dx0, g['ffn1_norm'] = _ffn_bwd(dx1, x0, ffn1_norm, ffn1_saved, "ffn1", send)
    g['final_norm'] = g_final

    sent, landed = _scatter_wait([scattering[n] for n in big_names], big_axes, dx0)
    halves = []
    half_idx = core.reshape(1).astype(jnp.int32)
    for n, a, own_full, parts in zip(big_names, big_axes, sent, landed):
        R, Cn = _shard_shape(own_full, a)
        start = (core * (R // 2), chip * Cn) if a == 1 else (chip * R + core * (R // 2), 0)
        own = lax.dynamic_slice(own_full, start, (R // 2, Cn))
        halves.append(_sum_partials(own, parts, half_idx, f"sum_{n}"))
    for n, sg in zip(big_names, _exchange_halves(halves)):
        g[n] = sg

    small_shapes = [wts[n].shape for n in SMALL]
    n_small = sum(int(wts[n].size) for n in SMALL)
    n_red = n_small + CONV_KERNEL * CONV_WIDTH + 1
    red = _all_reduce_small(_pack([g[n] for n in SMALL] + [g_dw, loss_vec[0, :1]], -(-n_red // 1024) * 8))
    red_list = _unpack(red, small_shapes + [(CONV_KERNEL, CONV_WIDTH), ()])
    for n, rg in zip(SMALL, red_list[:-2]):
        g[n] = rg
    loss = red_list[-1]
    dw_cols = conv_dw_w.shape[2]
    g['conv_dw_w'] = lax.dynamic_slice(red_list[-2], (0, chip * dw_cols), (CONV_KERNEL, dw_cols))[None]

    delta, new_m, new_v = {}, {}, {}
    for n in big_names:
        g[n] = g[n][None]
        d, nm, nv = _adamw(wts[n][0], g[n][0], mom1[n][0], mom2[n][0], f"adamw_{n}")
        delta[n], new_m[n], new_v[n] = d[None], nm[None], nv[None]
    rest = SMALL + ['conv_dw_w']
    rest_shapes = [wts[n].shape for n in rest]
    rows = -(-sum(int(wts[n].size) for n in rest) // 1024) * 8
    packed = [_pack([src[n] for n in rest], rows) for src in (wts, g, mom1, mom2)]
    for out, res in zip((delta, new_m, new_v), _adamw(*packed, "adamw_small")):
        for n, a in zip(rest, _unpack(res, rest_shapes)):
            out[n] = a

    grad_x = dx0.reshape(Bl, S, Dm)
    return (loss, grad_x, *[g[n] for n in WEIGHTS], *[delta[n] for n in WEIGHTS],
            *[new_m[n] for n in WEIGHTS], *[new_v[n] for n in WEIGHTS])
```

```python
import jax
import jax.numpy as jnp
from jax import lax
from jax.experimental import pallas as pl
from jax.experimental.pallas import tpu as pltpu

F32 = jnp.float32
BF16 = jnp.bfloat16

CHUNK = 64
LEFT_CHUNKS = 8
MAX_REL = 128
N_REL = (CHUNK - 1) + MAX_REL + 1
CONV_WIDTH = 512
CONV_KERNEL = 31
ATT_HEADS = 8
ATT_WIDTH = 512
MEM_LEN = 256
MEM_HEADS = 4
MEM_HEAD_DIM = 128
MEM_WIDTH = 512
EPS = 1e-6
MASK_VALUE = -1e30
ATT_SCALE = 64 ** -0.5
MEM_SCALE = 128 ** -0.5

ADAM_LR = 0.001
ADAM_B1 = 0.9
ADAM_B2 = 0.999
ADAM_EPS = 1e-08
ADAM_WD = 0.01
ADAM_STEP = 10

QB = 256
KW = 3 * QB
CONV_PAD = 32
CONV_TILE = 256

VMEM_LIMIT = 56 << 20
MXU_COLS = 256

WEIGHTS = ['ffn1_norm', 'ffn1_w_up', 'ffn1_w_down', 'mix_norm', 'mem_norm', 'w_in', 'b_gate', 'conv_dw_w',
           'conv_dw_b', 'conv_ln_g', 'conv_ln_b', 'conv_w_pw', 'att_rel_bias', 'att_w_o', 'mem_w_kv', 'mem_w_o',
           'w_out', 'ffn2_norm', 'ffn2_w_up', 'ffn2_w_down', 'final_norm']
BIG = [('ffn1_w_up', 1), ('ffn1_w_down', 0), ('w_in', 1), ('conv_w_pw', 1), ('att_w_o', 1), ('mem_w_kv', 0),
       ('mem_w_o', 1), ('w_out', 0), ('ffn2_w_up', 1), ('ffn2_w_down', 0)]
SMALL = ['ffn1_norm', 'mix_norm', 'mem_norm', 'b_gate', 'conv_dw_b', 'conv_ln_g', 'conv_ln_b', 'att_rel_bias',
         'ffn2_norm', 'final_norm']
N_DEV = 8
MESH = pl.DeviceIdType.MESH


def _pick(n, cands):
    for c in cands:
        if n % c == 0:
            return c
    return n


def _sig(x):
    return 0.5 * jnp.tanh(0.5 * x) + 0.5


def _params(sem=None, vmem=VMEM_LIMIT):
    return pltpu.CompilerParams(dimension_semantics=sem, vmem_limit_bytes=vmem)


def _dot(a, b, mode='nn'):
    dims = {'nn': (((1,), (0,)), ((), ())), 'nt': (((1,), (1,)), ((), ())), 'tn': (((0,), (0,)), ((), ()))}[mode]
    return lax.dot_general(a.astype(BF16), b.astype(BF16), dims, preferred_element_type=F32)


def _mm(a, b, mode, name, out_dtype, res=None, scale=1.0, after=None):
    if mode == 'nn':
        (M, C), (_, N) = a.shape, b.shape
    elif mode == 'nt':
        (M, C), (N, _) = a.shape, b.shape
    else:
        (C, M), (_, N) = a.shape, b.shape
    tm = _pick(M, (1024, 1408, 512, 256, 128))
    tn = _pick(N, (1024, 1408, 512, 256, 128))
    tc = C if C <= 2816 else _pick(C, (2048, 1024, 1408, 512, 256, 128))
    nk = C // tc
    if mode == 'nn':
        a_spec = pl.BlockSpec((tm, tc), lambda i, j, k: (i, k))
        b_spec = pl.BlockSpec((tc, tn), lambda i, j, k: (k, j))
    elif mode == 'nt':
        a_spec = pl.BlockSpec((tm, tc), lambda i, j, k: (i, k))
        b_spec = pl.BlockSpec((tn, tc), lambda i, j, k: (j, k))
    else:
        a_spec = pl.BlockSpec((tc, tm), lambda i, j, k: (k, i))
        b_spec = pl.BlockSpec((tc, tn), lambda i, j, k: (k, j))
    o_spec = pl.BlockSpec((tm, tn), lambda i, j, k: (i, j))
    has_res = res is not None
    has_after = after is not None

    def body(*refs):
        a_ref, b_ref = refs[:2]
        r_ref = refs[2] if has_res else None
        o_ref, acc_ref = refs[-2:]
        k = pl.program_id(2)

        def finish(acc):
            if scale != 1.0:
                acc = acc * scale
            if r_ref is not None:
                acc = r_ref[...] + acc
            o_ref[...] = acc.astype(o_ref.dtype)

        if nk == 1:
            finish(_dot(a_ref[...], b_ref[...], mode))
        else:
            @pl.when(k == 0)
            def _():
                acc_ref[...] = jnp.zeros_like(acc_ref)

            acc_ref[...] += _dot(a_ref[...], b_ref[...], mode)

            @pl.when(k == nk - 1)
            def _():
                finish(acc_ref[...])

    in_specs = [a_spec, b_spec] + ([o_spec] if has_res else []) + ([ANY] if has_after else [])
    args = (a, b) + ((res,) if has_res else ()) + ((after,) if has_after else ())
    acc_shape = (tm, tn) if nk > 1 else (8, 128)
    return pl.pallas_call(
        body, name=name, grid=(M // tm, N // tn, nk), in_specs=in_specs, out_specs=o_spec,
        out_shape=jax.ShapeDtypeStruct((M, N), out_dtype), scratch_shapes=[pltpu.VMEM(acc_shape, F32)],
        compiler_params=_params(("parallel", "parallel", "arbitrary")))(*args)


def _row_tile(T):
    return _pick(T, (512, 256, 128, 64, 32, 16, 8))


def _rms_fwd(x, g, name, after=None):
    T, Dm = x.shape
    tm = _row_tile(T)

    def body(x_ref, g_ref, *rest):
        o_ref = rest[-1]
        xv = x_ref[...]
        r = lax.rsqrt(jnp.mean(xv * xv, axis=-1, keepdims=True) + EPS)
        o_ref[...] = ((xv * r) * g_ref[...]).astype(o_ref.dtype)

    extra = () if after is None else (after,)
    return pl.pallas_call(
        body, name=name, grid=(T // tm,),
        in_specs=[pl.BlockSpec((tm, Dm), lambda i: (i, 0)), pl.BlockSpec((1, Dm), lambda i: (0, 0))]
        + [ANY] * len(extra),
        out_specs=pl.BlockSpec((tm, Dm), lambda i: (i, 0)), out_shape=jax.ShapeDtypeStruct((T, Dm), BF16),
        compiler_params=_params(("parallel",)))(x, g, *extra)


def _rms_bwd(x, g, dh, dres, name):
    T, Dm = x.shape
    tm = _row_tile(T)

    def body(x_ref, g_ref, dh_ref, dr_ref, dx_ref, dg_ref):
        i = pl.program_id(0)
        xv = x_ref[...]
        r = lax.rsqrt(jnp.mean(xv * xv, axis=-1, keepdims=True) + EPS)
        xr = xv * r
        dh_v = dh_ref[...].astype(F32)
        dyg = dh_v * g_ref[...]
        dx = r * (dyg - xr * jnp.mean(dyg * xr, axis=-1, keepdims=True))
        dx_ref[...] = dr_ref[...] + dx

        @pl.when(i == 0)
        def _():
            dg_ref[...] = jnp.zeros_like(dg_ref)

        dg_ref[...] += jnp.sum(dh_v * xr, axis=0, keepdims=True)

    row = pl.BlockSpec((tm, Dm), lambda i: (i, 0))
    vec = pl.BlockSpec((1, Dm), lambda i: (0, 0))
    return pl.pallas_call(
        body, name=name, grid=(T // tm,), in_specs=[row, vec, row, row], out_specs=[row, vec],
        out_shape=[jax.ShapeDtypeStruct((T, Dm), F32), jax.ShapeDtypeStruct((1, Dm), F32)],
        compiler_params=_params(("arbitrary",)))(x, g, dh, dres)


def _final_fwd_bwd(x3, tgt, g):
    T, Dm = x3.shape
    tm = _row_tile(T)

    def body(x_ref, t_ref, g_ref, dx_ref, dg_ref, loss_ref):
        i = pl.program_id(0)
        xv = x_ref[...]
        gg = g_ref[...]
        r = lax.rsqrt(jnp.mean(xv * xv, axis=-1, keepdims=True) + EPS)
        xr = xv * r
        err = xr * gg - t_ref[...]
        dout = err * (1.0 / Dm)
        dyg = dout * gg
        dx_ref[...] = r * (dyg - xr * jnp.mean(dyg * xr, axis=-1, keepdims=True))

        @pl.when(i == 0)
        def _():
            dg_ref[...] = jnp.zeros_like(dg_ref)
            loss_ref[...] = jnp.zeros_like(loss_ref)

        dg_ref[...] += jnp.sum(dout * xr, axis=0, keepdims=True)
        loss_ref[...] += jnp.zeros_like(loss_ref) + (0.5 / Dm) * jnp.sum(err * err)

    row = pl.BlockSpec((tm, Dm), lambda i: (i, 0))
    vec = pl.BlockSpec((1, Dm), lambda i: (0, 0))
    one = pl.BlockSpec((1, 128), lambda i: (0, 0))
    return pl.pallas_call(
        body, name="final_fwd_bwd", grid=(T // tm,), in_specs=[row, row, vec], out_specs=[row, vec, one],
        out_shape=[jax.ShapeDtypeStruct((T, Dm), F32), jax.ShapeDtypeStruct((1, Dm), F32),
                   jax.ShapeDtypeStruct((1, 128), F32)],
        compiler_params=_params(("arbitrary",)))(x3, tgt, g)


def _ffn_up(h, w_up, name):
    T, K = h.shape
    Fh = w_up.shape[1] // 2
    tm = _pick(T, (1024, 512, 256, 128))
    tn = _pick(Fh, (1408, 512, 256, 128))
    nj = Fh // tn

    def body(h_ref, wa_ref, wb_ref, a_ref, b_ref, act_ref):
        hv = h_ref[...]
        for c0 in range(0, tn, MXU_COLS):
            cs = slice(c0, min(c0 + MXU_COLS, tn))
            a = _dot(hv, wa_ref[:, cs])
            b = _dot(hv, wb_ref[:, cs])
            a_ref[:, cs] = a.astype(BF16)
            b_ref[:, cs] = b.astype(BF16)
            act_ref[:, cs] = (a * _sig(a) * b).astype(BF16)

    out = pl.BlockSpec((tm, tn), lambda i, j: (i, j))
    return pl.pallas_call(
        body, name=name, grid=(T // tm, nj),
        in_specs=[pl.BlockSpec((tm, K), lambda i, j: (i, 0)), pl.BlockSpec((K, tn), lambda i, j: (0, j)),
                  pl.BlockSpec((K, tn), lambda i, j: (0, j + nj))],
        out_specs=[out, out, out], out_shape=[jax.ShapeDtypeStruct((T, Fh), BF16)] * 3,
        compiler_params=_params(("parallel", "parallel")))(h, w_up, w_up)


def _ffn_down(act, w_down, x, name, next_g=None):
    T, Fh = act.shape
    Dm = w_down.shape[1]
    tm = _pick(T, (1024, 512, 256, 128))
    rows = _pick(tm, (256, 128))
    with_norm = next_g is not None

    def body(a_ref, w_ref, x_ref, *rest):
        o_ref = rest[-2] if with_norm else rest[-1]
        o_ref[...] = x_ref[...] + 0.5 * _dot(a_ref[...], w_ref[...])
        if with_norm:
            g_ref, h_ref = rest[0], rest[-1]

            def chunk(c, carry):
                rs = pl.ds(pl.multiple_of(c * rows, rows), rows)
                ov = o_ref[rs, :]
                r = lax.rsqrt(jnp.mean(ov * ov, axis=-1, keepdims=True) + EPS)
                h_ref[rs, :] = ((ov * r) * g_ref[...]).astype(BF16)
                return carry

            lax.fori_loop(0, tm // rows, chunk, 0)

    row = pl.BlockSpec((tm, Dm), lambda i: (i, 0))
    in_specs = [pl.BlockSpec((tm, Fh), lambda i: (i, 0)), pl.BlockSpec((Fh, Dm), lambda i: (0, 0)), row]
    out_specs, out_shape = [row], [jax.ShapeDtypeStruct((T, Dm), F32)]
    args = (act, w_down, x)
    if with_norm:
        in_specs.append(pl.BlockSpec((1, Dm), lambda i: (0, 0)))
        out_specs.append(row)
        out_shape.append(jax.ShapeDtypeStruct((T, Dm), BF16))
        args += (next_g,)
    res = pl.pallas_call(body, name=name, grid=(T // tm,), in_specs=in_specs, out_specs=out_specs,
                         out_shape=out_shape, compiler_params=_params(("parallel",)))(*args)
    return (res[0], res[1]) if with_norm else (res[0], None)


def _ffn_down_bwd(dout, w_down, a, b, name, after=None):
    T, Dm = dout.shape
    Fh = w_down.shape[0]
    tm = _pick(T, (1024, 512, 256, 128))
    tn = _pick(Fh, (1408, 512, 256, 128))

    def body(d_ref, w_ref, a_ref, b_ref, *rest):
        da_ref, db_ref = rest[-2:]
        dv = (d_ref[...] * 0.5).astype(BF16)
        for c0 in range(0, tn, MXU_COLS):
            cs = slice(c0, min(c0 + MXU_COLS, tn))
            dact = _dot(dv, w_ref[cs, :], 'nt')
            av = a_ref[:, cs].astype(F32)
            bv = b_ref[:, cs].astype(F32)
            s = _sig(av)
            da_ref[:, cs] = (dact * bv * s * (1.0 + av * (1.0 - s))).astype(BF16)
            db_ref[:, cs] = (dact * av * s).astype(BF16)

    tile = pl.BlockSpec((tm, tn), lambda i, j: (i, j))
    extra = () if after is None else (after,)
    return pl.pallas_call(
        body, name=name, grid=(T // tm, Fh // tn),
        in_specs=[pl.BlockSpec((tm, Dm), lambda i, j: (i, 0)), pl.BlockSpec((tn, Dm), lambda i, j: (j, 0)),
                  tile, tile] + [ANY] * len(extra),
        out_specs=[tile, tile], out_shape=[jax.ShapeDtypeStruct((T, Fh), BF16)] * 2,
        compiler_params=_params(("parallel", "parallel")))(dout, w_down, a, b, *extra)


def _dx_rms_bwd(pieces, w, x, g, dres, name, after=None):
    T, Dm = x.shape
    width = pieces[0].shape[1]
    tm = _pick(T, (1024, 512, 256, 128))
    tc = _pick(width, (1408, 2048, 1024, 512, 256, 128))
    per = width // tc
    nk = per * len(pieces)
    npc = len(pieces)
    rows = _pick(tm, (256, 128))

    def body(*refs):
        p_refs = refs[:npc]
        w_ref, x_hbm, g_ref, dr_hbm = refs[npc:npc + 4]
        dx_ref, dg_ref, acc_ref, x_buf, dr_buf, sems = refs[-6:]
        i = pl.program_id(0)
        k = pl.program_id(1)
        tile = pl.ds(pl.multiple_of(i * tm, tm), tm)
        fetch_x = pltpu.make_async_copy(x_hbm.at[tile, :], x_buf, sems.at[0])
        fetch_dr = pltpu.make_async_copy(dr_hbm.at[tile, :], dr_buf, sems.at[1])

        @pl.when(k == 0)
        def _():
            fetch_x.start()
            fetch_dr.start()
            acc_ref[...] = jnp.zeros_like(acc_ref)

        @pl.when((i == 0) & (k == 0))
        def _():
            dg_ref[...] = jnp.zeros_like(dg_ref)

        for p in range(npc):
            @pl.when((k >= p * per) & (k < (p + 1) * per))
            def _(p=p):
                acc_ref[...] += _dot(p_refs[p][...], w_ref[...], 'nt')

        @pl.when(k == nk - 1)
        def _():
            fetch_x.wait()
            fetch_dr.wait()

            def chunk(c, carry):
                rs = pl.ds(pl.multiple_of(c * rows, rows), rows)
                dh = acc_ref[rs, :]
                xv = x_buf[rs, :]
                r = lax.rsqrt(jnp.mean(xv * xv, axis=-1, keepdims=True) + EPS)
                xr = xv * r
                dyg = dh * g_ref[...]
                dx_ref[rs, :] = dr_buf[rs, :] + r * (dyg - xr * jnp.mean(dyg * xr, axis=-1, keepdims=True))
                dg_ref[...] += jnp.sum(dh * xr, axis=0, keepdims=True)
                return carry

            lax.fori_loop(0, tm // rows, chunk, 0)

    def piece_spec(p):
        return pl.BlockSpec((tm, tc), lambda i, k: (i, jnp.clip(k - p * per, 0, per - 1)))

    row = pl.BlockSpec((tm, Dm), lambda i, k: (i, 0))
    vec = pl.BlockSpec((1, Dm), lambda i, k: (0, 0))
    extra = () if after is None else (after,)
    return pl.pallas_call(
        body, name=name, grid=(T // tm, nk),
        in_specs=[piece_spec(p) for p in range(npc)] + [pl.BlockSpec((Dm, tc), lambda i, k: (0, k)), ANY, vec, ANY]
        + [ANY] * len(extra),
        out_specs=[row, vec], out_shape=[jax.ShapeDtypeStruct((T, Dm), F32), jax.ShapeDtypeStruct((1, Dm), F32)],
        scratch_shapes=[pltpu.VMEM((tm, Dm), F32), pltpu.VMEM((tm, Dm), F32), pltpu.VMEM((tm, Dm), F32),
                        pltpu.SemaphoreType.DMA((2,))],
        compiler_params=_params(("arbitrary", "arbitrary")))(*pieces, w, x, g, dres, *extra)


def _dw_pieces(a, pieces, name):
    C, M = a.shape
    width = pieces[0].shape[1]
    npc = len(pieces)
    tm = _pick(M, (1024, 512, 256, 128))
    tn = _pick(width, (1408, 1024, 512, 256, 128))
    tc = _pick(C, (2048, 1024, 512, 256, 128))
    per = width // tn
    nk = C // tc

    def body(*refs):
        a_ref = refs[0]
        p_refs = refs[1:1 + npc]
        o_ref, acc_ref = refs[-2:]
        j = pl.program_id(1)
        k = pl.program_id(2)

        @pl.when(k == 0)
        def _():
            acc_ref[...] = jnp.zeros_like(acc_ref)

        for p in range(npc):
            @pl.when((j >= p * per) & (j < (p + 1) * per))
            def _(p=p):
                acc_ref[...] += _dot(a_ref[...], p_refs[p][...], 'tn')

        @pl.when(k == nk - 1)
        def _():
            o_ref[...] = acc_ref[...].astype(o_ref.dtype)

    def piece_spec(p):
        return pl.BlockSpec((tc, tn), lambda i, j, k: (k, jnp.clip(j - p * per, 0, per - 1)))

    return pl.pallas_call(
        body, name=name, grid=(M // tm, per * npc, nk),
        in_specs=[pl.BlockSpec((tc, tm), lambda i, j, k: (k, i))] + [piece_spec(p) for p in range(npc)],
        out_specs=pl.BlockSpec((tm, tn), lambda i, j, k: (i, j)),
        out_shape=jax.ShapeDtypeStruct((M, width * npc), BF16), scratch_shapes=[pltpu.VMEM((tm, tn), F32)],
        compiler_params=_params(("parallel", "parallel", "arbitrary")))(a, *pieces)


def _mix_fwd(c_act, o_att, o_mem, proj, b_gate, x1, w_pw, w_o, w_mo, w_out, next_g):
    T, Dm = x1.shape
    W = c_act.shape[1]
    tm = _pick(T, (256, 128, 64, 32, 16, 8))

    def body(c_ref, oa_ref, om_ref, gl_ref, bg_ref, x1_ref, wpw_ref, wo_ref, wmo_ref, wout_ref, ng_ref,
             x2_ref, yc_ref, ya_ref, ym_ref, h_ref):
        yc = _dot(c_ref[...], wpw_ref[...])
        ya = _dot(oa_ref[...], wo_ref[...])
        ym = _dot(om_ref[...], wmo_ref[...])
        g = _sig(gl_ref[...].astype(F32) + bg_ref[...])
        y = g[:, :Dm] * yc + g[:, Dm:2 * Dm] * ya + g[:, 2 * Dm:] * ym
        x2 = x1_ref[...] + _dot(y, wout_ref[...])
        x2_ref[...] = x2
        r = lax.rsqrt(jnp.mean(x2 * x2, axis=-1, keepdims=True) + EPS)
        h_ref[...] = ((x2 * r) * ng_ref[...]).astype(BF16)
        yc_ref[...] = yc.astype(BF16)
        ya_ref[...] = ya.astype(BF16)
        ym_ref[...] = ym.astype(BF16)

    rowW = pl.BlockSpec((tm, W), lambda i: (i, 0))
    rowD = pl.BlockSpec((tm, Dm), lambda i: (i, 0))
    full = lambda s: pl.BlockSpec(s, lambda i: (0, 0))
    return pl.pallas_call(
        body, name="mix_fwd", grid=(T // tm,),
        in_specs=[rowW, rowW, rowW, pl.BlockSpec((tm, 3 * Dm), lambda i: (i, 1)), full((1, 3 * Dm)), rowD,
                  full((W, Dm)), full((W, Dm)), full((W, Dm)), full((Dm, Dm)), full((1, Dm))],
        out_specs=[rowD] * 5,
        out_shape=[jax.ShapeDtypeStruct((T, Dm), F32)] + [jax.ShapeDtypeStruct((T, Dm), BF16)] * 4,
        compiler_params=_params(("parallel",)))(c_act, o_att, o_mem, proj, b_gate, x1, w_pw, w_o, w_mo, w_out,
                                                next_g)


def _mix_bwd(dx2, yc, ya, ym, c_act, o_att, o_mem, proj, b_gate, w_pw, w_o, w_mo, w_out):
    T, Dm = dx2.shape
    W = w_pw.shape[0]
    tm = _pick(T, (256, 128, 64, 32, 16, 8))
    nt = T // tm

    def body(dx_ref, yc_ref, ya_ref, ym_ref, c_ref, oa_ref, om_ref, gl_ref, bg_ref, wpw_ref, wo_ref, wmo_ref,
             wout_ref, dgl_ref, dbg_ref, dc_ref, doa_ref, dom_ref, gpw_ref, go_ref, gmo_ref, gout_ref,
             apw, ao, amo, aout):
        i = pl.program_id(0)

        @pl.when(i == 0)
        def _():
            dbg_ref[...] = jnp.zeros_like(dbg_ref)
            for acc in (apw, ao, amo, aout):
                acc[...] = jnp.zeros_like(acc)

        dxv = dx_ref[...].astype(BF16)
        dy = _dot(dxv, wout_ref[...], 'nt')
        g = _sig(gl_ref[...].astype(F32) + bg_ref[...])
        branches = ((yc_ref, c_ref, wpw_ref, dc_ref, apw), (ya_ref, oa_ref, wo_ref, doa_ref, ao),
                    (ym_ref, om_ref, wmo_ref, dom_ref, amo))
        y = jnp.zeros((tm, Dm), F32)
        for n, (y_ref, in_ref, w_ref, dk_ref, acc) in enumerate(branches):
            gk = g[:, n * Dm:(n + 1) * Dm]
            yk = y_ref[...].astype(F32)
            dyk = dy * gk
            dgl = dyk * yk * (1.0 - gk)
            dgl_ref[:, n * Dm:(n + 1) * Dm] = dgl.astype(BF16)
            dbg_ref[:, n * Dm:(n + 1) * Dm] += jnp.sum(dgl, axis=0, keepdims=True)
            dyk = dyk.astype(BF16)
            dk_ref[...] = _dot(dyk, w_ref[...], 'nt').astype(BF16)
            acc[...] += _dot(in_ref[...], dyk, 'tn')
            y = y + gk * yk
        aout[...] += _dot(y, dxv, 'tn')

        @pl.when(i == nt - 1)
        def _():
            for acc, out in ((apw, gpw_ref), (ao, go_ref), (amo, gmo_ref), (aout, gout_ref)):
                out[...] = acc[...].astype(BF16)

    rowW = pl.BlockSpec((tm, W), lambda i: (i, 0))
    rowD = pl.BlockSpec((tm, Dm), lambda i: (i, 0))
    full = lambda s: pl.BlockSpec(s, lambda i: (0, 0))
    return pl.pallas_call(
        body, name="mix_bwd", grid=(nt,),
        in_specs=[rowD, rowD, rowD, rowD, rowW, rowW, rowW, pl.BlockSpec((tm, 3 * Dm), lambda i: (i, 1)),
                  full((1, 3 * Dm)), full((W, Dm)), full((W, Dm)), full((W, Dm)), full((Dm, Dm))],
        out_specs=[pl.BlockSpec((tm, 3 * Dm), lambda i: (i, 1)), full((1, 3 * Dm)), rowW, rowW, rowW,
                   full((W, Dm)), full((W, Dm)), full((W, Dm)), full((Dm, Dm))],
        out_shape=[jax.ShapeDtypeStruct((T, 6 * Dm), BF16), jax.ShapeDtypeStruct((1, 3 * Dm), F32)]
        + [jax.ShapeDtypeStruct((T, W), BF16)] * 3 + [jax.ShapeDtypeStruct((W, Dm), BF16)] * 3
        + [jax.ShapeDtypeStruct((Dm, Dm), BF16)],
        scratch_shapes=[pltpu.VMEM((W, Dm), F32)] * 3 + [pltpu.VMEM((Dm, Dm), F32)],
        compiler_params=_params(("arbitrary",)))(dx2, yc, ya, ym, c_act, o_att, o_mem, proj, b_gate, w_pw, w_o,
                                                 w_mo, w_out)


def _ln_swish(cv, lg, lb):
    mu = jnp.mean(cv, axis=-1, keepdims=True)
    xc = cv - mu
    r = lax.rsqrt(jnp.mean(xc * xc, axis=-1, keepdims=True) + EPS)
    n = xc * r
    l = n * lg + lb
    return r, n, l


def _shift_copies(src, r0, win, shifts):
    win[...] = src[pl.ds(r0, CONV_TILE + CONV_PAD + 8), :]
    for s in range(8):
        shifts[s] = win[s:s + CONV_TILE + CONV_PAD, :]


def _tap(shifts, d):
    return shifts[d % 8, d - d % 8:d - d % 8 + CONV_TILE, :]


def _conv_fwd(proj3, dw_w, dw_b, ln_g, ln_b):
    Bl, S, _ = proj3.shape
    C, K, TS, PAD = CONV_WIDTH, CONV_KERNEL, CONV_TILE, CONV_PAD
    nt = S // TS

    def body(u_ref, w_ref, b_ref, lg_ref, lb_ref, cv_ref, c_ref, vbuf, win, shifts):
        vbuf[0:PAD, :] = jnp.zeros((PAD, C), F32)
        vbuf[S + PAD:S + PAD + 8, :] = jnp.zeros((8, C), F32)

        def glu(t, carry):
            r0 = pl.multiple_of(t * TS, TS)
            u = u_ref[pl.ds(r0, TS), :].astype(F32)
            vbuf[pl.ds(PAD + r0, TS), :] = u[:, :C] * _sig(u[:, C:])
            return carry

        lax.fori_loop(0, nt, glu, 0)

        def conv(t, carry):
            r0 = pl.multiple_of(t * TS, TS)
            _shift_copies(vbuf, r0, win, shifts)
            acc = jnp.zeros((TS, C), F32)
            for j in range(K):
                acc = acc + w_ref[j:j + 1, :] * _tap(shifts, PAD - (K - 1) + j)
            cv = acc + b_ref[...]
            cv_ref[pl.ds(r0, TS), :] = cv
            _, _, l = _ln_swish(cv, lg_ref[...], lb_ref[...])
            c_ref[pl.ds(r0, TS), :] = (l * _sig(l)).astype(BF16)
            return carry

        lax.fori_loop(0, nt, conv, 0)

    vec = pl.BlockSpec((1, C), lambda b: (0, 0))
    return pl.pallas_call(
        body, name="conv_fwd", grid=(Bl,),
        in_specs=[pl.BlockSpec((None, S, 2 * C), lambda b: (b, 0, 0)), pl.BlockSpec((K, C), lambda b: (0, 0)),
                  vec, vec, vec],
        out_specs=[pl.BlockSpec((None, S, C), lambda b: (b, 0, 0))] * 2,
        out_shape=[jax.ShapeDtypeStruct((Bl, S, C), F32), jax.ShapeDtypeStruct((Bl, S, C), BF16)],
        scratch_shapes=[pltpu.VMEM((S + PAD + 8, C), F32), pltpu.VMEM((TS + PAD + 8, C), F32),
                        pltpu.VMEM((8, TS + PAD, C), F32)],
        compiler_params=_params(("parallel",)))(proj3, dw_w, dw_b, ln_g, ln_b)


def _conv_bwd(proj3, cv, dc, dw_w, ln_g, ln_b, dproj3):
    Bl, S, _ = proj3.shape
    C, K, TS, PAD = CONV_WIDTH, CONV_KERNEL, CONV_TILE, CONV_PAD
    nt = S // TS

    def body(u_ref, cv_ref, dc_ref, w_ref, lg_ref, lb_ref, through_ref, du_ref, dw_ref, db_ref, dlg_ref, dlb_ref,
             vbuf, gbuf, win, shifts, dwacc):
        b = pl.program_id(0)

        @pl.when(b == 0)
        def _():
            dw_ref[...] = jnp.zeros_like(dw_ref)
            db_ref[...] = jnp.zeros_like(db_ref)
            dlg_ref[...] = jnp.zeros_like(dlg_ref)
            dlb_ref[...] = jnp.zeros_like(dlb_ref)

        vbuf[0:PAD, :] = jnp.zeros((PAD, C), F32)
        vbuf[S + PAD:S + PAD + 8, :] = jnp.zeros((8, C), F32)
        gbuf[S:S + PAD + 8, :] = jnp.zeros((PAD + 8, C), F32)
        dwacc[...] = jnp.zeros_like(dwacc)

        def norm_bwd(t, carry):
            r0 = pl.multiple_of(t * TS, TS)
            u = u_ref[pl.ds(r0, TS), :].astype(F32)
            vbuf[pl.ds(PAD + r0, TS), :] = u[:, :C] * _sig(u[:, C:])
            r, n, l = _ln_swish(cv_ref[pl.ds(r0, TS), :], lg_ref[...], lb_ref[...])
            s = _sig(l)
            dl = dc_ref[pl.ds(r0, TS), :].astype(F32) * s * (1.0 + l * (1.0 - s))
            dlg_ref[...] += jnp.sum(dl * n, axis=0, keepdims=True)
            dlb_ref[...] += jnp.sum(dl, axis=0, keepdims=True)
            dn = dl * lg_ref[...]
            dcv = r * (dn - jnp.mean(dn, axis=-1, keepdims=True) - n * jnp.mean(dn * n, axis=-1, keepdims=True))
            gbuf[pl.ds(r0, TS), :] = dcv
            db_ref[...] += jnp.sum(dcv, axis=0, keepdims=True)
            return carry

        lax.fori_loop(0, nt, norm_bwd, 0)

        def conv_bwd(t, carry):
            r0 = pl.multiple_of(t * TS, TS)
            _shift_copies(gbuf, r0, win, shifts)
            dv = jnp.zeros((TS, C), F32)
            for j in range(K):
                dv = dv + w_ref[j:j + 1, :] * _tap(shifts, K - 1 - j)
            u = u_ref[pl.ds(r0, TS), :].astype(F32)
            a, g = u[:, :C], u[:, C:]
            s = _sig(g)
            du_ref[pl.ds(r0, TS), 0:C] = (dv * s).astype(BF16)
            du_ref[pl.ds(r0, TS), C:2 * C] = (dv * a * s * (1.0 - s)).astype(BF16)
            dcv = gbuf[pl.ds(r0, TS), :]
            _shift_copies(vbuf, r0, win, shifts)
            for j in range(K):
                prod = dcv * _tap(shifts, PAD - (K - 1) + j)
                dwacc[j] += jnp.sum(prod.reshape(TS // 8, 8, C), axis=0)
            return carry

        lax.fori_loop(0, nt, conv_bwd, 0)
        dw_ref[...] += jnp.sum(dwacc[...], axis=1)

    vec = pl.BlockSpec((1, C), lambda b: (0, 0))
    seq = lambda w: pl.BlockSpec((None, S, w), lambda b: (b, 0, 0))
    return pl.pallas_call(
        body, name="conv_bwd", grid=(Bl,),
        in_specs=[seq(2 * C), seq(C), seq(C), pl.BlockSpec((K, C), lambda b: (0, 0)), vec, vec, ANY],
        out_specs=[seq(2 * C), pl.BlockSpec((K, C), lambda b: (0, 0)), vec, vec, vec],
        out_shape=[jax.ShapeDtypeStruct(dproj3.shape, BF16), jax.ShapeDtypeStruct((K, C), F32)]
        + [jax.ShapeDtypeStruct((1, C), F32)] * 3,
        input_output_aliases={6: 0},
        scratch_shapes=[pltpu.VMEM((S + PAD + 8, C), F32), pltpu.VMEM((S + PAD + 8, C), F32),
                        pltpu.VMEM((TS + PAD + 8, C), F32), pltpu.VMEM((8, TS + PAD, C), F32),
                        pltpu.VMEM((K, 8, C), F32)],
        compiler_params=_params(("arbitrary",)))(proj3, cv, dc, dw_w, ln_g, ln_b, dproj3)


def _att_bias(rel_bias):
    H = rel_bias.shape[0]
    Wd = KW + QB
    c = jnp.arange(Wd)
    by_offset = rel_bias[:, jnp.clip(KW - c, -(CHUNK - 1), MAX_REL) + (CHUNK - 1)].reshape(H, 1, Wd)

    def body(t_ref, o_ref):
        rows = jnp.broadcast_to(t_ref[...], (QB, Wd))
        skew = pltpu.roll(rows, 0, 1, stride=1, stride_axis=0)[:, QB:]
        qi = lax.broadcasted_iota(jnp.int32, (QB, KW), 0)
        kj = lax.broadcasted_iota(jnp.int32, (QB, KW), 1)
        dchunk = ((KW - QB) + qi) // CHUNK - kj // CHUNK
        o_ref[...] = jnp.where((dchunk >= 0) & (dchunk <= LEFT_CHUNKS), skew, MASK_VALUE)

    return pl.pallas_call(
        body, name="att_bias", grid=(H,), in_specs=[pl.BlockSpec((None, 1, Wd), lambda h: (h, 0, 0))],
        out_specs=pl.BlockSpec((None, QB, KW), lambda h: (h, 0, 0)),
        out_shape=jax.ShapeDtypeStruct((H, QB, KW), F32), compiler_params=_params(("parallel",)))(by_offset)


def _head_masks():
    lane = lax.broadcasted_iota(jnp.int32, (1, 128), 1)
    return (lane < 64, lane >= 64)


def _att_probs(qh, k2, bias, valid):
    s = _dot(qh, k2, 'nt') * ATT_SCALE + bias
    s = jnp.where(valid, s, MASK_VALUE)
    e = jnp.exp(s - jnp.max(s, axis=-1, keepdims=True))
    return e * (1.0 / jnp.sum(e, axis=-1, keepdims=True))


def _att_specs(S, q_col):
    nb = S // QB
    q_spec = pl.BlockSpec((None, QB, ATT_WIDTH), lambda b, i: (b, jnp.minimum(i, nb - 1), q_col))

    def kv_spec(col, kb):
        return pl.BlockSpec((None, QB, ATT_WIDTH),
                            lambda b, i: (b, jnp.clip(i - 2 + kb, 0, nb - 1), col))

    return q_spec, [kv_spec(3, kb) for kb in range(3)], [kv_spec(4, kb) for kb in range(3)]


def _att_fwd(proj3, bias):
    Bl, S, _ = proj3.shape
    nb = S // QB
    q_spec, k_specs, v_specs = _att_specs(S, 2)

    def body(q_ref, k0, k1, k2r, v0, v1, v2r, bias_ref, o_ref):
        i = pl.program_id(1)
        masks = _head_masks()
        valid = lax.broadcasted_iota(jnp.int32, (QB, KW), 1) >= (2 - i) * QB
        for pr in range(ATT_HEADS // 2):
            ls = slice(128 * pr, 128 * (pr + 1))
            q2 = q_ref[:, ls]
            k2 = jnp.concatenate([k0[:, ls], k1[:, ls], k2r[:, ls]], axis=0)
            v2 = jnp.concatenate([v0[:, ls], v1[:, ls], v2r[:, ls]], axis=0)
            o2 = jnp.zeros((QB, 128), F32)
            for hh in range(2):
                p = _att_probs(jnp.where(masks[hh], q2, 0), k2, bias_ref[2 * pr + hh], valid)
                o2 = o2 + _dot(p, jnp.where(masks[hh], v2, 0))
            o_ref[:, ls] = o2.astype(BF16)

    return pl.pallas_call(
        body, name="att_fwd", grid=(Bl, nb),
        in_specs=[q_spec] + k_specs + v_specs + [pl.BlockSpec((ATT_HEADS, QB, KW), lambda b, i: (0, 0, 0))],
        out_specs=pl.BlockSpec((None, QB, ATT_WIDTH), lambda b, i: (b, i, 0)),
        out_shape=jax.ShapeDtypeStruct((Bl, S, ATT_WIDTH), BF16),
        compiler_params=_params(("parallel", "arbitrary")))(*([proj3] * 7), bias)


def _att_bwd(proj3, do, bias, dproj3):
    Bl, S, _ = proj3.shape
    nb = S // QB
    q_spec, k_specs, v_specs = _att_specs(S, 2)
    do_spec = pl.BlockSpec((None, QB, ATT_WIDTH), lambda b, i: (b, jnp.minimum(i, nb - 1), 0))
    kv_out = pl.BlockSpec((None, QB, ATT_WIDTH), lambda b, i: (b, jnp.clip(i - 2, 0, nb - 1), 0))
    bias_spec = pl.BlockSpec((ATT_HEADS, QB, KW), lambda b, i: (0, 0, 0))

    def body(q_ref, k0, k1, k2r, v0, v1, v2r, do_ref, bias_ref, through_ref, dq_ref, dk_ref, dv_ref, db_ref,
             dkw, dvw):
        b = pl.program_id(0)
        i = pl.program_id(1)

        @pl.when((b == 0) & (i == 0))
        def _():
            db_ref[...] = jnp.zeros_like(db_ref)

        @pl.when(i == 0)
        def _():
            dkw[...] = jnp.zeros_like(dkw)
            dvw[...] = jnp.zeros_like(dvw)

        @pl.when(i < nb)
        def _():
            masks = _head_masks()
            valid = lax.broadcasted_iota(jnp.int32, (QB, KW), 1) >= (2 - i) * QB
            for pr in range(ATT_HEADS // 2):
                ls = slice(128 * pr, 128 * (pr + 1))
                q2 = q_ref[:, ls]
                do2 = do_ref[:, ls]
                k2 = jnp.concatenate([k0[:, ls], k1[:, ls], k2r[:, ls]], axis=0)
                v2 = jnp.concatenate([v0[:, ls], v1[:, ls], v2r[:, ls]], axis=0)
                dq2 = jnp.zeros((QB, 128), F32)
                dk2 = jnp.zeros((KW, 128), F32)
                dv2 = jnp.zeros((KW, 128), F32)
                for hh in range(2):
                    h = 2 * pr + hh
                    qh = jnp.where(masks[hh], q2, 0)
                    doh = jnp.where(masks[hh], do2, 0)
                    p = _att_probs(qh, k2, bias_ref[h], valid)
                    dp = _dot(doh, v2, 'nt')
                    ds = p * (dp - jnp.sum(p * dp, axis=-1, keepdims=True))
                    db_ref[h] += ds
                    dq2 = dq2 + _dot(ds, jnp.where(masks[hh], k2, 0))
                    dk2 = dk2 + _dot(ds, qh, 'tn')
                    dv2 = dv2 + _dot(p, doh, 'tn')
                dq_ref[:, ls] = (dq2 * ATT_SCALE).astype(BF16)
                dkw[:, ls] += dk2 * ATT_SCALE
                dvw[:, ls] += dv2

        dk_ref[...] = dkw[0:QB, :].astype(BF16)
        dv_ref[...] = dvw[0:QB, :].astype(BF16)
        for buf in (dkw, dvw):
            rest = buf[QB:KW, :]
            buf[0:KW - QB, :] = rest
            buf[KW - QB:KW, :] = jnp.zeros((QB, ATT_WIDTH), F32)

    blk = jax.ShapeDtypeStruct((Bl, S, ATT_WIDTH), BF16)
    return pl.pallas_call(
        body, name="att_bwd", grid=(Bl, nb + 2),
        in_specs=[q_spec] + k_specs + v_specs + [do_spec, bias_spec, ANY],
        out_specs=[q_spec, kv_out, kv_out, bias_spec],
        out_shape=[jax.ShapeDtypeStruct(dproj3.shape, BF16), blk, blk,
                   jax.ShapeDtypeStruct((ATT_HEADS, QB, KW), F32)],
        input_output_aliases={9: 0},
        scratch_shapes=[pltpu.VMEM((KW, ATT_WIDTH), F32), pltpu.VMEM((KW, ATT_WIDTH), F32)],
        compiler_params=_params(("arbitrary", "arbitrary")))(*([proj3] * 7), do, bias, dproj3)


def _rel_bias_grad(dbias):
    H = dbias.shape[0]
    Wd = KW + QB
    c = jnp.arange(Wd)[:, None] + 1
    bins = (jnp.clip(KW - c, -(CHUNK - 1), MAX_REL) + (CHUNK - 1) == jnp.arange(N_REL)[None, :]).astype(F32)

    def body(d_ref, bins_ref, o_ref, cols):
        rows = lax.broadcasted_iota(jnp.int32, (QB, QB), 0)
        lanes = lax.broadcasted_iota(jnp.int32, (QB, QB), 1)
        reverse = (rows + lanes == QB - 1).astype(F32)
        for h in range(H):
            flipped = jnp.dot(reverse, d_ref[h], preferred_element_type=F32, precision=lax.Precision.HIGHEST)
            wide = jnp.concatenate([flipped, jnp.zeros((QB, QB), F32)], axis=1)
            skew = pltpu.roll(wide, 0, 1, stride=1, stride_axis=0)
            cols[h:h + 1, :] = jnp.sum(skew, axis=0, keepdims=True)
        o_ref[...] = jnp.dot(cols[...], bins_ref[...], preferred_element_type=F32, precision=lax.Precision.HIGHEST)

    return pl.pallas_call(
        body, name="rel_bias_grad", grid=(1,),
        in_specs=[pl.BlockSpec((H, QB, KW), lambda i: (0, 0, 0)), pl.BlockSpec((Wd, N_REL), lambda i: (0, 0))],
        out_specs=pl.BlockSpec((H, N_REL), lambda i: (0, 0)), out_shape=jax.ShapeDtypeStruct((H, N_REL), F32),
        scratch_shapes=[pltpu.VMEM((H, Wd), F32)],
        compiler_params=_params(("arbitrary",)))(dbias, bins)


MEM_TILE = 512


def _mem_probs(qh, kh):
    s = _dot(qh, kh, 'nt') * MEM_SCALE
    e = jnp.exp(s - jnp.max(s, axis=-1, keepdims=True))
    return e * (1.0 / jnp.sum(e, axis=-1, keepdims=True))


def _mem_fwd(proj3, kv3):
    Bl, S, _ = proj3.shape
    tq = _pick(S, (MEM_TILE, 256))
    hd = MEM_HEAD_DIM

    def body(q_ref, kv_ref, o_ref):
        for h in range(MEM_HEADS):
            p = _mem_probs(q_ref[:, h * hd:(h + 1) * hd], kv_ref[:, h * hd:(h + 1) * hd])
            o_ref[:, h * hd:(h + 1) * hd] = _dot(p, kv_ref[:, MEM_WIDTH + h * hd:MEM_WIDTH + (h + 1) * hd]).astype(BF16)

    return pl.pallas_call(
        body, name="mem_fwd", grid=(Bl, S // tq),
        in_specs=[pl.BlockSpec((None, tq, MEM_WIDTH), lambda b, i: (b, i, 5)),
                  pl.BlockSpec((None, MEM_LEN, 2 * MEM_WIDTH), lambda b, i: (b, 0, 0))],
        out_specs=pl.BlockSpec((None, tq, MEM_WIDTH), lambda b, i: (b, i, 0)),
        out_shape=jax.ShapeDtypeStruct((Bl, S, MEM_WIDTH), BF16),
        compiler_params=_params(("parallel", "parallel")))(proj3, kv3)


def _mem_bwd(proj3, kv3, do, dproj3):
    Bl, S, _ = proj3.shape
    tq = _pick(S, (MEM_TILE, 256))
    hd = MEM_HEAD_DIM

    def body(q_ref, kv_ref, do_ref, through_ref, dq_ref, dkv_ref):
        i = pl.program_id(1)

        @pl.when(i == 0)
        def _():
            dkv_ref[...] = jnp.zeros_like(dkv_ref)

        for h in range(MEM_HEADS):
            ks = slice(h * hd, (h + 1) * hd)
            vs = slice(MEM_WIDTH + h * hd, MEM_WIDTH + (h + 1) * hd)
            qh, kh, vh, doh = q_ref[:, ks], kv_ref[:, ks], kv_ref[:, vs], do_ref[:, ks]
            p = _mem_probs(qh, kh)
            dp = _dot(doh, vh, 'nt')
            ds = p * (dp - jnp.sum(p * dp, axis=-1, keepdims=True))
            dq_ref[:, ks] = (_dot(ds, kh) * MEM_SCALE).astype(BF16)
            dkv_ref[:, ks] += _dot(ds, qh, 'tn') * MEM_SCALE
            dkv_ref[:, vs] += _dot(p, doh, 'tn')

    return pl.pallas_call(
        body, name="mem_bwd", grid=(Bl, S // tq),
        in_specs=[pl.BlockSpec((None, tq, MEM_WIDTH), lambda b, i: (b, i, 5)),
                  pl.BlockSpec((None, MEM_LEN, 2 * MEM_WIDTH), lambda b, i: (b, 0, 0)),
                  pl.BlockSpec((None, tq, MEM_WIDTH), lambda b, i: (b, i, 0)), ANY],
        out_specs=[pl.BlockSpec((None, tq, MEM_WIDTH), lambda b, i: (b, i, 5)),
                   pl.BlockSpec((None, MEM_LEN, 2 * MEM_WIDTH), lambda b, i: (b, 0, 0))],
        out_shape=[jax.ShapeDtypeStruct(dproj3.shape, BF16),
                   jax.ShapeDtypeStruct((Bl, MEM_LEN, 2 * MEM_WIDTH), F32)],
        input_output_aliases={3: 0},
        compiler_params=_params(("parallel", "arbitrary")))(proj3, kv3, do, dproj3)


def _position():
    x, y, c = lax.axis_index("x"), lax.axis_index("y"), lax.axis_index("c")
    return x, y, c, 4 * x + 2 * y + c


def _device(idx):
    return ((idx >> 2) & 1, (idx >> 1) & 1, idx & 1)


def _half_block(ref, axis, shard_shape, k, h):
    R, Cn = shard_shape
    if axis == 1:
        return ref.at[pl.ds(h * (R // 2), R // 2), pl.ds(k * Cn, Cn)]
    return ref.at[pl.ds(k * R + h * (R // 2), R // 2), :]


def _block(ref, axis, shard_shape, k):
    R, Cn = shard_shape
    if axis == 1:
        return ref.at[:, pl.ds(k * Cn, Cn)]
    return ref.at[pl.ds(k * R, R), :]


def _half(ref, h):
    R = ref.shape[0]
    return ref.at[pl.ds(h * (R // 2), R // 2), :]


ANY = pl.BlockSpec(memory_space=pl.ANY)


HBM = pl.BlockSpec(memory_space=pltpu.HBM)
SEM = pl.BlockSpec(memory_space=pltpu.SEMAPHORE)
VMEM_WHOLE = pl.BlockSpec(memory_space=pltpu.VMEM)
EFFECT = pltpu.SideEffectType.DATAFLOW_SIDE_EFFECTING


def _in_hbm(a):
    return pltpu.with_memory_space_constraint(a, pltpu.HBM)


def _split_start(body, name, sources, lands, n_copies):
    n = len(sources)
    out_shape, out_specs = [], []
    for _ in range(n):
        out_shape += [pltpu.SemaphoreType.DMA((n_copies,)), pltpu.SemaphoreType.DMA((n_copies,))]
        out_specs += [SEM, SEM]
    out_shape += [pltpu.HBM(a.shape, a.dtype) for a in list(sources) + list(lands)]
    out_specs += [HBM] * (2 * n)
    out_shape.append(jax.ShapeDtypeStruct((8, 128), F32))
    out_specs.append(VMEM_WHOLE)

    def call_body(*refs):
        srcs, lnds = refs[:n], refs[n:2 * n]
        sems = refs[2 * n:4 * n]
        token = refs[-1]
        body(srcs, lnds, sems[0::2], sems[1::2])
        token[...] = jnp.zeros_like(token)

    res = pl.pallas_call(
        call_body, name=name, in_specs=[HBM] * (2 * n), out_specs=out_specs, out_shape=out_shape,
        input_output_aliases={i: 2 * n + i for i in range(2 * n)},
        compiler_params=pltpu.CompilerParams(has_side_effects=EFFECT))(
            *[_in_hbm(a) for a in list(sources) + list(lands)])
    pairs = [(res[2 * w], res[2 * w + 1], res[2 * n + w], res[3 * n + w]) for w in range(n)]
    return pairs, res[-1]


def _split_wait(body, name, pairs, after):
    n = len(pairs)

    def call_body(*refs):
        srcs, lnds = refs[:n], refs[n:2 * n]
        sems = refs[2 * n:4 * n]
        body(srcs, lnds, sems[0::2], sems[1::2])

    args = [_in_hbm(p[2]) for p in pairs] + [_in_hbm(p[3]) for p in pairs]
    for p in pairs:
        args += [p[0], p[1]]
    res = pl.pallas_call(
        call_body, name=name, in_specs=[HBM] * (2 * n) + [SEM] * (2 * n) + [ANY], out_specs=[HBM] * (2 * n),
        out_shape=[pltpu.HBM(a.shape, a.dtype) for a in args[:2 * n]],
        input_output_aliases={i: i for i in range(2 * n)},
        compiler_params=pltpu.CompilerParams(has_side_effects=EFFECT))(*args, after)
    return res[:n], res[n:]


def _place_block(shard, axis, chip_idx, name, dtype, after=None):
    R, Cn = shard.shape
    tr = _pick(R, (256, 176, 128, 64, 32, 16, 8))
    nblk = R // tr

    def body(k_ref, s_ref, *rest):
        v = s_ref[...].astype(dtype)
        rest[-2][...] = v
        rest[-1][...] = v

    if axis == 1:
        out_shape, out_index = (R, 4 * Cn), lambda i, k: (i, k[0])
    else:
        out_shape, out_index = (4 * R, Cn), lambda i, k: (k[0] * nblk + i, 0)
    extra = () if after is None else (after,)
    return pl.pallas_call(
        body, name=name,
        grid_spec=pltpu.PrefetchScalarGridSpec(
            num_scalar_prefetch=1, grid=(nblk,),
            in_specs=[pl.BlockSpec((tr, Cn), lambda i, k: (i, 0))] + [ANY] * len(extra),
            out_specs=[pl.BlockSpec((tr, Cn), lambda i, k: (i, 0)), pl.BlockSpec((tr, Cn), out_index)]),
        out_shape=[jax.ShapeDtypeStruct((R, Cn), dtype), jax.ShapeDtypeStruct(out_shape, dtype)],
        compiler_params=_params(("parallel",)))(chip_idx, shard, *extra)


def _gather_copy(srcs, lnds, send, recv, axes, shapes, w, j, me):
    chip = me >> 1
    return (pltpu.make_async_remote_copy(
        src_ref=srcs[w], dst_ref=_block(lnds[w], axes[w], shapes[w], chip), send_sem=send[w].at[j],
        recv_sem=recv[w].at[j], device_id=_device(me ^ (2 * (j + 1))), device_id_type=MESH),
            pltpu.make_async_remote_copy(
        src_ref=srcs[w], dst_ref=_block(lnds[w], axes[w], shapes[w], chip ^ (j + 1)), send_sem=send[w].at[j],
        recv_sem=recv[w].at[j], device_id=_device(me ^ (2 * (j + 1))), device_id_type=MESH))


def _gather_start(shards, lands, axes, name):
    shapes = [s.shape for s in shards]

    def body(srcs, lnds, send, recv):
        x, y, c, me = _position()
        for w in range(len(shards)):
            for j in range(3):
                _gather_copy(srcs, lnds, send, recv, axes, shapes, w, j, me)[0].start()

    return _split_start(body, name, shards, lands, 3)


def _gather_wait(pairs, axes, after, name):
    shapes = [p[2].shape for p in pairs]

    def body(srcs, lnds, send, recv):
        x, y, c, me = _position()
        for w in range(len(pairs)):
            for j in range(3):
                sent, landed = _gather_copy(srcs, lnds, send, recv, axes, shapes, w, j, me)
                sent.wait_send()
                landed.wait_recv()

    return _split_wait(body, name, pairs, after)[1]


def _shard_shape(grad, axis):
    return (grad.shape[0], grad.shape[1] // 4) if axis == 1 else (grad.shape[0] // 4, grad.shape[1])


def _scatter_copy(srcs, lnds, send, recv, axes, shapes, w, m, me):
    peer = me ^ m
    return pltpu.make_async_remote_copy(
        src_ref=_half_block(srcs[w], axes[w], shapes[w], peer >> 1, peer & 1), dst_ref=lnds[w].at[m - 1],
        send_sem=send[w].at[m - 1], recv_sem=recv[w].at[m - 1], device_id=_device(peer), device_id_type=MESH)


def _scatter_start(grads, axes, name):
    shapes = [_shard_shape(g, a) for g, a in zip(grads, axes)]
    lands = [lax.empty((N_DEV - 1, R // 2, Cn), BF16) for R, Cn in shapes]

    def body(srcs, lnds, send, recv):
        x, y, c, me = _position()
        for w in range(len(grads)):
            for m in range(1, N_DEV):
                _scatter_copy(srcs, lnds, send, recv, axes, shapes, w, m, me).start()

    return _split_start(body, name, grads, lands, N_DEV - 1)


def _scatter_wait(pairs, axes, after):
    shapes = [_shard_shape(p[2], a) for p, a in zip(pairs, axes)]

    def body(srcs, lnds, send, recv):
        x, y, c, me = _position()
        for w in range(len(pairs)):
            for m in range(1, N_DEV):
                cp = _scatter_copy(srcs, lnds, send, recv, axes, shapes, w, m, me)
                cp.wait_send()
                cp.wait_recv()

    return _split_wait(body, "scatter_wait", pairs, after)


def _sum_partials(grad, axis, parts, where, name):
    _, R, Cn = parts.shape
    tr = _pick(R, (256, 176, 128, 64, 32, 16, 8))
    nblk = R // tr

    def body(where_ref, own_ref, p_ref, o_ref):
        acc = own_ref[...].astype(F32)
        for d in range(N_DEV - 1):
            acc = acc + p_ref[d].astype(F32)
        o_ref[...] = acc

    if axis == 1:
        own_index = lambda i, wr: (wr[0] * nblk + i, wr[1])
    else:
        own_index = lambda i, wr: (wr[1] * 2 * nblk + wr[0] * nblk + i, 0)
    return pl.pallas_call(
        body, name=name,
        grid_spec=pltpu.PrefetchScalarGridSpec(
            num_scalar_prefetch=1, grid=(nblk,),
            in_specs=[pl.BlockSpec((tr, Cn), own_index),
                      pl.BlockSpec((N_DEV - 1, tr, Cn), lambda i, wr: (0, i, 0))],
            out_specs=pl.BlockSpec((tr, Cn), lambda i, wr: (wr[0] * nblk + i, 0))),
        out_shape=jax.ShapeDtypeStruct((2 * R, Cn), F32),
        compiler_params=_params(("parallel",)))(where, grad, parts)


def _exchange_halves(grads):
    n = len(grads)

    def body(*refs):
        outs = refs[n:2 * n]
        send, recv = refs[2 * n:]
        x, y, c, me = _position()

        def copy(w, half):
            rows = _half(outs[w], half)
            return pltpu.make_async_remote_copy(src_ref=rows, dst_ref=rows, send_sem=send.at[w],
                                                recv_sem=recv.at[w], device_id=_device(me ^ 1), device_id_type=MESH)

        for w in range(n):
            copy(w, c).start()
        for w in range(n):
            copy(w, 1 - c).wait_recv()
        for w in range(n):
            copy(w, c).wait_send()

    return pl.pallas_call(
        body, name="exchange_halves", in_specs=[ANY] * n, out_specs=[ANY] * n,
        out_shape=[jax.ShapeDtypeStruct(a.shape, a.dtype) for a in grads],
        input_output_aliases={i: i for i in range(n)},
        scratch_shapes=[pltpu.SemaphoreType.DMA((n,)), pltpu.SemaphoreType.DMA((n,))],
        compiler_params=pltpu.CompilerParams(has_side_effects=True))(*grads)


def _all_reduce_small(vec):
    R, L = vec.shape

    def body(v_ref, o_ref, buf, send, recv):
        x, y, c, me = _position()
        buf[me] = v_ref[...]

        def copy(m, slot):
            return pltpu.make_async_remote_copy(src_ref=v_ref, dst_ref=buf.at[slot], send_sem=send.at[m - 1],
                                                recv_sem=recv.at[m - 1], device_id=_device(me ^ m),
                                                device_id_type=MESH)

        for m in range(1, N_DEV):
            copy(m, me).start()
        for m in range(1, N_DEV):
            copy(m, me ^ m).wait_recv()
        for m in range(1, N_DEV):
            copy(m, me).wait_send()
        acc = buf[0]
        for d in range(1, N_DEV):
            acc = acc + buf[d]
        o_ref[...] = acc

    vm = pl.BlockSpec(memory_space=pltpu.VMEM)
    return pl.pallas_call(
        body, name="all_reduce_small", in_specs=[vm], out_specs=vm, out_shape=jax.ShapeDtypeStruct((R, L), F32),
        scratch_shapes=[pltpu.VMEM((N_DEV, R, L), F32), pltpu.SemaphoreType.DMA((N_DEV - 1,)),
                        pltpu.SemaphoreType.DMA((N_DEV - 1,))],
        compiler_params=pltpu.CompilerParams(has_side_effects=True))(vec)


def _adamw(w, g, m, v, name):
    R, Cn = w.shape
    tr = _pick(R, (256, 176, 128, 64, 40, 32, 16, 8))

    def body(w_ref, g_ref, m_ref, v_ref, d_ref, nm_ref, nv_ref):
        gv = g_ref[...]
        nm = ADAM_B1 * m_ref[...] + (1.0 - ADAM_B1) * gv
        nv = ADAM_B2 * v_ref[...] + (1.0 - ADAM_B2) * (gv * gv)
        m_hat = nm / (1.0 - ADAM_B1 ** ADAM_STEP)
        v_hat = nv / (1.0 - ADAM_B2 ** ADAM_STEP)
        d_ref[...] = -ADAM_LR * (m_hat / (jnp.sqrt(v_hat) + ADAM_EPS) + ADAM_WD * w_ref[...])
        nm_ref[...] = nm
        nv_ref[...] = nv

    spec = pl.BlockSpec((tr, Cn), lambda i: (i, 0))
    return pl.pallas_call(
        body, name=name, grid=(R // tr,), in_specs=[spec] * 4, out_specs=[spec] * 3,
        out_shape=[jax.ShapeDtypeStruct((R, Cn), F32)] * 3, compiler_params=_params(("parallel",)))(w, g, m, v)


def _pack(arrays, rows):
    flat = jnp.concatenate([a.reshape(-1).astype(F32) for a in arrays])
    return jnp.pad(flat, (0, rows * 128 - flat.shape[0])).reshape(rows, 128)


def _unpack(packed, shapes):
    flat = packed.reshape(-1)
    out, off = [], 0
    for s in shapes:
        size = 1
        for d in s:
            size *= d
        out.append(flat[off:off + size].reshape(s))
        off += size
    return out


def _ffn_fwd(x, h, arrived, tag, next_g=None):
    w_up = arrived(f"{tag}_w_up", h)
    a, b, act = _ffn_up(h, w_up, f"{tag}_up")
    w_down = arrived(f"{tag}_w_down", act)
    out, h_next = _ffn_down(act, w_down, x, f"{tag}_down", next_g)
    return out, h_next, (h, a, b, act, w_up, w_down)


def _ffn_bwd(dout, x, norm, saved, tag, send):
    h, a, b, act, w_up, w_down = saved
    g_down = _mm(act, dout, 'tn', f"{tag}_down_dw", BF16, scale=0.5)
    token = send([f"{tag}_w_down"], [g_down])
    da, db = _ffn_down_bwd(dout, w_down, a, b, f"{tag}_down_dx", after=token)
    g_up = _dw_pieces(h, [da, db], f"{tag}_up_dw")
    token = send([f"{tag}_w_up"], [g_up])
    return _dx_rms_bwd([da, db], w_up, x, norm, dout, f"{tag}_up_dx", after=token)


def kernel(x, mem, ffn1_norm, ffn1_w_up, ffn1_w_down, mix_norm, mem_norm, w_in, b_gate, conv_dw_w, conv_dw_b, conv_ln_g, conv_ln_b, conv_w_pw, att_rel_bias, att_w_o, mem_w_kv, mem_w_o, w_out, ffn2_norm, ffn2_w_up, ffn2_w_down, final_norm, loss_target, m_ffn1_norm, m_ffn1_w_up, m_ffn1_w_down, m_mix_norm, m_mem_norm, m_w_in, m_b_gate, m_conv_dw_w, m_conv_dw_b, m_conv_ln_g, m_conv_ln_b, m_conv_w_pw, m_att_rel_bias, m_att_w_o, m_mem_w_kv, m_mem_w_o, m_w_out, m_ffn2_norm, m_ffn2_w_up, m_ffn2_w_down, m_final_norm, v_ffn1_norm, v_ffn1_w_up, v_ffn1_w_down, v_mix_norm, v_mem_norm, v_w_in, v_b_gate, v_conv_dw_w, v_conv_dw_b, v_conv_ln_g, v_conv_ln_b, v_conv_w_pw, v_att_rel_bias, v_att_w_o, v_mem_w_kv, v_mem_w_o, v_w_out, v_ffn2_norm, v_ffn2_w_up, v_ffn2_w_down, v_final_norm):
    given = dict(locals())
    wts = {n: given[n] for n in WEIGHTS}
    mom1 = {n: given["m_" + n] for n in WEIGHTS}
    mom2 = {n: given["v_" + n] for n in WEIGHTS}
    Bl, S, Dm = x.shape
    T = Bl * S
    x0 = x.reshape(T, Dm)
    tgt = loss_target.reshape(T, Dm)
    mem2 = mem.reshape(Bl * MEM_LEN, Dm)

    big_names = [n for n, _ in BIG]
    big_axes = [a for _, a in BIG]
    chip = 2 * lax.axis_index("x") + lax.axis_index("y")

    core = lax.axis_index("c")
    axis_of = dict(BIG)

    gather_groups = [['ffn1_w_up'], ['ffn1_w_down'], ['w_in', 'conv_dw_w'],
                     ['mem_w_kv', 'conv_w_pw', 'att_w_o', 'mem_w_o', 'w_out'], ['ffn2_w_up'], ['ffn2_w_down']]
    gather_names = [n for grp in gather_groups for n in grp]
    gather_axes = [axis_of.get(n, 1) for n in gather_names]
    raw = [jnp.pad(conv_dw_w[0], ((0, 1), (0, 0))) if n == 'conv_dw_w' else wts[n][0] for n in gather_names]
    kinds = [F32 if n == 'conv_dw_w' else BF16 for n in gather_names]
    chip_idx = chip.reshape(1).astype(jnp.int32)
    shard0, land0 = _place_block(raw[0], gather_axes[0], chip_idx, f"place_{gather_names[0]}", kinds[0])
    first, first_token = _gather_start([shard0], [land0], gather_axes[:1], "gather_start_first")
    placed = [_place_block(r, a, chip_idx, f"place_{n}", kd, after=first_token)
              for r, a, n, kd in zip(raw[1:], gather_axes[1:], gather_names[1:], kinds[1:])]
    rest, gather_token = _gather_start([p[0] for p in placed], [p[1] for p in placed], gather_axes[1:],
                                       "gather_start_rest")
    in_flight = dict(zip(gather_names, first + rest))
    full = {}

    def arrived(name, after):
        if name not in full:
            grp = next(grp for grp in gather_groups if name in grp)
            lands = _gather_wait([in_flight[n] for n in grp], [axis_of.get(n, 1) for n in grp], after,
                                 f"gather_wait_{grp[0]}")
            full.update(zip(grp, lands))
        return full[name]

    scattering = {}

    def send(names, grads):
        pairs, token = _scatter_start(grads, [axis_of[n] for n in names], f"scatter_start_{names[0]}")
        scattering.update(zip(names, pairs))
        return token

    final_g = final_norm.reshape(1, Dm)
    bias = _att_bias(att_rel_bias[0] + first_token[:1, :1])

    x1, h, ffn1_saved = _ffn_fwd(x0, _rms_fwd(x0, ffn1_norm, "ffn1_norm", after=gather_token), arrived, "ffn1",
                                 next_g=mix_norm)
    w_in_full = arrived('w_in', h)
    dw_full = full['conv_dw_w'][:CONV_KERNEL]
    proj = _mm(h, w_in_full, 'nn', "w_in", BF16)
    proj3 = proj.reshape(Bl, S, proj.shape[1])
    cv, c_act = _conv_fwd(proj3, dw_full, conv_dw_b, conv_ln_g, conv_ln_b)
    o_att = _att_fwd(proj3, bias)
    mem_h = _rms_fwd(mem2, mem_norm, "mem_norm")
    kv = _mm(mem_h, arrived('mem_w_kv', o_att), 'nn', "mem_kv", BF16)
    kv3 = kv.reshape(Bl, MEM_LEN, 2 * MEM_WIDTH)
    o_mem = _mem_fwd(proj3, kv3)
    c_act2, o_att2, o_mem2 = c_act.reshape(T, -1), o_att.reshape(T, -1), o_mem.reshape(T, -1)
    x2, yc, ya, ym, h2 = _mix_fwd(c_act2, o_att2, o_mem2, proj, b_gate, x1, full['conv_w_pw'], full['att_w_o'],
                                  full['mem_w_o'], full['w_out'], ffn2_norm)
    x3, _, ffn2_saved = _ffn_fwd(x2, h2, arrived, "ffn2")
    dx3, g_final, loss_vec = _final_fwd_bwd(x3, tgt, final_g)

    g = {}
    dx2, g['ffn2_norm'] = _ffn_bwd(dx3, x2, ffn2_norm, ffn2_saved, "ffn2", send)
    dgl, g['b_gate'], dc, doa, dom, g_pw, g_o, g_mo, g_out = _mix_bwd(
        dx2, yc, ya, ym, c_act2, o_att2, o_mem2, proj, b_gate, full['conv_w_pw'], full['att_w_o'],
        full['mem_w_o'], full['w_out'])
    token = send(['w_out', 'conv_w_pw', 'att_w_o', 'mem_w_o'], [g_out, g_pw, g_o, g_mo])
    dproj3 = dgl.reshape(Bl, S, -1)
    dproj3, g_dw, g['conv_dw_b'], g['conv_ln_g'], g['conv_ln_b'] = _conv_bwd(
        proj3, cv, dc.reshape(Bl, S, -1), dw_full, conv_ln_g, conv_ln_b, dproj3)
    dproj3, dk, dv, dbias = _att_bwd(proj3, doa.reshape(Bl, S, -1), bias, dproj3)
    g['att_rel_bias'] = _rel_bias_grad(dbias)
    dproj3, dkv = _mem_bwd(proj3, kv3, dom.reshape(Bl, S, -1), dproj3)
    dkv2 = dkv.reshape(Bl * MEM_LEN, 2 * MEM_WIDTH)
    g_kv = _mm(mem_h, dkv2, 'tn', "mem_kv_dw", BF16, after=token)
    dmem_h = _mm(dkv2, full['mem_w_kv'], 'nt', "mem_kv_dx", F32)
    _, g['mem_norm'] = _rms_bwd(mem2, mem_norm, dmem_h, dmem_h, "mem_norm_bwd")
    dkdv = jnp.concatenate([dk.reshape(T, -1), dv.reshape(T, -1)], axis=1)
    dproj = lax.dynamic_update_slice(dproj3.reshape(T, -1), dkdv, (0, 2 * CONV_WIDTH + ATT_WIDTH))
    token = send(['mem_w_kv', 'w_in'], [g_kv, _mm(h, dproj, 'tn', "w_in_dw", BF16)])
    dx1, g['mix_norm'] = _dx_rms_bwd([dproj], w_in_full, x1, mix_norm, dx2, "w_in_dx", after=token)
    dx0, g['ffn1_norm'] = _ffn_bwd(dx1, x0, ffn1_norm, ffn1_saved, "ffn1", send)
    g['final_norm'] = g_final

    sent, landed = _scatter_wait([scattering[n] for n in big_names], big_axes, dx0)
    where = jnp.stack([core, chip]).astype(jnp.int32)
    halves = [_sum_partials(own_full, a, parts, where, f"sum_{n}")
              for n, a, own_full, parts in zip(big_names, big_axes, sent, landed)]
    for n, sg in zip(big_names, _exchange_halves(halves)):
        g[n] = sg

    small_shapes = [wts[n].shape for n in SMALL]
    n_small = sum(int(wts[n].size) for n in SMALL)
    n_red = n_small + CONV_KERNEL * CONV_WIDTH + 1
    red = _all_reduce_small(_pack([g[n] for n in SMALL] + [g_dw, loss_vec[0, :1]], -(-n_red // 1024) * 8))
    red_list = _unpack(red, small_shapes + [(CONV_KERNEL, CONV_WIDTH), ()])
    for n, rg in zip(SMALL, red_list[:-2]):
        g[n] = rg
    loss = red_list[-1]
    dw_cols = conv_dw_w.shape[2]
    g['conv_dw_w'] = lax.dynamic_slice(red_list[-2], (0, chip * dw_cols), (CONV_KERNEL, dw_cols))[None]

    delta, new_m, new_v = {}, {}, {}
    for n in big_names:
        g[n] = g[n][None]
        d, nm, nv = _adamw(wts[n][0], g[n][0], mom1[n][0], mom2[n][0], f"adamw_{n}")
        delta[n], new_m[n], new_v[n] = d[None], nm[None], nv[None]
    rest = SMALL + ['conv_dw_w']
    rest_shapes = [wts[n].shape for n in rest]
    rows = -(-sum(int(wts[n].size) for n in rest) // 1024) * 8
    packed = [_pack([src[n] for n in rest], rows) for src in (wts, g, mom1, mom2)]
    for out, res in zip((delta, new_m, new_v), _adamw(*packed, "adamw_small")):
        for n, a in zip(rest, _unpack(res, rest_shapes)):
            out[n] = a

    grad_x = dx0.reshape(Bl, S, Dm)
    return (loss, grad_x, *[g[n] for n in WEIGHTS], *[delta[n] for n in WEIGHTS],
            *[new_m[n] for n in WEIGHTS], *[new_v[n] for n in WEIGHTS])
```
